```python
import jax, jax.numpy as jnp
from jax import lax
import numpy as np

D_MODEL = 1024
BATCH = 8
SEQ = 8192
DEPTH = 2

HEAD_DIM = 64
N_Q_HEADS = 8
N_KV_HEADS = 2
ATTN_WIDTH = N_Q_HEADS * HEAD_DIM
KV_WIDTH = N_KV_HEADS * HEAD_DIM
LRU_HEADS = 4
LRU_WIDTH = LRU_HEADS * HEAD_DIM
CONV_GROUPS = 4
CONV_WIDTH = CONV_GROUPS * HEAD_DIM
MIX_WIDTH = ATTN_WIDTH + LRU_WIDTH + CONV_WIDTH
IN_SPLITS = (ATTN_WIDTH, KV_WIDTH, KV_WIDTH, LRU_WIDTH, LRU_WIDTH, CONV_WIDTH, CONV_WIDTH, CONV_WIDTH)
IN_PROJ_WIDTH = sum(IN_SPLITS)
WINDOW = 128
BLOCK = 128
LRU_CONV_K = 4
SHORT_CONV_K = 3
LRU_C = 8.0
D_FF = ((8 * D_MODEL // 3 + 127) // 128) * 128
N_SUBLAYERS = 3
N_MOD = 3 * N_SUBLAYERS
EPS = 1e-6
NEG_INF = -1e30

kernel_name = "hymba_hybrid_rglru_swa_shortconv_macaron"


def rmsnorm(x, g):
    xf = x.astype(jnp.float32)
    y = xf * lax.rsqrt(jnp.mean(xf * xf, axis=-1, keepdims=True) + EPS)
    return (y * g.astype(jnp.float32)).astype(x.dtype)


def modulate(x, g, shift, scale):
    return rmsnorm(x, g) * (1.0 + scale[:, None, :]) + shift[:, None, :]


def swiglu(h, w_gu, w_down):
    gu = h @ w_gu
    g, u = jnp.split(gu, 2, axis=-1)
    return (jax.nn.silu(g) * u) @ w_down


def causal_depthwise_conv(x, w, b=None):
    K = w.shape[0]
    S = x.shape[1]
    xp = jnp.pad(x, ((0, 0), (K - 1, 0), (0, 0)))
    y = xp[:, 0:S] * w[0]
    for k in range(1, K):
        y = y + xp[:, k:k + S] * w[k]
    if b is not None:
        y = y + b
    return y


def alibi_slopes(n_heads):
    return jnp.asarray(2.0 ** (-8.0 * np.arange(1, n_heads + 1) / n_heads), dtype=jnp.float32)


def sliding_window_attention(q, k, v, sinks):
    b, s = q.shape[:2]
    nb = s // BLOCK
    G = N_Q_HEADS // N_KV_HEADS
    qb = q.reshape(b, nb, BLOCK, N_KV_HEADS, G, HEAD_DIM)

    def band(t):
        tb = t.reshape(b, nb, BLOCK, N_KV_HEADS, HEAD_DIM)
        prev = jnp.pad(tb[:, :-1], ((0, 0), (1, 0), (0, 0), (0, 0), (0, 0)))
        return jnp.concatenate([prev, tb], axis=2)

    kb, vb = band(k), band(v)
    scores = jnp.einsum('bnqkgd,bnskd->bnkgqs', qb, kb).astype(jnp.float32) * (HEAD_DIM ** -0.5)
    qi = jnp.arange(BLOCK)[:, None]
    kj = jnp.arange(2 * BLOCK)[None, :]
    dist = qi + BLOCK - kj
    in_window = (dist >= 0) & (dist < WINDOW)
    blk = jnp.arange(nb)[:, None, None]
    valid = in_window[None] & ((blk > 0) | (kj[None] >= BLOCK))
    slopes = alibi_slopes(N_Q_HEADS).reshape(N_KV_HEADS, G)
    bias = -slopes[:, :, None, None] * dist.astype(jnp.float32)
    scores = jnp.where(valid[None, :, None, None], scores + bias[None, None], NEG_INF)
    sink = sinks.astype(jnp.float32).reshape(N_KV_HEADS, G)[None, None, :, :, None, None]
    m = jnp.maximum(jnp.max(scores, axis=-1, keepdims=True), sink)
    p = jnp.exp(scores - m)
    denom = jnp.sum(p, axis=-1, keepdims=True) + jnp.exp(sink - m)
    out = jnp.einsum('bnkgqs,bnskd->bnqkgd', (p / denom).astype(v.dtype), vb)
    return out.reshape(b, s, ATTN_WIDTH)


def rg_lru(x, gate_a_w, gate_a_b, gate_x_w, gate_x_b, lam):
    b, s, _ = x.shape
    xh = x.reshape(b, s, LRU_HEADS, HEAD_DIM)
    r = jax.nn.sigmoid(jnp.einsum('bshd,hde->bshe', xh, gate_a_w).reshape(b, s, LRU_WIDTH) + gate_a_b)
    i = jax.nn.sigmoid(jnp.einsum('bshd,hde->bshe', xh, gate_x_w).reshape(b, s, LRU_WIDTH) + gate_x_b)
    log_a = -LRU_C * r.astype(jnp.float32) * jax.nn.softplus(-lam.astype(jnp.float32))
    a = jnp.exp(log_a)
    mult = jnp.sqrt(-jnp.expm1(2.0 * log_a))
    first = (jnp.arange(s) == 0)[None, :, None]
    mult = jnp.where(first, 1.0, mult)
    u = mult * (i * x).astype(jnp.float32)

    def combine(left, right):
        a1, b1 = left
        a2, b2 = right
        return a1 * a2, a2 * b1 + b2

    _, h = lax.associative_scan(combine, (a, u), axis=1)
    return h.astype(x.dtype)


def hybrid_mixer(h, w_in, w_out, sinks, lru_conv_w, lru_conv_b, lru_gate_a_w, lru_gate_a_b,
                 lru_gate_x_w, lru_gate_x_b, lru_lambda, sc_conv_w):
    proj = h @ w_in
    offs = np.cumsum(IN_SPLITS)[:-1].tolist()
    q, k, v, lx, lg, sb, sc, sx = jnp.split(proj, offs, axis=-1)
    y_attn = sliding_window_attention(q, k, v, sinks)
    lx = causal_depthwise_conv(lx, lru_conv_w, lru_conv_b)
    y_lru = jax.nn.gelu(lg) * rg_lru(lx, lru_gate_a_w, lru_gate_a_b, lru_gate_x_w, lru_gate_x_b, lru_lambda)
    y_sc = sb * causal_depthwise_conv(sc * sx, sc_conv_w)
    y = jnp.concatenate([y_attn, y_lru, y_sc], axis=-1)
    return y @ w_out


def _fwd_setup_inputs(seed: int = 0) -> dict:
    key = jax.random.key(seed)
    ks = jax.random.split(key, 24)
    f32 = jnp.float32

    def nrm(k, shape, scale):
        return jax.random.normal(k, shape, f32) * scale

    u = jax.random.uniform(ks[17], (DEPTH, LRU_WIDTH), f32, 0.9, 0.999)
    a0 = u ** (1.0 / LRU_C)
    lru_lambda = jnp.log(a0) - jnp.log1p(-a0)
    return {
        "x": nrm(ks[0], (BATCH, SEQ, D_MODEL), 1.0),
        "c": nrm(ks[1], (BATCH, D_MODEL), 1.0),
        "w_mod": nrm(ks[2], (DEPTH, D_MODEL, N_MOD * D_MODEL), D_MODEL ** -0.5),
        "b_mod": nrm(ks[3], (DEPTH, N_MOD * D_MODEL), 0.02),
        "g_norm": 1.0 + nrm(ks[4], (DEPTH, N_SUBLAYERS, D_MODEL), 0.02),
        "w_ffn1_gu": nrm(ks[5], (DEPTH, D_MODEL, 2 * D_FF), D_MODEL ** -0.5),
        "w_ffn1_down": nrm(ks[6], (DEPTH, D_FF, D_MODEL), D_FF ** -0.5),
        "w_ffn2_gu": nrm(ks[7], (DEPTH, D_MODEL, 2 * D_FF), D_MODEL ** -0.5),
        "w_ffn2_down": nrm(ks[8], (DEPTH, D_FF, D_MODEL), D_FF ** -0.5),
        "w_in": nrm(ks[9], (DEPTH, D_MODEL, IN_PROJ_WIDTH), D_MODEL ** -0.5),
        "w_out": nrm(ks[10], (DEPTH, MIX_WIDTH, D_MODEL), MIX_WIDTH ** -0.5),
        "attn_sinks": nrm(ks[11], (DEPTH, N_Q_HEADS), 0.5),
        "lru_conv_w": nrm(ks[12], (DEPTH, LRU_CONV_K, LRU_WIDTH), LRU_CONV_K ** -0.5),
        "lru_conv_b": nrm(ks[13], (DEPTH, LRU_WIDTH), 0.02),
        "lru_gate_a_w": nrm(ks[14], (DEPTH, LRU_HEADS, HEAD_DIM, HEAD_DIM), HEAD_DIM ** -0.5),
        "lru_gate_a_b": nrm(ks[15], (DEPTH, LRU_WIDTH), 0.02),
        "lru_gate_x_w": nrm(ks[16], (DEPTH, LRU_HEADS, HEAD_DIM, HEAD_DIM), HEAD_DIM ** -0.5),
        "lru_gate_x_b": nrm(ks[18], (DEPTH, LRU_WIDTH), 0.02),
        "lru_lambda": lru_lambda,
        "sc_conv_w": nrm(ks[19], (DEPTH, SHORT_CONV_K, CONV_WIDTH), SHORT_CONV_K ** -0.5),
        "g_final": 1.0 + nrm(ks[20], (D_MODEL,), 0.02),
    }


def _fwd_reference(x, c, w_mod, b_mod, g_norm, w_ffn1_gu, w_ffn1_down, w_ffn2_gu, w_ffn2_down,
              w_in, w_out, attn_sinks, lru_conv_w, lru_conv_b, lru_gate_a_w, lru_gate_a_b,
              lru_gate_x_w, lru_gate_x_b, lru_lambda, sc_conv_w, g_final):
    c_act = jax.nn.silu(c)
    for l in range(DEPTH):
        mod = (c_act @ w_mod[l] + b_mod[l]).reshape(c.shape[0], N_MOD, D_MODEL)
        h = modulate(x, g_norm[l, 0], mod[:, 0], mod[:, 1])
        x = x + 0.5 * mod[:, 2][:, None, :] * swiglu(h, w_ffn1_gu[l], w_ffn1_down[l])
        h = modulate(x, g_norm[l, 1], mod[:, 3], mod[:, 4])
        x = x + mod[:, 5][:, None, :] * hybrid_mixer(
            h, w_in[l], w_out[l], attn_sinks[l], lru_conv_w[l], lru_conv_b[l],
            lru_gate_a_w[l], lru_gate_a_b[l], lru_gate_x_w[l], lru_gate_x_b[l],
            lru_lambda[l], sc_conv_w[l])
        h = modulate(x, g_norm[l, 2], mod[:, 6], mod[:, 7])
        x = x + 0.5 * mod[:, 8][:, None, :] * swiglu(h, w_ffn2_gu[l], w_ffn2_down[l])
    return rmsnorm(x, g_final)


import jax as _jax
import jax.numpy as _jnp

TWIN_FORMAT = 'train_step'
FWD_PARAMS = ['x', 'c', 'w_mod', 'b_mod', 'g_norm', 'w_ffn1_gu', 'w_ffn1_down', 'w_ffn2_gu', 'w_ffn2_down', 'w_in', 'w_out', 'attn_sinks', 'lru_conv_w', 'lru_conv_b', 'lru_gate_a_w', 'lru_gate_a_b', 'lru_gate_x_w', 'lru_gate_x_b', 'lru_lambda', 'sc_conv_w', 'g_final']
TWIN_WEIGHTS = ['w_mod', 'b_mod', 'g_norm', 'w_ffn1_gu', 'w_ffn1_down', 'w_ffn2_gu', 'w_ffn2_down', 'w_in', 'w_out', 'attn_sinks', 'lru_conv_w', 'lru_conv_b', 'lru_gate_a_w', 'lru_gate_a_b', 'lru_gate_x_w', 'lru_gate_x_b', 'lru_lambda', 'sc_conv_w', 'g_final']
TWIN_DIFF_INPUT = 'x'
TWIN_INPUTS = ['x', 'c', 'w_mod', 'b_mod', 'g_norm', 'w_ffn1_gu', 'w_ffn1_down', 'w_ffn2_gu', 'w_ffn2_down', 'w_in', 'w_out', 'attn_sinks', 'lru_conv_w', 'lru_conv_b', 'lru_gate_a_w', 'lru_gate_a_b', 'lru_gate_x_w', 'lru_gate_x_b', 'lru_lambda', 'sc_conv_w', 'g_final', 'loss_target', 'm_w_mod', 'm_b_mod', 'm_g_norm', 'm_w_ffn1_gu', 'm_w_ffn1_down', 'm_w_ffn2_gu', 'm_w_ffn2_down', 'm_w_in', 'm_w_out', 'm_attn_sinks', 'm_lru_conv_w', 'm_lru_conv_b', 'm_lru_gate_a_w', 'm_lru_gate_a_b', 'm_lru_gate_x_w', 'm_lru_gate_x_b', 'm_lru_lambda', 'm_sc_conv_w', 'm_g_final', 'v_w_mod', 'v_b_mod', 'v_g_norm', 'v_w_ffn1_gu', 'v_w_ffn1_down', 'v_w_ffn2_gu', 'v_w_ffn2_down', 'v_w_in', 'v_w_out', 'v_attn_sinks', 'v_lru_conv_w', 'v_lru_conv_b', 'v_lru_gate_a_w', 'v_lru_gate_a_b', 'v_lru_gate_x_w', 'v_lru_gate_x_b', 'v_lru_lambda', 'v_sc_conv_w', 'v_g_final']
TWIN_OUTPUTS = ['loss', 'grad_x', 'grad_w_mod', 'grad_b_mod', 'grad_g_norm', 'grad_w_ffn1_gu', 'grad_w_ffn1_down', 'grad_w_ffn2_gu', 'grad_w_ffn2_down', 'grad_w_in', 'grad_w_out', 'grad_attn_sinks', 'grad_lru_conv_w', 'grad_lru_conv_b', 'grad_lru_gate_a_w', 'grad_lru_gate_a_b', 'grad_lru_gate_x_w', 'grad_lru_gate_x_b', 'grad_lru_lambda', 'grad_sc_conv_w', 'grad_g_final', 'delta_w_mod', 'delta_b_mod', 'delta_g_norm', 'delta_w_ffn1_gu', 'delta_w_ffn1_down', 'delta_w_ffn2_gu', 'delta_w_ffn2_down', 'delta_w_in', 'delta_w_out', 'delta_attn_sinks', 'delta_lru_conv_w', 'delta_lru_conv_b', 'delta_lru_gate_a_w', 'delta_lru_gate_a_b', 'delta_lru_gate_x_w', 'delta_lru_gate_x_b', 'delta_lru_lambda', 'delta_sc_conv_w', 'delta_g_final', 'new_m_w_mod', 'new_m_b_mod', 'new_m_g_norm', 'new_m_w_ffn1_gu', 'new_m_w_ffn1_down', 'new_m_w_ffn2_gu', 'new_m_w_ffn2_down', 'new_m_w_in', 'new_m_w_out', 'new_m_attn_sinks', 'new_m_lru_conv_w', 'new_m_lru_conv_b', 'new_m_lru_gate_a_w', 'new_m_lru_gate_a_b', 'new_m_lru_gate_x_w', 'new_m_lru_gate_x_b', 'new_m_lru_lambda', 'new_m_sc_conv_w', 'new_m_g_final', 'new_v_w_mod', 'new_v_b_mod', 'new_v_g_norm', 'new_v_w_ffn1_gu', 'new_v_w_ffn1_down', 'new_v_w_ffn2_gu', 'new_v_w_ffn2_down', 'new_v_w_in', 'new_v_w_out', 'new_v_attn_sinks', 'new_v_lru_conv_w', 'new_v_lru_conv_b', 'new_v_lru_gate_a_w', 'new_v_lru_gate_a_b', 'new_v_lru_gate_x_w', 'new_v_lru_gate_x_b', 'new_v_lru_lambda', 'new_v_sc_conv_w', 'new_v_g_final']
TWIN_LEAF_KINDS = {'loss': 'loss', 'grad_x': 'grad_x', 'grad_w_mod': 'grad_w', 'grad_b_mod': 'grad_w', 'grad_g_norm': 'grad_w', 'grad_w_ffn1_gu': 'grad_w', 'grad_w_ffn1_down': 'grad_w', 'grad_w_ffn2_gu': 'grad_w', 'grad_w_ffn2_down': 'grad_w', 'grad_w_in': 'grad_w', 'grad_w_out': 'grad_w', 'grad_attn_sinks': 'grad_w', 'grad_lru_conv_w': 'grad_w', 'grad_lru_conv_b': 'grad_w', 'grad_lru_gate_a_w': 'grad_w', 'grad_lru_gate_a_b': 'grad_w', 'grad_lru_gate_x_w': 'grad_w', 'grad_lru_gate_x_b': 'grad_w', 'grad_lru_lambda': 'grad_w', 'grad_sc_conv_w': 'grad_w', 'grad_g_final': 'grad_w', 'delta_w_mod': 'delta_w', 'delta_b_mod': 'delta_w', 'delta_g_norm': 'delta_w', 'delta_w_ffn1_gu': 'delta_w', 'delta_w_ffn1_down': 'delta_w', 'delta_w_ffn2_gu': 'delta_w', 'delta_w_ffn2_down': 'delta_w', 'delta_w_in': 'delta_w', 'delta_w_out': 'delta_w', 'delta_attn_sinks': 'delta_w', 'delta_lru_conv_w': 'delta_w', 'delta_lru_conv_b': 'delta_w', 'delta_lru_gate_a_w': 'delta_w', 'delta_lru_gate_a_b': 'delta_w', 'delta_lru_gate_x_w': 'delta_w', 'delta_lru_gate_x_b': 'delta_w', 'delta_lru_lambda': 'delta_w', 'delta_sc_conv_w': 'delta_w', 'delta_g_final': 'delta_w', 'new_m_w_mod': 'new_m', 'new_m_b_mod': 'new_m', 'new_m_g_norm': 'new_m', 'new_m_w_ffn1_gu': 'new_m', 'new_m_w_ffn1_down': 'new_m', 'new_m_w_ffn2_gu': 'new_m', 'new_m_w_ffn2_down': 'new_m', 'new_m_w_in': 'new_m', 'new_m_w_out': 'new_m', 'new_m_attn_sinks': 'new_m', 'new_m_lru_conv_w': 'new_m', 'new_m_lru_conv_b': 'new_m', 'new_m_lru_gate_a_w': 'new_m', 'new_m_lru_gate_a_b': 'new_m', 'new_m_lru_gate_x_w': 'new_m', 'new_m_lru_gate_x_b': 'new_m', 'new_m_lru_lambda': 'new_m', 'new_m_sc_conv_w': 'new_m', 'new_m_g_final': 'new_m', 'new_v_w_mod': 'new_v', 'new_v_b_mod': 'new_v', 'new_v_g_norm': 'new_v', 'new_v_w_ffn1_gu': 'new_v', 'new_v_w_ffn1_down': 'new_v', 'new_v_w_ffn2_gu': 'new_v', 'new_v_w_ffn2_down': 'new_v', 'new_v_w_in': 'new_v', 'new_v_w_out': 'new_v', 'new_v_attn_sinks': 'new_v', 'new_v_lru_conv_w': 'new_v', 'new_v_lru_conv_b': 'new_v', 'new_v_lru_gate_a_w': 'new_v', 'new_v_lru_gate_a_b': 'new_v', 'new_v_lru_gate_x_w': 'new_v', 'new_v_lru_gate_x_b': 'new_v', 'new_v_lru_lambda': 'new_v', 'new_v_sc_conv_w': 'new_v', 'new_v_g_final': 'new_v'}


def _forward(args):
    return _fwd_reference(*[args[k] for k in FWD_PARAMS])


def _output_shape():
    def fwd():
        inp = _fwd_setup_inputs(0)
        return _fwd_reference(*[inp[k] for k in FWD_PARAMS])
    out = _jax.eval_shape(fwd)
    return out.shape, out.dtype

N_MICROBATCH = 1
ADAM_LR = 0.001
ADAM_B1 = 0.9
ADAM_B2 = 0.999
ADAM_EPS = 1e-08
ADAM_WD = 0.01
ADAM_STEP = 10
PER_EXAMPLE_BATCH_AXIS = {'x': 0, 'c': 0, 'loss_target': 0}
SHARED_INPUTS = []
_WEIGHT_DTYPES = {'w_mod': _jnp.float32, 'b_mod': _jnp.float32, 'g_norm': _jnp.float32, 'w_ffn1_gu': _jnp.float32, 'w_ffn1_down': _jnp.float32, 'w_ffn2_gu': _jnp.float32, 'w_ffn2_down': _jnp.float32, 'w_in': _jnp.float32, 'w_out': _jnp.float32, 'attn_sinks': _jnp.float32, 'lru_conv_w': _jnp.float32, 'lru_conv_b': _jnp.float32, 'lru_gate_a_w': _jnp.float32, 'lru_gate_a_b': _jnp.float32, 'lru_gate_x_w': _jnp.float32, 'lru_gate_x_b': _jnp.float32, 'lru_lambda': _jnp.float32, 'sc_conv_w': _jnp.float32, 'g_final': _jnp.float32}
MOMENT_SCALE = {'w_mod': 1.336206e-01, 'b_mod': 2.339745e-01, 'g_norm': 1.425958e-01, 'w_ffn1_gu': 4.971861e-02, 'w_ffn1_down': 8.160861e-02, 'w_ffn2_gu': 3.284754e-02, 'w_ffn2_down': 5.362071e-02, 'w_in': 2.232881e-01, 'w_out': 2.485120e-01, 'attn_sinks': 6.264719e-02, 'lru_conv_w': 5.375359e-01, 'lru_conv_b': 7.236085e-01, 'lru_gate_a_w': 7.848754e-02, 'lru_gate_a_b': 9.291412e-02, 'lru_gate_x_w': 1.416346e-01, 'lru_gate_x_b': 1.822014e-01, 'lru_lambda': 1.993702e-01, 'sc_conv_w': 2.343309e-01, 'g_final': 6.627445e+01}


def _to_microbatches(a, axis):
    t = _jnp.moveaxis(a, axis, 0)
    t = t.reshape((N_MICROBATCH, t.shape[0] // N_MICROBATCH) + t.shape[1:])
    return _jnp.moveaxis(t, 1, axis + 1)


def setup_inputs(seed: int = 0) -> dict:
    inp = _fwd_setup_inputs(seed)
    key = _jax.random.fold_in(_jax.random.key(seed), 7919)
    shape, _ = _output_shape()
    out = dict(inp)
    out["loss_target"] = _jax.random.normal(_jax.random.fold_in(key, 0), shape, _jnp.float32)
    for i, name in enumerate(TWIN_WEIGHTS):
        w = inp[name].astype(_jnp.float32)
        if MOMENT_SCALE is None:
            s = _jnp.sqrt(_jnp.mean(_jnp.square(w)) + 1e-30)
        else:
            s = MOMENT_SCALE[name]
        km, kv = _jax.random.split(_jax.random.fold_in(key, i + 1))
        out[name] = w
        out["m_" + name] = s * _jax.random.normal(km, w.shape, _jnp.float32)
        out["v_" + name] = (s * s) * _jax.random.uniform(kv, w.shape, _jnp.float32, 0.5, 1.5)
    if N_MICROBATCH > 1:
        for name, axis in PER_EXAMPLE_BATCH_AXIS.items():
            out[name] = _to_microbatches(out[name], axis)
    return {'x': out['x'], 'c': out['c'], 'w_mod': out['w_mod'], 'b_mod': out['b_mod'], 'g_norm': out['g_norm'], 'w_ffn1_gu': out['w_ffn1_gu'], 'w_ffn1_down': out['w_ffn1_down'], 'w_ffn2_gu': out['w_ffn2_gu'], 'w_ffn2_down': out['w_ffn2_down'], 'w_in': out['w_in'], 'w_out': out['w_out'], 'attn_sinks': out['attn_sinks'], 'lru_conv_w': out['lru_conv_w'], 'lru_conv_b': out['lru_conv_b'], 'lru_gate_a_w': out['lru_gate_a_w'], 'lru_gate_a_b': out['lru_gate_a_b'], 'lru_gate_x_w': out['lru_gate_x_w'], 'lru_gate_x_b': out['lru_gate_x_b'], 'lru_lambda': out['lru_lambda'], 'sc_conv_w': out['sc_conv_w'], 'g_final': out['g_final'], 'loss_target': out['loss_target'], 'm_w_mod': out['m_w_mod'], 'm_b_mod': out['m_b_mod'], 'm_g_norm': out['m_g_norm'], 'm_w_ffn1_gu': out['m_w_ffn1_gu'], 'm_w_ffn1_down': out['m_w_ffn1_down'], 'm_w_ffn2_gu': out['m_w_ffn2_gu'], 'm_w_ffn2_down': out['m_w_ffn2_down'], 'm_w_in': out['m_w_in'], 'm_w_out': out['m_w_out'], 'm_attn_sinks': out['m_attn_sinks'], 'm_lru_conv_w': out['m_lru_conv_w'], 'm_lru_conv_b': out['m_lru_conv_b'], 'm_lru_gate_a_w': out['m_lru_gate_a_w'], 'm_lru_gate_a_b': out['m_lru_gate_a_b'], 'm_lru_gate_x_w': out['m_lru_gate_x_w'], 'm_lru_gate_x_b': out['m_lru_gate_x_b'], 'm_lru_lambda': out['m_lru_lambda'], 'm_sc_conv_w': out['m_sc_conv_w'], 'm_g_final': out['m_g_final'], 'v_w_mod': out['v_w_mod'], 'v_b_mod': out['v_b_mod'], 'v_g_norm': out['v_g_norm'], 'v_w_ffn1_gu': out['v_w_ffn1_gu'], 'v_w_ffn1_down': out['v_w_ffn1_down'], 'v_w_ffn2_gu': out['v_w_ffn2_gu'], 'v_w_ffn2_down': out['v_w_ffn2_down'], 'v_w_in': out['v_w_in'], 'v_w_out': out['v_w_out'], 'v_attn_sinks': out['v_attn_sinks'], 'v_lru_conv_w': out['v_lru_conv_w'], 'v_lru_conv_b': out['v_lru_conv_b'], 'v_lru_gate_a_w': out['v_lru_gate_a_w'], 'v_lru_gate_a_b': out['v_lru_gate_a_b'], 'v_lru_gate_x_w': out['v_lru_gate_x_w'], 'v_lru_gate_x_b': out['v_lru_gate_x_b'], 'v_lru_lambda': out['v_lru_lambda'], 'v_sc_conv_w': out['v_sc_conv_w'], 'v_g_final': out['v_g_final']}


def _loss(weights, diff, rest, loss_target):
    with _jax.named_scope("forward"):
        args = {**rest, TWIN_DIFF_INPUT: diff, **{k: w.astype(_WEIGHT_DTYPES[k]) for k, w in weights.items()}}
        y = _forward(args)
    with _jax.named_scope("loss_head"):
        err = _jnp.square(y.astype(_jnp.float32) - loss_target)
        return 0.5 * _jnp.sum(_jnp.mean(err, axis=-1)) if err.ndim else 0.5 * err


def _adamw(w, g, m, v):
    m = ADAM_B1 * m + (1.0 - ADAM_B1) * g
    v = ADAM_B2 * v + (1.0 - ADAM_B2) * _jnp.square(g)
    m_hat = m / (1.0 - ADAM_B1 ** ADAM_STEP)
    v_hat = v / (1.0 - ADAM_B2 ** ADAM_STEP)
    delta = -ADAM_LR * (m_hat / (_jnp.sqrt(v_hat) + ADAM_EPS) + ADAM_WD * w)
    return delta, m, v


def reference(x, c, w_mod, b_mod, g_norm, w_ffn1_gu, w_ffn1_down, w_ffn2_gu, w_ffn2_down, w_in, w_out, attn_sinks, lru_conv_w, lru_conv_b, lru_gate_a_w, lru_gate_a_b, lru_gate_x_w, lru_gate_x_b, lru_lambda, sc_conv_w, g_final, loss_target, m_w_mod, m_b_mod, m_g_norm, m_w_ffn1_gu, m_w_ffn1_down, m_w_ffn2_gu, m_w_ffn2_down, m_w_in, m_w_out, m_attn_sinks, m_lru_conv_w, m_lru_conv_b, m_lru_gate_a_w, m_lru_gate_a_b, m_lru_gate_x_w, m_lru_gate_x_b, m_lru_lambda, m_sc_conv_w, m_g_final, v_w_mod, v_b_mod, v_g_norm, v_w_ffn1_gu, v_w_ffn1_down, v_w_ffn2_gu, v_w_ffn2_down, v_w_in, v_w_out, v_attn_sinks, v_lru_conv_w, v_lru_conv_b, v_lru_gate_a_w, v_lru_gate_a_b, v_lru_gate_x_w, v_lru_gate_x_b, v_lru_lambda, v_sc_conv_w, v_g_final):
    given = dict(x=x, c=c, w_mod=w_mod, b_mod=b_mod, g_norm=g_norm, w_ffn1_gu=w_ffn1_gu, w_ffn1_down=w_ffn1_down, w_ffn2_gu=w_ffn2_gu, w_ffn2_down=w_ffn2_down, w_in=w_in, w_out=w_out, attn_sinks=attn_sinks, lru_conv_w=lru_conv_w, lru_conv_b=lru_conv_b, lru_gate_a_w=lru_gate_a_w, lru_gate_a_b=lru_gate_a_b, lru_gate_x_w=lru_gate_x_w, lru_gate_x_b=lru_gate_x_b, lru_lambda=lru_lambda, sc_conv_w=sc_conv_w, g_final=g_final, loss_target=loss_target, m_w_mod=m_w_mod, m_b_mod=m_b_mod, m_g_norm=m_g_norm, m_w_ffn1_gu=m_w_ffn1_gu, m_w_ffn1_down=m_w_ffn1_down, m_w_ffn2_gu=m_w_ffn2_gu, m_w_ffn2_down=m_w_ffn2_down, m_w_in=m_w_in, m_w_out=m_w_out, m_attn_sinks=m_attn_sinks, m_lru_conv_w=m_lru_conv_w, m_lru_conv_b=m_lru_conv_b, m_lru_gate_a_w=m_lru_gate_a_w, m_lru_gate_a_b=m_lru_gate_a_b, m_lru_gate_x_w=m_lru_gate_x_w, m_lru_gate_x_b=m_lru_gate_x_b, m_lru_lambda=m_lru_lambda, m_sc_conv_w=m_sc_conv_w, m_g_final=m_g_final, v_w_mod=v_w_mod, v_b_mod=v_b_mod, v_g_norm=v_g_norm, v_w_ffn1_gu=v_w_ffn1_gu, v_w_ffn1_down=v_w_ffn1_down, v_w_ffn2_gu=v_w_ffn2_gu, v_w_ffn2_down=v_w_ffn2_down, v_w_in=v_w_in, v_w_out=v_w_out, v_attn_sinks=v_attn_sinks, v_lru_conv_w=v_lru_conv_w, v_lru_conv_b=v_lru_conv_b, v_lru_gate_a_w=v_lru_gate_a_w, v_lru_gate_a_b=v_lru_gate_a_b, v_lru_gate_x_w=v_lru_gate_x_w, v_lru_gate_x_b=v_lru_gate_x_b, v_lru_lambda=v_lru_lambda, v_sc_conv_w=v_sc_conv_w, v_g_final=v_g_final)
    weights = {n: given[n] for n in TWIN_WEIGHTS}
    shared = {n: given[n] for n in SHARED_INPUTS}
    per_example = {n: given[n] for n in ['x', 'c']}
    grad_fn = _jax.value_and_grad(_loss, argnums=(0, 1))

    def one_microbatch(ex, loss_target):
        ex = dict(ex)
        diff = ex.pop(TWIN_DIFF_INPUT)
        return grad_fn(weights, diff, {**shared, **ex}, loss_target)

    if N_MICROBATCH == 1:
        loss, (grad_w, grad_x) = one_microbatch(per_example, given["loss_target"])
    else:
        def body(carry, xs):
            loss_sum, grad_sum = carry
            l_k, (gw_k, gx_k) = one_microbatch(xs[0], xs[1])
            with _jax.named_scope("update"):
                return (loss_sum + l_k, _jax.tree.map(_jnp.add, grad_sum, gw_k)), gx_k

        init = (_jnp.zeros((), _jnp.float32), _jax.tree.map(_jnp.zeros_like, weights))
        (loss, grad_w), grad_x = _jax.lax.scan(body, init, (per_example, given["loss_target"]))
    with _jax.named_scope("update"):
        delta_w, new_m, new_v = {}, {}, {}
        for n in TWIN_WEIGHTS:
            delta_w[n], new_m[n], new_v[n] = _adamw(weights[n], grad_w[n], given["m_" + n], given["v_" + n])
    return (loss, grad_x, *[grad_w[n] for n in TWIN_WEIGHTS], *[delta_w[n] for n in TWIN_WEIGHTS],
            *[new_m[n] for n in TWIN_WEIGHTS], *[new_v[n] for n in TWIN_WEIGHTS])
```

```python
import math

import jax
import jax.numpy as jnp
from jax import lax
from jax.experimental import pallas as pl
from jax.experimental.pallas import tpu as pltpu

F32 = jnp.float32
BF16 = jnp.bfloat16

D_MODEL = 1024
DEPTH = 2
HEAD_DIM = 64
N_Q_HEADS = 8
ATTN_WIDTH = 512
KV_WIDTH = 128
LRU_WIDTH = 256
CONV_WIDTH = 256
IN_PROJ_WIDTH = 2048
BLOCK = 128
D_FF = 2816
EPS = 1e-6
NEG_INF = -1e30
LRU_C = 8.0
N_CHIPS = 4
N_DEV = 8

C_Q, C_KV, C_LX, C_LG, C_SB, C_SC, C_SX = 0, 512, 768, 1024, 1280, 1536, 1792

ADAM_LR = 0.001
ADAM_B1 = 0.9
ADAM_B2 = 0.999
ADAM_EPS = 1e-08
ADAM_WD = 0.01
ADAM_STEP = 10

LANES = 128
SUBLANES = 8
VMEM_LIMIT = 56 * 1024 * 1024
MIX_TILE = 256

MESH = pl.DeviceIdType.MESH


def _cp(*sem):
    return pltpu.CompilerParams(dimension_semantics=sem, vmem_limit_bytes=VMEM_LIMIT)


def _tile(n, pref):
    t = min(n, pref)
    while n % t:
        t //= 2
    return t


def _sigmoid(v):
    return 1.0 / (1.0 + jnp.exp(-v))


def _expm1(v):
    series = v * (1.0 + v * (0.5 + v * (1.0 / 6.0 + v * (1.0 / 24.0 + v * (1.0 / 120.0)))))
    return jnp.where(v > -0.1, series, jnp.exp(v) - 1.0)


def _softplus_neg(lam):
    e = jnp.exp(-jnp.abs(lam))
    log1p = jnp.where(e < 1e-2, e * (1.0 - e * (0.5 - e * (1.0 / 3.0))), jnp.log(1.0 + e))
    return jnp.maximum(-lam, 0.0) + log1p


_GELU_K = math.sqrt(2.0 / math.pi)
_GELU_C = 0.044715


def _gelu(v):
    t = jnp.tanh(_GELU_K * (v + _GELU_C * v * v * v))
    return 0.5 * v * (1.0 + t), t


def _gelu_grad(v, t):
    return 0.5 * (1.0 + t) + 0.5 * v * (1.0 - t * t) * _GELU_K * (1.0 + 3.0 * _GELU_C * v * v)


def _dot(a, b):
    return jnp.dot(a, b, preferred_element_type=F32)


def _dot_nt(a, b):
    return lax.dot_general(a, b, (((1,), (1,)), ((), ())), preferred_element_type=F32)


def _dot_tn(a, b):
    return lax.dot_general(a, b, (((0,), (0,)), ((), ())), preferred_element_type=F32)


def _norm_matmul(x, nrm, w, out_dtype, name):
    S, Dm = x.shape
    N = w.shape[1]
    tm = _tile(S, 1024)
    tn = 1408 if N % 1408 == 0 else 1024

    def body(x_ref, nrm_ref, w_ref, o_ref, h_ref, hs):
        @pl.when(pl.program_id(1) == 0)
        def _():
            xv = x_ref[...]
            rstd = lax.rsqrt(jnp.mean(xv * xv, axis=-1, keepdims=True) + EPS)
            hn = (xv * rstd) * nrm_ref[0:1, :]
            hb = (hn * (1.0 + nrm_ref[2:3, :]) + nrm_ref[1:2, :]).astype(BF16)
            hs[...] = hb
            h_ref[...] = hb
        o_ref[...] = _dot(hs[...], w_ref[...]).astype(o_ref.dtype)

    return pl.pallas_call(
        body, name=name, grid=(S // tm, N // tn),
        in_specs=[pl.BlockSpec((tm, Dm), lambda i, n: (i, 0)),
                  pl.BlockSpec((8, Dm), lambda i, n: (0, 0)),
                  pl.BlockSpec((Dm, tn), lambda i, n: (0, n))],
        out_specs=[pl.BlockSpec((tm, tn), lambda i, n: (i, n)),
                   pl.BlockSpec((tm, Dm), lambda i, n: (i, 0))],
        out_shape=[jax.ShapeDtypeStruct((S, N), out_dtype), jax.ShapeDtypeStruct((S, Dm), BF16)],
        scratch_shapes=[pltpu.VMEM((tm, Dm), BF16)],
        compiler_params=_cp("arbitrary", "arbitrary"),
    )(x, nrm, w)


def _proj_residual(a, w, x, nrm, coef, swiglu, name):
    S, Ka = a.shape
    K, Dm = w.shape
    tm = _tile(S, 256)

    def body(a_ref, w_ref, x_ref, nrm_ref, o_ref, y_ref):
        if swiglu:
            g = a_ref[:, :K].astype(F32)
            u = a_ref[:, K:].astype(F32)
            act = (g * _sigmoid(g) * u).astype(BF16)
        else:
            act = a_ref[...]
        y = _dot(act, w_ref[...])
        o_ref[...] = x_ref[...] + (coef * nrm_ref[3:4, :]) * y
        y_ref[...] = y.astype(BF16)

    return pl.pallas_call(
        body, name=name, grid=(S // tm,),
        in_specs=[pl.BlockSpec((tm, Ka), lambda i: (i, 0)),
                  pl.BlockSpec((K, Dm), lambda i: (0, 0)),
                  pl.BlockSpec((tm, Dm), lambda i: (i, 0)),
                  pl.BlockSpec((8, Dm), lambda i: (0, 0))],
        out_specs=[pl.BlockSpec((tm, Dm), lambda i: (i, 0)),
                   pl.BlockSpec((tm, Dm), lambda i: (i, 0))],
        out_shape=[jax.ShapeDtypeStruct((S, Dm), F32), jax.ShapeDtypeStruct((S, Dm), BF16)],
        compiler_params=_cp("arbitrary"),
    )(a, w, x, nrm)


def _proj_residual_bwd(dxo, gu, y, w, nrm, coef, name):
    swiglu = gu is not None
    S, Dm = dxo.shape
    K = w.shape[0]
    Ka = 2 * K if swiglu else K
    tm = _tile(S, 256)

    def body(*refs):
        if swiglu:
            dxo_ref, y_ref, w_ref, nrm_ref, a_ref, da_ref, act_ref, dy_ref, dgate_ref = refs
        else:
            dxo_ref, y_ref, w_ref, nrm_ref, da_ref, dy_ref, dgate_ref = refs
        dxo_v = dxo_ref[...]
        dyb = ((coef * nrm_ref[3:4, :]) * dxo_v).astype(BF16)
        dy_ref[...] = dyb

        @pl.when(pl.program_id(0) == 0)
        def _():
            dgate_ref[...] = jnp.zeros_like(dgate_ref)
        dgate_ref[0:1, :] += jnp.sum(coef * y_ref[...].astype(F32) * dxo_v, axis=0, keepdims=True)

        dact = _dot_nt(dyb, w_ref[...])
        if swiglu:
            g = a_ref[:, :K].astype(F32)
            u = a_ref[:, K:].astype(F32)
            s = _sigmoid(g)
            si = g * s
            da_ref[:, :K] = (dact * u * (s * (1.0 + g * (1.0 - s)))).astype(BF16)
            da_ref[:, K:] = (dact * si).astype(BF16)
            act_ref[...] = (si * u).astype(BF16)
        else:
            da_ref[...] = dact

    row = lambda i: (i, 0)
    fix = lambda i: (0, 0)
    in_specs = [pl.BlockSpec((tm, Dm), row), pl.BlockSpec((tm, Dm), row),
                pl.BlockSpec((K, Dm), fix), pl.BlockSpec((8, Dm), fix)]
    args = [dxo, y, w, nrm]
    if swiglu:
        in_specs.append(pl.BlockSpec((tm, Ka), row))
        args.append(gu)
        out_specs = [pl.BlockSpec((tm, Ka), row), pl.BlockSpec((tm, K), row),
                     pl.BlockSpec((tm, Dm), row), pl.BlockSpec((8, Dm), fix)]
        out_shape = [jax.ShapeDtypeStruct((S, Ka), BF16), jax.ShapeDtypeStruct((S, K), BF16),
                     jax.ShapeDtypeStruct((S, Dm), BF16), jax.ShapeDtypeStruct((8, Dm), F32)]
    else:
        out_specs = [pl.BlockSpec((tm, Ka), row), pl.BlockSpec((tm, Dm), row), pl.BlockSpec((8, Dm), fix)]
        out_shape = [jax.ShapeDtypeStruct((S, Ka), F32), jax.ShapeDtypeStruct((S, Dm), BF16),
                     jax.ShapeDtypeStruct((8, Dm), F32)]
    return pl.pallas_call(
        body, name=name, grid=(S // tm,), in_specs=in_specs, out_specs=out_specs, out_shape=out_shape,
        compiler_params=_cp("arbitrary"),
    )(*args)


def _atb(a, b, out_dtype, bm, bn, name):
    S, M = a.shape
    N = b.shape[1]
    bk = _tile(S, 512)
    nk = S // bk

    def body(a_ref, b_ref, o_ref, acc):
        k = pl.program_id(2)

        @pl.when(k == 0)
        def _():
            acc[...] = jnp.zeros_like(acc)
        acc[...] += _dot_tn(a_ref[...], b_ref[...])

        @pl.when(k == nk - 1)
        def _():
            o_ref[...] = acc[...].astype(o_ref.dtype)

    return pl.pallas_call(
        body, name=name, grid=(M // bm, N // bn, nk),
        in_specs=[pl.BlockSpec((bk, bm), lambda m, n, k: (k, m)),
                  pl.BlockSpec((bk, bn), lambda m, n, k: (k, n))],
        out_specs=pl.BlockSpec((bm, bn), lambda m, n, k: (m, n)),
        out_shape=jax.ShapeDtypeStruct((M, N), out_dtype),
        scratch_shapes=[pltpu.VMEM((bm, bn), F32)],
        compiler_params=_cp("arbitrary", "arbitrary", "arbitrary"),
    )(a, b)


def _nt_norm_bwd(dout, w, x, nrm, dxo, name):
    S, N = dout.shape
    Dm = w.shape[0]
    tm = _tile(S, 512)
    tk = 1408 if N % 1408 == 0 else 1024
    nk = N // tk

    def body(do_ref, w_ref, x_ref, nrm_ref, dxo_ref, dx_ref, red_ref, acc):
        i = pl.program_id(0)
        k = pl.program_id(1)

        @pl.when(k == 0)
        def _():
            acc[...] = jnp.zeros_like(acc)
        acc[...] += _dot_nt(do_ref[...], w_ref[...])

        @pl.when(k == nk - 1)
        def _():
            @pl.when(i == 0)
            def _():
                red_ref[...] = jnp.zeros_like(red_ref)
            dh = acc[...]
            xv = x_ref[...]
            rstd = lax.rsqrt(jnp.mean(xv * xv, axis=-1, keepdims=True) + EPS)
            xn = xv * rstd
            gain = nrm_ref[0:1, :]
            hn = xn * gain
            dhn = dh * (1.0 + nrm_ref[2:3, :])
            red_ref[0:1, :] += jnp.sum(dh, axis=0, keepdims=True)
            red_ref[1:2, :] += jnp.sum(dh * hn, axis=0, keepdims=True)
            red_ref[2:3, :] += jnp.sum(dhn * xn, axis=0, keepdims=True)
            dxn = dhn * gain
            dx = rstd * (dxn - xn * jnp.mean(dxn * xn, axis=-1, keepdims=True))
            dx_ref[...] = dxo_ref[...] + dx

    return pl.pallas_call(
        body, name=name, grid=(S // tm, nk),
        in_specs=[pl.BlockSpec((tm, tk), lambda i, k: (i, k)),
                  pl.BlockSpec((Dm, tk), lambda i, k: (0, k)),
                  pl.BlockSpec((tm, Dm), lambda i, k: (i, 0)),
                  pl.BlockSpec((8, Dm), lambda i, k: (0, 0)),
                  pl.BlockSpec((tm, Dm), lambda i, k: (i, 0))],
        out_specs=[pl.BlockSpec((tm, Dm), lambda i, k: (i, 0)),
                   pl.BlockSpec((8, Dm), lambda i, k: (0, 0))],
        out_shape=[jax.ShapeDtypeStruct((S, Dm), F32), jax.ShapeDtypeStruct((8, Dm), F32)],
        scratch_shapes=[pltpu.VMEM((tm, Dm), F32)],
        compiler_params=_cp("arbitrary", "arbitrary"),
    )(dout, w, x, nrm, dxo)


def _final_loss(x, gf, tgt, name):
    S, Dm = x.shape
    tm = _tile(S, 512)

    def body(x_ref, g_ref, t_ref, dx_ref, st_ref):
        @pl.when(pl.program_id(0) == 0)
        def _():
            st_ref[...] = jnp.zeros_like(st_ref)
        xv = x_ref[...]
        rstd = lax.rsqrt(jnp.mean(xv * xv, axis=-1, keepdims=True) + EPS)
        xn = xv * rstd
        gain = g_ref[0:1, :]
        err = xn * gain - t_ref[...]
        st_ref[1:2, :] += jnp.full((1, Dm), 0.5 / Dm, F32) * jnp.sum(err * err)
        dy = err * (1.0 / Dm)
        st_ref[0:1, :] += jnp.sum(dy * xn, axis=0, keepdims=True)
        dxn = dy * gain
        dx_ref[...] = rstd * (dxn - xn * jnp.mean(dxn * xn, axis=-1, keepdims=True))

    return pl.pallas_call(
        body, name=name, grid=(S // tm,),
        in_specs=[pl.BlockSpec((tm, Dm), lambda i: (i, 0)),
                  pl.BlockSpec((8, Dm), lambda i: (0, 0)),
                  pl.BlockSpec((tm, Dm), lambda i: (i, 0))],
        out_specs=[pl.BlockSpec((tm, Dm), lambda i: (i, 0)),
                   pl.BlockSpec((8, Dm), lambda i: (0, 0))],
        out_shape=[jax.ShapeDtypeStruct((S, Dm), F32), jax.ShapeDtypeStruct((8, Dm), F32)],
        compiler_params=_cp("arbitrary"),
    )(x, gf, tgt)


def _alibi_slope(h):
    return float(2.0 ** (-8.0 * (h + 1) / N_Q_HEADS))


def _head_planes(pair_cols):
    lane = lax.broadcasted_iota(jnp.int32, pair_cols.shape, 1)
    low = lane < HEAD_DIM
    h0_lo = jnp.where(low, pair_cols, 0.0)
    h1_hi = jnp.where(low, 0.0, pair_cols)
    h0_hi = pltpu.roll(h0_lo, HEAD_DIM, 1)
    h1_lo = pltpu.roll(h1_hi, HEAD_DIM, 1)
    return ((h0_lo.astype(BF16), h0_hi.astype(BF16)), (h1_lo.astype(BF16), h1_hi.astype(BF16)))


def _to_plane(v, e, g):
    lane = lax.broadcasted_iota(jnp.int32, v.shape, 1)
    keep = (lane < HEAD_DIM) if e == 0 else (lane >= HEAD_DIM)
    v = jnp.where(keep, v, 0.0)
    return v if e == g else pltpu.roll(v, HEAD_DIM, 1)


def _band_mask(first_block):
    qi = lax.broadcasted_iota(jnp.int32, (BLOCK, 2 * BLOCK), 0)
    kj = lax.broadcasted_iota(jnp.int32, (BLOCK, 2 * BLOCK), 1)
    dist = qi + BLOCK - kj
    valid = (dist >= 0) & (dist < BLOCK) & (kj >= first_block * BLOCK)
    return dist.astype(F32), valid


def _softmax_band(qp, kx, h, dist, valid, sink):
    s = _dot_nt(qp, kx)
    s = jnp.where(valid, s - _alibi_slope(h) * dist, NEG_INF)
    m = jnp.maximum(jnp.max(s, axis=-1, keepdims=True), sink)
    p = jnp.exp(s - m)
    den = jnp.sum(p, axis=-1, keepdims=True) + jnp.exp(sink - m)
    return p / den, jnp.exp(sink - m) / den


def _past(cur, prev, s, row):
    return jnp.where(row < s, pltpu.roll(prev, s, 0), pltpu.roll(cur, s, 0))


def _future(cur, nxt, s, row):
    T = cur.shape[0]
    return jnp.where(row >= T - s, pltpu.roll(nxt, T - s, 0), pltpu.roll(cur, T - s, 0))


def _edge_row(v, last):
    T = v.shape[0]
    r8 = lax.broadcasted_iota(jnp.int32, (SUBLANES, v.shape[1]), 0)
    blk = v[T - SUBLANES:, :] if last else v[:SUBLANES, :]
    return jnp.sum(jnp.where(r8 == (SUBLANES - 1 if last else 0), blk, 0.0), axis=0, keepdims=True)


def _lru_gates(lx, lx_prev, small_ref, wa_ref, wx_ref, row, t0):
    xc = (small_ref[4:5, :] + small_ref[3:4, :] * lx + small_ref[2:3, :] * _past(lx, lx_prev, 1, row)
          + small_ref[1:2, :] * _past(lx, lx_prev, 2, row) + small_ref[0:1, :] * _past(lx, lx_prev, 3, row))
    xcb = xc.astype(BF16)
    r = _sigmoid(_dot(xcb, wa_ref[...]) + small_ref[5:6, :])
    ig = _sigmoid(_dot(xcb, wx_ref[...]) + small_ref[6:7, :])
    sp = _softplus_neg(small_ref[7:8, :])
    la = (-LRU_C) * r * sp
    a = jnp.exp(la)
    first = (row + t0) == 0
    mult = jnp.where(first, 1.0, jnp.sqrt(-_expm1(2.0 * la)))
    return xc, xcb, r, ig, sp, a, mult, first


def _mixer_fwd(proj, sinks, small, wa, wx, name):
    S = proj.shape[0]
    T = MIX_TILE
    nT = S // T
    nb = T // BLOCK

    def body(proj_ref, sink_ref, small_ref, wa_ref, wx_ref, y_ref, hp_ref, kvp, lxp, zp, hcar):
        i = pl.program_id(0)

        @pl.when(i == 0)
        def _():
            kvp[...] = jnp.zeros_like(kvp)
            lxp[...] = jnp.zeros_like(lxp)
            zp[...] = jnp.zeros_like(zp)
            hcar[...] = jnp.zeros_like(hcar)

        row = lax.broadcasted_iota(jnp.int32, (T, LRU_WIDTH), 0)

        kv = proj_ref[:, C_KV:C_KV + 2 * KV_WIDTH]
        ext = jnp.concatenate([kvp[...], kv], axis=0)
        kx = _head_planes(ext[:, :KV_WIDTH])
        vx = _head_planes(ext[:, KV_WIDTH:])
        first_tile = jnp.where(i == 0, 1, 0)
        for b in range(nb):
            dist, valid = _band_mask(first_tile if b == 0 else 0)
            keys = slice(b * BLOCK, (b + 2) * BLOCK)
            for pair in range(N_Q_HEADS // 2):
                g = pair // 2
                qp = (proj_ref[b * BLOCK:(b + 1) * BLOCK, pair * LANES:(pair + 1) * LANES] * 0.125).astype(BF16)
                o = jnp.zeros((BLOCK, LANES), F32)
                for e in range(2):
                    h = 2 * pair + e
                    pn, _ = _softmax_band(qp, kx[g][e][keys], h, dist, valid, sink_ref[h])
                    o = o + _dot(pn.astype(BF16), vx[g][e][keys])
                y_ref[b * BLOCK:(b + 1) * BLOCK, pair * LANES:(pair + 1) * LANES] = o.astype(BF16)
        kvp[...] = kv[T - BLOCK:, :]

        lx = proj_ref[:, C_LX:C_LX + LRU_WIDTH]
        xc, _, _, ig, _, a, mult, _ = _lru_gates(lx, lxp[...], small_ref, wa_ref, wx_ref, row, i * T)
        lxp[...] = lx
        aa = a
        bb = mult * (ig * xc)
        s = 1
        while s < T:
            a_sh = jnp.where(row >= s, pltpu.roll(aa, s, 0), 1.0)
            b_sh = jnp.where(row >= s, pltpu.roll(bb, s, 0), 0.0)
            bb = aa * b_sh + bb
            aa = aa * a_sh
            s *= 2
        hc = hcar[0:1, :]
        hh = bb + aa * hc
        hp_ref[...] = jnp.where(row < 1, hc, pltpu.roll(hh, 1, 0))
        hcar[...] = jnp.broadcast_to(_edge_row(hh, True), hcar.shape)
        gl, _ = _gelu(proj_ref[:, C_LG:C_LG + LRU_WIDTH])
        y_ref[:, ATTN_WIDTH:ATTN_WIDTH + LRU_WIDTH] = (gl * hh).astype(BF16)

        z = proj_ref[:, C_SC:C_SC + CONV_WIDTH] * proj_ref[:, C_SX:C_SX + CONV_WIDTH]
        c3 = (small_ref[10:11, :] * z + small_ref[9:10, :] * _past(z, zp[...], 1, row)
              + small_ref[8:9, :] * _past(z, zp[...], 2, row))
        zp[...] = z
        y_ref[:, ATTN_WIDTH + LRU_WIDTH:] = (proj_ref[:, C_SB:C_SB + CONV_WIDTH] * c3).astype(BF16)

    fix = lambda i: (0, 0)
    return pl.pallas_call(
        body, name=name, grid=(nT,),
        in_specs=[pl.BlockSpec((T, IN_PROJ_WIDTH), lambda i: (i, 0)),
                  pl.BlockSpec(memory_space=pltpu.SMEM),
                  pl.BlockSpec((16, LRU_WIDTH), fix),
                  pl.BlockSpec((LRU_WIDTH, LRU_WIDTH), fix),
                  pl.BlockSpec((LRU_WIDTH, LRU_WIDTH), fix)],
        out_specs=[pl.BlockSpec((T, D_MODEL), lambda i: (i, 0)),
                   pl.BlockSpec((T, LRU_WIDTH), lambda i: (i, 0))],
        out_shape=[jax.ShapeDtypeStruct((S, D_MODEL), BF16), jax.ShapeDtypeStruct((S, LRU_WIDTH), F32)],
        scratch_shapes=[pltpu.VMEM((BLOCK, 2 * KV_WIDTH), F32), pltpu.VMEM((T, LRU_WIDTH), F32),
                        pltpu.VMEM((T, CONV_WIDTH), F32), pltpu.VMEM((SUBLANES, LRU_WIDTH), F32)],
        compiler_params=_cp("arbitrary"),
    )(proj, sinks, small, wa, wx)


def _mixer_bwd(proj, dymix, hprev, sinks, small, wa, wx, name):
    S = proj.shape[0]
    T = MIX_TILE
    nT = S // T
    nb = T // BLOCK
    bpt = T // BLOCK

    def body(proj_ref, kvprev_ref, lxprev_ref, scprev_ref, sxprev_ref, dy_ref, hp_ref, sink_ref, small_ref,
             wa_ref, wx_ref, dp_ref, dsm_ref, dsink_ref, dwa_ref, dwx_ref,
             dk_s, dv_s, dkv_c, dxc_n, dc3_n, p_c):
        i = pl.program_id(0)
        ti = nT - 1 - i
        has_prev = jnp.where(ti == 0, 0.0, 1.0)

        @pl.when(i == 0)
        def _():
            for r in (dkv_c, dxc_n, dc3_n, p_c, dsm_ref, dsink_ref, dwa_ref, dwx_ref):
                r[...] = jnp.zeros_like(r)

        row = lax.broadcasted_iota(jnp.int32, (T, LRU_WIDTH), 0)

        kv = proj_ref[:, C_KV:C_KV + 2 * KV_WIDTH]
        ext = jnp.concatenate([kvprev_ref[...] * has_prev, kv], axis=0)
        kx = _head_planes(ext[:, :KV_WIDTH])
        vx = _head_planes(ext[:, KV_WIDTH:])
        dk_s[...] = jnp.zeros_like(dk_s)
        dv_s[...] = jnp.zeros_like(dv_s)
        dk_s[T:, :] = dkv_c[:, :KV_WIDTH]
        dv_s[T:, :] = dkv_c[:, KV_WIDTH:]
        first_tile = jnp.where(ti == 0, 1, 0)
        for b in range(nb):
            dist, valid = _band_mask(first_tile if b == 0 else 0)
            keys = slice(b * BLOCK, (b + 2) * BLOCK)
            rows = slice(b * BLOCK, (b + 1) * BLOCK)
            for pair in range(N_Q_HEADS // 2):
                g = pair // 2
                cols = slice(pair * LANES, (pair + 1) * LANES)
                qp = (proj_ref[rows, cols] * 0.125).astype(BF16)
                dob = dy_ref[rows, cols].astype(BF16)
                dq = jnp.zeros((BLOCK, LANES), F32)
                for e in range(2):
                    h = 2 * pair + e
                    pn, psink = _softmax_band(qp, kx[g][e][keys], h, dist, valid, sink_ref[h])
                    dpm = _dot_nt(dob, vx[g][e][keys])
                    dsum = jnp.sum(pn * dpm, axis=-1, keepdims=True)
                    ds = (pn * (dpm - dsum)).astype(BF16)
                    dsink_ref[h:h + 1, :] += jnp.full((1, LANES), -1.0, F32) * jnp.sum(psink * dsum)
                    dv_s[keys, :] += _to_plane(_dot_tn(pn.astype(BF16), dob), e, g)
                    dk_s[keys, :] += _to_plane(_dot_tn(ds, qp), e, g)
                    dq = dq + _dot(ds, kx[g][e][keys])
                dp_ref[rows, cols] = (0.125 * dq).astype(BF16)
        dp_ref[:, C_KV:C_KV + KV_WIDTH] = dk_s[BLOCK:, :].astype(BF16)
        dp_ref[:, C_KV + KV_WIDTH:C_KV + 2 * KV_WIDTH] = dv_s[BLOCK:, :].astype(BF16)
        dkv_c[:, :KV_WIDTH] = dk_s[:BLOCK, :]
        dkv_c[:, KV_WIDTH:] = dv_s[:BLOCK, :]

        lx = proj_ref[:, C_LX:C_LX + LRU_WIDTH]
        lxprev = lxprev_ref[...] * has_prev
        xc, xcb, r, ig, sp, a, mult, first = _lru_gates(lx, lxprev, small_ref, wa_ref, wx_ref, row, ti * T)
        hp = hp_ref[...]
        hh = a * hp + mult * (ig * xc)
        lg = proj_ref[:, C_LG:C_LG + LRU_WIDTH]
        gl, th = _gelu(lg)
        dyl = dy_ref[:, ATTN_WIDTH:ATTN_WIDTH + LRU_WIDTH]
        dp_ref[:, C_LG:C_LG + LRU_WIDTH] = (dyl * hh * _gelu_grad(lg, th)).astype(BF16)
        aa = jnp.where(row < T - 1, pltpu.roll(a, T - 1, 0), 1.0)
        bb = dyl * gl
        s = 1
        while s < T:
            a_sh = jnp.where(row < T - s, pltpu.roll(aa, T - s, 0), 1.0)
            b_sh = jnp.where(row < T - s, pltpu.roll(bb, T - s, 0), 0.0)
            bb = bb + aa * b_sh
            aa = aa * a_sh
            s *= 2
        G = bb + aa * p_c[0:1, :]
        p_c[...] = jnp.broadcast_to(_edge_row(a * G, False), p_c.shape)
        da = G * hp
        dmult = G * (ig * xc)
        dig = G * mult * xc
        dxc = G * mult * ig
        dla = da * a + dmult * jnp.where(first, 0.0, -(a * a) / mult)
        dr = dla * ((-LRU_C) * sp)
        lam = small_ref[7:8, :]
        dsm_ref[7:8, :] += jnp.sum(dla * ((-LRU_C) * r), axis=0, keepdims=True) * (-_sigmoid(-lam))
        dpa = dr * r * (1.0 - r)
        dpx = dig * ig * (1.0 - ig)
        dsm_ref[5:6, :] += jnp.sum(dpa, axis=0, keepdims=True)
        dsm_ref[6:7, :] += jnp.sum(dpx, axis=0, keepdims=True)
        dpab = dpa.astype(BF16)
        dpxb = dpx.astype(BF16)
        dwa_ref[...] += _dot_tn(xcb, dpab)
        dwx_ref[...] += _dot_tn(xcb, dpxb)
        dxc = dxc + _dot_nt(dpab, wa_ref[...]) + _dot_nt(dpxb, wx_ref[...])
        dsm_ref[4:5, :] += jnp.sum(dxc, axis=0, keepdims=True)
        dsm_ref[3:4, :] += jnp.sum(dxc * lx, axis=0, keepdims=True)
        for k in range(3):
            dsm_ref[k:k + 1, :] += jnp.sum(dxc * _past(lx, lxprev, 3 - k, row), axis=0, keepdims=True)
        nxt = dxc_n[...]
        dlx = (small_ref[3:4, :] * dxc + small_ref[2:3, :] * _future(dxc, nxt, 1, row)
               + small_ref[1:2, :] * _future(dxc, nxt, 2, row) + small_ref[0:1, :] * _future(dxc, nxt, 3, row))
        dxc_n[...] = dxc
        dp_ref[:, C_LX:C_LX + LRU_WIDTH] = dlx.astype(BF16)

        sc = proj_ref[:, C_SC:C_SC + CONV_WIDTH]
        sx = proj_ref[:, C_SX:C_SX + CONV_WIDTH]
        sb = proj_ref[:, C_SB:C_SB + CONV_WIDTH]
        z = sc * sx
        zprev = (scprev_ref[...] * sxprev_ref[...]) * has_prev
        z1 = _past(z, zprev, 1, row)
        z2 = _past(z, zprev, 2, row)
        c3 = small_ref[10:11, :] * z + small_ref[9:10, :] * z1 + small_ref[8:9, :] * z2
        dys = dy_ref[:, ATTN_WIDTH + LRU_WIDTH:]
        dp_ref[:, C_SB:C_SB + CONV_WIDTH] = (dys * c3).astype(BF16)
        dc3 = dys * sb
        dsm_ref[10:11, :] += jnp.sum(dc3 * z, axis=0, keepdims=True)
        dsm_ref[9:10, :] += jnp.sum(dc3 * z1, axis=0, keepdims=True)
        dsm_ref[8:9, :] += jnp.sum(dc3 * z2, axis=0, keepdims=True)
        nxt3 = dc3_n[...]
        dz = (small_ref[10:11, :] * dc3 + small_ref[9:10, :] * _future(dc3, nxt3, 1, row)
              + small_ref[8:9, :] * _future(dc3, nxt3, 2, row))
        dc3_n[...] = dc3
        dp_ref[:, C_SC:C_SC + CONV_WIDTH] = (dz * sx).astype(BF16)
        dp_ref[:, C_SX:C_SX + CONV_WIDTH] = (dz * sc).astype(BF16)

    fix = lambda i: (0, 0)
    cur = lambda i: (nT - 1 - i, 0)
    prev_cols = lambda cb: (lambda i: (jnp.maximum(nT - 2 - i, 0), cb))
    return pl.pallas_call(
        body, name=name, grid=(nT,),
        in_specs=[pl.BlockSpec((T, IN_PROJ_WIDTH), cur),
                  pl.BlockSpec((BLOCK, 2 * KV_WIDTH),
                               lambda i: (jnp.maximum((nT - 1 - i) * bpt - 1, 0), C_KV // (2 * KV_WIDTH))),
                  pl.BlockSpec((T, LRU_WIDTH), prev_cols(C_LX // LRU_WIDTH)),
                  pl.BlockSpec((T, CONV_WIDTH), prev_cols(C_SC // CONV_WIDTH)),
                  pl.BlockSpec((T, CONV_WIDTH), prev_cols(C_SX // CONV_WIDTH)),
                  pl.BlockSpec((T, D_MODEL), cur),
                  pl.BlockSpec((T, LRU_WIDTH), cur),
                  pl.BlockSpec(memory_space=pltpu.SMEM),
                  pl.BlockSpec((16, LRU_WIDTH), fix),
                  pl.BlockSpec((LRU_WIDTH, LRU_WIDTH), fix),
                  pl.BlockSpec((LRU_WIDTH, LRU_WIDTH), fix)],
        out_specs=[pl.BlockSpec((T, IN_PROJ_WIDTH), cur),
                   pl.BlockSpec((16, LRU_WIDTH), fix),
                   pl.BlockSpec((SUBLANES, LANES), fix),
                   pl.BlockSpec((LRU_WIDTH, LRU_WIDTH), fix),
                   pl.BlockSpec((LRU_WIDTH, LRU_WIDTH), fix)],
        out_shape=[jax.ShapeDtypeStruct((S, IN_PROJ_WIDTH), BF16),
                   jax.ShapeDtypeStruct((16, LRU_WIDTH), F32),
                   jax.ShapeDtypeStruct((SUBLANES, LANES), F32),
                   jax.ShapeDtypeStruct((LRU_WIDTH, LRU_WIDTH), F32),
                   jax.ShapeDtypeStruct((LRU_WIDTH, LRU_WIDTH), F32)],
        scratch_shapes=[pltpu.VMEM((T + BLOCK, KV_WIDTH), F32), pltpu.VMEM((T + BLOCK, KV_WIDTH), F32),
                        pltpu.VMEM((BLOCK, 2 * KV_WIDTH), F32), pltpu.VMEM((T, LRU_WIDTH), F32),
                        pltpu.VMEM((T, CONV_WIDTH), F32), pltpu.VMEM((SUBLANES, LRU_WIDTH), F32)],
        compiler_params=_cp("arbitrary"),
    )(proj, proj, proj, proj, proj, dymix, hprev, sinks, small, wa, wx)


def _mod_matmul(c_all, w_mod, name):
    L, Dm, N = w_mod.shape
    R = c_all.shape[0]
    tn = 768

    def body(c_ref, w_ref, o_ref, ca_ref):
        cv = c_ref[...]
        ca = (cv * _sigmoid(cv)).astype(BF16)
        ca_ref[...] = ca
        o_ref[0] = _dot(ca, w_ref[0].astype(BF16))

    return pl.pallas_call(
        body, name=name, grid=(L, N // tn),
        in_specs=[pl.BlockSpec((R, Dm), lambda l, n: (0, 0)),
                  pl.BlockSpec((1, Dm, tn), lambda l, n: (l, 0, n))],
        out_specs=[pl.BlockSpec((1, R, tn), lambda l, n: (l, 0, n)), pl.BlockSpec((R, Dm), lambda l, n: (0, 0))],
        out_shape=[jax.ShapeDtypeStruct((L, R, N), F32), jax.ShapeDtypeStruct((R, Dm), BF16)],
        compiler_params=_cp("arbitrary", "arbitrary"),
    )(c_all, w_mod)


def _adamw(w, gs, m, v, name):
    R, C = w.shape
    tr = 8
    for cand in (512, 256, 128, 64, 32, 16, 8):
        if R % cand == 0 and cand * C * 4 <= (1 << 20):
            tr = cand
            break
    ng = len(gs)
    bc1 = 1.0 - ADAM_B1 ** ADAM_STEP
    bc2 = 1.0 - ADAM_B2 ** ADAM_STEP

    def body(*refs):
        w_ref = refs[0]
        g_refs = refs[1:1 + ng]
        m_ref, v_ref, go_ref, d_ref, mo_ref, vo_ref = refs[1 + ng:]
        g = g_refs[0][...].astype(F32)
        for gr in g_refs[1:]:
            g = g + gr[...].astype(F32)
        mn = ADAM_B1 * m_ref[...] + (1.0 - ADAM_B1) * g
        vn = ADAM_B2 * v_ref[...] + (1.0 - ADAM_B2) * (g * g)
        go_ref[...] = g
        mo_ref[...] = mn
        vo_ref[...] = vn
        d_ref[...] = (-ADAM_LR) * ((mn / bc1) / (jnp.sqrt(vn / bc2) + ADAM_EPS) + ADAM_WD * w_ref[...])

    spec = pl.BlockSpec((tr, C), lambda i: (i, 0))
    return pl.pallas_call(
        body, name=name, grid=(R // tr,),
        in_specs=[spec] * (3 + ng), out_specs=[spec] * 4,
        out_shape=[jax.ShapeDtypeStruct((R, C), F32)] * 4,
        compiler_params=_cp("arbitrary"),
    )(w, *gs, m, v)


def _sum4(recv, name):
    _, R, C = recv.shape
    tr = 8
    for cand in (512, 256, 128, 64, 32, 16):
        if R % cand == 0 and cand * C * 4 <= (1 << 20):
            tr = cand
            break

    def body(r_ref, o_ref):
        o_ref[...] = ((r_ref[0].astype(F32) + r_ref[1].astype(F32)) + r_ref[2].astype(F32)) + r_ref[3].astype(F32)

    return pl.pallas_call(
        body, name=name, grid=(R // tr,),
        in_specs=[pl.BlockSpec((4, tr, C), lambda i: (0, i, 0))],
        out_specs=pl.BlockSpec((tr, C), lambda i: (i, 0)),
        out_shape=jax.ShapeDtypeStruct((R, C), F32),
        compiler_params=_cp("arbitrary"),
    )(recv)


def _all_gather_small(v, name):
    M, N = v.shape

    def body(x_ref, out_ref, sum_ref, send_sems, recv_sems, local_sem):
        x, y, c = lax.axis_index("x"), lax.axis_index("y"), lax.axis_index("c")
        me, sibling = (x, y, c), (x, y, 1 - c)
        chips = [(1 - x, y), (x, 1 - y), (1 - x, 1 - y)]

        def rows(px, py, pc):
            return out_ref.at[pl.ds(pl.multiple_of((4 * px + 2 * py + pc) * M, SUBLANES), M), :]

        def copy(k, block, to, src=None):
            return pltpu.make_async_remote_copy(
                src_ref=rows(*block) if src is None else src, dst_ref=rows(*block),
                send_sem=send_sems.at[k], recv_sem=recv_sems.at[k], device_id=to, device_id_type=MESH)

        mine = pltpu.make_async_copy(x_ref, rows(*me), local_sem)
        mine.start()
        first = [copy(0, me, sibling, src=x_ref)]
        first += [copy(1 + j, me, (*chip, c), src=x_ref) for j, chip in enumerate(chips)]
        for cp in first:
            cp.start()
        passed = [copy(4 + j, (*chip, c), sibling) for j, chip in enumerate(chips)]
        for j, chip in enumerate(chips):
            copy(1 + j, (*chip, c), me).wait_recv()
            passed[j].start()
        copy(0, sibling, me).wait_recv()
        for j, chip in enumerate(chips):
            copy(4 + j, (*chip, 1 - c), me).wait_recv()
        for cp in first + passed:
            cp.wait_send()
        mine.wait()
        acc = out_ref[0:M, :]
        for d in range(1, N_DEV):
            acc = acc + out_ref[d * M:(d + 1) * M, :]
        sum_ref[...] = acc

    return pl.pallas_call(
        body, name=name,
        out_shape=[jax.ShapeDtypeStruct((N_DEV * M, N), F32), jax.ShapeDtypeStruct((M, N), F32)],
        in_specs=[pl.BlockSpec(memory_space=pltpu.VMEM)],
        out_specs=[pl.BlockSpec(memory_space=pltpu.VMEM), pl.BlockSpec(memory_space=pltpu.VMEM)],
        scratch_shapes=[pltpu.SemaphoreType.DMA((7,)), pltpu.SemaphoreType.DMA((7,)), pltpu.SemaphoreType.DMA],
        compiler_params=pltpu.CompilerParams(vmem_limit_bytes=VMEM_LIMIT),
    )(v)


_BIG = (("w_ffn1_gu", 2), ("w_ffn1_down", 1), ("w_ffn2_gu", 2), ("w_ffn2_down", 1), ("w_in", 2), ("w_out", 1))


def _window(ref, axis, j, width):
    start = pl.multiple_of(j * width, LANES if axis == 2 else 16)
    if axis == 2:
        return ref.at[:, :, pl.ds(start, width)]
    return ref.at[:, pl.ds(start, width), :]


def _chip_peers():
    x, y, c = lax.axis_index("x"), lax.axis_index("y"), lax.axis_index("c")
    return x, y, c, [(1 - x, y), (x, 1 - y), (1 - x, 1 - y)]


def _gather_weights(shards, axes, name):
    n = len(shards)
    widths = [s.shape[ax] for s, ax in zip(shards, axes)]
    fulls = [jax.ShapeDtypeStruct(tuple(d * N_CHIPS if k == ax else d for k, d in enumerate(s.shape)), s.dtype)
             for s, ax in zip(shards, axes)]

    def body(*refs):
        ins, outs = refs[:n], refs[n:2 * n]
        send_sems, recv_sems, local_sems = refs[2 * n:]
        x, y, c, chips = _chip_peers()
        j = 2 * x + y
        local = [pltpu.make_async_copy(ins[t], _window(outs[t], axes[t], j, widths[t]), local_sems.at[t])
                 for t in range(n)]
        for cp in local:
            cp.start()

        def copy(t, k, block_chip):
            px, py = chips[k]
            return pltpu.make_async_remote_copy(
                src_ref=ins[t], dst_ref=_window(outs[t], axes[t], block_chip, widths[t]),
                send_sem=send_sems.at[3 * t + k], recv_sem=recv_sems.at[3 * t + k],
                device_id=(px, py, c), device_id_type=MESH)

        sends = [copy(t, k, j) for t in range(n) for k in range(3)]
        for cp in sends:
            cp.start()
        for t in range(n):
            for k in range(3):
                px, py = chips[k]
                copy(t, k, 2 * px + py).wait_recv()
        for cp in sends:
            cp.wait_send()
        for cp in local:
            cp.wait()

    hbm = pl.BlockSpec(memory_space=pltpu.HBM)
    return pl.pallas_call(
        body, name=name, out_shape=fulls, in_specs=[hbm] * n, out_specs=[hbm] * n,
        scratch_shapes=[pltpu.SemaphoreType.DMA((3 * n,)), pltpu.SemaphoreType.DMA((3 * n,)),
                        pltpu.SemaphoreType.DMA((n,))],
    )(*shards)


def _scatter_grads(fulls, axes, name):
    n = len(fulls)
    widths = [f.shape[ax] // N_CHIPS for f, ax in zip(fulls, axes)]
    outs_shape = [jax.ShapeDtypeStruct((4,) + tuple(widths[t] if k == axes[t] else d for k, d in enumerate(f.shape)),
                                       f.dtype) for t, f in enumerate(fulls)]

    def body(*refs):
        ins, outs = refs[:n], refs[n:2 * n]
        send_sems, recv_sems, local_sems = refs[2 * n:]
        x, y, c, chips = _chip_peers()
        j = 2 * x + y
        local = [pltpu.make_async_copy(_window(ins[t], axes[t], j, widths[t]), outs[t].at[3], local_sems.at[t])
                 for t in range(n)]
        for cp in local:
            cp.start()

        def copy(t, k):
            px, py = chips[k]
            return pltpu.make_async_remote_copy(
                src_ref=_window(ins[t], axes[t], 2 * px + py, widths[t]), dst_ref=outs[t].at[k],
                send_sem=send_sems.at[3 * t + k], recv_sem=recv_sems.at[3 * t + k],
                device_id=(px, py, c), device_id_type=MESH)

        sends = [copy(t, k) for t in range(n) for k in range(3)]
        for cp in sends:
            cp.start()
        for cp in sends:
            cp.wait_recv()
        for cp in sends:
            cp.wait_send()
        for cp in local:
            cp.wait()

    hbm = pl.BlockSpec(memory_space=pltpu.HBM)
    return pl.pallas_call(
        body, name=name, out_shape=outs_shape, in_specs=[hbm] * n, out_specs=[hbm] * n,
        scratch_shapes=[pltpu.SemaphoreType.DMA((3 * n,)), pltpu.SemaphoreType.DMA((3 * n,)),
                        pltpu.SemaphoreType.DMA((n,))],
    )(*fulls)


def _swap_with_sibling(parts, name):
    n = len(parts)

    def body(*refs):
        ins, outs = refs[:n], refs[n:2 * n]
        send_sems, recv_sems = refs[2 * n:]
        x, y, c = lax.axis_index("x"), lax.axis_index("y"), lax.axis_index("c")
        copies = [pltpu.make_async_remote_copy(
            src_ref=ins[t], dst_ref=outs[t], send_sem=send_sems.at[t], recv_sem=recv_sems.at[t],
            device_id=(x, y, 1 - c), device_id_type=MESH) for t in range(n)]
        for cp in copies:
            cp.start()
        for cp in copies:
            cp.wait_recv()
        for cp in copies:
            cp.wait_send()

    hbm = pl.BlockSpec(memory_space=pltpu.HBM)
    return pl.pallas_call(
        body, name=name, out_shape=[jax.ShapeDtypeStruct(p.shape, p.dtype) for p in parts],
        in_specs=[hbm] * n, out_specs=[hbm] * n,
        scratch_shapes=[pltpu.SemaphoreType.DMA((n,)), pltpu.SemaphoreType.DMA((n,))],
    )(*parts)


def _pack(arrs, rows_multiple=SUBLANES):
    flat = jnp.concatenate([a.astype(F32).reshape(-1) for a in arrs])
    unit = rows_multiple * LANES
    total = -(-flat.shape[0] // unit) * unit
    return jnp.pad(flat, (0, total - flat.shape[0])).reshape(total // LANES, LANES)


def _unpack(flat, shapes):
    out, off = [], 0
    for shp in shapes:
        n = int(math.prod(shp))
        out.append(flat[off:off + n].reshape(shp))
        off += n
    return out


def _block_diag(w):
    out = jnp.zeros((LRU_WIDTH, LRU_WIDTH), F32)
    for h in range(4):
        out = lax.dynamic_update_slice(out, w[h], (h * HEAD_DIM, h * HEAD_DIM))
    return out


def _diag_blocks(w):
    return jnp.stack([w[h * HEAD_DIM:(h + 1) * HEAD_DIM, h * HEAD_DIM:(h + 1) * HEAD_DIM] for h in range(4)])


def _rows8(*rows):
    z = jnp.zeros((8 - len(rows), rows[0].shape[-1]), F32)
    return jnp.concatenate([jnp.stack(rows), z], axis=0)


def _ffn_fwd(x, nrm, w_gu, w_down, tag):
    gu, h = _norm_matmul(x, nrm, w_gu, BF16, tag + "_gu")
    x_out, y = _proj_residual(gu, w_down, x, nrm, 0.5, True, tag + "_down")
    return x_out, (x, h, gu, y)


def _ffn_bwd(dxo, saved, nrm, w_gu, w_down, tag):
    x, h, gu, y = saved
    dgu, act, dy, dgate = _proj_residual_bwd(dxo, gu, y, w_down, nrm, 0.5, tag + "_down_bwd")
    dw_down = _atb(act, dy, BF16, 1408, 1024, tag + "_dw_down")
    dw_gu = _atb(h, dgu, BF16, 1024, 1408, tag + "_dw_gu")
    dx, red = _nt_norm_bwd(dgu, w_gu, x, nrm, dxo, tag + "_gu_bwd")
    return dx, dw_gu, dw_down, (red[0], red[1], dgate[0]), red[2]


def kernel(x, c, w_mod, b_mod, g_norm, w_ffn1_gu, w_ffn1_down, w_ffn2_gu, w_ffn2_down, w_in, w_out, attn_sinks, lru_conv_w, lru_conv_b, lru_gate_a_w, lru_gate_a_b, lru_gate_x_w, lru_gate_x_b, lru_lambda, sc_conv_w, g_final, loss_target, m_w_mod, m_b_mod, m_g_norm, m_w_ffn1_gu, m_w_ffn1_down, m_w_ffn2_gu, m_w_ffn2_down, m_w_in, m_w_out, m_attn_sinks, m_lru_conv_w, m_lru_conv_b, m_lru_gate_a_w, m_lru_gate_a_b, m_lru_gate_x_w, m_lru_gate_x_b, m_lru_lambda, m_sc_conv_w, m_g_final, v_w_mod, v_b_mod, v_g_norm, v_w_ffn1_gu, v_w_ffn1_down, v_w_ffn2_gu, v_w_ffn2_down, v_w_in, v_w_out, v_attn_sinks, v_lru_conv_w, v_lru_conv_b, v_lru_gate_a_w, v_lru_gate_a_b, v_lru_gate_x_w, v_lru_gate_x_b, v_lru_lambda, v_sc_conv_w, v_g_final):
    W = dict(w_mod=w_mod, b_mod=b_mod, g_norm=g_norm, w_ffn1_gu=w_ffn1_gu, w_ffn1_down=w_ffn1_down,
             w_ffn2_gu=w_ffn2_gu, w_ffn2_down=w_ffn2_down, w_in=w_in, w_out=w_out, attn_sinks=attn_sinks,
             lru_conv_w=lru_conv_w, lru_conv_b=lru_conv_b, lru_gate_a_w=lru_gate_a_w, lru_gate_a_b=lru_gate_a_b,
             lru_gate_x_w=lru_gate_x_w, lru_gate_x_b=lru_gate_x_b, lru_lambda=lru_lambda, sc_conv_w=sc_conv_w,
             g_final=g_final)
    M1 = dict(w_mod=m_w_mod, b_mod=m_b_mod, g_norm=m_g_norm, w_ffn1_gu=m_w_ffn1_gu, w_ffn1_down=m_w_ffn1_down,
              w_ffn2_gu=m_w_ffn2_gu, w_ffn2_down=m_w_ffn2_down, w_in=m_w_in, w_out=m_w_out,
              attn_sinks=m_attn_sinks, lru_conv_w=m_lru_conv_w, lru_conv_b=m_lru_conv_b,
              lru_gate_a_w=m_lru_gate_a_w, lru_gate_a_b=m_lru_gate_a_b, lru_gate_x_w=m_lru_gate_x_w,
              lru_gate_x_b=m_lru_gate_x_b, lru_lambda=m_lru_lambda, sc_conv_w=m_sc_conv_w, g_final=m_g_final)
    V1 = dict(w_mod=v_w_mod, b_mod=v_b_mod, g_norm=v_g_norm, w_ffn1_gu=v_w_ffn1_gu, w_ffn1_down=v_w_ffn1_down,
              w_ffn2_gu=v_w_ffn2_gu, w_ffn2_down=v_w_ffn2_down, w_in=v_w_in, w_out=v_w_out,
              attn_sinks=v_attn_sinks, lru_conv_w=v_lru_conv_w, lru_conv_b=v_lru_conv_b,
              lru_gate_a_w=v_lru_gate_a_w, lru_gate_a_b=v_lru_gate_a_b, lru_gate_x_w=v_lru_gate_x_w,
              lru_gate_x_b=v_lru_gate_x_b, lru_lambda=v_lru_lambda, sc_conv_w=v_sc_conv_w, g_final=v_g_final)
    names = ["w_mod", "b_mod", "g_norm", "w_ffn1_gu", "w_ffn1_down", "w_ffn2_gu", "w_ffn2_down", "w_in", "w_out",
             "attn_sinks", "lru_conv_w", "lru_conv_b", "lru_gate_a_w", "lru_gate_a_b", "lru_gate_x_w",
             "lru_gate_x_b", "lru_lambda", "sc_conv_w", "g_final"]

    xs = x[0]
    tgt = loss_target[0]
    S = xs.shape[0]
    chip = 2 * lax.axis_index("x") + lax.axis_index("y")
    batch = 2 * chip + lax.axis_index("c")
    L = DEPTH

    fwd_shapes = [(D_MODEL,), g_norm.shape, lru_conv_w.shape, sc_conv_w.shape]
    gathered, _ = _all_gather_small(_pack([c[0], g_norm, lru_conv_w, sc_conv_w]), "gather_small_fwd")
    gathered = gathered.reshape(N_DEV, -1)
    c_all = gathered[:, :D_MODEL]
    per_chip = [_unpack(gathered[2 * jj], fwd_shapes) for jj in range(N_CHIPS)]
    g_norm_full = jnp.concatenate([p[1] for p in per_chip], axis=-1)
    lru_conv_w_full = jnp.concatenate([p[2] for p in per_chip], axis=-1)
    sc_conv_w_full = jnp.concatenate([p[3] for p in per_chip], axis=-1)

    c_pad = jnp.concatenate([c_all, jnp.zeros_like(c_all)], axis=0)
    mod_part, c_act = _mod_matmul(c_pad, w_mod, "mod_matmul")
    mod_all, _ = _all_gather_small(mod_part.reshape(-1, LANES), "gather_mod")
    mod_all = mod_all.reshape(N_DEV, L, 16, -1)
    mod_rows = [lax.dynamic_index_in_dim(mod_all[2 * jj], batch, axis=1, keepdims=False) for jj in range(N_CHIPS)]
    mod = (jnp.concatenate(mod_rows, axis=-1) + b_mod).reshape(L, 9, D_MODEL)

    def nrm_rows(l, s):
        return _rows8(g_norm_full[l, s], mod[l, 3 * s], mod[l, 3 * s + 1], mod[l, 3 * s + 2])

    big_axes = [ax for _, ax in _BIG]
    full = dict(zip([n for n, _ in _BIG],
                    _gather_weights([W[n].astype(BF16) for n, _ in _BIG], big_axes, "gather_weights")))

    def mixer_params(l):
        small = jnp.concatenate([lru_conv_w_full[l], lru_conv_b[l][None], lru_gate_a_b[l][None],
                                 lru_gate_x_b[l][None], lru_lambda[l][None], sc_conv_w_full[l],
                                 jnp.zeros((5, LRU_WIDTH), F32)], axis=0)
        return (attn_sinks[l], small, _block_diag(lru_gate_a_w[l]).astype(BF16),
                _block_diag(lru_gate_x_w[l]).astype(BF16))

    saved = []
    xcur = xs
    for l in range(L):
        n1, n2, n3 = nrm_rows(l, 0), nrm_rows(l, 1), nrm_rows(l, 2)
        xcur, s1 = _ffn_fwd(xcur, n1, full["w_ffn1_gu"][l], full["w_ffn1_down"][l], f"l{l}_ffn1")
        proj, h2 = _norm_matmul(xcur, n2, full["w_in"][l], F32, f"l{l}_mix_in")
        mp = mixer_params(l)
        ymix, hprev = _mixer_fwd(proj, *mp, f"l{l}_mix_core")
        x_mid, ymo = _proj_residual(ymix, full["w_out"][l], xcur, n2, 1.0, False, f"l{l}_mix_out")
        s2 = (xcur, h2, proj, ymix, ymo, hprev, mp)
        xcur, s3 = _ffn_fwd(x_mid, n3, full["w_ffn2_gu"][l], full["w_ffn2_down"][l], f"l{l}_ffn2")
        saved.append((n1, n2, n3, s1, s2, s3))

    dx, stats = _final_loss(xcur, _rows8(g_final), tgt, "final_loss")
    loss = lax.psum(stats[1, 0], ("x", "y", "c"))
    d_g_final = stats[0]

    gbig = {n: [None] * L for n, _ in _BIG}
    dmod, d_gnorm, d_small = [None] * L, [None] * L, [None] * L
    for l in reversed(range(L)):
        n1, n2, n3, s1, s2, s3 = saved[l]
        dx, gbig["w_ffn2_gu"][l], gbig["w_ffn2_down"][l], dm3, dg3 = _ffn_bwd(
            dx, s3, n3, full["w_ffn2_gu"][l], full["w_ffn2_down"][l], f"l{l}_ffn2")
        x_in, h2, proj, ymix, ymo, hprev, mp = s2
        dymix, dy, dgate = _proj_residual_bwd(dx, None, ymo, full["w_out"][l], n2, 1.0, f"l{l}_mix_out_bwd")
        gbig["w_out"][l] = _atb(ymix, dy, BF16, 1024, 1024, f"l{l}_dw_out")
        dproj, dsm, dsink, dwa, dwx = _mixer_bwd(proj, dymix, hprev, *mp, f"l{l}_mix_core_bwd")
        gbig["w_in"][l] = _atb(h2, dproj, BF16, 1024, 1024, f"l{l}_dw_in")
        dx, red = _nt_norm_bwd(dproj, full["w_in"][l], x_in, n2, dx, f"l{l}_mix_in_bwd")
        dm2, dg2 = (red[0], red[1], dgate[0]), red[2]
        dx, gbig["w_ffn1_gu"][l], gbig["w_ffn1_down"][l], dm1, dg1 = _ffn_bwd(
            dx, s1, n1, full["w_ffn1_gu"][l], full["w_ffn1_down"][l], f"l{l}_ffn1")
        dmod[l] = jnp.stack(list(dm1) + list(dm2) + list(dm3))
        d_gnorm[l] = jnp.stack([dg1, dg2, dg3])
        d_small[l] = (dsink[:, 0], dsm[0:4], dsm[4], _diag_blocks(dwa), dsm[5], _diag_blocks(dwx), dsm[6],
                      dsm[7], dsm[8:11])
    grad_x = dx[None]

    def both(k):
        return jnp.stack([d_small[0][k], d_small[1][k]])
    small_names = ["g_norm", "attn_sinks", "lru_conv_w", "lru_conv_b", "lru_gate_a_w", "lru_gate_a_b",
                   "lru_gate_x_w", "lru_gate_x_b", "lru_lambda", "sc_conv_w", "g_final"]
    small_parts = [jnp.stack(d_gnorm)] + [both(k) for k in range(9)] + [d_g_final]
    dmod_flat = jnp.stack(dmod).reshape(-1)
    bwd_gathered, bwd_sum = _all_gather_small(_pack([dmod_flat] + small_parts), "gather_small_bwd")
    n_mod = dmod_flat.shape[0]
    dmod_all = bwd_gathered.reshape(N_DEV, -1)[:, :n_mod].reshape(N_DEV, L, 9 * D_MODEL)
    bwd_sum = bwd_sum.reshape(-1)
    G = {"b_mod": bwd_sum[:n_mod].reshape(L, 9 * D_MODEL)}
    G.update(zip(small_names, _unpack(bwd_sum[n_mod:], [p.shape for p in small_parts])))
    for n in ("g_norm", "lru_conv_w", "sc_conv_w"):
        wdt = W[n].shape[-1]
        G[n] = lax.dynamic_slice_in_dim(G[n], chip * wdt, wdt, axis=G[n].ndim - 1)

    ncol = w_mod.shape[-1]
    dmod_cols = lax.dynamic_slice_in_dim(dmod_all, chip * ncol, ncol, axis=2)
    zeros8 = jnp.zeros((N_DEV, ncol), F32)
    g_w_mod = jnp.stack([_atb(c_act, jnp.concatenate([dmod_cols[:, l], zeros8], axis=0).astype(BF16), F32,
                              D_MODEL, 768, f"l{l}_dw_mod") for l in range(L)])

    recv = _scatter_grads([jnp.stack(gbig[n]) for n, _ in _BIG], big_axes, "scatter_grads")
    part = [_sum4(r.reshape(4, -1, r.shape[-1]), f"sum_{n}") for r, (n, _) in zip(recv, _BIG)]
    other = _swap_with_sibling(part, "swap_sibling")

    out_g, out_d, out_m, out_v = {}, {}, {}, {}
    for (n, _), p, o in zip(_BIG, part, other):
        shp = W[n].shape
        res = _adamw(W[n].reshape(p.shape), [p, o], M1[n].reshape(p.shape), V1[n].reshape(p.shape), f"adamw_{n}")
        out_g[n], out_d[n], out_m[n], out_v[n] = [r.reshape(shp) for r in res]
    res = _adamw(w_mod.reshape(-1, ncol), [g_w_mod.reshape(-1, ncol)], m_w_mod.reshape(-1, ncol),
                 v_w_mod.reshape(-1, ncol), "adamw_w_mod")
    out_g["w_mod"], out_d["w_mod"], out_m["w_mod"], out_v["w_mod"] = [r.reshape(w_mod.shape) for r in res]
    rest = ["b_mod"] + small_names
    shapes = [W[n].shape for n in rest]
    res = _adamw(_pack([W[n] for n in rest]), [_pack([G[n] for n in rest])], _pack([M1[n] for n in rest]),
                 _pack([V1[n] for n in rest]), "adamw_small")
    for dst, r in zip((out_g, out_d, out_m, out_v), res):
        dst.update(zip(rest, _unpack(r.reshape(-1), shapes)))

    return (loss, grad_x, *[out_g[n] for n in names], *[out_d[n] for n in names],
            *[out_m[n] for n in names], *[out_v[n] for n in names])
```

```python
import math

import jax
import jax.numpy as jnp
from jax import lax
from jax.experimental import pallas as pl
from jax.experimental.pallas import tpu as pltpu

F32 = jnp.float32
BF16 = jnp.bfloat16

D_MODEL = 1024
DEPTH = 2
HEAD_DIM = 64
N_Q_HEADS = 8
ATTN_WIDTH = 512
KV_WIDTH = 128
LRU_WIDTH = 256
CONV_WIDTH = 256
IN_PROJ_WIDTH = 2048
BLOCK = 128
D_FF = 2816
EPS = 1e-6
NEG_INF = -1e30
LRU_C = 8.0
N_CHIPS = 4
N_DEV = 8

C_Q, C_KV, C_LX, C_LG, C_SB, C_SC, C_SX = 0, 512, 768, 1024, 1280, 1536, 1792

ADAM_LR = 0.001
ADAM_B1 = 0.9
ADAM_B2 = 0.999
ADAM_EPS = 1e-08
ADAM_WD = 0.01
ADAM_STEP = 10

LANES = 128
SUBLANES = 8
VMEM_LIMIT = 56 * 1024 * 1024
MIX_TILE = 256

MESH = pl.DeviceIdType.MESH


def _cp(*sem):
    return pltpu.CompilerParams(dimension_semantics=sem, vmem_limit_bytes=VMEM_LIMIT)


def _tile(n, pref):
    t = min(n, pref)
    while n % t:
        t //= 2
    return t


def _sigmoid(v):
    return 1.0 / (1.0 + jnp.exp(-v))


def _expm1(v):
    series = v * (1.0 + v * (0.5 + v * (1.0 / 6.0 + v * (1.0 / 24.0 + v * (1.0 / 120.0)))))
    return jnp.where(v > -0.1, series, jnp.exp(v) - 1.0)


def _softplus_neg(lam):
    e = jnp.exp(-jnp.abs(lam))
    log1p = jnp.where(e < 1e-2, e * (1.0 - e * (0.5 - e * (1.0 / 3.0))), jnp.log(1.0 + e))
    return jnp.maximum(-lam, 0.0) + log1p


_GELU_K = math.sqrt(2.0 / math.pi)
_GELU_C = 0.044715


def _gelu(v):
    t = jnp.tanh(_GELU_K * (v + _GELU_C * v * v * v))
    return 0.5 * v * (1.0 + t), t


def _gelu_grad(v, t):
    return 0.5 * (1.0 + t) + 0.5 * v * (1.0 - t * t) * _GELU_K * (1.0 + 3.0 * _GELU_C * v * v)


def _dot(a, b):
    return jnp.dot(a, b, preferred_element_type=F32)


def _dot_nt(a, b):
    return lax.dot_general(a, b, (((1,), (1,)), ((), ())), preferred_element_type=F32)


def _dot_tn(a, b):
    return lax.dot_general(a, b, (((0,), (0,)), ((), ())), preferred_element_type=F32)


def _window(ref, axis, j, width):
    start = pl.multiple_of(j * width, LANES if axis == 1 else 16)
    if axis == 1:
        return ref.at[:, pl.ds(start, width)]
    return ref.at[pl.ds(start, width), :]


def _chip_peers():
    x, y, c = lax.axis_index("x"), lax.axis_index("y"), lax.axis_index("c")
    return x, y, c, [(1 - x, y), (x, 1 - y), (1 - x, 1 - y)]


class _GatherJob:
    def __init__(self, shard, axis):
        self.src, self.axis, self.width = shard, axis, shard.shape[axis]
        full = tuple(d * N_CHIPS if k == axis else d for k, d in enumerate(shard.shape))
        self.out_shape = jax.ShapeDtypeStruct(full, shard.dtype)

    def copies(self, src, dst, send, recv, loc, t):
        x, y, c, chips = _chip_peers()
        j = 2 * x + y
        local = pltpu.make_async_copy(src, _window(dst, self.axis, j, self.width), loc.at[t])

        def copy(k, block_chip):
            px, py = chips[k]
            return pltpu.make_async_remote_copy(
                src_ref=src, dst_ref=_window(dst, self.axis, block_chip, self.width),
                send_sem=send.at[3 * t + k], recv_sem=recv.at[3 * t + k], device_id=(px, py, c), device_id_type=MESH)

        sends = [copy(k, j) for k in range(3)]
        arrivals = [copy(k, 2 * chips[k][0] + chips[k][1]) for k in range(3)]
        return local, sends, arrivals


class _ScatterJob:
    def __init__(self, full, axis):
        self.src, self.axis, self.width = full, axis, full.shape[axis] // N_CHIPS
        shard = tuple(self.width if k == axis else d for k, d in enumerate(full.shape))
        self.out_shape = jax.ShapeDtypeStruct((4,) + shard, full.dtype)

    def copies(self, src, dst, send, recv, loc, t):
        x, y, c, chips = _chip_peers()
        local = pltpu.make_async_copy(_window(src, self.axis, 2 * x + y, self.width), dst.at[3], loc.at[t])
        sends = [pltpu.make_async_remote_copy(
            src_ref=_window(src, self.axis, 2 * px + py, self.width), dst_ref=dst.at[k],
            send_sem=send.at[3 * t + k], recv_sem=recv.at[3 * t + k], device_id=(px, py, c), device_id_type=MESH)
            for k, (px, py) in enumerate(chips)]
        return local, sends, sends


def _start_jobs(jobs, srcs, dsts, send, recv, loc):
    for t, job in enumerate(jobs):
        local, sends, _ = job.copies(srcs[t], dsts[t], send, recv, loc, t)
        local.start()
        for cp in sends:
            cp.start()


def _finish_jobs(jobs, srcs, dsts, send, recv, loc):
    for t, job in enumerate(jobs):
        local, sends, arrivals = job.copies(srcs[t], dsts[t], send, recv, loc, t)
        for cp in arrivals:
            cp.wait_recv()
        for cp in sends:
            cp.wait_send()
        local.wait()


def _job_scratch(n):
    return [pltpu.SemaphoreType.DMA((3 * n,)), pltpu.SemaphoreType.DMA((3 * n,)), pltpu.SemaphoreType.DMA((n,))]


def _pcall(body, *, name, grid, in_specs, out_specs, out_shape, sem, args, scratch_shapes=(), jobs=()):
    in_specs, out_specs, out_shape = list(in_specs), list(out_specs), list(out_shape)
    scratch_shapes = list(scratch_shapes)
    if not jobs:
        res = pl.pallas_call(body, name=name, grid=grid, in_specs=in_specs, out_specs=out_specs, out_shape=out_shape,
                             scratch_shapes=scratch_shapes, compiler_params=_cp(*sem))(*args)
        return list(res), []
    n_in, n_out, n_scr, nj = len(args), len(out_shape), len(scratch_shapes), len(jobs)

    def wrapped(*refs):
        ins, refs = refs[:n_in], refs[n_in:]
        jin, refs = refs[:nj], refs[nj:]
        outs, refs = refs[:n_out], refs[n_out:]
        jout, refs = refs[:nj], refs[nj:]
        scr, (send, recv, loc) = refs[:n_scr], refs[n_scr:]
        first = pl.program_id(0) == 0
        last = pl.program_id(0) == grid[0] - 1
        for d in range(1, len(grid)):
            first = jnp.logical_and(first, pl.program_id(d) == 0)
            last = jnp.logical_and(last, pl.program_id(d) == grid[d] - 1)

        @pl.when(first)
        def _():
            _start_jobs(jobs, jin, jout, send, recv, loc)
        body(*ins, *outs, *scr)

        @pl.when(last)
        def _():
            _finish_jobs(jobs, jin, jout, send, recv, loc)

    hbm = pl.BlockSpec(memory_space=pltpu.HBM)
    res = pl.pallas_call(
        wrapped, name=name, grid=grid, in_specs=in_specs + [hbm] * nj, out_specs=out_specs + [hbm] * nj,
        out_shape=out_shape + [job.out_shape for job in jobs], scratch_shapes=scratch_shapes + _job_scratch(nj),
        compiler_params=_cp(*sem))(*args, *[job.src for job in jobs])
    return list(res[:n_out]), list(res[n_out:])


def _comm_only(jobs, name):
    nj = len(jobs)

    def body(*refs):
        srcs, dsts, (send, recv, loc) = refs[:nj], refs[nj:2 * nj], refs[2 * nj:]
        _start_jobs(jobs, srcs, dsts, send, recv, loc)
        _finish_jobs(jobs, srcs, dsts, send, recv, loc)

    hbm = pl.BlockSpec(memory_space=pltpu.HBM)
    return list(pl.pallas_call(
        body, name=name, in_specs=[hbm] * nj, out_specs=[hbm] * nj, out_shape=[job.out_shape for job in jobs],
        scratch_shapes=_job_scratch(nj))(*[job.src for job in jobs]))


def _norm_matmul(x, nrm, w, out_dtype, name, jobs=()):
    S, Dm = x.shape
    N = w.shape[1]
    tm = _tile(S, 1024)
    tn = 1408 if N % 1408 == 0 else 1024

    def body(x_ref, nrm_ref, w_ref, o_ref, h_ref, hs):
        @pl.when(pl.program_id(1) == 0)
        def _():
            xv = x_ref[...]
            rstd = lax.rsqrt(jnp.mean(xv * xv, axis=-1, keepdims=True) + EPS)
            hn = (xv * rstd) * nrm_ref[0:1, :]
            hb = (hn * (1.0 + nrm_ref[2:3, :]) + nrm_ref[1:2, :]).astype(BF16)
            hs[...] = hb
            h_ref[...] = hb
        o_ref[...] = _dot(hs[...], w_ref[...]).astype(o_ref.dtype)

    return _pcall(
        body, name=name, grid=(S // tm, N // tn),
        in_specs=[pl.BlockSpec((tm, Dm), lambda i, n: (i, 0)),
                  pl.BlockSpec((8, Dm), lambda i, n: (0, 0)),
                  pl.BlockSpec((Dm, tn), lambda i, n: (0, n))],
        out_specs=[pl.BlockSpec((tm, tn), lambda i, n: (i, n)),
                   pl.BlockSpec((tm, Dm), lambda i, n: (i, 0))],
        out_shape=[jax.ShapeDtypeStruct((S, N), out_dtype), jax.ShapeDtypeStruct((S, Dm), BF16)],
        scratch_shapes=[pltpu.VMEM((tm, Dm), BF16)],
        sem=("arbitrary", "arbitrary"), args=(x, nrm, w), jobs=jobs)


def _proj_residual(a, w, x, nrm, coef, swiglu, name, jobs=()):
    S, Ka = a.shape
    K, Dm = w.shape
    tm = _tile(S, 256)

    def body(a_ref, w_ref, x_ref, nrm_ref, o_ref, y_ref):
        if swiglu:
            g = a_ref[:, :K].astype(F32)
            u = a_ref[:, K:].astype(F32)
            act = (g * _sigmoid(g) * u).astype(BF16)
        else:
            act = a_ref[...]
        y = _dot(act, w_ref[...])
        o_ref[...] = x_ref[...] + (coef * nrm_ref[3:4, :]) * y
        y_ref[...] = y.astype(BF16)

    return _pcall(
        body, name=name, grid=(S // tm,),
        in_specs=[pl.BlockSpec((tm, Ka), lambda i: (i, 0)),
                  pl.BlockSpec((K, Dm), lambda i: (0, 0)),
                  pl.BlockSpec((tm, Dm), lambda i: (i, 0)),
                  pl.BlockSpec((8, Dm), lambda i: (0, 0))],
        out_specs=[pl.BlockSpec((tm, Dm), lambda i: (i, 0)),
                   pl.BlockSpec((tm, Dm), lambda i: (i, 0))],
        out_shape=[jax.ShapeDtypeStruct((S, Dm), F32), jax.ShapeDtypeStruct((S, Dm), BF16)],
        sem=("arbitrary",), args=(a, w, x, nrm), jobs=jobs)


def _proj_residual_bwd(dxo, gu, y, w, nrm, coef, name):
    swiglu = gu is not None
    S, Dm = dxo.shape
    K = w.shape[0]
    Ka = 2 * K if swiglu else K
    tm = _tile(S, 256)

    def body(*refs):
        if swiglu:
            dxo_ref, y_ref, w_ref, nrm_ref, a_ref, da_ref, act_ref, dy_ref, dgate_ref = refs
        else:
            dxo_ref, y_ref, w_ref, nrm_ref, da_ref, dy_ref, dgate_ref = refs
        dxo_v = dxo_ref[...]
        dyb = ((coef * nrm_ref[3:4, :]) * dxo_v).astype(BF16)
        dy_ref[...] = dyb

        @pl.when(pl.program_id(0) == 0)
        def _():
            dgate_ref[...] = jnp.zeros_like(dgate_ref)
        dgate_ref[0:1, :] += jnp.sum(coef * y_ref[...].astype(F32) * dxo_v, axis=0, keepdims=True)

        dact = _dot_nt(dyb, w_ref[...])
        if swiglu:
            g = a_ref[:, :K].astype(F32)
            u = a_ref[:, K:].astype(F32)
            s = _sigmoid(g)
            si = g * s
            da_ref[:, :K] = (dact * u * (s * (1.0 + g * (1.0 - s)))).astype(BF16)
            da_ref[:, K:] = (dact * si).astype(BF16)
            act_ref[...] = (si * u).astype(BF16)
        else:
            da_ref[...] = dact

    row = lambda i: (i, 0)
    fix = lambda i: (0, 0)
    in_specs = [pl.BlockSpec((tm, Dm), row), pl.BlockSpec((tm, Dm), row),
                pl.BlockSpec((K, Dm), fix), pl.BlockSpec((8, Dm), fix)]
    args = [dxo, y, w, nrm]
    if swiglu:
        in_specs.append(pl.BlockSpec((tm, Ka), row))
        args.append(gu)
        out_specs = [pl.BlockSpec((tm, Ka), row), pl.BlockSpec((tm, K), row),
                     pl.BlockSpec((tm, Dm), row), pl.BlockSpec((8, Dm), fix)]
        out_shape = [jax.ShapeDtypeStruct((S, Ka), BF16), jax.ShapeDtypeStruct((S, K), BF16),
                     jax.ShapeDtypeStruct((S, Dm), BF16), jax.ShapeDtypeStruct((8, Dm), F32)]
    else:
        out_specs = [pl.BlockSpec((tm, Ka), row), pl.BlockSpec((tm, Dm), row), pl.BlockSpec((8, Dm), fix)]
        out_shape = [jax.ShapeDtypeStruct((S, Ka), F32), jax.ShapeDtypeStruct((S, Dm), BF16),
                     jax.ShapeDtypeStruct((8, Dm), F32)]
    return pl.pallas_call(
        body, name=name, grid=(S // tm,), in_specs=in_specs, out_specs=out_specs, out_shape=out_shape,
        compiler_params=_cp("arbitrary"),
    )(*args)


def _atb(a, b, out_dtype, bm, bn, name, jobs=()):
    S, M = a.shape
    N = b.shape[1]
    bk = _tile(S, 512)
    nk = S // bk

    def body(a_ref, b_ref, o_ref, acc):
        k = pl.program_id(2)

        @pl.when(k == 0)
        def _():
            acc[...] = jnp.zeros_like(acc)
        acc[...] += _dot_tn(a_ref[...], b_ref[...])

        @pl.when(k == nk - 1)
        def _():
            o_ref[...] = acc[...].astype(o_ref.dtype)

    (out,), extra = _pcall(
        body, name=name, grid=(M // bm, N // bn, nk),
        in_specs=[pl.BlockSpec((bk, bm), lambda m, n, k: (k, m)),
                  pl.BlockSpec((bk, bn), lambda m, n, k: (k, n))],
        out_specs=[pl.BlockSpec((bm, bn), lambda m, n, k: (m, n))],
        out_shape=[jax.ShapeDtypeStruct((M, N), out_dtype)],
        scratch_shapes=[pltpu.VMEM((bm, bn), F32)],
        sem=("arbitrary", "arbitrary", "arbitrary"), args=(a, b), jobs=jobs)
    return out, extra


def _nt_norm_bwd(dout, w, x, nrm, dxo, name, jobs=()):
    S, N = dout.shape
    Dm = w.shape[0]
    tm = _tile(S, 512)
    tk = 1408 if N % 1408 == 0 else 1024
    nk = N // tk

    def body(do_ref, w_ref, x_ref, nrm_ref, dxo_ref, dx_ref, red_ref, acc):
        i = pl.program_id(0)
        k = pl.program_id(1)

        @pl.when(k == 0)
        def _():
            acc[...] = jnp.zeros_like(acc)
        acc[...] += _dot_nt(do_ref[...], w_ref[...])

        @pl.when(k == nk - 1)
        def _():
            @pl.when(i == 0)
            def _():
                red_ref[...] = jnp.zeros_like(red_ref)
            dh = acc[...]
            xv = x_ref[...]
            rstd = lax.rsqrt(jnp.mean(xv * xv, axis=-1, keepdims=True) + EPS)
            xn = xv * rstd
            gain = nrm_ref[0:1, :]
            hn = xn * gain
            dhn = dh * (1.0 + nrm_ref[2:3, :])
            red_ref[0:1, :] += jnp.sum(dh, axis=0, keepdims=True)
            red_ref[1:2, :] += jnp.sum(dh * hn, axis=0, keepdims=True)
            red_ref[2:3, :] += jnp.sum(dhn * xn, axis=0, keepdims=True)
            dxn = dhn * gain
            dx = rstd * (dxn - xn * jnp.mean(dxn * xn, axis=-1, keepdims=True))
            dx_ref[...] = dxo_ref[...] + dx

    return _pcall(
        body, name=name, grid=(S // tm, nk),
        in_specs=[pl.BlockSpec((tm, tk), lambda i, k: (i, k)),
                  pl.BlockSpec((Dm, tk), lambda i, k: (0, k)),
                  pl.BlockSpec((tm, Dm), lambda i, k: (i, 0)),
                  pl.BlockSpec((8, Dm), lambda i, k: (0, 0)),
                  pl.BlockSpec((tm, Dm), lambda i, k: (i, 0))],
        out_specs=[pl.BlockSpec((tm, Dm), lambda i, k: (i, 0)),
                   pl.BlockSpec((8, Dm), lambda i, k: (0, 0))],
        out_shape=[jax.ShapeDtypeStruct((S, Dm), F32), jax.ShapeDtypeStruct((8, Dm), F32)],
        scratch_shapes=[pltpu.VMEM((tm, Dm), F32)],
        sem=("arbitrary", "arbitrary"), args=(dout, w, x, nrm, dxo), jobs=jobs)


def _final_loss(x, gf, tgt, name):
    S, Dm = x.shape
    tm = _tile(S, 512)

    def body(x_ref, g_ref, t_ref, dx_ref, st_ref):
        @pl.when(pl.program_id(0) == 0)
        def _():
            st_ref[...] = jnp.zeros_like(st_ref)
        xv = x_ref[...]
        rstd = lax.rsqrt(jnp.mean(xv * xv, axis=-1, keepdims=True) + EPS)
        xn = xv * rstd
        gain = g_ref[0:1, :]
        err = xn * gain - t_ref[...]
        st_ref[1:2, :] += jnp.full((1, Dm), 0.5 / Dm, F32) * jnp.sum(err * err)
        dy = err * (1.0 / Dm)
        st_ref[0:1, :] += jnp.sum(dy * xn, axis=0, keepdims=True)
        dxn = dy * gain
        dx_ref[...] = rstd * (dxn - xn * jnp.mean(dxn * xn, axis=-1, keepdims=True))

    return pl.pallas_call(
        body, name=name, grid=(S // tm,),
        in_specs=[pl.BlockSpec((tm, Dm), lambda i: (i, 0)),
                  pl.BlockSpec((8, Dm), lambda i: (0, 0)),
                  pl.BlockSpec((tm, Dm), lambda i: (i, 0))],
        out_specs=[pl.BlockSpec((tm, Dm), lambda i: (i, 0)),
                   pl.BlockSpec((8, Dm), lambda i: (0, 0))],
        out_shape=[jax.ShapeDtypeStruct((S, Dm), F32), jax.ShapeDtypeStruct((8, Dm), F32)],
        compiler_params=_cp("arbitrary"),
    )(x, gf, tgt)


def _alibi_slope(h):
    return float(2.0 ** (-8.0 * (h + 1) / N_Q_HEADS))


def _head_planes(pair_cols):
    lane = lax.broadcasted_iota(jnp.int32, pair_cols.shape, 1)
    low = lane < HEAD_DIM
    h0_lo = jnp.where(low, pair_cols, 0.0)
    h1_hi = jnp.where(low, 0.0, pair_cols)
    h0_hi = pltpu.roll(h0_lo, HEAD_DIM, 1)
    h1_lo = pltpu.roll(h1_hi, HEAD_DIM, 1)
    return ((h0_lo.astype(BF16), h0_hi.astype(BF16)), (h1_lo.astype(BF16), h1_hi.astype(BF16)))


def _to_plane(v, e, g):
    lane = lax.broadcasted_iota(jnp.int32, v.shape, 1)
    keep = (lane < HEAD_DIM) if e == 0 else (lane >= HEAD_DIM)
    v = jnp.where(keep, v, 0.0)
    return v if e == g else pltpu.roll(v, HEAD_DIM, 1)


def _band_mask(first_block):
    qi = lax.broadcasted_iota(jnp.int32, (BLOCK, 2 * BLOCK), 0)
    kj = lax.broadcasted_iota(jnp.int32, (BLOCK, 2 * BLOCK), 1)
    dist = qi + BLOCK - kj
    valid = (dist >= 0) & (dist < BLOCK) & (kj >= first_block * BLOCK)
    return dist.astype(F32), valid


def _softmax_band(qp, kx, h, dist, valid, sink):
    s = _dot_nt(qp, kx)
    s = jnp.where(valid, s - _alibi_slope(h) * dist, NEG_INF)
    m = jnp.maximum(jnp.max(s, axis=-1, keepdims=True), sink)
    p = jnp.exp(s - m)
    den = jnp.sum(p, axis=-1, keepdims=True) + jnp.exp(sink - m)
    return p / den, jnp.exp(sink - m) / den


def _past(cur, prev, s, row):
    return jnp.where(row < s, pltpu.roll(prev, s, 0), pltpu.roll(cur, s, 0))


def _future(cur, nxt, s, row):
    T = cur.shape[0]
    return jnp.where(row >= T - s, pltpu.roll(nxt, T - s, 0), pltpu.roll(cur, T - s, 0))


def _edge_row(v, last):
    T = v.shape[0]
    r8 = lax.broadcasted_iota(jnp.int32, (SUBLANES, v.shape[1]), 0)
    blk = v[T - SUBLANES:, :] if last else v[:SUBLANES, :]
    return jnp.sum(jnp.where(r8 == (SUBLANES - 1 if last else 0), blk, 0.0), axis=0, keepdims=True)


def _lru_gates(lx, lx_prev, small_ref, wa_ref, wx_ref, row, t0):
    xc = (small_ref[4:5, :] + small_ref[3:4, :] * lx + small_ref[2:3, :] * _past(lx, lx_prev, 1, row)
          + small_ref[1:2, :] * _past(lx, lx_prev, 2, row) + small_ref[0:1, :] * _past(lx, lx_prev, 3, row))
    xcb = xc.astype(BF16)
    r = _sigmoid(_dot(xcb, wa_ref[...]) + small_ref[5:6, :])
    ig = _sigmoid(_dot(xcb, wx_ref[...]) + small_ref[6:7, :])
    sp = _softplus_neg(small_ref[7:8, :])
    la = (-LRU_C) * r * sp
    a = jnp.exp(la)
    first = (row + t0) == 0
    mult = jnp.where(first, 1.0, jnp.sqrt(-_expm1(2.0 * la)))
    return xc, xcb, r, ig, sp, a, mult, first


def _mixer_fwd(proj, sinks, small, wa, wx, name, jobs=()):
    S = proj.shape[0]
    T = MIX_TILE
    nT = S // T
    nb = T // BLOCK

    def body(proj_ref, sink_ref, small_ref, wa_ref, wx_ref, y_ref, hp_ref, kvp, lxp, zp, hcar):
        i = pl.program_id(0)

        @pl.when(i == 0)
        def _():
            kvp[...] = jnp.zeros_like(kvp)
            lxp[...] = jnp.zeros_like(lxp)
            zp[...] = jnp.zeros_like(zp)
            hcar[...] = jnp.zeros_like(hcar)

        row = lax.broadcasted_iota(jnp.int32, (T, LRU_WIDTH), 0)

        kv = proj_ref[:, C_KV:C_KV + 2 * KV_WIDTH]
        ext = jnp.concatenate([kvp[...], kv], axis=0)
        kx = _head_planes(ext[:, :KV_WIDTH])
        vx = _head_planes(ext[:, KV_WIDTH:])
        first_tile = jnp.where(i == 0, 1, 0)
        for b in range(nb):
            dist, valid = _band_mask(first_tile if b == 0 else 0)
            keys = slice(b * BLOCK, (b + 2) * BLOCK)
            for pair in range(N_Q_HEADS // 2):
                g = pair // 2
                qp = (proj_ref[b * BLOCK:(b + 1) * BLOCK, pair * LANES:(pair + 1) * LANES] * 0.125).astype(BF16)
                o = jnp.zeros((BLOCK, LANES), F32)
                for e in range(2):
                    h = 2 * pair + e
                    pn, _ = _softmax_band(qp, kx[g][e][keys], h, dist, valid, sink_ref[h])
                    o = o + _dot(pn.astype(BF16), vx[g][e][keys])
                y_ref[b * BLOCK:(b + 1) * BLOCK, pair * LANES:(pair + 1) * LANES] = o.astype(BF16)
        kvp[...] = kv[T - BLOCK:, :]

        lx = proj_ref[:, C_LX:C_LX + LRU_WIDTH]
        xc, _, _, ig, _, a, mult, _ = _lru_gates(lx, lxp[...], small_ref, wa_ref, wx_ref, row, i * T)
        lxp[...] = lx
        aa = a
        bb = mult * (ig * xc)
        s = 1
        while s < T:
            a_sh = jnp.where(row >= s, pltpu.roll(aa, s, 0), 1.0)
            b_sh = jnp.where(row >= s, pltpu.roll(bb, s, 0), 0.0)
            bb = aa * b_sh + bb
            aa = aa * a_sh
            s *= 2
        hc = hcar[0:1, :]
        hh = bb + aa * hc
        hp_ref[...] = jnp.where(row < 1, hc, pltpu.roll(hh, 1, 0))
        hcar[...] = jnp.broadcast_to(_edge_row(hh, True), hcar.shape)
        gl, _ = _gelu(proj_ref[:, C_LG:C_LG + LRU_WIDTH])
        y_ref[:, ATTN_WIDTH:ATTN_WIDTH + LRU_WIDTH] = (gl * hh).astype(BF16)

        z = proj_ref[:, C_SC:C_SC + CONV_WIDTH] * proj_ref[:, C_SX:C_SX + CONV_WIDTH]
        c3 = (small_ref[10:11, :] * z + small_ref[9:10, :] * _past(z, zp[...], 1, row)
              + small_ref[8:9, :] * _past(z, zp[...], 2, row))
        zp[...] = z
        y_ref[:, ATTN_WIDTH + LRU_WIDTH:] = (proj_ref[:, C_SB:C_SB + CONV_WIDTH] * c3).astype(BF16)

    fix = lambda i: (0, 0)
    return _pcall(
        body, name=name, grid=(nT,),
        in_specs=[pl.BlockSpec((T, IN_PROJ_WIDTH), lambda i: (i, 0)),
                  pl.BlockSpec(memory_space=pltpu.SMEM),
                  pl.BlockSpec((16, LRU_WIDTH), fix),
                  pl.BlockSpec((LRU_WIDTH, LRU_WIDTH), fix),
                  pl.BlockSpec((LRU_WIDTH, LRU_WIDTH), fix)],
        out_specs=[pl.BlockSpec((T, D_MODEL), lambda i: (i, 0)),
                   pl.BlockSpec((T, LRU_WIDTH), lambda i: (i, 0))],
        out_shape=[jax.ShapeDtypeStruct((S, D_MODEL), BF16), jax.ShapeDtypeStruct((S, LRU_WIDTH), F32)],
        scratch_shapes=[pltpu.VMEM((BLOCK, 2 * KV_WIDTH), F32), pltpu.VMEM((T, LRU_WIDTH), F32),
                        pltpu.VMEM((T, CONV_WIDTH), F32), pltpu.VMEM((SUBLANES, LRU_WIDTH), F32)],
        sem=("arbitrary",), args=(proj, sinks, small, wa, wx), jobs=jobs)


def _mixer_bwd(proj, dymix, hprev, sinks, small, wa, wx, name):
    S = proj.shape[0]
    T = MIX_TILE
    nT = S // T
    nb = T // BLOCK
    bpt = T // BLOCK

    def body(proj_ref, kvprev_ref, lxprev_ref, scprev_ref, sxprev_ref, dy_ref, hp_ref, sink_ref, small_ref,
             wa_ref, wx_ref, dp_ref, dsm_ref, dsink_ref, dwa_ref, dwx_ref,
             dk_s, dv_s, dkv_c, dxc_n, dc3_n, p_c):
        i = pl.program_id(0)
        ti = nT - 1 - i
        has_prev = jnp.where(ti == 0, 0.0, 1.0)

        @pl.when(i == 0)
        def _():
            for r in (dkv_c, dxc_n, dc3_n, p_c, dsm_ref, dsink_ref, dwa_ref, dwx_ref):
                r[...] = jnp.zeros_like(r)

        row = lax.broadcasted_iota(jnp.int32, (T, LRU_WIDTH), 0)

        kv = proj_ref[:, C_KV:C_KV + 2 * KV_WIDTH]
        ext = jnp.concatenate([kvprev_ref[...] * has_prev, kv], axis=0)
        kx = _head_planes(ext[:, :KV_WIDTH])
        vx = _head_planes(ext[:, KV_WIDTH:])
        dk_s[...] = jnp.zeros_like(dk_s)
        dv_s[...] = jnp.zeros_like(dv_s)
        dk_s[T:, :] = dkv_c[:, :KV_WIDTH]
        dv_s[T:, :] = dkv_c[:, KV_WIDTH:]
        first_tile = jnp.where(ti == 0, 1, 0)
        for b in range(nb):
            dist, valid = _band_mask(first_tile if b == 0 else 0)
            keys = slice(b * BLOCK, (b + 2) * BLOCK)
            rows = slice(b * BLOCK, (b + 1) * BLOCK)
            for pair in range(N_Q_HEADS // 2):
                g = pair // 2
                cols = slice(pair * LANES, (pair + 1) * LANES)
                qp = (proj_ref[rows, cols] * 0.125).astype(BF16)
                dob = dy_ref[rows, cols].astype(BF16)
                dq = jnp.zeros((BLOCK, LANES), F32)
                for e in range(2):
                    h = 2 * pair + e
                    pn, psink = _softmax_band(qp, kx[g][e][keys], h, dist, valid, sink_ref[h])
                    dpm = _dot_nt(dob, vx[g][e][keys])
                    dsum = jnp.sum(pn * dpm, axis=-1, keepdims=True)
                    ds = (pn * (dpm - dsum)).astype(BF16)
                    dsink_ref[h:h + 1, :] += jnp.full((1, LANES), -1.0, F32) * jnp.sum(psink * dsum)
                    dv_s[keys, :] += _to_plane(_dot_tn(pn.astype(BF16), dob), e, g)
                    dk_s[keys, :] += _to_plane(_dot_tn(ds, qp), e, g)
                    dq = dq + _dot(ds, kx[g][e][keys])
                dp_ref[rows, cols] = (0.125 * dq).astype(BF16)
        dp_ref[:, C_KV:C_KV + KV_WIDTH] = dk_s[BLOCK:, :].astype(BF16)
        dp_ref[:, C_KV + KV_WIDTH:C_KV + 2 * KV_WIDTH] = dv_s[BLOCK:, :].astype(BF16)
        dkv_c[:, :KV_WIDTH] = dk_s[:BLOCK, :]
        dkv_c[:, KV_WIDTH:] = dv_s[:BLOCK, :]

        lx = proj_ref[:, C_LX:C_LX + LRU_WIDTH]
        lxprev = lxprev_ref[...] * has_prev
        xc, xcb, r, ig, sp, a, mult, first = _lru_gates(lx, lxprev, small_ref, wa_ref, wx_ref, row, ti * T)
        hp = hp_ref[...]
        hh = a * hp + mult * (ig * xc)
        lg = proj_ref[:, C_LG:C_LG + LRU_WIDTH]
        gl, th = _gelu(lg)
        dyl = dy_ref[:, ATTN_WIDTH:ATTN_WIDTH + LRU_WIDTH]
        dp_ref[:, C_LG:C_LG + LRU_WIDTH] = (dyl * hh * _gelu_grad(lg, th)).astype(BF16)
        aa = jnp.where(row < T - 1, pltpu.roll(a, T - 1, 0), 1.0)
        bb = dyl * gl
        s = 1
        while s < T:
            a_sh = jnp.where(row < T - s, pltpu.roll(aa, T - s, 0), 1.0)
            b_sh = jnp.where(row < T - s, pltpu.roll(bb, T - s, 0), 0.0)
            bb = bb + aa * b_sh
            aa = aa * a_sh
            s *= 2
        G = bb + aa * p_c[0:1, :]
        p_c[...] = jnp.broadcast_to(_edge_row(a * G, False), p_c.shape)
        da = G * hp
        dmult = G * (ig * xc)
        dig = G * mult * xc
        dxc = G * mult * ig
        dla = da * a + dmult * jnp.where(first, 0.0, -(a * a) / mult)
        dr = dla * ((-LRU_C) * sp)
        lam = small_ref[7:8, :]
        dsm_ref[7:8, :] += jnp.sum(dla * ((-LRU_C) * r), axis=0, keepdims=True) * (-_sigmoid(-lam))
        dpa = dr * r * (1.0 - r)
        dpx = dig * ig * (1.0 - ig)
        dsm_ref[5:6, :] += jnp.sum(dpa, axis=0, keepdims=True)
        dsm_ref[6:7, :] += jnp.sum(dpx, axis=0, keepdims=True)
        dpab = dpa.astype(BF16)
        dpxb = dpx.astype(BF16)
        dwa_ref[...] += _dot_tn(xcb, dpab)
        dwx_ref[...] += _dot_tn(xcb, dpxb)
        dxc = dxc + _dot_nt(dpab, wa_ref[...]) + _dot_nt(dpxb, wx_ref[...])
        dsm_ref[4:5, :] += jnp.sum(dxc, axis=0, keepdims=True)
        dsm_ref[3:4, :] += jnp.sum(dxc * lx, axis=0, keepdims=True)
        for k in range(3):
            dsm_ref[k:k + 1, :] += jnp.sum(dxc * _past(lx, lxprev, 3 - k, row), axis=0, keepdims=True)
        nxt = dxc_n[...]
        dlx = (small_ref[3:4, :] * dxc + small_ref[2:3, :] * _future(dxc, nxt, 1, row)
               + small_ref[1:2, :] * _future(dxc, nxt, 2, row) + small_ref[0:1, :] * _future(dxc, nxt, 3, row))
        dxc_n[...] = dxc
        dp_ref[:, C_LX:C_LX + LRU_WIDTH] = dlx.astype(BF16)

        sc = proj_ref[:, C_SC:C_SC + CONV_WIDTH]
        sx = proj_ref[:, C_SX:C_SX + CONV_WIDTH]
        sb = proj_ref[:, C_SB:C_SB + CONV_WIDTH]
        z = sc * sx
        zprev = (scprev_ref[...] * sxprev_ref[...]) * has_prev
        z1 = _past(z, zprev, 1, row)
        z2 = _past(z, zprev, 2, row)
        c3 = small_ref[10:11, :] * z + small_ref[9:10, :] * z1 + small_ref[8:9, :] * z2
        dys = dy_ref[:, ATTN_WIDTH + LRU_WIDTH:]
        dp_ref[:, C_SB:C_SB + CONV_WIDTH] = (dys * c3).astype(BF16)
        dc3 = dys * sb
        dsm_ref[10:11, :] += jnp.sum(dc3 * z, axis=0, keepdims=True)
        dsm_ref[9:10, :] += jnp.sum(dc3 * z1, axis=0, keepdims=True)
        dsm_ref[8:9, :] += jnp.sum(dc3 * z2, axis=0, keepdims=True)
        nxt3 = dc3_n[...]
        dz = (small_ref[10:11, :] * dc3 + small_ref[9:10, :] * _future(dc3, nxt3, 1, row)
              + small_ref[8:9, :] * _future(dc3, nxt3, 2, row))
        dc3_n[...] = dc3
        dp_ref[:, C_SC:C_SC + CONV_WIDTH] = (dz * sx).astype(BF16)
        dp_ref[:, C_SX:C_SX + CONV_WIDTH] = (dz * sc).astype(BF16)

    fix = lambda i: (0, 0)
    cur = lambda i: (nT - 1 - i, 0)
    prev_cols = lambda cb: (lambda i: (jnp.maximum(nT - 2 - i, 0), cb))
    return pl.pallas_call(
        body, name=name, grid=(nT,),
        in_specs=[pl.BlockSpec((T, IN_PROJ_WIDTH), cur),
                  pl.BlockSpec((BLOCK, 2 * KV_WIDTH),
                               lambda i: (jnp.maximum((nT - 1 - i) * bpt - 1, 0), C_KV // (2 * KV_WIDTH))),
                  pl.BlockSpec((T, LRU_WIDTH), prev_cols(C_LX // LRU_WIDTH)),
                  pl.BlockSpec((T, CONV_WIDTH), prev_cols(C_SC // CONV_WIDTH)),
                  pl.BlockSpec((T, CONV_WIDTH), prev_cols(C_SX // CONV_WIDTH)),
                  pl.BlockSpec((T, D_MODEL), cur),
                  pl.BlockSpec((T, LRU_WIDTH), cur),
                  pl.BlockSpec(memory_space=pltpu.SMEM),
                  pl.BlockSpec((16, LRU_WIDTH), fix),
                  pl.BlockSpec((LRU_WIDTH, LRU_WIDTH), fix),
                  pl.BlockSpec((LRU_WIDTH, LRU_WIDTH), fix)],
        out_specs=[pl.BlockSpec((T, IN_PROJ_WIDTH), cur),
                   pl.BlockSpec((16, LRU_WIDTH), fix),
                   pl.BlockSpec((SUBLANES, LANES), fix),
                   pl.BlockSpec((LRU_WIDTH, LRU_WIDTH), fix),
                   pl.BlockSpec((LRU_WIDTH, LRU_WIDTH), fix)],
        out_shape=[jax.ShapeDtypeStruct((S, IN_PROJ_WIDTH), BF16),
                   jax.ShapeDtypeStruct((16, LRU_WIDTH), F32),
                   jax.ShapeDtypeStruct((SUBLANES, LANES), F32),
                   jax.ShapeDtypeStruct((LRU_WIDTH, LRU_WIDTH), F32),
                   jax.ShapeDtypeStruct((LRU_WIDTH, LRU_WIDTH), F32)],
        scratch_shapes=[pltpu.VMEM((T + BLOCK, KV_WIDTH), F32), pltpu.VMEM((T + BLOCK, KV_WIDTH), F32),
                        pltpu.VMEM((BLOCK, 2 * KV_WIDTH), F32), pltpu.VMEM((T, LRU_WIDTH), F32),
                        pltpu.VMEM((T, CONV_WIDTH), F32), pltpu.VMEM((SUBLANES, LRU_WIDTH), F32)],
        compiler_params=_cp("arbitrary"),
    )(proj, proj, proj, proj, proj, dymix, hprev, sinks, small, wa, wx)


def _mod_matmul(c_all, w_mod, name):
    L, Dm, N = w_mod.shape
    R = c_all.shape[0]
    tn = 768

    def body(c_ref, w_ref, o_ref, ca_ref):
        cv = c_ref[...]
        ca = (cv * _sigmoid(cv)).astype(BF16)
        ca_ref[...] = ca
        o_ref[0] = _dot(ca, w_ref[0].astype(BF16))

    return pl.pallas_call(
        body, name=name, grid=(L, N // tn),
        in_specs=[pl.BlockSpec((R, Dm), lambda l, n: (0, 0)),
                  pl.BlockSpec((1, Dm, tn), lambda l, n: (l, 0, n))],
        out_specs=[pl.BlockSpec((1, R, tn), lambda l, n: (l, 0, n)), pl.BlockSpec((R, Dm), lambda l, n: (0, 0))],
        out_shape=[jax.ShapeDtypeStruct((L, R, N), F32), jax.ShapeDtypeStruct((R, Dm), BF16)],
        compiler_params=_cp("arbitrary", "arbitrary"),
    )(c_all, w_mod)


def _adamw(w, gs, m, v, name):
    R, C = w.shape
    tr = 8
    for cand in (512, 256, 128, 64, 32, 16, 8):
        if R % cand == 0 and cand * C * 4 <= (1 << 20):
            tr = cand
            break
    ng = len(gs)
    bc1 = 1.0 - ADAM_B1 ** ADAM_STEP
    bc2 = 1.0 - ADAM_B2 ** ADAM_STEP

    def body(*refs):
        w_ref = refs[0]
        g_refs = refs[1:1 + ng]
        m_ref, v_ref, go_ref, d_ref, mo_ref, vo_ref = refs[1 + ng:]
        g = g_refs[0][...].astype(F32)
        for gr in g_refs[1:]:
            g = g + gr[...].astype(F32)
        mn = ADAM_B1 * m_ref[...] + (1.0 - ADAM_B1) * g
        vn = ADAM_B2 * v_ref[...] + (1.0 - ADAM_B2) * (g * g)
        go_ref[...] = g
        mo_ref[...] = mn
        vo_ref[...] = vn
        d_ref[...] = (-ADAM_LR) * ((mn / bc1) / (jnp.sqrt(vn / bc2) + ADAM_EPS) + ADAM_WD * w_ref[...])

    spec = pl.BlockSpec((tr, C), lambda i: (i, 0))
    return pl.pallas_call(
        body, name=name, grid=(R // tr,),
        in_specs=[spec] * (3 + ng), out_specs=[spec] * 4,
        out_shape=[jax.ShapeDtypeStruct((R, C), F32)] * 4,
        compiler_params=_cp("arbitrary"),
    )(w, *gs, m, v)


def _sum4_layers(recvs, name):
    nl = len(recvs)
    _, R, C = recvs[0].shape
    tr = 8
    for cand in (512, 256, 128, 64, 32, 16):
        if R % cand == 0 and cand * C * 4 <= (1 << 20):
            tr = cand
            break
    ni = R // tr

    def body(*refs):
        r_refs, o_ref = refs[:nl], refs[nl]
        for l in range(nl):
            @pl.when(pl.program_id(0) == l)
            def _(r=r_refs[l]):
                o_ref[...] = ((r[0].astype(F32) + r[1].astype(F32)) + r[2].astype(F32)) + r[3].astype(F32)

    def spec(l):
        return pl.BlockSpec((4, tr, C), lambda ll, i: (0, jnp.where(ll == l, i, jnp.where(ll < l, 0, ni - 1)), 0))

    return pl.pallas_call(
        body, name=name, grid=(nl, ni),
        in_specs=[spec(l) for l in range(nl)],
        out_specs=pl.BlockSpec((tr, C), lambda ll, i: (ll * ni + i, 0)),
        out_shape=jax.ShapeDtypeStruct((nl * R, C), F32),
        compiler_params=_cp("arbitrary", "arbitrary"),
    )(*recvs)


def _all_gather_small(v, name):
    M, N = v.shape

    def body(x_ref, out_ref, sum_ref, send_sems, recv_sems, local_sem):
        x, y, c = lax.axis_index("x"), lax.axis_index("y"), lax.axis_index("c")
        me, sibling = (x, y, c), (x, y, 1 - c)
        chips = [(1 - x, y), (x, 1 - y), (1 - x, 1 - y)]

        def rows(px, py, pc):
            return out_ref.at[pl.ds(pl.multiple_of((4 * px + 2 * py + pc) * M, SUBLANES), M), :]

        def copy(k, block, to, src=None):
            return pltpu.make_async_remote_copy(
                src_ref=rows(*block) if src is None else src, dst_ref=rows(*block),
                send_sem=send_sems.at[k], recv_sem=recv_sems.at[k], device_id=to, device_id_type=MESH)

        mine = pltpu.make_async_copy(x_ref, rows(*me), local_sem)
        mine.start()
        first = [copy(0, me, sibling, src=x_ref)]
        first += [copy(1 + j, me, (*chip, c), src=x_ref) for j, chip in enumerate(chips)]
        for cp in first:
            cp.start()
        passed = [copy(4 + j, (*chip, c), sibling) for j, chip in enumerate(chips)]
        for j, chip in enumerate(chips):
            copy(1 + j, (*chip, c), me).wait_recv()
            passed[j].start()
        copy(0, sibling, me).wait_recv()
        for j, chip in enumerate(chips):
            copy(4 + j, (*chip, 1 - c), me).wait_recv()
        for cp in first + passed:
            cp.wait_send()
        mine.wait()
        acc = out_ref[0:M, :]
        for d in range(1, N_DEV):
            acc = acc + out_ref[d * M:(d + 1) * M, :]
        sum_ref[...] = acc

    return pl.pallas_call(
        body, name=name,
        out_shape=[jax.ShapeDtypeStruct((N_DEV * M, N), F32), jax.ShapeDtypeStruct((M, N), F32)],
        in_specs=[pl.BlockSpec(memory_space=pltpu.VMEM)],
        out_specs=[pl.BlockSpec(memory_space=pltpu.VMEM), pl.BlockSpec(memory_space=pltpu.VMEM)],
        scratch_shapes=[pltpu.SemaphoreType.DMA((7,)), pltpu.SemaphoreType.DMA((7,)), pltpu.SemaphoreType.DMA],
        compiler_params=pltpu.CompilerParams(vmem_limit_bytes=VMEM_LIMIT),
    )(v)


_BIG = (("w_ffn1_gu", 1), ("w_ffn1_down", 0), ("w_ffn2_gu", 1), ("w_ffn2_down", 0), ("w_in", 1), ("w_out", 0))
_AXIS = dict(_BIG)

_GATHER_PLAN = {
    "first": [(0, "w_ffn1_gu")],
    (0, "ffn1_gu"): [(0, "w_ffn1_down"), (0, "w_in"), (0, "w_out")],
    (0, "ffn1_down"): [(1, "w_ffn1_down")],
    (0, "mix_in"): [(0, "w_ffn2_down")],
    (0, "mix_core"): [(0, "w_ffn2_gu")],
    (0, "mix_out"): [(1, "w_in")],
    (0, "ffn2_gu"): [(1, "w_ffn1_gu")],
    (0, "ffn2_down"): [(1, "w_out")],
    (1, "ffn1_gu"): [(1, "w_ffn2_gu")],
    (1, "ffn1_down"): [(1, "w_ffn2_down")],
}


def _swap_with_sibling(parts, name):
    n = len(parts)

    def body(*refs):
        ins, outs = refs[:n], refs[n:2 * n]
        send_sems, recv_sems = refs[2 * n:]
        x, y, c = lax.axis_index("x"), lax.axis_index("y"), lax.axis_index("c")
        copies = [pltpu.make_async_remote_copy(
            src_ref=ins[t], dst_ref=outs[t], send_sem=send_sems.at[t], recv_sem=recv_sems.at[t],
            device_id=(x, y, 1 - c), device_id_type=MESH) for t in range(n)]
        for cp in copies:
            cp.start()
        for cp in copies:
            cp.wait_recv()
        for cp in copies:
            cp.wait_send()

    hbm = pl.BlockSpec(memory_space=pltpu.HBM)
    return pl.pallas_call(
        body, name=name, out_shape=[jax.ShapeDtypeStruct(p.shape, p.dtype) for p in parts],
        in_specs=[hbm] * n, out_specs=[hbm] * n,
        scratch_shapes=[pltpu.SemaphoreType.DMA((n,)), pltpu.SemaphoreType.DMA((n,))],
    )(*parts)


def _pack(arrs, rows_multiple=SUBLANES):
    flat = jnp.concatenate([a.astype(F32).reshape(-1) for a in arrs])
    unit = rows_multiple * LANES
    total = -(-flat.shape[0] // unit) * unit
    return jnp.pad(flat, (0, total - flat.shape[0])).reshape(total // LANES, LANES)


def _unpack(flat, shapes):
    out, off = [], 0
    for shp in shapes:
        n = int(math.prod(shp))
        out.append(flat[off:off + n].reshape(shp))
        off += n
    return out


def _block_diag(w):
    out = jnp.zeros((LRU_WIDTH, LRU_WIDTH), F32)
    for h in range(4):
        out = lax.dynamic_update_slice(out, w[h], (h * HEAD_DIM, h * HEAD_DIM))
    return out


def _diag_blocks(w):
    return jnp.stack([w[h * HEAD_DIM:(h + 1) * HEAD_DIM, h * HEAD_DIM:(h + 1) * HEAD_DIM] for h in range(4)])


def _rows8(*rows):
    z = jnp.zeros((8 - len(rows), rows[0].shape[-1]), F32)
    return jnp.concatenate([jnp.stack(rows), z], axis=0)


def kernel(x, c, w_mod, b_mod, g_norm, w_ffn1_gu, w_ffn1_down, w_ffn2_gu, w_ffn2_down, w_in, w_out, attn_sinks, lru_conv_w, lru_conv_b, lru_gate_a_w, lru_gate_a_b, lru_gate_x_w, lru_gate_x_b, lru_lambda, sc_conv_w, g_final, loss_target, m_w_mod, m_b_mod, m_g_norm, m_w_ffn1_gu, m_w_ffn1_down, m_w_ffn2_gu, m_w_ffn2_down, m_w_in, m_w_out, m_attn_sinks, m_lru_conv_w, m_lru_conv_b, m_lru_gate_a_w, m_lru_gate_a_b, m_lru_gate_x_w, m_lru_gate_x_b, m_lru_lambda, m_sc_conv_w, m_g_final, v_w_mod, v_b_mod, v_g_norm, v_w_ffn1_gu, v_w_ffn1_down, v_w_ffn2_gu, v_w_ffn2_down, v_w_in, v_w_out, v_attn_sinks, v_lru_conv_w, v_lru_conv_b, v_lru_gate_a_w, v_lru_gate_a_b, v_lru_gate_x_w, v_lru_gate_x_b, v_lru_lambda, v_sc_conv_w, v_g_final):
    W = dict(w_mod=w_mod, b_mod=b_mod, g_norm=g_norm, w_ffn1_gu=w_ffn1_gu, w_ffn1_down=w_ffn1_down,
             w_ffn2_gu=w_ffn2_gu, w_ffn2_down=w_ffn2_down, w_in=w_in, w_out=w_out, attn_sinks=attn_sinks,
             lru_conv_w=lru_conv_w, lru_conv_b=lru_conv_b, lru_gate_a_w=lru_gate_a_w, lru_gate_a_b=lru_gate_a_b,
             lru_gate_x_w=lru_gate_x_w, lru_gate_x_b=lru_gate_x_b, lru_lambda=lru_lambda, sc_conv_w=sc_conv_w,
             g_final=g_final)
    M1 = dict(w_mod=m_w_mod, b_mod=m_b_mod, g_norm=m_g_norm, w_ffn1_gu=m_w_ffn1_gu, w_ffn1_down=m_w_ffn1_down,
              w_ffn2_gu=m_w_ffn2_gu, w_ffn2_down=m_w_ffn2_down, w_in=m_w_in, w_out=m_w_out,
              attn_sinks=m_attn_sinks, lru_conv_w=m_lru_conv_w, lru_conv_b=m_lru_conv_b,
              lru_gate_a_w=m_lru_gate_a_w, lru_gate_a_b=m_lru_gate_a_b, lru_gate_x_w=m_lru_gate_x_w,
              lru_gate_x_b=m_lru_gate_x_b, lru_lambda=m_lru_lambda, sc_conv_w=m_sc_conv_w, g_final=m_g_final)
    V1 = dict(w_mod=v_w_mod, b_mod=v_b_mod, g_norm=v_g_norm, w_ffn1_gu=v_w_ffn1_gu, w_ffn1_down=v_w_ffn1_down,
              w_ffn2_gu=v_w_ffn2_gu, w_ffn2_down=v_w_ffn2_down, w_in=v_w_in, w_out=v_w_out,
              attn_sinks=v_attn_sinks, lru_conv_w=v_lru_conv_w, lru_conv_b=v_lru_conv_b,
              lru_gate_a_w=v_lru_gate_a_w, lru_gate_a_b=v_lru_gate_a_b, lru_gate_x_w=v_lru_gate_x_w,
              lru_gate_x_b=v_lru_gate_x_b, lru_lambda=v_lru_lambda, sc_conv_w=v_sc_conv_w, g_final=v_g_final)
    names = ["w_mod", "b_mod", "g_norm", "w_ffn1_gu", "w_ffn1_down", "w_ffn2_gu", "w_ffn2_down", "w_in", "w_out",
             "attn_sinks", "lru_conv_w", "lru_conv_b", "lru_gate_a_w", "lru_gate_a_b", "lru_gate_x_w",
             "lru_gate_x_b", "lru_lambda", "sc_conv_w", "g_final"]

    xs = x[0]
    tgt = loss_target[0]
    S = xs.shape[0]
    chip = 2 * lax.axis_index("x") + lax.axis_index("y")
    batch = 2 * chip + lax.axis_index("c")
    L = DEPTH

    fwd_shapes = [(D_MODEL,), g_norm.shape, lru_conv_w.shape, sc_conv_w.shape]
    gathered, _ = _all_gather_small(_pack([c[0], g_norm, lru_conv_w, sc_conv_w]), "gather_small_fwd")
    gathered = gathered.reshape(N_DEV, -1)
    c_all = gathered[:, :D_MODEL]
    per_chip = [_unpack(gathered[2 * jj], fwd_shapes) for jj in range(N_CHIPS)]
    g_norm_full = jnp.concatenate([p[1] for p in per_chip], axis=-1)
    lru_conv_w_full = jnp.concatenate([p[2] for p in per_chip], axis=-1)
    sc_conv_w_full = jnp.concatenate([p[3] for p in per_chip], axis=-1)

    c_pad = jnp.concatenate([c_all, jnp.zeros_like(c_all)], axis=0)
    mod_part, c_act = _mod_matmul(c_pad, w_mod, "mod_matmul")
    mod_all, _ = _all_gather_small(mod_part.reshape(-1, LANES), "gather_mod")
    mod_all = mod_all.reshape(N_DEV, L, 16, -1)
    mod_rows = [lax.dynamic_index_in_dim(mod_all[2 * jj], batch, axis=1, keepdims=False) for jj in range(N_CHIPS)]
    mod = (jnp.concatenate(mod_rows, axis=-1) + b_mod).reshape(L, 9, D_MODEL)

    def nrm_rows(l, s):
        return _rows8(g_norm_full[l, s], mod[l, 3 * s], mod[l, 3 * s + 1], mod[l, 3 * s + 2])

    full = {}

    def gather_jobs(key):
        return [_GatherJob(W[n][l].astype(BF16), _AXIS[n]) for l, n in _GATHER_PLAN.get(key, ())]

    def landed(key, outs):
        full.update(zip(_GATHER_PLAN.get(key, ()), outs))

    landed("first", _comm_only(gather_jobs("first"), "gather_first"))

    def mixer_params(l):
        small = jnp.concatenate([lru_conv_w_full[l], lru_conv_b[l][None], lru_gate_a_b[l][None],
                                 lru_gate_x_b[l][None], lru_lambda[l][None], sc_conv_w_full[l],
                                 jnp.zeros((5, LRU_WIDTH), F32)], axis=0)
        return (attn_sinks[l], small, _block_diag(lru_gate_a_w[l]).astype(BF16),
                _block_diag(lru_gate_x_w[l]).astype(BF16))

    saved = []
    xcur = xs
    for l in range(L):
        n1, n2, n3 = nrm_rows(l, 0), nrm_rows(l, 1), nrm_rows(l, 2)

        def ffn(which, xin, nrm):
            key = (l, which + "_gu")
            (gu, h), ex = _norm_matmul(xin, nrm, full[(l, f"w_{which}_gu")], BF16, f"l{l}_{which}_gu", gather_jobs(key))
            landed(key, ex)
            key = (l, which + "_down")
            (xo, y), ex = _proj_residual(gu, full[(l, f"w_{which}_down")], xin, nrm, 0.5, True, f"l{l}_{which}_down",
                                         gather_jobs(key))
            landed(key, ex)
            return xo, (xin, h, gu, y)

        x1, s1 = ffn("ffn1", xcur, n1)
        (proj, h2), ex = _norm_matmul(x1, n2, full[(l, "w_in")], F32, f"l{l}_mix_in", gather_jobs((l, "mix_in")))
        landed((l, "mix_in"), ex)
        mp = mixer_params(l)
        (ymix, hprev), ex = _mixer_fwd(proj, *mp, f"l{l}_mix_core", gather_jobs((l, "mix_core")))
        landed((l, "mix_core"), ex)
        (x2, ymo), ex = _proj_residual(ymix, full[(l, "w_out")], x1, n2, 1.0, False, f"l{l}_mix_out",
                                       gather_jobs((l, "mix_out")))
        landed((l, "mix_out"), ex)
        s2 = (x1, h2, proj, ymix, ymo, hprev, mp)
        xcur, s3 = ffn("ffn2", x2, n3)
        saved.append((n1, n2, n3, s1, s2, s3))

    dx, stats = _final_loss(xcur, _rows8(g_final), tgt, "final_loss")
    loss = lax.psum(stats[1, 0], ("x", "y", "c"))
    d_g_final = stats[0]

    recv = {}
    dmod, d_gnorm, d_small = [None] * L, [None] * L, [None] * L
    for l in reversed(range(L)):
        n1, n2, n3, s1, s2, s3 = saved[l]

        def ffn_bwd(which, dxo, sv, nrm):
            xin, h, gu, y = sv
            tag = f"l{l}_{which}"
            dgu, act, dy, dgate = _proj_residual_bwd(dxo, gu, y, full[(l, f"w_{which}_down")], nrm, 0.5,
                                                     tag + "_down_bwd")
            dw_down, _ = _atb(act, dy, BF16, 1408, 1024, tag + "_dw_down")
            dw_gu, (r_down,) = _atb(h, dgu, BF16, 1024, 1408, tag + "_dw_gu", [_ScatterJob(dw_down, 0)])
            (dxi, red), (r_gu,) = _nt_norm_bwd(dgu, full[(l, f"w_{which}_gu")], xin, nrm, dxo, tag + "_gu_bwd",
                                               [_ScatterJob(dw_gu, 1)])
            recv[(l, f"w_{which}_down")] = r_down
            recv[(l, f"w_{which}_gu")] = r_gu
            return dxi, (red[0], red[1], dgate[0]), red[2]

        dx, dm3, dg3 = ffn_bwd("ffn2", dx, s3, n3)
        x_in, h2, proj, ymix, ymo, hprev, mp = s2
        dymix, dy, dgate = _proj_residual_bwd(dx, None, ymo, full[(l, "w_out")], n2, 1.0, f"l{l}_mix_out_bwd")
        dw_out, _ = _atb(ymix, dy, BF16, 1024, 1024, f"l{l}_dw_out")
        dproj, dsm, dsink, dwa, dwx = _mixer_bwd(proj, dymix, hprev, *mp, f"l{l}_mix_core_bwd")
        dw_in, (recv[(l, "w_out")],) = _atb(h2, dproj, BF16, 1024, 1024, f"l{l}_dw_in", [_ScatterJob(dw_out, 0)])
        (dx, red), (recv[(l, "w_in")],) = _nt_norm_bwd(dproj, full[(l, "w_in")], x_in, n2, dx, f"l{l}_mix_in_bwd",
                                                       [_ScatterJob(dw_in, 1)])
        dm2, dg2 = (red[0], red[1], dgate[0]), red[2]
        dx, dm1, dg1 = ffn_bwd("ffn1", dx, s1, n1)
        dmod[l] = jnp.stack(list(dm1) + list(dm2) + list(dm3))
        d_gnorm[l] = jnp.stack([dg1, dg2, dg3])
        d_small[l] = (dsink[:, 0], dsm[0:4], dsm[4], _diag_blocks(dwa), dsm[5], _diag_blocks(dwx), dsm[6],
                      dsm[7], dsm[8:11])
    grad_x = dx[None]

    def both(k):
        return jnp.stack([d_small[0][k], d_small[1][k]])
    small_names = ["g_norm", "attn_sinks", "lru_conv_w", "lru_conv_b", "lru_gate_a_w", "lru_gate_a_b",
                   "lru_gate_x_w", "lru_gate_x_b", "lru_lambda", "sc_conv_w", "g_final"]
    small_parts = [jnp.stack(d_gnorm)] + [both(k) for k in range(9)] + [d_g_final]
    dmod_flat = jnp.stack(dmod).reshape(-1)
    bwd_gathered, bwd_sum = _all_gather_small(_pack([dmod_flat] + small_parts), "gather_small_bwd")
    n_mod = dmod_flat.shape[0]
    dmod_all = bwd_gathered.reshape(N_DEV, -1)[:, :n_mod].reshape(N_DEV, L, 9 * D_MODEL)
    bwd_sum = bwd_sum.reshape(-1)
    G = {"b_mod": bwd_sum[:n_mod].reshape(L, 9 * D_MODEL)}
    G.update(zip(small_names, _unpack(bwd_sum[n_mod:], [p.shape for p in small_parts])))
    for n in ("g_norm", "lru_conv_w", "sc_conv_w"):
        wdt = W[n].shape[-1]
        G[n] = lax.dynamic_slice_in_dim(G[n], chip * wdt, wdt, axis=G[n].ndim - 1)

    ncol = w_mod.shape[-1]
    dmod_cols = lax.dynamic_slice_in_dim(dmod_all, chip * ncol, ncol, axis=2)
    zeros8 = jnp.zeros((N_DEV, ncol), F32)
    g_w_mod = jnp.stack([_atb(c_act, jnp.concatenate([dmod_cols[:, l], zeros8], axis=0).astype(BF16), F32,
                              D_MODEL, 768, f"l{l}_dw_mod")[0] for l in range(L)])

    part = [_sum4_layers([recv[(l, n)] for l in range(L)], f"sum_{n}") for n, _ in _BIG]
    other = _swap_with_sibling(part, "swap_sibling")

    out_g, out_d, out_m, out_v = {}, {}, {}, {}
    for (n, _), p, o in zip(_BIG, part, other):
        shp = W[n].shape
        res = _adamw(W[n].reshape(p.shape), [p, o], M1[n].reshape(p.shape), V1[n].reshape(p.shape), f"adamw_{n}")
        out_g[n], out_d[n], out_m[n], out_v[n] = [r.reshape(shp) for r in res]
    res = _adamw(w_mod.reshape(-1, ncol), [g_w_mod.reshape(-1, ncol)], m_w_mod.reshape(-1, ncol),
                 v_w_mod.reshape(-1, ncol), "adamw_w_mod")
    out_g["w_mod"], out_d["w_mod"], out_m["w_mod"], out_v["w_mod"] = [r.reshape(w_mod.shape) for r in res]
    rest = ["b_mod"] + small_names
    shapes = [W[n].shape for n in rest]
    res = _adamw(_pack([W[n] for n in rest]), [_pack([G[n] for n in rest])], _pack([M1[n] for n in rest]),
                 _pack([V1[n] for n in rest]), "adamw_small")
    for dst, r in zip((out_g, out_d, out_m, out_v), res):
        dst.update(zip(rest, _unpack(r.reshape(-1), shapes)))

    return (loss, grad_x, *[out_g[n] for n in names], *[out_d[n] for n in names],
            *[out_m[n] for n in names], *[out_v[n] for n in names])
```

```python
import math

import jax
import jax.numpy as jnp
from jax import lax
from jax.experimental import pallas as pl
from jax.experimental.pallas import tpu as pltpu

F32 = jnp.float32
BF16 = jnp.bfloat16

D_MODEL = 1024
DEPTH = 2
HEAD_DIM = 64
N_Q_HEADS = 8
ATTN_WIDTH = 512
KV_WIDTH = 128
LRU_WIDTH = 256
CONV_WIDTH = 256
IN_PROJ_WIDTH = 2048
BLOCK = 128
D_FF = 2816
EPS = 1e-6
NEG_INF = -1e30
LRU_C = 8.0
N_CHIPS = 4
N_DEV = 8

C_Q, C_KV, C_LX, C_LG, C_SB, C_SC, C_SX = 0, 512, 768, 1024, 1280, 1536, 1792

ADAM_LR = 0.001
ADAM_B1 = 0.9
ADAM_B2 = 0.999
ADAM_EPS = 1e-08
ADAM_WD = 0.01
ADAM_STEP = 10

LANES = 128
SUBLANES = 8
VMEM_LIMIT = 56 * 1024 * 1024
MIX_TILE = 256

MESH = pl.DeviceIdType.MESH


def _cp(*sem):
    return pltpu.CompilerParams(dimension_semantics=sem, vmem_limit_bytes=VMEM_LIMIT)


def _tile(n, pref):
    t = min(n, pref)
    while n % t:
        t //= 2
    return t


MXU_DIM = 256


def _col_chunk(n):
    return max(c for c in range(MXU_DIM, 2816 + 1, MXU_DIM) if n % c == 0)


def _resident(shape):
    return pl.BlockSpec(shape, lambda *_: (0, 0), pipeline_mode=pl.Buffered(1))


def _sigmoid(v):
    return 1.0 / (1.0 + jnp.exp(-v))


def _expm1(v):
    series = v * (1.0 + v * (0.5 + v * (1.0 / 6.0 + v * (1.0 / 24.0 + v * (1.0 / 120.0)))))
    return jnp.where(v > -0.1, series, jnp.exp(v) - 1.0)


def _softplus_neg(lam):
    e = jnp.exp(-jnp.abs(lam))
    log1p = jnp.where(e < 1e-2, e * (1.0 - e * (0.5 - e * (1.0 / 3.0))), jnp.log(1.0 + e))
    return jnp.maximum(-lam, 0.0) + log1p


_GELU_K = math.sqrt(2.0 / math.pi)
_GELU_C = 0.044715


def _gelu(v):
    t = jnp.tanh(_GELU_K * (v + _GELU_C * v * v * v))
    return 0.5 * v * (1.0 + t), t


def _gelu_grad(v, t):
    return 0.5 * (1.0 + t) + 0.5 * v * (1.0 - t * t) * _GELU_K * (1.0 + 3.0 * _GELU_C * v * v)


def _dot(a, b):
    return jnp.dot(a, b, preferred_element_type=F32)


def _dot_nt(a, b):
    return lax.dot_general(a, b, (((1,), (1,)), ((), ())), preferred_element_type=F32)


def _dot_tn(a, b):
    return lax.dot_general(a, b, (((0,), (0,)), ((), ())), preferred_element_type=F32)


def _window(ref, axis, j, width):
    start = pl.multiple_of(j * width, LANES if axis == 1 else 16)
    if axis == 1:
        return ref.at[:, pl.ds(start, width)]
    return ref.at[pl.ds(start, width), :]


def _chip_peers():
    x, y, c = lax.axis_index("x"), lax.axis_index("y"), lax.axis_index("c")
    return x, y, c, [(1 - x, y), (x, 1 - y), (1 - x, 1 - y)]


class _GatherJob:
    def __init__(self, shard, axis):
        self.src, self.axis, self.width, self.half = shard, axis, shard.shape[axis], shard.shape[0] // 2
        full = tuple(d * N_CHIPS if k == axis else d for k, d in enumerate(shard.shape))
        self.out_shape = jax.ShapeDtypeStruct(full, shard.dtype)

    def _piece(self, ref, j, hf):
        if self.axis == 1:
            return ref.at[pl.ds(pl.multiple_of(hf * self.half, 16), self.half),
                          pl.ds(pl.multiple_of(j * self.width, LANES), self.width)]
        return ref.at[pl.ds(pl.multiple_of(j * self.width + hf * self.half, 16), self.half), :]

    def _copies(self, src, dst, send, recv, loc, t):
        x, y, c, chips = _chip_peers()
        j = 2 * x + y
        owners = [2 * px + py for px, py in chips]
        local = pltpu.make_async_copy(src, _window(dst, self.axis, j, self.width), loc.at[t])
        mine = src.at[pl.ds(pl.multiple_of(c * self.half, 16), self.half), :]

        def ici(k, owner):
            return pltpu.make_async_remote_copy(
                src_ref=mine, dst_ref=self._piece(dst, owner, c), send_sem=send.at[6 * t + k],
                recv_sem=recv.at[6 * t + k], device_id=(*chips[k], c), device_id_type=MESH)

        def relay(k, hf):
            piece = self._piece(dst, owners[k], hf)
            return pltpu.make_async_remote_copy(
                src_ref=piece, dst_ref=piece, send_sem=send.at[6 * t + 3 + k], recv_sem=recv.at[6 * t + 3 + k],
                device_id=(x, y, 1 - c), device_id_type=MESH)

        return (local, [ici(k, j) for k in range(3)], [ici(k, owners[k]) for k in range(3)],
                [relay(k, c) for k in range(3)], [relay(k, 1 - c) for k in range(3)])

    def start(self, *a):
        local, ici_out, _, _, _ = self._copies(*a)
        local.start()
        for cp in ici_out:
            cp.start()

    def relay(self, *a):
        _, _, ici_in, relay_out, _ = self._copies(*a)
        for arrived, onward in zip(ici_in, relay_out):
            arrived.wait_recv()
            onward.start()

    def finish(self, *a):
        local, ici_out, _, relay_out, relay_in = self._copies(*a)
        for cp in relay_in:
            cp.wait_recv()
        for cp in ici_out + relay_out:
            cp.wait_send()
        local.wait()


class _ScatterJob:
    def __init__(self, full, axis):
        self.src, self.axis, self.width = full, axis, full.shape[axis] // N_CHIPS
        shard = tuple(self.width if k == axis else d for k, d in enumerate(full.shape))
        self.out_shape = jax.ShapeDtypeStruct((4,) + shard, full.dtype)

    def _copies(self, src, dst, send, recv, loc, t):
        x, y, c, chips = _chip_peers()
        local = pltpu.make_async_copy(_window(src, self.axis, 2 * x + y, self.width), dst.at[3], loc.at[t])
        sends = [pltpu.make_async_remote_copy(
            src_ref=_window(src, self.axis, 2 * px + py, self.width), dst_ref=dst.at[k],
            send_sem=send.at[6 * t + k], recv_sem=recv.at[6 * t + k], device_id=(px, py, c), device_id_type=MESH)
            for k, (px, py) in enumerate(chips)]
        return local, sends

    def start(self, *a):
        local, sends = self._copies(*a)
        local.start()
        for cp in sends:
            cp.start()

    def relay(self, *a):
        pass

    def finish(self, *a):
        local, sends = self._copies(*a)
        for cp in sends:
            cp.wait_recv()
        for cp in sends:
            cp.wait_send()
        local.wait()


def _run_jobs(phase, jobs, srcs, dsts, sems):
    for t, job in enumerate(jobs):
        getattr(job, phase)(srcs[t], dsts[t], *sems, t)


def _job_scratch(n):
    return [pltpu.SemaphoreType.DMA((6 * n,)), pltpu.SemaphoreType.DMA((6 * n,)), pltpu.SemaphoreType.DMA((n,))]


def _pcall(body, *, name, grid, in_specs, out_specs, out_shape, sem, args, scratch_shapes=(), jobs=()):
    in_specs, out_specs, out_shape = list(in_specs), list(out_specs), list(out_shape)
    scratch_shapes = list(scratch_shapes)
    if not jobs:
        res = pl.pallas_call(body, name=name, grid=grid, in_specs=in_specs, out_specs=out_specs, out_shape=out_shape,
                             scratch_shapes=scratch_shapes, compiler_params=_cp(*sem))(*args)
        return list(res), []
    n_in, n_out, n_scr, nj = len(args), len(out_shape), len(scratch_shapes), len(jobs)
    relay_step = (3 * grid[0]) // 4
    relay_early = 0 < relay_step < grid[0] - 1

    def wrapped(*refs):
        ins, refs = refs[:n_in], refs[n_in:]
        jin, refs = refs[:nj], refs[nj:]
        outs, refs = refs[:n_out], refs[n_out:]
        jout, refs = refs[:nj], refs[nj:]
        scr, sems = refs[:n_scr], refs[n_scr:]
        inner_first = inner_last = True
        for d in range(1, len(grid)):
            inner_first = jnp.logical_and(inner_first, pl.program_id(d) == 0)
            inner_last = jnp.logical_and(inner_last, pl.program_id(d) == grid[d] - 1)

        @pl.when(jnp.logical_and(pl.program_id(0) == 0, inner_first))
        def _():
            _run_jobs("start", jobs, jin, jout, sems)

        if relay_early:
            @pl.when(jnp.logical_and(pl.program_id(0) == relay_step, inner_first))
            def _():
                _run_jobs("relay", jobs, jin, jout, sems)
        body(*ins, *outs, *scr)

        @pl.when(jnp.logical_and(pl.program_id(0) == grid[0] - 1, inner_last))
        def _():
            if not relay_early:
                _run_jobs("relay", jobs, jin, jout, sems)
            _run_jobs("finish", jobs, jin, jout, sems)

    hbm = pl.BlockSpec(memory_space=pltpu.HBM)
    res = pl.pallas_call(
        wrapped, name=name, grid=grid, in_specs=in_specs + [hbm] * nj, out_specs=out_specs + [hbm] * nj,
        out_shape=out_shape + [job.out_shape for job in jobs], scratch_shapes=scratch_shapes + _job_scratch(nj),
        compiler_params=_cp(*sem))(*args, *[job.src for job in jobs])
    return list(res[:n_out]), list(res[n_out:])


def _comm_only(jobs, name):
    nj = len(jobs)

    def body(*refs):
        srcs, dsts, sems = refs[:nj], refs[nj:2 * nj], refs[2 * nj:]
        for phase in ("start", "relay", "finish"):
            _run_jobs(phase, jobs, srcs, dsts, sems)

    hbm = pl.BlockSpec(memory_space=pltpu.HBM)
    return list(pl.pallas_call(
        body, name=name, in_specs=[hbm] * nj, out_specs=[hbm] * nj, out_shape=[job.out_shape for job in jobs],
        scratch_shapes=_job_scratch(nj))(*[job.src for job in jobs]))


def _norm_matmul(x, nrm, w, out_dtype, name, jobs=()):
    S, Dm = x.shape
    N = w.shape[1]
    tm = _tile(S, 512)
    cw = _col_chunk(N)

    def body(x_ref, nrm_ref, w_ref, o_ref, h_ref):
        xv = x_ref[...]
        rstd = lax.rsqrt(jnp.mean(xv * xv, axis=-1, keepdims=True) + EPS)
        hn = (xv * rstd) * nrm_ref[0:1, :]
        hb = (hn * (1.0 + nrm_ref[2:3, :]) + nrm_ref[1:2, :]).astype(BF16)
        h_ref[...] = hb
        for n in range(N // cw):
            o_ref[:, n * cw:(n + 1) * cw] = _dot(hb, w_ref[:, n * cw:(n + 1) * cw]).astype(o_ref.dtype)

    return _pcall(
        body, name=name, grid=(S // tm,),
        in_specs=[pl.BlockSpec((tm, Dm), lambda i: (i, 0)),
                  pl.BlockSpec((8, Dm), lambda i: (0, 0)),
                  _resident((Dm, N))],
        out_specs=[pl.BlockSpec((tm, N), lambda i: (i, 0)),
                   pl.BlockSpec((tm, Dm), lambda i: (i, 0))],
        out_shape=[jax.ShapeDtypeStruct((S, N), out_dtype), jax.ShapeDtypeStruct((S, Dm), BF16)],
        sem=("arbitrary",), args=(x, nrm, w), jobs=jobs)


def _proj_residual(a, w, x, nrm, coef, swiglu, name, jobs=()):
    S, Ka = a.shape
    K, Dm = w.shape
    tm = _tile(S, 256)

    def body(a_ref, w_ref, x_ref, nrm_ref, o_ref, y_ref):
        if swiglu:
            g = a_ref[:, :K].astype(F32)
            u = a_ref[:, K:].astype(F32)
            act = (g * _sigmoid(g) * u).astype(BF16)
        else:
            act = a_ref[...]
        y = _dot(act, w_ref[...])
        o_ref[...] = x_ref[...] + (coef * nrm_ref[3:4, :]) * y
        y_ref[...] = y.astype(BF16)

    return _pcall(
        body, name=name, grid=(S // tm,),
        in_specs=[pl.BlockSpec((tm, Ka), lambda i: (i, 0)),
                  _resident((K, Dm)),
                  pl.BlockSpec((tm, Dm), lambda i: (i, 0)),
                  pl.BlockSpec((8, Dm), lambda i: (0, 0))],
        out_specs=[pl.BlockSpec((tm, Dm), lambda i: (i, 0)),
                   pl.BlockSpec((tm, Dm), lambda i: (i, 0))],
        out_shape=[jax.ShapeDtypeStruct((S, Dm), F32), jax.ShapeDtypeStruct((S, Dm), BF16)],
        sem=("arbitrary",), args=(a, w, x, nrm), jobs=jobs)


def _proj_residual_bwd(dxo, gu, y, w, nrm, coef, name, jobs=()):
    swiglu = gu is not None
    S, Dm = dxo.shape
    K = w.shape[0]
    Ka = 2 * K if swiglu else K
    tm = _tile(S, 256)

    def body(*refs):
        if swiglu:
            dxo_ref, y_ref, w_ref, nrm_ref, a_ref, da_ref, act_ref, dy_ref, dgate_ref = refs
        else:
            dxo_ref, y_ref, w_ref, nrm_ref, da_ref, dy_ref, dgate_ref = refs
        dxo_v = dxo_ref[...]
        dyb = ((coef * nrm_ref[3:4, :]) * dxo_v).astype(BF16)
        dy_ref[...] = dyb

        @pl.when(pl.program_id(0) == 0)
        def _():
            dgate_ref[...] = jnp.zeros_like(dgate_ref)
        dgate_ref[0:1, :] += jnp.sum(coef * y_ref[...].astype(F32) * dxo_v, axis=0, keepdims=True)

        dact = _dot_nt(dyb, w_ref[...])
        if swiglu:
            g = a_ref[:, :K].astype(F32)
            u = a_ref[:, K:].astype(F32)
            s = _sigmoid(g)
            si = g * s
            da_ref[:, :K] = (dact * u * (s * (1.0 + g * (1.0 - s)))).astype(BF16)
            da_ref[:, K:] = (dact * si).astype(BF16)
            act_ref[...] = (si * u).astype(BF16)
        else:
            da_ref[...] = dact

    row = lambda i: (i, 0)
    fix = lambda i: (0, 0)
    in_specs = [pl.BlockSpec((tm, Dm), row), pl.BlockSpec((tm, Dm), row),
                _resident((K, Dm)), pl.BlockSpec((8, Dm), fix)]
    args = [dxo, y, w, nrm]
    if swiglu:
        in_specs.append(pl.BlockSpec((tm, Ka), row))
        args.append(gu)
        out_specs = [pl.BlockSpec((tm, Ka), row), pl.BlockSpec((tm, K), row),
                     pl.BlockSpec((tm, Dm), row), pl.BlockSpec((8, Dm), fix)]
        out_shape = [jax.ShapeDtypeStruct((S, Ka), BF16), jax.ShapeDtypeStruct((S, K), BF16),
                     jax.ShapeDtypeStruct((S, Dm), BF16), jax.ShapeDtypeStruct((8, Dm), F32)]
    else:
        out_specs = [pl.BlockSpec((tm, Ka), row), pl.BlockSpec((tm, Dm), row), pl.BlockSpec((8, Dm), fix)]
        out_shape = [jax.ShapeDtypeStruct((S, Ka), F32), jax.ShapeDtypeStruct((S, Dm), BF16),
                     jax.ShapeDtypeStruct((8, Dm), F32)]
    return _pcall(body, name=name, grid=(S // tm,), in_specs=in_specs, out_specs=out_specs, out_shape=out_shape,
                  sem=("arbitrary",), args=args, jobs=jobs)


def _atb(a, b, out_dtype, bm, bn, name, jobs=()):
    S, M = a.shape
    N = b.shape[1]
    bk = _tile(S, 1024)
    nk = S // bk

    def body(a_ref, b_ref, o_ref, acc):
        k = pl.program_id(2)

        @pl.when(k == 0)
        def _():
            acc[...] = jnp.zeros_like(acc)
        acc[...] += _dot_tn(a_ref[...], b_ref[...])

        @pl.when(k == nk - 1)
        def _():
            o_ref[...] = acc[...].astype(o_ref.dtype)

    (out,), extra = _pcall(
        body, name=name, grid=(M // bm, N // bn, nk),
        in_specs=[pl.BlockSpec((bk, bm), lambda m, n, k: (k, m)),
                  pl.BlockSpec((bk, bn), lambda m, n, k: (k, n))],
        out_specs=[pl.BlockSpec((bm, bn), lambda m, n, k: (m, n))],
        out_shape=[jax.ShapeDtypeStruct((M, N), out_dtype)],
        scratch_shapes=[pltpu.VMEM((bm, bn), F32)],
        sem=("arbitrary", "arbitrary", "arbitrary"), args=(a, b), jobs=jobs)
    return out, extra


def _nt_norm_bwd(dout, w, x, nrm, dxo, name, jobs=()):
    S, N = dout.shape
    Dm = w.shape[0]
    tm = _tile(S, 256)

    def body(do_ref, w_ref, x_ref, nrm_ref, dxo_ref, dx_ref, red_ref):
        @pl.when(pl.program_id(0) == 0)
        def _():
            red_ref[...] = jnp.zeros_like(red_ref)
        dh = _dot_nt(do_ref[...], w_ref[...])
        xv = x_ref[...]
        rstd = lax.rsqrt(jnp.mean(xv * xv, axis=-1, keepdims=True) + EPS)
        xn = xv * rstd
        gain = nrm_ref[0:1, :]
        hn = xn * gain
        dhn = dh * (1.0 + nrm_ref[2:3, :])
        red_ref[0:1, :] += jnp.sum(dh, axis=0, keepdims=True)
        red_ref[1:2, :] += jnp.sum(dh * hn, axis=0, keepdims=True)
        red_ref[2:3, :] += jnp.sum(dhn * xn, axis=0, keepdims=True)
        dxn = dhn * gain
        dx = rstd * (dxn - xn * jnp.mean(dxn * xn, axis=-1, keepdims=True))
        dx_ref[...] = dxo_ref[...] + dx

    return _pcall(
        body, name=name, grid=(S // tm,),
        in_specs=[pl.BlockSpec((tm, N), lambda i: (i, 0)),
                  _resident((Dm, N)),
                  pl.BlockSpec((tm, Dm), lambda i: (i, 0)),
                  pl.BlockSpec((8, Dm), lambda i: (0, 0)),
                  pl.BlockSpec((tm, Dm), lambda i: (i, 0))],
        out_specs=[pl.BlockSpec((tm, Dm), lambda i: (i, 0)),
                   pl.BlockSpec((8, Dm), lambda i: (0, 0))],
        out_shape=[jax.ShapeDtypeStruct((S, Dm), F32), jax.ShapeDtypeStruct((8, Dm), F32)],
        sem=("arbitrary",), args=(dout, w, x, nrm, dxo), jobs=jobs)


def _final_loss(x, gf, tgt, name):
    S, Dm = x.shape
    tm = _tile(S, 512)

    def body(x_ref, g_ref, t_ref, dx_ref, st_ref):
        @pl.when(pl.program_id(0) == 0)
        def _():
            st_ref[...] = jnp.zeros_like(st_ref)
        xv = x_ref[...]
        rstd = lax.rsqrt(jnp.mean(xv * xv, axis=-1, keepdims=True) + EPS)
        xn = xv * rstd
        gain = g_ref[0:1, :]
        err = xn * gain - t_ref[...]
        st_ref[1:2, :] += jnp.full((1, Dm), 0.5 / Dm, F32) * jnp.sum(err * err)
        dy = err * (1.0 / Dm)
        st_ref[0:1, :] += jnp.sum(dy * xn, axis=0, keepdims=True)
        dxn = dy * gain
        dx_ref[...] = rstd * (dxn - xn * jnp.mean(dxn * xn, axis=-1, keepdims=True))

    return pl.pallas_call(
        body, name=name, grid=(S // tm,),
        in_specs=[pl.BlockSpec((tm, Dm), lambda i: (i, 0)),
                  pl.BlockSpec((8, Dm), lambda i: (0, 0)),
                  pl.BlockSpec((tm, Dm), lambda i: (i, 0))],
        out_specs=[pl.BlockSpec((tm, Dm), lambda i: (i, 0)),
                   pl.BlockSpec((8, Dm), lambda i: (0, 0))],
        out_shape=[jax.ShapeDtypeStruct((S, Dm), F32), jax.ShapeDtypeStruct((8, Dm), F32)],
        compiler_params=_cp("arbitrary"),
    )(x, gf, tgt)


def _alibi_slope(h):
    return float(2.0 ** (-8.0 * (h + 1) / N_Q_HEADS))


def _head_planes(pair_cols):
    lane = lax.broadcasted_iota(jnp.int32, pair_cols.shape, 1)
    low = lane < HEAD_DIM
    h0_lo = jnp.where(low, pair_cols, 0.0)
    h1_hi = jnp.where(low, 0.0, pair_cols)
    h0_hi = pltpu.roll(h0_lo, HEAD_DIM, 1)
    h1_lo = pltpu.roll(h1_hi, HEAD_DIM, 1)
    return ((h0_lo.astype(BF16), h0_hi.astype(BF16)), (h1_lo.astype(BF16), h1_hi.astype(BF16)))


def _to_plane(v, e, g):
    lane = lax.broadcasted_iota(jnp.int32, v.shape, 1)
    keep = (lane < HEAD_DIM) if e == 0 else (lane >= HEAD_DIM)
    v = jnp.where(keep, v, 0.0)
    return v if e == g else pltpu.roll(v, HEAD_DIM, 1)


def _band_mask(first_block):
    qi = lax.broadcasted_iota(jnp.int32, (BLOCK, 2 * BLOCK), 0)
    kj = lax.broadcasted_iota(jnp.int32, (BLOCK, 2 * BLOCK), 1)
    dist = qi + BLOCK - kj
    valid = (dist >= 0) & (dist < BLOCK) & (kj >= first_block * BLOCK)
    return dist.astype(F32), valid


def _softmax_band(qp, kx, h, dist, valid, sink):
    s = _dot_nt(qp, kx)
    s = jnp.where(valid, s - _alibi_slope(h) * dist, NEG_INF)
    m = jnp.maximum(jnp.max(s, axis=-1, keepdims=True), sink)
    p = jnp.exp(s - m)
    den = jnp.sum(p, axis=-1, keepdims=True) + jnp.exp(sink - m)
    return p / den, jnp.exp(sink - m) / den


def _past(cur, prev, s, row):
    return jnp.where(row < s, pltpu.roll(prev, s, 0), pltpu.roll(cur, s, 0))


def _future(cur, nxt, s, row):
    T = cur.shape[0]
    return jnp.where(row >= T - s, pltpu.roll(nxt, T - s, 0), pltpu.roll(cur, T - s, 0))


def _edge_row(v, last):
    T = v.shape[0]
    r8 = lax.broadcasted_iota(jnp.int32, (SUBLANES, v.shape[1]), 0)
    blk = v[T - SUBLANES:, :] if last else v[:SUBLANES, :]
    return jnp.sum(jnp.where(r8 == (SUBLANES - 1 if last else 0), blk, 0.0), axis=0, keepdims=True)


def _lru_gates(lx, lx_prev, small_ref, wa_ref, wx_ref, row, t0):
    xc = (small_ref[4:5, :] + small_ref[3:4, :] * lx + small_ref[2:3, :] * _past(lx, lx_prev, 1, row)
          + small_ref[1:2, :] * _past(lx, lx_prev, 2, row) + small_ref[0:1, :] * _past(lx, lx_prev, 3, row))
    xcb = xc.astype(BF16)
    r = _sigmoid(_dot(xcb, wa_ref[...]) + small_ref[5:6, :])
    ig = _sigmoid(_dot(xcb, wx_ref[...]) + small_ref[6:7, :])
    sp = _softplus_neg(small_ref[7:8, :])
    la = (-LRU_C) * r * sp
    a = jnp.exp(la)
    first = (row + t0) == 0
    mult = jnp.where(first, 1.0, jnp.sqrt(-_expm1(2.0 * la)))
    return xc, xcb, r, ig, sp, a, mult, first


def _mixer_fwd(proj, sinks, small, wa, wx, name, jobs=()):
    S = proj.shape[0]
    T = MIX_TILE
    nT = S // T
    nb = T // BLOCK

    def body(proj_ref, sink_ref, small_ref, wa_ref, wx_ref, y_ref, hp_ref, kvp, lxp, zp, hcar):
        i = pl.program_id(0)

        @pl.when(i == 0)
        def _():
            kvp[...] = jnp.zeros_like(kvp)
            lxp[...] = jnp.zeros_like(lxp)
            zp[...] = jnp.zeros_like(zp)
            hcar[...] = jnp.zeros_like(hcar)

        row = lax.broadcasted_iota(jnp.int32, (T, LRU_WIDTH), 0)

        kv = proj_ref[:, C_KV:C_KV + 2 * KV_WIDTH]
        ext = jnp.concatenate([kvp[...], kv], axis=0)
        kx = _head_planes(ext[:, :KV_WIDTH])
        vx = _head_planes(ext[:, KV_WIDTH:])
        first_tile = jnp.where(i == 0, 1, 0)
        for b in range(nb):
            dist, valid = _band_mask(first_tile if b == 0 else 0)
            keys = slice(b * BLOCK, (b + 2) * BLOCK)
            for pair in range(N_Q_HEADS // 2):
                g = pair // 2
                qp = (proj_ref[b * BLOCK:(b + 1) * BLOCK, pair * LANES:(pair + 1) * LANES] * 0.125).astype(BF16)
                o = jnp.zeros((BLOCK, LANES), F32)
                for e in range(2):
                    h = 2 * pair + e
                    pn, _ = _softmax_band(qp, kx[g][e][keys], h, dist, valid, sink_ref[h])
                    o = o + _dot(pn.astype(BF16), vx[g][e][keys])
                y_ref[b * BLOCK:(b + 1) * BLOCK, pair * LANES:(pair + 1) * LANES] = o.astype(BF16)
        kvp[...] = kv[T - BLOCK:, :]

        lx = proj_ref[:, C_LX:C_LX + LRU_WIDTH]
        xc, _, _, ig, _, a, mult, _ = _lru_gates(lx, lxp[...], small_ref, wa_ref, wx_ref, row, i * T)
        lxp[...] = lx
        aa = a
        bb = mult * (ig * xc)
        s = 1
        while s < T:
            a_sh = jnp.where(row >= s, pltpu.roll(aa, s, 0), 1.0)
            b_sh = jnp.where(row >= s, pltpu.roll(bb, s, 0), 0.0)
            bb = aa * b_sh + bb
            aa = aa * a_sh
            s *= 2
        hc = hcar[0:1, :]
        hh = bb + aa * hc
        hp_ref[...] = jnp.where(row < 1, hc, pltpu.roll(hh, 1, 0))
        hcar[...] = jnp.broadcast_to(_edge_row(hh, True), hcar.shape)
        gl, _ = _gelu(proj_ref[:, C_LG:C_LG + LRU_WIDTH])
        y_ref[:, ATTN_WIDTH:ATTN_WIDTH + LRU_WIDTH] = (gl * hh).astype(BF16)

        z = proj_ref[:, C_SC:C_SC + CONV_WIDTH] * proj_ref[:, C_SX:C_SX + CONV_WIDTH]
        c3 = (small_ref[10:11, :] * z + small_ref[9:10, :] * _past(z, zp[...], 1, row)
              + small_ref[8:9, :] * _past(z, zp[...], 2, row))
        zp[...] = z
        y_ref[:, ATTN_WIDTH + LRU_WIDTH:] = (proj_ref[:, C_SB:C_SB + CONV_WIDTH] * c3).astype(BF16)

    fix = lambda i: (0, 0)
    return _pcall(
        body, name=name, grid=(nT,),
        in_specs=[pl.BlockSpec((T, IN_PROJ_WIDTH), lambda i: (i, 0)),
                  pl.BlockSpec(memory_space=pltpu.SMEM),
                  pl.BlockSpec((16, LRU_WIDTH), fix),
                  pl.BlockSpec((LRU_WIDTH, LRU_WIDTH), fix),
                  pl.BlockSpec((LRU_WIDTH, LRU_WIDTH), fix)],
        out_specs=[pl.BlockSpec((T, D_MODEL), lambda i: (i, 0)),
                   pl.BlockSpec((T, LRU_WIDTH), lambda i: (i, 0))],
        out_shape=[jax.ShapeDtypeStruct((S, D_MODEL), BF16), jax.ShapeDtypeStruct((S, LRU_WIDTH), F32)],
        scratch_shapes=[pltpu.VMEM((BLOCK, 2 * KV_WIDTH), F32), pltpu.VMEM((T, LRU_WIDTH), F32),
                        pltpu.VMEM((T, CONV_WIDTH), F32), pltpu.VMEM((SUBLANES, LRU_WIDTH), F32)],
        sem=("arbitrary",), args=(proj, sinks, small, wa, wx), jobs=jobs)


def _mixer_bwd(proj, dymix, hprev, sinks, small, wa, wx, name, jobs=()):
    S = proj.shape[0]
    T = MIX_TILE
    nT = S // T
    nb = T // BLOCK
    bpt = T // BLOCK

    def body(proj_ref, kvprev_ref, lxprev_ref, scprev_ref, sxprev_ref, dy_ref, hp_ref, sink_ref, small_ref,
             wa_ref, wx_ref, dp_ref, dsm_ref, dsink_ref, dwa_ref, dwx_ref,
             dk_s, dv_s, dkv_c, dxc_n, dc3_n, p_c):
        i = pl.program_id(0)
        ti = nT - 1 - i
        has_prev = jnp.where(ti == 0, 0.0, 1.0)

        @pl.when(i == 0)
        def _():
            for r in (dkv_c, dxc_n, dc3_n, p_c, dsm_ref, dsink_ref, dwa_ref, dwx_ref):
                r[...] = jnp.zeros_like(r)

        row = lax.broadcasted_iota(jnp.int32, (T, LRU_WIDTH), 0)

        kv = proj_ref[:, C_KV:C_KV + 2 * KV_WIDTH]
        ext = jnp.concatenate([kvprev_ref[...] * has_prev, kv], axis=0)
        kx = _head_planes(ext[:, :KV_WIDTH])
        vx = _head_planes(ext[:, KV_WIDTH:])
        dk_s[...] = jnp.zeros_like(dk_s)
        dv_s[...] = jnp.zeros_like(dv_s)
        dk_s[T:, :] = dkv_c[:, :KV_WIDTH]
        dv_s[T:, :] = dkv_c[:, KV_WIDTH:]
        first_tile = jnp.where(ti == 0, 1, 0)
        for b in range(nb):
            dist, valid = _band_mask(first_tile if b == 0 else 0)
            keys = slice(b * BLOCK, (b + 2) * BLOCK)
            rows = slice(b * BLOCK, (b + 1) * BLOCK)
            for pair in range(N_Q_HEADS // 2):
                g = pair // 2
                cols = slice(pair * LANES, (pair + 1) * LANES)
                qp = (proj_ref[rows, cols] * 0.125).astype(BF16)
                dob = dy_ref[rows, cols].astype(BF16)
                dq = jnp.zeros((BLOCK, LANES), F32)
                for e in range(2):
                    h = 2 * pair + e
                    pn, psink = _softmax_band(qp, kx[g][e][keys], h, dist, valid, sink_ref[h])
                    dpm = _dot_nt(dob, vx[g][e][keys])
                    dsum = jnp.sum(pn * dpm, axis=-1, keepdims=True)
                    ds = (pn * (dpm - dsum)).astype(BF16)
                    dsink_ref[h:h + 1, :] += jnp.full((1, LANES), -1.0, F32) * jnp.sum(psink * dsum)
                    dv_s[keys, :] += _to_plane(_dot_tn(pn.astype(BF16), dob), e, g)
                    dk_s[keys, :] += _to_plane(_dot_tn(ds, qp), e, g)
                    dq = dq + _dot(ds, kx[g][e][keys])
                dp_ref[rows, cols] = (0.125 * dq).astype(BF16)
        dp_ref[:, C_KV:C_KV + KV_WIDTH] = dk_s[BLOCK:, :].astype(BF16)
        dp_ref[:, C_KV + KV_WIDTH:C_KV + 2 * KV_WIDTH] = dv_s[BLOCK:, :].astype(BF16)
        dkv_c[:, :KV_WIDTH] = dk_s[:BLOCK, :]
        dkv_c[:, KV_WIDTH:] = dv_s[:BLOCK, :]

        lx = proj_ref[:, C_LX:C_LX + LRU_WIDTH]
        lxprev = lxprev_ref[...] * has_prev
        xc, xcb, r, ig, sp, a, mult, first = _lru_gates(lx, lxprev, small_ref, wa_ref, wx_ref, row, ti * T)
        hp = hp_ref[...]
        hh = a * hp + mult * (ig * xc)
        lg = proj_ref[:, C_LG:C_LG + LRU_WIDTH]
        gl, th = _gelu(lg)
        dyl = dy_ref[:, ATTN_WIDTH:ATTN_WIDTH + LRU_WIDTH]
        dp_ref[:, C_LG:C_LG + LRU_WIDTH] = (dyl * hh * _gelu_grad(lg, th)).astype(BF16)
        aa = jnp.where(row < T - 1, pltpu.roll(a, T - 1, 0), 1.0)
        bb = dyl * gl
        s = 1
        while s < T:
            a_sh = jnp.where(row < T - s, pltpu.roll(aa, T - s, 0), 1.0)
            b_sh = jnp.where(row < T - s, pltpu.roll(bb, T - s, 0), 0.0)
            bb = bb + aa * b_sh
            aa = aa * a_sh
            s *= 2
        G = bb + aa * p_c[0:1, :]
        p_c[...] = jnp.broadcast_to(_edge_row(a * G, False), p_c.shape)
        da = G * hp
        dmult = G * (ig * xc)
        dig = G * mult * xc
        dxc = G * mult * ig
        dla = da * a + dmult * jnp.where(first, 0.0, -(a * a) / mult)
        dr = dla * ((-LRU_C) * sp)
        lam = small_ref[7:8, :]
        dsm_ref[7:8, :] += jnp.sum(dla * ((-LRU_C) * r), axis=0, keepdims=True) * (-_sigmoid(-lam))
        dpa = dr * r * (1.0 - r)
        dpx = dig * ig * (1.0 - ig)
        dsm_ref[5:6, :] += jnp.sum(dpa, axis=0, keepdims=True)
        dsm_ref[6:7, :] += jnp.sum(dpx, axis=0, keepdims=True)
        dpab = dpa.astype(BF16)
        dpxb = dpx.astype(BF16)
        dwa_ref[...] += _dot_tn(xcb, dpab)
        dwx_ref[...] += _dot_tn(xcb, dpxb)
        dxc = dxc + _dot_nt(dpab, wa_ref[...]) + _dot_nt(dpxb, wx_ref[...])
        dsm_ref[4:5, :] += jnp.sum(dxc, axis=0, keepdims=True)
        dsm_ref[3:4, :] += jnp.sum(dxc * lx, axis=0, keepdims=True)
        for k in range(3):
            dsm_ref[k:k + 1, :] += jnp.sum(dxc * _past(lx, lxprev, 3 - k, row), axis=0, keepdims=True)
        nxt = dxc_n[...]
        dlx = (small_ref[3:4, :] * dxc + small_ref[2:3, :] * _future(dxc, nxt, 1, row)
               + small_ref[1:2, :] * _future(dxc, nxt, 2, row) + small_ref[0:1, :] * _future(dxc, nxt, 3, row))
        dxc_n[...] = dxc
        dp_ref[:, C_LX:C_LX + LRU_WIDTH] = dlx.astype(BF16)

        sc = proj_ref[:, C_SC:C_SC + CONV_WIDTH]
        sx = proj_ref[:, C_SX:C_SX + CONV_WIDTH]
        sb = proj_ref[:, C_SB:C_SB + CONV_WIDTH]
        z = sc * sx
        zprev = (scprev_ref[...] * sxprev_ref[...]) * has_prev
        z1 = _past(z, zprev, 1, row)
        z2 = _past(z, zprev, 2, row)
        c3 = small_ref[10:11, :] * z + small_ref[9:10, :] * z1 + small_ref[8:9, :] * z2
        dys = dy_ref[:, ATTN_WIDTH + LRU_WIDTH:]
        dp_ref[:, C_SB:C_SB + CONV_WIDTH] = (dys * c3).astype(BF16)
        dc3 = dys * sb
        dsm_ref[10:11, :] += jnp.sum(dc3 * z, axis=0, keepdims=True)
        dsm_ref[9:10, :] += jnp.sum(dc3 * z1, axis=0, keepdims=True)
        dsm_ref[8:9, :] += jnp.sum(dc3 * z2, axis=0, keepdims=True)
        nxt3 = dc3_n[...]
        dz = (small_ref[10:11, :] * dc3 + small_ref[9:10, :] * _future(dc3, nxt3, 1, row)
              + small_ref[8:9, :] * _future(dc3, nxt3, 2, row))
        dc3_n[...] = dc3
        dp_ref[:, C_SC:C_SC + CONV_WIDTH] = (dz * sx).astype(BF16)
        dp_ref[:, C_SX:C_SX + CONV_WIDTH] = (dz * sc).astype(BF16)

    fix = lambda i: (0, 0)
    cur = lambda i: (nT - 1 - i, 0)
    prev_cols = lambda cb: (lambda i: (jnp.maximum(nT - 2 - i, 0), cb))
    return _pcall(
        body, name=name, grid=(nT,),
        in_specs=[pl.BlockSpec((T, IN_PROJ_WIDTH), cur),
                  pl.BlockSpec((BLOCK, 2 * KV_WIDTH),
                               lambda i: (jnp.maximum((nT - 1 - i) * bpt - 1, 0), C_KV // (2 * KV_WIDTH))),
                  pl.BlockSpec((T, LRU_WIDTH), prev_cols(C_LX // LRU_WIDTH)),
                  pl.BlockSpec((T, CONV_WIDTH), prev_cols(C_SC // CONV_WIDTH)),
                  pl.BlockSpec((T, CONV_WIDTH), prev_cols(C_SX // CONV_WIDTH)),
                  pl.BlockSpec((T, D_MODEL), cur),
                  pl.BlockSpec((T, LRU_WIDTH), cur),
                  pl.BlockSpec(memory_space=pltpu.SMEM),
                  pl.BlockSpec((16, LRU_WIDTH), fix),
                  pl.BlockSpec((LRU_WIDTH, LRU_WIDTH), fix),
                  pl.BlockSpec((LRU_WIDTH, LRU_WIDTH), fix)],
        out_specs=[pl.BlockSpec((T, IN_PROJ_WIDTH), cur),
                   pl.BlockSpec((16, LRU_WIDTH), fix),
                   pl.BlockSpec((SUBLANES, LANES), fix),
                   pl.BlockSpec((LRU_WIDTH, LRU_WIDTH), fix),
                   pl.BlockSpec((LRU_WIDTH, LRU_WIDTH), fix)],
        out_shape=[jax.ShapeDtypeStruct((S, IN_PROJ_WIDTH), BF16),
                   jax.ShapeDtypeStruct((16, LRU_WIDTH), F32),
                   jax.ShapeDtypeStruct((SUBLANES, LANES), F32),
                   jax.ShapeDtypeStruct((LRU_WIDTH, LRU_WIDTH), F32),
                   jax.ShapeDtypeStruct((LRU_WIDTH, LRU_WIDTH), F32)],
        scratch_shapes=[pltpu.VMEM((T + BLOCK, KV_WIDTH), F32), pltpu.VMEM((T + BLOCK, KV_WIDTH), F32),
                        pltpu.VMEM((BLOCK, 2 * KV_WIDTH), F32), pltpu.VMEM((T, LRU_WIDTH), F32),
                        pltpu.VMEM((T, CONV_WIDTH), F32), pltpu.VMEM((SUBLANES, LRU_WIDTH), F32)],
        sem=("arbitrary",), args=(proj, proj, proj, proj, proj, dymix, hprev, sinks, small, wa, wx), jobs=jobs)


def _mod_matmul(c_all, w_mod, name):
    L, Dm, N = w_mod.shape
    R = c_all.shape[0]
    tn = 768

    def body(c_ref, w_ref, o_ref, ca_ref):
        cv = c_ref[...]
        ca = (cv * _sigmoid(cv)).astype(BF16)
        ca_ref[...] = ca
        o_ref[0] = _dot(ca, w_ref[0].astype(BF16))

    return pl.pallas_call(
        body, name=name, grid=(L, N // tn),
        in_specs=[pl.BlockSpec((R, Dm), lambda l, n: (0, 0)),
                  pl.BlockSpec((1, Dm, tn), lambda l, n: (l, 0, n))],
        out_specs=[pl.BlockSpec((1, R, tn), lambda l, n: (l, 0, n)), pl.BlockSpec((R, Dm), lambda l, n: (0, 0))],
        out_shape=[jax.ShapeDtypeStruct((L, R, N), F32), jax.ShapeDtypeStruct((R, Dm), BF16)],
        compiler_params=_cp("arbitrary", "arbitrary"),
    )(c_all, w_mod)


def _adamw(w, gs, m, v, name):
    R, C = w.shape
    tr = 8
    for cand in (512, 256, 128, 64, 32, 16, 8):
        if R % cand == 0 and cand * C * 4 <= (1 << 20):
            tr = cand
            break
    ng = len(gs)
    bc1 = 1.0 - ADAM_B1 ** ADAM_STEP
    bc2 = 1.0 - ADAM_B2 ** ADAM_STEP

    def body(*refs):
        w_ref = refs[0]
        g_refs = refs[1:1 + ng]
        m_ref, v_ref, go_ref, d_ref, mo_ref, vo_ref = refs[1 + ng:]
        g = g_refs[0][...].astype(F32)
        for gr in g_refs[1:]:
            g = g + gr[...].astype(F32)
        mn = ADAM_B1 * m_ref[...] + (1.0 - ADAM_B1) * g
        vn = ADAM_B2 * v_ref[...] + (1.0 - ADAM_B2) * (g * g)
        go_ref[...] = g
        mo_ref[...] = mn
        vo_ref[...] = vn
        d_ref[...] = (-ADAM_LR) * ((mn / bc1) / (jnp.sqrt(vn / bc2) + ADAM_EPS) + ADAM_WD * w_ref[...])

    spec = pl.BlockSpec((tr, C), lambda i: (i, 0))
    return pl.pallas_call(
        body, name=name, grid=(R // tr,),
        in_specs=[spec] * (3 + ng), out_specs=[spec] * 4,
        out_shape=[jax.ShapeDtypeStruct((R, C), F32)] * 4,
        compiler_params=_cp("arbitrary"),
    )(w, *gs, m, v)


def _sum4_layers(recvs, name):
    nl = len(recvs)
    _, R, C = recvs[0].shape
    tr = 8
    for cand in (512, 256, 128, 64, 32, 16):
        if R % cand == 0 and cand * C * 4 <= (1 << 20):
            tr = cand
            break
    ni = R // tr

    def body(*refs):
        r_refs, o_ref = refs[:nl], refs[nl]
        for l in range(nl):
            @pl.when(pl.program_id(0) == l)
            def _(r=r_refs[l]):
                o_ref[...] = ((r[0].astype(F32) + r[1].astype(F32)) + r[2].astype(F32)) + r[3].astype(F32)

    def spec(l):
        return pl.BlockSpec((4, tr, C), lambda ll, i: (0, jnp.where(ll == l, i, jnp.where(ll < l, 0, ni - 1)), 0))

    return pl.pallas_call(
        body, name=name, grid=(nl, ni),
        in_specs=[spec(l) for l in range(nl)],
        out_specs=pl.BlockSpec((tr, C), lambda ll, i: (ll * ni + i, 0)),
        out_shape=jax.ShapeDtypeStruct((nl * R, C), F32),
        compiler_params=_cp("arbitrary", "arbitrary"),
    )(*recvs)


def _all_gather_small(v, name):
    M, N = v.shape

    def body(x_ref, out_ref, sum_ref, send_sems, recv_sems, local_sem):
        x, y, c = lax.axis_index("x"), lax.axis_index("y"), lax.axis_index("c")
        me, sibling = (x, y, c), (x, y, 1 - c)
        chips = [(1 - x, y), (x, 1 - y), (1 - x, 1 - y)]

        def rows(px, py, pc):
            return out_ref.at[pl.ds(pl.multiple_of((4 * px + 2 * py + pc) * M, SUBLANES), M), :]

        def copy(k, block, to, src=None):
            return pltpu.make_async_remote_copy(
                src_ref=rows(*block) if src is None else src, dst_ref=rows(*block),
                send_sem=send_sems.at[k], recv_sem=recv_sems.at[k], device_id=to, device_id_type=MESH)

        mine = pltpu.make_async_copy(x_ref, rows(*me), local_sem)
        mine.start()
        first = [copy(0, me, sibling, src=x_ref)]
        first += [copy(1 + j, me, (*chip, c), src=x_ref) for j, chip in enumerate(chips)]
        for cp in first:
            cp.start()
        passed = [copy(4 + j, (*chip, c), sibling) for j, chip in enumerate(chips)]
        for j, chip in enumerate(chips):
            copy(1 + j, (*chip, c), me).wait_recv()
            passed[j].start()
        copy(0, sibling, me).wait_recv()
        for j, chip in enumerate(chips):
            copy(4 + j, (*chip, 1 - c), me).wait_recv()
        for cp in first + passed:
            cp.wait_send()
        mine.wait()
        acc = out_ref[0:M, :]
        for d in range(1, N_DEV):
            acc = acc + out_ref[d * M:(d + 1) * M, :]
        sum_ref[...] = acc

    return pl.pallas_call(
        body, name=name,
        out_shape=[jax.ShapeDtypeStruct((N_DEV * M, N), F32), jax.ShapeDtypeStruct((M, N), F32)],
        in_specs=[pl.BlockSpec(memory_space=pltpu.VMEM)],
        out_specs=[pl.BlockSpec(memory_space=pltpu.VMEM), pl.BlockSpec(memory_space=pltpu.VMEM)],
        scratch_shapes=[pltpu.SemaphoreType.DMA((7,)), pltpu.SemaphoreType.DMA((7,)), pltpu.SemaphoreType.DMA],
        compiler_params=pltpu.CompilerParams(vmem_limit_bytes=VMEM_LIMIT),
    )(v)


_BIG = (("w_ffn1_gu", 1), ("w_ffn1_down", 0), ("w_ffn2_gu", 1), ("w_ffn2_down", 0), ("w_in", 1), ("w_out", 0))
_AXIS = dict(_BIG)

_GATHER_PLAN = {
    "first": [(0, "w_ffn1_gu")],
    (0, "ffn1_gu"): [(0, "w_ffn1_down"), (0, "w_in"), (0, "w_out")],
    (0, "ffn1_down"): [(1, "w_ffn1_down")],
    (0, "mix_in"): [(0, "w_ffn2_down")],
    (0, "mix_core"): [(0, "w_ffn2_gu")],
    (0, "mix_out"): [(1, "w_in")],
    (0, "ffn2_gu"): [(1, "w_ffn1_gu")],
    (0, "ffn2_down"): [(1, "w_out")],
    (1, "ffn1_gu"): [(1, "w_ffn2_gu")],
    (1, "ffn1_down"): [(1, "w_ffn2_down")],
}


def _swap_with_sibling(parts, name):
    n = len(parts)

    def body(*refs):
        ins, outs = refs[:n], refs[n:2 * n]
        send_sems, recv_sems = refs[2 * n:]
        x, y, c = lax.axis_index("x"), lax.axis_index("y"), lax.axis_index("c")
        copies = [pltpu.make_async_remote_copy(
            src_ref=ins[t], dst_ref=outs[t], send_sem=send_sems.at[t], recv_sem=recv_sems.at[t],
            device_id=(x, y, 1 - c), device_id_type=MESH) for t in range(n)]
        for cp in copies:
            cp.start()
        for cp in copies:
            cp.wait_recv()
        for cp in copies:
            cp.wait_send()

    hbm = pl.BlockSpec(memory_space=pltpu.HBM)
    return pl.pallas_call(
        body, name=name, out_shape=[jax.ShapeDtypeStruct(p.shape, p.dtype) for p in parts],
        in_specs=[hbm] * n, out_specs=[hbm] * n,
        scratch_shapes=[pltpu.SemaphoreType.DMA((n,)), pltpu.SemaphoreType.DMA((n,))],
    )(*parts)


def _pack(arrs, rows_multiple=SUBLANES):
    flat = jnp.concatenate([a.astype(F32).reshape(-1) for a in arrs])
    unit = rows_multiple * LANES
    total = -(-flat.shape[0] // unit) * unit
    return jnp.pad(flat, (0, total - flat.shape[0])).reshape(total // LANES, LANES)


def _unpack(flat, shapes):
    out, off = [], 0
    for shp in shapes:
        n = int(math.prod(shp))
        out.append(flat[off:off + n].reshape(shp))
        off += n
    return out


def _block_diag(w):
    out = jnp.zeros((LRU_WIDTH, LRU_WIDTH), F32)
    for h in range(4):
        out = lax.dynamic_update_slice(out, w[h], (h * HEAD_DIM, h * HEAD_DIM))
    return out


def _diag_blocks(w):
    return jnp.stack([w[h * HEAD_DIM:(h + 1) * HEAD_DIM, h * HEAD_DIM:(h + 1) * HEAD_DIM] for h in range(4)])


def _rows8(*rows):
    z = jnp.zeros((8 - len(rows), rows[0].shape[-1]), F32)
    return jnp.concatenate([jnp.stack(rows), z], axis=0)


def kernel(x, c, w_mod, b_mod, g_norm, w_ffn1_gu, w_ffn1_down, w_ffn2_gu, w_ffn2_down, w_in, w_out, attn_sinks, lru_conv_w, lru_conv_b, lru_gate_a_w, lru_gate_a_b, lru_gate_x_w, lru_gate_x_b, lru_lambda, sc_conv_w, g_final, loss_target, m_w_mod, m_b_mod, m_g_norm, m_w_ffn1_gu, m_w_ffn1_down, m_w_ffn2_gu, m_w_ffn2_down, m_w_in, m_w_out, m_attn_sinks, m_lru_conv_w, m_lru_conv_b, m_lru_gate_a_w, m_lru_gate_a_b, m_lru_gate_x_w, m_lru_gate_x_b, m_lru_lambda, m_sc_conv_w, m_g_final, v_w_mod, v_b_mod, v_g_norm, v_w_ffn1_gu, v_w_ffn1_down, v_w_ffn2_gu, v_w_ffn2_down, v_w_in, v_w_out, v_attn_sinks, v_lru_conv_w, v_lru_conv_b, v_lru_gate_a_w, v_lru_gate_a_b, v_lru_gate_x_w, v_lru_gate_x_b, v_lru_lambda, v_sc_conv_w, v_g_final):
    W = dict(w_mod=w_mod, b_mod=b_mod, g_norm=g_norm, w_ffn1_gu=w_ffn1_gu, w_ffn1_down=w_ffn1_down,
             w_ffn2_gu=w_ffn2_gu, w_ffn2_down=w_ffn2_down, w_in=w_in, w_out=w_out, attn_sinks=attn_sinks,
             lru_conv_w=lru_conv_w, lru_conv_b=lru_conv_b, lru_gate_a_w=lru_gate_a_w, lru_gate_a_b=lru_gate_a_b,
             lru_gate_x_w=lru_gate_x_w, lru_gate_x_b=lru_gate_x_b, lru_lambda=lru_lambda, sc_conv_w=sc_conv_w,
             g_final=g_final)
    M1 = dict(w_mod=m_w_mod, b_mod=m_b_mod, g_norm=m_g_norm, w_ffn1_gu=m_w_ffn1_gu, w_ffn1_down=m_w_ffn1_down,
              w_ffn2_gu=m_w_ffn2_gu, w_ffn2_down=m_w_ffn2_down, w_in=m_w_in, w_out=m_w_out,
              attn_sinks=m_attn_sinks, lru_conv_w=m_lru_conv_w, lru_conv_b=m_lru_conv_b,
              lru_gate_a_w=m_lru_gate_a_w, lru_gate_a_b=m_lru_gate_a_b, lru_gate_x_w=m_lru_gate_x_w,
              lru_gate_x_b=m_lru_gate_x_b, lru_lambda=m_lru_lambda, sc_conv_w=m_sc_conv_w, g_final=m_g_final)
    V1 = dict(w_mod=v_w_mod, b_mod=v_b_mod, g_norm=v_g_norm, w_ffn1_gu=v_w_ffn1_gu, w_ffn1_down=v_w_ffn1_down,
              w_ffn2_gu=v_w_ffn2_gu, w_ffn2_down=v_w_ffn2_down, w_in=v_w_in, w_out=v_w_out,
              attn_sinks=v_attn_sinks, lru_conv_w=v_lru_conv_w, lru_conv_b=v_lru_conv_b,
              lru_gate_a_w=v_lru_gate_a_w, lru_gate_a_b=v_lru_gate_a_b, lru_gate_x_w=v_lru_gate_x_w,
              lru_gate_x_b=v_lru_gate_x_b, lru_lambda=v_lru_lambda, sc_conv_w=v_sc_conv_w, g_final=v_g_final)
    names = ["w_mod", "b_mod", "g_norm", "w_ffn1_gu", "w_ffn1_down", "w_ffn2_gu", "w_ffn2_down", "w_in", "w_out",
             "attn_sinks", "lru_conv_w", "lru_conv_b", "lru_gate_a_w", "lru_gate_a_b", "lru_gate_x_w",
             "lru_gate_x_b", "lru_lambda", "sc_conv_w", "g_final"]

    xs = x[0]
    tgt = loss_target[0]
    S = xs.shape[0]
    chip = 2 * lax.axis_index("x") + lax.axis_index("y")
    batch = 2 * chip + lax.axis_index("c")
    L = DEPTH

    fwd_shapes = [(D_MODEL,), g_norm.shape, lru_conv_w.shape, sc_conv_w.shape]
    gathered, _ = _all_gather_small(_pack([c[0], g_norm, lru_conv_w, sc_conv_w]), "gather_small_fwd")
    gathered = gathered.reshape(N_DEV, -1)
    c_all = gathered[:, :D_MODEL]
    per_chip = [_unpack(gathered[2 * jj], fwd_shapes) for jj in range(N_CHIPS)]
    g_norm_full = jnp.concatenate([p[1] for p in per_chip], axis=-1)
    lru_conv_w_full = jnp.concatenate([p[2] for p in per_chip], axis=-1)
    sc_conv_w_full = jnp.concatenate([p[3] for p in per_chip], axis=-1)

    c_pad = jnp.concatenate([c_all, jnp.zeros_like(c_all)], axis=0)
    mod_part, c_act = _mod_matmul(c_pad, w_mod, "mod_matmul")
    mod_all, _ = _all_gather_small(mod_part.reshape(-1, LANES), "gather_mod")
    mod_all = mod_all.reshape(N_DEV, L, 16, -1)
    mod_rows = [lax.dynamic_index_in_dim(mod_all[2 * jj], batch, axis=1, keepdims=False) for jj in range(N_CHIPS)]
    mod = (jnp.concatenate(mod_rows, axis=-1) + b_mod).reshape(L, 9, D_MODEL)

    def nrm_rows(l, s):
        return _rows8(g_norm_full[l, s], mod[l, 3 * s], mod[l, 3 * s + 1], mod[l, 3 * s + 2])

    full = {}

    def gather_jobs(key):
        return [_GatherJob(W[n][l].astype(BF16), _AXIS[n]) for l, n in _GATHER_PLAN.get(key, ())]

    def landed(key, outs):
        full.update(zip(_GATHER_PLAN.get(key, ()), outs))

    landed("first", _comm_only(gather_jobs("first"), "gather_first"))

    def mixer_params(l):
        small = jnp.concatenate([lru_conv_w_full[l], lru_conv_b[l][None], lru_gate_a_b[l][None],
                                 lru_gate_x_b[l][None], lru_lambda[l][None], sc_conv_w_full[l],
                                 jnp.zeros((5, LRU_WIDTH), F32)], axis=0)
        return (attn_sinks[l], small, _block_diag(lru_gate_a_w[l]).astype(BF16),
                _block_diag(lru_gate_x_w[l]).astype(BF16))

    saved = []
    xcur = xs
    for l in range(L):
        n1, n2, n3 = nrm_rows(l, 0), nrm_rows(l, 1), nrm_rows(l, 2)

        def ffn(which, xin, nrm):
            key = (l, which + "_gu")
            (gu, h), ex = _norm_matmul(xin, nrm, full[(l, f"w_{which}_gu")], BF16, f"l{l}_{which}_gu", gather_jobs(key))
            landed(key, ex)
            key = (l, which + "_down")
            (xo, y), ex = _proj_residual(gu, full[(l, f"w_{which}_down")], xin, nrm, 0.5, True, f"l{l}_{which}_down",
                                         gather_jobs(key))
            landed(key, ex)
            return xo, (xin, h, gu, y)

        x1, s1 = ffn("ffn1", xcur, n1)
        (proj, h2), ex = _norm_matmul(x1, n2, full[(l, "w_in")], F32, f"l{l}_mix_in", gather_jobs((l, "mix_in")))
        landed((l, "mix_in"), ex)
        mp = mixer_params(l)
        (ymix, hprev), ex = _mixer_fwd(proj, *mp, f"l{l}_mix_core", gather_jobs((l, "mix_core")))
        landed((l, "mix_core"), ex)
        (x2, ymo), ex = _proj_residual(ymix, full[(l, "w_out")], x1, n2, 1.0, False, f"l{l}_mix_out",
                                       gather_jobs((l, "mix_out")))
        landed((l, "mix_out"), ex)
        s2 = (x1, h2, proj, ymix, ymo, hprev, mp)
        xcur, s3 = ffn("ffn2", x2, n3)
        saved.append((n1, n2, n3, s1, s2, s3))

    dx, stats = _final_loss(xcur, _rows8(g_final), tgt, "final_loss")
    loss = lax.psum(stats[1, 0], ("x", "y", "c"))
    d_g_final = stats[0]

    recv = {}
    waiting = []

    def carried(fn, *a, extra=()):
        items = waiting + list(extra)
        waiting.clear()
        outs, landed_now = fn(*a, jobs=[_ScatterJob(g, _AXIS[n]) for _, n, g in items])
        recv.update(zip([(ll, n) for ll, n, _ in items], landed_now))
        return outs

    dmod, d_gnorm, d_small = [None] * L, [None] * L, [None] * L
    for l in reversed(range(L)):
        n1, n2, n3, s1, s2, s3 = saved[l]

        def ffn_bwd(which, dxo, sv, nrm, hold_last):
            xin, h, gu, y = sv
            tag = f"l{l}_{which}"
            dgu, act, dy, dgate = carried(_proj_residual_bwd, dxo, gu, y, full[(l, f"w_{which}_down")], nrm, 0.5,
                                          tag + "_down_bwd")
            dw_down, _ = _atb(act, dy, BF16, 2816, 1024, tag + "_dw_down")
            dw_gu = carried(_atb, h, dgu, BF16, 1024, 2816, tag + "_dw_gu", extra=[(l, f"w_{which}_down", dw_down)])
            mine = [(l, f"w_{which}_gu", dw_gu)]
            dxi, red = carried(_nt_norm_bwd, dgu, full[(l, f"w_{which}_gu")], xin, nrm, dxo, tag + "_gu_bwd",
                               extra=[] if hold_last else mine)
            if hold_last:
                waiting.extend(mine)
            return dxi, (red[0], red[1], dgate[0]), red[2]

        dx, dm3, dg3 = ffn_bwd("ffn2", dx, s3, n3, True)
        x_in, h2, proj, ymix, ymo, hprev, mp = s2
        (dymix, dy, dgate), _ = _proj_residual_bwd(dx, None, ymo, full[(l, "w_out")], n2, 1.0, f"l{l}_mix_out_bwd")
        dw_out, _ = _atb(ymix, dy, BF16, 1024, 1024, f"l{l}_dw_out")
        dproj, dsm, dsink, dwa, dwx = carried(_mixer_bwd, proj, dymix, hprev, *mp, f"l{l}_mix_core_bwd")
        dw_in = carried(_atb, h2, dproj, BF16, 1024, 2048, f"l{l}_dw_in", extra=[(l, "w_out", dw_out)])
        dx, red = carried(_nt_norm_bwd, dproj, full[(l, "w_in")], x_in, n2, dx, f"l{l}_mix_in_bwd",
                          extra=[(l, "w_in", dw_in)])
        dm2, dg2 = (red[0], red[1], dgate[0]), red[2]
        dx, dm1, dg1 = ffn_bwd("ffn1", dx, s1, n1, l > 0)
        dmod[l] = jnp.stack(list(dm1) + list(dm2) + list(dm3))
        d_gnorm[l] = jnp.stack([dg1, dg2, dg3])
        d_small[l] = (dsink[:, 0], dsm[0:4], dsm[4], _diag_blocks(dwa), dsm[5], _diag_blocks(dwx), dsm[6],
                      dsm[7], dsm[8:11])
    grad_x = dx[None]

    def both(k):
        return jnp.stack([d_small[0][k], d_small[1][k]])
    small_names = ["g_norm", "attn_sinks", "lru_conv_w", "lru_conv_b", "lru_gate_a_w", "lru_gate_a_b",
                   "lru_gate_x_w", "lru_gate_x_b", "lru_lambda", "sc_conv_w", "g_final"]
    small_parts = [jnp.stack(d_gnorm)] + [both(k) for k in range(9)] + [d_g_final]
    dmod_flat = jnp.stack(dmod).reshape(-1)
    bwd_gathered, bwd_sum = _all_gather_small(_pack([dmod_flat] + small_parts), "gather_small_bwd")
    n_mod = dmod_flat.shape[0]
    dmod_all = bwd_gathered.reshape(N_DEV, -1)[:, :n_mod].reshape(N_DEV, L, 9 * D_MODEL)
    bwd_sum = bwd_sum.reshape(-1)
    G = {"b_mod": bwd_sum[:n_mod].reshape(L, 9 * D_MODEL)}
    G.update(zip(small_names, _unpack(bwd_sum[n_mod:], [p.shape for p in small_parts])))
    for n in ("g_norm", "lru_conv_w", "sc_conv_w"):
        wdt = W[n].shape[-1]
        G[n] = lax.dynamic_slice_in_dim(G[n], chip * wdt, wdt, axis=G[n].ndim - 1)

    ncol = w_mod.shape[-1]
    dmod_cols = lax.dynamic_slice_in_dim(dmod_all, chip * ncol, ncol, axis=2)
    zeros8 = jnp.zeros((N_DEV, ncol), F32)
    g_w_mod = jnp.stack([_atb(c_act, jnp.concatenate([dmod_cols[:, l], zeros8], axis=0).astype(BF16), F32,
                              D_MODEL, 768, f"l{l}_dw_mod")[0] for l in range(L)])

    part = [_sum4_layers([recv[(l, n)] for l in range(L)], f"sum_{n}") for n, _ in _BIG]
    other = _swap_with_sibling(part, "swap_sibling")

    out_g, out_d, out_m, out_v = {}, {}, {}, {}
    for (n, _), p, o in zip(_BIG, part, other):
        shp = W[n].shape
        res = _adamw(W[n].reshape(p.shape), [p, o], M1[n].reshape(p.shape), V1[n].reshape(p.shape), f"adamw_{n}")
        out_g[n], out_d[n], out_m[n], out_v[n] = [r.reshape(shp) for r in res]
    res = _adamw(w_mod.reshape(-1, ncol), [g_w_mod.reshape(-1, ncol)], m_w_mod.reshape(-1, ncol),
                 v_w_mod.reshape(-1, ncol), "adamw_w_mod")
    out_g["w_mod"], out_d["w_mod"], out_m["w_mod"], out_v["w_mod"] = [r.reshape(w_mod.shape) for r in res]
    rest = ["b_mod"] + small_names
    shapes = [W[n].shape for n in rest]
    res = _adamw(_pack([W[n] for n in rest]), [_pack([G[n] for n in rest])], _pack([M1[n] for n in rest]),
                 _pack([V1[n] for n in rest]), "adamw_small")
    for dst, r in zip((out_g, out_d, out_m, out_v), res):
        dst.update(zip(rest, _unpack(r.reshape(-1), shapes)))

    return (loss, grad_x, *[out_g[n] for n in names], *[out_d[n] for n in names],
            *[out_m[n] for n in names], *[out_v[n] for n in names])
```

```python
import math

import jax
import jax.numpy as jnp
from jax import lax
from jax.experimental import pallas as pl
from jax.experimental.pallas import tpu as pltpu

F32 = jnp.float32
BF16 = jnp.bfloat16

D_MODEL = 1024
DEPTH = 2
HEAD_DIM = 64
N_Q_HEADS = 8
ATTN_WIDTH = 512
KV_WIDTH = 128
LRU_WIDTH = 256
CONV_WIDTH = 256
IN_PROJ_WIDTH = 2048
BLOCK = 128
D_FF = 2816
EPS = 1e-6
NEG_INF = -1e30
LRU_C = 8.0
N_CHIPS = 4
N_DEV = 8

C_Q, C_KV, C_LX, C_LG, C_SB, C_SC, C_SX = 0, 512, 768, 1024, 1280, 1536, 1792

ADAM_LR = 0.001
ADAM_B1 = 0.9
ADAM_B2 = 0.999
ADAM_EPS = 1e-08
ADAM_WD = 0.01
ADAM_STEP = 10

LANES = 128
SUBLANES = 8
VMEM_LIMIT = 56 * 1024 * 1024
MIX_TILE = 256

MESH = pl.DeviceIdType.MESH


def _cp(*sem):
    return pltpu.CompilerParams(dimension_semantics=sem, vmem_limit_bytes=VMEM_LIMIT)


def _tile(n, pref):
    t = min(n, pref)
    while n % t:
        t //= 2
    return t


MXU_DIM = 256


def _col_chunk(n):
    return max(c for c in range(MXU_DIM, 2816 + 1, MXU_DIM) if n % c == 0)


def _resident(shape):
    return pl.BlockSpec(shape, lambda *_: (0, 0), pipeline_mode=pl.Buffered(1))


def _sigmoid(v):
    return 1.0 / (1.0 + jnp.exp(-v))


def _expm1(v):
    series = v * (1.0 + v * (0.5 + v * (1.0 / 6.0)))
    return jnp.where(v > -0.01, series, jnp.exp(v) - 1.0)


def _softplus_neg(lam):
    e = jnp.exp(-jnp.abs(lam))
    log1p = jnp.where(e < 1e-2, e * (1.0 - e * (0.5 - e * (1.0 / 3.0))), jnp.log(1.0 + e))
    return jnp.maximum(-lam, 0.0) + log1p


_GELU_K = math.sqrt(2.0 / math.pi)
_GELU_C = 0.044715


def _gelu(v):
    t = jnp.tanh(_GELU_K * (v + _GELU_C * v * v * v))
    return 0.5 * v * (1.0 + t), t


def _gelu_grad(v, t):
    return 0.5 * (1.0 + t) + 0.5 * v * (1.0 - t * t) * _GELU_K * (1.0 + 3.0 * _GELU_C * v * v)


def _dot(a, b):
    return jnp.dot(a, b, preferred_element_type=F32)


def _dot_nt(a, b):
    return lax.dot_general(a, b, (((1,), (1,)), ((), ())), preferred_element_type=F32)


def _dot_tn(a, b):
    return lax.dot_general(a, b, (((0,), (0,)), ((), ())), preferred_element_type=F32)


def _window(ref, axis, j, width):
    start = pl.multiple_of(j * width, LANES if axis == 1 else 16)
    if axis == 1:
        return ref.at[:, pl.ds(start, width)]
    return ref.at[pl.ds(start, width), :]


def _chip_peers():
    x, y, c = lax.axis_index("x"), lax.axis_index("y"), lax.axis_index("c")
    return x, y, c, [(1 - x, y), (x, 1 - y), (1 - x, 1 - y)]


class _GatherJob:
    def __init__(self, shard, axis):
        self.src, self.axis, self.width, self.half = shard, axis, shard.shape[axis], shard.shape[0] // 2
        full = tuple(d * N_CHIPS if k == axis else d for k, d in enumerate(shard.shape))
        self.out_shape = jax.ShapeDtypeStruct(full, shard.dtype)

    def _piece(self, ref, j, hf):
        if self.axis == 1:
            return ref.at[pl.ds(pl.multiple_of(hf * self.half, 16), self.half),
                          pl.ds(pl.multiple_of(j * self.width, LANES), self.width)]
        return ref.at[pl.ds(pl.multiple_of(j * self.width + hf * self.half, 16), self.half), :]

    def _copies(self, src, dst, send, recv, loc, t):
        x, y, c, chips = _chip_peers()
        j = 2 * x + y
        owners = [2 * px + py for px, py in chips]
        local = pltpu.make_async_copy(src, _window(dst, self.axis, j, self.width), loc.at[t])
        mine = src.at[pl.ds(pl.multiple_of(c * self.half, 16), self.half), :]

        def ici(k, owner):
            return pltpu.make_async_remote_copy(
                src_ref=mine, dst_ref=self._piece(dst, owner, c), send_sem=send.at[JOB_SEMS * t + k],
                recv_sem=recv.at[JOB_SEMS * t + k], device_id=(*chips[k], c), device_id_type=MESH)

        def relay(k, hf):
            piece = self._piece(dst, owners[k], hf)
            return pltpu.make_async_remote_copy(
                src_ref=piece, dst_ref=piece, send_sem=send.at[JOB_SEMS * t + 4 + k],
                recv_sem=recv.at[JOB_SEMS * t + 4 + k],
                device_id=(x, y, 1 - c), device_id_type=MESH)

        return (local, [ici(k, j) for k in range(3)], [ici(k, owners[k]) for k in range(3)],
                [relay(k, c) for k in range(3)], [relay(k, 1 - c) for k in range(3)])

    def start(self, *a):
        local, ici_out, _, _, _ = self._copies(*a)
        local.start()
        for cp in ici_out:
            cp.start()

    def relay(self, *a):
        _, _, ici_in, relay_out, _ = self._copies(*a)
        for arrived, onward in zip(ici_in, relay_out):
            arrived.wait_recv()
            onward.start()

    def finish(self, *a):
        local, ici_out, _, relay_out, relay_in = self._copies(*a)
        for cp in relay_in:
            cp.wait_recv()
        for cp in ici_out + relay_out:
            cp.wait_send()
        local.wait()


class _ScatterJob:
    def __init__(self, full, axis):
        self.src, self.axis, self.width = full, axis, full.shape[axis] // N_CHIPS
        shard = tuple(self.width if k == axis else d for k, d in enumerate(full.shape))
        self.out_shape = jax.ShapeDtypeStruct((N_CHIPS,) + shard, full.dtype)

    def _copies(self, src, dst, send, recv, loc, t):
        x, y, c, chips = _chip_peers()
        local = pltpu.make_async_copy(_window(src, self.axis, 2 * x + y, self.width), dst.at[3], loc.at[t])
        sends = [pltpu.make_async_remote_copy(
            src_ref=_window(src, self.axis, 2 * px + py, self.width), dst_ref=dst.at[k],
            send_sem=send.at[JOB_SEMS * t + k], recv_sem=recv.at[JOB_SEMS * t + k], device_id=(px, py, c),
            device_id_type=MESH) for k, (px, py) in enumerate(chips)]
        return local, sends

    def start(self, *a):
        local, sends = self._copies(*a)
        local.start()
        for cp in sends:
            cp.start()

    def relay(self, *a):
        pass

    def finish(self, *a):
        local, sends = self._copies(*a)
        for cp in sends:
            cp.wait_recv()
        for cp in sends:
            cp.wait_send()
        local.wait()


class _SiblingJob:
    def __init__(self, arr):
        self.src, self.out_shape = arr, jax.ShapeDtypeStruct(arr.shape, arr.dtype)

    def _copy(self, src, dst, send, recv, loc, t):
        x, y, c = lax.axis_index("x"), lax.axis_index("y"), lax.axis_index("c")
        return pltpu.make_async_remote_copy(
            src_ref=src, dst_ref=dst, send_sem=send.at[JOB_SEMS * t], recv_sem=recv.at[JOB_SEMS * t],
            device_id=(x, y, 1 - c), device_id_type=MESH)

    def start(self, *a):
        self._copy(*a).start()

    def relay(self, *a):
        pass

    def finish(self, *a):
        self._copy(*a).wait()


JOB_SEMS = 8


def _run_jobs(phase, jobs, srcs, dsts, sems):
    for t, job in enumerate(jobs):
        getattr(job, phase)(srcs[t], dsts[t], *sems, t)


def _job_scratch(n):
    return [pltpu.SemaphoreType.DMA((JOB_SEMS * n,)), pltpu.SemaphoreType.DMA((JOB_SEMS * n,)),
            pltpu.SemaphoreType.DMA((n,))]


def _pcall(body, *, name, grid, in_specs, out_specs, out_shape, sem, args, scratch_shapes=(), jobs=()):
    in_specs, out_specs, out_shape = list(in_specs), list(out_specs), list(out_shape)
    scratch_shapes = list(scratch_shapes)
    if not jobs:
        res = pl.pallas_call(body, name=name, grid=grid, in_specs=in_specs, out_specs=out_specs, out_shape=out_shape,
                             scratch_shapes=scratch_shapes, compiler_params=_cp(*sem))(*args)
        return list(res), []
    n_in, n_out, n_scr, nj = len(args), len(out_shape), len(scratch_shapes), len(jobs)
    n_steps = math.prod(grid)
    relay_step = (3 * n_steps) // 4
    relay_early = 0 < relay_step < n_steps - 1

    def wrapped(*refs):
        ins, refs = refs[:n_in], refs[n_in:]
        jin, refs = refs[:nj], refs[nj:]
        outs, refs = refs[:n_out], refs[n_out:]
        jout, refs = refs[:nj], refs[nj:]
        scr, sems = refs[:n_scr], refs[n_scr:]
        step = pl.program_id(0)
        for d in range(1, len(grid)):
            step = step * grid[d] + pl.program_id(d)

        @pl.when(step == 0)
        def _():
            _run_jobs("start", jobs, jin, jout, sems)

        if relay_early:
            @pl.when(step == relay_step)
            def _():
                _run_jobs("relay", jobs, jin, jout, sems)
        body(*ins, *outs, *scr)

        @pl.when(step == n_steps - 1)
        def _():
            if not relay_early:
                _run_jobs("relay", jobs, jin, jout, sems)
            _run_jobs("finish", jobs, jin, jout, sems)

    hbm = pl.BlockSpec(memory_space=pltpu.HBM)
    res = pl.pallas_call(
        wrapped, name=name, grid=grid, in_specs=in_specs + [hbm] * nj, out_specs=out_specs + [hbm] * nj,
        out_shape=out_shape + [job.out_shape for job in jobs], scratch_shapes=scratch_shapes + _job_scratch(nj),
        compiler_params=_cp(*sem))(*args, *[job.src for job in jobs])
    return list(res[:n_out]), list(res[n_out:])


def _comm_only(jobs, name):
    nj = len(jobs)

    def body(*refs):
        srcs, dsts, sems = refs[:nj], refs[nj:2 * nj], refs[2 * nj:]
        for phase in ("start", "relay", "finish"):
            _run_jobs(phase, jobs, srcs, dsts, sems)

    hbm = pl.BlockSpec(memory_space=pltpu.HBM)
    return list(pl.pallas_call(
        body, name=name, in_specs=[hbm] * nj, out_specs=[hbm] * nj, out_shape=[job.out_shape for job in jobs],
        scratch_shapes=_job_scratch(nj))(*[job.src for job in jobs]))


def _norm_matmul(x, nrm, w, out_dtype, name, jobs=()):
    S, Dm = x.shape
    N = w.shape[1]
    tm = _tile(S, 512)
    cw = _col_chunk(N)

    def body(x_ref, nrm_ref, w_ref, o_ref, h_ref):
        xv = x_ref[...]
        rstd = lax.rsqrt(jnp.mean(xv * xv, axis=-1, keepdims=True) + EPS)
        hn = (xv * rstd) * nrm_ref[0:1, :]
        hb = (hn * (1.0 + nrm_ref[2:3, :]) + nrm_ref[1:2, :]).astype(BF16)
        h_ref[...] = hb
        for n in range(N // cw):
            o_ref[:, n * cw:(n + 1) * cw] = _dot(hb, w_ref[:, n * cw:(n + 1) * cw]).astype(o_ref.dtype)

    return _pcall(
        body, name=name, grid=(S // tm,),
        in_specs=[pl.BlockSpec((tm, Dm), lambda i: (i, 0)),
                  pl.BlockSpec((8, Dm), lambda i: (0, 0)),
                  _resident((Dm, N))],
        out_specs=[pl.BlockSpec((tm, N), lambda i: (i, 0)),
                   pl.BlockSpec((tm, Dm), lambda i: (i, 0))],
        out_shape=[jax.ShapeDtypeStruct((S, N), out_dtype), jax.ShapeDtypeStruct((S, Dm), BF16)],
        sem=("arbitrary",), args=(x, nrm, w), jobs=jobs)


def _proj_residual(a, w, x, nrm, coef, swiglu, name, jobs=()):
    S, Ka = a.shape
    K, Dm = w.shape
    tm = _tile(S, 256)

    def body(a_ref, w_ref, x_ref, nrm_ref, o_ref, y_ref):
        if swiglu:
            g = a_ref[:, :K].astype(F32)
            u = a_ref[:, K:].astype(F32)
            act = (g * _sigmoid(g) * u).astype(BF16)
        else:
            act = a_ref[...]
        y = _dot(act, w_ref[...])
        o_ref[...] = x_ref[...] + (coef * nrm_ref[3:4, :]) * y
        y_ref[...] = y.astype(BF16)

    return _pcall(
        body, name=name, grid=(S // tm,),
        in_specs=[pl.BlockSpec((tm, Ka), lambda i: (i, 0)),
                  _resident((K, Dm)),
                  pl.BlockSpec((tm, Dm), lambda i: (i, 0)),
                  pl.BlockSpec((8, Dm), lambda i: (0, 0))],
        out_specs=[pl.BlockSpec((tm, Dm), lambda i: (i, 0)),
                   pl.BlockSpec((tm, Dm), lambda i: (i, 0))],
        out_shape=[jax.ShapeDtypeStruct((S, Dm), F32), jax.ShapeDtypeStruct((S, Dm), BF16)],
        sem=("arbitrary",), args=(a, w, x, nrm), jobs=jobs)


def _proj_residual_bwd(dxo, gu, y, w, nrm, coef, name, jobs=()):
    swiglu = gu is not None
    S, Dm = dxo.shape
    K = w.shape[0]
    Ka = 2 * K if swiglu else K
    tm = _tile(S, 256)

    def body(*refs):
        if swiglu:
            dxo_ref, y_ref, w_ref, nrm_ref, a_ref, da_ref, act_ref, dy_ref, dgate_ref = refs
        else:
            dxo_ref, y_ref, w_ref, nrm_ref, da_ref, dy_ref, dgate_ref = refs
        dxo_v = dxo_ref[...]
        dyb = ((coef * nrm_ref[3:4, :]) * dxo_v).astype(BF16)
        dy_ref[...] = dyb

        @pl.when(pl.program_id(0) == 0)
        def _():
            dgate_ref[...] = jnp.zeros_like(dgate_ref)
        dgate_ref[0:1, :] += jnp.sum(coef * y_ref[...].astype(F32) * dxo_v, axis=0, keepdims=True)

        dact = _dot_nt(dyb, w_ref[...])
        if swiglu:
            g = a_ref[:, :K].astype(F32)
            u = a_ref[:, K:].astype(F32)
            s = _sigmoid(g)
            si = g * s
            da_ref[:, :K] = (dact * u * (s * (1.0 + g * (1.0 - s)))).astype(BF16)
            da_ref[:, K:] = (dact * si).astype(BF16)
            act_ref[...] = (si * u).astype(BF16)
        else:
            da_ref[...] = dact

    row = lambda i: (i, 0)
    fix = lambda i: (0, 0)
    in_specs = [pl.BlockSpec((tm, Dm), row), pl.BlockSpec((tm, Dm), row),
                _resident((K, Dm)), pl.BlockSpec((8, Dm), fix)]
    args = [dxo, y, w, nrm]
    if swiglu:
        in_specs.append(pl.BlockSpec((tm, Ka), row))
        args.append(gu)
        out_specs = [pl.BlockSpec((tm, Ka), row), pl.BlockSpec((tm, K), row),
                     pl.BlockSpec((tm, Dm), row), pl.BlockSpec((8, Dm), fix)]
        out_shape = [jax.ShapeDtypeStruct((S, Ka), BF16), jax.ShapeDtypeStruct((S, K), BF16),
                     jax.ShapeDtypeStruct((S, Dm), BF16), jax.ShapeDtypeStruct((8, Dm), F32)]
    else:
        out_specs = [pl.BlockSpec((tm, Ka), row), pl.BlockSpec((tm, Dm), row), pl.BlockSpec((8, Dm), fix)]
        out_shape = [jax.ShapeDtypeStruct((S, Ka), F32), jax.ShapeDtypeStruct((S, Dm), BF16),
                     jax.ShapeDtypeStruct((8, Dm), F32)]
    return _pcall(body, name=name, grid=(S // tm,), in_specs=in_specs, out_specs=out_specs, out_shape=out_shape,
                  sem=("arbitrary",), args=args, jobs=jobs)


def _atb(a, b, out_dtype, bm, bn, name, jobs=()):
    S, M = a.shape
    N = b.shape[1]
    bk = _tile(S, 1024)
    nk = S // bk

    def body(a_ref, b_ref, o_ref, acc):
        k = pl.program_id(2)

        @pl.when(k == 0)
        def _():
            acc[...] = jnp.zeros_like(acc)
        acc[...] += _dot_tn(a_ref[...], b_ref[...])

        @pl.when(k == nk - 1)
        def _():
            o_ref[...] = acc[...].astype(o_ref.dtype)

    (out,), extra = _pcall(
        body, name=name, grid=(M // bm, N // bn, nk),
        in_specs=[pl.BlockSpec((bk, bm), lambda m, n, k: (k, m)),
                  pl.BlockSpec((bk, bn), lambda m, n, k: (k, n))],
        out_specs=[pl.BlockSpec((bm, bn), lambda m, n, k: (m, n))],
        out_shape=[jax.ShapeDtypeStruct((M, N), out_dtype)],
        scratch_shapes=[pltpu.VMEM((bm, bn), F32)],
        sem=("arbitrary", "arbitrary", "arbitrary"), args=(a, b), jobs=jobs)
    return out, extra


def _nt_norm_bwd(dout, w, x, nrm, dxo, name, jobs=()):
    S, N = dout.shape
    Dm = w.shape[0]
    tm = _tile(S, 256)

    def body(do_ref, w_ref, x_ref, nrm_ref, dxo_ref, dx_ref, red_ref):
        @pl.when(pl.program_id(0) == 0)
        def _():
            red_ref[...] = jnp.zeros_like(red_ref)
        dh = _dot_nt(do_ref[...], w_ref[...])
        xv = x_ref[...]
        rstd = lax.rsqrt(jnp.mean(xv * xv, axis=-1, keepdims=True) + EPS)
        xn = xv * rstd
        gain = nrm_ref[0:1, :]
        hn = xn * gain
        dhn = dh * (1.0 + nrm_ref[2:3, :])
        red_ref[0:1, :] += jnp.sum(dh, axis=0, keepdims=True)
        red_ref[1:2, :] += jnp.sum(dh * hn, axis=0, keepdims=True)
        red_ref[2:3, :] += jnp.sum(dhn * xn, axis=0, keepdims=True)
        dxn = dhn * gain
        dx = rstd * (dxn - xn * jnp.mean(dxn * xn, axis=-1, keepdims=True))
        dx_ref[...] = dxo_ref[...] + dx

    return _pcall(
        body, name=name, grid=(S // tm,),
        in_specs=[pl.BlockSpec((tm, N), lambda i: (i, 0)),
                  _resident((Dm, N)),
                  pl.BlockSpec((tm, Dm), lambda i: (i, 0)),
                  pl.BlockSpec((8, Dm), lambda i: (0, 0)),
                  pl.BlockSpec((tm, Dm), lambda i: (i, 0))],
        out_specs=[pl.BlockSpec((tm, Dm), lambda i: (i, 0)),
                   pl.BlockSpec((8, Dm), lambda i: (0, 0))],
        out_shape=[jax.ShapeDtypeStruct((S, Dm), F32), jax.ShapeDtypeStruct((8, Dm), F32)],
        sem=("arbitrary",), args=(dout, w, x, nrm, dxo), jobs=jobs)


def _final_loss(x, gf, tgt, name):
    S, Dm = x.shape
    tm = _tile(S, 512)

    def body(x_ref, g_ref, t_ref, dx_ref, st_ref):
        @pl.when(pl.program_id(0) == 0)
        def _():
            st_ref[...] = jnp.zeros_like(st_ref)
        xv = x_ref[...]
        rstd = lax.rsqrt(jnp.mean(xv * xv, axis=-1, keepdims=True) + EPS)
        xn = xv * rstd
        gain = g_ref[0:1, :]
        err = xn * gain - t_ref[...]
        st_ref[1:2, :] += jnp.full((1, Dm), 0.5 / Dm, F32) * jnp.sum(err * err)
        dy = err * (1.0 / Dm)
        st_ref[0:1, :] += jnp.sum(dy * xn, axis=0, keepdims=True)
        dxn = dy * gain
        dx_ref[...] = rstd * (dxn - xn * jnp.mean(dxn * xn, axis=-1, keepdims=True))

    return pl.pallas_call(
        body, name=name, grid=(S // tm,),
        in_specs=[pl.BlockSpec((tm, Dm), lambda i: (i, 0)),
                  pl.BlockSpec((8, Dm), lambda i: (0, 0)),
                  pl.BlockSpec((tm, Dm), lambda i: (i, 0))],
        out_specs=[pl.BlockSpec((tm, Dm), lambda i: (i, 0)),
                   pl.BlockSpec((8, Dm), lambda i: (0, 0))],
        out_shape=[jax.ShapeDtypeStruct((S, Dm), F32), jax.ShapeDtypeStruct((8, Dm), F32)],
        compiler_params=_cp("arbitrary"),
    )(x, gf, tgt)


def _alibi_slope(h):
    return float(2.0 ** (-8.0 * (h + 1) / N_Q_HEADS))


def _head_planes(pair_cols):
    lane = lax.broadcasted_iota(jnp.int32, pair_cols.shape, 1)
    low = lane < HEAD_DIM
    h0_lo = jnp.where(low, pair_cols, 0.0)
    h1_hi = jnp.where(low, 0.0, pair_cols)
    h0_hi = pltpu.roll(h0_lo, HEAD_DIM, 1)
    h1_lo = pltpu.roll(h1_hi, HEAD_DIM, 1)
    return ((h0_lo.astype(BF16), h0_hi.astype(BF16)), (h1_lo.astype(BF16), h1_hi.astype(BF16)))


def _to_plane(v, e, g):
    lane = lax.broadcasted_iota(jnp.int32, v.shape, 1)
    keep = (lane < HEAD_DIM) if e == 0 else (lane >= HEAD_DIM)
    v = jnp.where(keep, v, 0.0)
    return v if e == g else pltpu.roll(v, HEAD_DIM, 1)


def _band_geometry(first_block):
    qi = lax.broadcasted_iota(jnp.int32, (BLOCK, BLOCK), 0)
    kj = lax.broadcasted_iota(jnp.int32, (BLOCK, BLOCK), 1)
    own = kj <= qi
    dist = jnp.where(own, qi - kj, qi + BLOCK - kj).astype(F32)
    valid = kj <= qi + BLOCK * (1 - first_block)
    return own, dist, valid


def _fold(band, own):
    return jnp.where(own, band[:, BLOCK:], band[:, :BLOCK])


def _unfold(v, own):
    return jnp.concatenate([jnp.where(own, 0.0, v), jnp.where(own, v, 0.0)], axis=1)


def _softmax_band(s, h, geometry, sink):
    own, dist, valid = geometry
    s = jnp.where(valid, s - _alibi_slope(h) * dist, NEG_INF)
    m = jnp.maximum(jnp.max(s, axis=-1, keepdims=True), sink)
    p = jnp.exp(s - m)
    e_sink = jnp.exp(sink - m)
    inv = 1.0 / (jnp.sum(p, axis=-1, keepdims=True) + e_sink)
    return p * inv, e_sink * inv


def _past(cur, prev, s, row):
    return jnp.where(row < s, pltpu.roll(prev, s, 0), pltpu.roll(cur, s, 0))


def _future(cur, nxt, s, row):
    T = cur.shape[0]
    return jnp.where(row >= T - s, pltpu.roll(nxt, T - s, 0), pltpu.roll(cur, T - s, 0))


def _edge_row(v, last):
    T = v.shape[0]
    r8 = lax.broadcasted_iota(jnp.int32, (SUBLANES, v.shape[1]), 0)
    blk = v[T - SUBLANES:, :] if last else v[:SUBLANES, :]
    return jnp.sum(jnp.where(r8 == (SUBLANES - 1 if last else 0), blk, 0.0), axis=0, keepdims=True)


def _lru_gates(lx, lx_prev, small_ref, wa_ref, wx_ref, row, t0):
    xc = (small_ref[4:5, :] + small_ref[3:4, :] * lx + small_ref[2:3, :] * _past(lx, lx_prev, 1, row)
          + small_ref[1:2, :] * _past(lx, lx_prev, 2, row) + small_ref[0:1, :] * _past(lx, lx_prev, 3, row))
    xcb = xc.astype(BF16)
    r = _sigmoid(_dot(xcb, wa_ref[...]) + small_ref[5:6, :])
    ig = _sigmoid(_dot(xcb, wx_ref[...]) + small_ref[6:7, :])
    sp = _softplus_neg(small_ref[7:8, :])
    la = (-LRU_C) * r * sp
    a = jnp.exp(la)
    first = (row + t0) == 0
    mult = jnp.where(first, 1.0, jnp.sqrt(-_expm1(2.0 * la)))
    return xc, xcb, r, ig, sp, a, mult, first


def _mixer_fwd(proj, sinks, small, wa, wx, name, jobs=()):
    S = proj.shape[0]
    T = MIX_TILE
    nT = S // T
    nb = T // BLOCK

    def body(proj_ref, sink_ref, small_ref, wa_ref, wx_ref, y_ref, hp_ref, kvp, lxp, zp, hcar):
        i = pl.program_id(0)

        @pl.when(i == 0)
        def _():
            kvp[...] = jnp.zeros_like(kvp)
            lxp[...] = jnp.zeros_like(lxp)
            zp[...] = jnp.zeros_like(zp)
            hcar[...] = jnp.zeros_like(hcar)

        row = lax.broadcasted_iota(jnp.int32, (T, LRU_WIDTH), 0)

        kv = proj_ref[:, C_KV:C_KV + 2 * KV_WIDTH]
        ext = jnp.concatenate([kvp[...], kv], axis=0)
        kx = _head_planes(ext[:, :KV_WIDTH])
        vx = _head_planes(ext[:, KV_WIDTH:])
        first_tile = jnp.where(i == 0, 1, 0)
        units = [(b, pair, e) for b in range(nb) for pair in range(N_Q_HEADS // 2) for e in range(2)]
        geometry = [_band_geometry(first_tile if b == 0 else 0) for b in range(nb)]
        keys = [slice(b * BLOCK, (b + 2) * BLOCK) for b in range(nb)]
        qp = {(b, pair): (proj_ref[b * BLOCK:(b + 1) * BLOCK, pair * LANES:(pair + 1) * LANES] * 0.125).astype(BF16)
              for b in range(nb) for pair in range(N_Q_HEADS // 2)}
        scores = [_fold(_dot_nt(qp[(b, pair)], kx[pair // 2][e][keys[b]]), geometry[b][0]) for b, pair, e in units]
        probs = [_unfold(_softmax_band(s, 2 * pair + e, geometry[b], sink_ref[2 * pair + e])[0],
                         geometry[b][0]).astype(BF16) for s, (b, pair, e) in zip(scores, units)]
        outs = [_dot(p, vx[pair // 2][e][keys[b]]) for p, (b, pair, e) in zip(probs, units)]
        for u in range(0, len(units), 2):
            b, pair, _ = units[u]
            y_ref[b * BLOCK:(b + 1) * BLOCK, pair * LANES:(pair + 1) * LANES] = (outs[u] + outs[u + 1]).astype(BF16)
        kvp[...] = kv[T - BLOCK:, :]

        lx = proj_ref[:, C_LX:C_LX + LRU_WIDTH]
        xc, _, _, ig, _, a, mult, _ = _lru_gates(lx, lxp[...], small_ref, wa_ref, wx_ref, row, i * T)
        lxp[...] = lx
        aa = a
        bb = mult * (ig * xc)
        s = 1
        while s < T:
            a_sh = jnp.where(row >= s, pltpu.roll(aa, s, 0), 1.0)
            b_sh = jnp.where(row >= s, pltpu.roll(bb, s, 0), 0.0)
            bb = aa * b_sh + bb
            aa = aa * a_sh
            s *= 2
        hc = hcar[0:1, :]
        hh = bb + aa * hc
        hp_ref[...] = jnp.where(row < 1, hc, pltpu.roll(hh, 1, 0))
        hcar[...] = jnp.broadcast_to(_edge_row(hh, True), hcar.shape)
        gl, _ = _gelu(proj_ref[:, C_LG:C_LG + LRU_WIDTH])
        y_ref[:, ATTN_WIDTH:ATTN_WIDTH + LRU_WIDTH] = (gl * hh).astype(BF16)

        z = proj_ref[:, C_SC:C_SC + CONV_WIDTH] * proj_ref[:, C_SX:C_SX + CONV_WIDTH]
        c3 = (small_ref[10:11, :] * z + small_ref[9:10, :] * _past(z, zp[...], 1, row)
              + small_ref[8:9, :] * _past(z, zp[...], 2, row))
        zp[...] = z
        y_ref[:, ATTN_WIDTH + LRU_WIDTH:] = (proj_ref[:, C_SB:C_SB + CONV_WIDTH] * c3).astype(BF16)

    fix = lambda i: (0, 0)
    return _pcall(
        body, name=name, grid=(nT,),
        in_specs=[pl.BlockSpec((T, IN_PROJ_WIDTH), lambda i: (i, 0)),
                  pl.BlockSpec(memory_space=pltpu.SMEM),
                  pl.BlockSpec((16, LRU_WIDTH), fix),
                  pl.BlockSpec((LRU_WIDTH, LRU_WIDTH), fix),
                  pl.BlockSpec((LRU_WIDTH, LRU_WIDTH), fix)],
        out_specs=[pl.BlockSpec((T, D_MODEL), lambda i: (i, 0)),
                   pl.BlockSpec((T, LRU_WIDTH), lambda i: (i, 0))],
        out_shape=[jax.ShapeDtypeStruct((S, D_MODEL), BF16), jax.ShapeDtypeStruct((S, LRU_WIDTH), F32)],
        scratch_shapes=[pltpu.VMEM((BLOCK, 2 * KV_WIDTH), F32), pltpu.VMEM((T, LRU_WIDTH), F32),
                        pltpu.VMEM((T, CONV_WIDTH), F32), pltpu.VMEM((SUBLANES, LRU_WIDTH), F32)],
        sem=("arbitrary",), args=(proj, sinks, small, wa, wx), jobs=jobs)


def _mixer_bwd(proj, dymix, hprev, sinks, small, wa, wx, name, jobs=()):
    S = proj.shape[0]
    T = MIX_TILE
    nT = S // T
    nb = T // BLOCK
    bpt = T // BLOCK

    def body(proj_ref, kvprev_ref, lxprev_ref, scprev_ref, sxprev_ref, dy_ref, hp_ref, sink_ref, small_ref,
             wa_ref, wx_ref, dp_ref, dsm_ref, dsink_ref, dwa_ref, dwx_ref,
             dk_s, dv_s, dkv_c, dxc_n, dc3_n, p_c):
        i = pl.program_id(0)
        ti = nT - 1 - i
        has_prev = jnp.where(ti == 0, 0.0, 1.0)

        @pl.when(i == 0)
        def _():
            for r in (dkv_c, dxc_n, dc3_n, p_c, dsm_ref, dsink_ref, dwa_ref, dwx_ref):
                r[...] = jnp.zeros_like(r)

        row = lax.broadcasted_iota(jnp.int32, (T, LRU_WIDTH), 0)

        kv = proj_ref[:, C_KV:C_KV + 2 * KV_WIDTH]
        ext = jnp.concatenate([kvprev_ref[...] * has_prev, kv], axis=0)
        kx = _head_planes(ext[:, :KV_WIDTH])
        vx = _head_planes(ext[:, KV_WIDTH:])
        dk_s[...] = jnp.zeros_like(dk_s)
        dv_s[...] = jnp.zeros_like(dv_s)
        dk_s[T:, :] = dkv_c[:, :KV_WIDTH]
        dv_s[T:, :] = dkv_c[:, KV_WIDTH:]
        first_tile = jnp.where(ti == 0, 1, 0)
        units = [(b, pair, e) for b in range(nb) for pair in range(N_Q_HEADS // 2) for e in range(2)]
        geometry = [_band_geometry(first_tile if b == 0 else 0) for b in range(nb)]
        keys = [slice(b * BLOCK, (b + 2) * BLOCK) for b in range(nb)]
        tile = {(b, pair): (slice(b * BLOCK, (b + 1) * BLOCK), slice(pair * LANES, (pair + 1) * LANES))
                for b in range(nb) for pair in range(N_Q_HEADS // 2)}
        qp = {k: (proj_ref[rc] * 0.125).astype(BF16) for k, rc in tile.items()}
        dob = {k: dy_ref[rc].astype(BF16) for k, rc in tile.items()}
        scores = [_fold(_dot_nt(qp[(b, pair)], kx[pair // 2][e][keys[b]]), geometry[b][0]) for b, pair, e in units]
        dprob = [_fold(_dot_nt(dob[(b, pair)], vx[pair // 2][e][keys[b]]), geometry[b][0]) for b, pair, e in units]
        pn_wide, ds_wide = [], []
        for s, dpm, (b, pair, e) in zip(scores, dprob, units):
            h = 2 * pair + e
            own = geometry[b][0]
            pn, psink = _softmax_band(s, h, geometry[b], sink_ref[h])
            dsum = jnp.sum(pn * dpm, axis=-1, keepdims=True)
            dsink_ref[h:h + 1, :] += jnp.full((1, LANES), -1.0, F32) * jnp.sum(psink * dsum)
            pn_wide.append(_unfold(pn, own).astype(BF16))
            ds_wide.append(_unfold(pn * (dpm - dsum), own).astype(BF16))
        dq = {}
        for pw, ds, (b, pair, e) in zip(pn_wide, ds_wide, units):
            g = pair // 2
            dv_s[keys[b], :] += _to_plane(_dot_tn(pw, dob[(b, pair)]), e, g)
            dk_s[keys[b], :] += _to_plane(_dot_tn(ds, qp[(b, pair)]), e, g)
            part = _dot(ds, kx[g][e][keys[b]])
            dq[(b, pair)] = part if e == 0 else dq[(b, pair)] + part
        for k, rc in tile.items():
            dp_ref[rc] = (0.125 * dq[k]).astype(BF16)
        dp_ref[:, C_KV:C_KV + KV_WIDTH] = dk_s[BLOCK:, :].astype(BF16)
        dp_ref[:, C_KV + KV_WIDTH:C_KV + 2 * KV_WIDTH] = dv_s[BLOCK:, :].astype(BF16)
        dkv_c[:, :KV_WIDTH] = dk_s[:BLOCK, :]
        dkv_c[:, KV_WIDTH:] = dv_s[:BLOCK, :]

        lx = proj_ref[:, C_LX:C_LX + LRU_WIDTH]
        lxprev = lxprev_ref[...] * has_prev
        xc, xcb, r, ig, sp, a, mult, first = _lru_gates(lx, lxprev, small_ref, wa_ref, wx_ref, row, ti * T)
        hp = hp_ref[...]
        hh = a * hp + mult * (ig * xc)
        lg = proj_ref[:, C_LG:C_LG + LRU_WIDTH]
        gl, th = _gelu(lg)
        dyl = dy_ref[:, ATTN_WIDTH:ATTN_WIDTH + LRU_WIDTH]
        dp_ref[:, C_LG:C_LG + LRU_WIDTH] = (dyl * hh * _gelu_grad(lg, th)).astype(BF16)
        aa = jnp.where(row < T - 1, pltpu.roll(a, T - 1, 0), 1.0)
        bb = dyl * gl
        s = 1
        while s < T:
            a_sh = jnp.where(row < T - s, pltpu.roll(aa, T - s, 0), 1.0)
            b_sh = jnp.where(row < T - s, pltpu.roll(bb, T - s, 0), 0.0)
            bb = bb + aa * b_sh
            aa = aa * a_sh
            s *= 2
        G = bb + aa * p_c[0:1, :]
        p_c[...] = jnp.broadcast_to(_edge_row(a * G, False), p_c.shape)
        da = G * hp
        dmult = G * (ig * xc)
        dig = G * mult * xc
        dxc = G * mult * ig
        dla = da * a + dmult * jnp.where(first, 0.0, -(a * a) / mult)
        dr = dla * ((-LRU_C) * sp)
        lam = small_ref[7:8, :]
        dsm_ref[7:8, :] += jnp.sum(dla * ((-LRU_C) * r), axis=0, keepdims=True) * (-_sigmoid(-lam))
        dpa = dr * r * (1.0 - r)
        dpx = dig * ig * (1.0 - ig)
        dsm_ref[5:6, :] += jnp.sum(dpa, axis=0, keepdims=True)
        dsm_ref[6:7, :] += jnp.sum(dpx, axis=0, keepdims=True)
        dpab = dpa.astype(BF16)
        dpxb = dpx.astype(BF16)
        dwa_ref[...] += _dot_tn(xcb, dpab)
        dwx_ref[...] += _dot_tn(xcb, dpxb)
        dxc = dxc + _dot_nt(dpab, wa_ref[...]) + _dot_nt(dpxb, wx_ref[...])
        dsm_ref[4:5, :] += jnp.sum(dxc, axis=0, keepdims=True)
        dsm_ref[3:4, :] += jnp.sum(dxc * lx, axis=0, keepdims=True)
        for k in range(3):
            dsm_ref[k:k + 1, :] += jnp.sum(dxc * _past(lx, lxprev, 3 - k, row), axis=0, keepdims=True)
        nxt = dxc_n[...]
        dlx = (small_ref[3:4, :] * dxc + small_ref[2:3, :] * _future(dxc, nxt, 1, row)
               + small_ref[1:2, :] * _future(dxc, nxt, 2, row) + small_ref[0:1, :] * _future(dxc, nxt, 3, row))
        dxc_n[...] = dxc
        dp_ref[:, C_LX:C_LX + LRU_WIDTH] = dlx.astype(BF16)

        sc = proj_ref[:, C_SC:C_SC + CONV_WIDTH]
        sx = proj_ref[:, C_SX:C_SX + CONV_WIDTH]
        sb = proj_ref[:, C_SB:C_SB + CONV_WIDTH]
        z = sc * sx
        zprev = (scprev_ref[...] * sxprev_ref[...]) * has_prev
        z1 = _past(z, zprev, 1, row)
        z2 = _past(z, zprev, 2, row)
        c3 = small_ref[10:11, :] * z + small_ref[9:10, :] * z1 + small_ref[8:9, :] * z2
        dys = dy_ref[:, ATTN_WIDTH + LRU_WIDTH:]
        dp_ref[:, C_SB:C_SB + CONV_WIDTH] = (dys * c3).astype(BF16)
        dc3 = dys * sb
        dsm_ref[10:11, :] += jnp.sum(dc3 * z, axis=0, keepdims=True)
        dsm_ref[9:10, :] += jnp.sum(dc3 * z1, axis=0, keepdims=True)
        dsm_ref[8:9, :] += jnp.sum(dc3 * z2, axis=0, keepdims=True)
        nxt3 = dc3_n[...]
        dz = (small_ref[10:11, :] * dc3 + small_ref[9:10, :] * _future(dc3, nxt3, 1, row)
              + small_ref[8:9, :] * _future(dc3, nxt3, 2, row))
        dc3_n[...] = dc3
        dp_ref[:, C_SC:C_SC + CONV_WIDTH] = (dz * sx).astype(BF16)
        dp_ref[:, C_SX:C_SX + CONV_WIDTH] = (dz * sc).astype(BF16)

    fix = lambda i: (0, 0)
    cur = lambda i: (nT - 1 - i, 0)
    prev_cols = lambda cb: (lambda i: (jnp.maximum(nT - 2 - i, 0), cb))
    return _pcall(
        body, name=name, grid=(nT,),
        in_specs=[pl.BlockSpec((T, IN_PROJ_WIDTH), cur),
                  pl.BlockSpec((BLOCK, 2 * KV_WIDTH),
                               lambda i: (jnp.maximum((nT - 1 - i) * bpt - 1, 0), C_KV // (2 * KV_WIDTH))),
                  pl.BlockSpec((T, LRU_WIDTH), prev_cols(C_LX // LRU_WIDTH)),
                  pl.BlockSpec((T, CONV_WIDTH), prev_cols(C_SC // CONV_WIDTH)),
                  pl.BlockSpec((T, CONV_WIDTH), prev_cols(C_SX // CONV_WIDTH)),
                  pl.BlockSpec((T, D_MODEL), cur),
                  pl.BlockSpec((T, LRU_WIDTH), cur),
                  pl.BlockSpec(memory_space=pltpu.SMEM),
                  pl.BlockSpec((16, LRU_WIDTH), fix),
                  pl.BlockSpec((LRU_WIDTH, LRU_WIDTH), fix),
                  pl.BlockSpec((LRU_WIDTH, LRU_WIDTH), fix)],
        out_specs=[pl.BlockSpec((T, IN_PROJ_WIDTH), cur),
                   pl.BlockSpec((16, LRU_WIDTH), fix),
                   pl.BlockSpec((SUBLANES, LANES), fix),
                   pl.BlockSpec((LRU_WIDTH, LRU_WIDTH), fix),
                   pl.BlockSpec((LRU_WIDTH, LRU_WIDTH), fix)],
        out_shape=[jax.ShapeDtypeStruct((S, IN_PROJ_WIDTH), BF16),
                   jax.ShapeDtypeStruct((16, LRU_WIDTH), F32),
                   jax.ShapeDtypeStruct((SUBLANES, LANES), F32),
                   jax.ShapeDtypeStruct((LRU_WIDTH, LRU_WIDTH), F32),
                   jax.ShapeDtypeStruct((LRU_WIDTH, LRU_WIDTH), F32)],
        scratch_shapes=[pltpu.VMEM((T + BLOCK, KV_WIDTH), F32), pltpu.VMEM((T + BLOCK, KV_WIDTH), F32),
                        pltpu.VMEM((BLOCK, 2 * KV_WIDTH), F32), pltpu.VMEM((T, LRU_WIDTH), F32),
                        pltpu.VMEM((T, CONV_WIDTH), F32), pltpu.VMEM((SUBLANES, LRU_WIDTH), F32)],
        sem=("arbitrary",), args=(proj, proj, proj, proj, proj, dymix, hprev, sinks, small, wa, wx), jobs=jobs)


def _mod_matmul(c_all, w_mod, name):
    L, Dm, N = w_mod.shape
    R = c_all.shape[0]
    tn = 768

    def body(c_ref, w_ref, o_ref, ca_ref):
        cv = c_ref[...]
        ca = (cv * _sigmoid(cv)).astype(BF16)
        ca_ref[...] = ca
        o_ref[0] = _dot(ca, w_ref[0].astype(BF16))

    return pl.pallas_call(
        body, name=name, grid=(L, N // tn),
        in_specs=[pl.BlockSpec((R, Dm), lambda l, n: (0, 0)),
                  pl.BlockSpec((1, Dm, tn), lambda l, n: (l, 0, n))],
        out_specs=[pl.BlockSpec((1, R, tn), lambda l, n: (l, 0, n)), pl.BlockSpec((R, Dm), lambda l, n: (0, 0))],
        out_shape=[jax.ShapeDtypeStruct((L, R, N), F32), jax.ShapeDtypeStruct((R, Dm), BF16)],
        compiler_params=_cp("arbitrary", "arbitrary"),
    )(c_all, w_mod)


def _adamw_update(g, w_ref, m_ref, v_ref, go_ref, d_ref, mo_ref, vo_ref):
    mn = ADAM_B1 * m_ref[...] + (1.0 - ADAM_B1) * g
    vn = ADAM_B2 * v_ref[...] + (1.0 - ADAM_B2) * (g * g)
    go_ref[...] = g
    mo_ref[...] = mn
    vo_ref[...] = vn
    m_hat = mn / (1.0 - ADAM_B1 ** ADAM_STEP)
    v_hat = vn / (1.0 - ADAM_B2 ** ADAM_STEP)
    d_ref[...] = (-ADAM_LR) * (m_hat / (jnp.sqrt(v_hat) + ADAM_EPS) + ADAM_WD * w_ref[...])


def _adamw(w, g, m, v, name, jobs=()):
    R, C = w.shape
    tr = 8
    for cand in (512, 256, 128, 64, 32, 16, 8):
        if R % cand == 0 and cand * C * 4 <= (1 << 20):
            tr = cand
            break

    def body(w_ref, g_ref, *rest):
        _adamw_update(g_ref[...], w_ref, *rest)

    spec = pl.BlockSpec((tr, C), lambda i: (i, 0))
    return _pcall(body, name=name, grid=(R // tr,), in_specs=[spec] * 4, out_specs=[spec] * 4,
                  out_shape=[jax.ShapeDtypeStruct((R, C), F32)] * 4, sem=("arbitrary",), args=(w, g, m, v),
                  jobs=jobs)


def _adamw_partials(w, partials, m, v, name):
    nl = len(partials)
    _, R, C = partials[0][0].shape
    tr = 8
    for cand in (256, 128, 64, 32, 16):
        if R % cand == 0 and cand * C * 4 <= (1 << 19):
            tr = cand
            break
    ni = R // tr

    def body(*refs):
        w_ref, p_refs = refs[0], refs[1:1 + 2 * nl]
        m_ref, v_ref, go_ref, d_ref, mo_ref, vo_ref = refs[1 + 2 * nl:]
        for l in range(nl):
            @pl.when(pl.program_id(0) == l)
            def _(pair=p_refs[2 * l:2 * l + 2]):
                own, sib = [((p[0].astype(F32) + p[1].astype(F32)) + p[2].astype(F32)) + p[3].astype(F32)
                            for p in pair]
                _adamw_update(own + sib, w_ref, m_ref, v_ref, go_ref, d_ref, mo_ref, vo_ref)

    def slots(l):
        return pl.BlockSpec((N_CHIPS, tr, C),
                            lambda ll, i: (0, jnp.where(ll == l, i, jnp.where(ll < l, 0, ni - 1)), 0))

    spec = pl.BlockSpec((tr, C), lambda ll, i: (ll * ni + i, 0))
    return pl.pallas_call(
        body, name=name, grid=(nl, ni),
        in_specs=[spec] + [slots(l) for l in range(nl) for _ in range(2)] + [spec, spec], out_specs=[spec] * 4,
        out_shape=[jax.ShapeDtypeStruct((nl * R, C), F32)] * 4,
        compiler_params=_cp("arbitrary", "arbitrary"),
    )(w, *[p for pair in partials for p in pair], m, v)


def _all_gather_small(v, name):
    M, N = v.shape

    def body(x_ref, out_ref, sum_ref, send_sems, recv_sems, local_sem):
        x, y, c = lax.axis_index("x"), lax.axis_index("y"), lax.axis_index("c")
        me, sibling = (x, y, c), (x, y, 1 - c)
        chips = [(1 - x, y), (x, 1 - y), (1 - x, 1 - y)]

        def rows(px, py, pc):
            return out_ref.at[pl.ds(pl.multiple_of((4 * px + 2 * py + pc) * M, SUBLANES), M), :]

        def copy(k, block, to, src=None):
            return pltpu.make_async_remote_copy(
                src_ref=rows(*block) if src is None else src, dst_ref=rows(*block),
                send_sem=send_sems.at[k], recv_sem=recv_sems.at[k], device_id=to, device_id_type=MESH)

        mine = pltpu.make_async_copy(x_ref, rows(*me), local_sem)
        mine.start()
        first = [copy(0, me, sibling, src=x_ref)]
        first += [copy(1 + j, me, (*chip, c), src=x_ref) for j, chip in enumerate(chips)]
        for cp in first:
            cp.start()
        passed = [copy(4 + j, (*chip, c), sibling) for j, chip in enumerate(chips)]
        for j, chip in enumerate(chips):
            copy(1 + j, (*chip, c), me).wait_recv()
            passed[j].start()
        copy(0, sibling, me).wait_recv()
        for j, chip in enumerate(chips):
            copy(4 + j, (*chip, 1 - c), me).wait_recv()
        for cp in first + passed:
            cp.wait_send()
        mine.wait()
        acc = out_ref[0:M, :]
        for d in range(1, N_DEV):
            acc = acc + out_ref[d * M:(d + 1) * M, :]
        sum_ref[...] = acc

    return pl.pallas_call(
        body, name=name,
        out_shape=[jax.ShapeDtypeStruct((N_DEV * M, N), F32), jax.ShapeDtypeStruct((M, N), F32)],
        in_specs=[pl.BlockSpec(memory_space=pltpu.VMEM)],
        out_specs=[pl.BlockSpec(memory_space=pltpu.VMEM), pl.BlockSpec(memory_space=pltpu.VMEM)],
        scratch_shapes=[pltpu.SemaphoreType.DMA((7,)), pltpu.SemaphoreType.DMA((7,)), pltpu.SemaphoreType.DMA],
        compiler_params=pltpu.CompilerParams(vmem_limit_bytes=VMEM_LIMIT),
    )(v)


_BIG = (("w_ffn1_gu", 1), ("w_ffn1_down", 0), ("w_ffn2_gu", 1), ("w_ffn2_down", 0), ("w_in", 1), ("w_out", 0))
_AXIS = dict(_BIG)

_GATHER_PLAN = {
    "first": [(0, "w_ffn1_gu")],
    (0, "ffn1_gu"): [(0, "w_ffn1_down"), (0, "w_in"), (0, "w_out")],
    (0, "ffn1_down"): [(1, "w_ffn1_down")],
    (0, "mix_in"): [(0, "w_ffn2_down")],
    (0, "mix_core"): [(0, "w_ffn2_gu")],
    (0, "mix_out"): [(1, "w_in")],
    (0, "ffn2_gu"): [(1, "w_ffn1_gu")],
    (0, "ffn2_down"): [(1, "w_out")],
    (1, "ffn1_gu"): [(1, "w_ffn2_gu")],
    (1, "ffn1_down"): [(1, "w_ffn2_down")],
}


def _pack(arrs, rows_multiple=SUBLANES):
    flat = jnp.concatenate([a.astype(F32).reshape(-1) for a in arrs])
    unit = rows_multiple * LANES
    total = -(-flat.shape[0] // unit) * unit
    return jnp.pad(flat, (0, total - flat.shape[0])).reshape(total // LANES, LANES)


def _unpack(flat, shapes):
    out, off = [], 0
    for shp in shapes:
        n = int(math.prod(shp))
        out.append(flat[off:off + n].reshape(shp))
        off += n
    return out


def _block_diag(w):
    out = jnp.zeros((LRU_WIDTH, LRU_WIDTH), F32)
    for h in range(4):
        out = lax.dynamic_update_slice(out, w[h], (h * HEAD_DIM, h * HEAD_DIM))
    return out


def _diag_blocks(w):
    return jnp.stack([w[h * HEAD_DIM:(h + 1) * HEAD_DIM, h * HEAD_DIM:(h + 1) * HEAD_DIM] for h in range(4)])


def _rows8(*rows):
    z = jnp.zeros((8 - len(rows), rows[0].shape[-1]), F32)
    return jnp.concatenate([jnp.stack(rows), z], axis=0)


def kernel(x, c, w_mod, b_mod, g_norm, w_ffn1_gu, w_ffn1_down, w_ffn2_gu, w_ffn2_down, w_in, w_out, attn_sinks, lru_conv_w, lru_conv_b, lru_gate_a_w, lru_gate_a_b, lru_gate_x_w, lru_gate_x_b, lru_lambda, sc_conv_w, g_final, loss_target, m_w_mod, m_b_mod, m_g_norm, m_w_ffn1_gu, m_w_ffn1_down, m_w_ffn2_gu, m_w_ffn2_down, m_w_in, m_w_out, m_attn_sinks, m_lru_conv_w, m_lru_conv_b, m_lru_gate_a_w, m_lru_gate_a_b, m_lru_gate_x_w, m_lru_gate_x_b, m_lru_lambda, m_sc_conv_w, m_g_final, v_w_mod, v_b_mod, v_g_norm, v_w_ffn1_gu, v_w_ffn1_down, v_w_ffn2_gu, v_w_ffn2_down, v_w_in, v_w_out, v_attn_sinks, v_lru_conv_w, v_lru_conv_b, v_lru_gate_a_w, v_lru_gate_a_b, v_lru_gate_x_w, v_lru_gate_x_b, v_lru_lambda, v_sc_conv_w, v_g_final):
    W = dict(w_mod=w_mod, b_mod=b_mod, g_norm=g_norm, w_ffn1_gu=w_ffn1_gu, w_ffn1_down=w_ffn1_down,
             w_ffn2_gu=w_ffn2_gu, w_ffn2_down=w_ffn2_down, w_in=w_in, w_out=w_out, attn_sinks=attn_sinks,
             lru_conv_w=lru_conv_w, lru_conv_b=lru_conv_b, lru_gate_a_w=lru_gate_a_w, lru_gate_a_b=lru_gate_a_b,
             lru_gate_x_w=lru_gate_x_w, lru_gate_x_b=lru_gate_x_b, lru_lambda=lru_lambda, sc_conv_w=sc_conv_w,
             g_final=g_final)
    M1 = dict(w_mod=m_w_mod, b_mod=m_b_mod, g_norm=m_g_norm, w_ffn1_gu=m_w_ffn1_gu, w_ffn1_down=m_w_ffn1_down,
              w_ffn2_gu=m_w_ffn2_gu, w_ffn2_down=m_w_ffn2_down, w_in=m_w_in, w_out=m_w_out,
              attn_sinks=m_attn_sinks, lru_conv_w=m_lru_conv_w, lru_conv_b=m_lru_conv_b,
              lru_gate_a_w=m_lru_gate_a_w, lru_gate_a_b=m_lru_gate_a_b, lru_gate_x_w=m_lru_gate_x_w,
              lru_gate_x_b=m_lru_gate_x_b, lru_lambda=m_lru_lambda, sc_conv_w=m_sc_conv_w, g_final=m_g_final)
    V1 = dict(w_mod=v_w_mod, b_mod=v_b_mod, g_norm=v_g_norm, w_ffn1_gu=v_w_ffn1_gu, w_ffn1_down=v_w_ffn1_down,
              w_ffn2_gu=v_w_ffn2_gu, w_ffn2_down=v_w_ffn2_down, w_in=v_w_in, w_out=v_w_out,
              attn_sinks=v_attn_sinks, lru_conv_w=v_lru_conv_w, lru_conv_b=v_lru_conv_b,
              lru_gate_a_w=v_lru_gate_a_w, lru_gate_a_b=v_lru_gate_a_b, lru_gate_x_w=v_lru_gate_x_w,
              lru_gate_x_b=v_lru_gate_x_b, lru_lambda=v_lru_lambda, sc_conv_w=v_sc_conv_w, g_final=v_g_final)
    names = ["w_mod", "b_mod", "g_norm", "w_ffn1_gu", "w_ffn1_down", "w_ffn2_gu", "w_ffn2_down", "w_in", "w_out",
             "attn_sinks", "lru_conv_w", "lru_conv_b", "lru_gate_a_w", "lru_gate_a_b", "lru_gate_x_w",
             "lru_gate_x_b", "lru_lambda", "sc_conv_w", "g_final"]

    xs = x[0]
    tgt = loss_target[0]
    S = xs.shape[0]
    chip = 2 * lax.axis_index("x") + lax.axis_index("y")
    batch = 2 * chip + lax.axis_index("c")
    L = DEPTH

    fwd_shapes = [(D_MODEL,), g_norm.shape, lru_conv_w.shape, sc_conv_w.shape]
    gathered, _ = _all_gather_small(_pack([c[0], g_norm, lru_conv_w, sc_conv_w]), "gather_small_fwd")
    gathered = gathered.reshape(N_DEV, -1)
    c_all = gathered[:, :D_MODEL]
    per_chip = [_unpack(gathered[2 * jj], fwd_shapes) for jj in range(N_CHIPS)]
    g_norm_full = jnp.concatenate([p[1] for p in per_chip], axis=-1)
    lru_conv_w_full = jnp.concatenate([p[2] for p in per_chip], axis=-1)
    sc_conv_w_full = jnp.concatenate([p[3] for p in per_chip], axis=-1)

    c_pad = jnp.concatenate([c_all, jnp.zeros_like(c_all)], axis=0)
    mod_part, c_act = _mod_matmul(c_pad, w_mod, "mod_matmul")
    mod_all, _ = _all_gather_small(mod_part.reshape(-1, LANES), "gather_mod")
    mod_all = mod_all.reshape(N_DEV, L, 16, -1)
    mod_rows = [lax.dynamic_index_in_dim(mod_all[2 * jj], batch, axis=1, keepdims=False) for jj in range(N_CHIPS)]
    mod = (jnp.concatenate(mod_rows, axis=-1) + b_mod).reshape(L, 9, D_MODEL)

    def nrm_rows(l, s):
        return _rows8(g_norm_full[l, s], mod[l, 3 * s], mod[l, 3 * s + 1], mod[l, 3 * s + 2])

    full = {}

    def gather_jobs(key):
        return [_GatherJob(W[n][l].astype(BF16), _AXIS[n]) for l, n in _GATHER_PLAN.get(key, ())]

    def landed(key, outs):
        full.update(zip(_GATHER_PLAN.get(key, ()), outs))

    landed("first", _comm_only(gather_jobs("first"), "gather_first"))

    def mixer_params(l):
        small = jnp.concatenate([lru_conv_w_full[l], lru_conv_b[l][None], lru_gate_a_b[l][None],
                                 lru_gate_x_b[l][None], lru_lambda[l][None], sc_conv_w_full[l],
                                 jnp.zeros((5, LRU_WIDTH), F32)], axis=0)
        return (attn_sinks[l], small, _block_diag(lru_gate_a_w[l]).astype(BF16),
                _block_diag(lru_gate_x_w[l]).astype(BF16))

    saved = []
    xcur = xs
    for l in range(L):
        n1, n2, n3 = nrm_rows(l, 0), nrm_rows(l, 1), nrm_rows(l, 2)

        def ffn(which, xin, nrm):
            key = (l, which + "_gu")
            (gu, h), ex = _norm_matmul(xin, nrm, full[(l, f"w_{which}_gu")], BF16, f"l{l}_{which}_gu", gather_jobs(key))
            landed(key, ex)
            key = (l, which + "_down")
            (xo, y), ex = _proj_residual(gu, full[(l, f"w_{which}_down")], xin, nrm, 0.5, True, f"l{l}_{which}_down",
                                         gather_jobs(key))
            landed(key, ex)
            return xo, (xin, h, gu, y)

        x1, s1 = ffn("ffn1", xcur, n1)
        (proj, h2), ex = _norm_matmul(x1, n2, full[(l, "w_in")], F32, f"l{l}_mix_in", gather_jobs((l, "mix_in")))
        landed((l, "mix_in"), ex)
        mp = mixer_params(l)
        (ymix, hprev), ex = _mixer_fwd(proj, *mp, f"l{l}_mix_core", gather_jobs((l, "mix_core")))
        landed((l, "mix_core"), ex)
        (x2, ymo), ex = _proj_residual(ymix, full[(l, "w_out")], x1, n2, 1.0, False, f"l{l}_mix_out",
                                       gather_jobs((l, "mix_out")))
        landed((l, "mix_out"), ex)
        s2 = (x1, h2, proj, ymix, ymo, hprev, mp)
        xcur, s3 = ffn("ffn2", x2, n3)
        saved.append((n1, n2, n3, s1, s2, s3))

    dx, stats = _final_loss(xcur, _rows8(g_final), tgt, "final_loss")
    loss = lax.psum(stats[1, 0], ("x", "y", "c"))
    d_g_final = stats[0]

    recv, theirs = {}, {}
    waiting = []

    def carried(fn, *a, extra=()):
        items = waiting + list(extra)
        waiting.clear()
        outs, landed_now = fn(*a, jobs=[_SiblingJob(recv[(ll, n)]) if g is None else _ScatterJob(g, _AXIS[n])
                                        for ll, n, g in items])
        for (ll, n, g), arr in zip(items, landed_now):
            if g is None:
                theirs[(ll, n)] = arr
            else:
                recv[(ll, n)] = arr
                waiting.append((ll, n, None))
        return outs

    dmod, d_gnorm, d_small = [None] * L, [None] * L, [None] * L
    for l in reversed(range(L)):
        n1, n2, n3, s1, s2, s3 = saved[l]

        def ffn_bwd(which, dxo, sv, nrm, hold_last):
            xin, h, gu, y = sv
            tag = f"l{l}_{which}"
            dgu, act, dy, dgate = carried(_proj_residual_bwd, dxo, gu, y, full[(l, f"w_{which}_down")], nrm, 0.5,
                                          tag + "_down_bwd")
            dw_down, _ = _atb(act, dy, BF16, 2816, 1024, tag + "_dw_down")
            dw_gu = carried(_atb, h, dgu, BF16, 1024, 2816, tag + "_dw_gu", extra=[(l, f"w_{which}_down", dw_down)])
            mine = [(l, f"w_{which}_gu", dw_gu)]
            dxi, red = carried(_nt_norm_bwd, dgu, full[(l, f"w_{which}_gu")], xin, nrm, dxo, tag + "_gu_bwd",
                               extra=[] if hold_last else mine)
            if hold_last:
                waiting.extend(mine)
            return dxi, (red[0], red[1], dgate[0]), red[2]

        dx, dm3, dg3 = ffn_bwd("ffn2", dx, s3, n3, True)
        x_in, h2, proj, ymix, ymo, hprev, mp = s2
        (dymix, dy, dgate), _ = _proj_residual_bwd(dx, None, ymo, full[(l, "w_out")], n2, 1.0, f"l{l}_mix_out_bwd")
        dw_out, _ = _atb(ymix, dy, BF16, 1024, 1024, f"l{l}_dw_out")
        dproj, dsm, dsink, dwa, dwx = carried(_mixer_bwd, proj, dymix, hprev, *mp, f"l{l}_mix_core_bwd")
        dw_in = carried(_atb, h2, dproj, BF16, 1024, 2048, f"l{l}_dw_in", extra=[(l, "w_out", dw_out)])
        dx, red = carried(_nt_norm_bwd, dproj, full[(l, "w_in")], x_in, n2, dx, f"l{l}_mix_in_bwd",
                          extra=[(l, "w_in", dw_in)])
        dm2, dg2 = (red[0], red[1], dgate[0]), red[2]
        dx, dm1, dg1 = ffn_bwd("ffn1", dx, s1, n1, l > 0)
        dmod[l] = jnp.stack(list(dm1) + list(dm2) + list(dm3))
        d_gnorm[l] = jnp.stack([dg1, dg2, dg3])
        d_small[l] = (dsink[:, 0], dsm[0:4], dsm[4], _diag_blocks(dwa), dsm[5], _diag_blocks(dwx), dsm[6],
                      dsm[7], dsm[8:11])
    grad_x = dx[None]

    def both(k):
        return jnp.stack([d_small[0][k], d_small[1][k]])
    small_names = ["g_norm", "attn_sinks", "lru_conv_w", "lru_conv_b", "lru_gate_a_w", "lru_gate_a_b",
                   "lru_gate_x_w", "lru_gate_x_b", "lru_lambda", "sc_conv_w", "g_final"]
    small_parts = [jnp.stack(d_gnorm)] + [both(k) for k in range(9)] + [d_g_final]
    dmod_flat = jnp.stack(dmod).reshape(-1)
    bwd_gathered, bwd_sum = _all_gather_small(_pack([dmod_flat] + small_parts), "gather_small_bwd")
    n_mod = dmod_flat.shape[0]
    dmod_all = bwd_gathered.reshape(N_DEV, -1)[:, :n_mod].reshape(N_DEV, L, 9 * D_MODEL)
    bwd_sum = bwd_sum.reshape(-1)
    G = {"b_mod": bwd_sum[:n_mod].reshape(L, 9 * D_MODEL)}
    G.update(zip(small_names, _unpack(bwd_sum[n_mod:], [p.shape for p in small_parts])))
    for n in ("g_norm", "lru_conv_w", "sc_conv_w"):
        wdt = W[n].shape[-1]
        G[n] = lax.dynamic_slice_in_dim(G[n], chip * wdt, wdt, axis=G[n].ndim - 1)

    ncol = w_mod.shape[-1]
    dmod_cols = lax.dynamic_slice_in_dim(dmod_all, chip * ncol, ncol, axis=2)
    zeros8 = jnp.zeros((N_DEV, ncol), F32)
    g_w_mod = jnp.stack([_atb(c_act, jnp.concatenate([dmod_cols[:, l], zeros8], axis=0).astype(BF16), F32,
                              D_MODEL, 768, f"l{l}_dw_mod")[0] for l in range(L)])

    out_g, out_d, out_m, out_v = {}, {}, {}, {}
    res = carried(_adamw, w_mod.reshape(-1, ncol), g_w_mod.reshape(-1, ncol), m_w_mod.reshape(-1, ncol),
                  v_w_mod.reshape(-1, ncol), "adamw_w_mod")
    out_g["w_mod"], out_d["w_mod"], out_m["w_mod"], out_v["w_mod"] = [r.reshape(w_mod.shape) for r in res]
    for n, _ in _BIG:
        shp = W[n].shape
        flat = (shp[0] * shp[1], shp[2])
        res = _adamw_partials(W[n].reshape(flat), [(recv[(l, n)], theirs[(l, n)]) for l in range(L)],
                              M1[n].reshape(flat), V1[n].reshape(flat), f"adamw_{n}")
        out_g[n], out_d[n], out_m[n], out_v[n] = [r.reshape(shp) for r in res]
    rest = ["b_mod"] + small_names
    shapes = [W[n].shape for n in rest]
    res, _ = _adamw(_pack([W[n] for n in rest]), _pack([G[n] for n in rest]), _pack([M1[n] for n in rest]),
                    _pack([V1[n] for n in rest]), "adamw_small")
    for dst, r in zip((out_g, out_d, out_m, out_v), res):
        dst.update(zip(rest, _unpack(r.reshape(-1), shapes)))

    return (loss, grad_x, *[out_g[n] for n in names], *[out_d[n] for n in names],
            *[out_m[n] for n in names], *[out_v[n] for n in names])
```

```python
import math

import jax
import jax.numpy as jnp
from jax import lax
from jax.experimental import pallas as pl
from jax.experimental.pallas import tpu as pltpu

F32 = jnp.float32
BF16 = jnp.bfloat16

D_MODEL = 1024
DEPTH = 2
HEAD_DIM = 64
N_Q_HEADS = 8
ATTN_WIDTH = 512
KV_WIDTH = 128
LRU_WIDTH = 256
CONV_WIDTH = 256
IN_PROJ_WIDTH = 2048
BLOCK = 128
D_FF = 2816
EPS = 1e-6
NEG_INF = -1e30
LRU_C = 8.0
N_CHIPS = 4
N_DEV = 8

C_Q, C_KV, C_LX, C_LG, C_SB, C_SC, C_SX = 0, 512, 768, 1024, 1280, 1536, 1792

ADAM_LR = 0.001
ADAM_B1 = 0.9
ADAM_B2 = 0.999
ADAM_EPS = 1e-08
ADAM_WD = 0.01
ADAM_STEP = 10

LANES = 128
SUBLANES = 8
VMEM_LIMIT = 56 * 1024 * 1024
MIX_TILE = 256

MESH = pl.DeviceIdType.MESH


def _cp(*sem):
    return pltpu.CompilerParams(dimension_semantics=sem, vmem_limit_bytes=VMEM_LIMIT)


def _tile(n, pref):
    t = min(n, pref)
    while n % t:
        t //= 2
    return t


MXU_DIM = 256


def _col_chunk(n):
    return max(c for c in range(MXU_DIM, 2816 + 1, MXU_DIM) if n % c == 0)


def _resident(shape):
    return pl.BlockSpec(shape, lambda *_: (0, 0), pipeline_mode=pl.Buffered(1))


def _sigmoid(v):
    return 1.0 / (1.0 + jnp.exp(-v))


def _expm1(v):
    series = v * (1.0 + v * (0.5 + v * (1.0 / 6.0)))
    return jnp.where(v > -0.01, series, jnp.exp(v) - 1.0)


def _softplus_neg(lam):
    e = jnp.exp(-jnp.abs(lam))
    log1p = jnp.where(e < 1e-2, e * (1.0 - e * (0.5 - e * (1.0 / 3.0))), jnp.log(1.0 + e))
    return jnp.maximum(-lam, 0.0) + log1p


_GELU_K = math.sqrt(2.0 / math.pi)
_GELU_C = 0.044715


def _gelu(v):
    t = jnp.tanh(_GELU_K * (v + _GELU_C * v * v * v))
    return 0.5 * v * (1.0 + t), t


def _gelu_grad(v, t):
    return 0.5 * (1.0 + t) + 0.5 * v * (1.0 - t * t) * _GELU_K * (1.0 + 3.0 * _GELU_C * v * v)


def _dot(a, b):
    return jnp.dot(a, b, preferred_element_type=F32)


def _dot_nt(a, b):
    return lax.dot_general(a, b, (((1,), (1,)), ((), ())), preferred_element_type=F32)


def _dot_tn(a, b):
    return lax.dot_general(a, b, (((0,), (0,)), ((), ())), preferred_element_type=F32)


def _window(ref, axis, j, width):
    start = pl.multiple_of(j * width, LANES if axis == 1 else 16)
    if axis == 1:
        return ref.at[:, pl.ds(start, width)]
    return ref.at[pl.ds(start, width), :]


def _chip_peers():
    x, y, c = lax.axis_index("x"), lax.axis_index("y"), lax.axis_index("c")
    return x, y, c, [(1 - x, y), (x, 1 - y), (1 - x, 1 - y)]


class _GatherJob:
    def __init__(self, shard, axis):
        self.src, self.axis, self.width, self.half = shard, axis, shard.shape[axis], shard.shape[0] // 2
        full = tuple(d * N_CHIPS if k == axis else d for k, d in enumerate(shard.shape))
        self.out_shape = jax.ShapeDtypeStruct(full, shard.dtype)

    def _piece(self, ref, j, hf):
        if self.axis == 1:
            return ref.at[pl.ds(pl.multiple_of(hf * self.half, 16), self.half),
                          pl.ds(pl.multiple_of(j * self.width, LANES), self.width)]
        return ref.at[pl.ds(pl.multiple_of(j * self.width + hf * self.half, 16), self.half), :]

    def _copies(self, src, dst, send, recv, loc, t):
        x, y, c, chips = _chip_peers()
        j = 2 * x + y
        owners = [2 * px + py for px, py in chips]
        local = pltpu.make_async_copy(src, _window(dst, self.axis, j, self.width), loc.at[t])
        mine = src.at[pl.ds(pl.multiple_of(c * self.half, 16), self.half), :]

        def ici(k, owner):
            return pltpu.make_async_remote_copy(
                src_ref=mine, dst_ref=self._piece(dst, owner, c), send_sem=send.at[JOB_SEMS * t + k],
                recv_sem=recv.at[JOB_SEMS * t + k], device_id=(*chips[k], c), device_id_type=MESH)

        def relay(k, hf):
            piece = self._piece(dst, owners[k], hf)
            return pltpu.make_async_remote_copy(
                src_ref=piece, dst_ref=piece, send_sem=send.at[JOB_SEMS * t + 4 + k],
                recv_sem=recv.at[JOB_SEMS * t + 4 + k],
                device_id=(x, y, 1 - c), device_id_type=MESH)

        return (local, [ici(k, j) for k in range(3)], [ici(k, owners[k]) for k in range(3)],
                [relay(k, c) for k in range(3)], [relay(k, 1 - c) for k in range(3)])

    def start(self, *a):
        local, ici_out, _, _, _ = self._copies(*a)
        local.start()
        for cp in ici_out:
            cp.start()

    def relay(self, *a):
        _, _, ici_in, relay_out, _ = self._copies(*a)
        for arrived, onward in zip(ici_in, relay_out):
            arrived.wait_recv()
            onward.start()

    def finish(self, *a):
        local, ici_out, _, relay_out, relay_in = self._copies(*a)
        for cp in relay_in:
            cp.wait_recv()
        for cp in ici_out + relay_out:
            cp.wait_send()
        local.wait()


class _ScatterJob:
    def __init__(self, full, axis):
        self.src, self.axis, self.width = full, axis, full.shape[axis] // N_CHIPS
        shard = tuple(self.width if k == axis else d for k, d in enumerate(full.shape))
        self.out_shape = jax.ShapeDtypeStruct((N_CHIPS,) + shard, full.dtype)

    def _copies(self, src, dst, send, recv, loc, t):
        x, y, c, chips = _chip_peers()
        local = pltpu.make_async_copy(_window(src, self.axis, 2 * x + y, self.width), dst.at[3], loc.at[t])
        sends = [pltpu.make_async_remote_copy(
            src_ref=_window(src, self.axis, 2 * px + py, self.width), dst_ref=dst.at[k],
            send_sem=send.at[JOB_SEMS * t + k], recv_sem=recv.at[JOB_SEMS * t + k], device_id=(px, py, c),
            device_id_type=MESH) for k, (px, py) in enumerate(chips)]
        return local, sends

    def start(self, *a):
        local, sends = self._copies(*a)
        local.start()
        for cp in sends:
            cp.start()

    def relay(self, *a):
        pass

    def finish(self, *a):
        local, sends = self._copies(*a)
        for cp in sends:
            cp.wait_recv()
        for cp in sends:
            cp.wait_send()
        local.wait()


class _SiblingJob:
    def __init__(self, arr):
        self.src, self.out_shape = arr, jax.ShapeDtypeStruct(arr.shape, arr.dtype)

    def _copy(self, src, dst, send, recv, loc, t):
        x, y, c = lax.axis_index("x"), lax.axis_index("y"), lax.axis_index("c")
        return pltpu.make_async_remote_copy(
            src_ref=src, dst_ref=dst, send_sem=send.at[JOB_SEMS * t], recv_sem=recv.at[JOB_SEMS * t],
            device_id=(x, y, 1 - c), device_id_type=MESH)

    def start(self, *a):
        self._copy(*a).start()

    def relay(self, *a):
        pass

    def finish(self, *a):
        self._copy(*a).wait()


JOB_SEMS = 8


def _run_jobs(phase, jobs, srcs, dsts, sems):
    for t, job in enumerate(jobs):
        getattr(job, phase)(srcs[t], dsts[t], *sems, t)


def _job_scratch(n):
    return [pltpu.SemaphoreType.DMA((JOB_SEMS * n,)), pltpu.SemaphoreType.DMA((JOB_SEMS * n,)),
            pltpu.SemaphoreType.DMA((n,))]


def _pcall(body, *, name, grid, in_specs, out_specs, out_shape, sem, args, scratch_shapes=(), jobs=()):
    in_specs, out_specs, out_shape = list(in_specs), list(out_specs), list(out_shape)
    scratch_shapes = list(scratch_shapes)
    if not jobs:
        res = pl.pallas_call(body, name=name, grid=grid, in_specs=in_specs, out_specs=out_specs, out_shape=out_shape,
                             scratch_shapes=scratch_shapes, compiler_params=_cp(*sem))(*args)
        return list(res), []
    n_in, n_out, n_scr, nj = len(args), len(out_shape), len(scratch_shapes), len(jobs)
    n_steps = math.prod(grid)
    relay_step = (3 * n_steps) // 4
    relay_early = 0 < relay_step < n_steps - 1

    def wrapped(*refs):
        ins, refs = refs[:n_in], refs[n_in:]
        jin, refs = refs[:nj], refs[nj:]
        outs, refs = refs[:n_out], refs[n_out:]
        jout, refs = refs[:nj], refs[nj:]
        scr, sems = refs[:n_scr], refs[n_scr:]
        step = pl.program_id(0)
        for d in range(1, len(grid)):
            step = step * grid[d] + pl.program_id(d)

        @pl.when(step == 0)
        def _():
            _run_jobs("start", jobs, jin, jout, sems)

        if relay_early:
            @pl.when(step == relay_step)
            def _():
                _run_jobs("relay", jobs, jin, jout, sems)
        body(*ins, *outs, *scr)

        @pl.when(step == n_steps - 1)
        def _():
            if not relay_early:
                _run_jobs("relay", jobs, jin, jout, sems)
            _run_jobs("finish", jobs, jin, jout, sems)

    hbm = pl.BlockSpec(memory_space=pltpu.HBM)
    res = pl.pallas_call(
        wrapped, name=name, grid=grid, in_specs=in_specs + [hbm] * nj, out_specs=out_specs + [hbm] * nj,
        out_shape=out_shape + [job.out_shape for job in jobs], scratch_shapes=scratch_shapes + _job_scratch(nj),
        compiler_params=_cp(*sem))(*args, *[job.src for job in jobs])
    return list(res[:n_out]), list(res[n_out:])


def _comm_only(jobs, name):
    nj = len(jobs)

    def body(*refs):
        srcs, dsts, sems = refs[:nj], refs[nj:2 * nj], refs[2 * nj:]
        for phase in ("start", "relay", "finish"):
            _run_jobs(phase, jobs, srcs, dsts, sems)

    hbm = pl.BlockSpec(memory_space=pltpu.HBM)
    return list(pl.pallas_call(
        body, name=name, in_specs=[hbm] * nj, out_specs=[hbm] * nj, out_shape=[job.out_shape for job in jobs],
        scratch_shapes=_job_scratch(nj))(*[job.src for job in jobs]))


def _norm_matmul(x, nrm, w, out_dtype, name, jobs=()):
    S, Dm = x.shape
    N = w.shape[1]
    tm = _tile(S, 512)
    cw = _col_chunk(N)

    def body(x_ref, nrm_ref, w_ref, o_ref, h_ref):
        xv = x_ref[...]
        rstd = lax.rsqrt(jnp.mean(xv * xv, axis=-1, keepdims=True) + EPS)
        hn = (xv * rstd) * nrm_ref[0:1, :]
        hb = (hn * (1.0 + nrm_ref[2:3, :]) + nrm_ref[1:2, :]).astype(BF16)
        h_ref[...] = hb
        for n in range(N // cw):
            o_ref[:, n * cw:(n + 1) * cw] = _dot(hb, w_ref[:, n * cw:(n + 1) * cw]).astype(o_ref.dtype)

    return _pcall(
        body, name=name, grid=(S // tm,),
        in_specs=[pl.BlockSpec((tm, Dm), lambda i: (i, 0)),
                  pl.BlockSpec((8, Dm), lambda i: (0, 0)),
                  _resident((Dm, N))],
        out_specs=[pl.BlockSpec((tm, N), lambda i: (i, 0)),
                   pl.BlockSpec((tm, Dm), lambda i: (i, 0))],
        out_shape=[jax.ShapeDtypeStruct((S, N), out_dtype), jax.ShapeDtypeStruct((S, Dm), BF16)],
        sem=("arbitrary",), args=(x, nrm, w), jobs=jobs)


def _proj_residual(a, w, x, nrm, coef, swiglu, name, jobs=()):
    S, Ka = a.shape
    K, Dm = w.shape
    tm = _tile(S, 256)

    def body(a_ref, w_ref, x_ref, nrm_ref, o_ref, y_ref):
        if swiglu:
            g = a_ref[:, :K].astype(F32)
            u = a_ref[:, K:].astype(F32)
            act = (g * _sigmoid(g) * u).astype(BF16)
        else:
            act = a_ref[...]
        y = _dot(act, w_ref[...])
        o_ref[...] = x_ref[...] + (coef * nrm_ref[3:4, :]) * y
        y_ref[...] = y.astype(BF16)

    return _pcall(
        body, name=name, grid=(S // tm,),
        in_specs=[pl.BlockSpec((tm, Ka), lambda i: (i, 0)),
                  _resident((K, Dm)),
                  pl.BlockSpec((tm, Dm), lambda i: (i, 0)),
                  pl.BlockSpec((8, Dm), lambda i: (0, 0))],
        out_specs=[pl.BlockSpec((tm, Dm), lambda i: (i, 0)),
                   pl.BlockSpec((tm, Dm), lambda i: (i, 0))],
        out_shape=[jax.ShapeDtypeStruct((S, Dm), F32), jax.ShapeDtypeStruct((S, Dm), BF16)],
        sem=("arbitrary",), args=(a, w, x, nrm), jobs=jobs)


def _proj_residual_bwd(dxo, a, y, w, nrm, coef, swiglu, name, jobs=()):
    S, Dm = dxo.shape
    K = w.shape[0]
    Ka = a.shape[1]
    tm = _tile(S, 256)
    n_steps = S // tm
    chunks = [(c0, min(6 * MXU_DIM, K - c0)) for c0 in range(0, K, 6 * MXU_DIM)]

    def body(dxo_ref, y_ref, w_ref, nrm_ref, a_ref, da_ref, dgate_ref, dw_ref, acc):
        @pl.when(pl.program_id(0) == 0)
        def _():
            dgate_ref[...] = jnp.zeros_like(dgate_ref)
            acc[...] = jnp.zeros_like(acc)

        dxo_v = dxo_ref[...]
        dyb = ((coef * nrm_ref[3:4, :]) * dxo_v).astype(BF16)
        dgate_ref[0:1, :] += jnp.sum(coef * y_ref[...].astype(F32) * dxo_v, axis=0, keepdims=True)
        for c0, cs in chunks:
            dact = _dot_nt(dyb, w_ref[c0:c0 + cs, :])
            if swiglu:
                g = a_ref[:, c0:c0 + cs].astype(F32)
                u = a_ref[:, K + c0:K + c0 + cs].astype(F32)
                s = _sigmoid(g)
                si = g * s
                da_ref[:, c0:c0 + cs] = (dact * u * (s * (1.0 + g * (1.0 - s)))).astype(BF16)
                da_ref[:, K + c0:K + c0 + cs] = (dact * si).astype(BF16)
                act = (si * u).astype(BF16)
            else:
                da_ref[:, c0:c0 + cs] = dact
                act = a_ref[:, c0:c0 + cs]
            acc[c0:c0 + cs, :] += _dot_tn(act, dyb)

        @pl.when(pl.program_id(0) == n_steps - 1)
        def _():
            dw_ref[...] = acc[...].astype(BF16)

    row = lambda i: (i, 0)
    fix = lambda i: (0, 0)
    return _pcall(
        body, name=name, grid=(n_steps,),
        in_specs=[pl.BlockSpec((tm, Dm), row), pl.BlockSpec((tm, Dm), row), _resident((K, Dm)),
                  pl.BlockSpec((8, Dm), fix), pl.BlockSpec((tm, Ka), row)],
        out_specs=[pl.BlockSpec((tm, Ka), row), pl.BlockSpec((8, Dm), fix), _resident((K, Dm))],
        out_shape=[jax.ShapeDtypeStruct((S, Ka), BF16 if swiglu else F32), jax.ShapeDtypeStruct((8, Dm), F32),
                   jax.ShapeDtypeStruct((K, Dm), BF16)],
        scratch_shapes=[pltpu.VMEM((K, Dm), F32)],
        sem=("arbitrary",), args=(dxo, y, w, nrm, a), jobs=jobs)


def _atb(a, b, out_dtype, bm, bn, name, jobs=()):
    S, M = a.shape
    N = b.shape[1]
    bk = _tile(S, 1024)
    nk = S // bk

    def body(a_ref, b_ref, o_ref, acc):
        k = pl.program_id(2)

        @pl.when(k == 0)
        def _():
            acc[...] = jnp.zeros_like(acc)
        acc[...] += _dot_tn(a_ref[...], b_ref[...])

        @pl.when(k == nk - 1)
        def _():
            o_ref[...] = acc[...].astype(o_ref.dtype)

    (out,), extra = _pcall(
        body, name=name, grid=(M // bm, N // bn, nk),
        in_specs=[pl.BlockSpec((bk, bm), lambda m, n, k: (k, m)),
                  pl.BlockSpec((bk, bn), lambda m, n, k: (k, n))],
        out_specs=[pl.BlockSpec((bm, bn), lambda m, n, k: (m, n))],
        out_shape=[jax.ShapeDtypeStruct((M, N), out_dtype)],
        scratch_shapes=[pltpu.VMEM((bm, bn), F32)],
        sem=("arbitrary", "arbitrary", "arbitrary"), args=(a, b), jobs=jobs)
    return out, extra


def _nt_norm_bwd(dout, w, x, nrm, dxo, name, jobs=()):
    S, N = dout.shape
    Dm = w.shape[0]
    tm = _tile(S, 256)

    def body(do_ref, w_ref, x_ref, nrm_ref, dxo_ref, dx_ref, red_ref):
        @pl.when(pl.program_id(0) == 0)
        def _():
            red_ref[...] = jnp.zeros_like(red_ref)
        dh = _dot_nt(do_ref[...], w_ref[...])
        xv = x_ref[...]
        rstd = lax.rsqrt(jnp.mean(xv * xv, axis=-1, keepdims=True) + EPS)
        xn = xv * rstd
        gain = nrm_ref[0:1, :]
        hn = xn * gain
        dhn = dh * (1.0 + nrm_ref[2:3, :])
        red_ref[0:1, :] += jnp.sum(dh, axis=0, keepdims=True)
        red_ref[1:2, :] += jnp.sum(dh * hn, axis=0, keepdims=True)
        red_ref[2:3, :] += jnp.sum(dhn * xn, axis=0, keepdims=True)
        dxn = dhn * gain
        dx = rstd * (dxn - xn * jnp.mean(dxn * xn, axis=-1, keepdims=True))
        dx_ref[...] = dxo_ref[...] + dx

    return _pcall(
        body, name=name, grid=(S // tm,),
        in_specs=[pl.BlockSpec((tm, N), lambda i: (i, 0)),
                  _resident((Dm, N)),
                  pl.BlockSpec((tm, Dm), lambda i: (i, 0)),
                  pl.BlockSpec((8, Dm), lambda i: (0, 0)),
                  pl.BlockSpec((tm, Dm), lambda i: (i, 0))],
        out_specs=[pl.BlockSpec((tm, Dm), lambda i: (i, 0)),
                   pl.BlockSpec((8, Dm), lambda i: (0, 0))],
        out_shape=[jax.ShapeDtypeStruct((S, Dm), F32), jax.ShapeDtypeStruct((8, Dm), F32)],
        sem=("arbitrary",), args=(dout, w, x, nrm, dxo), jobs=jobs)


def _final_loss(x, gf, tgt, name):
    S, Dm = x.shape
    tm = _tile(S, 512)

    def body(x_ref, g_ref, t_ref, dx_ref, st_ref):
        @pl.when(pl.program_id(0) == 0)
        def _():
            st_ref[...] = jnp.zeros_like(st_ref)
        xv = x_ref[...]
        rstd = lax.rsqrt(jnp.mean(xv * xv, axis=-1, keepdims=True) + EPS)
        xn = xv * rstd
        gain = g_ref[0:1, :]
        err = xn * gain - t_ref[...]
        st_ref[1:2, :] += jnp.full((1, Dm), 0.5 / Dm, F32) * jnp.sum(err * err)
        dy = err * (1.0 / Dm)
        st_ref[0:1, :] += jnp.sum(dy * xn, axis=0, keepdims=True)
        dxn = dy * gain
        dx_ref[...] = rstd * (dxn - xn * jnp.mean(dxn * xn, axis=-1, keepdims=True))

    return pl.pallas_call(
        body, name=name, grid=(S // tm,),
        in_specs=[pl.BlockSpec((tm, Dm), lambda i: (i, 0)),
                  pl.BlockSpec((8, Dm), lambda i: (0, 0)),
                  pl.BlockSpec((tm, Dm), lambda i: (i, 0))],
        out_specs=[pl.BlockSpec((tm, Dm), lambda i: (i, 0)),
                   pl.BlockSpec((8, Dm), lambda i: (0, 0))],
        out_shape=[jax.ShapeDtypeStruct((S, Dm), F32), jax.ShapeDtypeStruct((8, Dm), F32)],
        compiler_params=_cp("arbitrary"),
    )(x, gf, tgt)


def _alibi_slope(h):
    return float(2.0 ** (-8.0 * (h + 1) / N_Q_HEADS))


def _head_planes(pair_cols):
    lane = lax.broadcasted_iota(jnp.int32, pair_cols.shape, 1)
    low = lane < HEAD_DIM
    h0_lo = jnp.where(low, pair_cols, 0.0)
    h1_hi = jnp.where(low, 0.0, pair_cols)
    h0_hi = pltpu.roll(h0_lo, HEAD_DIM, 1)
    h1_lo = pltpu.roll(h1_hi, HEAD_DIM, 1)
    return ((h0_lo.astype(BF16), h0_hi.astype(BF16)), (h1_lo.astype(BF16), h1_hi.astype(BF16)))


def _to_plane(v, e, g):
    lane = lax.broadcasted_iota(jnp.int32, v.shape, 1)
    keep = (lane < HEAD_DIM) if e == 0 else (lane >= HEAD_DIM)
    v = jnp.where(keep, v, 0.0)
    return v if e == g else pltpu.roll(v, HEAD_DIM, 1)


def _band_geometry(first_block):
    qi = lax.broadcasted_iota(jnp.int32, (BLOCK, BLOCK), 0)
    kj = lax.broadcasted_iota(jnp.int32, (BLOCK, BLOCK), 1)
    own = kj <= qi
    dist = jnp.where(own, qi - kj, qi + BLOCK - kj).astype(F32)
    valid = kj <= qi + BLOCK * (1 - first_block)
    return own, dist, valid


def _fold(band, own):
    return jnp.where(own, band[:, BLOCK:], band[:, :BLOCK])


def _unfold(v, own):
    return jnp.concatenate([jnp.where(own, 0.0, v), jnp.where(own, v, 0.0)], axis=1)


def _softmax_band(s, h, geometry, sink):
    own, dist, valid = geometry
    s = jnp.where(valid, s - _alibi_slope(h) * dist, NEG_INF)
    m = jnp.maximum(jnp.max(s, axis=-1, keepdims=True), sink)
    p = jnp.exp(s - m)
    e_sink = jnp.exp(sink - m)
    inv = 1.0 / (jnp.sum(p, axis=-1, keepdims=True) + e_sink)
    return p * inv, e_sink * inv


def _past(cur, prev, s, row):
    return jnp.where(row < s, pltpu.roll(prev, s, 0), pltpu.roll(cur, s, 0))


def _future(cur, nxt, s, row):
    T = cur.shape[0]
    return jnp.where(row >= T - s, pltpu.roll(nxt, T - s, 0), pltpu.roll(cur, T - s, 0))


def _edge_row(v, last):
    T = v.shape[0]
    r8 = lax.broadcasted_iota(jnp.int32, (SUBLANES, v.shape[1]), 0)
    blk = v[T - SUBLANES:, :] if last else v[:SUBLANES, :]
    return jnp.sum(jnp.where(r8 == (SUBLANES - 1 if last else 0), blk, 0.0), axis=0, keepdims=True)


def _lru_gates(lx, lx_prev, small_ref, wa_ref, wx_ref, row, t0):
    xc = (small_ref[4:5, :] + small_ref[3:4, :] * lx + small_ref[2:3, :] * _past(lx, lx_prev, 1, row)
          + small_ref[1:2, :] * _past(lx, lx_prev, 2, row) + small_ref[0:1, :] * _past(lx, lx_prev, 3, row))
    xcb = xc.astype(BF16)
    r = _sigmoid(_dot(xcb, wa_ref[...]) + small_ref[5:6, :])
    ig = _sigmoid(_dot(xcb, wx_ref[...]) + small_ref[6:7, :])
    sp = _softplus_neg(small_ref[7:8, :])
    la = (-LRU_C) * r * sp
    a = jnp.exp(la)
    first = (row + t0) == 0
    mult = jnp.where(first, 1.0, jnp.sqrt(-_expm1(2.0 * la)))
    return xc, xcb, r, ig, sp, a, mult, first


def _mixer_fwd(proj, sinks, small, wa, wx, name, jobs=()):
    S = proj.shape[0]
    T = MIX_TILE
    nT = S // T
    nb = T // BLOCK

    def body(proj_ref, sink_ref, small_ref, wa_ref, wx_ref, y_ref, hp_ref, kvp, lxp, zp, hcar):
        i = pl.program_id(0)

        @pl.when(i == 0)
        def _():
            kvp[...] = jnp.zeros_like(kvp)
            lxp[...] = jnp.zeros_like(lxp)
            zp[...] = jnp.zeros_like(zp)
            hcar[...] = jnp.zeros_like(hcar)

        row = lax.broadcasted_iota(jnp.int32, (T, LRU_WIDTH), 0)

        kv = proj_ref[:, C_KV:C_KV + 2 * KV_WIDTH]
        ext = jnp.concatenate([kvp[...], kv], axis=0)
        kx = _head_planes(ext[:, :KV_WIDTH])
        vx = _head_planes(ext[:, KV_WIDTH:])
        first_tile = jnp.where(i == 0, 1, 0)
        units = [(b, pair, e) for b in range(nb) for pair in range(N_Q_HEADS // 2) for e in range(2)]
        geometry = [_band_geometry(first_tile if b == 0 else 0) for b in range(nb)]
        keys = [slice(b * BLOCK, (b + 2) * BLOCK) for b in range(nb)]
        qp = {(b, pair): (proj_ref[b * BLOCK:(b + 1) * BLOCK, pair * LANES:(pair + 1) * LANES] * 0.125).astype(BF16)
              for b in range(nb) for pair in range(N_Q_HEADS // 2)}
        scores = [_fold(_dot_nt(qp[(b, pair)], kx[pair // 2][e][keys[b]]), geometry[b][0]) for b, pair, e in units]
        probs = [_unfold(_softmax_band(s, 2 * pair + e, geometry[b], sink_ref[2 * pair + e])[0],
                         geometry[b][0]).astype(BF16) for s, (b, pair, e) in zip(scores, units)]
        outs = [_dot(p, vx[pair // 2][e][keys[b]]) for p, (b, pair, e) in zip(probs, units)]
        for u in range(0, len(units), 2):
            b, pair, _ = units[u]
            y_ref[b * BLOCK:(b + 1) * BLOCK, pair * LANES:(pair + 1) * LANES] = (outs[u] + outs[u + 1]).astype(BF16)
        kvp[...] = kv[T - BLOCK:, :]

        lx = proj_ref[:, C_LX:C_LX + LRU_WIDTH]
        xc, _, _, ig, _, a, mult, _ = _lru_gates(lx, lxp[...], small_ref, wa_ref, wx_ref, row, i * T)
        lxp[...] = lx
        aa = a
        bb = mult * (ig * xc)
        s = 1
        while s < T:
            a_sh = jnp.where(row >= s, pltpu.roll(aa, s, 0), 1.0)
            b_sh = jnp.where(row >= s, pltpu.roll(bb, s, 0), 0.0)
            bb = aa * b_sh + bb
            aa = aa * a_sh
            s *= 2
        hc = hcar[0:1, :]
        hh = bb + aa * hc
        hp_ref[...] = jnp.where(row < 1, hc, pltpu.roll(hh, 1, 0))
        hcar[...] = jnp.broadcast_to(_edge_row(hh, True), hcar.shape)
        gl, _ = _gelu(proj_ref[:, C_LG:C_LG + LRU_WIDTH])
        y_ref[:, ATTN_WIDTH:ATTN_WIDTH + LRU_WIDTH] = (gl * hh).astype(BF16)

        z = proj_ref[:, C_SC:C_SC + CONV_WIDTH] * proj_ref[:, C_SX:C_SX + CONV_WIDTH]
        c3 = (small_ref[10:11, :] * z + small_ref[9:10, :] * _past(z, zp[...], 1, row)
              + small_ref[8:9, :] * _past(z, zp[...], 2, row))
        zp[...] = z
        y_ref[:, ATTN_WIDTH + LRU_WIDTH:] = (proj_ref[:, C_SB:C_SB + CONV_WIDTH] * c3).astype(BF16)

    fix = lambda i: (0, 0)
    return _pcall(
        body, name=name, grid=(nT,),
        in_specs=[pl.BlockSpec((T, IN_PROJ_WIDTH), lambda i: (i, 0)),
                  pl.BlockSpec(memory_space=pltpu.SMEM),
                  pl.BlockSpec((16, LRU_WIDTH), fix),
                  pl.BlockSpec((LRU_WIDTH, LRU_WIDTH), fix),
                  pl.BlockSpec((LRU_WIDTH, LRU_WIDTH), fix)],
        out_specs=[pl.BlockSpec((T, D_MODEL), lambda i: (i, 0)),
                   pl.BlockSpec((T, LRU_WIDTH), lambda i: (i, 0))],
        out_shape=[jax.ShapeDtypeStruct((S, D_MODEL), BF16), jax.ShapeDtypeStruct((S, LRU_WIDTH), F32)],
        scratch_shapes=[pltpu.VMEM((BLOCK, 2 * KV_WIDTH), F32), pltpu.VMEM((T, LRU_WIDTH), F32),
                        pltpu.VMEM((T, CONV_WIDTH), F32), pltpu.VMEM((SUBLANES, LRU_WIDTH), F32)],
        sem=("arbitrary",), args=(proj, sinks, small, wa, wx), jobs=jobs)


def _mixer_bwd(proj, dymix, hprev, sinks, small, wa, wx, name, jobs=()):
    S = proj.shape[0]
    T = MIX_TILE
    nT = S // T
    nb = T // BLOCK
    bpt = T // BLOCK

    def body(proj_ref, kvprev_ref, lxprev_ref, scprev_ref, sxprev_ref, dy_ref, hp_ref, sink_ref, small_ref,
             wa_ref, wx_ref, dp_ref, dsm_ref, dsink_ref, dwa_ref, dwx_ref,
             dk_s, dv_s, dkv_c, dxc_n, dc3_n, p_c):
        i = pl.program_id(0)
        ti = nT - 1 - i
        has_prev = jnp.where(ti == 0, 0.0, 1.0)

        @pl.when(i == 0)
        def _():
            for r in (dkv_c, dxc_n, dc3_n, p_c, dsm_ref, dsink_ref, dwa_ref, dwx_ref):
                r[...] = jnp.zeros_like(r)

        row = lax.broadcasted_iota(jnp.int32, (T, LRU_WIDTH), 0)

        kv = proj_ref[:, C_KV:C_KV + 2 * KV_WIDTH]
        ext = jnp.concatenate([kvprev_ref[...] * has_prev, kv], axis=0)
        kx = _head_planes(ext[:, :KV_WIDTH])
        vx = _head_planes(ext[:, KV_WIDTH:])
        dk_s[...] = jnp.zeros_like(dk_s)
        dv_s[...] = jnp.zeros_like(dv_s)
        dk_s[T:, :] = dkv_c[:, :KV_WIDTH]
        dv_s[T:, :] = dkv_c[:, KV_WIDTH:]
        first_tile = jnp.where(ti == 0, 1, 0)
        units = [(b, pair, e) for b in range(nb) for pair in range(N_Q_HEADS // 2) for e in range(2)]
        geometry = [_band_geometry(first_tile if b == 0 else 0) for b in range(nb)]
        keys = [slice(b * BLOCK, (b + 2) * BLOCK) for b in range(nb)]
        tile = {(b, pair): (slice(b * BLOCK, (b + 1) * BLOCK), slice(pair * LANES, (pair + 1) * LANES))
                for b in range(nb) for pair in range(N_Q_HEADS // 2)}
        qp = {k: (proj_ref[rc] * 0.125).astype(BF16) for k, rc in tile.items()}
        dob = {k: dy_ref[rc].astype(BF16) for k, rc in tile.items()}
        scores = [_fold(_dot_nt(qp[(b, pair)], kx[pair // 2][e][keys[b]]), geometry[b][0]) for b, pair, e in units]
        dprob = [_fold(_dot_nt(dob[(b, pair)], vx[pair // 2][e][keys[b]]), geometry[b][0]) for b, pair, e in units]
        pn_wide, ds_wide = [], []
        for s, dpm, (b, pair, e) in zip(scores, dprob, units):
            h = 2 * pair + e
            own = geometry[b][0]
            pn, psink = _softmax_band(s, h, geometry[b], sink_ref[h])
            dsum = jnp.sum(pn * dpm, axis=-1, keepdims=True)
            dsink_ref[h:h + 1, :] += jnp.full((1, LANES), -1.0, F32) * jnp.sum(psink * dsum)
            pn_wide.append(_unfold(pn, own).astype(BF16))
            ds_wide.append(_unfold(pn * (dpm - dsum), own).astype(BF16))
        dq = {}
        for pw, ds, (b, pair, e) in zip(pn_wide, ds_wide, units):
            g = pair // 2
            dv_s[keys[b], :] += _to_plane(_dot_tn(pw, dob[(b, pair)]), e, g)
            dk_s[keys[b], :] += _to_plane(_dot_tn(ds, qp[(b, pair)]), e, g)
            part = _dot(ds, kx[g][e][keys[b]])
            dq[(b, pair)] = part if e == 0 else dq[(b, pair)] + part
        for k, rc in tile.items():
            dp_ref[rc] = (0.125 * dq[k]).astype(BF16)
        dp_ref[:, C_KV:C_KV + KV_WIDTH] = dk_s[BLOCK:, :].astype(BF16)
        dp_ref[:, C_KV + KV_WIDTH:C_KV + 2 * KV_WIDTH] = dv_s[BLOCK:, :].astype(BF16)
        dkv_c[:, :KV_WIDTH] = dk_s[:BLOCK, :]
        dkv_c[:, KV_WIDTH:] = dv_s[:BLOCK, :]

        lx = proj_ref[:, C_LX:C_LX + LRU_WIDTH]
        lxprev = lxprev_ref[...] * has_prev
        xc, xcb, r, ig, sp, a, mult, first = _lru_gates(lx, lxprev, small_ref, wa_ref, wx_ref, row, ti * T)
        hp = hp_ref[...]
        hh = a * hp + mult * (ig * xc)
        lg = proj_ref[:, C_LG:C_LG + LRU_WIDTH]
        gl, th = _gelu(lg)
        dyl = dy_ref[:, ATTN_WIDTH:ATTN_WIDTH + LRU_WIDTH]
        dp_ref[:, C_LG:C_LG + LRU_WIDTH] = (dyl * hh * _gelu_grad(lg, th)).astype(BF16)
        aa = jnp.where(row < T - 1, pltpu.roll(a, T - 1, 0), 1.0)
        bb = dyl * gl
        s = 1
        while s < T:
            a_sh = jnp.where(row < T - s, pltpu.roll(aa, T - s, 0), 1.0)
            b_sh = jnp.where(row < T - s, pltpu.roll(bb, T - s, 0), 0.0)
            bb = bb + aa * b_sh
            aa = aa * a_sh
            s *= 2
        G = bb + aa * p_c[0:1, :]
        p_c[...] = jnp.broadcast_to(_edge_row(a * G, False), p_c.shape)
        da = G * hp
        dmult = G * (ig * xc)
        dig = G * mult * xc
        dxc = G * mult * ig
        dla = da * a + dmult * jnp.where(first, 0.0, -(a * a) / mult)
        dr = dla * ((-LRU_C) * sp)
        lam = small_ref[7:8, :]
        dsm_ref[7:8, :] += jnp.sum(dla * ((-LRU_C) * r), axis=0, keepdims=True) * (-_sigmoid(-lam))
        dpa = dr * r * (1.0 - r)
        dpx = dig * ig * (1.0 - ig)
        dsm_ref[5:6, :] += jnp.sum(dpa, axis=0, keepdims=True)
        dsm_ref[6:7, :] += jnp.sum(dpx, axis=0, keepdims=True)
        dpab = dpa.astype(BF16)
        dpxb = dpx.astype(BF16)
        dwa_ref[...] += _dot_tn(xcb, dpab)
        dwx_ref[...] += _dot_tn(xcb, dpxb)
        dxc = dxc + _dot_nt(dpab, wa_ref[...]) + _dot_nt(dpxb, wx_ref[...])
        dsm_ref[4:5, :] += jnp.sum(dxc, axis=0, keepdims=True)
        dsm_ref[3:4, :] += jnp.sum(dxc * lx, axis=0, keepdims=True)
        for k in range(3):
            dsm_ref[k:k + 1, :] += jnp.sum(dxc * _past(lx, lxprev, 3 - k, row), axis=0, keepdims=True)
        nxt = dxc_n[...]
        dlx = (small_ref[3:4, :] * dxc + small_ref[2:3, :] * _future(dxc, nxt, 1, row)
               + small_ref[1:2, :] * _future(dxc, nxt, 2, row) + small_ref[0:1, :] * _future(dxc, nxt, 3, row))
        dxc_n[...] = dxc
        dp_ref[:, C_LX:C_LX + LRU_WIDTH] = dlx.astype(BF16)

        sc = proj_ref[:, C_SC:C_SC + CONV_WIDTH]
        sx = proj_ref[:, C_SX:C_SX + CONV_WIDTH]
        sb = proj_ref[:, C_SB:C_SB + CONV_WIDTH]
        z = sc * sx
        zprev = (scprev_ref[...] * sxprev_ref[...]) * has_prev
        z1 = _past(z, zprev, 1, row)
        z2 = _past(z, zprev, 2, row)
        c3 = small_ref[10:11, :] * z + small_ref[9:10, :] * z1 + small_ref[8:9, :] * z2
        dys = dy_ref[:, ATTN_WIDTH + LRU_WIDTH:]
        dp_ref[:, C_SB:C_SB + CONV_WIDTH] = (dys * c3).astype(BF16)
        dc3 = dys * sb
        dsm_ref[10:11, :] += jnp.sum(dc3 * z, axis=0, keepdims=True)
        dsm_ref[9:10, :] += jnp.sum(dc3 * z1, axis=0, keepdims=True)
        dsm_ref[8:9, :] += jnp.sum(dc3 * z2, axis=0, keepdims=True)
        nxt3 = dc3_n[...]
        dz = (small_ref[10:11, :] * dc3 + small_ref[9:10, :] * _future(dc3, nxt3, 1, row)
              + small_ref[8:9, :] * _future(dc3, nxt3, 2, row))
        dc3_n[...] = dc3
        dp_ref[:, C_SC:C_SC + CONV_WIDTH] = (dz * sx).astype(BF16)
        dp_ref[:, C_SX:C_SX + CONV_WIDTH] = (dz * sc).astype(BF16)

    fix = lambda i: (0, 0)
    cur = lambda i: (nT - 1 - i, 0)
    prev_cols = lambda cb: (lambda i: (jnp.maximum(nT - 2 - i, 0), cb))
    return _pcall(
        body, name=name, grid=(nT,),
        in_specs=[pl.BlockSpec((T, IN_PROJ_WIDTH), cur),
                  pl.BlockSpec((BLOCK, 2 * KV_WIDTH),
                               lambda i: (jnp.maximum((nT - 1 - i) * bpt - 1, 0), C_KV // (2 * KV_WIDTH))),
                  pl.BlockSpec((T, LRU_WIDTH), prev_cols(C_LX // LRU_WIDTH)),
                  pl.BlockSpec((T, CONV_WIDTH), prev_cols(C_SC // CONV_WIDTH)),
                  pl.BlockSpec((T, CONV_WIDTH), prev_cols(C_SX // CONV_WIDTH)),
                  pl.BlockSpec((T, D_MODEL), cur),
                  pl.BlockSpec((T, LRU_WIDTH), cur),
                  pl.BlockSpec(memory_space=pltpu.SMEM),
                  pl.BlockSpec((16, LRU_WIDTH), fix),
                  pl.BlockSpec((LRU_WIDTH, LRU_WIDTH), fix),
                  pl.BlockSpec((LRU_WIDTH, LRU_WIDTH), fix)],
        out_specs=[pl.BlockSpec((T, IN_PROJ_WIDTH), cur),
                   pl.BlockSpec((16, LRU_WIDTH), fix),
                   pl.BlockSpec((SUBLANES, LANES), fix),
                   pl.BlockSpec((LRU_WIDTH, LRU_WIDTH), fix),
                   pl.BlockSpec((LRU_WIDTH, LRU_WIDTH), fix)],
        out_shape=[jax.ShapeDtypeStruct((S, IN_PROJ_WIDTH), BF16),
                   jax.ShapeDtypeStruct((16, LRU_WIDTH), F32),
                   jax.ShapeDtypeStruct((SUBLANES, LANES), F32),
                   jax.ShapeDtypeStruct((LRU_WIDTH, LRU_WIDTH), F32),
                   jax.ShapeDtypeStruct((LRU_WIDTH, LRU_WIDTH), F32)],
        scratch_shapes=[pltpu.VMEM((T + BLOCK, KV_WIDTH), F32), pltpu.VMEM((T + BLOCK, KV_WIDTH), F32),
                        pltpu.VMEM((BLOCK, 2 * KV_WIDTH), F32), pltpu.VMEM((T, LRU_WIDTH), F32),
                        pltpu.VMEM((T, CONV_WIDTH), F32), pltpu.VMEM((SUBLANES, LRU_WIDTH), F32)],
        sem=("arbitrary",), args=(proj, proj, proj, proj, proj, dymix, hprev, sinks, small, wa, wx), jobs=jobs)


def _mod_matmul(c_all, w_mod, name):
    L, Dm, N = w_mod.shape
    R = c_all.shape[0]
    tn = 768

    def body(c_ref, w_ref, o_ref, ca_ref):
        cv = c_ref[...]
        ca = (cv * _sigmoid(cv)).astype(BF16)
        ca_ref[...] = ca
        o_ref[0] = _dot(ca, w_ref[0].astype(BF16))

    return pl.pallas_call(
        body, name=name, grid=(L, N // tn),
        in_specs=[pl.BlockSpec((R, Dm), lambda l, n: (0, 0)),
                  pl.BlockSpec((1, Dm, tn), lambda l, n: (l, 0, n))],
        out_specs=[pl.BlockSpec((1, R, tn), lambda l, n: (l, 0, n)), pl.BlockSpec((R, Dm), lambda l, n: (0, 0))],
        out_shape=[jax.ShapeDtypeStruct((L, R, N), F32), jax.ShapeDtypeStruct((R, Dm), BF16)],
        compiler_params=_cp("arbitrary", "arbitrary"),
    )(c_all, w_mod)


def _adamw_update(g, w_ref, m_ref, v_ref, go_ref, d_ref, mo_ref, vo_ref):
    mn = ADAM_B1 * m_ref[...] + (1.0 - ADAM_B1) * g
    vn = ADAM_B2 * v_ref[...] + (1.0 - ADAM_B2) * (g * g)
    go_ref[...] = g
    mo_ref[...] = mn
    vo_ref[...] = vn
    m_hat = mn / (1.0 - ADAM_B1 ** ADAM_STEP)
    v_hat = vn / (1.0 - ADAM_B2 ** ADAM_STEP)
    d_ref[...] = (-ADAM_LR) * (m_hat / (jnp.sqrt(v_hat) + ADAM_EPS) + ADAM_WD * w_ref[...])


def _adamw(w, g, m, v, name):
    R, C = w.shape
    tr = 8
    for cand in (512, 256, 128, 64, 32, 16, 8):
        if R % cand == 0 and cand * C * 4 <= (1 << 20):
            tr = cand
            break

    def body(w_ref, g_ref, *rest):
        _adamw_update(g_ref[...], w_ref, *rest)

    spec = pl.BlockSpec((tr, C), lambda i: (i, 0))
    return _pcall(body, name=name, grid=(R // tr,), in_specs=[spec] * 4, out_specs=[spec] * 4,
                  out_shape=[jax.ShapeDtypeStruct((R, C), F32)] * 4, sem=("arbitrary",), args=(w, g, m, v))


def _adamw_partials(w, partials, m, v, name):
    nl = len(partials)
    _, R, C = partials[0][0].shape
    tr = 8
    for cand in (256, 128, 64, 32, 16):
        if R % cand == 0 and cand * C * 4 <= (1 << 19):
            tr = cand
            break
    ni = R // tr

    def body(*refs):
        w_ref, p_refs = refs[0], refs[1:1 + 2 * nl]
        m_ref, v_ref, go_ref, d_ref, mo_ref, vo_ref = refs[1 + 2 * nl:]
        for l in range(nl):
            @pl.when(pl.program_id(0) == l)
            def _(pair=p_refs[2 * l:2 * l + 2]):
                own, sib = [((p[0].astype(F32) + p[1].astype(F32)) + p[2].astype(F32)) + p[3].astype(F32)
                            for p in pair]
                _adamw_update(own + sib, w_ref, m_ref, v_ref, go_ref, d_ref, mo_ref, vo_ref)

    def slots(l):
        return pl.BlockSpec((N_CHIPS, tr, C),
                            lambda ll, i: (0, jnp.where(ll == l, i, jnp.where(ll < l, 0, ni - 1)), 0))

    spec = pl.BlockSpec((tr, C), lambda ll, i: (ll * ni + i, 0))
    return pl.pallas_call(
        body, name=name, grid=(nl, ni),
        in_specs=[spec] + [slots(l) for l in range(nl) for _ in range(2)] + [spec, spec], out_specs=[spec] * 4,
        out_shape=[jax.ShapeDtypeStruct((nl * R, C), F32)] * 4,
        compiler_params=_cp("arbitrary", "arbitrary"),
    )(w, *[p for pair in partials for p in pair], m, v)


def _all_gather_small(v, name):
    M, N = v.shape

    def body(x_ref, out_ref, sum_ref, send_sems, recv_sems, local_sem):
        x, y, c = lax.axis_index("x"), lax.axis_index("y"), lax.axis_index("c")
        me, sibling = (x, y, c), (x, y, 1 - c)
        chips = [(1 - x, y), (x, 1 - y), (1 - x, 1 - y)]

        def rows(px, py, pc):
            return out_ref.at[pl.ds(pl.multiple_of((4 * px + 2 * py + pc) * M, SUBLANES), M), :]

        def copy(k, block, to, src=None):
            return pltpu.make_async_remote_copy(
                src_ref=rows(*block) if src is None else src, dst_ref=rows(*block),
                send_sem=send_sems.at[k], recv_sem=recv_sems.at[k], device_id=to, device_id_type=MESH)

        mine = pltpu.make_async_copy(x_ref, rows(*me), local_sem)
        mine.start()
        first = [copy(0, me, sibling, src=x_ref)]
        first += [copy(1 + j, me, (*chip, c), src=x_ref) for j, chip in enumerate(chips)]
        for cp in first:
            cp.start()
        passed = [copy(4 + j, (*chip, c), sibling) for j, chip in enumerate(chips)]
        for j, chip in enumerate(chips):
            copy(1 + j, (*chip, c), me).wait_recv()
            passed[j].start()
        copy(0, sibling, me).wait_recv()
        for j, chip in enumerate(chips):
            copy(4 + j, (*chip, 1 - c), me).wait_recv()
        for cp in first + passed:
            cp.wait_send()
        mine.wait()
        acc = out_ref[0:M, :]
        for d in range(1, N_DEV):
            acc = acc + out_ref[d * M:(d + 1) * M, :]
        sum_ref[...] = acc

    return pl.pallas_call(
        body, name=name,
        out_shape=[jax.ShapeDtypeStruct((N_DEV * M, N), F32), jax.ShapeDtypeStruct((M, N), F32)],
        in_specs=[pl.BlockSpec(memory_space=pltpu.VMEM)],
        out_specs=[pl.BlockSpec(memory_space=pltpu.VMEM), pl.BlockSpec(memory_space=pltpu.VMEM)],
        scratch_shapes=[pltpu.SemaphoreType.DMA((7,)), pltpu.SemaphoreType.DMA((7,)), pltpu.SemaphoreType.DMA],
        compiler_params=pltpu.CompilerParams(vmem_limit_bytes=VMEM_LIMIT),
    )(v)


_BIG = (("w_ffn1_gu", 1), ("w_ffn1_down", 0), ("w_ffn2_gu", 1), ("w_ffn2_down", 0), ("w_in", 1), ("w_out", 0))
_AXIS = dict(_BIG)

_GATHER_PLAN = {
    "first": [(0, "w_ffn1_gu")],
    (0, "ffn1_gu"): [(0, "w_ffn1_down"), (0, "w_in"), (0, "w_out")],
    (0, "ffn1_down"): [(1, "w_ffn1_down")],
    (0, "mix_in"): [(0, "w_ffn2_down")],
    (0, "mix_core"): [(0, "w_ffn2_gu")],
    (0, "mix_out"): [(1, "w_in")],
    (0, "ffn2_gu"): [(1, "w_ffn1_gu")],
    (0, "ffn2_down"): [(1, "w_out")],
    (1, "ffn1_gu"): [(1, "w_ffn2_gu")],
    (1, "ffn1_down"): [(1, "w_ffn2_down")],
}


def _pack(arrs, rows_multiple=SUBLANES):
    flat = jnp.concatenate([a.astype(F32).reshape(-1) for a in arrs])
    unit = rows_multiple * LANES
    total = -(-flat.shape[0] // unit) * unit
    return jnp.pad(flat, (0, total - flat.shape[0])).reshape(total // LANES, LANES)


def _unpack(flat, shapes):
    out, off = [], 0
    for shp in shapes:
        n = int(math.prod(shp))
        out.append(flat[off:off + n].reshape(shp))
        off += n
    return out


def _block_diag(w):
    out = jnp.zeros((LRU_WIDTH, LRU_WIDTH), F32)
    for h in range(4):
        out = lax.dynamic_update_slice(out, w[h], (h * HEAD_DIM, h * HEAD_DIM))
    return out


def _diag_blocks(w):
    return jnp.stack([w[h * HEAD_DIM:(h + 1) * HEAD_DIM, h * HEAD_DIM:(h + 1) * HEAD_DIM] for h in range(4)])


def _rows8(*rows):
    z = jnp.zeros((8 - len(rows), rows[0].shape[-1]), F32)
    return jnp.concatenate([jnp.stack(rows), z], axis=0)


def kernel(x, c, w_mod, b_mod, g_norm, w_ffn1_gu, w_ffn1_down, w_ffn2_gu, w_ffn2_down, w_in, w_out, attn_sinks, lru_conv_w, lru_conv_b, lru_gate_a_w, lru_gate_a_b, lru_gate_x_w, lru_gate_x_b, lru_lambda, sc_conv_w, g_final, loss_target, m_w_mod, m_b_mod, m_g_norm, m_w_ffn1_gu, m_w_ffn1_down, m_w_ffn2_gu, m_w_ffn2_down, m_w_in, m_w_out, m_attn_sinks, m_lru_conv_w, m_lru_conv_b, m_lru_gate_a_w, m_lru_gate_a_b, m_lru_gate_x_w, m_lru_gate_x_b, m_lru_lambda, m_sc_conv_w, m_g_final, v_w_mod, v_b_mod, v_g_norm, v_w_ffn1_gu, v_w_ffn1_down, v_w_ffn2_gu, v_w_ffn2_down, v_w_in, v_w_out, v_attn_sinks, v_lru_conv_w, v_lru_conv_b, v_lru_gate_a_w, v_lru_gate_a_b, v_lru_gate_x_w, v_lru_gate_x_b, v_lru_lambda, v_sc_conv_w, v_g_final):
    W = dict(w_mod=w_mod, b_mod=b_mod, g_norm=g_norm, w_ffn1_gu=w_ffn1_gu, w_ffn1_down=w_ffn1_down,
             w_ffn2_gu=w_ffn2_gu, w_ffn2_down=w_ffn2_down, w_in=w_in, w_out=w_out, attn_sinks=attn_sinks,
             lru_conv_w=lru_conv_w, lru_conv_b=lru_conv_b, lru_gate_a_w=lru_gate_a_w, lru_gate_a_b=lru_gate_a_b,
             lru_gate_x_w=lru_gate_x_w, lru_gate_x_b=lru_gate_x_b, lru_lambda=lru_lambda, sc_conv_w=sc_conv_w,
             g_final=g_final)
    M1 = dict(w_mod=m_w_mod, b_mod=m_b_mod, g_norm=m_g_norm, w_ffn1_gu=m_w_ffn1_gu, w_ffn1_down=m_w_ffn1_down,
              w_ffn2_gu=m_w_ffn2_gu, w_ffn2_down=m_w_ffn2_down, w_in=m_w_in, w_out=m_w_out,
              attn_sinks=m_attn_sinks, lru_conv_w=m_lru_conv_w, lru_conv_b=m_lru_conv_b,
              lru_gate_a_w=m_lru_gate_a_w, lru_gate_a_b=m_lru_gate_a_b, lru_gate_x_w=m_lru_gate_x_w,
              lru_gate_x_b=m_lru_gate_x_b, lru_lambda=m_lru_lambda, sc_conv_w=m_sc_conv_w, g_final=m_g_final)
    V1 = dict(w_mod=v_w_mod, b_mod=v_b_mod, g_norm=v_g_norm, w_ffn1_gu=v_w_ffn1_gu, w_ffn1_down=v_w_ffn1_down,
              w_ffn2_gu=v_w_ffn2_gu, w_ffn2_down=v_w_ffn2_down, w_in=v_w_in, w_out=v_w_out,
              attn_sinks=v_attn_sinks, lru_conv_w=v_lru_conv_w, lru_conv_b=v_lru_conv_b,
              lru_gate_a_w=v_lru_gate_a_w, lru_gate_a_b=v_lru_gate_a_b, lru_gate_x_w=v_lru_gate_x_w,
              lru_gate_x_b=v_lru_gate_x_b, lru_lambda=v_lru_lambda, sc_conv_w=v_sc_conv_w, g_final=v_g_final)
    names = ["w_mod", "b_mod", "g_norm", "w_ffn1_gu", "w_ffn1_down", "w_ffn2_gu", "w_ffn2_down", "w_in", "w_out",
             "attn_sinks", "lru_conv_w", "lru_conv_b", "lru_gate_a_w", "lru_gate_a_b", "lru_gate_x_w",
             "lru_gate_x_b", "lru_lambda", "sc_conv_w", "g_final"]

    xs = x[0]
    tgt = loss_target[0]
    S = xs.shape[0]
    chip = 2 * lax.axis_index("x") + lax.axis_index("y")
    batch = 2 * chip + lax.axis_index("c")
    L = DEPTH

    fwd_shapes = [(D_MODEL,), g_norm.shape, lru_conv_w.shape, sc_conv_w.shape]
    gathered, _ = _all_gather_small(_pack([c[0], g_norm, lru_conv_w, sc_conv_w]), "gather_small_fwd")
    gathered = gathered.reshape(N_DEV, -1)
    c_all = gathered[:, :D_MODEL]
    per_chip = [_unpack(gathered[2 * jj], fwd_shapes) for jj in range(N_CHIPS)]
    g_norm_full = jnp.concatenate([p[1] for p in per_chip], axis=-1)
    lru_conv_w_full = jnp.concatenate([p[2] for p in per_chip], axis=-1)
    sc_conv_w_full = jnp.concatenate([p[3] for p in per_chip], axis=-1)

    c_pad = jnp.concatenate([c_all, jnp.zeros_like(c_all)], axis=0)
    mod_part, c_act = _mod_matmul(c_pad, w_mod, "mod_matmul")
    mod_all, _ = _all_gather_small(mod_part.reshape(-1, LANES), "gather_mod")
    mod_all = mod_all.reshape(N_DEV, L, 16, -1)
    mod_rows = [lax.dynamic_index_in_dim(mod_all[2 * jj], batch, axis=1, keepdims=False) for jj in range(N_CHIPS)]
    mod = (jnp.concatenate(mod_rows, axis=-1) + b_mod).reshape(L, 9, D_MODEL)

    def nrm_rows(l, s):
        return _rows8(g_norm_full[l, s], mod[l, 3 * s], mod[l, 3 * s + 1], mod[l, 3 * s + 2])

    full = {}

    def gather_jobs(key):
        return [_GatherJob(W[n][l].astype(BF16), _AXIS[n]) for l, n in _GATHER_PLAN.get(key, ())]

    def landed(key, outs):
        full.update(zip(_GATHER_PLAN.get(key, ()), outs))

    landed("first", _comm_only(gather_jobs("first"), "gather_first"))

    def mixer_params(l):
        small = jnp.concatenate([lru_conv_w_full[l], lru_conv_b[l][None], lru_gate_a_b[l][None],
                                 lru_gate_x_b[l][None], lru_lambda[l][None], sc_conv_w_full[l],
                                 jnp.zeros((5, LRU_WIDTH), F32)], axis=0)
        return (attn_sinks[l], small, _block_diag(lru_gate_a_w[l]).astype(BF16),
                _block_diag(lru_gate_x_w[l]).astype(BF16))

    saved = []
    xcur = xs
    for l in range(L):
        n1, n2, n3 = nrm_rows(l, 0), nrm_rows(l, 1), nrm_rows(l, 2)

        def ffn(which, xin, nrm):
            key = (l, which + "_gu")
            (gu, h), ex = _norm_matmul(xin, nrm, full[(l, f"w_{which}_gu")], BF16, f"l{l}_{which}_gu", gather_jobs(key))
            landed(key, ex)
            key = (l, which + "_down")
            (xo, y), ex = _proj_residual(gu, full[(l, f"w_{which}_down")], xin, nrm, 0.5, True, f"l{l}_{which}_down",
                                         gather_jobs(key))
            landed(key, ex)
            return xo, (xin, h, gu, y)

        x1, s1 = ffn("ffn1", xcur, n1)
        (proj, h2), ex = _norm_matmul(x1, n2, full[(l, "w_in")], F32, f"l{l}_mix_in", gather_jobs((l, "mix_in")))
        landed((l, "mix_in"), ex)
        mp = mixer_params(l)
        (ymix, hprev), ex = _mixer_fwd(proj, *mp, f"l{l}_mix_core", gather_jobs((l, "mix_core")))
        landed((l, "mix_core"), ex)
        (x2, ymo), ex = _proj_residual(ymix, full[(l, "w_out")], x1, n2, 1.0, False, f"l{l}_mix_out",
                                       gather_jobs((l, "mix_out")))
        landed((l, "mix_out"), ex)
        s2 = (x1, h2, proj, ymix, ymo, hprev, mp)
        xcur, s3 = ffn("ffn2", x2, n3)
        saved.append((n1, n2, n3, s1, s2, s3))

    dx, stats = _final_loss(xcur, _rows8(g_final), tgt, "final_loss")
    loss = lax.psum(stats[1, 0], ("x", "y", "c"))
    d_g_final = stats[0]

    recv, theirs = {}, {}
    waiting = []

    def carried(fn, *a, extra=()):
        items = waiting + list(extra)
        waiting.clear()
        outs, landed_now = fn(*a, jobs=[_SiblingJob(recv[(ll, n)]) if g is None else _ScatterJob(g, _AXIS[n])
                                        for ll, n, g in items])
        for (ll, n, g), arr in zip(items, landed_now):
            if g is None:
                theirs[(ll, n)] = arr
            else:
                recv[(ll, n)] = arr
                waiting.append((ll, n, None))
        return outs

    dmod, d_gnorm, d_small = [None] * L, [None] * L, [None] * L
    for l in reversed(range(L)):
        n1, n2, n3, s1, s2, s3 = saved[l]

        def ffn_bwd(which, dxo, sv, nrm, hold_last):
            xin, h, gu, y = sv
            tag = f"l{l}_{which}"
            dgu, dgate, dw_down = carried(_proj_residual_bwd, dxo, gu, y, full[(l, f"w_{which}_down")], nrm, 0.5,
                                          True, tag + "_down_bwd")
            dw_gu = carried(_atb, h, dgu, BF16, 1024, 2816, tag + "_dw_gu", extra=[(l, f"w_{which}_down", dw_down)])
            mine = [(l, f"w_{which}_gu", dw_gu)]
            dxi, red = carried(_nt_norm_bwd, dgu, full[(l, f"w_{which}_gu")], xin, nrm, dxo, tag + "_gu_bwd",
                               extra=[] if hold_last else mine)
            if hold_last:
                waiting.extend(mine)
            return dxi, (red[0], red[1], dgate[0]), red[2]

        dx, dm3, dg3 = ffn_bwd("ffn2", dx, s3, n3, True)
        x_in, h2, proj, ymix, ymo, hprev, mp = s2
        (dymix, dgate, dw_out), _ = _proj_residual_bwd(dx, ymix, ymo, full[(l, "w_out")], n2, 1.0, False,
                                                       f"l{l}_mix_out_bwd")
        dproj, dsm, dsink, dwa, dwx = carried(_mixer_bwd, proj, dymix, hprev, *mp, f"l{l}_mix_core_bwd")
        dw_in = carried(_atb, h2, dproj, BF16, 1024, 2048, f"l{l}_dw_in", extra=[(l, "w_out", dw_out)])
        dx, red = carried(_nt_norm_bwd, dproj, full[(l, "w_in")], x_in, n2, dx, f"l{l}_mix_in_bwd",
                          extra=[(l, "w_in", dw_in)])
        dm2, dg2 = (red[0], red[1], dgate[0]), red[2]
        dx, dm1, dg1 = ffn_bwd("ffn1", dx, s1, n1, l > 0)
        dmod[l] = jnp.stack(list(dm1) + list(dm2) + list(dm3))
        d_gnorm[l] = jnp.stack([dg1, dg2, dg3])
        d_small[l] = (dsink[:, 0], dsm[0:4], dsm[4], _diag_blocks(dwa), dsm[5], _diag_blocks(dwx), dsm[6],
                      dsm[7], dsm[8:11])
    grad_x = dx[None]

    def both(k):
        return jnp.stack([d_small[0][k], d_small[1][k]])
    small_names = ["g_norm", "attn_sinks", "lru_conv_w", "lru_conv_b", "lru_gate_a_w", "lru_gate_a_b",
                   "lru_gate_x_w", "lru_gate_x_b", "lru_lambda", "sc_conv_w", "g_final"]
    small_parts = [jnp.stack(d_gnorm)] + [both(k) for k in range(9)] + [d_g_final]
    dmod_flat = jnp.stack(dmod).reshape(-1)
    bwd_gathered, bwd_sum = _all_gather_small(_pack([dmod_flat] + small_parts), "gather_small_bwd")
    n_mod = dmod_flat.shape[0]
    dmod_all = bwd_gathered.reshape(N_DEV, -1)[:, :n_mod].reshape(N_DEV, L, 9 * D_MODEL)
    bwd_sum = bwd_sum.reshape(-1)
    G = {"b_mod": bwd_sum[:n_mod].reshape(L, 9 * D_MODEL)}
    G.update(zip(small_names, _unpack(bwd_sum[n_mod:], [p.shape for p in small_parts])))
    for n in ("g_norm", "lru_conv_w", "sc_conv_w"):
        wdt = W[n].shape[-1]
        G[n] = lax.dynamic_slice_in_dim(G[n], chip * wdt, wdt, axis=G[n].ndim - 1)

    ncol = w_mod.shape[-1]
    dmod_cols = lax.dynamic_slice_in_dim(dmod_all, chip * ncol, ncol, axis=2)
    zeros8 = jnp.zeros((N_DEV, ncol), F32)
    g_w_mod = jnp.stack([carried(_atb, c_act, jnp.concatenate([dmod_cols[:, l], zeros8], axis=0).astype(BF16), F32,
                                 D_MODEL, 768, f"l{l}_dw_mod") for l in range(L)])

    out_g, out_d, out_m, out_v = {}, {}, {}, {}
    res, _ = _adamw(w_mod.reshape(-1, ncol), g_w_mod.reshape(-1, ncol), m_w_mod.reshape(-1, ncol),
                    v_w_mod.reshape(-1, ncol), "adamw_w_mod")
    out_g["w_mod"], out_d["w_mod"], out_m["w_mod"], out_v["w_mod"] = [r.reshape(w_mod.shape) for r in res]
    for n, _ in _BIG:
        shp = W[n].shape
        flat = (shp[0] * shp[1], shp[2])
        res = _adamw_partials(W[n].reshape(flat), [(recv[(l, n)], theirs[(l, n)]) for l in range(L)],
                              M1[n].reshape(flat), V1[n].reshape(flat), f"adamw_{n}")
        out_g[n], out_d[n], out_m[n], out_v[n] = [r.reshape(shp) for r in res]
    rest = ["b_mod"] + small_names
    shapes = [W[n].shape for n in rest]
    res, _ = _adamw(_pack([W[n] for n in rest]), _pack([G[n] for n in rest]), _pack([M1[n] for n in rest]),
                    _pack([V1[n] for n in rest]), "adamw_small")
    for dst, r in zip((out_g, out_d, out_m, out_v), res):
        dst.update(zip(rest, _unpack(r.reshape(-1), shapes)))

    return (loss, grad_x, *[out_g[n] for n in names], *[out_d[n] for n in names],
            *[out_m[n] for n in names], *[out_v[n] for n in names])
```

```python
import math

import jax
import jax.numpy as jnp
from jax import lax
from jax.experimental import pallas as pl
from jax.experimental.pallas import tpu as pltpu

F32 = jnp.float32
BF16 = jnp.bfloat16

D_MODEL = 1024
DEPTH = 2
HEAD_DIM = 64
N_Q_HEADS = 8
ATTN_WIDTH = 512
KV_WIDTH = 128
LRU_WIDTH = 256
CONV_WIDTH = 256
IN_PROJ_WIDTH = 2048
BLOCK = 128
D_FF = 2816
EPS = 1e-6
NEG_INF = -1e30
LRU_C = 8.0
N_CHIPS = 4
N_DEV = 8

C_Q, C_KV, C_LX, C_LG, C_SB, C_SC, C_SX = 0, 512, 768, 1024, 1280, 1536, 1792

ADAM_LR = 0.001
ADAM_B1 = 0.9
ADAM_B2 = 0.999
ADAM_EPS = 1e-08
ADAM_WD = 0.01
ADAM_STEP = 10

LANES = 128
SUBLANES = 8
VMEM_LIMIT = 56 * 1024 * 1024
MIX_TILE = 256

MESH = pl.DeviceIdType.MESH


def _cp(*sem):
    return pltpu.CompilerParams(dimension_semantics=sem, vmem_limit_bytes=VMEM_LIMIT)


def _tile(n, pref):
    t = min(n, pref)
    while n % t:
        t //= 2
    return t


MXU_DIM = 256


def _col_chunk(n):
    return max(c for c in range(MXU_DIM, 2816 + 1, MXU_DIM) if n % c == 0)


def _resident(shape):
    return pl.BlockSpec(shape, lambda *_: (0, 0), pipeline_mode=pl.Buffered(1))


def _sigmoid(v):
    return 1.0 / (1.0 + jnp.exp(-v))


def _expm1(v):
    series = v * (1.0 + v * (0.5 + v * (1.0 / 6.0)))
    return jnp.where(v > -0.01, series, jnp.exp(v) - 1.0)


def _softplus_neg(lam):
    e = jnp.exp(-jnp.abs(lam))
    log1p = jnp.where(e < 1e-2, e * (1.0 - e * (0.5 - e * (1.0 / 3.0))), jnp.log(1.0 + e))
    return jnp.maximum(-lam, 0.0) + log1p


_GELU_K = math.sqrt(2.0 / math.pi)
_GELU_C = 0.044715


def _gelu(v):
    t = jnp.tanh(_GELU_K * (v + _GELU_C * v * v * v))
    return 0.5 * v * (1.0 + t), t


def _gelu_grad(v, t):
    return 0.5 * (1.0 + t) + 0.5 * v * (1.0 - t * t) * _GELU_K * (1.0 + 3.0 * _GELU_C * v * v)


def _dot(a, b):
    return jnp.dot(a, b, preferred_element_type=F32)


def _dot_nt(a, b):
    return lax.dot_general(a, b, (((1,), (1,)), ((), ())), preferred_element_type=F32)


def _dot_tn(a, b):
    return lax.dot_general(a, b, (((0,), (0,)), ((), ())), preferred_element_type=F32)


def _window(ref, axis, j, width):
    start = pl.multiple_of(j * width, LANES if axis == 1 else 16)
    if axis == 1:
        return ref.at[:, pl.ds(start, width)]
    return ref.at[pl.ds(start, width), :]


def _chip_peers():
    x, y, c = lax.axis_index("x"), lax.axis_index("y"), lax.axis_index("c")
    return x, y, c, [(1 - x, y), (x, 1 - y), (1 - x, 1 - y)]


class _GatherJob:
    def __init__(self, shard, axis):
        self.src, self.axis, self.width, self.half = shard, axis, shard.shape[axis], shard.shape[0] // 2
        full = tuple(d * N_CHIPS if k == axis else d for k, d in enumerate(shard.shape))
        self.out_shape = jax.ShapeDtypeStruct(full, shard.dtype)

    def _piece(self, ref, j, hf):
        if self.axis == 1:
            return ref.at[pl.ds(pl.multiple_of(hf * self.half, 16), self.half),
                          pl.ds(pl.multiple_of(j * self.width, LANES), self.width)]
        return ref.at[pl.ds(pl.multiple_of(j * self.width + hf * self.half, 16), self.half), :]

    def _copies(self, src, dst, send, recv, loc, t):
        x, y, c, chips = _chip_peers()
        j = 2 * x + y
        owners = [2 * px + py for px, py in chips]
        local = pltpu.make_async_copy(src, _window(dst, self.axis, j, self.width), loc.at[t])
        mine = src.at[pl.ds(pl.multiple_of(c * self.half, 16), self.half), :]

        def ici(k, owner):
            return pltpu.make_async_remote_copy(
                src_ref=mine, dst_ref=self._piece(dst, owner, c), send_sem=send.at[JOB_SEMS * t + k],
                recv_sem=recv.at[JOB_SEMS * t + k], device_id=(*chips[k], c), device_id_type=MESH)

        def relay(k, hf):
            piece = self._piece(dst, owners[k], hf)
            return pltpu.make_async_remote_copy(
                src_ref=piece, dst_ref=piece, send_sem=send.at[JOB_SEMS * t + 4 + k],
                recv_sem=recv.at[JOB_SEMS * t + 4 + k],
                device_id=(x, y, 1 - c), device_id_type=MESH)

        return (local, [ici(k, j) for k in range(3)], [ici(k, owners[k]) for k in range(3)],
                [relay(k, c) for k in range(3)], [relay(k, 1 - c) for k in range(3)])

    def start(self, *a):
        local, ici_out, _, _, _ = self._copies(*a)
        local.start()
        for cp in ici_out:
            cp.start()

    def relay(self, *a):
        _, _, ici_in, relay_out, _ = self._copies(*a)
        for arrived, onward in zip(ici_in, relay_out):
            arrived.wait_recv()
            onward.start()

    def finish(self, *a):
        local, ici_out, _, relay_out, relay_in = self._copies(*a)
        for cp in relay_in:
            cp.wait_recv()
        for cp in ici_out + relay_out:
            cp.wait_send()
        local.wait()


class _ScatterJob:
    def __init__(self, full, axis):
        self.src, self.axis, self.width = full, axis, full.shape[axis] // N_CHIPS
        shard = tuple(self.width if k == axis else d for k, d in enumerate(full.shape))
        self.out_shape = jax.ShapeDtypeStruct((N_CHIPS,) + shard, full.dtype)

    def _copies(self, src, dst, send, recv, loc, t):
        x, y, c, chips = _chip_peers()
        local = pltpu.make_async_copy(_window(src, self.axis, 2 * x + y, self.width), dst.at[3], loc.at[t])
        sends = [pltpu.make_async_remote_copy(
            src_ref=_window(src, self.axis, 2 * px + py, self.width), dst_ref=dst.at[k],
            send_sem=send.at[JOB_SEMS * t + k], recv_sem=recv.at[JOB_SEMS * t + k], device_id=(px, py, c),
            device_id_type=MESH) for k, (px, py) in enumerate(chips)]
        return local, sends

    def start(self, *a):
        local, sends = self._copies(*a)
        local.start()
        for cp in sends:
            cp.start()

    def relay(self, *a):
        pass

    def finish(self, *a):
        local, sends = self._copies(*a)
        for cp in sends:
            cp.wait_recv()
        for cp in sends:
            cp.wait_send()
        local.wait()


class _SiblingJob:
    def __init__(self, arr):
        self.src, self.out_shape = arr, jax.ShapeDtypeStruct(arr.shape, arr.dtype)

    def _copy(self, src, dst, send, recv, loc, t):
        x, y, c = lax.axis_index("x"), lax.axis_index("y"), lax.axis_index("c")
        return pltpu.make_async_remote_copy(
            src_ref=src, dst_ref=dst, send_sem=send.at[JOB_SEMS * t], recv_sem=recv.at[JOB_SEMS * t],
            device_id=(x, y, 1 - c), device_id_type=MESH)

    def start(self, *a):
        self._copy(*a).start()

    def relay(self, *a):
        pass

    def finish(self, *a):
        self._copy(*a).wait()


JOB_SEMS = 8


def _run_jobs(phase, jobs, srcs, dsts, sems):
    for t, job in enumerate(jobs):
        getattr(job, phase)(srcs[t], dsts[t], *sems, t)


def _job_scratch(n):
    return [pltpu.SemaphoreType.DMA((JOB_SEMS * n,)), pltpu.SemaphoreType.DMA((JOB_SEMS * n,)),
            pltpu.SemaphoreType.DMA((n,))]


def _pcall(body, *, name, grid, in_specs, out_specs, out_shape, sem, args, scratch_shapes=(), jobs=()):
    in_specs, out_specs, out_shape = list(in_specs), list(out_specs), list(out_shape)
    scratch_shapes = list(scratch_shapes)
    if not jobs:
        res = pl.pallas_call(body, name=name, grid=grid, in_specs=in_specs, out_specs=out_specs, out_shape=out_shape,
                             scratch_shapes=scratch_shapes, compiler_params=_cp(*sem))(*args)
        return list(res), []
    n_in, n_out, n_scr, nj = len(args), len(out_shape), len(scratch_shapes), len(jobs)
    n_steps = math.prod(grid)
    relay_step = (3 * n_steps) // 4
    relay_early = 0 < relay_step < n_steps - 1

    def wrapped(*refs):
        ins, refs = refs[:n_in], refs[n_in:]
        jin, refs = refs[:nj], refs[nj:]
        outs, refs = refs[:n_out], refs[n_out:]
        jout, refs = refs[:nj], refs[nj:]
        scr, sems = refs[:n_scr], refs[n_scr:]
        step = pl.program_id(0)
        for d in range(1, len(grid)):
            step = step * grid[d] + pl.program_id(d)

        @pl.when(step == 0)
        def _():
            _run_jobs("start", jobs, jin, jout, sems)

        if relay_early:
            @pl.when(step == relay_step)
            def _():
                _run_jobs("relay", jobs, jin, jout, sems)
        body(*ins, *outs, *scr)

        @pl.when(step == n_steps - 1)
        def _():
            if not relay_early:
                _run_jobs("relay", jobs, jin, jout, sems)
            _run_jobs("finish", jobs, jin, jout, sems)

    hbm = pl.BlockSpec(memory_space=pltpu.HBM)
    res = pl.pallas_call(
        wrapped, name=name, grid=grid, in_specs=in_specs + [hbm] * nj, out_specs=out_specs + [hbm] * nj,
        out_shape=out_shape + [job.out_shape for job in jobs], scratch_shapes=scratch_shapes + _job_scratch(nj),
        compiler_params=_cp(*sem))(*args, *[job.src for job in jobs])
    return list(res[:n_out]), list(res[n_out:])


def _comm_only(jobs, name):
    nj = len(jobs)

    def body(*refs):
        srcs, dsts, sems = refs[:nj], refs[nj:2 * nj], refs[2 * nj:]
        for phase in ("start", "relay", "finish"):
            _run_jobs(phase, jobs, srcs, dsts, sems)

    hbm = pl.BlockSpec(memory_space=pltpu.HBM)
    return list(pl.pallas_call(
        body, name=name, in_specs=[hbm] * nj, out_specs=[hbm] * nj, out_shape=[job.out_shape for job in jobs],
        scratch_shapes=_job_scratch(nj))(*[job.src for job in jobs]))


def _norm_matmul(x, nrm, w, out_dtype, name, jobs=()):
    S, Dm = x.shape
    N = w.shape[1]
    tm = _tile(S, 512)
    cw = _col_chunk(N)

    def body(x_ref, nrm_ref, w_ref, o_ref, h_ref):
        xv = x_ref[...]
        rstd = lax.rsqrt(jnp.mean(xv * xv, axis=-1, keepdims=True) + EPS)
        hn = (xv * rstd) * nrm_ref[0:1, :]
        hb = (hn * (1.0 + nrm_ref[2:3, :]) + nrm_ref[1:2, :]).astype(BF16)
        h_ref[...] = hb
        for n in range(N // cw):
            o_ref[:, n * cw:(n + 1) * cw] = _dot(hb, w_ref[:, n * cw:(n + 1) * cw]).astype(o_ref.dtype)

    return _pcall(
        body, name=name, grid=(S // tm,),
        in_specs=[pl.BlockSpec((tm, Dm), lambda i: (i, 0)),
                  pl.BlockSpec((8, Dm), lambda i: (0, 0)),
                  _resident((Dm, N))],
        out_specs=[pl.BlockSpec((tm, N), lambda i: (i, 0)),
                   pl.BlockSpec((tm, Dm), lambda i: (i, 0))],
        out_shape=[jax.ShapeDtypeStruct((S, N), out_dtype), jax.ShapeDtypeStruct((S, Dm), BF16)],
        sem=("arbitrary",), args=(x, nrm, w), jobs=jobs)


def _hidden_chunks(k):
    return [(c0, min(6 * MXU_DIM, k - c0)) for c0 in range(0, k, 6 * MXU_DIM)]


def _ffn_fwd(x, nrm, w_gu, w_down, name, jobs=()):
    S, Dm = x.shape
    K = w_down.shape[0]
    tm = _tile(S, 256)

    def body(x_ref, nrm_ref, wgu_ref, wdn_ref, o_ref, h_ref, gu_ref, y_ref):
        xv = x_ref[...]
        rstd = lax.rsqrt(jnp.mean(xv * xv, axis=-1, keepdims=True) + EPS)
        hn = (xv * rstd) * nrm_ref[0:1, :]
        hb = (hn * (1.0 + nrm_ref[2:3, :]) + nrm_ref[1:2, :]).astype(BF16)
        h_ref[...] = hb
        y = jnp.zeros((tm, Dm), F32)
        for c0, cs in _hidden_chunks(K):
            g = _dot(hb, wgu_ref[:, c0:c0 + cs])
            u = _dot(hb, wgu_ref[:, K + c0:K + c0 + cs])
            gu_ref[:, c0:c0 + cs] = g.astype(BF16)
            gu_ref[:, K + c0:K + c0 + cs] = u.astype(BF16)
            y = y + _dot((g * _sigmoid(g) * u).astype(BF16), wdn_ref[c0:c0 + cs, :])
        o_ref[...] = xv + (0.5 * nrm_ref[3:4, :]) * y
        y_ref[...] = y.astype(BF16)

    row = lambda i: (i, 0)
    return _pcall(
        body, name=name, grid=(S // tm,),
        in_specs=[pl.BlockSpec((tm, Dm), row), pl.BlockSpec((8, Dm), lambda i: (0, 0)),
                  _resident((Dm, 2 * K)), _resident((K, Dm))],
        out_specs=[pl.BlockSpec((tm, Dm), row), pl.BlockSpec((tm, Dm), row), pl.BlockSpec((tm, 2 * K), row),
                   pl.BlockSpec((tm, Dm), row)],
        out_shape=[jax.ShapeDtypeStruct((S, Dm), F32), jax.ShapeDtypeStruct((S, Dm), BF16),
                   jax.ShapeDtypeStruct((S, 2 * K), BF16), jax.ShapeDtypeStruct((S, Dm), BF16)],
        sem=("arbitrary",), args=(x, nrm, w_gu, w_down), jobs=jobs)


def _proj_residual(a, w, x, nrm, name, jobs=()):
    S, K = a.shape
    Dm = w.shape[1]
    tm = _tile(S, 256)

    def body(a_ref, w_ref, x_ref, nrm_ref, o_ref, y_ref):
        y = _dot(a_ref[...], w_ref[...])
        o_ref[...] = x_ref[...] + nrm_ref[3:4, :] * y
        y_ref[...] = y.astype(BF16)

    return _pcall(
        body, name=name, grid=(S // tm,),
        in_specs=[pl.BlockSpec((tm, K), lambda i: (i, 0)),
                  _resident((K, Dm)),
                  pl.BlockSpec((tm, Dm), lambda i: (i, 0)),
                  pl.BlockSpec((8, Dm), lambda i: (0, 0))],
        out_specs=[pl.BlockSpec((tm, Dm), lambda i: (i, 0)),
                   pl.BlockSpec((tm, Dm), lambda i: (i, 0))],
        out_shape=[jax.ShapeDtypeStruct((S, Dm), F32), jax.ShapeDtypeStruct((S, Dm), BF16)],
        sem=("arbitrary",), args=(a, w, x, nrm), jobs=jobs)


def _proj_residual_bwd(dxo, a, y, w, nrm, coef, swiglu, name, jobs=()):
    S, Dm = dxo.shape
    K = w.shape[0]
    Ka = a.shape[1]
    tm = _tile(S, 256)
    n_steps = S // tm
    chunks = _hidden_chunks(K)

    def body(dxo_ref, y_ref, w_ref, nrm_ref, a_ref, da_ref, dgate_ref, dw_ref, acc):
        @pl.when(pl.program_id(0) == 0)
        def _():
            dgate_ref[...] = jnp.zeros_like(dgate_ref)
            acc[...] = jnp.zeros_like(acc)

        dxo_v = dxo_ref[...]
        dyb = ((coef * nrm_ref[3:4, :]) * dxo_v).astype(BF16)
        dgate_ref[0:1, :] += jnp.sum(coef * y_ref[...].astype(F32) * dxo_v, axis=0, keepdims=True)
        for c0, cs in chunks:
            dact = _dot_nt(dyb, w_ref[c0:c0 + cs, :])
            if swiglu:
                g = a_ref[:, c0:c0 + cs].astype(F32)
                u = a_ref[:, K + c0:K + c0 + cs].astype(F32)
                s = _sigmoid(g)
                si = g * s
                da_ref[:, c0:c0 + cs] = (dact * u * (s * (1.0 + g * (1.0 - s)))).astype(BF16)
                da_ref[:, K + c0:K + c0 + cs] = (dact * si).astype(BF16)
                act = (si * u).astype(BF16)
            else:
                da_ref[:, c0:c0 + cs] = dact
                act = a_ref[:, c0:c0 + cs]
            acc[c0:c0 + cs, :] += _dot_tn(act, dyb)

        @pl.when(pl.program_id(0) == n_steps - 1)
        def _():
            dw_ref[...] = acc[...].astype(BF16)

    row = lambda i: (i, 0)
    fix = lambda i: (0, 0)
    return _pcall(
        body, name=name, grid=(n_steps,),
        in_specs=[pl.BlockSpec((tm, Dm), row), pl.BlockSpec((tm, Dm), row), _resident((K, Dm)),
                  pl.BlockSpec((8, Dm), fix), pl.BlockSpec((tm, Ka), row)],
        out_specs=[pl.BlockSpec((tm, Ka), row), pl.BlockSpec((8, Dm), fix), _resident((K, Dm))],
        out_shape=[jax.ShapeDtypeStruct((S, Ka), BF16 if swiglu else F32), jax.ShapeDtypeStruct((8, Dm), F32),
                   jax.ShapeDtypeStruct((K, Dm), BF16)],
        scratch_shapes=[pltpu.VMEM((K, Dm), F32)],
        sem=("arbitrary",), args=(dxo, y, w, nrm, a), jobs=jobs)


def _atb(a, b, out_dtype, bm, bn, name, jobs=()):
    S, M = a.shape
    N = b.shape[1]
    bk = _tile(S, 1024)
    nk = S // bk

    def body(a_ref, b_ref, o_ref, acc):
        k = pl.program_id(2)

        @pl.when(k == 0)
        def _():
            acc[...] = jnp.zeros_like(acc)
        acc[...] += _dot_tn(a_ref[...], b_ref[...])

        @pl.when(k == nk - 1)
        def _():
            o_ref[...] = acc[...].astype(o_ref.dtype)

    (out,), extra = _pcall(
        body, name=name, grid=(M // bm, N // bn, nk),
        in_specs=[pl.BlockSpec((bk, bm), lambda m, n, k: (k, m)),
                  pl.BlockSpec((bk, bn), lambda m, n, k: (k, n))],
        out_specs=[pl.BlockSpec((bm, bn), lambda m, n, k: (m, n))],
        out_shape=[jax.ShapeDtypeStruct((M, N), out_dtype)],
        scratch_shapes=[pltpu.VMEM((bm, bn), F32)],
        sem=("arbitrary", "arbitrary", "arbitrary"), args=(a, b), jobs=jobs)
    return out, extra


def _nt_norm_bwd(dout, w, x, nrm, dxo, name, jobs=()):
    S, N = dout.shape
    Dm = w.shape[0]
    tm = _tile(S, 512)

    def body(do_ref, w_ref, x_ref, nrm_ref, dxo_ref, dx_ref, red_ref):
        @pl.when(pl.program_id(0) == 0)
        def _():
            red_ref[...] = jnp.zeros_like(red_ref)
        dh = _dot_nt(do_ref[...], w_ref[...])
        xv = x_ref[...]
        rstd = lax.rsqrt(jnp.mean(xv * xv, axis=-1, keepdims=True) + EPS)
        xn = xv * rstd
        gain = nrm_ref[0:1, :]
        hn = xn * gain
        dhn = dh * (1.0 + nrm_ref[2:3, :])
        red_ref[0:1, :] += jnp.sum(dh, axis=0, keepdims=True)
        red_ref[1:2, :] += jnp.sum(dh * hn, axis=0, keepdims=True)
        red_ref[2:3, :] += jnp.sum(dhn * xn, axis=0, keepdims=True)
        dxn = dhn * gain
        dx = rstd * (dxn - xn * jnp.mean(dxn * xn, axis=-1, keepdims=True))
        dx_ref[...] = dxo_ref[...] + dx

    return _pcall(
        body, name=name, grid=(S // tm,),
        in_specs=[pl.BlockSpec((tm, N), lambda i: (i, 0)),
                  _resident((Dm, N)),
                  pl.BlockSpec((tm, Dm), lambda i: (i, 0)),
                  pl.BlockSpec((8, Dm), lambda i: (0, 0)),
                  pl.BlockSpec((tm, Dm), lambda i: (i, 0))],
        out_specs=[pl.BlockSpec((tm, Dm), lambda i: (i, 0)),
                   pl.BlockSpec((8, Dm), lambda i: (0, 0))],
        out_shape=[jax.ShapeDtypeStruct((S, Dm), F32), jax.ShapeDtypeStruct((8, Dm), F32)],
        sem=("arbitrary",), args=(dout, w, x, nrm, dxo), jobs=jobs)


def _final_loss(x, gf, tgt, name):
    S, Dm = x.shape
    tm = _tile(S, 512)

    def body(x_ref, g_ref, t_ref, dx_ref, st_ref):
        @pl.when(pl.program_id(0) == 0)
        def _():
            st_ref[...] = jnp.zeros_like(st_ref)
        xv = x_ref[...]
        rstd = lax.rsqrt(jnp.mean(xv * xv, axis=-1, keepdims=True) + EPS)
        xn = xv * rstd
        gain = g_ref[0:1, :]
        err = xn * gain - t_ref[...]
        st_ref[1:2, :] += jnp.full((1, Dm), 0.5 / Dm, F32) * jnp.sum(err * err)
        dy = err * (1.0 / Dm)
        st_ref[0:1, :] += jnp.sum(dy * xn, axis=0, keepdims=True)
        dxn = dy * gain
        dx_ref[...] = rstd * (dxn - xn * jnp.mean(dxn * xn, axis=-1, keepdims=True))

    return pl.pallas_call(
        body, name=name, grid=(S // tm,),
        in_specs=[pl.BlockSpec((tm, Dm), lambda i: (i, 0)),
                  pl.BlockSpec((8, Dm), lambda i: (0, 0)),
                  pl.BlockSpec((tm, Dm), lambda i: (i, 0))],
        out_specs=[pl.BlockSpec((tm, Dm), lambda i: (i, 0)),
                   pl.BlockSpec((8, Dm), lambda i: (0, 0))],
        out_shape=[jax.ShapeDtypeStruct((S, Dm), F32), jax.ShapeDtypeStruct((8, Dm), F32)],
        compiler_params=_cp("arbitrary"),
    )(x, gf, tgt)


def _alibi_slope(h):
    return float(2.0 ** (-8.0 * (h + 1) / N_Q_HEADS))


def _head_planes(pair_cols):
    lane = lax.broadcasted_iota(jnp.int32, pair_cols.shape, 1)
    low = lane < HEAD_DIM
    h0_lo = jnp.where(low, pair_cols, 0.0)
    h1_hi = jnp.where(low, 0.0, pair_cols)
    h0_hi = pltpu.roll(h0_lo, HEAD_DIM, 1)
    h1_lo = pltpu.roll(h1_hi, HEAD_DIM, 1)
    return ((h0_lo.astype(BF16), h0_hi.astype(BF16)), (h1_lo.astype(BF16), h1_hi.astype(BF16)))


def _to_plane(v, e, g):
    lane = lax.broadcasted_iota(jnp.int32, v.shape, 1)
    keep = (lane < HEAD_DIM) if e == 0 else (lane >= HEAD_DIM)
    v = jnp.where(keep, v, 0.0)
    return v if e == g else pltpu.roll(v, HEAD_DIM, 1)


def _band_geometry(first_block):
    qi = lax.broadcasted_iota(jnp.int32, (BLOCK, BLOCK), 0)
    kj = lax.broadcasted_iota(jnp.int32, (BLOCK, BLOCK), 1)
    own = kj <= qi
    dist = jnp.where(own, qi - kj, qi + BLOCK - kj).astype(F32)
    valid = kj <= qi + BLOCK * (1 - first_block)
    return own, dist, valid


def _fold(band, own):
    return jnp.where(own, band[:, BLOCK:], band[:, :BLOCK])


def _unfold(v, own):
    return jnp.concatenate([jnp.where(own, 0.0, v), jnp.where(own, v, 0.0)], axis=1)


def _softmax_band(s, h, geometry, sink):
    own, dist, valid = geometry
    s = jnp.where(valid, s - _alibi_slope(h) * dist, NEG_INF)
    m = jnp.maximum(jnp.max(s, axis=-1, keepdims=True), sink)
    p = jnp.exp(s - m)
    e_sink = jnp.exp(sink - m)
    inv = 1.0 / (jnp.sum(p, axis=-1, keepdims=True) + e_sink)
    return p * inv, e_sink * inv


def _past(cur, prev, s, row):
    return jnp.where(row < s, pltpu.roll(prev, s, 0), pltpu.roll(cur, s, 0))


def _future(cur, nxt, s, row):
    T = cur.shape[0]
    return jnp.where(row >= T - s, pltpu.roll(nxt, T - s, 0), pltpu.roll(cur, T - s, 0))


def _edge_row(v, last):
    T = v.shape[0]
    r8 = lax.broadcasted_iota(jnp.int32, (SUBLANES, v.shape[1]), 0)
    blk = v[T - SUBLANES:, :] if last else v[:SUBLANES, :]
    return jnp.sum(jnp.where(r8 == (SUBLANES - 1 if last else 0), blk, 0.0), axis=0, keepdims=True)


def _lru_gates(lx, lx_prev, small_ref, wa_ref, wx_ref, row, t0):
    xc = (small_ref[4:5, :] + small_ref[3:4, :] * lx + small_ref[2:3, :] * _past(lx, lx_prev, 1, row)
          + small_ref[1:2, :] * _past(lx, lx_prev, 2, row) + small_ref[0:1, :] * _past(lx, lx_prev, 3, row))
    xcb = xc.astype(BF16)
    r = _sigmoid(_dot(xcb, wa_ref[...]) + small_ref[5:6, :])
    ig = _sigmoid(_dot(xcb, wx_ref[...]) + small_ref[6:7, :])
    sp = _softplus_neg(small_ref[7:8, :])
    la = (-LRU_C) * r * sp
    a = jnp.exp(la)
    first = (row + t0) == 0
    mult = jnp.where(first, 1.0, jnp.sqrt(-_expm1(2.0 * la)))
    return xc, xcb, r, ig, sp, a, mult, first


def _mixer_fwd(proj, sinks, small, wa, wx, name, jobs=()):
    S = proj.shape[0]
    T = MIX_TILE
    nT = S // T
    nb = T // BLOCK

    def body(proj_ref, sink_ref, small_ref, wa_ref, wx_ref, y_ref, hp_ref, kvp, lxp, zp, hcar):
        i = pl.program_id(0)

        @pl.when(i == 0)
        def _():
            kvp[...] = jnp.zeros_like(kvp)
            lxp[...] = jnp.zeros_like(lxp)
            zp[...] = jnp.zeros_like(zp)
            hcar[...] = jnp.zeros_like(hcar)

        row = lax.broadcasted_iota(jnp.int32, (T, LRU_WIDTH), 0)

        kv = proj_ref[:, C_KV:C_KV + 2 * KV_WIDTH]
        ext = jnp.concatenate([kvp[...], kv], axis=0)
        kx = _head_planes(ext[:, :KV_WIDTH])
        vx = _head_planes(ext[:, KV_WIDTH:])
        first_tile = jnp.where(i == 0, 1, 0)
        units = [(b, pair, e) for b in range(nb) for pair in range(N_Q_HEADS // 2) for e in range(2)]
        geometry = [_band_geometry(first_tile if b == 0 else 0) for b in range(nb)]
        keys = [slice(b * BLOCK, (b + 2) * BLOCK) for b in range(nb)]
        qp = {(b, pair): (proj_ref[b * BLOCK:(b + 1) * BLOCK, pair * LANES:(pair + 1) * LANES] * 0.125).astype(BF16)
              for b in range(nb) for pair in range(N_Q_HEADS // 2)}
        scores = [_fold(_dot_nt(qp[(b, pair)], kx[pair // 2][e][keys[b]]), geometry[b][0]) for b, pair, e in units]
        probs = [_unfold(_softmax_band(s, 2 * pair + e, geometry[b], sink_ref[2 * pair + e])[0],
                         geometry[b][0]).astype(BF16) for s, (b, pair, e) in zip(scores, units)]
        outs = [_dot(p, vx[pair // 2][e][keys[b]]) for p, (b, pair, e) in zip(probs, units)]
        for u in range(0, len(units), 2):
            b, pair, _ = units[u]
            y_ref[b * BLOCK:(b + 1) * BLOCK, pair * LANES:(pair + 1) * LANES] = (outs[u] + outs[u + 1]).astype(BF16)
        kvp[...] = kv[T - BLOCK:, :]

        lx = proj_ref[:, C_LX:C_LX + LRU_WIDTH]
        xc, _, _, ig, _, a, mult, _ = _lru_gates(lx, lxp[...], small_ref, wa_ref, wx_ref, row, i * T)
        lxp[...] = lx
        aa = a
        bb = mult * (ig * xc)
        s = 1
        while s < T:
            a_sh = jnp.where(row >= s, pltpu.roll(aa, s, 0), 1.0)
            b_sh = jnp.where(row >= s, pltpu.roll(bb, s, 0), 0.0)
            bb = aa * b_sh + bb
            aa = aa * a_sh
            s *= 2
        hc = hcar[0:1, :]
        hh = bb + aa * hc
        hp_ref[...] = jnp.where(row < 1, hc, pltpu.roll(hh, 1, 0))
        hcar[...] = jnp.broadcast_to(_edge_row(hh, True), hcar.shape)
        gl, _ = _gelu(proj_ref[:, C_LG:C_LG + LRU_WIDTH])
        y_ref[:, ATTN_WIDTH:ATTN_WIDTH + LRU_WIDTH] = (gl * hh).astype(BF16)

        z = proj_ref[:, C_SC:C_SC + CONV_WIDTH] * proj_ref[:, C_SX:C_SX + CONV_WIDTH]
        c3 = (small_ref[10:11, :] * z + small_ref[9:10, :] * _past(z, zp[...], 1, row)
              + small_ref[8:9, :] * _past(z, zp[...], 2, row))
        zp[...] = z
        y_ref[:, ATTN_WIDTH + LRU_WIDTH:] = (proj_ref[:, C_SB:C_SB + CONV_WIDTH] * c3).astype(BF16)

    fix = lambda i: (0, 0)
    return _pcall(
        body, name=name, grid=(nT,),
        in_specs=[pl.BlockSpec((T, IN_PROJ_WIDTH), lambda i: (i, 0)),
                  pl.BlockSpec(memory_space=pltpu.SMEM),
                  pl.BlockSpec((16, LRU_WIDTH), fix),
                  pl.BlockSpec((LRU_WIDTH, LRU_WIDTH), fix),
                  pl.BlockSpec((LRU_WIDTH, LRU_WIDTH), fix)],
        out_specs=[pl.BlockSpec((T, D_MODEL), lambda i: (i, 0)),
                   pl.BlockSpec((T, LRU_WIDTH), lambda i: (i, 0))],
        out_shape=[jax.ShapeDtypeStruct((S, D_MODEL), BF16), jax.ShapeDtypeStruct((S, LRU_WIDTH), F32)],
        scratch_shapes=[pltpu.VMEM((BLOCK, 2 * KV_WIDTH), F32), pltpu.VMEM((T, LRU_WIDTH), F32),
                        pltpu.VMEM((T, CONV_WIDTH), F32), pltpu.VMEM((SUBLANES, LRU_WIDTH), F32)],
        sem=("arbitrary",), args=(proj, sinks, small, wa, wx), jobs=jobs)


def _mixer_bwd(proj, dymix, hprev, sinks, small, wa, wx, name, jobs=()):
    S = proj.shape[0]
    T = MIX_TILE
    nT = S // T
    nb = T // BLOCK
    bpt = T // BLOCK

    def body(proj_ref, kvprev_ref, lxprev_ref, scprev_ref, sxprev_ref, dy_ref, hp_ref, sink_ref, small_ref,
             wa_ref, wx_ref, dp_ref, dsm_ref, dsink_ref, dwa_ref, dwx_ref,
             dk_s, dv_s, dkv_c, dxc_n, dc3_n, p_c):
        i = pl.program_id(0)
        ti = nT - 1 - i
        has_prev = jnp.where(ti == 0, 0.0, 1.0)

        @pl.when(i == 0)
        def _():
            for r in (dkv_c, dxc_n, dc3_n, p_c, dsm_ref, dsink_ref, dwa_ref, dwx_ref):
                r[...] = jnp.zeros_like(r)

        row = lax.broadcasted_iota(jnp.int32, (T, LRU_WIDTH), 0)

        kv = proj_ref[:, C_KV:C_KV + 2 * KV_WIDTH]
        ext = jnp.concatenate([kvprev_ref[...] * has_prev, kv], axis=0)
        kx = _head_planes(ext[:, :KV_WIDTH])
        vx = _head_planes(ext[:, KV_WIDTH:])
        dk_s[...] = jnp.zeros_like(dk_s)
        dv_s[...] = jnp.zeros_like(dv_s)
        dk_s[T:, :] = dkv_c[:, :KV_WIDTH]
        dv_s[T:, :] = dkv_c[:, KV_WIDTH:]
        first_tile = jnp.where(ti == 0, 1, 0)
        units = [(b, pair, e) for b in range(nb) for pair in range(N_Q_HEADS // 2) for e in range(2)]
        geometry = [_band_geometry(first_tile if b == 0 else 0) for b in range(nb)]
        keys = [slice(b * BLOCK, (b + 2) * BLOCK) for b in range(nb)]
        tile = {(b, pair): (slice(b * BLOCK, (b + 1) * BLOCK), slice(pair * LANES, (pair + 1) * LANES))
                for b in range(nb) for pair in range(N_Q_HEADS // 2)}
        qp = {k: (proj_ref[rc] * 0.125).astype(BF16) for k, rc in tile.items()}
        dob = {k: dy_ref[rc].astype(BF16) for k, rc in tile.items()}
        scores = [_fold(_dot_nt(qp[(b, pair)], kx[pair // 2][e][keys[b]]), geometry[b][0]) for b, pair, e in units]
        dprob = [_fold(_dot_nt(dob[(b, pair)], vx[pair // 2][e][keys[b]]), geometry[b][0]) for b, pair, e in units]
        pn_wide, ds_wide = [], []
        for s, dpm, (b, pair, e) in zip(scores, dprob, units):
            h = 2 * pair + e
            own = geometry[b][0]
            pn, psink = _softmax_band(s, h, geometry[b], sink_ref[h])
            dsum = jnp.sum(pn * dpm, axis=-1, keepdims=True)
            dsink_ref[h:h + 1, :] += jnp.full((1, LANES), -1.0, F32) * jnp.sum(psink * dsum)
            pn_wide.append(_unfold(pn, own).astype(BF16))
            ds_wide.append(_unfold(pn * (dpm - dsum), own).astype(BF16))
        dq = {}
        for pw, ds, (b, pair, e) in zip(pn_wide, ds_wide, units):
            g = pair // 2
            dv_s[keys[b], :] += _to_plane(_dot_tn(pw, dob[(b, pair)]), e, g)
            dk_s[keys[b], :] += _to_plane(_dot_tn(ds, qp[(b, pair)]), e, g)
            part = _dot(ds, kx[g][e][keys[b]])
            dq[(b, pair)] = part if e == 0 else dq[(b, pair)] + part
        for k, rc in tile.items():
            dp_ref[rc] = (0.125 * dq[k]).astype(BF16)
        dp_ref[:, C_KV:C_KV + KV_WIDTH] = dk_s[BLOCK:, :].astype(BF16)
        dp_ref[:, C_KV + KV_WIDTH:C_KV + 2 * KV_WIDTH] = dv_s[BLOCK:, :].astype(BF16)
        dkv_c[:, :KV_WIDTH] = dk_s[:BLOCK, :]
        dkv_c[:, KV_WIDTH:] = dv_s[:BLOCK, :]

        lx = proj_ref[:, C_LX:C_LX + LRU_WIDTH]
        lxprev = lxprev_ref[...] * has_prev
        xc, xcb, r, ig, sp, a, mult, first = _lru_gates(lx, lxprev, small_ref, wa_ref, wx_ref, row, ti * T)
        hp = hp_ref[...]
        hh = a * hp + mult * (ig * xc)
        lg = proj_ref[:, C_LG:C_LG + LRU_WIDTH]
        gl, th = _gelu(lg)
        dyl = dy_ref[:, ATTN_WIDTH:ATTN_WIDTH + LRU_WIDTH]
        dp_ref[:, C_LG:C_LG + LRU_WIDTH] = (dyl * hh * _gelu_grad(lg, th)).astype(BF16)
        aa = jnp.where(row < T - 1, pltpu.roll(a, T - 1, 0), 1.0)
        bb = dyl * gl
        s = 1
        while s < T:
            a_sh = jnp.where(row < T - s, pltpu.roll(aa, T - s, 0), 1.0)
            b_sh = jnp.where(row < T - s, pltpu.roll(bb, T - s, 0), 0.0)
            bb = bb + aa * b_sh
            aa = aa * a_sh
            s *= 2
        G = bb + aa * p_c[0:1, :]
        p_c[...] = jnp.broadcast_to(_edge_row(a * G, False), p_c.shape)
        da = G * hp
        dmult = G * (ig * xc)
        dig = G * mult * xc
        dxc = G * mult * ig
        dla = da * a + dmult * jnp.where(first, 0.0, -(a * a) / mult)
        dr = dla * ((-LRU_C) * sp)
        lam = small_ref[7:8, :]
        dsm_ref[7:8, :] += jnp.sum(dla * ((-LRU_C) * r), axis=0, keepdims=True) * (-_sigmoid(-lam))
        dpa = dr * r * (1.0 - r)
        dpx = dig * ig * (1.0 - ig)
        dsm_ref[5:6, :] += jnp.sum(dpa, axis=0, keepdims=True)
        dsm_ref[6:7, :] += jnp.sum(dpx, axis=0, keepdims=True)
        dpab = dpa.astype(BF16)
        dpxb = dpx.astype(BF16)
        dwa_ref[...] += _dot_tn(xcb, dpab)
        dwx_ref[...] += _dot_tn(xcb, dpxb)
        dxc = dxc + _dot_nt(dpab, wa_ref[...]) + _dot_nt(dpxb, wx_ref[...])
        dsm_ref[4:5, :] += jnp.sum(dxc, axis=0, keepdims=True)
        dsm_ref[3:4, :] += jnp.sum(dxc * lx, axis=0, keepdims=True)
        for k in range(3):
            dsm_ref[k:k + 1, :] += jnp.sum(dxc * _past(lx, lxprev, 3 - k, row), axis=0, keepdims=True)
        nxt = dxc_n[...]
        dlx = (small_ref[3:4, :] * dxc + small_ref[2:3, :] * _future(dxc, nxt, 1, row)
               + small_ref[1:2, :] * _future(dxc, nxt, 2, row) + small_ref[0:1, :] * _future(dxc, nxt, 3, row))
        dxc_n[...] = dxc
        dp_ref[:, C_LX:C_LX + LRU_WIDTH] = dlx.astype(BF16)

        sc = proj_ref[:, C_SC:C_SC + CONV_WIDTH]
        sx = proj_ref[:, C_SX:C_SX + CONV_WIDTH]
        sb = proj_ref[:, C_SB:C_SB + CONV_WIDTH]
        z = sc * sx
        zprev = (scprev_ref[...] * sxprev_ref[...]) * has_prev
        z1 = _past(z, zprev, 1, row)
        z2 = _past(z, zprev, 2, row)
        c3 = small_ref[10:11, :] * z + small_ref[9:10, :] * z1 + small_ref[8:9, :] * z2
        dys = dy_ref[:, ATTN_WIDTH + LRU_WIDTH:]
        dp_ref[:, C_SB:C_SB + CONV_WIDTH] = (dys * c3).astype(BF16)
        dc3 = dys * sb
        dsm_ref[10:11, :] += jnp.sum(dc3 * z, axis=0, keepdims=True)
        dsm_ref[9:10, :] += jnp.sum(dc3 * z1, axis=0, keepdims=True)
        dsm_ref[8:9, :] += jnp.sum(dc3 * z2, axis=0, keepdims=True)
        nxt3 = dc3_n[...]
        dz = (small_ref[10:11, :] * dc3 + small_ref[9:10, :] * _future(dc3, nxt3, 1, row)
              + small_ref[8:9, :] * _future(dc3, nxt3, 2, row))
        dc3_n[...] = dc3
        dp_ref[:, C_SC:C_SC + CONV_WIDTH] = (dz * sx).astype(BF16)
        dp_ref[:, C_SX:C_SX + CONV_WIDTH] = (dz * sc).astype(BF16)

    fix = lambda i: (0, 0)
    cur = lambda i: (nT - 1 - i, 0)
    prev_cols = lambda cb: (lambda i: (jnp.maximum(nT - 2 - i, 0), cb))
    return _pcall(
        body, name=name, grid=(nT,),
        in_specs=[pl.BlockSpec((T, IN_PROJ_WIDTH), cur),
                  pl.BlockSpec((BLOCK, 2 * KV_WIDTH),
                               lambda i: (jnp.maximum((nT - 1 - i) * bpt - 1, 0), C_KV // (2 * KV_WIDTH))),
                  pl.BlockSpec((T, LRU_WIDTH), prev_cols(C_LX // LRU_WIDTH)),
                  pl.BlockSpec((T, CONV_WIDTH), prev_cols(C_SC // CONV_WIDTH)),
                  pl.BlockSpec((T, CONV_WIDTH), prev_cols(C_SX // CONV_WIDTH)),
                  pl.BlockSpec((T, D_MODEL), cur),
                  pl.BlockSpec((T, LRU_WIDTH), cur),
                  pl.BlockSpec(memory_space=pltpu.SMEM),
                  pl.BlockSpec((16, LRU_WIDTH), fix),
                  pl.BlockSpec((LRU_WIDTH, LRU_WIDTH), fix),
                  pl.BlockSpec((LRU_WIDTH, LRU_WIDTH), fix)],
        out_specs=[pl.BlockSpec((T, IN_PROJ_WIDTH), cur),
                   pl.BlockSpec((16, LRU_WIDTH), fix),
                   pl.BlockSpec((SUBLANES, LANES), fix),
                   pl.BlockSpec((LRU_WIDTH, LRU_WIDTH), fix),
                   pl.BlockSpec((LRU_WIDTH, LRU_WIDTH), fix)],
        out_shape=[jax.ShapeDtypeStruct((S, IN_PROJ_WIDTH), BF16),
                   jax.ShapeDtypeStruct((16, LRU_WIDTH), F32),
                   jax.ShapeDtypeStruct((SUBLANES, LANES), F32),
                   jax.ShapeDtypeStruct((LRU_WIDTH, LRU_WIDTH), F32),
                   jax.ShapeDtypeStruct((LRU_WIDTH, LRU_WIDTH), F32)],
        scratch_shapes=[pltpu.VMEM((T + BLOCK, KV_WIDTH), F32), pltpu.VMEM((T + BLOCK, KV_WIDTH), F32),
                        pltpu.VMEM((BLOCK, 2 * KV_WIDTH), F32), pltpu.VMEM((T, LRU_WIDTH), F32),
                        pltpu.VMEM((T, CONV_WIDTH), F32), pltpu.VMEM((SUBLANES, LRU_WIDTH), F32)],
        sem=("arbitrary",), args=(proj, proj, proj, proj, proj, dymix, hprev, sinks, small, wa, wx), jobs=jobs)


def _mod_matmul(c_all, w_mod, name):
    L, Dm, N = w_mod.shape
    R = c_all.shape[0]
    tn = 768

    def body(c_ref, w_ref, o_ref, ca_ref):
        cv = c_ref[...]
        ca = (cv * _sigmoid(cv)).astype(BF16)
        ca_ref[...] = ca
        o_ref[0] = _dot(ca, w_ref[0].astype(BF16))

    return pl.pallas_call(
        body, name=name, grid=(L, N // tn),
        in_specs=[pl.BlockSpec((R, Dm), lambda l, n: (0, 0)),
                  pl.BlockSpec((1, Dm, tn), lambda l, n: (l, 0, n))],
        out_specs=[pl.BlockSpec((1, R, tn), lambda l, n: (l, 0, n)), pl.BlockSpec((R, Dm), lambda l, n: (0, 0))],
        out_shape=[jax.ShapeDtypeStruct((L, R, N), F32), jax.ShapeDtypeStruct((R, Dm), BF16)],
        compiler_params=_cp("arbitrary", "arbitrary"),
    )(c_all, w_mod)


def _adamw_update(g, w_ref, m_ref, v_ref, go_ref, d_ref, mo_ref, vo_ref):
    mn = ADAM_B1 * m_ref[...] + (1.0 - ADAM_B1) * g
    vn = ADAM_B2 * v_ref[...] + (1.0 - ADAM_B2) * (g * g)
    go_ref[...] = g
    mo_ref[...] = mn
    vo_ref[...] = vn
    m_hat = mn / (1.0 - ADAM_B1 ** ADAM_STEP)
    v_hat = vn / (1.0 - ADAM_B2 ** ADAM_STEP)
    d_ref[...] = (-ADAM_LR) * (m_hat / (jnp.sqrt(v_hat) + ADAM_EPS) + ADAM_WD * w_ref[...])


def _adamw(w, g, m, v, name):
    R, C = w.shape
    tr = 8
    for cand in (512, 256, 128, 64, 32, 16, 8):
        if R % cand == 0 and cand * C * 4 <= (1 << 20):
            tr = cand
            break

    def body(w_ref, g_ref, *rest):
        _adamw_update(g_ref[...], w_ref, *rest)

    spec = pl.BlockSpec((tr, C), lambda i: (i, 0))
    return _pcall(body, name=name, grid=(R // tr,), in_specs=[spec] * 4, out_specs=[spec] * 4,
                  out_shape=[jax.ShapeDtypeStruct((R, C), F32)] * 4, sem=("arbitrary",), args=(w, g, m, v))


def _adamw_partials(w, partials, m, v, name):
    nl = len(partials)
    _, R, C = partials[0][0].shape
    tr = 8
    for cand in (256, 128, 64, 32, 16):
        if R % cand == 0 and cand * C * 4 <= (1 << 19):
            tr = cand
            break
    ni = R // tr

    def body(*refs):
        w_ref, p_refs = refs[0], refs[1:1 + 2 * nl]
        m_ref, v_ref, go_ref, d_ref, mo_ref, vo_ref = refs[1 + 2 * nl:]
        for l in range(nl):
            @pl.when(pl.program_id(0) == l)
            def _(pair=p_refs[2 * l:2 * l + 2]):
                own, sib = [((p[0].astype(F32) + p[1].astype(F32)) + p[2].astype(F32)) + p[3].astype(F32)
                            for p in pair]
                _adamw_update(own + sib, w_ref, m_ref, v_ref, go_ref, d_ref, mo_ref, vo_ref)

    def slots(l):
        return pl.BlockSpec((N_CHIPS, tr, C),
                            lambda ll, i: (0, jnp.where(ll == l, i, jnp.where(ll < l, 0, ni - 1)), 0))

    spec = pl.BlockSpec((tr, C), lambda ll, i: (ll * ni + i, 0))
    return pl.pallas_call(
        body, name=name, grid=(nl, ni),
        in_specs=[spec] + [slots(l) for l in range(nl) for _ in range(2)] + [spec, spec], out_specs=[spec] * 4,
        out_shape=[jax.ShapeDtypeStruct((nl * R, C), F32)] * 4,
        compiler_params=_cp("arbitrary", "arbitrary"),
    )(w, *[p for pair in partials for p in pair], m, v)


def _all_gather_small(v, name):
    M, N = v.shape

    def body(x_ref, out_ref, sum_ref, send_sems, recv_sems, local_sem):
        x, y, c = lax.axis_index("x"), lax.axis_index("y"), lax.axis_index("c")
        me, sibling = (x, y, c), (x, y, 1 - c)
        chips = [(1 - x, y), (x, 1 - y), (1 - x, 1 - y)]

        def rows(px, py, pc):
            return out_ref.at[pl.ds(pl.multiple_of((4 * px + 2 * py + pc) * M, SUBLANES), M), :]

        def copy(k, block, to, src=None):
            return pltpu.make_async_remote_copy(
                src_ref=rows(*block) if src is None else src, dst_ref=rows(*block),
                send_sem=send_sems.at[k], recv_sem=recv_sems.at[k], device_id=to, device_id_type=MESH)

        mine = pltpu.make_async_copy(x_ref, rows(*me), local_sem)
        mine.start()
        first = [copy(0, me, sibling, src=x_ref)]
        first += [copy(1 + j, me, (*chip, c), src=x_ref) for j, chip in enumerate(chips)]
        for cp in first:
            cp.start()
        passed = [copy(4 + j, (*chip, c), sibling) for j, chip in enumerate(chips)]
        for j, chip in enumerate(chips):
            copy(1 + j, (*chip, c), me).wait_recv()
            passed[j].start()
        copy(0, sibling, me).wait_recv()
        for j, chip in enumerate(chips):
            copy(4 + j, (*chip, 1 - c), me).wait_recv()
        for cp in first + passed:
            cp.wait_send()
        mine.wait()
        acc = out_ref[0:M, :]
        for d in range(1, N_DEV):
            acc = acc + out_ref[d * M:(d + 1) * M, :]
        sum_ref[...] = acc

    return pl.pallas_call(
        body, name=name,
        out_shape=[jax.ShapeDtypeStruct((N_DEV * M, N), F32), jax.ShapeDtypeStruct((M, N), F32)],
        in_specs=[pl.BlockSpec(memory_space=pltpu.VMEM)],
        out_specs=[pl.BlockSpec(memory_space=pltpu.VMEM), pl.BlockSpec(memory_space=pltpu.VMEM)],
        scratch_shapes=[pltpu.SemaphoreType.DMA((7,)), pltpu.SemaphoreType.DMA((7,)), pltpu.SemaphoreType.DMA],
        compiler_params=pltpu.CompilerParams(vmem_limit_bytes=VMEM_LIMIT),
    )(v)


_BIG = (("w_ffn1_gu", 1), ("w_ffn1_down", 0), ("w_ffn2_gu", 1), ("w_ffn2_down", 0), ("w_in", 1), ("w_out", 0))
_AXIS = dict(_BIG)

_GATHER_PLAN = {
    "first": [(0, "w_ffn1_gu"), (0, "w_ffn1_down")],
    (0, "ffn1"): [(0, "w_in"), (0, "w_out"), (0, "w_ffn2_gu"), (0, "w_ffn2_down")],
    (0, "ffn2"): [(1, "w_ffn1_gu"), (1, "w_ffn1_down"), (1, "w_in"), (1, "w_out")],
    (1, "ffn1"): [(1, "w_ffn2_gu"), (1, "w_ffn2_down")],
}


def _pack(arrs, rows_multiple=SUBLANES):
    flat = jnp.concatenate([a.astype(F32).reshape(-1) for a in arrs])
    unit = rows_multiple * LANES
    total = -(-flat.shape[0] // unit) * unit
    return jnp.pad(flat, (0, total - flat.shape[0])).reshape(total // LANES, LANES)


def _unpack(flat, shapes):
    out, off = [], 0
    for shp in shapes:
        n = int(math.prod(shp))
        out.append(flat[off:off + n].reshape(shp))
        off += n
    return out


def _block_diag(w):
    out = jnp.zeros((LRU_WIDTH, LRU_WIDTH), F32)
    for h in range(4):
        out = lax.dynamic_update_slice(out, w[h], (h * HEAD_DIM, h * HEAD_DIM))
    return out


def _diag_blocks(w):
    return jnp.stack([w[h * HEAD_DIM:(h + 1) * HEAD_DIM, h * HEAD_DIM:(h + 1) * HEAD_DIM] for h in range(4)])


def _rows8(*rows):
    z = jnp.zeros((8 - len(rows), rows[0].shape[-1]), F32)
    return jnp.concatenate([jnp.stack(rows), z], axis=0)


def kernel(x, c, w_mod, b_mod, g_norm, w_ffn1_gu, w_ffn1_down, w_ffn2_gu, w_ffn2_down, w_in, w_out, attn_sinks, lru_conv_w, lru_conv_b, lru_gate_a_w, lru_gate_a_b, lru_gate_x_w, lru_gate_x_b, lru_lambda, sc_conv_w, g_final, loss_target, m_w_mod, m_b_mod, m_g_norm, m_w_ffn1_gu, m_w_ffn1_down, m_w_ffn2_gu, m_w_ffn2_down, m_w_in, m_w_out, m_attn_sinks, m_lru_conv_w, m_lru_conv_b, m_lru_gate_a_w, m_lru_gate_a_b, m_lru_gate_x_w, m_lru_gate_x_b, m_lru_lambda, m_sc_conv_w, m_g_final, v_w_mod, v_b_mod, v_g_norm, v_w_ffn1_gu, v_w_ffn1_down, v_w_ffn2_gu, v_w_ffn2_down, v_w_in, v_w_out, v_attn_sinks, v_lru_conv_w, v_lru_conv_b, v_lru_gate_a_w, v_lru_gate_a_b, v_lru_gate_x_w, v_lru_gate_x_b, v_lru_lambda, v_sc_conv_w, v_g_final):
    W = dict(w_mod=w_mod, b_mod=b_mod, g_norm=g_norm, w_ffn1_gu=w_ffn1_gu, w_ffn1_down=w_ffn1_down,
             w_ffn2_gu=w_ffn2_gu, w_ffn2_down=w_ffn2_down, w_in=w_in, w_out=w_out, attn_sinks=attn_sinks,
             lru_conv_w=lru_conv_w, lru_conv_b=lru_conv_b, lru_gate_a_w=lru_gate_a_w, lru_gate_a_b=lru_gate_a_b,
             lru_gate_x_w=lru_gate_x_w, lru_gate_x_b=lru_gate_x_b, lru_lambda=lru_lambda, sc_conv_w=sc_conv_w,
             g_final=g_final)
    M1 = dict(w_mod=m_w_mod, b_mod=m_b_mod, g_norm=m_g_norm, w_ffn1_gu=m_w_ffn1_gu, w_ffn1_down=m_w_ffn1_down,
              w_ffn2_gu=m_w_ffn2_gu, w_ffn2_down=m_w_ffn2_down, w_in=m_w_in, w_out=m_w_out,
              attn_sinks=m_attn_sinks, lru_conv_w=m_lru_conv_w, lru_conv_b=m_lru_conv_b,
              lru_gate_a_w=m_lru_gate_a_w, lru_gate_a_b=m_lru_gate_a_b, lru_gate_x_w=m_lru_gate_x_w,
              lru_gate_x_b=m_lru_gate_x_b, lru_lambda=m_lru_lambda, sc_conv_w=m_sc_conv_w, g_final=m_g_final)
    V1 = dict(w_mod=v_w_mod, b_mod=v_b_mod, g_norm=v_g_norm, w_ffn1_gu=v_w_ffn1_gu, w_ffn1_down=v_w_ffn1_down,
              w_ffn2_gu=v_w_ffn2_gu, w_ffn2_down=v_w_ffn2_down, w_in=v_w_in, w_out=v_w_out,
              attn_sinks=v_attn_sinks, lru_conv_w=v_lru_conv_w, lru_conv_b=v_lru_conv_b,
              lru_gate_a_w=v_lru_gate_a_w, lru_gate_a_b=v_lru_gate_a_b, lru_gate_x_w=v_lru_gate_x_w,
              lru_gate_x_b=v_lru_gate_x_b, lru_lambda=v_lru_lambda, sc_conv_w=v_sc_conv_w, g_final=v_g_final)
    names = ["w_mod", "b_mod", "g_norm", "w_ffn1_gu", "w_ffn1_down", "w_ffn2_gu", "w_ffn2_down", "w_in", "w_out",
             "attn_sinks", "lru_conv_w", "lru_conv_b", "lru_gate_a_w", "lru_gate_a_b", "lru_gate_x_w",
             "lru_gate_x_b", "lru_lambda", "sc_conv_w", "g_final"]

    xs = x[0]
    tgt = loss_target[0]
    S = xs.shape[0]
    chip = 2 * lax.axis_index("x") + lax.axis_index("y")
    batch = 2 * chip + lax.axis_index("c")
    L = DEPTH

    fwd_shapes = [(D_MODEL,), g_norm.shape, lru_conv_w.shape, sc_conv_w.shape]
    gathered, _ = _all_gather_small(_pack([c[0], g_norm, lru_conv_w, sc_conv_w]), "gather_small_fwd")
    gathered = gathered.reshape(N_DEV, -1)
    c_all = gathered[:, :D_MODEL]
    per_chip = [_unpack(gathered[2 * jj], fwd_shapes) for jj in range(N_CHIPS)]
    g_norm_full = jnp.concatenate([p[1] for p in per_chip], axis=-1)
    lru_conv_w_full = jnp.concatenate([p[2] for p in per_chip], axis=-1)
    sc_conv_w_full = jnp.concatenate([p[3] for p in per_chip], axis=-1)

    c_pad = jnp.concatenate([c_all, jnp.zeros_like(c_all)], axis=0)
    mod_part, c_act = _mod_matmul(c_pad, w_mod, "mod_matmul")
    mod_all, _ = _all_gather_small(mod_part.reshape(-1, LANES), "gather_mod")
    mod_all = mod_all.reshape(N_DEV, L, 16, -1)
    mod_rows = [lax.dynamic_index_in_dim(mod_all[2 * jj], batch, axis=1, keepdims=False) for jj in range(N_CHIPS)]
    mod = (jnp.concatenate(mod_rows, axis=-1) + b_mod).reshape(L, 9, D_MODEL)

    def nrm_rows(l, s):
        return _rows8(g_norm_full[l, s], mod[l, 3 * s], mod[l, 3 * s + 1], mod[l, 3 * s + 2])

    full = {}

    def gather_jobs(key):
        return [_GatherJob(W[n][l].astype(BF16), _AXIS[n]) for l, n in _GATHER_PLAN.get(key, ())]

    def landed(key, outs):
        full.update(zip(_GATHER_PLAN.get(key, ()), outs))

    landed("first", _comm_only(gather_jobs("first"), "gather_first"))

    def mixer_params(l):
        small = jnp.concatenate([lru_conv_w_full[l], lru_conv_b[l][None], lru_gate_a_b[l][None],
                                 lru_gate_x_b[l][None], lru_lambda[l][None], sc_conv_w_full[l],
                                 jnp.zeros((5, LRU_WIDTH), F32)], axis=0)
        return (attn_sinks[l], small, _block_diag(lru_gate_a_w[l]).astype(BF16),
                _block_diag(lru_gate_x_w[l]).astype(BF16))

    saved = []
    xcur = xs
    for l in range(L):
        n1, n2, n3 = nrm_rows(l, 0), nrm_rows(l, 1), nrm_rows(l, 2)

        def ffn(which, xin, nrm):
            key = (l, which)
            (xo, h, gu, y), ex = _ffn_fwd(xin, nrm, full[(l, f"w_{which}_gu")], full[(l, f"w_{which}_down")],
                                          f"l{l}_{which}", gather_jobs(key))
            landed(key, ex)
            return xo, (xin, h, gu, y)

        x1, s1 = ffn("ffn1", xcur, n1)
        (proj, h2), _ = _norm_matmul(x1, n2, full[(l, "w_in")], F32, f"l{l}_mix_in")
        mp = mixer_params(l)
        (ymix, hprev), _ = _mixer_fwd(proj, *mp, f"l{l}_mix_core")
        (x2, ymo), _ = _proj_residual(ymix, full[(l, "w_out")], x1, n2, f"l{l}_mix_out")
        s2 = (x1, h2, proj, ymix, ymo, hprev, mp)
        xcur, s3 = ffn("ffn2", x2, n3)
        saved.append((n1, n2, n3, s1, s2, s3))

    dx, stats = _final_loss(xcur, _rows8(g_final), tgt, "final_loss")
    loss = lax.psum(stats[1, 0], ("x", "y", "c"))
    d_g_final = stats[0]

    recv, theirs = {}, {}
    waiting = []

    def carried(fn, *a, extra=()):
        items = waiting + list(extra)
        waiting.clear()
        outs, landed_now = fn(*a, jobs=[_SiblingJob(recv[(ll, n)]) if g is None else _ScatterJob(g, _AXIS[n])
                                        for ll, n, g in items])
        for (ll, n, g), arr in zip(items, landed_now):
            if g is None:
                theirs[(ll, n)] = arr
            else:
                recv[(ll, n)] = arr
                waiting.append((ll, n, None))
        return outs

    dmod, d_gnorm, d_small = [None] * L, [None] * L, [None] * L
    for l in reversed(range(L)):
        n1, n2, n3, s1, s2, s3 = saved[l]

        def plain(fn, *a):
            return fn(*a)[0]

        def ffn_bwd(which, dxo, sv, nrm, last):
            xin, h, gu, y = sv
            tag = f"l{l}_{which}"
            dgu, dgate, dw_down = (carried if which == "ffn2" else plain)(
                _proj_residual_bwd, dxo, gu, y, full[(l, f"w_{which}_down")], nrm, 0.5, True, tag + "_down_bwd")
            dw_gu = carried(_atb, h, dgu, BF16, 1024, 2816, tag + "_dw_gu", extra=[(l, f"w_{which}_down", dw_down)])
            mine = [(l, f"w_{which}_gu", dw_gu)]
            dxi, red = (carried if last else plain)(
                _nt_norm_bwd, dgu, full[(l, f"w_{which}_gu")], xin, nrm, dxo, tag + "_gu_bwd",
                **(dict(extra=mine) if last else {}))
            if not last:
                waiting.extend(mine)
            return dxi, (red[0], red[1], dgate[0]), red[2]

        dx, dm3, dg3 = ffn_bwd("ffn2", dx, s3, n3, False)
        x_in, h2, proj, ymix, ymo, hprev, mp = s2
        dymix, dgate, dw_out = plain(_proj_residual_bwd, dx, ymix, ymo, full[(l, "w_out")], n2, 1.0, False,
                                     f"l{l}_mix_out_bwd")
        dproj, dsm, dsink, dwa, dwx = carried(_mixer_bwd, proj, dymix, hprev, *mp, f"l{l}_mix_core_bwd",
                                              extra=[(l, "w_out", dw_out)])
        dw_in = plain(_atb, h2, dproj, BF16, 1024, 2048, f"l{l}_dw_in")
        dx, red = carried(_nt_norm_bwd, dproj, full[(l, "w_in")], x_in, n2, dx, f"l{l}_mix_in_bwd",
                          extra=[(l, "w_in", dw_in)])
        dm2, dg2 = (red[0], red[1], dgate[0]), red[2]
        dx, dm1, dg1 = ffn_bwd("ffn1", dx, s1, n1, l == 0)
        dmod[l] = jnp.stack(list(dm1) + list(dm2) + list(dm3))
        d_gnorm[l] = jnp.stack([dg1, dg2, dg3])
        d_small[l] = (dsink[:, 0], dsm[0:4], dsm[4], _diag_blocks(dwa), dsm[5], _diag_blocks(dwx), dsm[6],
                      dsm[7], dsm[8:11])
    grad_x = dx[None]

    def both(k):
        return jnp.stack([d_small[0][k], d_small[1][k]])
    small_names = ["g_norm", "attn_sinks", "lru_conv_w", "lru_conv_b", "lru_gate_a_w", "lru_gate_a_b",
                   "lru_gate_x_w", "lru_gate_x_b", "lru_lambda", "sc_conv_w", "g_final"]
    small_parts = [jnp.stack(d_gnorm)] + [both(k) for k in range(9)] + [d_g_final]
    dmod_flat = jnp.stack(dmod).reshape(-1)
    bwd_gathered, bwd_sum = _all_gather_small(_pack([dmod_flat] + small_parts), "gather_small_bwd")
    n_mod = dmod_flat.shape[0]
    dmod_all = bwd_gathered.reshape(N_DEV, -1)[:, :n_mod].reshape(N_DEV, L, 9 * D_MODEL)
    bwd_sum = bwd_sum.reshape(-1)
    G = {"b_mod": bwd_sum[:n_mod].reshape(L, 9 * D_MODEL)}
    G.update(zip(small_names, _unpack(bwd_sum[n_mod:], [p.shape for p in small_parts])))
    for n in ("g_norm", "lru_conv_w", "sc_conv_w"):
        wdt = W[n].shape[-1]
        G[n] = lax.dynamic_slice_in_dim(G[n], chip * wdt, wdt, axis=G[n].ndim - 1)

    ncol = w_mod.shape[-1]
    dmod_cols = lax.dynamic_slice_in_dim(dmod_all, chip * ncol, ncol, axis=2)
    zeros8 = jnp.zeros((N_DEV, ncol), F32)
    g_w_mod = jnp.stack([carried(_atb, c_act, jnp.concatenate([dmod_cols[:, l], zeros8], axis=0).astype(BF16), F32,
                                 D_MODEL, 768, f"l{l}_dw_mod") for l in range(L)])

    out_g, out_d, out_m, out_v = {}, {}, {}, {}
    res, _ = _adamw(w_mod.reshape(-1, ncol), g_w_mod.reshape(-1, ncol), m_w_mod.reshape(-1, ncol),
                    v_w_mod.reshape(-1, ncol), "adamw_w_mod")
    out_g["w_mod"], out_d["w_mod"], out_m["w_mod"], out_v["w_mod"] = [r.reshape(w_mod.shape) for r in res]
    for n, _ in _BIG:
        shp = W[n].shape
        flat = (shp[0] * shp[1], shp[2])
        res = _adamw_partials(W[n].reshape(flat), [(recv[(l, n)], theirs[(l, n)]) for l in range(L)],
                              M1[n].reshape(flat), V1[n].reshape(flat), f"adamw_{n}")
        out_g[n], out_d[n], out_m[n], out_v[n] = [r.reshape(shp) for r in res]
    rest = ["b_mod"] + small_names
    shapes = [W[n].shape for n in rest]
    res, _ = _adamw(_pack([W[n] for n in rest]), _pack([G[n] for n in rest]), _pack([M1[n] for n in rest]),
                    _pack([V1[n] for n in rest]), "adamw_small")
    for dst, r in zip((out_g, out_d, out_m, out_v), res):
        dst.update(zip(rest, _unpack(r.reshape(-1), shapes)))

    return (loss, grad_x, *[out_g[n] for n in names], *[out_d[n] for n in names],
            *[out_m[n] for n in names], *[out_v[n] for n in names])
```

```python
import math

import jax
import jax.numpy as jnp
from jax import lax
from jax.experimental import pallas as pl
from jax.experimental.pallas import tpu as pltpu

F32 = jnp.float32
BF16 = jnp.bfloat16

D_MODEL = 1024
DEPTH = 2
HEAD_DIM = 64
N_Q_HEADS = 8
ATTN_WIDTH = 512
KV_WIDTH = 128
LRU_WIDTH = 256
CONV_WIDTH = 256
IN_PROJ_WIDTH = 2048
BLOCK = 128
D_FF = 2816
EPS = 1e-6
NEG_INF = -1e30
LRU_C = 8.0
N_CHIPS = 4
N_DEV = 8

C_Q, C_KV, C_LX, C_LG, C_SB, C_SC, C_SX = 0, 512, 768, 1024, 1280, 1536, 1792

ADAM_LR = 0.001
ADAM_B1 = 0.9
ADAM_B2 = 0.999
ADAM_EPS = 1e-08
ADAM_WD = 0.01
ADAM_STEP = 10

LANES = 128
SUBLANES = 8
VMEM_LIMIT = 56 * 1024 * 1024
MIX_TILE = 256

MESH = pl.DeviceIdType.MESH


def _cp(*sem):
    return pltpu.CompilerParams(dimension_semantics=sem, vmem_limit_bytes=VMEM_LIMIT)


def _tile(n, pref):
    t = min(n, pref)
    while n % t:
        t //= 2
    return t


MXU_DIM = 256


def _col_chunk(n):
    return max(c for c in range(MXU_DIM, 2816 + 1, MXU_DIM) if n % c == 0)


def _resident(shape):
    return pl.BlockSpec(shape, lambda *_: (0, 0), pipeline_mode=pl.Buffered(1))


def _sigmoid(v):
    return 1.0 / (1.0 + jnp.exp(-v))


def _expm1(v):
    series = v * (1.0 + v * (0.5 + v * (1.0 / 6.0)))
    return jnp.where(v > -0.01, series, jnp.exp(v) - 1.0)


def _softplus_neg(lam):
    e = jnp.exp(-jnp.abs(lam))
    log1p = jnp.where(e < 1e-2, e * (1.0 - e * (0.5 - e * (1.0 / 3.0))), jnp.log(1.0 + e))
    return jnp.maximum(-lam, 0.0) + log1p


_GELU_K = math.sqrt(2.0 / math.pi)
_GELU_C = 0.044715


def _gelu(v):
    t = jnp.tanh(_GELU_K * (v + _GELU_C * v * v * v))
    return 0.5 * v * (1.0 + t), t


def _gelu_grad(v, t):
    return 0.5 * (1.0 + t) + 0.5 * v * (1.0 - t * t) * _GELU_K * (1.0 + 3.0 * _GELU_C * v * v)


def _dot(a, b):
    return jnp.dot(a, b, preferred_element_type=F32)


def _dot_nt(a, b):
    return lax.dot_general(a, b, (((1,), (1,)), ((), ())), preferred_element_type=F32)


def _dot_tn(a, b):
    return lax.dot_general(a, b, (((0,), (0,)), ((), ())), preferred_element_type=F32)


def _window(ref, axis, j, width):
    start = pl.multiple_of(j * width, LANES if axis == 1 else 16)
    if axis == 1:
        return ref.at[:, pl.ds(start, width)]
    return ref.at[pl.ds(start, width), :]


def _chip_peers():
    x, y, c = lax.axis_index("x"), lax.axis_index("y"), lax.axis_index("c")
    return x, y, c, [(1 - x, y), (x, 1 - y), (1 - x, 1 - y)]


class _GatherJob:
    def __init__(self, shard, axis):
        self.src, self.axis, self.width, self.half = shard, axis, shard.shape[axis], shard.shape[0] // 2
        full = tuple(d * N_CHIPS if k == axis else d for k, d in enumerate(shard.shape))
        self.out_shape = jax.ShapeDtypeStruct(full, shard.dtype)

    def _piece(self, ref, j, hf):
        if self.axis == 1:
            return ref.at[pl.ds(pl.multiple_of(hf * self.half, 16), self.half),
                          pl.ds(pl.multiple_of(j * self.width, LANES), self.width)]
        return ref.at[pl.ds(pl.multiple_of(j * self.width + hf * self.half, 16), self.half), :]

    def _copies(self, src, dst, send, recv, loc, t):
        x, y, c, chips = _chip_peers()
        j = 2 * x + y
        owners = [2 * px + py for px, py in chips]
        local = pltpu.make_async_copy(src, _window(dst, self.axis, j, self.width), loc.at[t])
        mine = src.at[pl.ds(pl.multiple_of(c * self.half, 16), self.half), :]

        def ici(k, owner):
            return pltpu.make_async_remote_copy(
                src_ref=mine, dst_ref=self._piece(dst, owner, c), send_sem=send.at[JOB_SEMS * t + k],
                recv_sem=recv.at[JOB_SEMS * t + k], device_id=(*chips[k], c), device_id_type=MESH)

        def relay(k, hf):
            piece = self._piece(dst, owners[k], hf)
            return pltpu.make_async_remote_copy(
                src_ref=piece, dst_ref=piece, send_sem=send.at[JOB_SEMS * t + 4 + k],
                recv_sem=recv.at[JOB_SEMS * t + 4 + k],
                device_id=(x, y, 1 - c), device_id_type=MESH)

        return (local, [ici(k, j) for k in range(3)], [ici(k, owners[k]) for k in range(3)],
                [relay(k, c) for k in range(3)], [relay(k, 1 - c) for k in range(3)])

    def start(self, *a):
        local, ici_out, _, _, _ = self._copies(*a)
        local.start()
        for cp in ici_out:
            cp.start()

    def relay(self, *a):
        _, _, ici_in, relay_out, _ = self._copies(*a)
        for arrived, onward in zip(ici_in, relay_out):
            arrived.wait_recv()
            onward.start()

    def finish(self, *a):
        local, ici_out, _, relay_out, relay_in = self._copies(*a)
        for cp in relay_in:
            cp.wait_recv()
        for cp in ici_out + relay_out:
            cp.wait_send()
        local.wait()


class _ScatterJob:
    def __init__(self, full, axis):
        self.src, self.axis, self.width = full, axis, full.shape[axis] // N_CHIPS
        shard = tuple(self.width if k == axis else d for k, d in enumerate(full.shape))
        self.out_shape = jax.ShapeDtypeStruct((N_CHIPS,) + shard, full.dtype)

    def _copies(self, src, dst, send, recv, loc, t):
        x, y, c, chips = _chip_peers()
        local = pltpu.make_async_copy(_window(src, self.axis, 2 * x + y, self.width), dst.at[3], loc.at[t])
        sends = [pltpu.make_async_remote_copy(
            src_ref=_window(src, self.axis, 2 * px + py, self.width), dst_ref=dst.at[k],
            send_sem=send.at[JOB_SEMS * t + k], recv_sem=recv.at[JOB_SEMS * t + k], device_id=(px, py, c),
            device_id_type=MESH) for k, (px, py) in enumerate(chips)]
        return local, sends

    def start(self, *a):
        local, sends = self._copies(*a)
        local.start()
        for cp in sends:
            cp.start()

    def relay(self, *a):
        pass

    def finish(self, *a):
        local, sends = self._copies(*a)
        for cp in sends:
            cp.wait_recv()
        for cp in sends:
            cp.wait_send()
        local.wait()


class _SiblingJob:
    def __init__(self, arr):
        self.src, self.out_shape = arr, jax.ShapeDtypeStruct(arr.shape, arr.dtype)

    def _copy(self, src, dst, send, recv, loc, t):
        x, y, c = lax.axis_index("x"), lax.axis_index("y"), lax.axis_index("c")
        return pltpu.make_async_remote_copy(
            src_ref=src, dst_ref=dst, send_sem=send.at[JOB_SEMS * t], recv_sem=recv.at[JOB_SEMS * t],
            device_id=(x, y, 1 - c), device_id_type=MESH)

    def start(self, *a):
        self._copy(*a).start()

    def relay(self, *a):
        pass

    def finish(self, *a):
        self._copy(*a).wait()


JOB_SEMS = 8


def _run_jobs(phase, jobs, srcs, dsts, sems):
    for t, job in enumerate(jobs):
        getattr(job, phase)(srcs[t], dsts[t], *sems, t)


def _job_scratch(n):
    return [pltpu.SemaphoreType.DMA((JOB_SEMS * n,)), pltpu.SemaphoreType.DMA((JOB_SEMS * n,)),
            pltpu.SemaphoreType.DMA((n,))]


def _pcall(body, *, name, grid, in_specs, out_specs, out_shape, sem, args, scratch_shapes=(), jobs=()):
    in_specs, out_specs, out_shape = list(in_specs), list(out_specs), list(out_shape)
    scratch_shapes = list(scratch_shapes)
    if not jobs:
        res = pl.pallas_call(body, name=name, grid=grid, in_specs=in_specs, out_specs=out_specs, out_shape=out_shape,
                             scratch_shapes=scratch_shapes, compiler_params=_cp(*sem))(*args)
        return list(res), []
    n_in, n_out, n_scr, nj = len(args), len(out_shape), len(scratch_shapes), len(jobs)
    n_steps = math.prod(grid)
    relay_step = (3 * n_steps) // 4
    relay_early = 0 < relay_step < n_steps - 1

    def wrapped(*refs):
        ins, refs = refs[:n_in], refs[n_in:]
        jin, refs = refs[:nj], refs[nj:]
        outs, refs = refs[:n_out], refs[n_out:]
        jout, refs = refs[:nj], refs[nj:]
        scr, sems = refs[:n_scr], refs[n_scr:]
        step = pl.program_id(0)
        for d in range(1, len(grid)):
            step = step * grid[d] + pl.program_id(d)

        @pl.when(step == 0)
        def _():
            _run_jobs("start", jobs, jin, jout, sems)

        if relay_early:
            @pl.when(step == relay_step)
            def _():
                _run_jobs("relay", jobs, jin, jout, sems)
        body(*ins, *outs, *scr)

        @pl.when(step == n_steps - 1)
        def _():
            if not relay_early:
                _run_jobs("relay", jobs, jin, jout, sems)
            _run_jobs("finish", jobs, jin, jout, sems)

    hbm = pl.BlockSpec(memory_space=pltpu.HBM)
    res = pl.pallas_call(
        wrapped, name=name, grid=grid, in_specs=in_specs + [hbm] * nj, out_specs=out_specs + [hbm] * nj,
        out_shape=out_shape + [job.out_shape for job in jobs], scratch_shapes=scratch_shapes + _job_scratch(nj),
        compiler_params=_cp(*sem))(*args, *[job.src for job in jobs])
    return list(res[:n_out]), list(res[n_out:])


def _comm_only(jobs, name):
    nj = len(jobs)

    def body(*refs):
        srcs, dsts, sems = refs[:nj], refs[nj:2 * nj], refs[2 * nj:]
        for phase in ("start", "relay", "finish"):
            _run_jobs(phase, jobs, srcs, dsts, sems)

    hbm = pl.BlockSpec(memory_space=pltpu.HBM)
    return list(pl.pallas_call(
        body, name=name, in_specs=[hbm] * nj, out_specs=[hbm] * nj, out_shape=[job.out_shape for job in jobs],
        scratch_shapes=_job_scratch(nj))(*[job.src for job in jobs]))


def _norm_matmul(x, nrm, w, out_dtype, name, jobs=()):
    S, Dm = x.shape
    N = w.shape[1]
    tm = _tile(S, 512)
    cw = _col_chunk(N)

    def body(x_ref, nrm_ref, w_ref, o_ref, h_ref):
        xv = x_ref[...]
        rstd = lax.rsqrt(jnp.mean(xv * xv, axis=-1, keepdims=True) + EPS)
        hn = (xv * rstd) * nrm_ref[0:1, :]
        hb = (hn * (1.0 + nrm_ref[2:3, :]) + nrm_ref[1:2, :]).astype(BF16)
        h_ref[...] = hb
        for n in range(N // cw):
            o_ref[:, n * cw:(n + 1) * cw] = _dot(hb, w_ref[:, n * cw:(n + 1) * cw]).astype(o_ref.dtype)

    return _pcall(
        body, name=name, grid=(S // tm,),
        in_specs=[pl.BlockSpec((tm, Dm), lambda i: (i, 0)),
                  pl.BlockSpec((8, Dm), lambda i: (0, 0)),
                  _resident((Dm, N))],
        out_specs=[pl.BlockSpec((tm, N), lambda i: (i, 0)),
                   pl.BlockSpec((tm, Dm), lambda i: (i, 0))],
        out_shape=[jax.ShapeDtypeStruct((S, N), out_dtype), jax.ShapeDtypeStruct((S, Dm), BF16)],
        sem=("arbitrary",), args=(x, nrm, w), jobs=jobs)


def _hidden_chunks(k):
    return [(c0, min(6 * MXU_DIM, k - c0)) for c0 in range(0, k, 6 * MXU_DIM)]


def _loss_head(xv, gain, tgt, st_ref):
    dm = xv.shape[-1]
    rstd = lax.rsqrt(jnp.mean(xv * xv, axis=-1, keepdims=True) + EPS)
    xn = xv * rstd
    err = xn * gain - tgt
    st_ref[1:2, :] += jnp.full((1, dm), 0.5 / dm, F32) * jnp.sum(err * err)
    dy = err * (1.0 / dm)
    st_ref[0:1, :] += jnp.sum(dy * xn, axis=0, keepdims=True)
    dxn = dy * gain
    return rstd * (dxn - xn * jnp.mean(dxn * xn, axis=-1, keepdims=True))


def _ffn_fwd(x, nrm, w_gu, w_down, name, jobs=(), head=None):
    S, Dm = x.shape
    K = w_down.shape[0]
    tm = _tile(S, 256)

    def body(x_ref, nrm_ref, wgu_ref, wdn_ref, *rest):
        if head is None:
            o_ref, h_ref, gu_ref, y_ref = rest
        else:
            gf_ref, t_ref, o_ref, h_ref, gu_ref, y_ref, st_ref = rest

            @pl.when(pl.program_id(0) == 0)
            def _():
                st_ref[...] = jnp.zeros_like(st_ref)
        xv = x_ref[...]
        rstd = lax.rsqrt(jnp.mean(xv * xv, axis=-1, keepdims=True) + EPS)
        hn = (xv * rstd) * nrm_ref[0:1, :]
        hb = (hn * (1.0 + nrm_ref[2:3, :]) + nrm_ref[1:2, :]).astype(BF16)
        h_ref[...] = hb
        y = jnp.zeros((tm, Dm), F32)
        for c0, cs in _hidden_chunks(K):
            g = _dot(hb, wgu_ref[:, c0:c0 + cs])
            u = _dot(hb, wgu_ref[:, K + c0:K + c0 + cs])
            gu_ref[:, c0:c0 + cs] = g.astype(BF16)
            gu_ref[:, K + c0:K + c0 + cs] = u.astype(BF16)
            y = y + _dot((g * _sigmoid(g) * u).astype(BF16), wdn_ref[c0:c0 + cs, :])
        xo = xv + (0.5 * nrm_ref[3:4, :]) * y
        o_ref[...] = xo if head is None else _loss_head(xo, gf_ref[0:1, :], t_ref[...], st_ref)
        y_ref[...] = y.astype(BF16)

    row = lambda i: (i, 0)
    fix = lambda i: (0, 0)
    in_specs = [pl.BlockSpec((tm, Dm), row), pl.BlockSpec((8, Dm), fix), _resident((Dm, 2 * K)), _resident((K, Dm))]
    out_specs = [pl.BlockSpec((tm, Dm), row), pl.BlockSpec((tm, Dm), row), pl.BlockSpec((tm, 2 * K), row),
                 pl.BlockSpec((tm, Dm), row)]
    out_shape = [jax.ShapeDtypeStruct((S, Dm), F32), jax.ShapeDtypeStruct((S, Dm), BF16),
                 jax.ShapeDtypeStruct((S, 2 * K), BF16), jax.ShapeDtypeStruct((S, Dm), BF16)]
    args = (x, nrm, w_gu, w_down)
    if head is not None:
        in_specs += [pl.BlockSpec((8, Dm), fix), pl.BlockSpec((tm, Dm), row)]
        out_specs.append(pl.BlockSpec((8, Dm), fix))
        out_shape.append(jax.ShapeDtypeStruct((8, Dm), F32))
        args += tuple(head)
    return _pcall(body, name=name, grid=(S // tm,), in_specs=in_specs, out_specs=out_specs, out_shape=out_shape,
                  sem=("arbitrary",), args=args, jobs=jobs)


def _proj_residual(a, w, x, nrm, name, jobs=()):
    S, K = a.shape
    Dm = w.shape[1]
    tm = _tile(S, 256)

    def body(a_ref, w_ref, x_ref, nrm_ref, o_ref, y_ref):
        y = _dot(a_ref[...], w_ref[...])
        o_ref[...] = x_ref[...] + nrm_ref[3:4, :] * y
        y_ref[...] = y.astype(BF16)

    return _pcall(
        body, name=name, grid=(S // tm,),
        in_specs=[pl.BlockSpec((tm, K), lambda i: (i, 0)),
                  _resident((K, Dm)),
                  pl.BlockSpec((tm, Dm), lambda i: (i, 0)),
                  pl.BlockSpec((8, Dm), lambda i: (0, 0))],
        out_specs=[pl.BlockSpec((tm, Dm), lambda i: (i, 0)),
                   pl.BlockSpec((tm, Dm), lambda i: (i, 0))],
        out_shape=[jax.ShapeDtypeStruct((S, Dm), F32), jax.ShapeDtypeStruct((S, Dm), BF16)],
        sem=("arbitrary",), args=(a, w, x, nrm), jobs=jobs)


def _proj_residual_bwd(dxo, a, y, w, nrm, coef, swiglu, name, jobs=()):
    S, Dm = dxo.shape
    K = w.shape[0]
    Ka = a.shape[1]
    tm = _tile(S, 256)
    n_steps = S // tm
    chunks = _hidden_chunks(K)

    def body(dxo_ref, y_ref, w_ref, nrm_ref, a_ref, da_ref, dgate_ref, dw_ref, acc):
        @pl.when(pl.program_id(0) == 0)
        def _():
            dgate_ref[...] = jnp.zeros_like(dgate_ref)
            acc[...] = jnp.zeros_like(acc)

        dxo_v = dxo_ref[...]
        dyb = ((coef * nrm_ref[3:4, :]) * dxo_v).astype(BF16)
        dgate_ref[0:1, :] += jnp.sum(coef * y_ref[...].astype(F32) * dxo_v, axis=0, keepdims=True)
        for c0, cs in chunks:
            dact = _dot_nt(dyb, w_ref[c0:c0 + cs, :])
            if swiglu:
                g = a_ref[:, c0:c0 + cs].astype(F32)
                u = a_ref[:, K + c0:K + c0 + cs].astype(F32)
                s = _sigmoid(g)
                si = g * s
                da_ref[:, c0:c0 + cs] = (dact * u * (s * (1.0 + g * (1.0 - s)))).astype(BF16)
                da_ref[:, K + c0:K + c0 + cs] = (dact * si).astype(BF16)
                act = (si * u).astype(BF16)
            else:
                da_ref[:, c0:c0 + cs] = dact
                act = a_ref[:, c0:c0 + cs]
            acc[c0:c0 + cs, :] += _dot_tn(act, dyb)

        @pl.when(pl.program_id(0) == n_steps - 1)
        def _():
            dw_ref[...] = acc[...].astype(BF16)

    row = lambda i: (i, 0)
    fix = lambda i: (0, 0)
    return _pcall(
        body, name=name, grid=(n_steps,),
        in_specs=[pl.BlockSpec((tm, Dm), row), pl.BlockSpec((tm, Dm), row), _resident((K, Dm)),
                  pl.BlockSpec((8, Dm), fix), pl.BlockSpec((tm, Ka), row)],
        out_specs=[pl.BlockSpec((tm, Ka), row), pl.BlockSpec((8, Dm), fix), _resident((K, Dm))],
        out_shape=[jax.ShapeDtypeStruct((S, Ka), BF16 if swiglu else F32), jax.ShapeDtypeStruct((8, Dm), F32),
                   jax.ShapeDtypeStruct((K, Dm), BF16)],
        scratch_shapes=[pltpu.VMEM((K, Dm), F32)],
        sem=("arbitrary",), args=(dxo, y, w, nrm, a), jobs=jobs)


def _atb(a, b, out_dtype, bm, bn, name, jobs=()):
    S, M = a.shape
    N = b.shape[1]
    bk = _tile(S, 1024)
    nk = S // bk

    def body(a_ref, b_ref, o_ref, acc):
        k = pl.program_id(2)

        @pl.when(k == 0)
        def _():
            acc[...] = jnp.zeros_like(acc)
        acc[...] += _dot_tn(a_ref[...], b_ref[...])

        @pl.when(k == nk - 1)
        def _():
            o_ref[...] = acc[...].astype(o_ref.dtype)

    (out,), extra = _pcall(
        body, name=name, grid=(M // bm, N // bn, nk),
        in_specs=[pl.BlockSpec((bk, bm), lambda m, n, k: (k, m)),
                  pl.BlockSpec((bk, bn), lambda m, n, k: (k, n))],
        out_specs=[pl.BlockSpec((bm, bn), lambda m, n, k: (m, n))],
        out_shape=[jax.ShapeDtypeStruct((M, N), out_dtype)],
        scratch_shapes=[pltpu.VMEM((bm, bn), F32)],
        sem=("arbitrary", "arbitrary", "arbitrary"), args=(a, b), jobs=jobs)
    return out, extra


def _nt_norm_bwd(dout, w, x, nrm, dxo, name, jobs=()):
    S, N = dout.shape
    Dm = w.shape[0]
    tm = _tile(S, 512)

    def body(do_ref, w_ref, x_ref, nrm_ref, dxo_ref, dx_ref, red_ref):
        @pl.when(pl.program_id(0) == 0)
        def _():
            red_ref[...] = jnp.zeros_like(red_ref)
        dh = _dot_nt(do_ref[...], w_ref[...])
        xv = x_ref[...]
        rstd = lax.rsqrt(jnp.mean(xv * xv, axis=-1, keepdims=True) + EPS)
        xn = xv * rstd
        gain = nrm_ref[0:1, :]
        hn = xn * gain
        dhn = dh * (1.0 + nrm_ref[2:3, :])
        red_ref[0:1, :] += jnp.sum(dh, axis=0, keepdims=True)
        red_ref[1:2, :] += jnp.sum(dh * hn, axis=0, keepdims=True)
        red_ref[2:3, :] += jnp.sum(dhn * xn, axis=0, keepdims=True)
        dxn = dhn * gain
        dx = rstd * (dxn - xn * jnp.mean(dxn * xn, axis=-1, keepdims=True))
        dx_ref[...] = dxo_ref[...] + dx

    return _pcall(
        body, name=name, grid=(S // tm,),
        in_specs=[pl.BlockSpec((tm, N), lambda i: (i, 0)),
                  _resident((Dm, N)),
                  pl.BlockSpec((tm, Dm), lambda i: (i, 0)),
                  pl.BlockSpec((8, Dm), lambda i: (0, 0)),
                  pl.BlockSpec((tm, Dm), lambda i: (i, 0))],
        out_specs=[pl.BlockSpec((tm, Dm), lambda i: (i, 0)),
                   pl.BlockSpec((8, Dm), lambda i: (0, 0))],
        out_shape=[jax.ShapeDtypeStruct((S, Dm), F32), jax.ShapeDtypeStruct((8, Dm), F32)],
        sem=("arbitrary",), args=(dout, w, x, nrm, dxo), jobs=jobs)


def _alibi_slope(h):
    return float(2.0 ** (-8.0 * (h + 1) / N_Q_HEADS))


def _head_planes(pair_cols):
    lane = lax.broadcasted_iota(jnp.int32, pair_cols.shape, 1)
    low = lane < HEAD_DIM
    h0_lo = jnp.where(low, pair_cols, 0.0)
    h1_hi = jnp.where(low, 0.0, pair_cols)
    h0_hi = pltpu.roll(h0_lo, HEAD_DIM, 1)
    h1_lo = pltpu.roll(h1_hi, HEAD_DIM, 1)
    return ((h0_lo.astype(BF16), h0_hi.astype(BF16)), (h1_lo.astype(BF16), h1_hi.astype(BF16)))


def _band_geometry(first_block):
    qi = lax.broadcasted_iota(jnp.int32, (BLOCK, BLOCK), 0)
    kj = lax.broadcasted_iota(jnp.int32, (BLOCK, BLOCK), 1)
    own = kj <= qi
    dist = jnp.where(own, qi - kj, qi + BLOCK - kj).astype(F32)
    valid = kj <= qi + BLOCK * (1 - first_block)
    return own, dist, valid


def _fold(band, own):
    return jnp.where(own, band[:, BLOCK:], band[:, :BLOCK])


def _unfold(v, own):
    return jnp.concatenate([jnp.where(own, 0.0, v), jnp.where(own, v, 0.0)], axis=1)


def _softmax_band(s, h, geometry, sink):
    own, dist, valid = geometry
    s = jnp.where(valid, s - _alibi_slope(h) * dist, NEG_INF)
    m = jnp.maximum(jnp.max(s, axis=-1, keepdims=True), sink)
    p = jnp.exp(s - m)
    e_sink = jnp.exp(sink - m)
    inv = 1.0 / (jnp.sum(p, axis=-1, keepdims=True) + e_sink)
    return p * inv, e_sink * inv


def _past(cur, prev, s, row):
    return jnp.where(row < s, pltpu.roll(prev, s, 0), pltpu.roll(cur, s, 0))


def _future(cur, nxt, s, row):
    T = cur.shape[0]
    return jnp.where(row >= T - s, pltpu.roll(nxt, T - s, 0), pltpu.roll(cur, T - s, 0))


def _edge_row(v, last):
    T = v.shape[0]
    r8 = lax.broadcasted_iota(jnp.int32, (SUBLANES, v.shape[1]), 0)
    blk = v[T - SUBLANES:, :] if last else v[:SUBLANES, :]
    return jnp.sum(jnp.where(r8 == (SUBLANES - 1 if last else 0), blk, 0.0), axis=0, keepdims=True)


def _lru_gates(lx, lx_prev, small_ref, wa_ref, wx_ref, row, t0):
    xc = (small_ref[4:5, :] + small_ref[3:4, :] * lx + small_ref[2:3, :] * _past(lx, lx_prev, 1, row)
          + small_ref[1:2, :] * _past(lx, lx_prev, 2, row) + small_ref[0:1, :] * _past(lx, lx_prev, 3, row))
    xcb = xc.astype(BF16)
    r = _sigmoid(_dot(xcb, wa_ref[...]) + small_ref[5:6, :])
    ig = _sigmoid(_dot(xcb, wx_ref[...]) + small_ref[6:7, :])
    sp = _softplus_neg(small_ref[7:8, :])
    la = (-LRU_C) * r * sp
    a = jnp.exp(la)
    first = (row + t0) == 0
    mult = jnp.where(first, 1.0, jnp.sqrt(-_expm1(2.0 * la)))
    return xc, xcb, r, ig, sp, a, mult, first


def _mixer_fwd(proj, sinks, small, wa, wx, name, jobs=()):
    S = proj.shape[0]
    T = MIX_TILE
    nT = S // T
    nb = T // BLOCK

    def body(proj_ref, sink_ref, small_ref, wa_ref, wx_ref, y_ref, hp_ref, kvp, lxp, zp, hcar):
        i = pl.program_id(0)

        @pl.when(i == 0)
        def _():
            kvp[...] = jnp.zeros_like(kvp)
            lxp[...] = jnp.zeros_like(lxp)
            zp[...] = jnp.zeros_like(zp)
            hcar[...] = jnp.zeros_like(hcar)

        row = lax.broadcasted_iota(jnp.int32, (T, LRU_WIDTH), 0)

        kv = proj_ref[:, C_KV:C_KV + 2 * KV_WIDTH]
        ext = jnp.concatenate([kvp[...], kv], axis=0)
        kx = _head_planes(ext[:, :KV_WIDTH])
        vx = _head_planes(ext[:, KV_WIDTH:])
        first_tile = jnp.where(i == 0, 1, 0)
        units = [(b, pair, e) for b in range(nb) for pair in range(N_Q_HEADS // 2) for e in range(2)]
        geometry = [_band_geometry(first_tile if b == 0 else 0) for b in range(nb)]
        keys = [slice(b * BLOCK, (b + 2) * BLOCK) for b in range(nb)]
        qp = {(b, pair): (proj_ref[b * BLOCK:(b + 1) * BLOCK, pair * LANES:(pair + 1) * LANES] * 0.125).astype(BF16)
              for b in range(nb) for pair in range(N_Q_HEADS // 2)}
        scores = [_fold(_dot_nt(qp[(b, pair)], kx[pair // 2][e][keys[b]]), geometry[b][0]) for b, pair, e in units]
        probs = [_unfold(_softmax_band(s, 2 * pair + e, geometry[b], sink_ref[2 * pair + e])[0],
                         geometry[b][0]).astype(BF16) for s, (b, pair, e) in zip(scores, units)]
        outs = [_dot(p, vx[pair // 2][e][keys[b]]) for p, (b, pair, e) in zip(probs, units)]
        for u in range(0, len(units), 2):
            b, pair, _ = units[u]
            y_ref[b * BLOCK:(b + 1) * BLOCK, pair * LANES:(pair + 1) * LANES] = (outs[u] + outs[u + 1]).astype(BF16)
        kvp[...] = kv[T - BLOCK:, :]

        lx = proj_ref[:, C_LX:C_LX + LRU_WIDTH]
        xc, _, _, ig, _, a, mult, _ = _lru_gates(lx, lxp[...], small_ref, wa_ref, wx_ref, row, i * T)
        lxp[...] = lx
        aa = a
        bb = mult * (ig * xc)
        s = 1
        while s < T:
            a_sh = jnp.where(row >= s, pltpu.roll(aa, s, 0), 1.0)
            b_sh = jnp.where(row >= s, pltpu.roll(bb, s, 0), 0.0)
            bb = aa * b_sh + bb
            aa = aa * a_sh
            s *= 2
        hc = hcar[0:1, :]
        hh = bb + aa * hc
        hp_ref[...] = jnp.where(row < 1, hc, pltpu.roll(hh, 1, 0))
        hcar[...] = jnp.broadcast_to(_edge_row(hh, True), hcar.shape)
        gl, _ = _gelu(proj_ref[:, C_LG:C_LG + LRU_WIDTH])
        y_ref[:, ATTN_WIDTH:ATTN_WIDTH + LRU_WIDTH] = (gl * hh).astype(BF16)

        z = proj_ref[:, C_SC:C_SC + CONV_WIDTH] * proj_ref[:, C_SX:C_SX + CONV_WIDTH]
        c3 = (small_ref[10:11, :] * z + small_ref[9:10, :] * _past(z, zp[...], 1, row)
              + small_ref[8:9, :] * _past(z, zp[...], 2, row))
        zp[...] = z
        y_ref[:, ATTN_WIDTH + LRU_WIDTH:] = (proj_ref[:, C_SB:C_SB + CONV_WIDTH] * c3).astype(BF16)

    fix = lambda i: (0, 0)
    return _pcall(
        body, name=name, grid=(nT,),
        in_specs=[pl.BlockSpec((T, IN_PROJ_WIDTH), lambda i: (i, 0)),
                  pl.BlockSpec(memory_space=pltpu.SMEM),
                  pl.BlockSpec((16, LRU_WIDTH), fix),
                  pl.BlockSpec((LRU_WIDTH, LRU_WIDTH), fix),
                  pl.BlockSpec((LRU_WIDTH, LRU_WIDTH), fix)],
        out_specs=[pl.BlockSpec((T, D_MODEL), lambda i: (i, 0)),
                   pl.BlockSpec((T, LRU_WIDTH), lambda i: (i, 0))],
        out_shape=[jax.ShapeDtypeStruct((S, D_MODEL), BF16), jax.ShapeDtypeStruct((S, LRU_WIDTH), F32)],
        scratch_shapes=[pltpu.VMEM((BLOCK, 2 * KV_WIDTH), F32), pltpu.VMEM((T, LRU_WIDTH), F32),
                        pltpu.VMEM((T, CONV_WIDTH), F32), pltpu.VMEM((SUBLANES, LRU_WIDTH), F32)],
        sem=("arbitrary",), args=(proj, sinks, small, wa, wx), jobs=jobs)


def _mixer_bwd(proj, dymix, hprev, sinks, small, wa, wx, name, jobs=()):
    S = proj.shape[0]
    T = MIX_TILE
    nT = S // T
    nb = T // BLOCK
    bpt = T // BLOCK

    def body(proj_ref, kvprev_ref, lxprev_ref, scprev_ref, sxprev_ref, dy_ref, hp_ref, sink_ref, small_ref,
             wa_ref, wx_ref, dp_ref, dsm_ref, dsink_ref, dwa_ref, dwx_ref,
             dk_s, dv_s, dkv_c, dxc_n, dc3_n, p_c):
        i = pl.program_id(0)
        ti = nT - 1 - i
        has_prev = jnp.where(ti == 0, 0.0, 1.0)

        @pl.when(i == 0)
        def _():
            for r in (dkv_c, dxc_n, dc3_n, p_c, dsm_ref, dsink_ref, dwa_ref, dwx_ref):
                r[...] = jnp.zeros_like(r)

        row = lax.broadcasted_iota(jnp.int32, (T, LRU_WIDTH), 0)

        kv = proj_ref[:, C_KV:C_KV + 2 * KV_WIDTH]
        ext = jnp.concatenate([kvprev_ref[...] * has_prev, kv], axis=0)
        kx = _head_planes(ext[:, :KV_WIDTH])
        vx = _head_planes(ext[:, KV_WIDTH:])
        dk_s[...] = jnp.zeros_like(dk_s)
        dv_s[...] = jnp.zeros_like(dv_s)
        dk_s[:, T:] = dkv_c[:, :BLOCK]
        dv_s[:, T:] = dkv_c[:, BLOCK:]
        first_tile = jnp.where(ti == 0, 1, 0)
        units = [(b, pair, e) for b in range(nb) for pair in range(N_Q_HEADS // 2) for e in range(2)]
        geometry = [_band_geometry(first_tile if b == 0 else 0) for b in range(nb)]
        keys = [slice(b * BLOCK, (b + 2) * BLOCK) for b in range(nb)]
        tile = {(b, pair): (slice(b * BLOCK, (b + 1) * BLOCK), slice(pair * LANES, (pair + 1) * LANES))
                for b in range(nb) for pair in range(N_Q_HEADS // 2)}
        qp = {k: (proj_ref[rc] * 0.125).astype(BF16) for k, rc in tile.items()}
        dob = {k: dy_ref[rc].astype(BF16) for k, rc in tile.items()}
        qp_t = {k: jnp.transpose(proj_ref[rc] * 0.125).astype(BF16) for k, rc in tile.items()}
        dob_t = {k: jnp.transpose(dy_ref[rc]).astype(BF16) for k, rc in tile.items()}
        scores = [_fold(_dot_nt(qp[(b, pair)], kx[pair // 2][e][keys[b]]), geometry[b][0]) for b, pair, e in units]
        dprob = [_fold(_dot_nt(dob[(b, pair)], vx[pair // 2][e][keys[b]]), geometry[b][0]) for b, pair, e in units]
        pn_wide, ds_wide = [], []
        for s, dpm, (b, pair, e) in zip(scores, dprob, units):
            h = 2 * pair + e
            own = geometry[b][0]
            pn, psink = _softmax_band(s, h, geometry[b], sink_ref[h])
            dsum = jnp.sum(pn * dpm, axis=-1, keepdims=True)
            dsink_ref[h:h + 1, :] += jnp.full((1, LANES), -1.0, F32) * jnp.sum(psink * dsum)
            pn_wide.append(_unfold(pn, own).astype(BF16))
            ds_wide.append(_unfold(pn * (dpm - dsum), own).astype(BF16))
        dq = {}
        for pw, ds, (b, pair, e) in zip(pn_wide, ds_wide, units):
            g = pair // 2
            head_e = slice(e * HEAD_DIM, (e + 1) * HEAD_DIM)
            head_g = slice(g * HEAD_DIM, (g + 1) * HEAD_DIM)
            dv_s[head_g, keys[b]] += _dot(dob_t[(b, pair)], pw)[head_e, :]
            dk_s[head_g, keys[b]] += _dot(qp_t[(b, pair)], ds)[head_e, :]
            part = _dot(ds, kx[g][e][keys[b]])
            dq[(b, pair)] = part if e == 0 else dq[(b, pair)] + part
        for k, rc in tile.items():
            dp_ref[rc] = (0.125 * dq[k]).astype(BF16)
        dp_ref[:, C_KV:C_KV + KV_WIDTH] = jnp.transpose(dk_s[:, BLOCK:]).astype(BF16)
        dp_ref[:, C_KV + KV_WIDTH:C_KV + 2 * KV_WIDTH] = jnp.transpose(dv_s[:, BLOCK:]).astype(BF16)
        dkv_c[:, :BLOCK] = dk_s[:, :BLOCK]
        dkv_c[:, BLOCK:] = dv_s[:, :BLOCK]

        lx = proj_ref[:, C_LX:C_LX + LRU_WIDTH]
        lxprev = lxprev_ref[...] * has_prev
        xc, xcb, r, ig, sp, a, mult, first = _lru_gates(lx, lxprev, small_ref, wa_ref, wx_ref, row, ti * T)
        hp = hp_ref[...]
        hh = a * hp + mult * (ig * xc)
        lg = proj_ref[:, C_LG:C_LG + LRU_WIDTH]
        gl, th = _gelu(lg)
        dyl = dy_ref[:, ATTN_WIDTH:ATTN_WIDTH + LRU_WIDTH]
        dp_ref[:, C_LG:C_LG + LRU_WIDTH] = (dyl * hh * _gelu_grad(lg, th)).astype(BF16)
        aa = jnp.where(row < T - 1, pltpu.roll(a, T - 1, 0), 1.0)
        bb = dyl * gl
        s = 1
        while s < T:
            a_sh = jnp.where(row < T - s, pltpu.roll(aa, T - s, 0), 1.0)
            b_sh = jnp.where(row < T - s, pltpu.roll(bb, T - s, 0), 0.0)
            bb = bb + aa * b_sh
            aa = aa * a_sh
            s *= 2
        G = bb + aa * p_c[0:1, :]
        p_c[...] = jnp.broadcast_to(_edge_row(a * G, False), p_c.shape)
        da = G * hp
        dmult = G * (ig * xc)
        dig = G * mult * xc
        dxc = G * mult * ig
        dla = da * a + dmult * jnp.where(first, 0.0, -(a * a) / mult)
        dr = dla * ((-LRU_C) * sp)
        lam = small_ref[7:8, :]
        dsm_ref[7:8, :] += jnp.sum(dla * ((-LRU_C) * r), axis=0, keepdims=True) * (-_sigmoid(-lam))
        dpa = dr * r * (1.0 - r)
        dpx = dig * ig * (1.0 - ig)
        dsm_ref[5:6, :] += jnp.sum(dpa, axis=0, keepdims=True)
        dsm_ref[6:7, :] += jnp.sum(dpx, axis=0, keepdims=True)
        dpab = dpa.astype(BF16)
        dpxb = dpx.astype(BF16)
        dwa_ref[...] += _dot_tn(xcb, dpab)
        dwx_ref[...] += _dot_tn(xcb, dpxb)
        dxc = dxc + _dot_nt(dpab, wa_ref[...]) + _dot_nt(dpxb, wx_ref[...])
        dsm_ref[4:5, :] += jnp.sum(dxc, axis=0, keepdims=True)
        dsm_ref[3:4, :] += jnp.sum(dxc * lx, axis=0, keepdims=True)
        for k in range(3):
            dsm_ref[k:k + 1, :] += jnp.sum(dxc * _past(lx, lxprev, 3 - k, row), axis=0, keepdims=True)
        nxt = dxc_n[...]
        dlx = (small_ref[3:4, :] * dxc + small_ref[2:3, :] * _future(dxc, nxt, 1, row)
               + small_ref[1:2, :] * _future(dxc, nxt, 2, row) + small_ref[0:1, :] * _future(dxc, nxt, 3, row))
        dxc_n[...] = dxc
        dp_ref[:, C_LX:C_LX + LRU_WIDTH] = dlx.astype(BF16)

        sc = proj_ref[:, C_SC:C_SC + CONV_WIDTH]
        sx = proj_ref[:, C_SX:C_SX + CONV_WIDTH]
        sb = proj_ref[:, C_SB:C_SB + CONV_WIDTH]
        z = sc * sx
        zprev = (scprev_ref[...] * sxprev_ref[...]) * has_prev
        z1 = _past(z, zprev, 1, row)
        z2 = _past(z, zprev, 2, row)
        c3 = small_ref[10:11, :] * z + small_ref[9:10, :] * z1 + small_ref[8:9, :] * z2
        dys = dy_ref[:, ATTN_WIDTH + LRU_WIDTH:]
        dp_ref[:, C_SB:C_SB + CONV_WIDTH] = (dys * c3).astype(BF16)
        dc3 = dys * sb
        dsm_ref[10:11, :] += jnp.sum(dc3 * z, axis=0, keepdims=True)
        dsm_ref[9:10, :] += jnp.sum(dc3 * z1, axis=0, keepdims=True)
        dsm_ref[8:9, :] += jnp.sum(dc3 * z2, axis=0, keepdims=True)
        nxt3 = dc3_n[...]
        dz = (small_ref[10:11, :] * dc3 + small_ref[9:10, :] * _future(dc3, nxt3, 1, row)
              + small_ref[8:9, :] * _future(dc3, nxt3, 2, row))
        dc3_n[...] = dc3
        dp_ref[:, C_SC:C_SC + CONV_WIDTH] = (dz * sx).astype(BF16)
        dp_ref[:, C_SX:C_SX + CONV_WIDTH] = (dz * sc).astype(BF16)

    fix = lambda i: (0, 0)
    cur = lambda i: (nT - 1 - i, 0)
    prev_cols = lambda cb: (lambda i: (jnp.maximum(nT - 2 - i, 0), cb))
    return _pcall(
        body, name=name, grid=(nT,),
        in_specs=[pl.BlockSpec((T, IN_PROJ_WIDTH), cur),
                  pl.BlockSpec((BLOCK, 2 * KV_WIDTH),
                               lambda i: (jnp.maximum((nT - 1 - i) * bpt - 1, 0), C_KV // (2 * KV_WIDTH))),
                  pl.BlockSpec((T, LRU_WIDTH), prev_cols(C_LX // LRU_WIDTH)),
                  pl.BlockSpec((T, CONV_WIDTH), prev_cols(C_SC // CONV_WIDTH)),
                  pl.BlockSpec((T, CONV_WIDTH), prev_cols(C_SX // CONV_WIDTH)),
                  pl.BlockSpec((T, D_MODEL), cur),
                  pl.BlockSpec((T, LRU_WIDTH), cur),
                  pl.BlockSpec(memory_space=pltpu.SMEM),
                  pl.BlockSpec((16, LRU_WIDTH), fix),
                  pl.BlockSpec((LRU_WIDTH, LRU_WIDTH), fix),
                  pl.BlockSpec((LRU_WIDTH, LRU_WIDTH), fix)],
        out_specs=[pl.BlockSpec((T, IN_PROJ_WIDTH), cur),
                   pl.BlockSpec((16, LRU_WIDTH), fix),
                   pl.BlockSpec((SUBLANES, LANES), fix),
                   pl.BlockSpec((LRU_WIDTH, LRU_WIDTH), fix),
                   pl.BlockSpec((LRU_WIDTH, LRU_WIDTH), fix)],
        out_shape=[jax.ShapeDtypeStruct((S, IN_PROJ_WIDTH), BF16),
                   jax.ShapeDtypeStruct((16, LRU_WIDTH), F32),
                   jax.ShapeDtypeStruct((SUBLANES, LANES), F32),
                   jax.ShapeDtypeStruct((LRU_WIDTH, LRU_WIDTH), F32),
                   jax.ShapeDtypeStruct((LRU_WIDTH, LRU_WIDTH), F32)],
        scratch_shapes=[pltpu.VMEM((KV_WIDTH, T + BLOCK), F32), pltpu.VMEM((KV_WIDTH, T + BLOCK), F32),
                        pltpu.VMEM((BLOCK, 2 * KV_WIDTH), F32), pltpu.VMEM((T, LRU_WIDTH), F32),
                        pltpu.VMEM((T, CONV_WIDTH), F32), pltpu.VMEM((SUBLANES, LRU_WIDTH), F32)],
        sem=("arbitrary",), args=(proj, proj, proj, proj, proj, dymix, hprev, sinks, small, wa, wx), jobs=jobs)


def _mod_matmul(c_all, w_mod, name):
    L, Dm, N = w_mod.shape
    R = c_all.shape[0]
    tn = 768

    def body(c_ref, w_ref, o_ref, ca_ref):
        cv = c_ref[...]
        ca = (cv * _sigmoid(cv)).astype(BF16)
        ca_ref[...] = ca
        o_ref[0] = _dot(ca, w_ref[0].astype(BF16))

    return pl.pallas_call(
        body, name=name, grid=(L, N // tn),
        in_specs=[pl.BlockSpec((R, Dm), lambda l, n: (0, 0)),
                  pl.BlockSpec((1, Dm, tn), lambda l, n: (l, 0, n))],
        out_specs=[pl.BlockSpec((1, R, tn), lambda l, n: (l, 0, n)), pl.BlockSpec((R, Dm), lambda l, n: (0, 0))],
        out_shape=[jax.ShapeDtypeStruct((L, R, N), F32), jax.ShapeDtypeStruct((R, Dm), BF16)],
        compiler_params=_cp("arbitrary", "arbitrary"),
    )(c_all, w_mod)


def _adamw_update(g, w_ref, m_ref, v_ref, go_ref, d_ref, mo_ref, vo_ref):
    mn = ADAM_B1 * m_ref[...] + (1.0 - ADAM_B1) * g
    vn = ADAM_B2 * v_ref[...] + (1.0 - ADAM_B2) * (g * g)
    go_ref[...] = g
    mo_ref[...] = mn
    vo_ref[...] = vn
    m_hat = mn / (1.0 - ADAM_B1 ** ADAM_STEP)
    v_hat = vn / (1.0 - ADAM_B2 ** ADAM_STEP)
    d_ref[...] = (-ADAM_LR) * (m_hat / (jnp.sqrt(v_hat) + ADAM_EPS) + ADAM_WD * w_ref[...])


def _adamw(w, g, m, v, name):
    R, C = w.shape
    tr = 8
    for cand in (512, 256, 128, 64, 32, 16, 8):
        if R % cand == 0 and cand * C * 4 <= (1 << 20):
            tr = cand
            break

    def body(w_ref, g_ref, *rest):
        _adamw_update(g_ref[...], w_ref, *rest)

    spec = pl.BlockSpec((tr, C), lambda i: (i, 0))
    return _pcall(body, name=name, grid=(R // tr,), in_specs=[spec] * 4, out_specs=[spec] * 4,
                  out_shape=[jax.ShapeDtypeStruct((R, C), F32)] * 4, sem=("arbitrary",), args=(w, g, m, v))


def _adamw_partials(w, partials, m, v, name):
    nl = len(partials)
    _, R, C = partials[0][0].shape
    tr = 8
    for cand in (256, 128, 64, 32, 16):
        if R % cand == 0 and cand * C * 4 <= (1 << 19):
            tr = cand
            break
    ni = R // tr

    def body(*refs):
        w_ref, p_refs = refs[0], refs[1:1 + 2 * nl]
        m_ref, v_ref, go_ref, d_ref, mo_ref, vo_ref = refs[1 + 2 * nl:]
        for l in range(nl):
            @pl.when(pl.program_id(0) == l)
            def _(pair=p_refs[2 * l:2 * l + 2]):
                own, sib = [((p[0].astype(F32) + p[1].astype(F32)) + p[2].astype(F32)) + p[3].astype(F32)
                            for p in pair]
                _adamw_update(own + sib, w_ref, m_ref, v_ref, go_ref, d_ref, mo_ref, vo_ref)

    def slots(l):
        return pl.BlockSpec((N_CHIPS, tr, C),
                            lambda ll, i: (0, jnp.where(ll == l, i, jnp.where(ll < l, 0, ni - 1)), 0))

    spec = pl.BlockSpec((tr, C), lambda ll, i: (ll * ni + i, 0))
    return pl.pallas_call(
        body, name=name, grid=(nl, ni),
        in_specs=[spec] + [slots(l) for l in range(nl) for _ in range(2)] + [spec, spec], out_specs=[spec] * 4,
        out_shape=[jax.ShapeDtypeStruct((nl * R, C), F32)] * 4,
        compiler_params=_cp("arbitrary", "arbitrary"),
    )(w, *[p for pair in partials for p in pair], m, v)


def _all_gather_small(v, name):
    M, N = v.shape

    def body(x_ref, out_ref, sum_ref, send_sems, recv_sems, local_sem):
        x, y, c = lax.axis_index("x"), lax.axis_index("y"), lax.axis_index("c")
        me, sibling = (x, y, c), (x, y, 1 - c)
        chips = [(1 - x, y), (x, 1 - y), (1 - x, 1 - y)]

        def rows(px, py, pc):
            return out_ref.at[pl.ds(pl.multiple_of((4 * px + 2 * py + pc) * M, SUBLANES), M), :]

        def copy(k, block, to, src=None):
            return pltpu.make_async_remote_copy(
                src_ref=rows(*block) if src is None else src, dst_ref=rows(*block),
                send_sem=send_sems.at[k], recv_sem=recv_sems.at[k], device_id=to, device_id_type=MESH)

        mine = pltpu.make_async_copy(x_ref, rows(*me), local_sem)
        mine.start()
        first = [copy(0, me, sibling, src=x_ref)]
        first += [copy(1 + j, me, (*chip, c), src=x_ref) for j, chip in enumerate(chips)]
        for cp in first:
            cp.start()
        passed = [copy(4 + j, (*chip, c), sibling) for j, chip in enumerate(chips)]
        for j, chip in enumerate(chips):
            copy(1 + j, (*chip, c), me).wait_recv()
            passed[j].start()
        copy(0, sibling, me).wait_recv()
        for j, chip in enumerate(chips):
            copy(4 + j, (*chip, 1 - c), me).wait_recv()
        for cp in first + passed:
            cp.wait_send()
        mine.wait()
        acc = out_ref[0:M, :]
        for d in range(1, N_DEV):
            acc = acc + out_ref[d * M:(d + 1) * M, :]
        sum_ref[...] = acc

    return pl.pallas_call(
        body, name=name,
        out_shape=[jax.ShapeDtypeStruct((N_DEV * M, N), F32), jax.ShapeDtypeStruct((M, N), F32)],
        in_specs=[pl.BlockSpec(memory_space=pltpu.VMEM)],
        out_specs=[pl.BlockSpec(memory_space=pltpu.VMEM), pl.BlockSpec(memory_space=pltpu.VMEM)],
        scratch_shapes=[pltpu.SemaphoreType.DMA((7,)), pltpu.SemaphoreType.DMA((7,)), pltpu.SemaphoreType.DMA],
        compiler_params=pltpu.CompilerParams(vmem_limit_bytes=VMEM_LIMIT),
    )(v)


_BIG = (("w_ffn1_gu", 1), ("w_ffn1_down", 0), ("w_ffn2_gu", 1), ("w_ffn2_down", 0), ("w_in", 1), ("w_out", 0))
_AXIS = dict(_BIG)

_GATHER_PLAN = {
    "first": [(0, "w_ffn1_gu"), (0, "w_ffn1_down")],
    (0, "ffn1"): [(0, "w_in"), (0, "w_out"), (0, "w_ffn2_gu")],
    (0, "mix_core"): [(0, "w_ffn2_down")],
    (0, "ffn2"): [(1, "w_ffn1_gu"), (1, "w_ffn1_down")],
    (1, "ffn1"): [(1, "w_in"), (1, "w_out"), (1, "w_ffn2_gu")],
    (1, "mix_core"): [(1, "w_ffn2_down")],
}


def _pack(arrs, rows_multiple=SUBLANES):
    flat = jnp.concatenate([a.astype(F32).reshape(-1) for a in arrs])
    unit = rows_multiple * LANES
    total = -(-flat.shape[0] // unit) * unit
    return jnp.pad(flat, (0, total - flat.shape[0])).reshape(total // LANES, LANES)


def _unpack(flat, shapes):
    out, off = [], 0
    for shp in shapes:
        n = int(math.prod(shp))
        out.append(flat[off:off + n].reshape(shp))
        off += n
    return out


def _block_diag(w):
    out = jnp.zeros((LRU_WIDTH, LRU_WIDTH), F32)
    for h in range(4):
        out = lax.dynamic_update_slice(out, w[h], (h * HEAD_DIM, h * HEAD_DIM))
    return out


def _diag_blocks(w):
    return jnp.stack([w[h * HEAD_DIM:(h + 1) * HEAD_DIM, h * HEAD_DIM:(h + 1) * HEAD_DIM] for h in range(4)])


def _rows8(*rows):
    z = jnp.zeros((8 - len(rows), rows[0].shape[-1]), F32)
    return jnp.concatenate([jnp.stack(rows), z], axis=0)


def kernel(x, c, w_mod, b_mod, g_norm, w_ffn1_gu, w_ffn1_down, w_ffn2_gu, w_ffn2_down, w_in, w_out, attn_sinks, lru_conv_w, lru_conv_b, lru_gate_a_w, lru_gate_a_b, lru_gate_x_w, lru_gate_x_b, lru_lambda, sc_conv_w, g_final, loss_target, m_w_mod, m_b_mod, m_g_norm, m_w_ffn1_gu, m_w_ffn1_down, m_w_ffn2_gu, m_w_ffn2_down, m_w_in, m_w_out, m_attn_sinks, m_lru_conv_w, m_lru_conv_b, m_lru_gate_a_w, m_lru_gate_a_b, m_lru_gate_x_w, m_lru_gate_x_b, m_lru_lambda, m_sc_conv_w, m_g_final, v_w_mod, v_b_mod, v_g_norm, v_w_ffn1_gu, v_w_ffn1_down, v_w_ffn2_gu, v_w_ffn2_down, v_w_in, v_w_out, v_attn_sinks, v_lru_conv_w, v_lru_conv_b, v_lru_gate_a_w, v_lru_gate_a_b, v_lru_gate_x_w, v_lru_gate_x_b, v_lru_lambda, v_sc_conv_w, v_g_final):
    W = dict(w_mod=w_mod, b_mod=b_mod, g_norm=g_norm, w_ffn1_gu=w_ffn1_gu, w_ffn1_down=w_ffn1_down,
             w_ffn2_gu=w_ffn2_gu, w_ffn2_down=w_ffn2_down, w_in=w_in, w_out=w_out, attn_sinks=attn_sinks,
             lru_conv_w=lru_conv_w, lru_conv_b=lru_conv_b, lru_gate_a_w=lru_gate_a_w, lru_gate_a_b=lru_gate_a_b,
             lru_gate_x_w=lru_gate_x_w, lru_gate_x_b=lru_gate_x_b, lru_lambda=lru_lambda, sc_conv_w=sc_conv_w,
             g_final=g_final)
    M1 = dict(w_mod=m_w_mod, b_mod=m_b_mod, g_norm=m_g_norm, w_ffn1_gu=m_w_ffn1_gu, w_ffn1_down=m_w_ffn1_down,
              w_ffn2_gu=m_w_ffn2_gu, w_ffn2_down=m_w_ffn2_down, w_in=m_w_in, w_out=m_w_out,
              attn_sinks=m_attn_sinks, lru_conv_w=m_lru_conv_w, lru_conv_b=m_lru_conv_b,
              lru_gate_a_w=m_lru_gate_a_w, lru_gate_a_b=m_lru_gate_a_b, lru_gate_x_w=m_lru_gate_x_w,
              lru_gate_x_b=m_lru_gate_x_b, lru_lambda=m_lru_lambda, sc_conv_w=m_sc_conv_w, g_final=m_g_final)
    V1 = dict(w_mod=v_w_mod, b_mod=v_b_mod, g_norm=v_g_norm, w_ffn1_gu=v_w_ffn1_gu, w_ffn1_down=v_w_ffn1_down,
              w_ffn2_gu=v_w_ffn2_gu, w_ffn2_down=v_w_ffn2_down, w_in=v_w_in, w_out=v_w_out,
              attn_sinks=v_attn_sinks, lru_conv_w=v_lru_conv_w, lru_conv_b=v_lru_conv_b,
              lru_gate_a_w=v_lru_gate_a_w, lru_gate_a_b=v_lru_gate_a_b, lru_gate_x_w=v_lru_gate_x_w,
              lru_gate_x_b=v_lru_gate_x_b, lru_lambda=v_lru_lambda, sc_conv_w=v_sc_conv_w, g_final=v_g_final)
    names = ["w_mod", "b_mod", "g_norm", "w_ffn1_gu", "w_ffn1_down", "w_ffn2_gu", "w_ffn2_down", "w_in", "w_out",
             "attn_sinks", "lru_conv_w", "lru_conv_b", "lru_gate_a_w", "lru_gate_a_b", "lru_gate_x_w",
             "lru_gate_x_b", "lru_lambda", "sc_conv_w", "g_final"]

    xs = x[0]
    tgt = loss_target[0]
    S = xs.shape[0]
    chip = 2 * lax.axis_index("x") + lax.axis_index("y")
    batch = 2 * chip + lax.axis_index("c")
    L = DEPTH

    fwd_shapes = [(D_MODEL,), g_norm.shape, lru_conv_w.shape, sc_conv_w.shape]
    gathered, _ = _all_gather_small(_pack([c[0], g_norm, lru_conv_w, sc_conv_w]), "gather_small_fwd")
    gathered = gathered.reshape(N_DEV, -1)
    c_all = gathered[:, :D_MODEL]
    per_chip = [_unpack(gathered[2 * jj], fwd_shapes) for jj in range(N_CHIPS)]
    g_norm_full = jnp.concatenate([p[1] for p in per_chip], axis=-1)
    lru_conv_w_full = jnp.concatenate([p[2] for p in per_chip], axis=-1)
    sc_conv_w_full = jnp.concatenate([p[3] for p in per_chip], axis=-1)

    c_pad = jnp.concatenate([c_all, jnp.zeros_like(c_all)], axis=0)
    mod_part, c_act = _mod_matmul(c_pad, w_mod, "mod_matmul")
    mod_all, _ = _all_gather_small(mod_part.reshape(-1, LANES), "gather_mod")
    mod_all = mod_all.reshape(N_DEV, L, 16, -1)
    mod_rows = [lax.dynamic_index_in_dim(mod_all[2 * jj], batch, axis=1, keepdims=False) for jj in range(N_CHIPS)]
    mod = (jnp.concatenate(mod_rows, axis=-1) + b_mod).reshape(L, 9, D_MODEL)

    def nrm_rows(l, s):
        return _rows8(g_norm_full[l, s], mod[l, 3 * s], mod[l, 3 * s + 1], mod[l, 3 * s + 2])

    full = {}

    def gather_jobs(key):
        return [_GatherJob(W[n][l].astype(BF16), _AXIS[n]) for l, n in _GATHER_PLAN.get(key, ())]

    def landed(key, outs):
        full.update(zip(_GATHER_PLAN.get(key, ()), outs))

    landed("first", _comm_only(gather_jobs("first"), "gather_first"))

    def mixer_params(l):
        small = jnp.concatenate([lru_conv_w_full[l], lru_conv_b[l][None], lru_gate_a_b[l][None],
                                 lru_gate_x_b[l][None], lru_lambda[l][None], sc_conv_w_full[l],
                                 jnp.zeros((5, LRU_WIDTH), F32)], axis=0)
        return (attn_sinks[l], small, _block_diag(lru_gate_a_w[l]).astype(BF16),
                _block_diag(lru_gate_x_w[l]).astype(BF16))

    saved = []
    xcur = xs
    for l in range(L):
        n1, n2, n3 = nrm_rows(l, 0), nrm_rows(l, 1), nrm_rows(l, 2)

        def ffn(which, xin, nrm, head=None):
            key = (l, which)
            (xo, h, gu, y, *stats), ex = _ffn_fwd(xin, nrm, full[(l, f"w_{which}_gu")], full[(l, f"w_{which}_down")],
                                                  f"l{l}_{which}", gather_jobs(key), head)
            landed(key, ex)
            return (xo, *stats), (xin, h, gu, y)

        (x1,), s1 = ffn("ffn1", xcur, n1)
        (proj, h2), _ = _norm_matmul(x1, n2, full[(l, "w_in")], F32, f"l{l}_mix_in")
        mp = mixer_params(l)
        (ymix, hprev), ex = _mixer_fwd(proj, *mp, f"l{l}_mix_core", gather_jobs((l, "mix_core")))
        landed((l, "mix_core"), ex)
        (x2, ymo), _ = _proj_residual(ymix, full[(l, "w_out")], x1, n2, f"l{l}_mix_out")
        s2 = (x1, h2, proj, ymix, ymo, hprev, mp)
        (xcur, *stats), s3 = ffn("ffn2", x2, n3, (_rows8(g_final), tgt) if l == L - 1 else None)
        saved.append((n1, n2, n3, s1, s2, s3))

    dx, stats = xcur, stats[0]
    loss = lax.psum(stats[1, 0], ("x", "y", "c"))
    d_g_final = stats[0]

    recv, theirs = {}, {}
    waiting = []

    def carried(fn, *a, extra=()):
        items = waiting + list(extra)
        waiting.clear()
        outs, landed_now = fn(*a, jobs=[_SiblingJob(recv[(ll, n)]) if g is None else _ScatterJob(g, _AXIS[n])
                                        for ll, n, g in items])
        for (ll, n, g), arr in zip(items, landed_now):
            if g is None:
                theirs[(ll, n)] = arr
            else:
                recv[(ll, n)] = arr
                waiting.append((ll, n, None))
        return outs

    dmod, d_gnorm, d_small = [None] * L, [None] * L, [None] * L
    for l in reversed(range(L)):
        n1, n2, n3, s1, s2, s3 = saved[l]

        def plain(fn, *a):
            return fn(*a)[0]

        def ffn_bwd(which, dxo, sv, nrm, last):
            xin, h, gu, y = sv
            tag = f"l{l}_{which}"
            dgu, dgate, dw_down = (carried if which == "ffn2" else plain)(
                _proj_residual_bwd, dxo, gu, y, full[(l, f"w_{which}_down")], nrm, 0.5, True, tag + "_down_bwd")
            dw_gu = carried(_atb, h, dgu, BF16, 1024, 2816, tag + "_dw_gu", extra=[(l, f"w_{which}_down", dw_down)])
            mine = [(l, f"w_{which}_gu", dw_gu)]
            dxi, red = (carried if last else plain)(
                _nt_norm_bwd, dgu, full[(l, f"w_{which}_gu")], xin, nrm, dxo, tag + "_gu_bwd",
                **(dict(extra=mine) if last else {}))
            if not last:
                waiting.extend(mine)
            return dxi, (red[0], red[1], dgate[0]), red[2]

        dx, dm3, dg3 = ffn_bwd("ffn2", dx, s3, n3, False)
        x_in, h2, proj, ymix, ymo, hprev, mp = s2
        dymix, dgate, dw_out = plain(_proj_residual_bwd, dx, ymix, ymo, full[(l, "w_out")], n2, 1.0, False,
                                     f"l{l}_mix_out_bwd")
        dproj, dsm, dsink, dwa, dwx = carried(_mixer_bwd, proj, dymix, hprev, *mp, f"l{l}_mix_core_bwd",
                                              extra=[(l, "w_out", dw_out)])
        dw_in = plain(_atb, h2, dproj, BF16, 1024, 2048, f"l{l}_dw_in")
        dx, red = carried(_nt_norm_bwd, dproj, full[(l, "w_in")], x_in, n2, dx, f"l{l}_mix_in_bwd",
                          extra=[(l, "w_in", dw_in)])
        dm2, dg2 = (red[0], red[1], dgate[0]), red[2]
        dx, dm1, dg1 = ffn_bwd("ffn1", dx, s1, n1, l == 0)
        dmod[l] = jnp.stack(list(dm1) + list(dm2) + list(dm3))
        d_gnorm[l] = jnp.stack([dg1, dg2, dg3])
        d_small[l] = (dsink[:, 0], dsm[0:4], dsm[4], _diag_blocks(dwa), dsm[5], _diag_blocks(dwx), dsm[6],
                      dsm[7], dsm[8:11])
    grad_x = dx[None]

    def both(k):
        return jnp.stack([d_small[0][k], d_small[1][k]])
    small_names = ["g_norm", "attn_sinks", "lru_conv_w", "lru_conv_b", "lru_gate_a_w", "lru_gate_a_b",
                   "lru_gate_x_w", "lru_gate_x_b", "lru_lambda", "sc_conv_w", "g_final"]
    small_parts = [jnp.stack(d_gnorm)] + [both(k) for k in range(9)] + [d_g_final]
    dmod_flat = jnp.stack(dmod).reshape(-1)
    bwd_gathered, bwd_sum = _all_gather_small(_pack([dmod_flat] + small_parts), "gather_small_bwd")
    n_mod = dmod_flat.shape[0]
    dmod_all = bwd_gathered.reshape(N_DEV, -1)[:, :n_mod].reshape(N_DEV, L, 9 * D_MODEL)
    bwd_sum = bwd_sum.reshape(-1)
    G = {"b_mod": bwd_sum[:n_mod].reshape(L, 9 * D_MODEL)}
    G.update(zip(small_names, _unpack(bwd_sum[n_mod:], [p.shape for p in small_parts])))
    for n in ("g_norm", "lru_conv_w", "sc_conv_w"):
        wdt = W[n].shape[-1]
        G[n] = lax.dynamic_slice_in_dim(G[n], chip * wdt, wdt, axis=G[n].ndim - 1)

    ncol = w_mod.shape[-1]
    dmod_cols = lax.dynamic_slice_in_dim(dmod_all, chip * ncol, ncol, axis=2)
    zeros8 = jnp.zeros((N_DEV, ncol), F32)
    g_w_mod = jnp.stack([carried(_atb, c_act, jnp.concatenate([dmod_cols[:, l], zeros8], axis=0).astype(BF16), F32,
                                 D_MODEL, 768, f"l{l}_dw_mod") for l in range(L)])

    out_g, out_d, out_m, out_v = {}, {}, {}, {}
    res, _ = _adamw(w_mod.reshape(-1, ncol), g_w_mod.reshape(-1, ncol), m_w_mod.reshape(-1, ncol),
                    v_w_mod.reshape(-1, ncol), "adamw_w_mod")
    out_g["w_mod"], out_d["w_mod"], out_m["w_mod"], out_v["w_mod"] = [r.reshape(w_mod.shape) for r in res]
    for n, _ in _BIG:
        shp = W[n].shape
        flat = (shp[0] * shp[1], shp[2])
        res = _adamw_partials(W[n].reshape(flat), [(recv[(l, n)], theirs[(l, n)]) for l in range(L)],
                              M1[n].reshape(flat), V1[n].reshape(flat), f"adamw_{n}")
        out_g[n], out_d[n], out_m[n], out_v[n] = [r.reshape(shp) for r in res]
    rest = ["b_mod"] + small_names
    shapes = [W[n].shape for n in rest]
    res, _ = _adamw(_pack([W[n] for n in rest]), _pack([G[n] for n in rest]), _pack([M1[n] for n in rest]),
                    _pack([V1[n] for n in rest]), "adamw_small")
    for dst, r in zip((out_g, out_d, out_m, out_v), res):
        dst.update(zip(rest, _unpack(r.reshape(-1), shapes)))

    return (loss, grad_x, *[out_g[n] for n in names], *[out_d[n] for n in names],
            *[out_m[n] for n in names], *[out_v[n] for n in names])
```

```python
import math

import jax
import jax.numpy as jnp
from jax import lax
from jax.experimental import pallas as pl
from jax.experimental.pallas import tpu as pltpu

F32 = jnp.float32
BF16 = jnp.bfloat16

D_MODEL = 1024
DEPTH = 2
HEAD_DIM = 64
N_Q_HEADS = 8
ATTN_WIDTH = 512
KV_WIDTH = 128
LRU_WIDTH = 256
CONV_WIDTH = 256
IN_PROJ_WIDTH = 2048
BLOCK = 128
D_FF = 2816
EPS = 1e-6
NEG_INF = -1e30
LRU_C = 8.0
N_CHIPS = 4
N_DEV = 8

C_Q, C_KV, C_LX, C_LG, C_SB, C_SC, C_SX = 0, 512, 768, 1024, 1280, 1536, 1792

ADAM_LR = 0.001
ADAM_B1 = 0.9
ADAM_B2 = 0.999
ADAM_EPS = 1e-08
ADAM_WD = 0.01
ADAM_STEP = 10

LANES = 128
SUBLANES = 8
VMEM_LIMIT = 56 * 1024 * 1024
MIX_TILE = 256

MESH = pl.DeviceIdType.MESH


def _cp(*sem):
    return pltpu.CompilerParams(dimension_semantics=sem, vmem_limit_bytes=VMEM_LIMIT)


def _tile(n, pref):
    t = min(n, pref)
    while n % t:
        t //= 2
    return t


MXU_DIM = 256


def _col_chunk(n):
    return max(c for c in range(MXU_DIM, 2816 + 1, MXU_DIM) if n % c == 0)


def _resident(shape):
    return pl.BlockSpec(shape, lambda *_: (0, 0), pipeline_mode=pl.Buffered(1))


def _sigmoid(v):
    return 1.0 / (1.0 + jnp.exp(-v))


def _expm1(v):
    series = v * (1.0 + v * (0.5 + v * (1.0 / 6.0)))
    return jnp.where(v > -0.01, series, jnp.exp(v) - 1.0)


def _softplus_neg(lam):
    e = jnp.exp(-jnp.abs(lam))
    log1p = jnp.where(e < 1e-2, e * (1.0 - e * (0.5 - e * (1.0 / 3.0))), jnp.log(1.0 + e))
    return jnp.maximum(-lam, 0.0) + log1p


_GELU_K = math.sqrt(2.0 / math.pi)
_GELU_C = 0.044715


def _gelu(v):
    t = jnp.tanh(_GELU_K * (v + _GELU_C * v * v * v))
    return 0.5 * v * (1.0 + t), t


def _gelu_grad(v, t):
    return 0.5 * (1.0 + t) + 0.5 * v * (1.0 - t * t) * _GELU_K * (1.0 + 3.0 * _GELU_C * v * v)


def _dot(a, b):
    return jnp.dot(a, b, preferred_element_type=F32)


def _dot_nt(a, b):
    return lax.dot_general(a, b, (((1,), (1,)), ((), ())), preferred_element_type=F32)


def _dot_tn(a, b):
    return lax.dot_general(a, b, (((0,), (0,)), ((), ())), preferred_element_type=F32)


def _window(ref, axis, j, width):
    start = pl.multiple_of(j * width, LANES if axis == 1 else 16)
    if axis == 1:
        return ref.at[:, pl.ds(start, width)]
    return ref.at[pl.ds(start, width), :]


def _chip_peers():
    x, y, c = lax.axis_index("x"), lax.axis_index("y"), lax.axis_index("c")
    return x, y, c, [(1 - x, y), (x, 1 - y), (1 - x, 1 - y)]


class _GatherJob:
    def __init__(self, shard, axis):
        self.src, self.axis, self.width, self.half = shard, axis, shard.shape[axis], shard.shape[0] // 2
        full = tuple(d * N_CHIPS if k == axis else d for k, d in enumerate(shard.shape))
        self.out_shape = jax.ShapeDtypeStruct(full, shard.dtype)

    def _piece(self, ref, j, hf):
        if self.axis == 1:
            return ref.at[pl.ds(pl.multiple_of(hf * self.half, 16), self.half),
                          pl.ds(pl.multiple_of(j * self.width, LANES), self.width)]
        return ref.at[pl.ds(pl.multiple_of(j * self.width + hf * self.half, 16), self.half), :]

    def _copies(self, src, dst, send, recv, loc, t):
        x, y, c, chips = _chip_peers()
        j = 2 * x + y
        owners = [2 * px + py for px, py in chips]
        local = pltpu.make_async_copy(src, _window(dst, self.axis, j, self.width), loc.at[t])
        mine = src.at[pl.ds(pl.multiple_of(c * self.half, 16), self.half), :]

        def ici(k, owner):
            return pltpu.make_async_remote_copy(
                src_ref=mine, dst_ref=self._piece(dst, owner, c), send_sem=send.at[JOB_SEMS * t + k],
                recv_sem=recv.at[JOB_SEMS * t + k], device_id=(*chips[k], c), device_id_type=MESH)

        def relay(k, hf):
            piece = self._piece(dst, owners[k], hf)
            return pltpu.make_async_remote_copy(
                src_ref=piece, dst_ref=piece, send_sem=send.at[JOB_SEMS * t + 4 + k],
                recv_sem=recv.at[JOB_SEMS * t + 4 + k],
                device_id=(x, y, 1 - c), device_id_type=MESH)

        return (local, [ici(k, j) for k in range(3)], [ici(k, owners[k]) for k in range(3)],
                [relay(k, c) for k in range(3)], [relay(k, 1 - c) for k in range(3)])

    def start(self, *a):
        local, ici_out, _, _, _ = self._copies(*a)
        local.start()
        for cp in ici_out:
            cp.start()

    def relay(self, *a):
        _, _, ici_in, relay_out, _ = self._copies(*a)
        for arrived, onward in zip(ici_in, relay_out):
            arrived.wait_recv()
            onward.start()

    def finish(self, *a):
        local, ici_out, _, relay_out, relay_in = self._copies(*a)
        for cp in relay_in:
            cp.wait_recv()
        for cp in ici_out + relay_out:
            cp.wait_send()
        local.wait()


class _ScatterJob:
    def __init__(self, full, axis):
        self.src, self.axis, self.width = full, axis, full.shape[axis] // N_CHIPS
        shard = tuple(self.width if k == axis else d for k, d in enumerate(full.shape))
        self.out_shape = jax.ShapeDtypeStruct((N_CHIPS,) + shard, full.dtype)

    def _copies(self, src, dst, send, recv, loc, t):
        x, y, c, chips = _chip_peers()
        local = pltpu.make_async_copy(_window(src, self.axis, 2 * x + y, self.width), dst.at[3], loc.at[t])
        sends = [pltpu.make_async_remote_copy(
            src_ref=_window(src, self.axis, 2 * px + py, self.width), dst_ref=dst.at[k],
            send_sem=send.at[JOB_SEMS * t + k], recv_sem=recv.at[JOB_SEMS * t + k], device_id=(px, py, c),
            device_id_type=MESH) for k, (px, py) in enumerate(chips)]
        return local, sends

    def start(self, *a):
        local, sends = self._copies(*a)
        local.start()
        for cp in sends:
            cp.start()

    def relay(self, *a):
        pass

    def finish(self, *a):
        local, sends = self._copies(*a)
        for cp in sends:
            cp.wait_recv()
        for cp in sends:
            cp.wait_send()
        local.wait()


class _SiblingJob:
    def __init__(self, arr):
        self.src, self.out_shape = arr, jax.ShapeDtypeStruct(arr.shape, arr.dtype)

    def _copy(self, src, dst, send, recv, loc, t):
        x, y, c = lax.axis_index("x"), lax.axis_index("y"), lax.axis_index("c")
        return pltpu.make_async_remote_copy(
            src_ref=src, dst_ref=dst, send_sem=send.at[JOB_SEMS * t], recv_sem=recv.at[JOB_SEMS * t],
            device_id=(x, y, 1 - c), device_id_type=MESH)

    def start(self, *a):
        self._copy(*a).start()

    def relay(self, *a):
        pass

    def finish(self, *a):
        self._copy(*a).wait()


JOB_SEMS = 8


def _run_jobs(phase, jobs, srcs, dsts, sems):
    for t, job in enumerate(jobs):
        getattr(job, phase)(srcs[t], dsts[t], *sems, t)


def _job_scratch(n):
    return [pltpu.SemaphoreType.DMA((JOB_SEMS * n,)), pltpu.SemaphoreType.DMA((JOB_SEMS * n,)),
            pltpu.SemaphoreType.DMA((n,))]


def _pcall(body, *, name, grid, in_specs, out_specs, out_shape, sem, args, scratch_shapes=(), jobs=()):
    in_specs, out_specs, out_shape = list(in_specs), list(out_specs), list(out_shape)
    scratch_shapes = list(scratch_shapes)
    if not jobs:
        res = pl.pallas_call(body, name=name, grid=grid, in_specs=in_specs, out_specs=out_specs, out_shape=out_shape,
                             scratch_shapes=scratch_shapes, compiler_params=_cp(*sem))(*args)
        return list(res), []
    n_in, n_out, n_scr, nj = len(args), len(out_shape), len(scratch_shapes), len(jobs)
    n_steps = math.prod(grid)
    relay_step = (3 * n_steps) // 4
    relay_early = 0 < relay_step < n_steps - 1

    def wrapped(*refs):
        ins, refs = refs[:n_in], refs[n_in:]
        jin, refs = refs[:nj], refs[nj:]
        outs, refs = refs[:n_out], refs[n_out:]
        jout, refs = refs[:nj], refs[nj:]
        scr, sems = refs[:n_scr], refs[n_scr:]
        step = pl.program_id(0)
        for d in range(1, len(grid)):
            step = step * grid[d] + pl.program_id(d)

        @pl.when(step == 0)
        def _():
            _run_jobs("start", jobs, jin, jout, sems)

        if relay_early:
            @pl.when(step == relay_step)
            def _():
                _run_jobs("relay", jobs, jin, jout, sems)
        body(*ins, *outs, *scr)

        @pl.when(step == n_steps - 1)
        def _():
            if not relay_early:
                _run_jobs("relay", jobs, jin, jout, sems)
            _run_jobs("finish", jobs, jin, jout, sems)

    hbm = pl.BlockSpec(memory_space=pltpu.HBM)
    res = pl.pallas_call(
        wrapped, name=name, grid=grid, in_specs=in_specs + [hbm] * nj, out_specs=out_specs + [hbm] * nj,
        out_shape=out_shape + [job.out_shape for job in jobs], scratch_shapes=scratch_shapes + _job_scratch(nj),
        compiler_params=_cp(*sem))(*args, *[job.src for job in jobs])
    return list(res[:n_out]), list(res[n_out:])


def _norm_matmul(x, nrm, w, out_dtype, name, jobs=()):
    S, Dm = x.shape
    N = w.shape[1]
    tm = _tile(S, 512)
    cw = _col_chunk(N)

    def body(x_ref, nrm_ref, w_ref, o_ref, h_ref):
        xv = x_ref[...]
        rstd = lax.rsqrt(jnp.mean(xv * xv, axis=-1, keepdims=True) + EPS)
        hn = (xv * rstd) * nrm_ref[0:1, :]
        hb = (hn * (1.0 + nrm_ref[2:3, :]) + nrm_ref[1:2, :]).astype(BF16)
        h_ref[...] = hb
        for n in range(N // cw):
            o_ref[:, n * cw:(n + 1) * cw] = _dot(hb, w_ref[:, n * cw:(n + 1) * cw]).astype(o_ref.dtype)

    return _pcall(
        body, name=name, grid=(S // tm,),
        in_specs=[pl.BlockSpec((tm, Dm), lambda i: (i, 0)),
                  pl.BlockSpec((8, Dm), lambda i: (0, 0)),
                  _resident((Dm, N))],
        out_specs=[pl.BlockSpec((tm, N), lambda i: (i, 0)),
                   pl.BlockSpec((tm, Dm), lambda i: (i, 0))],
        out_shape=[jax.ShapeDtypeStruct((S, N), out_dtype), jax.ShapeDtypeStruct((S, Dm), BF16)],
        sem=("arbitrary",), args=(x, nrm, w), jobs=jobs)


def _hidden_chunks(k):
    return [(c0, min(6 * MXU_DIM, k - c0)) for c0 in range(0, k, 6 * MXU_DIM)]


def _loss_head(xv, gain, tgt, st_ref):
    dm = xv.shape[-1]
    rstd = lax.rsqrt(jnp.mean(xv * xv, axis=-1, keepdims=True) + EPS)
    xn = xv * rstd
    err = xn * gain - tgt
    st_ref[1:2, :] += jnp.full((1, dm), 0.5 / dm, F32) * jnp.sum(err * err)
    dy = err * (1.0 / dm)
    st_ref[0:1, :] += jnp.sum(dy * xn, axis=0, keepdims=True)
    dxn = dy * gain
    return rstd * (dxn - xn * jnp.mean(dxn * xn, axis=-1, keepdims=True))


def _ffn_fwd(x, nrm, w_gu, w_down, name, jobs=(), head=None):
    S, Dm = x.shape
    K = w_down.shape[0]
    tm = _tile(S, 256)

    def body(x_ref, nrm_ref, wgu_ref, wdn_ref, *rest):
        if head is None:
            o_ref, h_ref, gu_ref, y_ref = rest
        else:
            gf_ref, t_ref, o_ref, h_ref, gu_ref, y_ref, st_ref = rest

            @pl.when(pl.program_id(0) == 0)
            def _():
                st_ref[...] = jnp.zeros_like(st_ref)
        xv = x_ref[...]
        rstd = lax.rsqrt(jnp.mean(xv * xv, axis=-1, keepdims=True) + EPS)
        hn = (xv * rstd) * nrm_ref[0:1, :]
        hb = (hn * (1.0 + nrm_ref[2:3, :]) + nrm_ref[1:2, :]).astype(BF16)
        h_ref[...] = hb
        y = jnp.zeros((tm, Dm), F32)
        for c0, cs in _hidden_chunks(K):
            g = _dot(hb, wgu_ref[:, c0:c0 + cs])
            u = _dot(hb, wgu_ref[:, K + c0:K + c0 + cs])
            gu_ref[:, c0:c0 + cs] = g.astype(BF16)
            gu_ref[:, K + c0:K + c0 + cs] = u.astype(BF16)
            y = y + _dot((g * _sigmoid(g) * u).astype(BF16), wdn_ref[c0:c0 + cs, :])
        xo = xv + (0.5 * nrm_ref[3:4, :]) * y
        o_ref[...] = xo if head is None else _loss_head(xo, gf_ref[0:1, :], t_ref[...], st_ref)
        y_ref[...] = y.astype(BF16)

    row = lambda i: (i, 0)
    fix = lambda i: (0, 0)
    in_specs = [pl.BlockSpec((tm, Dm), row), pl.BlockSpec((8, Dm), fix), _resident((Dm, 2 * K)), _resident((K, Dm))]
    out_specs = [pl.BlockSpec((tm, Dm), row), pl.BlockSpec((tm, Dm), row), pl.BlockSpec((tm, 2 * K), row),
                 pl.BlockSpec((tm, Dm), row)]
    out_shape = [jax.ShapeDtypeStruct((S, Dm), F32), jax.ShapeDtypeStruct((S, Dm), BF16),
                 jax.ShapeDtypeStruct((S, 2 * K), BF16), jax.ShapeDtypeStruct((S, Dm), BF16)]
    args = (x, nrm, w_gu, w_down)
    if head is not None:
        in_specs += [pl.BlockSpec((8, Dm), fix), pl.BlockSpec((tm, Dm), row)]
        out_specs.append(pl.BlockSpec((8, Dm), fix))
        out_shape.append(jax.ShapeDtypeStruct((8, Dm), F32))
        args += tuple(head)
    return _pcall(body, name=name, grid=(S // tm,), in_specs=in_specs, out_specs=out_specs, out_shape=out_shape,
                  sem=("arbitrary",), args=args, jobs=jobs)


def _proj_residual(a, w, x, nrm, name, jobs=()):
    S, K = a.shape
    Dm = w.shape[1]
    tm = _tile(S, 256)

    def body(a_ref, w_ref, x_ref, nrm_ref, o_ref, y_ref):
        y = _dot(a_ref[...], w_ref[...])
        o_ref[...] = x_ref[...] + nrm_ref[3:4, :] * y
        y_ref[...] = y.astype(BF16)

    return _pcall(
        body, name=name, grid=(S // tm,),
        in_specs=[pl.BlockSpec((tm, K), lambda i: (i, 0)),
                  _resident((K, Dm)),
                  pl.BlockSpec((tm, Dm), lambda i: (i, 0)),
                  pl.BlockSpec((8, Dm), lambda i: (0, 0))],
        out_specs=[pl.BlockSpec((tm, Dm), lambda i: (i, 0)),
                   pl.BlockSpec((tm, Dm), lambda i: (i, 0))],
        out_shape=[jax.ShapeDtypeStruct((S, Dm), F32), jax.ShapeDtypeStruct((S, Dm), BF16)],
        sem=("arbitrary",), args=(a, w, x, nrm), jobs=jobs)


def _proj_residual_bwd(dxo, a, y, w, nrm, coef, swiglu, name, jobs=()):
    S, Dm = dxo.shape
    K = w.shape[0]
    Ka = a.shape[1]
    tm = _tile(S, 256)
    n_steps = S // tm
    chunks = _hidden_chunks(K)
    paired = n_steps % 2 == 0

    def body(dxo_ref, y_ref, w_ref, nrm_ref, a_ref, da_ref, dgate_ref, dw_ref, acc, act_park, dy_park):
        @pl.when(pl.program_id(0) == 0)
        def _():
            dgate_ref[...] = jnp.zeros_like(dgate_ref)
            acc[...] = jnp.zeros_like(acc)

        odd = pl.program_id(0) % 2 == 1
        dxo_v = dxo_ref[...]
        dyb = ((coef * nrm_ref[3:4, :]) * dxo_v).astype(BF16)
        dgate_ref[0:1, :] += jnp.sum(coef * y_ref[...].astype(F32) * dxo_v, axis=0, keepdims=True)
        for c0, cs in chunks:
            dact = _dot_nt(dyb, w_ref[c0:c0 + cs, :])
            if swiglu:
                g = a_ref[:, c0:c0 + cs].astype(F32)
                u = a_ref[:, K + c0:K + c0 + cs].astype(F32)
                s = _sigmoid(g)
                si = g * s
                da_ref[:, c0:c0 + cs] = (dact * u * (s * (1.0 + g * (1.0 - s)))).astype(BF16)
                da_ref[:, K + c0:K + c0 + cs] = (dact * si).astype(BF16)
                act = (si * u).astype(BF16)
            else:
                da_ref[:, c0:c0 + cs] = dact
                act = a_ref[:, c0:c0 + cs]
            if paired:
                @pl.when(jnp.logical_not(odd))
                def _(act=act, c0=c0, cs=cs):
                    act_park[:, c0:c0 + cs] = act

                @pl.when(odd)
                def _(act=act, c0=c0, cs=cs):
                    both = jnp.concatenate([act_park[:, c0:c0 + cs], act], axis=0)
                    acc[c0:c0 + cs, :] += _dot_tn(both, jnp.concatenate([dy_park[...], dyb], axis=0))
            else:
                acc[c0:c0 + cs, :] += _dot_tn(act, dyb)
        if paired:
            @pl.when(jnp.logical_not(odd))
            def _():
                dy_park[...] = dyb

        @pl.when(pl.program_id(0) == n_steps - 1)
        def _():
            dw_ref[...] = acc[...].astype(BF16)

    row = lambda i: (i, 0)
    fix = lambda i: (0, 0)
    return _pcall(
        body, name=name, grid=(n_steps,),
        in_specs=[pl.BlockSpec((tm, Dm), row), pl.BlockSpec((tm, Dm), row), _resident((K, Dm)),
                  pl.BlockSpec((8, Dm), fix), pl.BlockSpec((tm, Ka), row)],
        out_specs=[pl.BlockSpec((tm, Ka), row), pl.BlockSpec((8, Dm), fix), _resident((K, Dm))],
        out_shape=[jax.ShapeDtypeStruct((S, Ka), BF16 if swiglu else F32), jax.ShapeDtypeStruct((8, Dm), F32),
                   jax.ShapeDtypeStruct((K, Dm), BF16)],
        scratch_shapes=[pltpu.VMEM((K, Dm), F32), pltpu.VMEM((tm, K), BF16), pltpu.VMEM((tm, Dm), BF16)],
        sem=("arbitrary",), args=(dxo, y, w, nrm, a), jobs=jobs)


def _atb(a, b, out_dtype, bm, bn, name, jobs=()):
    S, M = a.shape
    N = b.shape[1]
    bk = _tile(S, 1024)
    nk = S // bk

    def body(a_ref, b_ref, o_ref, acc):
        k = pl.program_id(2)

        @pl.when(k == 0)
        def _():
            acc[...] = jnp.zeros_like(acc)
        acc[...] += _dot_tn(a_ref[...], b_ref[...])

        @pl.when(k == nk - 1)
        def _():
            o_ref[...] = acc[...].astype(o_ref.dtype)

    (out,), extra = _pcall(
        body, name=name, grid=(M // bm, N // bn, nk),
        in_specs=[pl.BlockSpec((bk, bm), lambda m, n, k: (k, m)),
                  pl.BlockSpec((bk, bn), lambda m, n, k: (k, n))],
        out_specs=[pl.BlockSpec((bm, bn), lambda m, n, k: (m, n))],
        out_shape=[jax.ShapeDtypeStruct((M, N), out_dtype)],
        scratch_shapes=[pltpu.VMEM((bm, bn), F32)],
        sem=("arbitrary", "arbitrary", "arbitrary"), args=(a, b), jobs=jobs)
    return out, extra


def _nt_norm_bwd(dout, w, x, nrm, dxo, name, jobs=()):
    S, N = dout.shape
    Dm = w.shape[0]
    tm = _tile(S, 512)

    def body(do_ref, w_ref, x_ref, nrm_ref, dxo_ref, dx_ref, red_ref):
        @pl.when(pl.program_id(0) == 0)
        def _():
            red_ref[...] = jnp.zeros_like(red_ref)
        dh = _dot_nt(do_ref[...], w_ref[...])
        xv = x_ref[...]
        rstd = lax.rsqrt(jnp.mean(xv * xv, axis=-1, keepdims=True) + EPS)
        xn = xv * rstd
        gain = nrm_ref[0:1, :]
        hn = xn * gain
        dhn = dh * (1.0 + nrm_ref[2:3, :])
        red_ref[0:1, :] += jnp.sum(dh, axis=0, keepdims=True)
        red_ref[1:2, :] += jnp.sum(dh * hn, axis=0, keepdims=True)
        red_ref[2:3, :] += jnp.sum(dhn * xn, axis=0, keepdims=True)
        dxn = dhn * gain
        dx = rstd * (dxn - xn * jnp.mean(dxn * xn, axis=-1, keepdims=True))
        dx_ref[...] = dxo_ref[...] + dx

    return _pcall(
        body, name=name, grid=(S // tm,),
        in_specs=[pl.BlockSpec((tm, N), lambda i: (i, 0)),
                  _resident((Dm, N)),
                  pl.BlockSpec((tm, Dm), lambda i: (i, 0)),
                  pl.BlockSpec((8, Dm), lambda i: (0, 0)),
                  pl.BlockSpec((tm, Dm), lambda i: (i, 0))],
        out_specs=[pl.BlockSpec((tm, Dm), lambda i: (i, 0)),
                   pl.BlockSpec((8, Dm), lambda i: (0, 0))],
        out_shape=[jax.ShapeDtypeStruct((S, Dm), F32), jax.ShapeDtypeStruct((8, Dm), F32)],
        sem=("arbitrary",), args=(dout, w, x, nrm, dxo), jobs=jobs)


def _alibi_slope(h):
    return float(2.0 ** (-8.0 * (h + 1) / N_Q_HEADS))


def _head_planes(pair_cols):
    lane = lax.broadcasted_iota(jnp.int32, pair_cols.shape, 1)
    low = lane < HEAD_DIM
    h0_lo = jnp.where(low, pair_cols, 0.0)
    h1_hi = jnp.where(low, 0.0, pair_cols)
    h0_hi = pltpu.roll(h0_lo, HEAD_DIM, 1)
    h1_lo = pltpu.roll(h1_hi, HEAD_DIM, 1)
    return ((h0_lo.astype(BF16), h0_hi.astype(BF16)), (h1_lo.astype(BF16), h1_hi.astype(BF16)))


def _band_geometry(first_block):
    qi = lax.broadcasted_iota(jnp.int32, (BLOCK, BLOCK), 0)
    kj = lax.broadcasted_iota(jnp.int32, (BLOCK, BLOCK), 1)
    own = kj <= qi
    dist = jnp.where(own, qi - kj, qi + BLOCK - kj).astype(F32)
    valid = kj <= qi + BLOCK * (1 - first_block)
    return own, dist, valid


def _fold(band, own):
    return jnp.where(own, band[:, BLOCK:], band[:, :BLOCK])


def _unfold(v, own):
    return jnp.concatenate([jnp.where(own, 0.0, v), jnp.where(own, v, 0.0)], axis=1)


def _softmax_band(s, h, geometry, sink):
    own, dist, valid = geometry
    s = jnp.where(valid, s - _alibi_slope(h) * dist, NEG_INF)
    m = jnp.maximum(jnp.max(s, axis=-1, keepdims=True), sink)
    p = jnp.exp(s - m)
    e_sink = jnp.exp(sink - m)
    inv = 1.0 / (jnp.sum(p, axis=-1, keepdims=True) + e_sink)
    return p * inv, e_sink * inv


def _past(cur, prev, s, row):
    return jnp.where(row < s, pltpu.roll(prev, s, 0), pltpu.roll(cur, s, 0))


def _future(cur, nxt, s, row):
    T = cur.shape[0]
    return jnp.where(row >= T - s, pltpu.roll(nxt, T - s, 0), pltpu.roll(cur, T - s, 0))


def _edge_row(v, last):
    T = v.shape[0]
    r8 = lax.broadcasted_iota(jnp.int32, (SUBLANES, v.shape[1]), 0)
    blk = v[T - SUBLANES:, :] if last else v[:SUBLANES, :]
    return jnp.sum(jnp.where(r8 == (SUBLANES - 1 if last else 0), blk, 0.0), axis=0, keepdims=True)


def _lru_gates(lx, lx_prev, small_ref, wa_ref, wx_ref, row, t0):
    xc = (small_ref[4:5, :] + small_ref[3:4, :] * lx + small_ref[2:3, :] * _past(lx, lx_prev, 1, row)
          + small_ref[1:2, :] * _past(lx, lx_prev, 2, row) + small_ref[0:1, :] * _past(lx, lx_prev, 3, row))
    xcb = xc.astype(BF16)
    r = _sigmoid(_dot(xcb, wa_ref[...]) + small_ref[5:6, :])
    ig = _sigmoid(_dot(xcb, wx_ref[...]) + small_ref[6:7, :])
    sp = _softplus_neg(small_ref[7:8, :])
    la = (-LRU_C) * r * sp
    a = jnp.exp(la)
    first = (row + t0) == 0
    mult = jnp.where(first, 1.0, jnp.sqrt(-_expm1(2.0 * la)))
    return xc, xcb, r, ig, sp, a, mult, first


def _mixer_fwd(proj, sinks, small, wa, wx, name, jobs=()):
    S = proj.shape[0]
    T = MIX_TILE
    nT = S // T
    nb = T // BLOCK

    def body(proj_ref, sink_ref, small_ref, wa_ref, wx_ref, y_ref, hp_ref, kvp, lxp, zp, hcar):
        i = pl.program_id(0)

        @pl.when(i == 0)
        def _():
            kvp[...] = jnp.zeros_like(kvp)
            lxp[...] = jnp.zeros_like(lxp)
            zp[...] = jnp.zeros_like(zp)
            hcar[...] = jnp.zeros_like(hcar)

        row = lax.broadcasted_iota(jnp.int32, (T, LRU_WIDTH), 0)

        kv = proj_ref[:, C_KV:C_KV + 2 * KV_WIDTH]
        ext = jnp.concatenate([kvp[...], kv], axis=0)
        kx = _head_planes(ext[:, :KV_WIDTH])
        vx = _head_planes(ext[:, KV_WIDTH:])
        first_tile = jnp.where(i == 0, 1, 0)
        units = [(b, pair, e) for b in range(nb) for pair in range(N_Q_HEADS // 2) for e in range(2)]
        geometry = [_band_geometry(first_tile if b == 0 else 0) for b in range(nb)]
        keys = [slice(b * BLOCK, (b + 2) * BLOCK) for b in range(nb)]
        qp = {(b, pair): (proj_ref[b * BLOCK:(b + 1) * BLOCK, pair * LANES:(pair + 1) * LANES] * 0.125).astype(BF16)
              for b in range(nb) for pair in range(N_Q_HEADS // 2)}
        scores = [_fold(_dot_nt(qp[(b, pair)], kx[pair // 2][e][keys[b]]), geometry[b][0]) for b, pair, e in units]
        probs = [_unfold(_softmax_band(s, 2 * pair + e, geometry[b], sink_ref[2 * pair + e])[0],
                         geometry[b][0]).astype(BF16) for s, (b, pair, e) in zip(scores, units)]
        outs = [_dot(p, vx[pair // 2][e][keys[b]]) for p, (b, pair, e) in zip(probs, units)]
        for u in range(0, len(units), 2):
            b, pair, _ = units[u]
            y_ref[b * BLOCK:(b + 1) * BLOCK, pair * LANES:(pair + 1) * LANES] = (outs[u] + outs[u + 1]).astype(BF16)
        kvp[...] = kv[T - BLOCK:, :]

        lx = proj_ref[:, C_LX:C_LX + LRU_WIDTH]
        xc, _, _, ig, _, a, mult, _ = _lru_gates(lx, lxp[...], small_ref, wa_ref, wx_ref, row, i * T)
        lxp[...] = lx
        aa = a
        bb = mult * (ig * xc)
        s = 1
        while s < T:
            a_sh = jnp.where(row >= s, pltpu.roll(aa, s, 0), 1.0)
            b_sh = jnp.where(row >= s, pltpu.roll(bb, s, 0), 0.0)
            bb = aa * b_sh + bb
            aa = aa * a_sh
            s *= 2
        hc = hcar[0:1, :]
        hh = bb + aa * hc
        hp_ref[...] = jnp.where(row < 1, hc, pltpu.roll(hh, 1, 0))
        hcar[...] = jnp.broadcast_to(_edge_row(hh, True), hcar.shape)
        gl, _ = _gelu(proj_ref[:, C_LG:C_LG + LRU_WIDTH])
        y_ref[:, ATTN_WIDTH:ATTN_WIDTH + LRU_WIDTH] = (gl * hh).astype(BF16)

        z = proj_ref[:, C_SC:C_SC + CONV_WIDTH] * proj_ref[:, C_SX:C_SX + CONV_WIDTH]
        c3 = (small_ref[10:11, :] * z + small_ref[9:10, :] * _past(z, zp[...], 1, row)
              + small_ref[8:9, :] * _past(z, zp[...], 2, row))
        zp[...] = z
        y_ref[:, ATTN_WIDTH + LRU_WIDTH:] = (proj_ref[:, C_SB:C_SB + CONV_WIDTH] * c3).astype(BF16)

    fix = lambda i: (0, 0)
    return _pcall(
        body, name=name, grid=(nT,),
        in_specs=[pl.BlockSpec((T, IN_PROJ_WIDTH), lambda i: (i, 0)),
                  pl.BlockSpec(memory_space=pltpu.SMEM),
                  pl.BlockSpec((16, LRU_WIDTH), fix),
                  pl.BlockSpec((LRU_WIDTH, LRU_WIDTH), fix),
                  pl.BlockSpec((LRU_WIDTH, LRU_WIDTH), fix)],
        out_specs=[pl.BlockSpec((T, D_MODEL), lambda i: (i, 0)),
                   pl.BlockSpec((T, LRU_WIDTH), lambda i: (i, 0))],
        out_shape=[jax.ShapeDtypeStruct((S, D_MODEL), BF16), jax.ShapeDtypeStruct((S, LRU_WIDTH), F32)],
        scratch_shapes=[pltpu.VMEM((BLOCK, 2 * KV_WIDTH), F32), pltpu.VMEM((T, LRU_WIDTH), F32),
                        pltpu.VMEM((T, CONV_WIDTH), F32), pltpu.VMEM((SUBLANES, LRU_WIDTH), F32)],
        sem=("arbitrary",), args=(proj, sinks, small, wa, wx), jobs=jobs)


def _mixer_bwd(proj, dymix, hprev, sinks, small, wa, wx, name, jobs=()):
    S = proj.shape[0]
    T = MIX_TILE
    nT = S // T
    nb = T // BLOCK
    bpt = T // BLOCK

    def body(proj_ref, kvprev_ref, lxprev_ref, scprev_ref, sxprev_ref, dy_ref, hp_ref, sink_ref, small_ref,
             wa_ref, wx_ref, dp_ref, dsm_ref, dsink_ref, dwa_ref, dwx_ref,
             dk_s, dv_s, dkv_c, dxc_n, dc3_n, p_c):
        i = pl.program_id(0)
        ti = nT - 1 - i
        has_prev = jnp.where(ti == 0, 0.0, 1.0)

        @pl.when(i == 0)
        def _():
            for r in (dkv_c, dxc_n, dc3_n, p_c, dsm_ref, dsink_ref, dwa_ref, dwx_ref):
                r[...] = jnp.zeros_like(r)

        row = lax.broadcasted_iota(jnp.int32, (T, LRU_WIDTH), 0)

        kv = proj_ref[:, C_KV:C_KV + 2 * KV_WIDTH]
        ext = jnp.concatenate([kvprev_ref[...] * has_prev, kv], axis=0)
        kx = _head_planes(ext[:, :KV_WIDTH])
        vx = _head_planes(ext[:, KV_WIDTH:])
        dk_s[...] = jnp.zeros_like(dk_s)
        dv_s[...] = jnp.zeros_like(dv_s)
        dk_s[:, T:] = dkv_c[:, :BLOCK]
        dv_s[:, T:] = dkv_c[:, BLOCK:]
        first_tile = jnp.where(ti == 0, 1, 0)
        units = [(b, pair, e) for b in range(nb) for pair in range(N_Q_HEADS // 2) for e in range(2)]
        geometry = [_band_geometry(first_tile if b == 0 else 0) for b in range(nb)]
        keys = [slice(b * BLOCK, (b + 2) * BLOCK) for b in range(nb)]
        tile = {(b, pair): (slice(b * BLOCK, (b + 1) * BLOCK), slice(pair * LANES, (pair + 1) * LANES))
                for b in range(nb) for pair in range(N_Q_HEADS // 2)}
        qp = {k: (proj_ref[rc] * 0.125).astype(BF16) for k, rc in tile.items()}
        dob = {k: dy_ref[rc].astype(BF16) for k, rc in tile.items()}
        qp_t = {k: jnp.transpose(proj_ref[rc] * 0.125).astype(BF16) for k, rc in tile.items()}
        dob_t = {k: jnp.transpose(dy_ref[rc]).astype(BF16) for k, rc in tile.items()}
        scores = [_fold(_dot_nt(qp[(b, pair)], kx[pair // 2][e][keys[b]]), geometry[b][0]) for b, pair, e in units]
        dprob = [_fold(_dot_nt(dob[(b, pair)], vx[pair // 2][e][keys[b]]), geometry[b][0]) for b, pair, e in units]
        pn_wide, ds_wide = [], []
        for s, dpm, (b, pair, e) in zip(scores, dprob, units):
            h = 2 * pair + e
            own = geometry[b][0]
            pn, psink = _softmax_band(s, h, geometry[b], sink_ref[h])
            dsum = jnp.sum(pn * dpm, axis=-1, keepdims=True)
            dsink_ref[h:h + 1, :] += jnp.full((1, LANES), -1.0, F32) * jnp.sum(psink * dsum)
            pn_wide.append(_unfold(pn, own).astype(BF16))
            ds_wide.append(_unfold(pn * (dpm - dsum), own).astype(BF16))
        dq = {}
        for pw, ds, (b, pair, e) in zip(pn_wide, ds_wide, units):
            g = pair // 2
            head_e = slice(e * HEAD_DIM, (e + 1) * HEAD_DIM)
            head_g = slice(g * HEAD_DIM, (g + 1) * HEAD_DIM)
            dv_s[head_g, keys[b]] += _dot(dob_t[(b, pair)], pw)[head_e, :]
            dk_s[head_g, keys[b]] += _dot(qp_t[(b, pair)], ds)[head_e, :]
            part = _dot(ds, kx[g][e][keys[b]])
            dq[(b, pair)] = part if e == 0 else dq[(b, pair)] + part
        for k, rc in tile.items():
            dp_ref[rc] = (0.125 * dq[k]).astype(BF16)
        dp_ref[:, C_KV:C_KV + KV_WIDTH] = jnp.transpose(dk_s[:, BLOCK:]).astype(BF16)
        dp_ref[:, C_KV + KV_WIDTH:C_KV + 2 * KV_WIDTH] = jnp.transpose(dv_s[:, BLOCK:]).astype(BF16)
        dkv_c[:, :BLOCK] = dk_s[:, :BLOCK]
        dkv_c[:, BLOCK:] = dv_s[:, :BLOCK]

        lx = proj_ref[:, C_LX:C_LX + LRU_WIDTH]
        lxprev = lxprev_ref[...] * has_prev
        xc, xcb, r, ig, sp, a, mult, first = _lru_gates(lx, lxprev, small_ref, wa_ref, wx_ref, row, ti * T)
        hp = hp_ref[...]
        hh = a * hp + mult * (ig * xc)
        lg = proj_ref[:, C_LG:C_LG + LRU_WIDTH]
        gl, th = _gelu(lg)
        dyl = dy_ref[:, ATTN_WIDTH:ATTN_WIDTH + LRU_WIDTH]
        dp_ref[:, C_LG:C_LG + LRU_WIDTH] = (dyl * hh * _gelu_grad(lg, th)).astype(BF16)
        aa = jnp.where(row < T - 1, pltpu.roll(a, T - 1, 0), 1.0)
        bb = dyl * gl
        s = 1
        while s < T:
            a_sh = jnp.where(row < T - s, pltpu.roll(aa, T - s, 0), 1.0)
            b_sh = jnp.where(row < T - s, pltpu.roll(bb, T - s, 0), 0.0)
            bb = bb + aa * b_sh
            aa = aa * a_sh
            s *= 2
        G = bb + aa * p_c[0:1, :]
        p_c[...] = jnp.broadcast_to(_edge_row(a * G, False), p_c.shape)
        da = G * hp
        dmult = G * (ig * xc)
        dig = G * mult * xc
        dxc = G * mult * ig
        dla = da * a + dmult * jnp.where(first, 0.0, -(a * a) / mult)
        dr = dla * ((-LRU_C) * sp)
        lam = small_ref[7:8, :]
        dsm_ref[7:8, :] += jnp.sum(dla * ((-LRU_C) * r), axis=0, keepdims=True) * (-_sigmoid(-lam))
        dpa = dr * r * (1.0 - r)
        dpx = dig * ig * (1.0 - ig)
        dsm_ref[5:6, :] += jnp.sum(dpa, axis=0, keepdims=True)
        dsm_ref[6:7, :] += jnp.sum(dpx, axis=0, keepdims=True)
        dpab = dpa.astype(BF16)
        dpxb = dpx.astype(BF16)
        dwa_ref[...] += _dot_tn(xcb, dpab)
        dwx_ref[...] += _dot_tn(xcb, dpxb)
        dxc = dxc + _dot_nt(dpab, wa_ref[...]) + _dot_nt(dpxb, wx_ref[...])
        dsm_ref[4:5, :] += jnp.sum(dxc, axis=0, keepdims=True)
        dsm_ref[3:4, :] += jnp.sum(dxc * lx, axis=0, keepdims=True)
        for k in range(3):
            dsm_ref[k:k + 1, :] += jnp.sum(dxc * _past(lx, lxprev, 3 - k, row), axis=0, keepdims=True)
        nxt = dxc_n[...]
        dlx = (small_ref[3:4, :] * dxc + small_ref[2:3, :] * _future(dxc, nxt, 1, row)
               + small_ref[1:2, :] * _future(dxc, nxt, 2, row) + small_ref[0:1, :] * _future(dxc, nxt, 3, row))
        dxc_n[...] = dxc
        dp_ref[:, C_LX:C_LX + LRU_WIDTH] = dlx.astype(BF16)

        sc = proj_ref[:, C_SC:C_SC + CONV_WIDTH]
        sx = proj_ref[:, C_SX:C_SX + CONV_WIDTH]
        sb = proj_ref[:, C_SB:C_SB + CONV_WIDTH]
        z = sc * sx
        zprev = (scprev_ref[...] * sxprev_ref[...]) * has_prev
        z1 = _past(z, zprev, 1, row)
        z2 = _past(z, zprev, 2, row)
        c3 = small_ref[10:11, :] * z + small_ref[9:10, :] * z1 + small_ref[8:9, :] * z2
        dys = dy_ref[:, ATTN_WIDTH + LRU_WIDTH:]
        dp_ref[:, C_SB:C_SB + CONV_WIDTH] = (dys * c3).astype(BF16)
        dc3 = dys * sb
        dsm_ref[10:11, :] += jnp.sum(dc3 * z, axis=0, keepdims=True)
        dsm_ref[9:10, :] += jnp.sum(dc3 * z1, axis=0, keepdims=True)
        dsm_ref[8:9, :] += jnp.sum(dc3 * z2, axis=0, keepdims=True)
        nxt3 = dc3_n[...]
        dz = (small_ref[10:11, :] * dc3 + small_ref[9:10, :] * _future(dc3, nxt3, 1, row)
              + small_ref[8:9, :] * _future(dc3, nxt3, 2, row))
        dc3_n[...] = dc3
        dp_ref[:, C_SC:C_SC + CONV_WIDTH] = (dz * sx).astype(BF16)
        dp_ref[:, C_SX:C_SX + CONV_WIDTH] = (dz * sc).astype(BF16)

    fix = lambda i: (0, 0)
    cur = lambda i: (nT - 1 - i, 0)
    prev_cols = lambda cb: (lambda i: (jnp.maximum(nT - 2 - i, 0), cb))
    return _pcall(
        body, name=name, grid=(nT,),
        in_specs=[pl.BlockSpec((T, IN_PROJ_WIDTH), cur),
                  pl.BlockSpec((BLOCK, 2 * KV_WIDTH),
                               lambda i: (jnp.maximum((nT - 1 - i) * bpt - 1, 0), C_KV // (2 * KV_WIDTH))),
                  pl.BlockSpec((T, LRU_WIDTH), prev_cols(C_LX // LRU_WIDTH)),
                  pl.BlockSpec((T, CONV_WIDTH), prev_cols(C_SC // CONV_WIDTH)),
                  pl.BlockSpec((T, CONV_WIDTH), prev_cols(C_SX // CONV_WIDTH)),
                  pl.BlockSpec((T, D_MODEL), cur),
                  pl.BlockSpec((T, LRU_WIDTH), cur),
                  pl.BlockSpec(memory_space=pltpu.SMEM),
                  pl.BlockSpec((16, LRU_WIDTH), fix),
                  pl.BlockSpec((LRU_WIDTH, LRU_WIDTH), fix),
                  pl.BlockSpec((LRU_WIDTH, LRU_WIDTH), fix)],
        out_specs=[pl.BlockSpec((T, IN_PROJ_WIDTH), cur),
                   pl.BlockSpec((16, LRU_WIDTH), fix),
                   pl.BlockSpec((SUBLANES, LANES), fix),
                   pl.BlockSpec((LRU_WIDTH, LRU_WIDTH), fix),
                   pl.BlockSpec((LRU_WIDTH, LRU_WIDTH), fix)],
        out_shape=[jax.ShapeDtypeStruct((S, IN_PROJ_WIDTH), BF16),
                   jax.ShapeDtypeStruct((16, LRU_WIDTH), F32),
                   jax.ShapeDtypeStruct((SUBLANES, LANES), F32),
                   jax.ShapeDtypeStruct((LRU_WIDTH, LRU_WIDTH), F32),
                   jax.ShapeDtypeStruct((LRU_WIDTH, LRU_WIDTH), F32)],
        scratch_shapes=[pltpu.VMEM((KV_WIDTH, T + BLOCK), F32), pltpu.VMEM((KV_WIDTH, T + BLOCK), F32),
                        pltpu.VMEM((BLOCK, 2 * KV_WIDTH), F32), pltpu.VMEM((T, LRU_WIDTH), F32),
                        pltpu.VMEM((T, CONV_WIDTH), F32), pltpu.VMEM((SUBLANES, LRU_WIDTH), F32)],
        sem=("arbitrary",), args=(proj, proj, proj, proj, proj, dymix, hprev, sinks, small, wa, wx), jobs=jobs)


def _mod_matmul(c_all, w_mod, name, jobs=()):
    L, Dm, N = w_mod.shape
    R = c_all.shape[0]
    tn = 768

    def body(c_ref, w_ref, o_ref, ca_ref):
        cv = c_ref[...]
        ca = (cv * _sigmoid(cv)).astype(BF16)
        ca_ref[...] = ca
        o_ref[0] = _dot(ca, w_ref[0].astype(BF16))

    return _pcall(
        body, name=name, grid=(L, N // tn),
        in_specs=[pl.BlockSpec((R, Dm), lambda l, n: (0, 0)),
                  pl.BlockSpec((1, Dm, tn), lambda l, n: (l, 0, n))],
        out_specs=[pl.BlockSpec((1, R, tn), lambda l, n: (l, 0, n)), pl.BlockSpec((R, Dm), lambda l, n: (0, 0))],
        out_shape=[jax.ShapeDtypeStruct((L, R, N), F32), jax.ShapeDtypeStruct((R, Dm), BF16)],
        sem=("arbitrary", "arbitrary"), args=(c_all, w_mod), jobs=jobs)


def _adamw_update(g, w_ref, m_ref, v_ref, go_ref, d_ref, mo_ref, vo_ref):
    mn = ADAM_B1 * m_ref[...] + (1.0 - ADAM_B1) * g
    vn = ADAM_B2 * v_ref[...] + (1.0 - ADAM_B2) * (g * g)
    go_ref[...] = g
    mo_ref[...] = mn
    vo_ref[...] = vn
    m_hat = mn / (1.0 - ADAM_B1 ** ADAM_STEP)
    v_hat = vn / (1.0 - ADAM_B2 ** ADAM_STEP)
    d_ref[...] = (-ADAM_LR) * (m_hat / (jnp.sqrt(v_hat) + ADAM_EPS) + ADAM_WD * w_ref[...])


def _adamw(w, g, m, v, name):
    R, C = w.shape
    tr = 8
    for cand in (512, 256, 128, 64, 32, 16, 8):
        if R % cand == 0 and cand * C * 4 <= (1 << 20):
            tr = cand
            break

    def body(w_ref, g_ref, *rest):
        _adamw_update(g_ref[...], w_ref, *rest)

    spec = pl.BlockSpec((tr, C), lambda i: (i, 0))
    return _pcall(body, name=name, grid=(R // tr,), in_specs=[spec] * 4, out_specs=[spec] * 4,
                  out_shape=[jax.ShapeDtypeStruct((R, C), F32)] * 4, sem=("arbitrary",), args=(w, g, m, v))


def _adamw_partials(w, partials, m, v, name):
    nl = len(partials)
    _, R, C = partials[0][0].shape
    tr = 8
    for cand in (256, 128, 64, 32, 16):
        if R % cand == 0 and cand * C * 4 <= (1 << 19):
            tr = cand
            break
    ni = R // tr

    def body(*refs):
        w_ref, p_refs = refs[0], refs[1:1 + 2 * nl]
        m_ref, v_ref, go_ref, d_ref, mo_ref, vo_ref = refs[1 + 2 * nl:]
        for l in range(nl):
            @pl.when(pl.program_id(0) == l)
            def _(pair=p_refs[2 * l:2 * l + 2]):
                own, sib = [((p[0].astype(F32) + p[1].astype(F32)) + p[2].astype(F32)) + p[3].astype(F32)
                            for p in pair]
                _adamw_update(own + sib, w_ref, m_ref, v_ref, go_ref, d_ref, mo_ref, vo_ref)

    def slots(l):
        return pl.BlockSpec((N_CHIPS, tr, C),
                            lambda ll, i: (0, jnp.where(ll == l, i, jnp.where(ll < l, 0, ni - 1)), 0))

    spec = pl.BlockSpec((tr, C), lambda ll, i: (ll * ni + i, 0))
    return pl.pallas_call(
        body, name=name, grid=(nl, ni),
        in_specs=[spec] + [slots(l) for l in range(nl) for _ in range(2)] + [spec, spec], out_specs=[spec] * 4,
        out_shape=[jax.ShapeDtypeStruct((nl * R, C), F32)] * 4,
        compiler_params=_cp("arbitrary", "arbitrary"),
    )(w, *[p for pair in partials for p in pair], m, v)


def _all_gather_small(v, name):
    M, N = v.shape

    def body(x_ref, out_ref, sum_ref, send_sems, recv_sems, local_sem):
        x, y, c = lax.axis_index("x"), lax.axis_index("y"), lax.axis_index("c")
        me, sibling = (x, y, c), (x, y, 1 - c)
        chips = [(1 - x, y), (x, 1 - y), (1 - x, 1 - y)]

        def rows(px, py, pc):
            return out_ref.at[pl.ds(pl.multiple_of((4 * px + 2 * py + pc) * M, SUBLANES), M), :]

        def copy(k, block, to, src=None):
            return pltpu.make_async_remote_copy(
                src_ref=rows(*block) if src is None else src, dst_ref=rows(*block),
                send_sem=send_sems.at[k], recv_sem=recv_sems.at[k], device_id=to, device_id_type=MESH)

        mine = pltpu.make_async_copy(x_ref, rows(*me), local_sem)
        mine.start()
        first = [copy(0, me, sibling, src=x_ref)]
        first += [copy(1 + j, me, (*chip, c), src=x_ref) for j, chip in enumerate(chips)]
        for cp in first:
            cp.start()
        passed = [copy(4 + j, (*chip, c), sibling) for j, chip in enumerate(chips)]
        for j, chip in enumerate(chips):
            copy(1 + j, (*chip, c), me).wait_recv()
            passed[j].start()
        copy(0, sibling, me).wait_recv()
        for j, chip in enumerate(chips):
            copy(4 + j, (*chip, 1 - c), me).wait_recv()
        for cp in first + passed:
            cp.wait_send()
        mine.wait()
        acc = out_ref[0:M, :]
        for d in range(1, N_DEV):
            acc = acc + out_ref[d * M:(d + 1) * M, :]
        sum_ref[...] = acc

    return pl.pallas_call(
        body, name=name,
        out_shape=[jax.ShapeDtypeStruct((N_DEV * M, N), F32), jax.ShapeDtypeStruct((M, N), F32)],
        in_specs=[pl.BlockSpec(memory_space=pltpu.VMEM)],
        out_specs=[pl.BlockSpec(memory_space=pltpu.VMEM), pl.BlockSpec(memory_space=pltpu.VMEM)],
        scratch_shapes=[pltpu.SemaphoreType.DMA((7,)), pltpu.SemaphoreType.DMA((7,)), pltpu.SemaphoreType.DMA],
        compiler_params=pltpu.CompilerParams(vmem_limit_bytes=VMEM_LIMIT),
    )(v)


_BIG = (("w_ffn1_gu", 1), ("w_ffn1_down", 0), ("w_ffn2_gu", 1), ("w_ffn2_down", 0), ("w_in", 1), ("w_out", 0))
_AXIS = dict(_BIG)

_GATHER_PLAN = {
    "first": [(0, "w_ffn1_gu"), (0, "w_ffn1_down")],
    (0, "ffn1"): [(0, "w_in"), (0, "w_out"), (0, "w_ffn2_gu")],
    (0, "mix_core"): [(0, "w_ffn2_down")],
    (0, "ffn2"): [(1, "w_ffn1_gu"), (1, "w_ffn1_down")],
    (1, "ffn1"): [(1, "w_in"), (1, "w_out"), (1, "w_ffn2_gu")],
    (1, "mix_core"): [(1, "w_ffn2_down")],
}


def _pack(arrs, rows_multiple=SUBLANES):
    flat = jnp.concatenate([a.astype(F32).reshape(-1) for a in arrs])
    unit = rows_multiple * LANES
    total = -(-flat.shape[0] // unit) * unit
    return jnp.pad(flat, (0, total - flat.shape[0])).reshape(total // LANES, LANES)


def _unpack(flat, shapes):
    out, off = [], 0
    for shp in shapes:
        n = int(math.prod(shp))
        out.append(flat[off:off + n].reshape(shp))
        off += n
    return out


def _block_diag(w):
    out = jnp.zeros((LRU_WIDTH, LRU_WIDTH), F32)
    for h in range(4):
        out = lax.dynamic_update_slice(out, w[h], (h * HEAD_DIM, h * HEAD_DIM))
    return out


def _diag_blocks(w):
    return jnp.stack([w[h * HEAD_DIM:(h + 1) * HEAD_DIM, h * HEAD_DIM:(h + 1) * HEAD_DIM] for h in range(4)])


def _rows8(*rows):
    z = jnp.zeros((8 - len(rows), rows[0].shape[-1]), F32)
    return jnp.concatenate([jnp.stack(rows), z], axis=0)


def kernel(x, c, w_mod, b_mod, g_norm, w_ffn1_gu, w_ffn1_down, w_ffn2_gu, w_ffn2_down, w_in, w_out, attn_sinks, lru_conv_w, lru_conv_b, lru_gate_a_w, lru_gate_a_b, lru_gate_x_w, lru_gate_x_b, lru_lambda, sc_conv_w, g_final, loss_target, m_w_mod, m_b_mod, m_g_norm, m_w_ffn1_gu, m_w_ffn1_down, m_w_ffn2_gu, m_w_ffn2_down, m_w_in, m_w_out, m_attn_sinks, m_lru_conv_w, m_lru_conv_b, m_lru_gate_a_w, m_lru_gate_a_b, m_lru_gate_x_w, m_lru_gate_x_b, m_lru_lambda, m_sc_conv_w, m_g_final, v_w_mod, v_b_mod, v_g_norm, v_w_ffn1_gu, v_w_ffn1_down, v_w_ffn2_gu, v_w_ffn2_down, v_w_in, v_w_out, v_attn_sinks, v_lru_conv_w, v_lru_conv_b, v_lru_gate_a_w, v_lru_gate_a_b, v_lru_gate_x_w, v_lru_gate_x_b, v_lru_lambda, v_sc_conv_w, v_g_final):
    W = dict(w_mod=w_mod, b_mod=b_mod, g_norm=g_norm, w_ffn1_gu=w_ffn1_gu, w_ffn1_down=w_ffn1_down,
             w_ffn2_gu=w_ffn2_gu, w_ffn2_down=w_ffn2_down, w_in=w_in, w_out=w_out, attn_sinks=attn_sinks,
             lru_conv_w=lru_conv_w, lru_conv_b=lru_conv_b, lru_gate_a_w=lru_gate_a_w, lru_gate_a_b=lru_gate_a_b,
             lru_gate_x_w=lru_gate_x_w, lru_gate_x_b=lru_gate_x_b, lru_lambda=lru_lambda, sc_conv_w=sc_conv_w,
             g_final=g_final)
    M1 = dict(w_mod=m_w_mod, b_mod=m_b_mod, g_norm=m_g_norm, w_ffn1_gu=m_w_ffn1_gu, w_ffn1_down=m_w_ffn1_down,
              w_ffn2_gu=m_w_ffn2_gu, w_ffn2_down=m_w_ffn2_down, w_in=m_w_in, w_out=m_w_out,
              attn_sinks=m_attn_sinks, lru_conv_w=m_lru_conv_w, lru_conv_b=m_lru_conv_b,
              lru_gate_a_w=m_lru_gate_a_w, lru_gate_a_b=m_lru_gate_a_b, lru_gate_x_w=m_lru_gate_x_w,
              lru_gate_x_b=m_lru_gate_x_b, lru_lambda=m_lru_lambda, sc_conv_w=m_sc_conv_w, g_final=m_g_final)
    V1 = dict(w_mod=v_w_mod, b_mod=v_b_mod, g_norm=v_g_norm, w_ffn1_gu=v_w_ffn1_gu, w_ffn1_down=v_w_ffn1_down,
              w_ffn2_gu=v_w_ffn2_gu, w_ffn2_down=v_w_ffn2_down, w_in=v_w_in, w_out=v_w_out,
              attn_sinks=v_attn_sinks, lru_conv_w=v_lru_conv_w, lru_conv_b=v_lru_conv_b,
              lru_gate_a_w=v_lru_gate_a_w, lru_gate_a_b=v_lru_gate_a_b, lru_gate_x_w=v_lru_gate_x_w,
              lru_gate_x_b=v_lru_gate_x_b, lru_lambda=v_lru_lambda, sc_conv_w=v_sc_conv_w, g_final=v_g_final)
    names = ["w_mod", "b_mod", "g_norm", "w_ffn1_gu", "w_ffn1_down", "w_ffn2_gu", "w_ffn2_down", "w_in", "w_out",
             "attn_sinks", "lru_conv_w", "lru_conv_b", "lru_gate_a_w", "lru_gate_a_b", "lru_gate_x_w",
             "lru_gate_x_b", "lru_lambda", "sc_conv_w", "g_final"]

    xs = x[0]
    tgt = loss_target[0]
    S = xs.shape[0]
    chip = 2 * lax.axis_index("x") + lax.axis_index("y")
    batch = 2 * chip + lax.axis_index("c")
    L = DEPTH

    fwd_shapes = [(D_MODEL,), g_norm.shape, lru_conv_w.shape, sc_conv_w.shape]
    gathered, _ = _all_gather_small(_pack([c[0], g_norm, lru_conv_w, sc_conv_w]), "gather_small_fwd")
    gathered = gathered.reshape(N_DEV, -1)
    c_all = gathered[:, :D_MODEL]
    per_chip = [_unpack(gathered[2 * jj], fwd_shapes) for jj in range(N_CHIPS)]
    g_norm_full = jnp.concatenate([p[1] for p in per_chip], axis=-1)
    lru_conv_w_full = jnp.concatenate([p[2] for p in per_chip], axis=-1)
    sc_conv_w_full = jnp.concatenate([p[3] for p in per_chip], axis=-1)

    full = {}

    def gather_jobs(key):
        return [_GatherJob(W[n][l].astype(BF16), _AXIS[n]) for l, n in _GATHER_PLAN.get(key, ())]

    def landed(key, outs):
        full.update(zip(_GATHER_PLAN.get(key, ()), outs))

    c_pad = jnp.concatenate([c_all, jnp.zeros_like(c_all)], axis=0)
    (mod_part, c_act), ex = _mod_matmul(c_pad, w_mod, "mod_matmul", gather_jobs("first"))
    landed("first", ex)
    mod_all, _ = _all_gather_small(mod_part.reshape(-1, LANES), "gather_mod")
    mod_all = mod_all.reshape(N_DEV, L, 16, -1)
    mod_rows = [lax.dynamic_index_in_dim(mod_all[2 * jj], batch, axis=1, keepdims=False) for jj in range(N_CHIPS)]
    mod = (jnp.concatenate(mod_rows, axis=-1) + b_mod).reshape(L, 9, D_MODEL)

    def nrm_rows(l, s):
        return _rows8(g_norm_full[l, s], mod[l, 3 * s], mod[l, 3 * s + 1], mod[l, 3 * s + 2])

    def mixer_params(l):
        small = jnp.concatenate([lru_conv_w_full[l], lru_conv_b[l][None], lru_gate_a_b[l][None],
                                 lru_gate_x_b[l][None], lru_lambda[l][None], sc_conv_w_full[l],
                                 jnp.zeros((5, LRU_WIDTH), F32)], axis=0)
        return (attn_sinks[l], small, _block_diag(lru_gate_a_w[l]).astype(BF16),
                _block_diag(lru_gate_x_w[l]).astype(BF16))

    saved = []
    xcur = xs
    for l in range(L):
        n1, n2, n3 = nrm_rows(l, 0), nrm_rows(l, 1), nrm_rows(l, 2)

        def ffn(which, xin, nrm, head=None):
            key = (l, which)
            (xo, h, gu, y, *stats), ex = _ffn_fwd(xin, nrm, full[(l, f"w_{which}_gu")], full[(l, f"w_{which}_down")],
                                                  f"l{l}_{which}", gather_jobs(key), head)
            landed(key, ex)
            return (xo, *stats), (xin, h, gu, y)

        (x1,), s1 = ffn("ffn1", xcur, n1)
        (proj, h2), _ = _norm_matmul(x1, n2, full[(l, "w_in")], F32, f"l{l}_mix_in")
        mp = mixer_params(l)
        (ymix, hprev), ex = _mixer_fwd(proj, *mp, f"l{l}_mix_core", gather_jobs((l, "mix_core")))
        landed((l, "mix_core"), ex)
        (x2, ymo), _ = _proj_residual(ymix, full[(l, "w_out")], x1, n2, f"l{l}_mix_out")
        s2 = (x1, h2, proj, ymix, ymo, hprev, mp)
        (xcur, *stats), s3 = ffn("ffn2", x2, n3, (_rows8(g_final), tgt) if l == L - 1 else None)
        saved.append((n1, n2, n3, s1, s2, s3))

    dx, stats = xcur, stats[0]
    loss = lax.psum(stats[1, 0], ("x", "y", "c"))
    d_g_final = stats[0]

    recv, theirs = {}, {}
    waiting = []

    def carried(fn, *a, extra=()):
        items = waiting + list(extra)
        waiting.clear()
        outs, landed_now = fn(*a, jobs=[_SiblingJob(recv[(ll, n)]) if g is None else _ScatterJob(g, _AXIS[n])
                                        for ll, n, g in items])
        for (ll, n, g), arr in zip(items, landed_now):
            if g is None:
                theirs[(ll, n)] = arr
            else:
                recv[(ll, n)] = arr
                waiting.append((ll, n, None))
        return outs

    dmod, d_gnorm, d_small = [None] * L, [None] * L, [None] * L
    for l in reversed(range(L)):
        n1, n2, n3, s1, s2, s3 = saved[l]

        def plain(fn, *a):
            return fn(*a)[0]

        def ffn_bwd(which, dxo, sv, nrm, last):
            xin, h, gu, y = sv
            tag = f"l{l}_{which}"
            dgu, dgate, dw_down = (carried if which == "ffn2" else plain)(
                _proj_residual_bwd, dxo, gu, y, full[(l, f"w_{which}_down")], nrm, 0.5, True, tag + "_down_bwd")
            dw_gu = carried(_atb, h, dgu, BF16, 1024, 2816, tag + "_dw_gu", extra=[(l, f"w_{which}_down", dw_down)])
            mine = [(l, f"w_{which}_gu", dw_gu)]
            dxi, red = (carried if last else plain)(
                _nt_norm_bwd, dgu, full[(l, f"w_{which}_gu")], xin, nrm, dxo, tag + "_gu_bwd",
                **(dict(extra=mine) if last else {}))
            if not last:
                waiting.extend(mine)
            return dxi, (red[0], red[1], dgate[0]), red[2]

        dx, dm3, dg3 = ffn_bwd("ffn2", dx, s3, n3, False)
        x_in, h2, proj, ymix, ymo, hprev, mp = s2
        dymix, dgate, dw_out = plain(_proj_residual_bwd, dx, ymix, ymo, full[(l, "w_out")], n2, 1.0, False,
                                     f"l{l}_mix_out_bwd")
        dproj, dsm, dsink, dwa, dwx = carried(_mixer_bwd, proj, dymix, hprev, *mp, f"l{l}_mix_core_bwd",
                                              extra=[(l, "w_out", dw_out)])
        dw_in = plain(_atb, h2, dproj, BF16, 1024, 2048, f"l{l}_dw_in")
        dx, red = carried(_nt_norm_bwd, dproj, full[(l, "w_in")], x_in, n2, dx, f"l{l}_mix_in_bwd",
                          extra=[(l, "w_in", dw_in)])
        dm2, dg2 = (red[0], red[1], dgate[0]), red[2]
        dx, dm1, dg1 = ffn_bwd("ffn1", dx, s1, n1, l == 0)
        dmod[l] = jnp.stack(list(dm1) + list(dm2) + list(dm3))
        d_gnorm[l] = jnp.stack([dg1, dg2, dg3])
        d_small[l] = (dsink[:, 0], dsm[0:4], dsm[4], _diag_blocks(dwa), dsm[5], _diag_blocks(dwx), dsm[6],
                      dsm[7], dsm[8:11])
    grad_x = dx[None]

    def both(k):
        return jnp.stack([d_small[0][k], d_small[1][k]])
    small_names = ["g_norm", "attn_sinks", "lru_conv_w", "lru_conv_b", "lru_gate_a_w", "lru_gate_a_b",
                   "lru_gate_x_w", "lru_gate_x_b", "lru_lambda", "sc_conv_w", "g_final"]
    small_parts = [jnp.stack(d_gnorm)] + [both(k) for k in range(9)] + [d_g_final]
    dmod_flat = jnp.stack(dmod).reshape(-1)
    bwd_gathered, bwd_sum = _all_gather_small(_pack([dmod_flat] + small_parts), "gather_small_bwd")
    n_mod = dmod_flat.shape[0]
    dmod_all = bwd_gathered.reshape(N_DEV, -1)[:, :n_mod].reshape(N_DEV, L, 9 * D_MODEL)
    bwd_sum = bwd_sum.reshape(-1)
    G = {"b_mod": bwd_sum[:n_mod].reshape(L, 9 * D_MODEL)}
    G.update(zip(small_names, _unpack(bwd_sum[n_mod:], [p.shape for p in small_parts])))
    for n in ("g_norm", "lru_conv_w", "sc_conv_w"):
        wdt = W[n].shape[-1]
        G[n] = lax.dynamic_slice_in_dim(G[n], chip * wdt, wdt, axis=G[n].ndim - 1)

    ncol = w_mod.shape[-1]
    dmod_cols = lax.dynamic_slice_in_dim(dmod_all, chip * ncol, ncol, axis=2)
    zeros8 = jnp.zeros((N_DEV, ncol), F32)
    g_w_mod = jnp.stack([carried(_atb, c_act, jnp.concatenate([dmod_cols[:, l], zeros8], axis=0).astype(BF16), F32,
                                 D_MODEL, 768, f"l{l}_dw_mod") for l in range(L)])

    out_g, out_d, out_m, out_v = {}, {}, {}, {}
    res, _ = _adamw(w_mod.reshape(-1, ncol), g_w_mod.reshape(-1, ncol), m_w_mod.reshape(-1, ncol),
                    v_w_mod.reshape(-1, ncol), "adamw_w_mod")
    out_g["w_mod"], out_d["w_mod"], out_m["w_mod"], out_v["w_mod"] = [r.reshape(w_mod.shape) for r in res]
    for n, _ in _BIG:
        shp = W[n].shape
        flat = (shp[0] * shp[1], shp[2])
        res = _adamw_partials(W[n].reshape(flat), [(recv[(l, n)], theirs[(l, n)]) for l in range(L)],
                              M1[n].reshape(flat), V1[n].reshape(flat), f"adamw_{n}")
        out_g[n], out_d[n], out_m[n], out_v[n] = [r.reshape(shp) for r in res]
    rest = ["b_mod"] + small_names
    shapes = [W[n].shape for n in rest]
    res, _ = _adamw(_pack([W[n] for n in rest]), _pack([G[n] for n in rest]), _pack([M1[n] for n in rest]),
                    _pack([V1[n] for n in rest]), "adamw_small")
    for dst, r in zip((out_g, out_d, out_m, out_v), res):
        dst.update(zip(rest, _unpack(r.reshape(-1), shapes)))

    return (loss, grad_x, *[out_g[n] for n in names], *[out_d[n] for n in names],
            *[out_m[n] for n in names], *[out_v[n] for n in names])
```

```python
import math

import jax
import jax.numpy as jnp
from jax import lax
from jax.experimental import pallas as pl
from jax.experimental.pallas import tpu as pltpu

F32 = jnp.float32
BF16 = jnp.bfloat16

D_MODEL = 1024
DEPTH = 2
HEAD_DIM = 64
N_Q_HEADS = 8
ATTN_WIDTH = 512
KV_WIDTH = 128
LRU_WIDTH = 256
CONV_WIDTH = 256
IN_PROJ_WIDTH = 2048
BLOCK = 128
D_FF = 2816
EPS = 1e-6
NEG_INF = -1e30
LRU_C = 8.0
N_CHIPS = 4
N_DEV = 8

C_Q, C_KV, C_LX, C_LG, C_SB, C_SC, C_SX = 0, 512, 768, 1024, 1280, 1536, 1792

ADAM_LR = 0.001
ADAM_B1 = 0.9
ADAM_B2 = 0.999
ADAM_EPS = 1e-08
ADAM_WD = 0.01
ADAM_STEP = 10

LANES = 128
SUBLANES = 8
VMEM_LIMIT = 56 * 1024 * 1024
MIX_TILE = 256

MESH = pl.DeviceIdType.MESH


def _cp(*sem):
    return pltpu.CompilerParams(dimension_semantics=sem, vmem_limit_bytes=VMEM_LIMIT)


def _tile(n, pref):
    t = min(n, pref)
    while n % t:
        t //= 2
    return t


MXU_DIM = 256


def _resident(shape):
    return pl.BlockSpec(shape, lambda *_: (0, 0), pipeline_mode=pl.Buffered(1))


def _sigmoid(v):
    return 1.0 / (1.0 + jnp.exp(-v))


def _expm1(v):
    series = v * (1.0 + v * (0.5 + v * (1.0 / 6.0)))
    return jnp.where(v > -0.01, series, jnp.exp(v) - 1.0)


def _softplus_neg(lam):
    e = jnp.exp(-jnp.abs(lam))
    log1p = jnp.where(e < 1e-2, e * (1.0 - e * (0.5 - e * (1.0 / 3.0))), jnp.log(1.0 + e))
    return jnp.maximum(-lam, 0.0) + log1p


_GELU_K = math.sqrt(2.0 / math.pi)
_GELU_C = 0.044715


def _gelu(v):
    t = jnp.tanh(_GELU_K * (v + _GELU_C * v * v * v))
    return 0.5 * v * (1.0 + t), t


def _gelu_grad(v, t):
    return 0.5 * (1.0 + t) + 0.5 * v * (1.0 - t * t) * _GELU_K * (1.0 + 3.0 * _GELU_C * v * v)


def _dot(a, b):
    return jnp.dot(a, b, preferred_element_type=F32)


def _dot_nt(a, b):
    return lax.dot_general(a, b, (((1,), (1,)), ((), ())), preferred_element_type=F32)


def _dot_tn(a, b):
    return lax.dot_general(a, b, (((0,), (0,)), ((), ())), preferred_element_type=F32)


def _window(ref, axis, j, width):
    start = pl.multiple_of(j * width, LANES if axis == 1 else 16)
    if axis == 1:
        return ref.at[:, pl.ds(start, width)]
    return ref.at[pl.ds(start, width), :]


def _chip_peers():
    x, y, c = lax.axis_index("x"), lax.axis_index("y"), lax.axis_index("c")
    return x, y, c, [(1 - x, y), (x, 1 - y), (1 - x, 1 - y)]


class _GatherJob:
    def __init__(self, shard, axis):
        self.src, self.axis, self.width, self.half = shard, axis, shard.shape[axis], shard.shape[0] // 2
        full = tuple(d * N_CHIPS if k == axis else d for k, d in enumerate(shard.shape))
        self.out_shape = jax.ShapeDtypeStruct(full, shard.dtype)

    def _piece(self, ref, j, hf):
        if self.axis == 1:
            return ref.at[pl.ds(pl.multiple_of(hf * self.half, 16), self.half),
                          pl.ds(pl.multiple_of(j * self.width, LANES), self.width)]
        return ref.at[pl.ds(pl.multiple_of(j * self.width + hf * self.half, 16), self.half), :]

    def _copies(self, src, dst, send, recv, loc, t):
        x, y, c, chips = _chip_peers()
        j = 2 * x + y
        owners = [2 * px + py for px, py in chips]
        local = pltpu.make_async_copy(src, _window(dst, self.axis, j, self.width), loc.at[t])
        mine = src.at[pl.ds(pl.multiple_of(c * self.half, 16), self.half), :]

        def ici(k, owner):
            return pltpu.make_async_remote_copy(
                src_ref=mine, dst_ref=self._piece(dst, owner, c), send_sem=send.at[JOB_SEMS * t + k],
                recv_sem=recv.at[JOB_SEMS * t + k], device_id=(*chips[k], c), device_id_type=MESH)

        def relay(k, hf):
            piece = self._piece(dst, owners[k], hf)
            return pltpu.make_async_remote_copy(
                src_ref=piece, dst_ref=piece, send_sem=send.at[JOB_SEMS * t + 4 + k],
                recv_sem=recv.at[JOB_SEMS * t + 4 + k],
                device_id=(x, y, 1 - c), device_id_type=MESH)

        return (local, [ici(k, j) for k in range(3)], [ici(k, owners[k]) for k in range(3)],
                [relay(k, c) for k in range(3)], [relay(k, 1 - c) for k in range(3)])

    def start(self, *a):
        local, ici_out, _, _, _ = self._copies(*a)
        local.start()
        for cp in ici_out:
            cp.start()

    def relay(self, *a):
        _, _, ici_in, relay_out, _ = self._copies(*a)
        for arrived, onward in zip(ici_in, relay_out):
            arrived.wait_recv()
            onward.start()

    def finish(self, *a):
        local, ici_out, _, relay_out, relay_in = self._copies(*a)
        for cp in relay_in:
            cp.wait_recv()
        for cp in ici_out + relay_out:
            cp.wait_send()
        local.wait()


class _ScatterJob:
    def __init__(self, full, axis):
        self.src, self.axis, self.width = full, axis, full.shape[axis] // N_CHIPS
        shard = tuple(self.width if k == axis else d for k, d in enumerate(full.shape))
        self.out_shape = jax.ShapeDtypeStruct((N_CHIPS,) + shard, full.dtype)

    def _copies(self, src, dst, send, recv, loc, t):
        x, y, c, chips = _chip_peers()
        local = pltpu.make_async_copy(_window(src, self.axis, 2 * x + y, self.width), dst.at[3], loc.at[t])
        sends = [pltpu.make_async_remote_copy(
            src_ref=_window(src, self.axis, 2 * px + py, self.width), dst_ref=dst.at[k],
            send_sem=send.at[JOB_SEMS * t + k], recv_sem=recv.at[JOB_SEMS * t + k], device_id=(px, py, c),
            device_id_type=MESH) for k, (px, py) in enumerate(chips)]
        return local, sends

    def start(self, *a):
        local, sends = self._copies(*a)
        local.start()
        for cp in sends:
            cp.start()

    def relay(self, *a):
        pass

    def finish(self, *a):
        local, sends = self._copies(*a)
        for cp in sends:
            cp.wait_recv()
        for cp in sends:
            cp.wait_send()
        local.wait()


class _SiblingJob:
    def __init__(self, arr):
        self.src, self.out_shape = arr, jax.ShapeDtypeStruct(arr.shape, arr.dtype)

    def _copy(self, src, dst, send, recv, loc, t):
        x, y, c = lax.axis_index("x"), lax.axis_index("y"), lax.axis_index("c")
        return pltpu.make_async_remote_copy(
            src_ref=src, dst_ref=dst, send_sem=send.at[JOB_SEMS * t], recv_sem=recv.at[JOB_SEMS * t],
            device_id=(x, y, 1 - c), device_id_type=MESH)

    def start(self, *a):
        self._copy(*a).start()

    def relay(self, *a):
        pass

    def finish(self, *a):
        self._copy(*a).wait()


JOB_SEMS = 8


def _run_jobs(phase, jobs, srcs, dsts, sems):
    for t, job in enumerate(jobs):
        getattr(job, phase)(srcs[t], dsts[t], *sems, t)


def _job_scratch(n):
    return [pltpu.SemaphoreType.DMA((JOB_SEMS * n,)), pltpu.SemaphoreType.DMA((JOB_SEMS * n,)),
            pltpu.SemaphoreType.DMA((n,))]


def _pcall(body, *, name, grid, in_specs, out_specs, out_shape, sem, args, scratch_shapes=(), jobs=()):
    in_specs, out_specs, out_shape = list(in_specs), list(out_specs), list(out_shape)
    scratch_shapes = list(scratch_shapes)
    if not jobs:
        res = pl.pallas_call(body, name=name, grid=grid, in_specs=in_specs, out_specs=out_specs, out_shape=out_shape,
                             scratch_shapes=scratch_shapes, compiler_params=_cp(*sem))(*args)
        return list(res), []
    n_in, n_out, n_scr, nj = len(args), len(out_shape), len(scratch_shapes), len(jobs)
    n_steps = math.prod(grid)
    relay_step = (3 * n_steps) // 4
    relay_early = 0 < relay_step < n_steps - 1

    def wrapped(*refs):
        ins, refs = refs[:n_in], refs[n_in:]
        jin, refs = refs[:nj], refs[nj:]
        outs, refs = refs[:n_out], refs[n_out:]
        jout, refs = refs[:nj], refs[nj:]
        scr, sems = refs[:n_scr], refs[n_scr:]
        step = pl.program_id(0)
        for d in range(1, len(grid)):
            step = step * grid[d] + pl.program_id(d)

        @pl.when(step == 0)
        def _():
            _run_jobs("start", jobs, jin, jout, sems)

        if relay_early:
            @pl.when(step == relay_step)
            def _():
                _run_jobs("relay", jobs, jin, jout, sems)
        body(*ins, *outs, *scr)

        @pl.when(step == n_steps - 1)
        def _():
            if not relay_early:
                _run_jobs("relay", jobs, jin, jout, sems)
            _run_jobs("finish", jobs, jin, jout, sems)

    hbm = pl.BlockSpec(memory_space=pltpu.HBM)
    res = pl.pallas_call(
        wrapped, name=name, grid=grid, in_specs=in_specs + [hbm] * nj, out_specs=out_specs + [hbm] * nj,
        out_shape=out_shape + [job.out_shape for job in jobs], scratch_shapes=scratch_shapes + _job_scratch(nj),
        compiler_params=_cp(*sem))(*args, *[job.src for job in jobs])
    return list(res[:n_out]), list(res[n_out:])


def _hidden_chunks(k):
    return [(c0, min(6 * MXU_DIM, k - c0)) for c0 in range(0, k, 6 * MXU_DIM)]


def _loss_head(xv, gain, tgt, st_ref):
    dm = xv.shape[-1]
    rstd = lax.rsqrt(jnp.mean(xv * xv, axis=-1, keepdims=True) + EPS)
    xn = xv * rstd
    err = xn * gain - tgt
    st_ref[1:2, :] += jnp.full((1, dm), 0.5 / dm, F32) * jnp.sum(err * err)
    dy = err * (1.0 / dm)
    st_ref[0:1, :] += jnp.sum(dy * xn, axis=0, keepdims=True)
    dxn = dy * gain
    return rstd * (dxn - xn * jnp.mean(dxn * xn, axis=-1, keepdims=True))


def _ffn_fwd(x, nrm, w_gu, w_down, name, jobs=(), head=None):
    S, Dm = x.shape
    K = w_down.shape[0]
    tm = _tile(S, 256)

    def body(x_ref, nrm_ref, wgu_ref, wdn_ref, *rest):
        if head is None:
            o_ref, h_ref, gu_ref, y_ref = rest
        else:
            gf_ref, t_ref, o_ref, h_ref, gu_ref, y_ref, st_ref = rest

            @pl.when(pl.program_id(0) == 0)
            def _():
                st_ref[...] = jnp.zeros_like(st_ref)
        xv = x_ref[...]
        rstd = lax.rsqrt(jnp.mean(xv * xv, axis=-1, keepdims=True) + EPS)
        hn = (xv * rstd) * nrm_ref[0:1, :]
        hb = (hn * (1.0 + nrm_ref[2:3, :]) + nrm_ref[1:2, :]).astype(BF16)
        h_ref[...] = hb
        y = jnp.zeros((tm, Dm), F32)
        for c0, cs in _hidden_chunks(K):
            g = _dot(hb, wgu_ref[:, c0:c0 + cs])
            u = _dot(hb, wgu_ref[:, K + c0:K + c0 + cs])
            gu_ref[:, c0:c0 + cs] = g.astype(BF16)
            gu_ref[:, K + c0:K + c0 + cs] = u.astype(BF16)
            y = y + _dot((g * _sigmoid(g) * u).astype(BF16), wdn_ref[c0:c0 + cs, :])
        xo = xv + (0.5 * nrm_ref[3:4, :]) * y
        o_ref[...] = xo if head is None else _loss_head(xo, gf_ref[0:1, :], t_ref[...], st_ref)
        y_ref[...] = y.astype(BF16)

    row = lambda i: (i, 0)
    fix = lambda i: (0, 0)
    in_specs = [pl.BlockSpec((tm, Dm), row), pl.BlockSpec((8, Dm), fix), _resident((Dm, 2 * K)), _resident((K, Dm))]
    out_specs = [pl.BlockSpec((tm, Dm), row), pl.BlockSpec((tm, Dm), row), pl.BlockSpec((tm, 2 * K), row),
                 pl.BlockSpec((tm, Dm), row)]
    out_shape = [jax.ShapeDtypeStruct((S, Dm), F32), jax.ShapeDtypeStruct((S, Dm), BF16),
                 jax.ShapeDtypeStruct((S, 2 * K), BF16), jax.ShapeDtypeStruct((S, Dm), BF16)]
    args = (x, nrm, w_gu, w_down)
    if head is not None:
        in_specs += [pl.BlockSpec((8, Dm), fix), pl.BlockSpec((tm, Dm), row)]
        out_specs.append(pl.BlockSpec((8, Dm), fix))
        out_shape.append(jax.ShapeDtypeStruct((8, Dm), F32))
        args += tuple(head)
    return _pcall(body, name=name, grid=(S // tm,), in_specs=in_specs, out_specs=out_specs, out_shape=out_shape,
                  sem=("arbitrary",), args=args, jobs=jobs)


def _ffn_down_bwd(dxo, gu, y, w, nrm, name, jobs=()):
    S, Dm = dxo.shape
    K = w.shape[0]
    Ka = gu.shape[1]
    coef = 0.5
    tm = _tile(S, 256)
    n_steps = S // tm
    chunks = _hidden_chunks(K)

    def body(dxo_ref, y_ref, w_ref, nrm_ref, a_ref, da_ref, dgate_ref, dw_ref, acc):
        @pl.when(pl.program_id(0) == 0)
        def _():
            dgate_ref[...] = jnp.zeros_like(dgate_ref)
            acc[...] = jnp.zeros_like(acc)

        dxo_v = dxo_ref[...]
        dyb = ((coef * nrm_ref[3:4, :]) * dxo_v).astype(BF16)
        dgate_ref[0:1, :] += jnp.sum(coef * y_ref[...].astype(F32) * dxo_v, axis=0, keepdims=True)
        for c0, cs in chunks:
            dact = _dot_nt(dyb, w_ref[c0:c0 + cs, :])
            g = a_ref[:, c0:c0 + cs].astype(F32)
            u = a_ref[:, K + c0:K + c0 + cs].astype(F32)
            s = _sigmoid(g)
            si = g * s
            da_ref[:, c0:c0 + cs] = (dact * u * (s * (1.0 + g * (1.0 - s)))).astype(BF16)
            da_ref[:, K + c0:K + c0 + cs] = (dact * si).astype(BF16)
            acc[c0:c0 + cs, :] += _dot_tn((si * u).astype(BF16), dyb)

        @pl.when(pl.program_id(0) == n_steps - 1)
        def _():
            dw_ref[...] = acc[...].astype(BF16)

    row = lambda i: (i, 0)
    fix = lambda i: (0, 0)
    return _pcall(
        body, name=name, grid=(n_steps,),
        in_specs=[pl.BlockSpec((tm, Dm), row), pl.BlockSpec((tm, Dm), row), _resident((K, Dm)),
                  pl.BlockSpec((8, Dm), fix), pl.BlockSpec((tm, Ka), row)],
        out_specs=[pl.BlockSpec((tm, Ka), row), pl.BlockSpec((8, Dm), fix), _resident((K, Dm))],
        out_shape=[jax.ShapeDtypeStruct((S, Ka), BF16), jax.ShapeDtypeStruct((8, Dm), F32),
                   jax.ShapeDtypeStruct((K, Dm), BF16)],
        scratch_shapes=[pltpu.VMEM((K, Dm), F32)],
        sem=("arbitrary",), args=(dxo, y, w, nrm, gu), jobs=jobs)


def _atb(a, b, out_dtype, bm, bn, name, jobs=()):
    S, M = a.shape
    N = b.shape[1]
    bk = _tile(S, 1024)
    nk = S // bk

    def body(a_ref, b_ref, o_ref, acc):
        k = pl.program_id(2)

        @pl.when(k == 0)
        def _():
            acc[...] = jnp.zeros_like(acc)
        acc[...] += _dot_tn(a_ref[...], b_ref[...])

        @pl.when(k == nk - 1)
        def _():
            o_ref[...] = acc[...].astype(o_ref.dtype)

    (out,), extra = _pcall(
        body, name=name, grid=(M // bm, N // bn, nk),
        in_specs=[pl.BlockSpec((bk, bm), lambda m, n, k: (k, m)),
                  pl.BlockSpec((bk, bn), lambda m, n, k: (k, n))],
        out_specs=[pl.BlockSpec((bm, bn), lambda m, n, k: (m, n))],
        out_shape=[jax.ShapeDtypeStruct((M, N), out_dtype)],
        scratch_shapes=[pltpu.VMEM((bm, bn), F32)],
        sem=("arbitrary", "arbitrary", "arbitrary"), args=(a, b), jobs=jobs)
    return out, extra


def _norm_bwd(dh, xv, nrm_ref, red_ref):
    rstd = lax.rsqrt(jnp.mean(xv * xv, axis=-1, keepdims=True) + EPS)
    xn = xv * rstd
    gain = nrm_ref[0:1, :]
    hn = xn * gain
    dhn = dh * (1.0 + nrm_ref[2:3, :])
    red_ref[0:1, :] += jnp.sum(dh, axis=0, keepdims=True)
    red_ref[1:2, :] += jnp.sum(dh * hn, axis=0, keepdims=True)
    red_ref[2:3, :] += jnp.sum(dhn * xn, axis=0, keepdims=True)
    dxn = dhn * gain
    return rstd * (dxn - xn * jnp.mean(dxn * xn, axis=-1, keepdims=True))


def _nt_norm_bwd(dout, w, x, nrm, dxo, name, jobs=()):
    S, N = dout.shape
    Dm = w.shape[0]
    tm = _tile(S, 512)

    def body(do_ref, w_ref, x_ref, nrm_ref, dxo_ref, dx_ref, red_ref):
        @pl.when(pl.program_id(0) == 0)
        def _():
            red_ref[...] = jnp.zeros_like(red_ref)
        dh = _dot_nt(do_ref[...], w_ref[...])
        dx_ref[...] = dxo_ref[...] + _norm_bwd(dh, x_ref[...], nrm_ref, red_ref)

    return _pcall(
        body, name=name, grid=(S // tm,),
        in_specs=[pl.BlockSpec((tm, N), lambda i: (i, 0)),
                  _resident((Dm, N)),
                  pl.BlockSpec((tm, Dm), lambda i: (i, 0)),
                  pl.BlockSpec((8, Dm), lambda i: (0, 0)),
                  pl.BlockSpec((tm, Dm), lambda i: (i, 0))],
        out_specs=[pl.BlockSpec((tm, Dm), lambda i: (i, 0)),
                   pl.BlockSpec((8, Dm), lambda i: (0, 0))],
        out_shape=[jax.ShapeDtypeStruct((S, Dm), F32), jax.ShapeDtypeStruct((8, Dm), F32)],
        sem=("arbitrary",), args=(dout, w, x, nrm, dxo), jobs=jobs)


def _alibi_slope(h):
    return float(2.0 ** (-8.0 * (h + 1) / N_Q_HEADS))


def _head_planes(pair_cols):
    lane = lax.broadcasted_iota(jnp.int32, pair_cols.shape, 1)
    low = lane < HEAD_DIM
    h0_lo = jnp.where(low, pair_cols, 0.0)
    h1_hi = jnp.where(low, 0.0, pair_cols)
    h0_hi = pltpu.roll(h0_lo, HEAD_DIM, 1)
    h1_lo = pltpu.roll(h1_hi, HEAD_DIM, 1)
    return ((h0_lo.astype(BF16), h0_hi.astype(BF16)), (h1_lo.astype(BF16), h1_hi.astype(BF16)))


def _band_geometry(first_block):
    qi = lax.broadcasted_iota(jnp.int32, (BLOCK, BLOCK), 0)
    kj = lax.broadcasted_iota(jnp.int32, (BLOCK, BLOCK), 1)
    own = kj <= qi
    dist = jnp.where(own, qi - kj, qi + BLOCK - kj).astype(F32)
    valid = kj <= qi + BLOCK * (1 - first_block)
    return own, dist, valid


def _fold(band, own):
    return jnp.where(own, band[:, BLOCK:], band[:, :BLOCK])


def _unfold(v, own):
    return jnp.concatenate([jnp.where(own, 0.0, v), jnp.where(own, v, 0.0)], axis=1)


def _softmax_band(s, h, geometry, sink):
    own, dist, valid = geometry
    s = jnp.where(valid, s - _alibi_slope(h) * dist, NEG_INF)
    m = jnp.maximum(jnp.max(s, axis=-1, keepdims=True), sink)
    p = jnp.exp(s - m)
    e_sink = jnp.exp(sink - m)
    inv = 1.0 / (jnp.sum(p, axis=-1, keepdims=True) + e_sink)
    return p * inv, e_sink * inv


def _past(cur, prev, s, row):
    return jnp.where(row < s, pltpu.roll(prev, s, 0), pltpu.roll(cur, s, 0))


def _future(cur, nxt, s, row):
    T = cur.shape[0]
    return jnp.where(row >= T - s, pltpu.roll(nxt, T - s, 0), pltpu.roll(cur, T - s, 0))


def _edge_row(v, last):
    T = v.shape[0]
    r8 = lax.broadcasted_iota(jnp.int32, (SUBLANES, v.shape[1]), 0)
    blk = v[T - SUBLANES:, :] if last else v[:SUBLANES, :]
    return jnp.sum(jnp.where(r8 == (SUBLANES - 1 if last else 0), blk, 0.0), axis=0, keepdims=True)


def _lru_gates(lx, lx_prev, small_ref, wa_ref, wx_ref, row, t0):
    xc = (small_ref[4:5, :] + small_ref[3:4, :] * lx + small_ref[2:3, :] * _past(lx, lx_prev, 1, row)
          + small_ref[1:2, :] * _past(lx, lx_prev, 2, row) + small_ref[0:1, :] * _past(lx, lx_prev, 3, row))
    xcb = xc.astype(BF16)
    r = _sigmoid(_dot(xcb, wa_ref[...]) + small_ref[5:6, :])
    ig = _sigmoid(_dot(xcb, wx_ref[...]) + small_ref[6:7, :])
    sp = _softplus_neg(small_ref[7:8, :])
    la = (-LRU_C) * r * sp
    a = jnp.exp(la)
    first = (row + t0) == 0
    mult = jnp.where(first, 1.0, jnp.sqrt(-_expm1(2.0 * la)))
    return xc, xcb, r, ig, sp, a, mult, first


def _mixer_fwd(x, nrm, w_in, w_out, sinks, small, wa, wx, name, jobs=()):
    S, Dm = x.shape
    T = MIX_TILE
    nT = S // T
    nb = T // BLOCK

    def body(x_ref, nrm_ref, w_in_ref, w_out_ref, sink_ref, small_ref, wa_ref, wx_ref,
             xo_ref, h_ref, proj_ref, y_ref, ymo_ref, hp_ref, *carried_state):
        xv = x_ref[...]
        rstd = lax.rsqrt(jnp.mean(xv * xv, axis=-1, keepdims=True) + EPS)
        hn = (xv * rstd) * nrm_ref[0:1, :]
        hb = (hn * (1.0 + nrm_ref[2:3, :]) + nrm_ref[1:2, :]).astype(BF16)
        h_ref[...] = hb
        proj_ref[...] = _dot(hb, w_in_ref[...])
        core(proj_ref, sink_ref, small_ref, wa_ref, wx_ref, y_ref, hp_ref, *carried_state)
        yo = _dot(y_ref[...], w_out_ref[...])
        xo_ref[...] = xv + nrm_ref[3:4, :] * yo
        ymo_ref[...] = yo.astype(BF16)

    def core(proj_ref, sink_ref, small_ref, wa_ref, wx_ref, y_ref, hp_ref, kvp, lxp, zp, hcar):
        i = pl.program_id(0)

        @pl.when(i == 0)
        def _():
            kvp[...] = jnp.zeros_like(kvp)
            lxp[...] = jnp.zeros_like(lxp)
            zp[...] = jnp.zeros_like(zp)
            hcar[...] = jnp.zeros_like(hcar)

        row = lax.broadcasted_iota(jnp.int32, (T, LRU_WIDTH), 0)

        kv = proj_ref[:, C_KV:C_KV + 2 * KV_WIDTH]
        ext = jnp.concatenate([kvp[...], kv], axis=0)
        kx = _head_planes(ext[:, :KV_WIDTH])
        vx = _head_planes(ext[:, KV_WIDTH:])
        first_tile = jnp.where(i == 0, 1, 0)
        units = [(b, pair, e) for b in range(nb) for pair in range(N_Q_HEADS // 2) for e in range(2)]
        geometry = [_band_geometry(first_tile if b == 0 else 0) for b in range(nb)]
        keys = [slice(b * BLOCK, (b + 2) * BLOCK) for b in range(nb)]
        qp = {(b, pair): (proj_ref[b * BLOCK:(b + 1) * BLOCK, pair * LANES:(pair + 1) * LANES] * 0.125).astype(BF16)
              for b in range(nb) for pair in range(N_Q_HEADS // 2)}
        scores = [_fold(_dot_nt(qp[(b, pair)], kx[pair // 2][e][keys[b]]), geometry[b][0]) for b, pair, e in units]
        probs = [_unfold(_softmax_band(s, 2 * pair + e, geometry[b], sink_ref[2 * pair + e])[0],
                         geometry[b][0]).astype(BF16) for s, (b, pair, e) in zip(scores, units)]
        outs = [_dot(p, vx[pair // 2][e][keys[b]]) for p, (b, pair, e) in zip(probs, units)]
        for u in range(0, len(units), 2):
            b, pair, _ = units[u]
            y_ref[b * BLOCK:(b + 1) * BLOCK, pair * LANES:(pair + 1) * LANES] = (outs[u] + outs[u + 1]).astype(BF16)
        kvp[...] = kv[T - BLOCK:, :]

        lx = proj_ref[:, C_LX:C_LX + LRU_WIDTH]
        xc, _, _, ig, _, a, mult, _ = _lru_gates(lx, lxp[...], small_ref, wa_ref, wx_ref, row, i * T)
        lxp[...] = lx
        aa = a
        bb = mult * (ig * xc)
        s = 1
        while s < T:
            a_sh = jnp.where(row >= s, pltpu.roll(aa, s, 0), 1.0)
            b_sh = jnp.where(row >= s, pltpu.roll(bb, s, 0), 0.0)
            bb = aa * b_sh + bb
            aa = aa * a_sh
            s *= 2
        hc = hcar[0:1, :]
        hh = bb + aa * hc
        hp_ref[...] = jnp.where(row < 1, hc, pltpu.roll(hh, 1, 0))
        hcar[...] = jnp.broadcast_to(_edge_row(hh, True), hcar.shape)
        gl, _ = _gelu(proj_ref[:, C_LG:C_LG + LRU_WIDTH])
        y_ref[:, ATTN_WIDTH:ATTN_WIDTH + LRU_WIDTH] = (gl * hh).astype(BF16)

        z = proj_ref[:, C_SC:C_SC + CONV_WIDTH] * proj_ref[:, C_SX:C_SX + CONV_WIDTH]
        c3 = (small_ref[10:11, :] * z + small_ref[9:10, :] * _past(z, zp[...], 1, row)
              + small_ref[8:9, :] * _past(z, zp[...], 2, row))
        zp[...] = z
        y_ref[:, ATTN_WIDTH + LRU_WIDTH:] = (proj_ref[:, C_SB:C_SB + CONV_WIDTH] * c3).astype(BF16)

    fix = lambda i: (0, 0)
    row = lambda i: (i, 0)
    return _pcall(
        body, name=name, grid=(nT,),
        in_specs=[pl.BlockSpec((T, Dm), row), pl.BlockSpec((8, Dm), fix),
                  _resident((Dm, IN_PROJ_WIDTH)), _resident((D_MODEL, Dm)),
                  pl.BlockSpec(memory_space=pltpu.SMEM),
                  pl.BlockSpec((16, LRU_WIDTH), fix),
                  pl.BlockSpec((LRU_WIDTH, LRU_WIDTH), fix),
                  pl.BlockSpec((LRU_WIDTH, LRU_WIDTH), fix)],
        out_specs=[pl.BlockSpec((T, Dm), row), pl.BlockSpec((T, Dm), row), pl.BlockSpec((T, IN_PROJ_WIDTH), row),
                   pl.BlockSpec((T, D_MODEL), row), pl.BlockSpec((T, Dm), row), pl.BlockSpec((T, LRU_WIDTH), row)],
        out_shape=[jax.ShapeDtypeStruct((S, Dm), F32), jax.ShapeDtypeStruct((S, Dm), BF16),
                   jax.ShapeDtypeStruct((S, IN_PROJ_WIDTH), F32), jax.ShapeDtypeStruct((S, D_MODEL), BF16),
                   jax.ShapeDtypeStruct((S, Dm), BF16), jax.ShapeDtypeStruct((S, LRU_WIDTH), F32)],
        scratch_shapes=[pltpu.VMEM((BLOCK, 2 * KV_WIDTH), F32), pltpu.VMEM((T, LRU_WIDTH), F32),
                        pltpu.VMEM((T, CONV_WIDTH), F32), pltpu.VMEM((SUBLANES, LRU_WIDTH), F32)],
        sem=("arbitrary",), args=(x, nrm, w_in, w_out, sinks, small, wa, wx), jobs=jobs)


def _mixer_bwd(x, nrm, dxo, h, proj, ymix, ymo, hprev, w_in, w_out, sinks, small, wa, wx, name, jobs=()):
    S, Dm = x.shape
    T = MIX_TILE
    nT = S // T
    nb = T // BLOCK
    bpt = T // BLOCK

    def body(x_ref, nrm_ref, dxo_ref, h_ref, proj_ref, kvprev_ref, lxprev_ref, scprev_ref, sxprev_ref, ymix_ref,
             ymo_ref, hp_ref, w_in_ref, w_out_ref, sink_ref, small_ref, wa_ref, wx_ref,
             dx_ref, red_ref, dgate_ref, dwo_ref, dwi_ref, dsm_ref, dsink_ref, dwa_ref, dwx_ref,
             dy_s, dp_s, acc_o, acc_i, *carried_state):
        @pl.when(pl.program_id(0) == 0)
        def _():
            for r in (red_ref, dgate_ref, acc_o, acc_i):
                r[...] = jnp.zeros_like(r)

        dxo_v = dxo_ref[...]
        dyb = (nrm_ref[3:4, :] * dxo_v).astype(BF16)
        dgate_ref[0:1, :] += jnp.sum(ymo_ref[...].astype(F32) * dxo_v, axis=0, keepdims=True)
        dy_s[...] = _dot_nt(dyb, w_out_ref[...])
        acc_o[...] += _dot_tn(ymix_ref[...], dyb)
        core(proj_ref, kvprev_ref, lxprev_ref, scprev_ref, sxprev_ref, dy_s, hp_ref, sink_ref, small_ref,
             wa_ref, wx_ref, dp_s, dsm_ref, dsink_ref, dwa_ref, dwx_ref, *carried_state)
        dpb = dp_s[...]
        acc_i[...] += _dot_tn(h_ref[...], dpb)
        dx_ref[...] = dxo_v + _norm_bwd(_dot_nt(dpb, w_in_ref[...]), x_ref[...], nrm_ref, red_ref)

        @pl.when(pl.program_id(0) == nT - 1)
        def _():
            dwo_ref[...] = acc_o[...].astype(BF16)
            dwi_ref[...] = acc_i[...].astype(BF16)

    def core(proj_ref, kvprev_ref, lxprev_ref, scprev_ref, sxprev_ref, dy_ref, hp_ref, sink_ref, small_ref,
             wa_ref, wx_ref, dp_ref, dsm_ref, dsink_ref, dwa_ref, dwx_ref,
             dk_s, dv_s, dkv_c, dxc_n, dc3_n, p_c):
        i = pl.program_id(0)
        ti = nT - 1 - i
        has_prev = jnp.where(ti == 0, 0.0, 1.0)

        @pl.when(i == 0)
        def _():
            for r in (dkv_c, dxc_n, dc3_n, p_c, dsm_ref, dsink_ref, dwa_ref, dwx_ref):
                r[...] = jnp.zeros_like(r)

        row = lax.broadcasted_iota(jnp.int32, (T, LRU_WIDTH), 0)

        kv = proj_ref[:, C_KV:C_KV + 2 * KV_WIDTH]
        ext = jnp.concatenate([kvprev_ref[...] * has_prev, kv], axis=0)
        kx = _head_planes(ext[:, :KV_WIDTH])
        vx = _head_planes(ext[:, KV_WIDTH:])
        dk_s[...] = jnp.zeros_like(dk_s)
        dv_s[...] = jnp.zeros_like(dv_s)
        dk_s[:, T:] = dkv_c[:, :BLOCK]
        dv_s[:, T:] = dkv_c[:, BLOCK:]
        first_tile = jnp.where(ti == 0, 1, 0)
        units = [(b, pair, e) for b in range(nb) for pair in range(N_Q_HEADS // 2) for e in range(2)]
        geometry = [_band_geometry(first_tile if b == 0 else 0) for b in range(nb)]
        keys = [slice(b * BLOCK, (b + 2) * BLOCK) for b in range(nb)]
        tile = {(b, pair): (slice(b * BLOCK, (b + 1) * BLOCK), slice(pair * LANES, (pair + 1) * LANES))
                for b in range(nb) for pair in range(N_Q_HEADS // 2)}
        qp = {k: (proj_ref[rc] * 0.125).astype(BF16) for k, rc in tile.items()}
        dob = {k: dy_ref[rc].astype(BF16) for k, rc in tile.items()}
        qp_t = {k: jnp.transpose(proj_ref[rc] * 0.125).astype(BF16) for k, rc in tile.items()}
        dob_t = {k: jnp.transpose(dy_ref[rc]).astype(BF16) for k, rc in tile.items()}
        scores = [_fold(_dot_nt(qp[(b, pair)], kx[pair // 2][e][keys[b]]), geometry[b][0]) for b, pair, e in units]
        dprob = [_fold(_dot_nt(dob[(b, pair)], vx[pair // 2][e][keys[b]]), geometry[b][0]) for b, pair, e in units]
        pn_wide, ds_wide = [], []
        for s, dpm, (b, pair, e) in zip(scores, dprob, units):
            h = 2 * pair + e
            own = geometry[b][0]
            pn, psink = _softmax_band(s, h, geometry[b], sink_ref[h])
            dsum = jnp.sum(pn * dpm, axis=-1, keepdims=True)
            dsink_ref[h:h + 1, :] += jnp.full((1, LANES), -1.0, F32) * jnp.sum(psink * dsum)
            pn_wide.append(_unfold(pn, own).astype(BF16))
            ds_wide.append(_unfold(pn * (dpm - dsum), own).astype(BF16))
        dq = {}
        for pw, ds, (b, pair, e) in zip(pn_wide, ds_wide, units):
            g = pair // 2
            head_e = slice(e * HEAD_DIM, (e + 1) * HEAD_DIM)
            head_g = slice(g * HEAD_DIM, (g + 1) * HEAD_DIM)
            dv_s[head_g, keys[b]] += _dot(dob_t[(b, pair)], pw)[head_e, :]
            dk_s[head_g, keys[b]] += _dot(qp_t[(b, pair)], ds)[head_e, :]
            part = _dot(ds, kx[g][e][keys[b]])
            dq[(b, pair)] = part if e == 0 else dq[(b, pair)] + part
        for k, rc in tile.items():
            dp_ref[rc] = (0.125 * dq[k]).astype(BF16)
        dp_ref[:, C_KV:C_KV + KV_WIDTH] = jnp.transpose(dk_s[:, BLOCK:]).astype(BF16)
        dp_ref[:, C_KV + KV_WIDTH:C_KV + 2 * KV_WIDTH] = jnp.transpose(dv_s[:, BLOCK:]).astype(BF16)
        dkv_c[:, :BLOCK] = dk_s[:, :BLOCK]
        dkv_c[:, BLOCK:] = dv_s[:, :BLOCK]

        lx = proj_ref[:, C_LX:C_LX + LRU_WIDTH]
        lxprev = lxprev_ref[...] * has_prev
        xc, xcb, r, ig, sp, a, mult, first = _lru_gates(lx, lxprev, small_ref, wa_ref, wx_ref, row, ti * T)
        hp = hp_ref[...]
        hh = a * hp + mult * (ig * xc)
        lg = proj_ref[:, C_LG:C_LG + LRU_WIDTH]
        gl, th = _gelu(lg)
        dyl = dy_ref[:, ATTN_WIDTH:ATTN_WIDTH + LRU_WIDTH]
        dp_ref[:, C_LG:C_LG + LRU_WIDTH] = (dyl * hh * _gelu_grad(lg, th)).astype(BF16)
        aa = jnp.where(row < T - 1, pltpu.roll(a, T - 1, 0), 1.0)
        bb = dyl * gl
        s = 1
        while s < T:
            a_sh = jnp.where(row < T - s, pltpu.roll(aa, T - s, 0), 1.0)
            b_sh = jnp.where(row < T - s, pltpu.roll(bb, T - s, 0), 0.0)
            bb = bb + aa * b_sh
            aa = aa * a_sh
            s *= 2
        G = bb + aa * p_c[0:1, :]
        p_c[...] = jnp.broadcast_to(_edge_row(a * G, False), p_c.shape)
        da = G * hp
        dmult = G * (ig * xc)
        dig = G * mult * xc
        dxc = G * mult * ig
        dla = da * a + dmult * jnp.where(first, 0.0, -(a * a) / mult)
        dr = dla * ((-LRU_C) * sp)
        lam = small_ref[7:8, :]
        dsm_ref[7:8, :] += jnp.sum(dla * ((-LRU_C) * r), axis=0, keepdims=True) * (-_sigmoid(-lam))
        dpa = dr * r * (1.0 - r)
        dpx = dig * ig * (1.0 - ig)
        dsm_ref[5:6, :] += jnp.sum(dpa, axis=0, keepdims=True)
        dsm_ref[6:7, :] += jnp.sum(dpx, axis=0, keepdims=True)
        dpab = dpa.astype(BF16)
        dpxb = dpx.astype(BF16)
        dwa_ref[...] += _dot_tn(xcb, dpab)
        dwx_ref[...] += _dot_tn(xcb, dpxb)
        dxc = dxc + _dot_nt(dpab, wa_ref[...]) + _dot_nt(dpxb, wx_ref[...])
        dsm_ref[4:5, :] += jnp.sum(dxc, axis=0, keepdims=True)
        dsm_ref[3:4, :] += jnp.sum(dxc * lx, axis=0, keepdims=True)
        for k in range(3):
            dsm_ref[k:k + 1, :] += jnp.sum(dxc * _past(lx, lxprev, 3 - k, row), axis=0, keepdims=True)
        nxt = dxc_n[...]
        dlx = (small_ref[3:4, :] * dxc + small_ref[2:3, :] * _future(dxc, nxt, 1, row)
               + small_ref[1:2, :] * _future(dxc, nxt, 2, row) + small_ref[0:1, :] * _future(dxc, nxt, 3, row))
        dxc_n[...] = dxc
        dp_ref[:, C_LX:C_LX + LRU_WIDTH] = dlx.astype(BF16)

        sc = proj_ref[:, C_SC:C_SC + CONV_WIDTH]
        sx = proj_ref[:, C_SX:C_SX + CONV_WIDTH]
        sb = proj_ref[:, C_SB:C_SB + CONV_WIDTH]
        z = sc * sx
        zprev = (scprev_ref[...] * sxprev_ref[...]) * has_prev
        z1 = _past(z, zprev, 1, row)
        z2 = _past(z, zprev, 2, row)
        c3 = small_ref[10:11, :] * z + small_ref[9:10, :] * z1 + small_ref[8:9, :] * z2
        dys = dy_ref[:, ATTN_WIDTH + LRU_WIDTH:]
        dp_ref[:, C_SB:C_SB + CONV_WIDTH] = (dys * c3).astype(BF16)
        dc3 = dys * sb
        dsm_ref[10:11, :] += jnp.sum(dc3 * z, axis=0, keepdims=True)
        dsm_ref[9:10, :] += jnp.sum(dc3 * z1, axis=0, keepdims=True)
        dsm_ref[8:9, :] += jnp.sum(dc3 * z2, axis=0, keepdims=True)
        nxt3 = dc3_n[...]
        dz = (small_ref[10:11, :] * dc3 + small_ref[9:10, :] * _future(dc3, nxt3, 1, row)
              + small_ref[8:9, :] * _future(dc3, nxt3, 2, row))
        dc3_n[...] = dc3
        dp_ref[:, C_SC:C_SC + CONV_WIDTH] = (dz * sx).astype(BF16)
        dp_ref[:, C_SX:C_SX + CONV_WIDTH] = (dz * sc).astype(BF16)

    fix = lambda i: (0, 0)
    cur = lambda i: (nT - 1 - i, 0)
    prev_cols = lambda cb: (lambda i: (jnp.maximum(nT - 2 - i, 0), cb))
    return _pcall(
        body, name=name, grid=(nT,),
        in_specs=[pl.BlockSpec((T, Dm), cur), pl.BlockSpec((8, Dm), fix), pl.BlockSpec((T, Dm), cur),
                  pl.BlockSpec((T, Dm), cur),
                  pl.BlockSpec((T, IN_PROJ_WIDTH), cur),
                  pl.BlockSpec((BLOCK, 2 * KV_WIDTH),
                               lambda i: (jnp.maximum((nT - 1 - i) * bpt - 1, 0), C_KV // (2 * KV_WIDTH))),
                  pl.BlockSpec((T, LRU_WIDTH), prev_cols(C_LX // LRU_WIDTH)),
                  pl.BlockSpec((T, CONV_WIDTH), prev_cols(C_SC // CONV_WIDTH)),
                  pl.BlockSpec((T, CONV_WIDTH), prev_cols(C_SX // CONV_WIDTH)),
                  pl.BlockSpec((T, D_MODEL), cur), pl.BlockSpec((T, Dm), cur),
                  pl.BlockSpec((T, LRU_WIDTH), cur),
                  _resident((Dm, IN_PROJ_WIDTH)), _resident((D_MODEL, Dm)),
                  pl.BlockSpec(memory_space=pltpu.SMEM),
                  pl.BlockSpec((16, LRU_WIDTH), fix),
                  pl.BlockSpec((LRU_WIDTH, LRU_WIDTH), fix),
                  pl.BlockSpec((LRU_WIDTH, LRU_WIDTH), fix)],
        out_specs=[pl.BlockSpec((T, Dm), cur), pl.BlockSpec((8, Dm), fix), pl.BlockSpec((8, Dm), fix),
                   _resident((D_MODEL, Dm)), _resident((Dm, IN_PROJ_WIDTH)),
                   pl.BlockSpec((16, LRU_WIDTH), fix),
                   pl.BlockSpec((SUBLANES, LANES), fix),
                   pl.BlockSpec((LRU_WIDTH, LRU_WIDTH), fix),
                   pl.BlockSpec((LRU_WIDTH, LRU_WIDTH), fix)],
        out_shape=[jax.ShapeDtypeStruct((S, Dm), F32), jax.ShapeDtypeStruct((8, Dm), F32),
                   jax.ShapeDtypeStruct((8, Dm), F32),
                   jax.ShapeDtypeStruct((D_MODEL, Dm), BF16), jax.ShapeDtypeStruct((Dm, IN_PROJ_WIDTH), BF16),
                   jax.ShapeDtypeStruct((16, LRU_WIDTH), F32),
                   jax.ShapeDtypeStruct((SUBLANES, LANES), F32),
                   jax.ShapeDtypeStruct((LRU_WIDTH, LRU_WIDTH), F32),
                   jax.ShapeDtypeStruct((LRU_WIDTH, LRU_WIDTH), F32)],
        scratch_shapes=[pltpu.VMEM((T, D_MODEL), F32), pltpu.VMEM((T, IN_PROJ_WIDTH), BF16),
                        pltpu.VMEM((D_MODEL, Dm), F32), pltpu.VMEM((Dm, IN_PROJ_WIDTH), F32),
                        pltpu.VMEM((KV_WIDTH, T + BLOCK), F32), pltpu.VMEM((KV_WIDTH, T + BLOCK), F32),
                        pltpu.VMEM((BLOCK, 2 * KV_WIDTH), F32), pltpu.VMEM((T, LRU_WIDTH), F32),
                        pltpu.VMEM((T, CONV_WIDTH), F32), pltpu.VMEM((SUBLANES, LRU_WIDTH), F32)],
        sem=("arbitrary",),
        args=(x, nrm, dxo, h, proj, proj, proj, proj, proj, ymix, ymo, hprev, w_in, w_out, sinks, small, wa, wx),
        jobs=jobs)


def _mod_matmul(c_all, w_mod, name, jobs=()):
    L, Dm, N = w_mod.shape
    R = c_all.shape[0]
    tn = 768

    def body(c_ref, w_ref, o_ref, ca_ref):
        cv = c_ref[...]
        ca = (cv * _sigmoid(cv)).astype(BF16)
        ca_ref[...] = ca
        o_ref[0] = _dot(ca, w_ref[0].astype(BF16))

    return _pcall(
        body, name=name, grid=(L, N // tn),
        in_specs=[pl.BlockSpec((R, Dm), lambda l, n: (0, 0)),
                  pl.BlockSpec((1, Dm, tn), lambda l, n: (l, 0, n))],
        out_specs=[pl.BlockSpec((1, R, tn), lambda l, n: (l, 0, n)), pl.BlockSpec((R, Dm), lambda l, n: (0, 0))],
        out_shape=[jax.ShapeDtypeStruct((L, R, N), F32), jax.ShapeDtypeStruct((R, Dm), BF16)],
        sem=("arbitrary", "arbitrary"), args=(c_all, w_mod), jobs=jobs)


def _adamw_update(g, w_ref, m_ref, v_ref, go_ref, d_ref, mo_ref, vo_ref):
    mn = ADAM_B1 * m_ref[...] + (1.0 - ADAM_B1) * g
    vn = ADAM_B2 * v_ref[...] + (1.0 - ADAM_B2) * (g * g)
    go_ref[...] = g
    mo_ref[...] = mn
    vo_ref[...] = vn
    m_hat = mn / (1.0 - ADAM_B1 ** ADAM_STEP)
    v_hat = vn / (1.0 - ADAM_B2 ** ADAM_STEP)
    d_ref[...] = (-ADAM_LR) * (m_hat / (jnp.sqrt(v_hat) + ADAM_EPS) + ADAM_WD * w_ref[...])


def _adamw(w, g, m, v, name):
    R, C = w.shape
    tr = 8
    for cand in (512, 256, 128, 64, 32, 16, 8):
        if R % cand == 0 and cand * C * 4 <= (1 << 20):
            tr = cand
            break

    def body(w_ref, g_ref, *rest):
        _adamw_update(g_ref[...], w_ref, *rest)

    spec = pl.BlockSpec((tr, C), lambda i: (i, 0))
    return _pcall(body, name=name, grid=(R // tr,), in_specs=[spec] * 4, out_specs=[spec] * 4,
                  out_shape=[jax.ShapeDtypeStruct((R, C), F32)] * 4, sem=("arbitrary",), args=(w, g, m, v))


def _adamw_partials(w, partials, m, v, name):
    nl = len(partials)
    _, R, C = partials[0][0].shape
    tr = 8
    for cand in (256, 128, 64, 32, 16):
        if R % cand == 0 and cand * C * 4 <= (1 << 19):
            tr = cand
            break
    ni = R // tr

    def body(*refs):
        w_ref, p_refs = refs[0], refs[1:1 + 2 * nl]
        m_ref, v_ref, go_ref, d_ref, mo_ref, vo_ref = refs[1 + 2 * nl:]
        for l in range(nl):
            @pl.when(pl.program_id(0) == l)
            def _(pair=p_refs[2 * l:2 * l + 2]):
                own, sib = [((p[0].astype(F32) + p[1].astype(F32)) + p[2].astype(F32)) + p[3].astype(F32)
                            for p in pair]
                _adamw_update(own + sib, w_ref, m_ref, v_ref, go_ref, d_ref, mo_ref, vo_ref)

    def slots(l):
        return pl.BlockSpec((N_CHIPS, tr, C),
                            lambda ll, i: (0, jnp.where(ll == l, i, jnp.where(ll < l, 0, ni - 1)), 0))

    spec = pl.BlockSpec((tr, C), lambda ll, i: (ll * ni + i, 0))
    return pl.pallas_call(
        body, name=name, grid=(nl, ni),
        in_specs=[spec] + [slots(l) for l in range(nl) for _ in range(2)] + [spec, spec], out_specs=[spec] * 4,
        out_shape=[jax.ShapeDtypeStruct((nl * R, C), F32)] * 4,
        compiler_params=_cp("arbitrary", "arbitrary"),
    )(w, *[p for pair in partials for p in pair], m, v)


def _all_gather_small(v, name):
    M, N = v.shape

    def body(x_ref, out_ref, sum_ref, send_sems, recv_sems, local_sem):
        x, y, c = lax.axis_index("x"), lax.axis_index("y"), lax.axis_index("c")
        me, sibling = (x, y, c), (x, y, 1 - c)
        chips = [(1 - x, y), (x, 1 - y), (1 - x, 1 - y)]

        def rows(px, py, pc):
            return out_ref.at[pl.ds(pl.multiple_of((4 * px + 2 * py + pc) * M, SUBLANES), M), :]

        def copy(k, block, to, src=None):
            return pltpu.make_async_remote_copy(
                src_ref=rows(*block) if src is None else src, dst_ref=rows(*block),
                send_sem=send_sems.at[k], recv_sem=recv_sems.at[k], device_id=to, device_id_type=MESH)

        mine = pltpu.make_async_copy(x_ref, rows(*me), local_sem)
        mine.start()
        first = [copy(0, me, sibling, src=x_ref)]
        first += [copy(1 + j, me, (*chip, c), src=x_ref) for j, chip in enumerate(chips)]
        for cp in first:
            cp.start()
        passed = [copy(4 + j, (*chip, c), sibling) for j, chip in enumerate(chips)]
        for j, chip in enumerate(chips):
            copy(1 + j, (*chip, c), me).wait_recv()
            passed[j].start()
        copy(0, sibling, me).wait_recv()
        for j, chip in enumerate(chips):
            copy(4 + j, (*chip, 1 - c), me).wait_recv()
        for cp in first + passed:
            cp.wait_send()
        mine.wait()
        acc = out_ref[0:M, :]
        for d in range(1, N_DEV):
            acc = acc + out_ref[d * M:(d + 1) * M, :]
        sum_ref[...] = acc

    return pl.pallas_call(
        body, name=name,
        out_shape=[jax.ShapeDtypeStruct((N_DEV * M, N), F32), jax.ShapeDtypeStruct((M, N), F32)],
        in_specs=[pl.BlockSpec(memory_space=pltpu.VMEM)],
        out_specs=[pl.BlockSpec(memory_space=pltpu.VMEM), pl.BlockSpec(memory_space=pltpu.VMEM)],
        scratch_shapes=[pltpu.SemaphoreType.DMA((7,)), pltpu.SemaphoreType.DMA((7,)), pltpu.SemaphoreType.DMA],
        compiler_params=pltpu.CompilerParams(vmem_limit_bytes=VMEM_LIMIT),
    )(v)


_BIG = (("w_ffn1_gu", 1), ("w_ffn1_down", 0), ("w_ffn2_gu", 1), ("w_ffn2_down", 0), ("w_in", 1), ("w_out", 0))
_AXIS = dict(_BIG)

_GATHER_PLAN = {
    "first": [(0, "w_ffn1_gu"), (0, "w_ffn1_down")],
    (0, "ffn1"): [(0, "w_in"), (0, "w_out"), (0, "w_ffn2_gu")],
    (0, "mix"): [(0, "w_ffn2_down")],
    (0, "ffn2"): [(1, "w_ffn1_gu"), (1, "w_ffn1_down")],
    (1, "ffn1"): [(1, "w_in"), (1, "w_out"), (1, "w_ffn2_gu")],
    (1, "mix"): [(1, "w_ffn2_down")],
}


def _pack(arrs, rows_multiple=SUBLANES):
    flat = jnp.concatenate([a.astype(F32).reshape(-1) for a in arrs])
    unit = rows_multiple * LANES
    total = -(-flat.shape[0] // unit) * unit
    return jnp.pad(flat, (0, total - flat.shape[0])).reshape(total // LANES, LANES)


def _unpack(flat, shapes):
    out, off = [], 0
    for shp in shapes:
        n = int(math.prod(shp))
        out.append(flat[off:off + n].reshape(shp))
        off += n
    return out


def _block_diag(w):
    out = jnp.zeros((LRU_WIDTH, LRU_WIDTH), F32)
    for h in range(4):
        out = lax.dynamic_update_slice(out, w[h], (h * HEAD_DIM, h * HEAD_DIM))
    return out


def _diag_blocks(w):
    return jnp.stack([w[h * HEAD_DIM:(h + 1) * HEAD_DIM, h * HEAD_DIM:(h + 1) * HEAD_DIM] for h in range(4)])


def _rows8(*rows):
    z = jnp.zeros((8 - len(rows), rows[0].shape[-1]), F32)
    return jnp.concatenate([jnp.stack(rows), z], axis=0)


def kernel(x, c, w_mod, b_mod, g_norm, w_ffn1_gu, w_ffn1_down, w_ffn2_gu, w_ffn2_down, w_in, w_out, attn_sinks, lru_conv_w, lru_conv_b, lru_gate_a_w, lru_gate_a_b, lru_gate_x_w, lru_gate_x_b, lru_lambda, sc_conv_w, g_final, loss_target, m_w_mod, m_b_mod, m_g_norm, m_w_ffn1_gu, m_w_ffn1_down, m_w_ffn2_gu, m_w_ffn2_down, m_w_in, m_w_out, m_attn_sinks, m_lru_conv_w, m_lru_conv_b, m_lru_gate_a_w, m_lru_gate_a_b, m_lru_gate_x_w, m_lru_gate_x_b, m_lru_lambda, m_sc_conv_w, m_g_final, v_w_mod, v_b_mod, v_g_norm, v_w_ffn1_gu, v_w_ffn1_down, v_w_ffn2_gu, v_w_ffn2_down, v_w_in, v_w_out, v_attn_sinks, v_lru_conv_w, v_lru_conv_b, v_lru_gate_a_w, v_lru_gate_a_b, v_lru_gate_x_w, v_lru_gate_x_b, v_lru_lambda, v_sc_conv_w, v_g_final):
    W = dict(w_mod=w_mod, b_mod=b_mod, g_norm=g_norm, w_ffn1_gu=w_ffn1_gu, w_ffn1_down=w_ffn1_down,
             w_ffn2_gu=w_ffn2_gu, w_ffn2_down=w_ffn2_down, w_in=w_in, w_out=w_out, attn_sinks=attn_sinks,
             lru_conv_w=lru_conv_w, lru_conv_b=lru_conv_b, lru_gate_a_w=lru_gate_a_w, lru_gate_a_b=lru_gate_a_b,
             lru_gate_x_w=lru_gate_x_w, lru_gate_x_b=lru_gate_x_b, lru_lambda=lru_lambda, sc_conv_w=sc_conv_w,
             g_final=g_final)
    M1 = dict(w_mod=m_w_mod, b_mod=m_b_mod, g_norm=m_g_norm, w_ffn1_gu=m_w_ffn1_gu, w_ffn1_down=m_w_ffn1_down,
              w_ffn2_gu=m_w_ffn2_gu, w_ffn2_down=m_w_ffn2_down, w_in=m_w_in, w_out=m_w_out,
              attn_sinks=m_attn_sinks, lru_conv_w=m_lru_conv_w, lru_conv_b=m_lru_conv_b,
              lru_gate_a_w=m_lru_gate_a_w, lru_gate_a_b=m_lru_gate_a_b, lru_gate_x_w=m_lru_gate_x_w,
              lru_gate_x_b=m_lru_gate_x_b, lru_lambda=m_lru_lambda, sc_conv_w=m_sc_conv_w, g_final=m_g_final)
    V1 = dict(w_mod=v_w_mod, b_mod=v_b_mod, g_norm=v_g_norm, w_ffn1_gu=v_w_ffn1_gu, w_ffn1_down=v_w_ffn1_down,
              w_ffn2_gu=v_w_ffn2_gu, w_ffn2_down=v_w_ffn2_down, w_in=v_w_in, w_out=v_w_out,
              attn_sinks=v_attn_sinks, lru_conv_w=v_lru_conv_w, lru_conv_b=v_lru_conv_b,
              lru_gate_a_w=v_lru_gate_a_w, lru_gate_a_b=v_lru_gate_a_b, lru_gate_x_w=v_lru_gate_x_w,
              lru_gate_x_b=v_lru_gate_x_b, lru_lambda=v_lru_lambda, sc_conv_w=v_sc_conv_w, g_final=v_g_final)
    names = ["w_mod", "b_mod", "g_norm", "w_ffn1_gu", "w_ffn1_down", "w_ffn2_gu", "w_ffn2_down", "w_in", "w_out",
             "attn_sinks", "lru_conv_w", "lru_conv_b", "lru_gate_a_w", "lru_gate_a_b", "lru_gate_x_w",
             "lru_gate_x_b", "lru_lambda", "sc_conv_w", "g_final"]

    xs = x[0]
    tgt = loss_target[0]
    S = xs.shape[0]
    chip = 2 * lax.axis_index("x") + lax.axis_index("y")
    batch = 2 * chip + lax.axis_index("c")
    L = DEPTH

    fwd_shapes = [(D_MODEL,), g_norm.shape, lru_conv_w.shape, sc_conv_w.shape]
    gathered, _ = _all_gather_small(_pack([c[0], g_norm, lru_conv_w, sc_conv_w]), "gather_small_fwd")
    gathered = gathered.reshape(N_DEV, -1)
    c_all = gathered[:, :D_MODEL]
    per_chip = [_unpack(gathered[2 * jj], fwd_shapes) for jj in range(N_CHIPS)]
    g_norm_full = jnp.concatenate([p[1] for p in per_chip], axis=-1)
    lru_conv_w_full = jnp.concatenate([p[2] for p in per_chip], axis=-1)
    sc_conv_w_full = jnp.concatenate([p[3] for p in per_chip], axis=-1)

    full = {}

    def gather_jobs(key):
        return [_GatherJob(W[n][l].astype(BF16), _AXIS[n]) for l, n in _GATHER_PLAN.get(key, ())]

    def landed(key, outs):
        full.update(zip(_GATHER_PLAN.get(key, ()), outs))

    c_pad = jnp.concatenate([c_all, jnp.zeros_like(c_all)], axis=0)
    (mod_part, c_act), ex = _mod_matmul(c_pad, w_mod, "mod_matmul", gather_jobs("first"))
    landed("first", ex)
    mod_all, _ = _all_gather_small(mod_part.reshape(-1, LANES), "gather_mod")
    mod_all = mod_all.reshape(N_DEV, L, 16, -1)
    mod_rows = [lax.dynamic_index_in_dim(mod_all[2 * jj], batch, axis=1, keepdims=False) for jj in range(N_CHIPS)]
    mod = (jnp.concatenate(mod_rows, axis=-1) + b_mod).reshape(L, 9, D_MODEL)

    def nrm_rows(l, s):
        return _rows8(g_norm_full[l, s], mod[l, 3 * s], mod[l, 3 * s + 1], mod[l, 3 * s + 2])

    def mixer_params(l):
        small = jnp.concatenate([lru_conv_w_full[l], lru_conv_b[l][None], lru_gate_a_b[l][None],
                                 lru_gate_x_b[l][None], lru_lambda[l][None], sc_conv_w_full[l],
                                 jnp.zeros((5, LRU_WIDTH), F32)], axis=0)
        return (attn_sinks[l], small, _block_diag(lru_gate_a_w[l]).astype(BF16),
                _block_diag(lru_gate_x_w[l]).astype(BF16))

    saved = []
    xcur = xs
    for l in range(L):
        n1, n2, n3 = nrm_rows(l, 0), nrm_rows(l, 1), nrm_rows(l, 2)

        def ffn(which, xin, nrm, head=None):
            key = (l, which)
            (xo, h, gu, y, *stats), ex = _ffn_fwd(xin, nrm, full[(l, f"w_{which}_gu")], full[(l, f"w_{which}_down")],
                                                  f"l{l}_{which}", gather_jobs(key), head)
            landed(key, ex)
            return (xo, *stats), (xin, h, gu, y)

        (x1,), s1 = ffn("ffn1", xcur, n1)
        mp = mixer_params(l)
        (x2, h2, proj, ymix, ymo, hprev), ex = _mixer_fwd(x1, n2, full[(l, "w_in")], full[(l, "w_out")], *mp,
                                                          f"l{l}_mix", gather_jobs((l, "mix")))
        landed((l, "mix"), ex)
        s2 = (x1, h2, proj, ymix, ymo, hprev, mp)
        (xcur, *stats), s3 = ffn("ffn2", x2, n3, (_rows8(g_final), tgt) if l == L - 1 else None)
        saved.append((n1, n2, n3, s1, s2, s3))

    dx, stats = xcur, stats[0]
    loss = lax.psum(stats[1, 0], ("x", "y", "c"))
    d_g_final = stats[0]

    recv, theirs = {}, {}
    waiting = []

    def carried(fn, *a, extra=()):
        items = waiting + list(extra)
        waiting.clear()
        outs, landed_now = fn(*a, jobs=[_SiblingJob(recv[(ll, n)]) if g is None else _ScatterJob(g, _AXIS[n])
                                        for ll, n, g in items])
        for (ll, n, g), arr in zip(items, landed_now):
            if g is None:
                theirs[(ll, n)] = arr
            else:
                recv[(ll, n)] = arr
                waiting.append((ll, n, None))
        return outs

    dmod, d_gnorm, d_small = [None] * L, [None] * L, [None] * L
    for l in reversed(range(L)):
        n1, n2, n3, s1, s2, s3 = saved[l]

        def plain(fn, *a):
            return fn(*a)[0]

        def ffn_bwd(which, dxo, sv, nrm, last):
            xin, h, gu, y = sv
            tag = f"l{l}_{which}"
            dgu, dgate, dw_down = carried(
                _ffn_down_bwd, dxo, gu, y, full[(l, f"w_{which}_down")], nrm, tag + "_down_bwd")
            dw_gu = carried(_atb, h, dgu, BF16, 1024, 2816, tag + "_dw_gu", extra=[(l, f"w_{which}_down", dw_down)])
            mine = [(l, f"w_{which}_gu", dw_gu)]
            dxi, red = (carried if last else plain)(
                _nt_norm_bwd, dgu, full[(l, f"w_{which}_gu")], xin, nrm, dxo, tag + "_gu_bwd",
                **(dict(extra=mine) if last else {}))
            if not last:
                waiting.extend(mine)
            return dxi, (red[0], red[1], dgate[0]), red[2]

        dx, dm3, dg3 = ffn_bwd("ffn2", dx, s3, n3, False)
        x_in, h2, proj, ymix, ymo, hprev, mp = s2
        dx, red, dgate, dw_out, dw_in, dsm, dsink, dwa, dwx = carried(
            _mixer_bwd, x_in, n2, dx, h2, proj, ymix, ymo, hprev, full[(l, "w_in")], full[(l, "w_out")], *mp,
            f"l{l}_mix_bwd")
        waiting.extend([(l, "w_out", dw_out), (l, "w_in", dw_in)])
        dm2, dg2 = (red[0], red[1], dgate[0]), red[2]
        dx, dm1, dg1 = ffn_bwd("ffn1", dx, s1, n1, l == 0)
        dmod[l] = jnp.stack(list(dm1) + list(dm2) + list(dm3))
        d_gnorm[l] = jnp.stack([dg1, dg2, dg3])
        d_small[l] = (dsink[:, 0], dsm[0:4], dsm[4], _diag_blocks(dwa), dsm[5], _diag_blocks(dwx), dsm[6],
                      dsm[7], dsm[8:11])
    grad_x = dx[None]

    def both(k):
        return jnp.stack([d_small[0][k], d_small[1][k]])
    small_names = ["g_norm", "attn_sinks", "lru_conv_w", "lru_conv_b", "lru_gate_a_w", "lru_gate_a_b",
                   "lru_gate_x_w", "lru_gate_x_b", "lru_lambda", "sc_conv_w", "g_final"]
    small_parts = [jnp.stack(d_gnorm)] + [both(k) for k in range(9)] + [d_g_final]
    dmod_flat = jnp.stack(dmod).reshape(-1)
    bwd_gathered, bwd_sum = _all_gather_small(_pack([dmod_flat] + small_parts), "gather_small_bwd")
    n_mod = dmod_flat.shape[0]
    dmod_all = bwd_gathered.reshape(N_DEV, -1)[:, :n_mod].reshape(N_DEV, L, 9 * D_MODEL)
    bwd_sum = bwd_sum.reshape(-1)
    G = {"b_mod": bwd_sum[:n_mod].reshape(L, 9 * D_MODEL)}
    G.update(zip(small_names, _unpack(bwd_sum[n_mod:], [p.shape for p in small_parts])))
    for n in ("g_norm", "lru_conv_w", "sc_conv_w"):
        wdt = W[n].shape[-1]
        G[n] = lax.dynamic_slice_in_dim(G[n], chip * wdt, wdt, axis=G[n].ndim - 1)

    ncol = w_mod.shape[-1]
    dmod_cols = lax.dynamic_slice_in_dim(dmod_all, chip * ncol, ncol, axis=2)
    zeros8 = jnp.zeros((N_DEV, ncol), F32)
    g_w_mod = jnp.stack([carried(_atb, c_act, jnp.concatenate([dmod_cols[:, l], zeros8], axis=0).astype(BF16), F32,
                                 D_MODEL, 768, f"l{l}_dw_mod") for l in range(L)])

    out_g, out_d, out_m, out_v = {}, {}, {}, {}
    res, _ = _adamw(w_mod.reshape(-1, ncol), g_w_mod.reshape(-1, ncol), m_w_mod.reshape(-1, ncol),
                    v_w_mod.reshape(-1, ncol), "adamw_w_mod")
    out_g["w_mod"], out_d["w_mod"], out_m["w_mod"], out_v["w_mod"] = [r.reshape(w_mod.shape) for r in res]
    for n, _ in _BIG:
        shp = W[n].shape
        flat = (shp[0] * shp[1], shp[2])
        res = _adamw_partials(W[n].reshape(flat), [(recv[(l, n)], theirs[(l, n)]) for l in range(L)],
                              M1[n].reshape(flat), V1[n].reshape(flat), f"adamw_{n}")
        out_g[n], out_d[n], out_m[n], out_v[n] = [r.reshape(shp) for r in res]
    rest = ["b_mod"] + small_names
    shapes = [W[n].shape for n in rest]
    res, _ = _adamw(_pack([W[n] for n in rest]), _pack([G[n] for n in rest]), _pack([M1[n] for n in rest]),
                    _pack([V1[n] for n in rest]), "adamw_small")
    for dst, r in zip((out_g, out_d, out_m, out_v), res):
        dst.update(zip(rest, _unpack(r.reshape(-1), shapes)))

    return (loss, grad_x, *[out_g[n] for n in names], *[out_d[n] for n in names],
            *[out_m[n] for n in names], *[out_v[n] for n in names])
```

```python
import math

import jax
import jax.numpy as jnp
from jax import lax
from jax.experimental import pallas as pl
from jax.experimental.pallas import tpu as pltpu

F32 = jnp.float32
BF16 = jnp.bfloat16

D_MODEL = 1024
DEPTH = 2
HEAD_DIM = 64
N_Q_HEADS = 8
ATTN_WIDTH = 512
KV_WIDTH = 128
LRU_WIDTH = 256
CONV_WIDTH = 256
IN_PROJ_WIDTH = 2048
BLOCK = 128
D_FF = 2816
EPS = 1e-6
NEG_INF = -1e30
LRU_C = 8.0
N_CHIPS = 4
N_DEV = 8

C_Q, C_KV, C_LX, C_LG, C_SB, C_SC, C_SX = 0, 512, 768, 1024, 1280, 1536, 1792

ADAM_LR = 0.001
ADAM_B1 = 0.9
ADAM_B2 = 0.999
ADAM_EPS = 1e-08
ADAM_WD = 0.01
ADAM_STEP = 10

LANES = 128
SUBLANES = 8
VMEM_LIMIT = 56 * 1024 * 1024
MIX_TILE = 256

MESH = pl.DeviceIdType.MESH


def _cp(*sem):
    return pltpu.CompilerParams(dimension_semantics=sem, vmem_limit_bytes=VMEM_LIMIT)


def _tile(n, pref):
    t = min(n, pref)
    while n % t:
        t //= 2
    return t


MXU_DIM = 256


def _resident(shape):
    return pl.BlockSpec(shape, lambda *_: (0, 0), pipeline_mode=pl.Buffered(1))


def _sigmoid(v):
    return 1.0 / (1.0 + jnp.exp(-v))


def _expm1(v):
    series = v * (1.0 + v * (0.5 + v * (1.0 / 6.0)))
    return jnp.where(v > -0.01, series, jnp.exp(v) - 1.0)


def _softplus_neg(lam):
    e = jnp.exp(-jnp.abs(lam))
    log1p = jnp.where(e < 1e-2, e * (1.0 - e * (0.5 - e * (1.0 / 3.0))), jnp.log(1.0 + e))
    return jnp.maximum(-lam, 0.0) + log1p


_GELU_K = math.sqrt(2.0 / math.pi)
_GELU_C = 0.044715


def _gelu(v):
    t = jnp.tanh(_GELU_K * (v + _GELU_C * v * v * v))
    return 0.5 * v * (1.0 + t), t


def _gelu_grad(v, t):
    return 0.5 * (1.0 + t) + 0.5 * v * (1.0 - t * t) * _GELU_K * (1.0 + 3.0 * _GELU_C * v * v)


def _dot(a, b):
    return jnp.dot(a, b, preferred_element_type=F32)


def _dot_nt(a, b):
    return lax.dot_general(a, b, (((1,), (1,)), ((), ())), preferred_element_type=F32)


def _dot_tn(a, b):
    return lax.dot_general(a, b, (((0,), (0,)), ((), ())), preferred_element_type=F32)


def _window(ref, axis, j, width):
    start = pl.multiple_of(j * width, LANES if axis == 1 else 16)
    if axis == 1:
        return ref.at[:, pl.ds(start, width)]
    return ref.at[pl.ds(start, width), :]


def _chip_peers():
    x, y, c = lax.axis_index("x"), lax.axis_index("y"), lax.axis_index("c")
    return x, y, c, [(1 - x, y), (x, 1 - y), (1 - x, 1 - y)]


class _GatherJob:
    def __init__(self, shard, axis):
        self.src, self.axis, self.width, self.half = shard, axis, shard.shape[axis], shard.shape[0] // 2
        full = tuple(d * N_CHIPS if k == axis else d for k, d in enumerate(shard.shape))
        self.out_shape = jax.ShapeDtypeStruct(full, shard.dtype)

    def _piece(self, ref, j, hf):
        if self.axis == 1:
            return ref.at[pl.ds(pl.multiple_of(hf * self.half, 16), self.half),
                          pl.ds(pl.multiple_of(j * self.width, LANES), self.width)]
        return ref.at[pl.ds(pl.multiple_of(j * self.width + hf * self.half, 16), self.half), :]

    def _copies(self, src, dst, send, recv, loc, t):
        x, y, c, chips = _chip_peers()
        j = 2 * x + y
        owners = [2 * px + py for px, py in chips]
        local = pltpu.make_async_copy(src, _window(dst, self.axis, j, self.width), loc.at[t])
        mine = src.at[pl.ds(pl.multiple_of(c * self.half, 16), self.half), :]

        def ici(k, owner):
            return pltpu.make_async_remote_copy(
                src_ref=mine, dst_ref=self._piece(dst, owner, c), send_sem=send.at[JOB_SEMS * t + k],
                recv_sem=recv.at[JOB_SEMS * t + k], device_id=(*chips[k], c), device_id_type=MESH)

        def relay(k, hf):
            piece = self._piece(dst, owners[k], hf)
            return pltpu.make_async_remote_copy(
                src_ref=piece, dst_ref=piece, send_sem=send.at[JOB_SEMS * t + 4 + k],
                recv_sem=recv.at[JOB_SEMS * t + 4 + k],
                device_id=(x, y, 1 - c), device_id_type=MESH)

        return (local, [ici(k, j) for k in range(3)], [ici(k, owners[k]) for k in range(3)],
                [relay(k, c) for k in range(3)], [relay(k, 1 - c) for k in range(3)])

    def start(self, *a):
        local, ici_out, _, _, _ = self._copies(*a)
        local.start()
        for cp in ici_out:
            cp.start()

    def relay(self, *a):
        _, _, ici_in, relay_out, _ = self._copies(*a)
        for arrived, onward in zip(ici_in, relay_out):
            arrived.wait_recv()
            onward.start()

    def finish(self, *a):
        local, ici_out, _, relay_out, relay_in = self._copies(*a)
        for cp in relay_in:
            cp.wait_recv()
        for cp in ici_out + relay_out:
            cp.wait_send()
        local.wait()


class _ScatterJob:
    def __init__(self, full, axis):
        self.src, self.axis, self.width = full, axis, full.shape[axis] // N_CHIPS
        shard = tuple(self.width if k == axis else d for k, d in enumerate(full.shape))
        self.out_shape = jax.ShapeDtypeStruct((N_CHIPS,) + shard, full.dtype)

    def _copies(self, src, dst, send, recv, loc, t):
        x, y, c, chips = _chip_peers()
        local = pltpu.make_async_copy(_window(src, self.axis, 2 * x + y, self.width), dst.at[3], loc.at[t])
        sends = [pltpu.make_async_remote_copy(
            src_ref=_window(src, self.axis, 2 * px + py, self.width), dst_ref=dst.at[k],
            send_sem=send.at[JOB_SEMS * t + k], recv_sem=recv.at[JOB_SEMS * t + k], device_id=(px, py, c),
            device_id_type=MESH) for k, (px, py) in enumerate(chips)]
        return local, sends

    def start(self, *a):
        local, sends = self._copies(*a)
        local.start()
        for cp in sends:
            cp.start()

    def relay(self, *a):
        pass

    def finish(self, *a):
        local, sends = self._copies(*a)
        for cp in sends:
            cp.wait_recv()
        for cp in sends:
            cp.wait_send()
        local.wait()


class _SiblingJob:
    def __init__(self, arr):
        self.src, self.out_shape = arr, jax.ShapeDtypeStruct(arr.shape, arr.dtype)

    def _copy(self, src, dst, send, recv, loc, t):
        x, y, c = lax.axis_index("x"), lax.axis_index("y"), lax.axis_index("c")
        return pltpu.make_async_remote_copy(
            src_ref=src, dst_ref=dst, send_sem=send.at[JOB_SEMS * t], recv_sem=recv.at[JOB_SEMS * t],
            device_id=(x, y, 1 - c), device_id_type=MESH)

    def start(self, *a):
        self._copy(*a).start()

    def relay(self, *a):
        pass

    def finish(self, *a):
        self._copy(*a).wait()


JOB_SEMS = 8


def _run_jobs(phase, jobs, srcs, dsts, sems):
    for t, job in enumerate(jobs):
        getattr(job, phase)(srcs[t], dsts[t], *sems, t)


def _job_scratch(n):
    return [pltpu.SemaphoreType.DMA((JOB_SEMS * n,)), pltpu.SemaphoreType.DMA((JOB_SEMS * n,)),
            pltpu.SemaphoreType.DMA((n,))]


def _pcall(body, *, name, grid, in_specs, out_specs, out_shape, sem, args, scratch_shapes=(), jobs=()):
    in_specs, out_specs, out_shape = list(in_specs), list(out_specs), list(out_shape)
    scratch_shapes = list(scratch_shapes)
    if not jobs:
        res = pl.pallas_call(body, name=name, grid=grid, in_specs=in_specs, out_specs=out_specs, out_shape=out_shape,
                             scratch_shapes=scratch_shapes, compiler_params=_cp(*sem))(*args)
        return list(res), []
    n_in, n_out, n_scr, nj = len(args), len(out_shape), len(scratch_shapes), len(jobs)
    n_steps = math.prod(grid)
    relay_step = (3 * n_steps) // 4
    relay_early = 0 < relay_step < n_steps - 1

    def wrapped(*refs):
        ins, refs = refs[:n_in], refs[n_in:]
        jin, refs = refs[:nj], refs[nj:]
        outs, refs = refs[:n_out], refs[n_out:]
        jout, refs = refs[:nj], refs[nj:]
        scr, sems = refs[:n_scr], refs[n_scr:]
        step = pl.program_id(0)
        for d in range(1, len(grid)):
            step = step * grid[d] + pl.program_id(d)

        @pl.when(step == 0)
        def _():
            _run_jobs("start", jobs, jin, jout, sems)

        if relay_early:
            @pl.when(step == relay_step)
            def _():
                _run_jobs("relay", jobs, jin, jout, sems)
        body(*ins, *outs, *scr)

        @pl.when(step == n_steps - 1)
        def _():
            if not relay_early:
                _run_jobs("relay", jobs, jin, jout, sems)
            _run_jobs("finish", jobs, jin, jout, sems)

    hbm = pl.BlockSpec(memory_space=pltpu.HBM)
    res = pl.pallas_call(
        wrapped, name=name, grid=grid, in_specs=in_specs + [hbm] * nj, out_specs=out_specs + [hbm] * nj,
        out_shape=out_shape + [job.out_shape for job in jobs], scratch_shapes=scratch_shapes + _job_scratch(nj),
        compiler_params=_cp(*sem))(*args, *[job.src for job in jobs])
    return list(res[:n_out]), list(res[n_out:])


def _hidden_chunks(k):
    return [(c0, min(6 * MXU_DIM, k - c0)) for c0 in range(0, k, 6 * MXU_DIM)]


def _loss_head(xv, gain, tgt, st_ref):
    dm = xv.shape[-1]
    rstd = lax.rsqrt(jnp.mean(xv * xv, axis=-1, keepdims=True) + EPS)
    xn = xv * rstd
    err = xn * gain - tgt
    st_ref[1:2, :] += jnp.full((1, dm), 0.5 / dm, F32) * jnp.sum(err * err)
    dy = err * (1.0 / dm)
    st_ref[0:1, :] += jnp.sum(dy * xn, axis=0, keepdims=True)
    dxn = dy * gain
    return rstd * (dxn - xn * jnp.mean(dxn * xn, axis=-1, keepdims=True))


def _ffn_fwd(x, nrm, w_gu, w_down, name, jobs=(), head=None):
    S, Dm = x.shape
    K = w_down.shape[0]
    tm = _tile(S, 256)

    def body(x_ref, nrm_ref, wgu_ref, wdn_ref, *rest):
        if head is None:
            o_ref, h_ref, gu_ref, y_ref = rest
        else:
            gf_ref, t_ref, o_ref, h_ref, gu_ref, y_ref, st_ref = rest

            @pl.when(pl.program_id(0) == 0)
            def _():
                st_ref[...] = jnp.zeros_like(st_ref)
        xv = x_ref[...]
        rstd = lax.rsqrt(jnp.mean(xv * xv, axis=-1, keepdims=True) + EPS)
        hn = (xv * rstd) * nrm_ref[0:1, :]
        hb = (hn * (1.0 + nrm_ref[2:3, :]) + nrm_ref[1:2, :]).astype(BF16)
        h_ref[...] = hb
        y = jnp.zeros((tm, Dm), F32)
        for c0, cs in _hidden_chunks(K):
            g = _dot(hb, wgu_ref[:, c0:c0 + cs])
            u = _dot(hb, wgu_ref[:, K + c0:K + c0 + cs])
            gu_ref[:, c0:c0 + cs] = g.astype(BF16)
            gu_ref[:, K + c0:K + c0 + cs] = u.astype(BF16)
            y = y + _dot((g * _sigmoid(g) * u).astype(BF16), wdn_ref[c0:c0 + cs, :])
        xo = xv + (0.5 * nrm_ref[3:4, :]) * y
        o_ref[...] = xo if head is None else _loss_head(xo, gf_ref[0:1, :], t_ref[...], st_ref)
        y_ref[...] = y.astype(BF16)

    row = lambda i: (i, 0)
    fix = lambda i: (0, 0)
    in_specs = [pl.BlockSpec((tm, Dm), row), pl.BlockSpec((8, Dm), fix), _resident((Dm, 2 * K)), _resident((K, Dm))]
    out_specs = [pl.BlockSpec((tm, Dm), row), pl.BlockSpec((tm, Dm), row), pl.BlockSpec((tm, 2 * K), row),
                 pl.BlockSpec((tm, Dm), row)]
    out_shape = [jax.ShapeDtypeStruct((S, Dm), F32), jax.ShapeDtypeStruct((S, Dm), BF16),
                 jax.ShapeDtypeStruct((S, 2 * K), BF16), jax.ShapeDtypeStruct((S, Dm), BF16)]
    args = (x, nrm, w_gu, w_down)
    if head is not None:
        in_specs += [pl.BlockSpec((8, Dm), fix), pl.BlockSpec((tm, Dm), row)]
        out_specs.append(pl.BlockSpec((8, Dm), fix))
        out_shape.append(jax.ShapeDtypeStruct((8, Dm), F32))
        args += tuple(head)
    return _pcall(body, name=name, grid=(S // tm,), in_specs=in_specs, out_specs=out_specs, out_shape=out_shape,
                  sem=("arbitrary",), args=args, jobs=jobs)


def _ffn_down_bwd(dxo, gu, y, w, nrm, name, jobs=()):
    S, Dm = dxo.shape
    K = w.shape[0]
    Ka = gu.shape[1]
    coef = 0.5
    tm = _tile(S, 256)
    n_steps = S // tm
    chunks = _hidden_chunks(K)

    def body(dxo_ref, y_ref, w_ref, nrm_ref, a_ref, da_ref, dgate_ref, dw_ref, acc):
        @pl.when(pl.program_id(0) == 0)
        def _():
            dgate_ref[...] = jnp.zeros_like(dgate_ref)
            acc[...] = jnp.zeros_like(acc)

        dxo_v = dxo_ref[...]
        dyb = ((coef * nrm_ref[3:4, :]) * dxo_v).astype(BF16)
        dgate_ref[0:1, :] += jnp.sum(coef * y_ref[...].astype(F32) * dxo_v, axis=0, keepdims=True)
        for c0, cs in chunks:
            dact = _dot_nt(dyb, w_ref[c0:c0 + cs, :])
            g = a_ref[:, c0:c0 + cs].astype(F32)
            u = a_ref[:, K + c0:K + c0 + cs].astype(F32)
            s = _sigmoid(g)
            si = g * s
            da_ref[:, c0:c0 + cs] = (dact * u * (s * (1.0 + g * (1.0 - s)))).astype(BF16)
            da_ref[:, K + c0:K + c0 + cs] = (dact * si).astype(BF16)
            acc[c0:c0 + cs, :] += _dot_tn((si * u).astype(BF16), dyb)

        @pl.when(pl.program_id(0) == n_steps - 1)
        def _():
            dw_ref[...] = acc[...].astype(BF16)

    row = lambda i: (i, 0)
    fix = lambda i: (0, 0)
    return _pcall(
        body, name=name, grid=(n_steps,),
        in_specs=[pl.BlockSpec((tm, Dm), row), pl.BlockSpec((tm, Dm), row), _resident((K, Dm)),
                  pl.BlockSpec((8, Dm), fix), pl.BlockSpec((tm, Ka), row)],
        out_specs=[pl.BlockSpec((tm, Ka), row), pl.BlockSpec((8, Dm), fix), _resident((K, Dm))],
        out_shape=[jax.ShapeDtypeStruct((S, Ka), BF16), jax.ShapeDtypeStruct((8, Dm), F32),
                   jax.ShapeDtypeStruct((K, Dm), BF16)],
        scratch_shapes=[pltpu.VMEM((K, Dm), F32)],
        sem=("arbitrary",), args=(dxo, y, w, nrm, gu), jobs=jobs)


def _atb(a, b, out_dtype, bm, bn, name, jobs=()):
    S, M = a.shape
    N = b.shape[1]
    bk = _tile(S, 1024)
    nk = S // bk

    def body(a_ref, b_ref, o_ref, acc):
        k = pl.program_id(2)

        @pl.when(k == 0)
        def _():
            acc[...] = jnp.zeros_like(acc)
        acc[...] += _dot_tn(a_ref[...], b_ref[...])

        @pl.when(k == nk - 1)
        def _():
            o_ref[...] = acc[...].astype(o_ref.dtype)

    (out,), extra = _pcall(
        body, name=name, grid=(M // bm, N // bn, nk),
        in_specs=[pl.BlockSpec((bk, bm), lambda m, n, k: (k, m)),
                  pl.BlockSpec((bk, bn), lambda m, n, k: (k, n))],
        out_specs=[pl.BlockSpec((bm, bn), lambda m, n, k: (m, n))],
        out_shape=[jax.ShapeDtypeStruct((M, N), out_dtype)],
        scratch_shapes=[pltpu.VMEM((bm, bn), F32)],
        sem=("arbitrary", "arbitrary", "arbitrary"), args=(a, b), jobs=jobs)
    return out, extra


def _norm_bwd(dh, xv, nrm_ref, red_ref):
    rstd = lax.rsqrt(jnp.mean(xv * xv, axis=-1, keepdims=True) + EPS)
    xn = xv * rstd
    gain = nrm_ref[0:1, :]
    hn = xn * gain
    dhn = dh * (1.0 + nrm_ref[2:3, :])
    red_ref[0:1, :] += jnp.sum(dh, axis=0, keepdims=True)
    red_ref[1:2, :] += jnp.sum(dh * hn, axis=0, keepdims=True)
    red_ref[2:3, :] += jnp.sum(dhn * xn, axis=0, keepdims=True)
    dxn = dhn * gain
    return rstd * (dxn - xn * jnp.mean(dxn * xn, axis=-1, keepdims=True))


def _nt_norm_bwd(dout, w, x, nrm, dxo, name, jobs=()):
    S, N = dout.shape
    Dm = w.shape[0]
    tm = _tile(S, 512)

    def body(do_ref, w_ref, x_ref, nrm_ref, dxo_ref, dx_ref, red_ref):
        @pl.when(pl.program_id(0) == 0)
        def _():
            red_ref[...] = jnp.zeros_like(red_ref)
        dh = _dot_nt(do_ref[...], w_ref[...])
        dx_ref[...] = dxo_ref[...] + _norm_bwd(dh, x_ref[...], nrm_ref, red_ref)

    return _pcall(
        body, name=name, grid=(S // tm,),
        in_specs=[pl.BlockSpec((tm, N), lambda i: (i, 0)),
                  _resident((Dm, N)),
                  pl.BlockSpec((tm, Dm), lambda i: (i, 0)),
                  pl.BlockSpec((8, Dm), lambda i: (0, 0)),
                  pl.BlockSpec((tm, Dm), lambda i: (i, 0))],
        out_specs=[pl.BlockSpec((tm, Dm), lambda i: (i, 0)),
                   pl.BlockSpec((8, Dm), lambda i: (0, 0))],
        out_shape=[jax.ShapeDtypeStruct((S, Dm), F32), jax.ShapeDtypeStruct((8, Dm), F32)],
        sem=("arbitrary",), args=(dout, w, x, nrm, dxo), jobs=jobs)


def _alibi_slope(h):
    return float(2.0 ** (-8.0 * (h + 1) / N_Q_HEADS))


def _head_planes(pair_cols):
    lane = lax.broadcasted_iota(jnp.int32, pair_cols.shape, 1)
    low = lane < HEAD_DIM
    h0_lo = jnp.where(low, pair_cols, 0.0)
    h1_hi = jnp.where(low, 0.0, pair_cols)
    h0_hi = pltpu.roll(h0_lo, HEAD_DIM, 1)
    h1_lo = pltpu.roll(h1_hi, HEAD_DIM, 1)
    return ((h0_lo.astype(BF16), h0_hi.astype(BF16)), (h1_lo.astype(BF16), h1_hi.astype(BF16)))


def _band_geometry(first_block):
    qi = lax.broadcasted_iota(jnp.int32, (BLOCK, BLOCK), 0)
    kj = lax.broadcasted_iota(jnp.int32, (BLOCK, BLOCK), 1)
    own = kj <= qi
    dist = jnp.where(own, qi - kj, qi + BLOCK - kj).astype(F32)
    valid = kj <= qi + BLOCK * (1 - first_block)
    return own, dist, valid


def _fold(band, own):
    return jnp.where(own, band[:, BLOCK:], band[:, :BLOCK])


def _unfold(v, own):
    return jnp.concatenate([jnp.where(own, 0.0, v), jnp.where(own, v, 0.0)], axis=1)


def _softmax_band(s, h, geometry, sink):
    own, dist, valid = geometry
    s = jnp.where(valid, s - _alibi_slope(h) * dist, NEG_INF)
    m = jnp.maximum(jnp.max(s, axis=-1, keepdims=True), sink)
    p = jnp.exp(s - m)
    e_sink = jnp.exp(sink - m)
    inv = 1.0 / (jnp.sum(p, axis=-1, keepdims=True) + e_sink)
    return p * inv, e_sink * inv


def _past(cur, prev, s, row):
    return jnp.where(row < s, pltpu.roll(prev, s, 0), pltpu.roll(cur, s, 0))


def _future(cur, nxt, s, row):
    T = cur.shape[0]
    return jnp.where(row >= T - s, pltpu.roll(nxt, T - s, 0), pltpu.roll(cur, T - s, 0))


def _edge_row(v, last):
    T = v.shape[0]
    r8 = lax.broadcasted_iota(jnp.int32, (SUBLANES, v.shape[1]), 0)
    blk = v[T - SUBLANES:, :] if last else v[:SUBLANES, :]
    return jnp.sum(jnp.where(r8 == (SUBLANES - 1 if last else 0), blk, 0.0), axis=0, keepdims=True)


def _lru_gates(lx, lx_prev, small_ref, wa_ref, wx_ref, row, t0):
    xc = (small_ref[4:5, :] + small_ref[3:4, :] * lx + small_ref[2:3, :] * _past(lx, lx_prev, 1, row)
          + small_ref[1:2, :] * _past(lx, lx_prev, 2, row) + small_ref[0:1, :] * _past(lx, lx_prev, 3, row))
    xcb = xc.astype(BF16)
    r = _sigmoid(_dot(xcb, wa_ref[...]) + small_ref[5:6, :])
    ig = _sigmoid(_dot(xcb, wx_ref[...]) + small_ref[6:7, :])
    sp = _softplus_neg(small_ref[7:8, :])
    la = (-LRU_C) * r * sp
    a = jnp.exp(la)
    first = (row + t0) == 0
    mult = jnp.where(first, 1.0, jnp.sqrt(-_expm1(2.0 * la)))
    return xc, xcb, r, ig, sp, a, mult, first


def _mixer_fwd(x, nrm, w_in, w_out, sinks, small, wa, wx, name, jobs=()):
    S, Dm = x.shape
    T = MIX_TILE
    nT = S // T
    nb = T // BLOCK

    def body(x_ref, nrm_ref, w_in_ref, w_out_ref, sink_ref, small_ref, wa_ref, wx_ref,
             xo_ref, h_ref, proj_ref, y_ref, ymo_ref, hp_ref, *carried_state):
        xv = x_ref[...]
        rstd = lax.rsqrt(jnp.mean(xv * xv, axis=-1, keepdims=True) + EPS)
        hn = (xv * rstd) * nrm_ref[0:1, :]
        hb = (hn * (1.0 + nrm_ref[2:3, :]) + nrm_ref[1:2, :]).astype(BF16)
        h_ref[...] = hb
        proj_ref[...] = _dot(hb, w_in_ref[...])
        core(proj_ref, sink_ref, small_ref, wa_ref, wx_ref, y_ref, hp_ref, *carried_state)
        yo = _dot(y_ref[...], w_out_ref[...])
        xo_ref[...] = xv + nrm_ref[3:4, :] * yo
        ymo_ref[...] = yo.astype(BF16)

    def core(proj_ref, sink_ref, small_ref, wa_ref, wx_ref, y_ref, hp_ref, kvp, lxp, zp, hcar):
        i = pl.program_id(0)

        @pl.when(i == 0)
        def _():
            kvp[...] = jnp.zeros_like(kvp)
            lxp[...] = jnp.zeros_like(lxp)
            zp[...] = jnp.zeros_like(zp)
            hcar[...] = jnp.zeros_like(hcar)

        row = lax.broadcasted_iota(jnp.int32, (T, LRU_WIDTH), 0)

        kv = proj_ref[:, C_KV:C_KV + 2 * KV_WIDTH]
        ext = jnp.concatenate([kvp[...], kv], axis=0)
        kx = _head_planes(ext[:, :KV_WIDTH])
        vx = _head_planes(ext[:, KV_WIDTH:])
        first_tile = jnp.where(i == 0, 1, 0)
        units = [(b, pair, e) for b in range(nb) for pair in range(N_Q_HEADS // 2) for e in range(2)]
        geometry = [_band_geometry(first_tile if b == 0 else 0) for b in range(nb)]
        keys = [slice(b * BLOCK, (b + 2) * BLOCK) for b in range(nb)]
        qp = {(b, pair): (proj_ref[b * BLOCK:(b + 1) * BLOCK, pair * LANES:(pair + 1) * LANES] * 0.125).astype(BF16)
              for b in range(nb) for pair in range(N_Q_HEADS // 2)}
        scores = [_fold(_dot_nt(qp[(b, pair)], kx[pair // 2][e][keys[b]]), geometry[b][0]) for b, pair, e in units]
        probs = [_unfold(_softmax_band(s, 2 * pair + e, geometry[b], sink_ref[2 * pair + e])[0],
                         geometry[b][0]).astype(BF16) for s, (b, pair, e) in zip(scores, units)]
        outs = [_dot(p, vx[pair // 2][e][keys[b]]) for p, (b, pair, e) in zip(probs, units)]
        for u in range(0, len(units), 2):
            b, pair, _ = units[u]
            y_ref[b * BLOCK:(b + 1) * BLOCK, pair * LANES:(pair + 1) * LANES] = (outs[u] + outs[u + 1]).astype(BF16)
        kvp[...] = kv[T - BLOCK:, :]

        lx = proj_ref[:, C_LX:C_LX + LRU_WIDTH]
        xc, _, _, ig, _, a, mult, _ = _lru_gates(lx, lxp[...], small_ref, wa_ref, wx_ref, row, i * T)
        lxp[...] = lx
        aa = a
        bb = mult * (ig * xc)
        s = 1
        while s < T:
            a_sh = jnp.where(row >= s, pltpu.roll(aa, s, 0), 1.0)
            b_sh = jnp.where(row >= s, pltpu.roll(bb, s, 0), 0.0)
            bb = aa * b_sh + bb
            aa = aa * a_sh
            s *= 2
        hc = hcar[0:1, :]
        hh = bb + aa * hc
        hp_ref[...] = jnp.where(row < 1, hc, pltpu.roll(hh, 1, 0))
        hcar[...] = jnp.broadcast_to(_edge_row(hh, True), hcar.shape)
        gl, _ = _gelu(proj_ref[:, C_LG:C_LG + LRU_WIDTH])
        y_ref[:, ATTN_WIDTH:ATTN_WIDTH + LRU_WIDTH] = (gl * hh).astype(BF16)

        z = proj_ref[:, C_SC:C_SC + CONV_WIDTH] * proj_ref[:, C_SX:C_SX + CONV_WIDTH]
        c3 = (small_ref[10:11, :] * z + small_ref[9:10, :] * _past(z, zp[...], 1, row)
              + small_ref[8:9, :] * _past(z, zp[...], 2, row))
        zp[...] = z
        y_ref[:, ATTN_WIDTH + LRU_WIDTH:] = (proj_ref[:, C_SB:C_SB + CONV_WIDTH] * c3).astype(BF16)

    fix = lambda i: (0, 0)
    row = lambda i: (i, 0)
    return _pcall(
        body, name=name, grid=(nT,),
        in_specs=[pl.BlockSpec((T, Dm), row), pl.BlockSpec((8, Dm), fix),
                  _resident((Dm, IN_PROJ_WIDTH)), _resident((D_MODEL, Dm)),
                  pl.BlockSpec(memory_space=pltpu.SMEM),
                  pl.BlockSpec((16, LRU_WIDTH), fix),
                  pl.BlockSpec((LRU_WIDTH, LRU_WIDTH), fix),
                  pl.BlockSpec((LRU_WIDTH, LRU_WIDTH), fix)],
        out_specs=[pl.BlockSpec((T, Dm), row), pl.BlockSpec((T, Dm), row), pl.BlockSpec((T, IN_PROJ_WIDTH), row),
                   pl.BlockSpec((T, D_MODEL), row), pl.BlockSpec((T, Dm), row), pl.BlockSpec((T, LRU_WIDTH), row)],
        out_shape=[jax.ShapeDtypeStruct((S, Dm), F32), jax.ShapeDtypeStruct((S, Dm), BF16),
                   jax.ShapeDtypeStruct((S, IN_PROJ_WIDTH), F32), jax.ShapeDtypeStruct((S, D_MODEL), BF16),
                   jax.ShapeDtypeStruct((S, Dm), BF16), jax.ShapeDtypeStruct((S, LRU_WIDTH), F32)],
        scratch_shapes=[pltpu.VMEM((BLOCK, 2 * KV_WIDTH), F32), pltpu.VMEM((T, LRU_WIDTH), F32),
                        pltpu.VMEM((T, CONV_WIDTH), F32), pltpu.VMEM((SUBLANES, LRU_WIDTH), F32)],
        sem=("arbitrary",), args=(x, nrm, w_in, w_out, sinks, small, wa, wx), jobs=jobs)


def _mixer_bwd(x, nrm, dxo, h, proj, ymix, ymo, hprev, w_in, w_out, sinks, small, wa, wx, name, jobs=()):
    S, Dm = x.shape
    T = MIX_TILE
    nT = S // T
    nb = T // BLOCK
    bpt = T // BLOCK

    def body(x_ref, nrm_ref, dxo_ref, h_ref, proj_ref, kvprev_ref, lxprev_ref, scprev_ref, sxprev_ref, ymix_ref,
             ymo_ref, hp_ref, w_in_ref, w_out_ref, sink_ref, small_ref, wa_ref, wx_ref,
             dx_ref, red_ref, dgate_ref, dwo_ref, dwi_ref, dsm_ref, dsink_ref, dwa_ref, dwx_ref,
             dy_s, dp_s, acc_o, acc_i, *carried_state):
        @pl.when(pl.program_id(0) == 0)
        def _():
            for r in (red_ref, dgate_ref, acc_o, acc_i):
                r[...] = jnp.zeros_like(r)

        dxo_v = dxo_ref[...]
        dyb = (nrm_ref[3:4, :] * dxo_v).astype(BF16)
        dgate_ref[0:1, :] += jnp.sum(ymo_ref[...].astype(F32) * dxo_v, axis=0, keepdims=True)
        dy_s[...] = _dot_nt(dyb, w_out_ref[...])
        acc_o[...] += _dot_tn(ymix_ref[...], dyb)
        core(proj_ref, kvprev_ref, lxprev_ref, scprev_ref, sxprev_ref, dy_s, hp_ref, sink_ref, small_ref,
             wa_ref, wx_ref, dp_s, dsm_ref, dsink_ref, dwa_ref, dwx_ref, *carried_state)
        dpb = dp_s[...]
        acc_i[...] += _dot_tn(h_ref[...], dpb)
        dx_ref[...] = dxo_v + _norm_bwd(_dot_nt(dpb, w_in_ref[...]), x_ref[...], nrm_ref, red_ref)

        @pl.when(pl.program_id(0) == nT - 1)
        def _():
            dwo_ref[...] = acc_o[...].astype(BF16)
            dwi_ref[...] = acc_i[...].astype(BF16)

    def core(proj_ref, kvprev_ref, lxprev_ref, scprev_ref, sxprev_ref, dy_ref, hp_ref, sink_ref, small_ref,
             wa_ref, wx_ref, dp_ref, dsm_ref, dsink_ref, dwa_ref, dwx_ref,
             dk_s, dv_s, dkv_c, dxc_n, dc3_n, p_c):
        i = pl.program_id(0)
        ti = nT - 1 - i
        has_prev = jnp.where(ti == 0, 0.0, 1.0)

        @pl.when(i == 0)
        def _():
            for r in (dkv_c, dxc_n, dc3_n, p_c, dsm_ref, dsink_ref, dwa_ref, dwx_ref):
                r[...] = jnp.zeros_like(r)

        row = lax.broadcasted_iota(jnp.int32, (T, LRU_WIDTH), 0)

        kv = proj_ref[:, C_KV:C_KV + 2 * KV_WIDTH]
        ext = jnp.concatenate([kvprev_ref[...] * has_prev, kv], axis=0)
        kx = _head_planes(ext[:, :KV_WIDTH])
        vx = _head_planes(ext[:, KV_WIDTH:])
        dk_s[...] = jnp.zeros_like(dk_s)
        dv_s[...] = jnp.zeros_like(dv_s)
        dk_s[:, T:] = dkv_c[:, :BLOCK]
        dv_s[:, T:] = dkv_c[:, BLOCK:]
        first_tile = jnp.where(ti == 0, 1, 0)
        units = [(b, pair, e) for b in range(nb) for pair in range(N_Q_HEADS // 2) for e in range(2)]
        geometry = [_band_geometry(first_tile if b == 0 else 0) for b in range(nb)]
        keys = [slice(b * BLOCK, (b + 2) * BLOCK) for b in range(nb)]
        tile = {(b, pair): (slice(b * BLOCK, (b + 1) * BLOCK), slice(pair * LANES, (pair + 1) * LANES))
                for b in range(nb) for pair in range(N_Q_HEADS // 2)}
        qp = {k: (proj_ref[rc] * 0.125).astype(BF16) for k, rc in tile.items()}
        dob = {k: dy_ref[rc].astype(BF16) for k, rc in tile.items()}
        qp_t = {k: jnp.transpose(proj_ref[rc] * 0.125).astype(BF16) for k, rc in tile.items()}
        dob_t = {k: jnp.transpose(dy_ref[rc]).astype(BF16) for k, rc in tile.items()}
        scores = [_fold(_dot_nt(qp[(b, pair)], kx[pair // 2][e][keys[b]]), geometry[b][0]) for b, pair, e in units]
        dprob = [_fold(_dot_nt(dob[(b, pair)], vx[pair // 2][e][keys[b]]), geometry[b][0]) for b, pair, e in units]
        pn_wide, ds_wide = [], []
        for s, dpm, (b, pair, e) in zip(scores, dprob, units):
            h = 2 * pair + e
            own = geometry[b][0]
            pn, psink = _softmax_band(s, h, geometry[b], sink_ref[h])
            dsum = jnp.sum(pn * dpm, axis=-1, keepdims=True)
            dsink_ref[h:h + 1, :] += jnp.full((1, LANES), -1.0, F32) * jnp.sum(psink * dsum)
            pn_wide.append(_unfold(pn, own).astype(BF16))
            ds_wide.append(_unfold(pn * (dpm - dsum), own).astype(BF16))
        dq = {}
        for pw, ds, (b, pair, e) in zip(pn_wide, ds_wide, units):
            g = pair // 2
            head_e = slice(e * HEAD_DIM, (e + 1) * HEAD_DIM)
            head_g = slice(g * HEAD_DIM, (g + 1) * HEAD_DIM)
            dv_s[head_g, keys[b]] += _dot(dob_t[(b, pair)], pw)[head_e, :]
            dk_s[head_g, keys[b]] += _dot(qp_t[(b, pair)], ds)[head_e, :]
            part = _dot(ds, kx[g][e][keys[b]])
            dq[(b, pair)] = part if e == 0 else dq[(b, pair)] + part
        for k, rc in tile.items():
            dp_ref[rc] = (0.125 * dq[k]).astype(BF16)
        dp_ref[:, C_KV:C_KV + KV_WIDTH] = jnp.transpose(dk_s[:, BLOCK:]).astype(BF16)
        dp_ref[:, C_KV + KV_WIDTH:C_KV + 2 * KV_WIDTH] = jnp.transpose(dv_s[:, BLOCK:]).astype(BF16)
        dkv_c[:, :BLOCK] = dk_s[:, :BLOCK]
        dkv_c[:, BLOCK:] = dv_s[:, :BLOCK]

        lx = proj_ref[:, C_LX:C_LX + LRU_WIDTH]
        lxprev = lxprev_ref[...] * has_prev
        xc, xcb, r, ig, sp, a, mult, first = _lru_gates(lx, lxprev, small_ref, wa_ref, wx_ref, row, ti * T)
        hp = hp_ref[...]
        hh = a * hp + mult * (ig * xc)
        lg = proj_ref[:, C_LG:C_LG + LRU_WIDTH]
        gl, th = _gelu(lg)
        dyl = dy_ref[:, ATTN_WIDTH:ATTN_WIDTH + LRU_WIDTH]
        dp_ref[:, C_LG:C_LG + LRU_WIDTH] = (dyl * hh * _gelu_grad(lg, th)).astype(BF16)
        aa = jnp.where(row < T - 1, pltpu.roll(a, T - 1, 0), 1.0)
        bb = dyl * gl
        s = 1
        while s < T:
            a_sh = jnp.where(row < T - s, pltpu.roll(aa, T - s, 0), 1.0)
            b_sh = jnp.where(row < T - s, pltpu.roll(bb, T - s, 0), 0.0)
            bb = bb + aa * b_sh
            aa = aa * a_sh
            s *= 2
        G = bb + aa * p_c[0:1, :]
        p_c[...] = jnp.broadcast_to(_edge_row(a * G, False), p_c.shape)
        da = G * hp
        dmult = G * (ig * xc)
        dig = G * mult * xc
        dxc = G * mult * ig
        dla = da * a + dmult * jnp.where(first, 0.0, -(a * a) / mult)
        dr = dla * ((-LRU_C) * sp)
        lam = small_ref[7:8, :]
        dsm_ref[7:8, :] += jnp.sum(dla * ((-LRU_C) * r), axis=0, keepdims=True) * (-_sigmoid(-lam))
        dpa = dr * r * (1.0 - r)
        dpx = dig * ig * (1.0 - ig)
        dsm_ref[5:6, :] += jnp.sum(dpa, axis=0, keepdims=True)
        dsm_ref[6:7, :] += jnp.sum(dpx, axis=0, keepdims=True)
        dpab = dpa.astype(BF16)
        dpxb = dpx.astype(BF16)
        dwa_ref[...] += _dot_tn(xcb, dpab)
        dwx_ref[...] += _dot_tn(xcb, dpxb)
        dxc = dxc + _dot_nt(dpab, wa_ref[...]) + _dot_nt(dpxb, wx_ref[...])
        dsm_ref[4:5, :] += jnp.sum(dxc, axis=0, keepdims=True)
        dsm_ref[3:4, :] += jnp.sum(dxc * lx, axis=0, keepdims=True)
        for k in range(3):
            dsm_ref[k:k + 1, :] += jnp.sum(dxc * _past(lx, lxprev, 3 - k, row), axis=0, keepdims=True)
        nxt = dxc_n[...]
        dlx = (small_ref[3:4, :] * dxc + small_ref[2:3, :] * _future(dxc, nxt, 1, row)
               + small_ref[1:2, :] * _future(dxc, nxt, 2, row) + small_ref[0:1, :] * _future(dxc, nxt, 3, row))
        dxc_n[...] = dxc
        dp_ref[:, C_LX:C_LX + LRU_WIDTH] = dlx.astype(BF16)

        sc = proj_ref[:, C_SC:C_SC + CONV_WIDTH]
        sx = proj_ref[:, C_SX:C_SX + CONV_WIDTH]
        sb = proj_ref[:, C_SB:C_SB + CONV_WIDTH]
        z = sc * sx
        zprev = (scprev_ref[...] * sxprev_ref[...]) * has_prev
        z1 = _past(z, zprev, 1, row)
        z2 = _past(z, zprev, 2, row)
        c3 = small_ref[10:11, :] * z + small_ref[9:10, :] * z1 + small_ref[8:9, :] * z2
        dys = dy_ref[:, ATTN_WIDTH + LRU_WIDTH:]
        dp_ref[:, C_SB:C_SB + CONV_WIDTH] = (dys * c3).astype(BF16)
        dc3 = dys * sb
        dsm_ref[10:11, :] += jnp.sum(dc3 * z, axis=0, keepdims=True)
        dsm_ref[9:10, :] += jnp.sum(dc3 * z1, axis=0, keepdims=True)
        dsm_ref[8:9, :] += jnp.sum(dc3 * z2, axis=0, keepdims=True)
        nxt3 = dc3_n[...]
        dz = (small_ref[10:11, :] * dc3 + small_ref[9:10, :] * _future(dc3, nxt3, 1, row)
              + small_ref[8:9, :] * _future(dc3, nxt3, 2, row))
        dc3_n[...] = dc3
        dp_ref[:, C_SC:C_SC + CONV_WIDTH] = (dz * sx).astype(BF16)
        dp_ref[:, C_SX:C_SX + CONV_WIDTH] = (dz * sc).astype(BF16)

    fix = lambda i: (0, 0)
    cur = lambda i: (nT - 1 - i, 0)
    prev_cols = lambda cb: (lambda i: (jnp.maximum(nT - 2 - i, 0), cb))
    return _pcall(
        body, name=name, grid=(nT,),
        in_specs=[pl.BlockSpec((T, Dm), cur), pl.BlockSpec((8, Dm), fix), pl.BlockSpec((T, Dm), cur),
                  pl.BlockSpec((T, Dm), cur),
                  pl.BlockSpec((T, IN_PROJ_WIDTH), cur),
                  pl.BlockSpec((BLOCK, 2 * KV_WIDTH),
                               lambda i: (jnp.maximum((nT - 1 - i) * bpt - 1, 0), C_KV // (2 * KV_WIDTH))),
                  pl.BlockSpec((T, LRU_WIDTH), prev_cols(C_LX // LRU_WIDTH)),
                  pl.BlockSpec((T, CONV_WIDTH), prev_cols(C_SC // CONV_WIDTH)),
                  pl.BlockSpec((T, CONV_WIDTH), prev_cols(C_SX // CONV_WIDTH)),
                  pl.BlockSpec((T, D_MODEL), cur), pl.BlockSpec((T, Dm), cur),
                  pl.BlockSpec((T, LRU_WIDTH), cur),
                  _resident((Dm, IN_PROJ_WIDTH)), _resident((D_MODEL, Dm)),
                  pl.BlockSpec(memory_space=pltpu.SMEM),
                  pl.BlockSpec((16, LRU_WIDTH), fix),
                  pl.BlockSpec((LRU_WIDTH, LRU_WIDTH), fix),
                  pl.BlockSpec((LRU_WIDTH, LRU_WIDTH), fix)],
        out_specs=[pl.BlockSpec((T, Dm), cur), pl.BlockSpec((8, Dm), fix), pl.BlockSpec((8, Dm), fix),
                   _resident((D_MODEL, Dm)), _resident((Dm, IN_PROJ_WIDTH)),
                   pl.BlockSpec((16, LRU_WIDTH), fix),
                   pl.BlockSpec((SUBLANES, LANES), fix),
                   pl.BlockSpec((LRU_WIDTH, LRU_WIDTH), fix),
                   pl.BlockSpec((LRU_WIDTH, LRU_WIDTH), fix)],
        out_shape=[jax.ShapeDtypeStruct((S, Dm), F32), jax.ShapeDtypeStruct((8, Dm), F32),
                   jax.ShapeDtypeStruct((8, Dm), F32),
                   jax.ShapeDtypeStruct((D_MODEL, Dm), BF16), jax.ShapeDtypeStruct((Dm, IN_PROJ_WIDTH), BF16),
                   jax.ShapeDtypeStruct((16, LRU_WIDTH), F32),
                   jax.ShapeDtypeStruct((SUBLANES, LANES), F32),
                   jax.ShapeDtypeStruct((LRU_WIDTH, LRU_WIDTH), F32),
                   jax.ShapeDtypeStruct((LRU_WIDTH, LRU_WIDTH), F32)],
        scratch_shapes=[pltpu.VMEM((T, D_MODEL), F32), pltpu.VMEM((T, IN_PROJ_WIDTH), BF16),
                        pltpu.VMEM((D_MODEL, Dm), F32), pltpu.VMEM((Dm, IN_PROJ_WIDTH), F32),
                        pltpu.VMEM((KV_WIDTH, T + BLOCK), F32), pltpu.VMEM((KV_WIDTH, T + BLOCK), F32),
                        pltpu.VMEM((BLOCK, 2 * KV_WIDTH), F32), pltpu.VMEM((T, LRU_WIDTH), F32),
                        pltpu.VMEM((T, CONV_WIDTH), F32), pltpu.VMEM((SUBLANES, LRU_WIDTH), F32)],
        sem=("arbitrary",),
        args=(x, nrm, dxo, h, proj, proj, proj, proj, proj, ymix, ymo, hprev, w_in, w_out, sinks, small, wa, wx),
        jobs=jobs)


def _adamw_update(g, w_ref, m_ref, v_ref, go_ref, d_ref, mo_ref, vo_ref):
    mn = ADAM_B1 * m_ref[...] + (1.0 - ADAM_B1) * g
    vn = ADAM_B2 * v_ref[...] + (1.0 - ADAM_B2) * (g * g)
    go_ref[...] = g
    mo_ref[...] = mn
    vo_ref[...] = vn
    m_hat = mn / (1.0 - ADAM_B1 ** ADAM_STEP)
    v_hat = vn / (1.0 - ADAM_B2 ** ADAM_STEP)
    d_ref[...] = (-ADAM_LR) * (m_hat / (jnp.sqrt(v_hat) + ADAM_EPS) + ADAM_WD * w_ref[...])


def _adamw(w, g, m, v, name):
    R, C = w.shape
    tr = 8
    for cand in (512, 256, 128, 64, 32, 16, 8):
        if R % cand == 0 and cand * C * 4 <= (1 << 20):
            tr = cand
            break

    def body(w_ref, g_ref, *rest):
        _adamw_update(g_ref[...], w_ref, *rest)

    spec = pl.BlockSpec((tr, C), lambda i: (i, 0))
    return _pcall(body, name=name, grid=(R // tr,), in_specs=[spec] * 4, out_specs=[spec] * 4,
                  out_shape=[jax.ShapeDtypeStruct((R, C), F32)] * 4, sem=("arbitrary",), args=(w, g, m, v))


def _adamw_partials(w, partials, m, v, name):
    nl = len(partials)
    _, R, C = partials[0][0].shape
    tr = 8
    for cand in (256, 128, 64, 32, 16):
        if R % cand == 0 and cand * C * 4 <= (1 << 19):
            tr = cand
            break
    ni = R // tr

    def body(*refs):
        w_ref, p_refs = refs[0], refs[1:1 + 2 * nl]
        m_ref, v_ref, go_ref, d_ref, mo_ref, vo_ref = refs[1 + 2 * nl:]
        for l in range(nl):
            @pl.when(pl.program_id(0) == l)
            def _(pair=p_refs[2 * l:2 * l + 2]):
                own, sib = [((p[0].astype(F32) + p[1].astype(F32)) + p[2].astype(F32)) + p[3].astype(F32)
                            for p in pair]
                _adamw_update(own + sib, w_ref, m_ref, v_ref, go_ref, d_ref, mo_ref, vo_ref)

    def slots(l):
        return pl.BlockSpec((N_CHIPS, tr, C),
                            lambda ll, i: (0, jnp.where(ll == l, i, jnp.where(ll < l, 0, ni - 1)), 0))

    spec = pl.BlockSpec((tr, C), lambda ll, i: (ll * ni + i, 0))
    return pl.pallas_call(
        body, name=name, grid=(nl, ni),
        in_specs=[spec] + [slots(l) for l in range(nl) for _ in range(2)] + [spec, spec], out_specs=[spec] * 4,
        out_shape=[jax.ShapeDtypeStruct((nl * R, C), F32)] * 4,
        compiler_params=_cp("arbitrary", "arbitrary"),
    )(w, *[p for pair in partials for p in pair], m, v)


GATHER_SEMS = 7


def _two_level_gather(x_ref, out_ref, send_sems, recv_sems, local_sem, base=0):
    M = x_ref.shape[0]
    x, y, c = lax.axis_index("x"), lax.axis_index("y"), lax.axis_index("c")
    me, sibling = (x, y, c), (x, y, 1 - c)
    chips = [(1 - x, y), (x, 1 - y), (1 - x, 1 - y)]

    def rows(px, py, pc):
        return out_ref.at[pl.ds(pl.multiple_of((4 * px + 2 * py + pc) * M, SUBLANES), M), :]

    def copy(k, block, to, src=None):
        return pltpu.make_async_remote_copy(
            src_ref=rows(*block) if src is None else src, dst_ref=rows(*block),
            send_sem=send_sems.at[base + k], recv_sem=recv_sems.at[base + k], device_id=to, device_id_type=MESH)

    mine = pltpu.make_async_copy(x_ref, rows(*me), local_sem)
    mine.start()
    first = [copy(0, me, sibling, src=x_ref)]
    first += [copy(1 + j, me, (*chip, c), src=x_ref) for j, chip in enumerate(chips)]
    for cp in first:
        cp.start()
    passed = [copy(4 + j, (*chip, c), sibling) for j, chip in enumerate(chips)]
    for j, chip in enumerate(chips):
        copy(1 + j, (*chip, c), me).wait_recv()
        passed[j].start()
    copy(0, sibling, me).wait_recv()
    for j, chip in enumerate(chips):
        copy(4 + j, (*chip, 1 - c), me).wait_recv()
    for cp in first + passed:
        cp.wait_send()
    mine.wait()


def _prologue(pack, w_mod, jobs, name):
    M = pack.shape[0]
    L, Dm, N = w_mod.shape
    nj = len(jobs)
    rows_c = Dm // LANES
    tn = 768

    def body(*refs):
        pack_ref, w_ref = refs[:2]
        jin, refs = refs[2:2 + nj], refs[2 + nj:]
        g_ref, ca_ref, mod_ref = refs[:3]
        jout, refs = refs[3:3 + nj], refs[3 + nj:]
        part_ref, send_sems, recv_sems, local_sem, *jsems = refs
        _run_jobs("start", jobs, jin, jout, jsems)
        _two_level_gather(pack_ref, g_ref, send_sems, recv_sems, local_sem.at[0], 0)
        ca_ref[...] = jnp.zeros_like(ca_ref)
        for r in range(rows_c):
            cv = g_ref[pl.ds(r, N_DEV, stride=M), :]
            ca_ref[0:N_DEV, r * LANES:(r + 1) * LANES] = (cv * _sigmoid(cv)).astype(BF16)
        ca = ca_ref[...]
        for l in range(L):
            for n0 in range(0, N, tn):
                part_ref[l * 16:(l + 1) * 16, n0:n0 + tn] = _dot(ca, w_ref[l, :, n0:n0 + tn].astype(BF16))
        _two_level_gather(part_ref, mod_ref, send_sems, recv_sems, local_sem.at[1], GATHER_SEMS)
        _run_jobs("relay", jobs, jin, jout, jsems)
        _run_jobs("finish", jobs, jin, jout, jsems)

    vmem = pl.BlockSpec(memory_space=pltpu.VMEM)
    hbm = pl.BlockSpec(memory_space=pltpu.HBM)
    res = pl.pallas_call(
        body, name=name,
        out_shape=[jax.ShapeDtypeStruct((N_DEV * M, LANES), F32), jax.ShapeDtypeStruct((16, Dm), BF16),
                   jax.ShapeDtypeStruct((N_DEV * L * 16, N), F32)] + [job.out_shape for job in jobs],
        in_specs=[vmem, vmem] + [hbm] * nj, out_specs=[vmem, vmem, vmem] + [hbm] * nj,
        scratch_shapes=[pltpu.VMEM((L * 16, N), F32), pltpu.SemaphoreType.DMA((2 * GATHER_SEMS,)),
                        pltpu.SemaphoreType.DMA((2 * GATHER_SEMS,)), pltpu.SemaphoreType.DMA((2,))]
        + _job_scratch(nj),
        compiler_params=pltpu.CompilerParams(vmem_limit_bytes=VMEM_LIMIT),
    )(pack, w_mod, *[job.src for job in jobs])
    return res[0], res[1], res[2], list(res[3:])


def _all_gather_small(v, name):
    M, N = v.shape

    def body(x_ref, out_ref, sum_ref, send_sems, recv_sems, local_sem):
        _two_level_gather(x_ref, out_ref, send_sems, recv_sems, local_sem)
        acc = out_ref[0:M, :]
        for d in range(1, N_DEV):
            acc = acc + out_ref[d * M:(d + 1) * M, :]
        sum_ref[...] = acc

    return pl.pallas_call(
        body, name=name,
        out_shape=[jax.ShapeDtypeStruct((N_DEV * M, N), F32), jax.ShapeDtypeStruct((M, N), F32)],
        in_specs=[pl.BlockSpec(memory_space=pltpu.VMEM)],
        out_specs=[pl.BlockSpec(memory_space=pltpu.VMEM), pl.BlockSpec(memory_space=pltpu.VMEM)],
        scratch_shapes=[pltpu.SemaphoreType.DMA((GATHER_SEMS,)), pltpu.SemaphoreType.DMA((GATHER_SEMS,)),
                        pltpu.SemaphoreType.DMA],
        compiler_params=pltpu.CompilerParams(vmem_limit_bytes=VMEM_LIMIT),
    )(v)


_BIG = (("w_ffn1_gu", 1), ("w_ffn1_down", 0), ("w_ffn2_gu", 1), ("w_ffn2_down", 0), ("w_in", 1), ("w_out", 0))
_AXIS = dict(_BIG)

_GATHER_PLAN = {
    "first": [(0, "w_ffn1_gu"), (0, "w_ffn1_down")],
    (0, "ffn1"): [(0, "w_in"), (0, "w_out"), (0, "w_ffn2_gu")],
    (0, "mix"): [(0, "w_ffn2_down")],
    (0, "ffn2"): [(1, "w_ffn1_gu"), (1, "w_ffn1_down")],
    (1, "ffn1"): [(1, "w_in"), (1, "w_out"), (1, "w_ffn2_gu")],
    (1, "mix"): [(1, "w_ffn2_down")],
}


def _pack(arrs, rows_multiple=SUBLANES):
    flat = jnp.concatenate([a.astype(F32).reshape(-1) for a in arrs])
    unit = rows_multiple * LANES
    total = -(-flat.shape[0] // unit) * unit
    return jnp.pad(flat, (0, total - flat.shape[0])).reshape(total // LANES, LANES)


def _unpack(flat, shapes):
    out, off = [], 0
    for shp in shapes:
        n = int(math.prod(shp))
        out.append(flat[off:off + n].reshape(shp))
        off += n
    return out


def _block_diag(w):
    out = jnp.zeros((LRU_WIDTH, LRU_WIDTH), F32)
    for h in range(4):
        out = lax.dynamic_update_slice(out, w[h], (h * HEAD_DIM, h * HEAD_DIM))
    return out


def _diag_blocks(w):
    return jnp.stack([w[h * HEAD_DIM:(h + 1) * HEAD_DIM, h * HEAD_DIM:(h + 1) * HEAD_DIM] for h in range(4)])


def _rows8(*rows):
    z = jnp.zeros((8 - len(rows), rows[0].shape[-1]), F32)
    return jnp.concatenate([jnp.stack(rows), z], axis=0)


def kernel(x, c, w_mod, b_mod, g_norm, w_ffn1_gu, w_ffn1_down, w_ffn2_gu, w_ffn2_down, w_in, w_out, attn_sinks, lru_conv_w, lru_conv_b, lru_gate_a_w, lru_gate_a_b, lru_gate_x_w, lru_gate_x_b, lru_lambda, sc_conv_w, g_final, loss_target, m_w_mod, m_b_mod, m_g_norm, m_w_ffn1_gu, m_w_ffn1_down, m_w_ffn2_gu, m_w_ffn2_down, m_w_in, m_w_out, m_attn_sinks, m_lru_conv_w, m_lru_conv_b, m_lru_gate_a_w, m_lru_gate_a_b, m_lru_gate_x_w, m_lru_gate_x_b, m_lru_lambda, m_sc_conv_w, m_g_final, v_w_mod, v_b_mod, v_g_norm, v_w_ffn1_gu, v_w_ffn1_down, v_w_ffn2_gu, v_w_ffn2_down, v_w_in, v_w_out, v_attn_sinks, v_lru_conv_w, v_lru_conv_b, v_lru_gate_a_w, v_lru_gate_a_b, v_lru_gate_x_w, v_lru_gate_x_b, v_lru_lambda, v_sc_conv_w, v_g_final):
    W = dict(w_mod=w_mod, b_mod=b_mod, g_norm=g_norm, w_ffn1_gu=w_ffn1_gu, w_ffn1_down=w_ffn1_down,
             w_ffn2_gu=w_ffn2_gu, w_ffn2_down=w_ffn2_down, w_in=w_in, w_out=w_out, attn_sinks=attn_sinks,
             lru_conv_w=lru_conv_w, lru_conv_b=lru_conv_b, lru_gate_a_w=lru_gate_a_w, lru_gate_a_b=lru_gate_a_b,
             lru_gate_x_w=lru_gate_x_w, lru_gate_x_b=lru_gate_x_b, lru_lambda=lru_lambda, sc_conv_w=sc_conv_w,
             g_final=g_final)
    M1 = dict(w_mod=m_w_mod, b_mod=m_b_mod, g_norm=m_g_norm, w_ffn1_gu=m_w_ffn1_gu, w_ffn1_down=m_w_ffn1_down,
              w_ffn2_gu=m_w_ffn2_gu, w_ffn2_down=m_w_ffn2_down, w_in=m_w_in, w_out=m_w_out,
              attn_sinks=m_attn_sinks, lru_conv_w=m_lru_conv_w, lru_conv_b=m_lru_conv_b,
              lru_gate_a_w=m_lru_gate_a_w, lru_gate_a_b=m_lru_gate_a_b, lru_gate_x_w=m_lru_gate_x_w,
              lru_gate_x_b=m_lru_gate_x_b, lru_lambda=m_lru_lambda, sc_conv_w=m_sc_conv_w, g_final=m_g_final)
    V1 = dict(w_mod=v_w_mod, b_mod=v_b_mod, g_norm=v_g_norm, w_ffn1_gu=v_w_ffn1_gu, w_ffn1_down=v_w_ffn1_down,
              w_ffn2_gu=v_w_ffn2_gu, w_ffn2_down=v_w_ffn2_down, w_in=v_w_in, w_out=v_w_out,
              attn_sinks=v_attn_sinks, lru_conv_w=v_lru_conv_w, lru_conv_b=v_lru_conv_b,
              lru_gate_a_w=v_lru_gate_a_w, lru_gate_a_b=v_lru_gate_a_b, lru_gate_x_w=v_lru_gate_x_w,
              lru_gate_x_b=v_lru_gate_x_b, lru_lambda=v_lru_lambda, sc_conv_w=v_sc_conv_w, g_final=v_g_final)
    names = ["w_mod", "b_mod", "g_norm", "w_ffn1_gu", "w_ffn1_down", "w_ffn2_gu", "w_ffn2_down", "w_in", "w_out",
             "attn_sinks", "lru_conv_w", "lru_conv_b", "lru_gate_a_w", "lru_gate_a_b", "lru_gate_x_w",
             "lru_gate_x_b", "lru_lambda", "sc_conv_w", "g_final"]

    xs = x[0]
    tgt = loss_target[0]
    S = xs.shape[0]
    chip = 2 * lax.axis_index("x") + lax.axis_index("y")
    batch = 2 * chip + lax.axis_index("c")
    L = DEPTH

    full = {}

    def gather_jobs(key):
        return [_GatherJob(W[n][l].astype(BF16), _AXIS[n]) for l, n in _GATHER_PLAN.get(key, ())]

    def landed(key, outs):
        full.update(zip(_GATHER_PLAN.get(key, ()), outs))

    fwd_shapes = [(D_MODEL,), g_norm.shape, lru_conv_w.shape, sc_conv_w.shape]
    gathered, c_act, mod_all, ex = _prologue(_pack([c[0], g_norm, lru_conv_w, sc_conv_w]), w_mod,
                                             gather_jobs("first"), "prologue")
    landed("first", ex)
    gathered = gathered.reshape(N_DEV, -1)
    per_chip = [_unpack(gathered[2 * jj], fwd_shapes) for jj in range(N_CHIPS)]
    g_norm_full = jnp.concatenate([p[1] for p in per_chip], axis=-1)
    lru_conv_w_full = jnp.concatenate([p[2] for p in per_chip], axis=-1)
    sc_conv_w_full = jnp.concatenate([p[3] for p in per_chip], axis=-1)
    mod_all = mod_all.reshape(N_DEV, L, 16, -1)
    mod_rows = [lax.dynamic_index_in_dim(mod_all[2 * jj], batch, axis=1, keepdims=False) for jj in range(N_CHIPS)]
    mod = (jnp.concatenate(mod_rows, axis=-1) + b_mod).reshape(L, 9, D_MODEL)

    def nrm_rows(l, s):
        return _rows8(g_norm_full[l, s], mod[l, 3 * s], mod[l, 3 * s + 1], mod[l, 3 * s + 2])

    def mixer_params(l):
        small = jnp.concatenate([lru_conv_w_full[l], lru_conv_b[l][None], lru_gate_a_b[l][None],
                                 lru_gate_x_b[l][None], lru_lambda[l][None], sc_conv_w_full[l],
                                 jnp.zeros((5, LRU_WIDTH), F32)], axis=0)
        return (attn_sinks[l], small, _block_diag(lru_gate_a_w[l]).astype(BF16),
                _block_diag(lru_gate_x_w[l]).astype(BF16))

    saved = []
    xcur = xs
    for l in range(L):
        n1, n2, n3 = nrm_rows(l, 0), nrm_rows(l, 1), nrm_rows(l, 2)

        def ffn(which, xin, nrm, head=None):
            key = (l, which)
            (xo, h, gu, y, *stats), ex = _ffn_fwd(xin, nrm, full[(l, f"w_{which}_gu")], full[(l, f"w_{which}_down")],
                                                  f"l{l}_{which}", gather_jobs(key), head)
            landed(key, ex)
            return (xo, *stats), (xin, h, gu, y)

        (x1,), s1 = ffn("ffn1", xcur, n1)
        mp = mixer_params(l)
        (x2, h2, proj, ymix, ymo, hprev), ex = _mixer_fwd(x1, n2, full[(l, "w_in")], full[(l, "w_out")], *mp,
                                                          f"l{l}_mix", gather_jobs((l, "mix")))
        landed((l, "mix"), ex)
        s2 = (x1, h2, proj, ymix, ymo, hprev, mp)
        (xcur, *stats), s3 = ffn("ffn2", x2, n3, (_rows8(g_final), tgt) if l == L - 1 else None)
        saved.append((n1, n2, n3, s1, s2, s3))

    dx, stats = xcur, stats[0]
    loss_here, d_g_final = stats[1, 0:1], stats[0]

    recv, theirs = {}, {}
    waiting = []

    def carried(fn, *a, extra=()):
        items = waiting + list(extra)
        waiting.clear()
        outs, landed_now = fn(*a, jobs=[_SiblingJob(recv[(ll, n)]) if g is None else _ScatterJob(g, _AXIS[n])
                                        for ll, n, g in items])
        for (ll, n, g), arr in zip(items, landed_now):
            if g is None:
                theirs[(ll, n)] = arr
            else:
                recv[(ll, n)] = arr
                waiting.append((ll, n, None))
        return outs

    dmod, d_gnorm, d_small = [None] * L, [None] * L, [None] * L
    for l in reversed(range(L)):
        n1, n2, n3, s1, s2, s3 = saved[l]

        def plain(fn, *a):
            return fn(*a)[0]

        def ffn_bwd(which, dxo, sv, nrm, last):
            xin, h, gu, y = sv
            tag = f"l{l}_{which}"
            dgu, dgate, dw_down = carried(
                _ffn_down_bwd, dxo, gu, y, full[(l, f"w_{which}_down")], nrm, tag + "_down_bwd")
            dw_gu = carried(_atb, h, dgu, BF16, 1024, 2816, tag + "_dw_gu", extra=[(l, f"w_{which}_down", dw_down)])
            mine = [(l, f"w_{which}_gu", dw_gu)]
            dxi, red = (carried if last else plain)(
                _nt_norm_bwd, dgu, full[(l, f"w_{which}_gu")], xin, nrm, dxo, tag + "_gu_bwd",
                **(dict(extra=mine) if last else {}))
            if not last:
                waiting.extend(mine)
            return dxi, (red[0], red[1], dgate[0]), red[2]

        dx, dm3, dg3 = ffn_bwd("ffn2", dx, s3, n3, False)
        x_in, h2, proj, ymix, ymo, hprev, mp = s2
        dx, red, dgate, dw_out, dw_in, dsm, dsink, dwa, dwx = carried(
            _mixer_bwd, x_in, n2, dx, h2, proj, ymix, ymo, hprev, full[(l, "w_in")], full[(l, "w_out")], *mp,
            f"l{l}_mix_bwd")
        waiting.extend([(l, "w_out", dw_out), (l, "w_in", dw_in)])
        dm2, dg2 = (red[0], red[1], dgate[0]), red[2]
        dx, dm1, dg1 = ffn_bwd("ffn1", dx, s1, n1, l == 0)
        dmod[l] = jnp.stack(list(dm1) + list(dm2) + list(dm3))
        d_gnorm[l] = jnp.stack([dg1, dg2, dg3])
        d_small[l] = (dsink[:, 0], dsm[0:4], dsm[4], _diag_blocks(dwa), dsm[5], _diag_blocks(dwx), dsm[6],
                      dsm[7], dsm[8:11])
    grad_x = dx[None]

    def both(k):
        return jnp.stack([d_small[0][k], d_small[1][k]])
    small_names = ["g_norm", "attn_sinks", "lru_conv_w", "lru_conv_b", "lru_gate_a_w", "lru_gate_a_b",
                   "lru_gate_x_w", "lru_gate_x_b", "lru_lambda", "sc_conv_w", "g_final"]
    small_parts = [jnp.stack(d_gnorm)] + [both(k) for k in range(9)] + [d_g_final]
    dmod_flat = jnp.stack(dmod).reshape(-1)
    bwd_gathered, bwd_sum = _all_gather_small(_pack([dmod_flat] + small_parts + [loss_here]), "gather_small_bwd")
    n_mod = dmod_flat.shape[0]
    dmod_all = bwd_gathered.reshape(N_DEV, -1)[:, :n_mod].reshape(N_DEV, L, 9 * D_MODEL)
    bwd_sum = bwd_sum.reshape(-1)
    G = {"b_mod": bwd_sum[:n_mod].reshape(L, 9 * D_MODEL)}
    *small_sums, loss = _unpack(bwd_sum[n_mod:], [p.shape for p in small_parts] + [(1,)])
    loss = loss[0]
    G.update(zip(small_names, small_sums))
    for n in ("g_norm", "lru_conv_w", "sc_conv_w"):
        wdt = W[n].shape[-1]
        G[n] = lax.dynamic_slice_in_dim(G[n], chip * wdt, wdt, axis=G[n].ndim - 1)

    ncol = w_mod.shape[-1]
    dmod_cols = lax.dynamic_slice_in_dim(dmod_all, chip * ncol, ncol, axis=2)
    zeros8 = jnp.zeros((N_DEV, ncol), F32)
    g_w_mod = jnp.stack([carried(_atb, c_act, jnp.concatenate([dmod_cols[:, l], zeros8], axis=0).astype(BF16), F32,
                                 D_MODEL, 768, f"l{l}_dw_mod") for l in range(L)])

    out_g, out_d, out_m, out_v = {}, {}, {}, {}
    res, _ = _adamw(w_mod.reshape(-1, ncol), g_w_mod.reshape(-1, ncol), m_w_mod.reshape(-1, ncol),
                    v_w_mod.reshape(-1, ncol), "adamw_w_mod")
    out_g["w_mod"], out_d["w_mod"], out_m["w_mod"], out_v["w_mod"] = [r.reshape(w_mod.shape) for r in res]
    for n, _ in _BIG:
        shp = W[n].shape
        flat = (shp[0] * shp[1], shp[2])
        res = _adamw_partials(W[n].reshape(flat), [(recv[(l, n)], theirs[(l, n)]) for l in range(L)],
                              M1[n].reshape(flat), V1[n].reshape(flat), f"adamw_{n}")
        out_g[n], out_d[n], out_m[n], out_v[n] = [r.reshape(shp) for r in res]
    rest = ["b_mod"] + small_names
    shapes = [W[n].shape for n in rest]
    res, _ = _adamw(_pack([W[n] for n in rest]), _pack([G[n] for n in rest]), _pack([M1[n] for n in rest]),
                    _pack([V1[n] for n in rest]), "adamw_small")
    for dst, r in zip((out_g, out_d, out_m, out_v), res):
        dst.update(zip(rest, _unpack(r.reshape(-1), shapes)))

    return (loss, grad_x, *[out_g[n] for n in names], *[out_d[n] for n in names],
            *[out_m[n] for n in names], *[out_v[n] for n in names])
```

```python
import math

import jax
import jax.numpy as jnp
from jax import lax
from jax.experimental import pallas as pl
from jax.experimental.pallas import tpu as pltpu

F32 = jnp.float32
BF16 = jnp.bfloat16

D_MODEL = 1024
DEPTH = 2
HEAD_DIM = 64
N_Q_HEADS = 8
ATTN_WIDTH = 512
KV_WIDTH = 128
LRU_WIDTH = 256
CONV_WIDTH = 256
IN_PROJ_WIDTH = 2048
BLOCK = 128
D_FF = 2816
EPS = 1e-6
NEG_INF = -1e30
LRU_C = 8.0
N_CHIPS = 4
N_DEV = 8

C_Q, C_KV, C_LX, C_LG, C_SB, C_SC, C_SX = 0, 512, 768, 1024, 1280, 1536, 1792

ADAM_LR = 0.001
ADAM_B1 = 0.9
ADAM_B2 = 0.999
ADAM_EPS = 1e-08
ADAM_WD = 0.01
ADAM_STEP = 10

LANES = 128
SUBLANES = 8
VMEM_LIMIT = 56 * 1024 * 1024
MIX_TILE = 256

MESH = pl.DeviceIdType.MESH


def _cp(*sem):
    return pltpu.CompilerParams(dimension_semantics=sem, vmem_limit_bytes=VMEM_LIMIT)


def _tile(n, pref):
    t = min(n, pref)
    while n % t:
        t //= 2
    return t


MXU_DIM = 256


def _resident(shape):
    return pl.BlockSpec(shape, lambda *_: (0, 0), pipeline_mode=pl.Buffered(1))


def _sigmoid(v):
    return 1.0 / (1.0 + jnp.exp(-v))


def _expm1(v):
    series = v * (1.0 + v * (0.5 + v * (1.0 / 6.0)))
    return jnp.where(v > -0.01, series, jnp.exp(v) - 1.0)


def _softplus_neg(lam):
    e = jnp.exp(-jnp.abs(lam))
    log1p = jnp.where(e < 1e-2, e * (1.0 - e * (0.5 - e * (1.0 / 3.0))), jnp.log(1.0 + e))
    return jnp.maximum(-lam, 0.0) + log1p


_GELU_K = math.sqrt(2.0 / math.pi)
_GELU_C = 0.044715


def _gelu(v):
    t = jnp.tanh(_GELU_K * (v + _GELU_C * v * v * v))
    return 0.5 * v * (1.0 + t), t


def _gelu_grad(v, t):
    return 0.5 * (1.0 + t) + 0.5 * v * (1.0 - t * t) * _GELU_K * (1.0 + 3.0 * _GELU_C * v * v)


def _dot(a, b):
    return jnp.dot(a, b, preferred_element_type=F32)


def _dot_nt(a, b):
    return lax.dot_general(a, b, (((1,), (1,)), ((), ())), preferred_element_type=F32)


def _dot_tn(a, b):
    return lax.dot_general(a, b, (((0,), (0,)), ((), ())), preferred_element_type=F32)


def _window(ref, axis, j, width):
    start = pl.multiple_of(j * width, LANES if axis == 1 else 16)
    if axis == 1:
        return ref.at[:, pl.ds(start, width)]
    return ref.at[pl.ds(start, width), :]


def _chip_peers():
    x, y, c = lax.axis_index("x"), lax.axis_index("y"), lax.axis_index("c")
    return x, y, c, [(1 - x, y), (x, 1 - y), (1 - x, 1 - y)]


class _GatherJob:
    def __init__(self, shard, axis):
        self.src, self.axis, self.width, self.half = shard, axis, shard.shape[axis], shard.shape[0] // 2
        full = tuple(d * N_CHIPS if k == axis else d for k, d in enumerate(shard.shape))
        self.out_shape = jax.ShapeDtypeStruct(full, shard.dtype)

    def _piece(self, ref, j, hf):
        if self.axis == 1:
            return ref.at[pl.ds(pl.multiple_of(hf * self.half, 16), self.half),
                          pl.ds(pl.multiple_of(j * self.width, LANES), self.width)]
        return ref.at[pl.ds(pl.multiple_of(j * self.width + hf * self.half, 16), self.half), :]

    def _copies(self, src, dst, send, recv, loc, t):
        x, y, c, chips = _chip_peers()
        j = 2 * x + y
        owners = [2 * px + py for px, py in chips]
        local = pltpu.make_async_copy(src, _window(dst, self.axis, j, self.width), loc.at[t])
        mine = src.at[pl.ds(pl.multiple_of(c * self.half, 16), self.half), :]

        def ici(k, owner):
            return pltpu.make_async_remote_copy(
                src_ref=mine, dst_ref=self._piece(dst, owner, c), send_sem=send.at[JOB_SEMS * t + k],
                recv_sem=recv.at[JOB_SEMS * t + k], device_id=(*chips[k], c), device_id_type=MESH)

        def relay(k, hf):
            piece = self._piece(dst, owners[k], hf)
            return pltpu.make_async_remote_copy(
                src_ref=piece, dst_ref=piece, send_sem=send.at[JOB_SEMS * t + 4 + k],
                recv_sem=recv.at[JOB_SEMS * t + 4 + k],
                device_id=(x, y, 1 - c), device_id_type=MESH)

        return (local, [ici(k, j) for k in range(3)], [ici(k, owners[k]) for k in range(3)],
                [relay(k, c) for k in range(3)], [relay(k, 1 - c) for k in range(3)])

    def start(self, *a):
        local, ici_out, _, _, _ = self._copies(*a)
        local.start()
        for cp in ici_out:
            cp.start()

    def relay(self, *a):
        _, _, ici_in, relay_out, _ = self._copies(*a)
        for arrived, onward in zip(ici_in, relay_out):
            arrived.wait_recv()
            onward.start()

    def finish(self, *a):
        local, ici_out, _, relay_out, relay_in = self._copies(*a)
        for cp in relay_in:
            cp.wait_recv()
        for cp in ici_out + relay_out:
            cp.wait_send()
        local.wait()


class _ScatterJob:
    def __init__(self, full, axis):
        self.src, self.axis, self.width = full, axis, full.shape[axis] // N_CHIPS
        shard = tuple(self.width if k == axis else d for k, d in enumerate(full.shape))
        self.out_shape = jax.ShapeDtypeStruct((N_CHIPS,) + shard, full.dtype)

    def _copies(self, src, dst, send, recv, loc, t):
        x, y, c, chips = _chip_peers()
        local = pltpu.make_async_copy(_window(src, self.axis, 2 * x + y, self.width), dst.at[3], loc.at[t])
        sends = [pltpu.make_async_remote_copy(
            src_ref=_window(src, self.axis, 2 * px + py, self.width), dst_ref=dst.at[k],
            send_sem=send.at[JOB_SEMS * t + k], recv_sem=recv.at[JOB_SEMS * t + k], device_id=(px, py, c),
            device_id_type=MESH) for k, (px, py) in enumerate(chips)]
        return local, sends

    def start(self, *a):
        local, sends = self._copies(*a)
        local.start()
        for cp in sends:
            cp.start()

    def relay(self, *a):
        pass

    def finish(self, *a):
        local, sends = self._copies(*a)
        for cp in sends:
            cp.wait_recv()
        for cp in sends:
            cp.wait_send()
        local.wait()


class _SiblingJob:
    def __init__(self, arr):
        self.src, self.out_shape = arr, jax.ShapeDtypeStruct(arr.shape, arr.dtype)

    def _copy(self, src, dst, send, recv, loc, t):
        x, y, c = lax.axis_index("x"), lax.axis_index("y"), lax.axis_index("c")
        return pltpu.make_async_remote_copy(
            src_ref=src, dst_ref=dst, send_sem=send.at[JOB_SEMS * t], recv_sem=recv.at[JOB_SEMS * t],
            device_id=(x, y, 1 - c), device_id_type=MESH)

    def start(self, *a):
        self._copy(*a).start()

    def relay(self, *a):
        pass

    def finish(self, *a):
        self._copy(*a).wait()


JOB_SEMS = 8


def _run_jobs(phase, jobs, srcs, dsts, sems):
    for t, job in enumerate(jobs):
        getattr(job, phase)(srcs[t], dsts[t], *sems, t)


def _job_scratch(n):
    return [pltpu.SemaphoreType.DMA((JOB_SEMS * n,)), pltpu.SemaphoreType.DMA((JOB_SEMS * n,)),
            pltpu.SemaphoreType.DMA((n,))]


def _pcall(body, *, name, grid, in_specs, out_specs, out_shape, sem, args, scratch_shapes=(), jobs=()):
    in_specs, out_specs, out_shape = list(in_specs), list(out_specs), list(out_shape)
    scratch_shapes = list(scratch_shapes)
    if not jobs:
        res = pl.pallas_call(body, name=name, grid=grid, in_specs=in_specs, out_specs=out_specs, out_shape=out_shape,
                             scratch_shapes=scratch_shapes, compiler_params=_cp(*sem))(*args)
        return list(res), []
    n_in, n_out, n_scr, nj = len(args), len(out_shape), len(scratch_shapes), len(jobs)
    n_steps = math.prod(grid)
    relay_step = (3 * n_steps) // 4
    relay_early = 0 < relay_step < n_steps - 1

    def wrapped(*refs):
        ins, refs = refs[:n_in], refs[n_in:]
        jin, refs = refs[:nj], refs[nj:]
        outs, refs = refs[:n_out], refs[n_out:]
        jout, refs = refs[:nj], refs[nj:]
        scr, sems = refs[:n_scr], refs[n_scr:]
        step = pl.program_id(0)
        for d in range(1, len(grid)):
            step = step * grid[d] + pl.program_id(d)

        @pl.when(step == 0)
        def _():
            _run_jobs("start", jobs, jin, jout, sems)

        if relay_early:
            @pl.when(step == relay_step)
            def _():
                _run_jobs("relay", jobs, jin, jout, sems)
        body(*ins, *outs, *scr)

        @pl.when(step == n_steps - 1)
        def _():
            if not relay_early:
                _run_jobs("relay", jobs, jin, jout, sems)
            _run_jobs("finish", jobs, jin, jout, sems)

    hbm = pl.BlockSpec(memory_space=pltpu.HBM)
    res = pl.pallas_call(
        wrapped, name=name, grid=grid, in_specs=in_specs + [hbm] * nj, out_specs=out_specs + [hbm] * nj,
        out_shape=out_shape + [job.out_shape for job in jobs], scratch_shapes=scratch_shapes + _job_scratch(nj),
        compiler_params=_cp(*sem))(*args, *[job.src for job in jobs])
    return list(res[:n_out]), list(res[n_out:])


def _hidden_chunks(k):
    return [(c0, min(6 * MXU_DIM, k - c0)) for c0 in range(0, k, 6 * MXU_DIM)]


def _loss_head(xv, gain, tgt, st_ref):
    dm = xv.shape[-1]
    rstd = lax.rsqrt(jnp.mean(xv * xv, axis=-1, keepdims=True) + EPS)
    xn = xv * rstd
    err = xn * gain - tgt
    st_ref[1:2, :] += jnp.full((1, dm), 0.5 / dm, F32) * jnp.sum(err * err)
    dy = err * (1.0 / dm)
    st_ref[0:1, :] += jnp.sum(dy * xn, axis=0, keepdims=True)
    dxn = dy * gain
    return rstd * (dxn - xn * jnp.mean(dxn * xn, axis=-1, keepdims=True))


def _ffn_fwd(x, nrm, w_gu, w_down, name, jobs=(), head=None):
    S, Dm = x.shape
    K = w_down.shape[0]
    tm = _tile(S, 256)

    def body(x_ref, nrm_ref, wgu_ref, wdn_ref, *rest):
        if head is None:
            o_ref, h_ref, gu_ref, y_ref = rest
        else:
            gf_ref, t_ref, o_ref, h_ref, gu_ref, y_ref, st_ref = rest

            @pl.when(pl.program_id(0) == 0)
            def _():
                st_ref[...] = jnp.zeros_like(st_ref)
        xv = x_ref[...]
        rstd = lax.rsqrt(jnp.mean(xv * xv, axis=-1, keepdims=True) + EPS)
        hn = (xv * rstd) * nrm_ref[0:1, :]
        hb = (hn * (1.0 + nrm_ref[2:3, :]) + nrm_ref[1:2, :]).astype(BF16)
        h_ref[...] = hb
        y = jnp.zeros((tm, Dm), F32)
        for c0, cs in _hidden_chunks(K):
            g = _dot(hb, wgu_ref[:, c0:c0 + cs])
            u = _dot(hb, wgu_ref[:, K + c0:K + c0 + cs])
            gu_ref[:, c0:c0 + cs] = g.astype(BF16)
            gu_ref[:, K + c0:K + c0 + cs] = u.astype(BF16)
            y = y + _dot((g * _sigmoid(g) * u).astype(BF16), wdn_ref[c0:c0 + cs, :])
        xo = xv + (0.5 * nrm_ref[3:4, :]) * y
        o_ref[...] = xo if head is None else _loss_head(xo, gf_ref[0:1, :], t_ref[...], st_ref)
        y_ref[...] = y.astype(BF16)

    row = lambda i: (i, 0)
    fix = lambda i: (0, 0)
    in_specs = [pl.BlockSpec((tm, Dm), row), pl.BlockSpec((8, Dm), fix), _resident((Dm, 2 * K)), _resident((K, Dm))]
    out_specs = [pl.BlockSpec((tm, Dm), row), pl.BlockSpec((tm, Dm), row), pl.BlockSpec((tm, 2 * K), row),
                 pl.BlockSpec((tm, Dm), row)]
    out_shape = [jax.ShapeDtypeStruct((S, Dm), F32), jax.ShapeDtypeStruct((S, Dm), BF16),
                 jax.ShapeDtypeStruct((S, 2 * K), BF16), jax.ShapeDtypeStruct((S, Dm), BF16)]
    args = (x, nrm, w_gu, w_down)
    if head is not None:
        in_specs += [pl.BlockSpec((8, Dm), fix), pl.BlockSpec((tm, Dm), row)]
        out_specs.append(pl.BlockSpec((8, Dm), fix))
        out_shape.append(jax.ShapeDtypeStruct((8, Dm), F32))
        args += tuple(head)
    return _pcall(body, name=name, grid=(S // tm,), in_specs=in_specs, out_specs=out_specs, out_shape=out_shape,
                  sem=("arbitrary",), args=args, jobs=jobs)


def _ffn_down_bwd(dxo, gu, y, w, nrm, name, jobs=()):
    S, Dm = dxo.shape
    K = w.shape[0]
    Ka = gu.shape[1]
    coef = 0.5
    tm = _tile(S, 256)
    n_steps = S // tm
    chunks = _hidden_chunks(K)

    def body(dxo_ref, y_ref, w_ref, nrm_ref, a_ref, da_ref, dgate_ref, dw_ref, acc):
        @pl.when(pl.program_id(0) == 0)
        def _():
            dgate_ref[...] = jnp.zeros_like(dgate_ref)
            acc[...] = jnp.zeros_like(acc)

        dxo_v = dxo_ref[...]
        dyb = ((coef * nrm_ref[3:4, :]) * dxo_v).astype(BF16)
        dgate_ref[0:1, :] += jnp.sum(coef * y_ref[...].astype(F32) * dxo_v, axis=0, keepdims=True)
        for c0, cs in chunks:
            dact = _dot_nt(dyb, w_ref[c0:c0 + cs, :])
            g = a_ref[:, c0:c0 + cs].astype(F32)
            u = a_ref[:, K + c0:K + c0 + cs].astype(F32)
            s = _sigmoid(g)
            si = g * s
            da_ref[:, c0:c0 + cs] = (dact * u * (s * (1.0 + g * (1.0 - s)))).astype(BF16)
            da_ref[:, K + c0:K + c0 + cs] = (dact * si).astype(BF16)
            acc[c0:c0 + cs, :] += _dot_tn((si * u).astype(BF16), dyb)

        @pl.when(pl.program_id(0) == n_steps - 1)
        def _():
            dw_ref[...] = acc[...].astype(BF16)

    row = lambda i: (i, 0)
    fix = lambda i: (0, 0)
    return _pcall(
        body, name=name, grid=(n_steps,),
        in_specs=[pl.BlockSpec((tm, Dm), row), pl.BlockSpec((tm, Dm), row), _resident((K, Dm)),
                  pl.BlockSpec((8, Dm), fix), pl.BlockSpec((tm, Ka), row)],
        out_specs=[pl.BlockSpec((tm, Ka), row), pl.BlockSpec((8, Dm), fix), _resident((K, Dm))],
        out_shape=[jax.ShapeDtypeStruct((S, Ka), BF16), jax.ShapeDtypeStruct((8, Dm), F32),
                   jax.ShapeDtypeStruct((K, Dm), BF16)],
        scratch_shapes=[pltpu.VMEM((K, Dm), F32)],
        sem=("arbitrary",), args=(dxo, y, w, nrm, gu), jobs=jobs)


def _atb(a, b, out_dtype, bm, bn, name, jobs=()):
    S, M = a.shape
    N = b.shape[1]
    bk = _tile(S, 1024)
    nk = S // bk

    def body(a_ref, b_ref, o_ref, acc):
        k = pl.program_id(2)

        @pl.when(k == 0)
        def _():
            acc[...] = jnp.zeros_like(acc)
        acc[...] += _dot_tn(a_ref[...], b_ref[...])

        @pl.when(k == nk - 1)
        def _():
            o_ref[...] = acc[...].astype(o_ref.dtype)

    (out,), extra = _pcall(
        body, name=name, grid=(M // bm, N // bn, nk),
        in_specs=[pl.BlockSpec((bk, bm), lambda m, n, k: (k, m)),
                  pl.BlockSpec((bk, bn), lambda m, n, k: (k, n))],
        out_specs=[pl.BlockSpec((bm, bn), lambda m, n, k: (m, n))],
        out_shape=[jax.ShapeDtypeStruct((M, N), out_dtype)],
        scratch_shapes=[pltpu.VMEM((bm, bn), F32)],
        sem=("arbitrary", "arbitrary", "arbitrary"), args=(a, b), jobs=jobs)
    return out, extra


def _norm_bwd(dh, xv, nrm_ref, red_ref):
    rstd = lax.rsqrt(jnp.mean(xv * xv, axis=-1, keepdims=True) + EPS)
    xn = xv * rstd
    gain = nrm_ref[0:1, :]
    hn = xn * gain
    dhn = dh * (1.0 + nrm_ref[2:3, :])
    red_ref[0:1, :] += jnp.sum(dh, axis=0, keepdims=True)
    red_ref[1:2, :] += jnp.sum(dh * hn, axis=0, keepdims=True)
    red_ref[2:3, :] += jnp.sum(dhn * xn, axis=0, keepdims=True)
    dxn = dhn * gain
    return rstd * (dxn - xn * jnp.mean(dxn * xn, axis=-1, keepdims=True))


def _nt_norm_bwd(dout, w, x, nrm, dxo, name, jobs=()):
    S, N = dout.shape
    Dm = w.shape[0]
    tm = _tile(S, 512)

    def body(do_ref, w_ref, x_ref, nrm_ref, dxo_ref, dx_ref, red_ref):
        @pl.when(pl.program_id(0) == 0)
        def _():
            red_ref[...] = jnp.zeros_like(red_ref)
        dh = _dot_nt(do_ref[...], w_ref[...])
        dx_ref[...] = dxo_ref[...] + _norm_bwd(dh, x_ref[...], nrm_ref, red_ref)

    return _pcall(
        body, name=name, grid=(S // tm,),
        in_specs=[pl.BlockSpec((tm, N), lambda i: (i, 0)),
                  _resident((Dm, N)),
                  pl.BlockSpec((tm, Dm), lambda i: (i, 0)),
                  pl.BlockSpec((8, Dm), lambda i: (0, 0)),
                  pl.BlockSpec((tm, Dm), lambda i: (i, 0))],
        out_specs=[pl.BlockSpec((tm, Dm), lambda i: (i, 0)),
                   pl.BlockSpec((8, Dm), lambda i: (0, 0))],
        out_shape=[jax.ShapeDtypeStruct((S, Dm), F32), jax.ShapeDtypeStruct((8, Dm), F32)],
        sem=("arbitrary",), args=(dout, w, x, nrm, dxo), jobs=jobs)


def _alibi_slope(h):
    return float(2.0 ** (-8.0 * (h + 1) / N_Q_HEADS))


def _head_planes(pair_cols):
    lane = lax.broadcasted_iota(jnp.int32, pair_cols.shape, 1)
    low = lane < HEAD_DIM
    h0_lo = jnp.where(low, pair_cols, 0.0)
    h1_hi = jnp.where(low, 0.0, pair_cols)
    h0_hi = pltpu.roll(h0_lo, HEAD_DIM, 1)
    h1_lo = pltpu.roll(h1_hi, HEAD_DIM, 1)
    return ((h0_lo.astype(BF16), h0_hi.astype(BF16)), (h1_lo.astype(BF16), h1_hi.astype(BF16)))


def _band_geometry(first_block):
    qi = lax.broadcasted_iota(jnp.int32, (BLOCK, BLOCK), 0)
    kj = lax.broadcasted_iota(jnp.int32, (BLOCK, BLOCK), 1)
    own = kj <= qi
    dist = jnp.where(own, qi - kj, qi + BLOCK - kj).astype(F32)
    valid = kj <= qi + BLOCK * (1 - first_block)
    return own, dist, valid


def _fold(band, own):
    return jnp.where(own, band[:, BLOCK:], band[:, :BLOCK])


def _unfold(v, own):
    return jnp.concatenate([jnp.where(own, 0.0, v), jnp.where(own, v, 0.0)], axis=1)


def _softmax_band(s, h, geometry, sink):
    own, dist, valid = geometry
    s = jnp.where(valid, s - _alibi_slope(h) * dist, NEG_INF)
    m = jnp.maximum(jnp.max(s, axis=-1, keepdims=True), sink)
    p = jnp.exp(s - m)
    e_sink = jnp.exp(sink - m)
    inv = 1.0 / (jnp.sum(p, axis=-1, keepdims=True) + e_sink)
    return p * inv, e_sink * inv


def _past(cur, prev, s, row):
    return jnp.where(row < s, pltpu.roll(prev, s, 0), pltpu.roll(cur, s, 0))


def _future(cur, nxt, s, row):
    T = cur.shape[0]
    return jnp.where(row >= T - s, pltpu.roll(nxt, T - s, 0), pltpu.roll(cur, T - s, 0))


def _edge_row(v, last):
    T = v.shape[0]
    r8 = lax.broadcasted_iota(jnp.int32, (SUBLANES, v.shape[1]), 0)
    blk = v[T - SUBLANES:, :] if last else v[:SUBLANES, :]
    return jnp.sum(jnp.where(r8 == (SUBLANES - 1 if last else 0), blk, 0.0), axis=0, keepdims=True)


def _lru_gates(lx, lx_prev, small_ref, wa_ref, wx_ref, row, t0):
    xc = (small_ref[4:5, :] + small_ref[3:4, :] * lx + small_ref[2:3, :] * _past(lx, lx_prev, 1, row)
          + small_ref[1:2, :] * _past(lx, lx_prev, 2, row) + small_ref[0:1, :] * _past(lx, lx_prev, 3, row))
    xcb = xc.astype(BF16)
    r = _sigmoid(_dot(xcb, wa_ref[...]) + small_ref[5:6, :])
    ig = _sigmoid(_dot(xcb, wx_ref[...]) + small_ref[6:7, :])
    sp = _softplus_neg(small_ref[7:8, :])
    la = (-LRU_C) * r * sp
    a = jnp.exp(la)
    first = (row + t0) == 0
    mult = jnp.where(first, 1.0, jnp.sqrt(-_expm1(2.0 * la)))
    return xc, xcb, r, ig, sp, a, mult, first


def _mixer_fwd(x, nrm, w_in, w_out, sinks, small, wa, wx, name, jobs=()):
    S, Dm = x.shape
    T = MIX_TILE
    nT = S // T
    nb = T // BLOCK

    def body(x_ref, nrm_ref, w_in_ref, w_out_ref, sink_ref, small_ref, wa_ref, wx_ref,
             xo_ref, h_ref, proj_ref, y_ref, ymo_ref, hp_ref, *carried_state):
        xv = x_ref[...]
        rstd = lax.rsqrt(jnp.mean(xv * xv, axis=-1, keepdims=True) + EPS)
        hn = (xv * rstd) * nrm_ref[0:1, :]
        hb = (hn * (1.0 + nrm_ref[2:3, :]) + nrm_ref[1:2, :]).astype(BF16)
        h_ref[...] = hb
        proj_ref[...] = _dot(hb, w_in_ref[...])
        core(proj_ref, sink_ref, small_ref, wa_ref, wx_ref, y_ref, hp_ref, *carried_state)
        yo = _dot(y_ref[...], w_out_ref[...])
        xo_ref[...] = xv + nrm_ref[3:4, :] * yo
        ymo_ref[...] = yo.astype(BF16)

    def core(proj_ref, sink_ref, small_ref, wa_ref, wx_ref, y_ref, hp_ref, kvp, lxp, zp, hcar):
        i = pl.program_id(0)

        @pl.when(i == 0)
        def _():
            kvp[...] = jnp.zeros_like(kvp)
            lxp[...] = jnp.zeros_like(lxp)
            zp[...] = jnp.zeros_like(zp)
            hcar[...] = jnp.zeros_like(hcar)

        row = lax.broadcasted_iota(jnp.int32, (T, LRU_WIDTH), 0)

        kv = proj_ref[:, C_KV:C_KV + 2 * KV_WIDTH]
        ext = jnp.concatenate([kvp[...], kv], axis=0)
        kx = _head_planes(ext[:, :KV_WIDTH])
        vx = _head_planes(ext[:, KV_WIDTH:])
        first_tile = jnp.where(i == 0, 1, 0)
        units = [(b, pair, e) for b in range(nb) for pair in range(N_Q_HEADS // 2) for e in range(2)]
        geometry = [_band_geometry(first_tile if b == 0 else 0) for b in range(nb)]
        keys = [slice(b * BLOCK, (b + 2) * BLOCK) for b in range(nb)]
        qp = {(b, pair): (proj_ref[b * BLOCK:(b + 1) * BLOCK, pair * LANES:(pair + 1) * LANES] * 0.125).astype(BF16)
              for b in range(nb) for pair in range(N_Q_HEADS // 2)}
        scores = [_fold(_dot_nt(qp[(b, pair)], kx[pair // 2][e][keys[b]]), geometry[b][0]) for b, pair, e in units]
        probs = [_unfold(_softmax_band(s, 2 * pair + e, geometry[b], sink_ref[2 * pair + e])[0],
                         geometry[b][0]).astype(BF16) for s, (b, pair, e) in zip(scores, units)]
        outs = [_dot(p, vx[pair // 2][e][keys[b]]) for p, (b, pair, e) in zip(probs, units)]
        for u in range(0, len(units), 2):
            b, pair, _ = units[u]
            y_ref[b * BLOCK:(b + 1) * BLOCK, pair * LANES:(pair + 1) * LANES] = (outs[u] + outs[u + 1]).astype(BF16)
        kvp[...] = kv[T - BLOCK:, :]

        lx = proj_ref[:, C_LX:C_LX + LRU_WIDTH]
        xc, _, _, ig, _, a, mult, _ = _lru_gates(lx, lxp[...], small_ref, wa_ref, wx_ref, row, i * T)
        lxp[...] = lx
        aa = a
        bb = mult * (ig * xc)
        s = 1
        while s < T:
            a_sh = jnp.where(row >= s, pltpu.roll(aa, s, 0), 1.0)
            b_sh = jnp.where(row >= s, pltpu.roll(bb, s, 0), 0.0)
            bb = aa * b_sh + bb
            aa = aa * a_sh
            s *= 2
        hc = hcar[0:1, :]
        hh = bb + aa * hc
        hp_ref[...] = jnp.where(row < 1, hc, pltpu.roll(hh, 1, 0))
        hcar[...] = jnp.broadcast_to(_edge_row(hh, True), hcar.shape)
        gl, _ = _gelu(proj_ref[:, C_LG:C_LG + LRU_WIDTH])
        y_ref[:, ATTN_WIDTH:ATTN_WIDTH + LRU_WIDTH] = (gl * hh).astype(BF16)

        z = proj_ref[:, C_SC:C_SC + CONV_WIDTH] * proj_ref[:, C_SX:C_SX + CONV_WIDTH]
        c3 = (small_ref[10:11, :] * z + small_ref[9:10, :] * _past(z, zp[...], 1, row)
              + small_ref[8:9, :] * _past(z, zp[...], 2, row))
        zp[...] = z
        y_ref[:, ATTN_WIDTH + LRU_WIDTH:] = (proj_ref[:, C_SB:C_SB + CONV_WIDTH] * c3).astype(BF16)

    fix = lambda i: (0, 0)
    row = lambda i: (i, 0)
    return _pcall(
        body, name=name, grid=(nT,),
        in_specs=[pl.BlockSpec((T, Dm), row), pl.BlockSpec((8, Dm), fix),
                  _resident((Dm, IN_PROJ_WIDTH)), _resident((D_MODEL, Dm)),
                  pl.BlockSpec(memory_space=pltpu.SMEM),
                  pl.BlockSpec((16, LRU_WIDTH), fix),
                  pl.BlockSpec((LRU_WIDTH, LRU_WIDTH), fix),
                  pl.BlockSpec((LRU_WIDTH, LRU_WIDTH), fix)],
        out_specs=[pl.BlockSpec((T, Dm), row), pl.BlockSpec((T, Dm), row), pl.BlockSpec((T, IN_PROJ_WIDTH), row),
                   pl.BlockSpec((T, D_MODEL), row), pl.BlockSpec((T, Dm), row), pl.BlockSpec((T, LRU_WIDTH), row)],
        out_shape=[jax.ShapeDtypeStruct((S, Dm), F32), jax.ShapeDtypeStruct((S, Dm), BF16),
                   jax.ShapeDtypeStruct((S, IN_PROJ_WIDTH), F32), jax.ShapeDtypeStruct((S, D_MODEL), BF16),
                   jax.ShapeDtypeStruct((S, Dm), BF16), jax.ShapeDtypeStruct((S, LRU_WIDTH), F32)],
        scratch_shapes=[pltpu.VMEM((BLOCK, 2 * KV_WIDTH), F32), pltpu.VMEM((T, LRU_WIDTH), F32),
                        pltpu.VMEM((T, CONV_WIDTH), F32), pltpu.VMEM((SUBLANES, LRU_WIDTH), F32)],
        sem=("arbitrary",), args=(x, nrm, w_in, w_out, sinks, small, wa, wx), jobs=jobs)


def _mixer_bwd(x, nrm, dxo, h, proj, ymix, ymo, hprev, w_in, w_out, sinks, small, wa, wx, name, jobs=()):
    S, Dm = x.shape
    T = MIX_TILE
    nT = S // T
    nb = T // BLOCK
    bpt = T // BLOCK

    def body(x_ref, nrm_ref, dxo_ref, h_ref, proj_ref, kvprev_ref, lxprev_ref, scprev_ref, sxprev_ref, ymix_ref,
             ymo_ref, hp_ref, w_in_ref, w_out_ref, sink_ref, small_ref, wa_ref, wx_ref,
             dx_ref, red_ref, dgate_ref, dwo_ref, dwi_ref, dsm_ref, dsink_ref, dwa_ref, dwx_ref,
             dy_s, dp_s, acc_o, acc_i, *carried_state):
        @pl.when(pl.program_id(0) == 0)
        def _():
            for r in (red_ref, dgate_ref, acc_o, acc_i):
                r[...] = jnp.zeros_like(r)

        dxo_v = dxo_ref[...]
        dyb = (nrm_ref[3:4, :] * dxo_v).astype(BF16)
        dgate_ref[0:1, :] += jnp.sum(ymo_ref[...].astype(F32) * dxo_v, axis=0, keepdims=True)
        dy_s[...] = _dot_nt(dyb, w_out_ref[...])
        acc_o[...] += _dot_tn(ymix_ref[...], dyb)
        core(proj_ref, kvprev_ref, lxprev_ref, scprev_ref, sxprev_ref, dy_s, hp_ref, sink_ref, small_ref,
             wa_ref, wx_ref, dp_s, dsm_ref, dsink_ref, dwa_ref, dwx_ref, *carried_state)
        dpb = dp_s[...]
        acc_i[...] += _dot_tn(h_ref[...], dpb)
        dx_ref[...] = dxo_v + _norm_bwd(_dot_nt(dpb, w_in_ref[...]), x_ref[...], nrm_ref, red_ref)

        @pl.when(pl.program_id(0) == nT - 1)
        def _():
            dwo_ref[...] = acc_o[...].astype(BF16)
            dwi_ref[...] = acc_i[...].astype(BF16)

    def core(proj_ref, kvprev_ref, lxprev_ref, scprev_ref, sxprev_ref, dy_ref, hp_ref, sink_ref, small_ref,
             wa_ref, wx_ref, dp_ref, dsm_ref, dsink_ref, dwa_ref, dwx_ref,
             dk_s, dv_s, dkv_c, dxc_n, dc3_n, p_c):
        i = pl.program_id(0)
        ti = nT - 1 - i
        has_prev = jnp.where(ti == 0, 0.0, 1.0)

        @pl.when(i == 0)
        def _():
            for r in (dkv_c, dxc_n, dc3_n, p_c, dsm_ref, dsink_ref, dwa_ref, dwx_ref):
                r[...] = jnp.zeros_like(r)

        row = lax.broadcasted_iota(jnp.int32, (T, LRU_WIDTH), 0)

        kv = proj_ref[:, C_KV:C_KV + 2 * KV_WIDTH]
        ext = jnp.concatenate([kvprev_ref[...] * has_prev, kv], axis=0)
        kx = _head_planes(ext[:, :KV_WIDTH])
        vx = _head_planes(ext[:, KV_WIDTH:])
        dk_s[...] = jnp.zeros_like(dk_s)
        dv_s[...] = jnp.zeros_like(dv_s)
        dk_s[:, T:] = dkv_c[:, :BLOCK]
        dv_s[:, T:] = dkv_c[:, BLOCK:]
        first_tile = jnp.where(ti == 0, 1, 0)
        units = [(b, pair, e) for b in range(nb) for pair in range(N_Q_HEADS // 2) for e in range(2)]
        geometry = [_band_geometry(first_tile if b == 0 else 0) for b in range(nb)]
        keys = [slice(b * BLOCK, (b + 2) * BLOCK) for b in range(nb)]
        tile = {(b, pair): (slice(b * BLOCK, (b + 1) * BLOCK), slice(pair * LANES, (pair + 1) * LANES))
                for b in range(nb) for pair in range(N_Q_HEADS // 2)}
        qp = {k: (proj_ref[rc] * 0.125).astype(BF16) for k, rc in tile.items()}
        dob = {k: dy_ref[rc].astype(BF16) for k, rc in tile.items()}
        qp_t = {k: jnp.transpose(proj_ref[rc] * 0.125).astype(BF16) for k, rc in tile.items()}
        dob_t = {k: jnp.transpose(dy_ref[rc]).astype(BF16) for k, rc in tile.items()}
        scores = [_fold(_dot_nt(qp[(b, pair)], kx[pair // 2][e][keys[b]]), geometry[b][0]) for b, pair, e in units]
        dprob = [_fold(_dot_nt(dob[(b, pair)], vx[pair // 2][e][keys[b]]), geometry[b][0]) for b, pair, e in units]
        pn_wide, ds_wide = [], []
        for s, dpm, (b, pair, e) in zip(scores, dprob, units):
            h = 2 * pair + e
            own = geometry[b][0]
            pn, psink = _softmax_band(s, h, geometry[b], sink_ref[h])
            dsum = jnp.sum(pn * dpm, axis=-1, keepdims=True)
            dsink_ref[h:h + 1, :] += jnp.full((1, LANES), -1.0, F32) * jnp.sum(psink * dsum)
            pn_wide.append(_unfold(pn, own).astype(BF16))
            ds_wide.append(_unfold(pn * (dpm - dsum), own).astype(BF16))
        dq = {}
        for pw, ds, (b, pair, e) in zip(pn_wide, ds_wide, units):
            g = pair // 2
            head_e = slice(e * HEAD_DIM, (e + 1) * HEAD_DIM)
            head_g = slice(g * HEAD_DIM, (g + 1) * HEAD_DIM)
            dv_s[head_g, keys[b]] += _dot(dob_t[(b, pair)], pw)[head_e, :]
            dk_s[head_g, keys[b]] += _dot(qp_t[(b, pair)], ds)[head_e, :]
            part = _dot(ds, kx[g][e][keys[b]])
            dq[(b, pair)] = part if e == 0 else dq[(b, pair)] + part
        for k, rc in tile.items():
            dp_ref[rc] = (0.125 * dq[k]).astype(BF16)
        dp_ref[:, C_KV:C_KV + KV_WIDTH] = jnp.transpose(dk_s[:, BLOCK:]).astype(BF16)
        dp_ref[:, C_KV + KV_WIDTH:C_KV + 2 * KV_WIDTH] = jnp.transpose(dv_s[:, BLOCK:]).astype(BF16)
        dkv_c[:, :BLOCK] = dk_s[:, :BLOCK]
        dkv_c[:, BLOCK:] = dv_s[:, :BLOCK]

        lx = proj_ref[:, C_LX:C_LX + LRU_WIDTH]
        lxprev = lxprev_ref[...] * has_prev
        xc, xcb, r, ig, sp, a, mult, first = _lru_gates(lx, lxprev, small_ref, wa_ref, wx_ref, row, ti * T)
        hp = hp_ref[...]
        hh = a * hp + mult * (ig * xc)
        lg = proj_ref[:, C_LG:C_LG + LRU_WIDTH]
        gl, th = _gelu(lg)
        dyl = dy_ref[:, ATTN_WIDTH:ATTN_WIDTH + LRU_WIDTH]
        dp_ref[:, C_LG:C_LG + LRU_WIDTH] = (dyl * hh * _gelu_grad(lg, th)).astype(BF16)
        aa = jnp.where(row < T - 1, pltpu.roll(a, T - 1, 0), 1.0)
        bb = dyl * gl
        s = 1
        while s < T:
            a_sh = jnp.where(row < T - s, pltpu.roll(aa, T - s, 0), 1.0)
            b_sh = jnp.where(row < T - s, pltpu.roll(bb, T - s, 0), 0.0)
            bb = bb + aa * b_sh
            aa = aa * a_sh
            s *= 2
        G = bb + aa * p_c[0:1, :]
        p_c[...] = jnp.broadcast_to(_edge_row(a * G, False), p_c.shape)
        da = G * hp
        dmult = G * (ig * xc)
        dig = G * mult * xc
        dxc = G * mult * ig
        dla = da * a + dmult * jnp.where(first, 0.0, -(a * a) / mult)
        dr = dla * ((-LRU_C) * sp)
        lam = small_ref[7:8, :]
        dsm_ref[7:8, :] += jnp.sum(dla * ((-LRU_C) * r), axis=0, keepdims=True) * (-_sigmoid(-lam))
        dpa = dr * r * (1.0 - r)
        dpx = dig * ig * (1.0 - ig)
        dsm_ref[5:6, :] += jnp.sum(dpa, axis=0, keepdims=True)
        dsm_ref[6:7, :] += jnp.sum(dpx, axis=0, keepdims=True)
        dpab = dpa.astype(BF16)
        dpxb = dpx.astype(BF16)
        dwa_ref[...] += _dot_tn(xcb, dpab)
        dwx_ref[...] += _dot_tn(xcb, dpxb)
        dxc = dxc + _dot_nt(dpab, wa_ref[...]) + _dot_nt(dpxb, wx_ref[...])
        dsm_ref[4:5, :] += jnp.sum(dxc, axis=0, keepdims=True)
        dsm_ref[3:4, :] += jnp.sum(dxc * lx, axis=0, keepdims=True)
        for k in range(3):
            dsm_ref[k:k + 1, :] += jnp.sum(dxc * _past(lx, lxprev, 3 - k, row), axis=0, keepdims=True)
        nxt = dxc_n[...]
        dlx = (small_ref[3:4, :] * dxc + small_ref[2:3, :] * _future(dxc, nxt, 1, row)
               + small_ref[1:2, :] * _future(dxc, nxt, 2, row) + small_ref[0:1, :] * _future(dxc, nxt, 3, row))
        dxc_n[...] = dxc
        dp_ref[:, C_LX:C_LX + LRU_WIDTH] = dlx.astype(BF16)

        sc = proj_ref[:, C_SC:C_SC + CONV_WIDTH]
        sx = proj_ref[:, C_SX:C_SX + CONV_WIDTH]
        sb = proj_ref[:, C_SB:C_SB + CONV_WIDTH]
        z = sc * sx
        zprev = (scprev_ref[...] * sxprev_ref[...]) * has_prev
        z1 = _past(z, zprev, 1, row)
        z2 = _past(z, zprev, 2, row)
        c3 = small_ref[10:11, :] * z + small_ref[9:10, :] * z1 + small_ref[8:9, :] * z2
        dys = dy_ref[:, ATTN_WIDTH + LRU_WIDTH:]
        dp_ref[:, C_SB:C_SB + CONV_WIDTH] = (dys * c3).astype(BF16)
        dc3 = dys * sb
        dsm_ref[10:11, :] += jnp.sum(dc3 * z, axis=0, keepdims=True)
        dsm_ref[9:10, :] += jnp.sum(dc3 * z1, axis=0, keepdims=True)
        dsm_ref[8:9, :] += jnp.sum(dc3 * z2, axis=0, keepdims=True)
        nxt3 = dc3_n[...]
        dz = (small_ref[10:11, :] * dc3 + small_ref[9:10, :] * _future(dc3, nxt3, 1, row)
              + small_ref[8:9, :] * _future(dc3, nxt3, 2, row))
        dc3_n[...] = dc3
        dp_ref[:, C_SC:C_SC + CONV_WIDTH] = (dz * sx).astype(BF16)
        dp_ref[:, C_SX:C_SX + CONV_WIDTH] = (dz * sc).astype(BF16)

    fix = lambda i: (0, 0)
    cur = lambda i: (nT - 1 - i, 0)
    prev_cols = lambda cb: (lambda i: (jnp.maximum(nT - 2 - i, 0), cb))
    return _pcall(
        body, name=name, grid=(nT,),
        in_specs=[pl.BlockSpec((T, Dm), cur), pl.BlockSpec((8, Dm), fix), pl.BlockSpec((T, Dm), cur),
                  pl.BlockSpec((T, Dm), cur),
                  pl.BlockSpec((T, IN_PROJ_WIDTH), cur),
                  pl.BlockSpec((BLOCK, 2 * KV_WIDTH),
                               lambda i: (jnp.maximum((nT - 1 - i) * bpt - 1, 0), C_KV // (2 * KV_WIDTH))),
                  pl.BlockSpec((T, LRU_WIDTH), prev_cols(C_LX // LRU_WIDTH)),
                  pl.BlockSpec((T, CONV_WIDTH), prev_cols(C_SC // CONV_WIDTH)),
                  pl.BlockSpec((T, CONV_WIDTH), prev_cols(C_SX // CONV_WIDTH)),
                  pl.BlockSpec((T, D_MODEL), cur), pl.BlockSpec((T, Dm), cur),
                  pl.BlockSpec((T, LRU_WIDTH), cur),
                  _resident((Dm, IN_PROJ_WIDTH)), _resident((D_MODEL, Dm)),
                  pl.BlockSpec(memory_space=pltpu.SMEM),
                  pl.BlockSpec((16, LRU_WIDTH), fix),
                  pl.BlockSpec((LRU_WIDTH, LRU_WIDTH), fix),
                  pl.BlockSpec((LRU_WIDTH, LRU_WIDTH), fix)],
        out_specs=[pl.BlockSpec((T, Dm), cur), pl.BlockSpec((8, Dm), fix), pl.BlockSpec((8, Dm), fix),
                   _resident((D_MODEL, Dm)), _resident((Dm, IN_PROJ_WIDTH)),
                   pl.BlockSpec((16, LRU_WIDTH), fix),
                   pl.BlockSpec((SUBLANES, LANES), fix),
                   pl.BlockSpec((LRU_WIDTH, LRU_WIDTH), fix),
                   pl.BlockSpec((LRU_WIDTH, LRU_WIDTH), fix)],
        out_shape=[jax.ShapeDtypeStruct((S, Dm), F32), jax.ShapeDtypeStruct((8, Dm), F32),
                   jax.ShapeDtypeStruct((8, Dm), F32),
                   jax.ShapeDtypeStruct((D_MODEL, Dm), BF16), jax.ShapeDtypeStruct((Dm, IN_PROJ_WIDTH), BF16),
                   jax.ShapeDtypeStruct((16, LRU_WIDTH), F32),
                   jax.ShapeDtypeStruct((SUBLANES, LANES), F32),
                   jax.ShapeDtypeStruct((LRU_WIDTH, LRU_WIDTH), F32),
                   jax.ShapeDtypeStruct((LRU_WIDTH, LRU_WIDTH), F32)],
        scratch_shapes=[pltpu.VMEM((T, D_MODEL), F32), pltpu.VMEM((T, IN_PROJ_WIDTH), BF16),
                        pltpu.VMEM((D_MODEL, Dm), F32), pltpu.VMEM((Dm, IN_PROJ_WIDTH), F32),
                        pltpu.VMEM((KV_WIDTH, T + BLOCK), F32), pltpu.VMEM((KV_WIDTH, T + BLOCK), F32),
                        pltpu.VMEM((BLOCK, 2 * KV_WIDTH), F32), pltpu.VMEM((T, LRU_WIDTH), F32),
                        pltpu.VMEM((T, CONV_WIDTH), F32), pltpu.VMEM((SUBLANES, LRU_WIDTH), F32)],
        sem=("arbitrary",),
        args=(x, nrm, dxo, h, proj, proj, proj, proj, proj, ymix, ymo, hprev, w_in, w_out, sinks, small, wa, wx),
        jobs=jobs)


def _adamw_update(g, w_ref, m_ref, v_ref, go_ref, d_ref, mo_ref, vo_ref):
    mn = ADAM_B1 * m_ref[...] + (1.0 - ADAM_B1) * g
    vn = ADAM_B2 * v_ref[...] + (1.0 - ADAM_B2) * (g * g)
    go_ref[...] = g
    mo_ref[...] = mn
    vo_ref[...] = vn
    m_hat = mn / (1.0 - ADAM_B1 ** ADAM_STEP)
    v_hat = vn / (1.0 - ADAM_B2 ** ADAM_STEP)
    d_ref[...] = (-ADAM_LR) * (m_hat / (jnp.sqrt(v_hat) + ADAM_EPS) + ADAM_WD * w_ref[...])


def _adamw(w, g, m, v, name):
    R, C = w.shape
    tr = 8
    for cand in (512, 256, 128, 64, 32, 16, 8):
        if R % cand == 0 and cand * C * 4 <= (1 << 20):
            tr = cand
            break

    def body(w_ref, g_ref, *rest):
        _adamw_update(g_ref[...], w_ref, *rest)

    spec = pl.BlockSpec((tr, C), lambda i: (i, 0))
    return _pcall(body, name=name, grid=(R // tr,), in_specs=[spec] * 4, out_specs=[spec] * 4,
                  out_shape=[jax.ShapeDtypeStruct((R, C), F32)] * 4, sem=("arbitrary",), args=(w, g, m, v))


def _adamw_many(ws, gs, ms, vs, name):
    n = len(ws)

    def body(*refs):
        w_r, g_r, m_r, v_r, go, do, mo, vo = (refs[k * n:(k + 1) * n] for k in range(8))
        for t in range(n):
            _adamw_update(g_r[t][...], w_r[t], m_r[t], v_r[t], go[t], do[t], mo[t], vo[t])

    vmem = pl.BlockSpec(memory_space=pltpu.VMEM)
    res = pl.pallas_call(
        body, name=name, in_specs=[vmem] * (4 * n), out_specs=[vmem] * (4 * n),
        out_shape=[jax.ShapeDtypeStruct(w.shape, F32) for w in ws] * 4,
        compiler_params=pltpu.CompilerParams(vmem_limit_bytes=VMEM_LIMIT),
    )(*ws, *gs, *ms, *vs)
    return [res[k * n:(k + 1) * n] for k in range(4)]


def _adamw_partials(w, partials, m, v, name):
    nl = len(partials)
    _, R, C = partials[0][0].shape
    tr = 8
    for cand in (256, 128, 64, 32, 16):
        if R % cand == 0 and cand * C * 4 <= (1 << 19):
            tr = cand
            break
    ni = R // tr

    def body(*refs):
        w_ref, p_refs = refs[0], refs[1:1 + 2 * nl]
        m_ref, v_ref, go_ref, d_ref, mo_ref, vo_ref = refs[1 + 2 * nl:]
        for l in range(nl):
            @pl.when(pl.program_id(0) == l)
            def _(pair=p_refs[2 * l:2 * l + 2]):
                own, sib = [((p[0].astype(F32) + p[1].astype(F32)) + p[2].astype(F32)) + p[3].astype(F32)
                            for p in pair]
                _adamw_update(own + sib, w_ref, m_ref, v_ref, go_ref, d_ref, mo_ref, vo_ref)

    def slots(l):
        return pl.BlockSpec((N_CHIPS, tr, C),
                            lambda ll, i: (0, jnp.where(ll == l, i, jnp.where(ll < l, 0, ni - 1)), 0))

    spec = pl.BlockSpec((tr, C), lambda ll, i: (ll * ni + i, 0))
    return pl.pallas_call(
        body, name=name, grid=(nl, ni),
        in_specs=[spec] + [slots(l) for l in range(nl) for _ in range(2)] + [spec, spec], out_specs=[spec] * 4,
        out_shape=[jax.ShapeDtypeStruct((nl * R, C), F32)] * 4,
        compiler_params=_cp("arbitrary", "arbitrary"),
    )(w, *[p for pair in partials for p in pair], m, v)


GATHER_SEMS = 7


def _two_level_gather(x_ref, out_ref, send_sems, recv_sems, local_sem, base=0):
    M = x_ref.shape[0]
    x, y, c = lax.axis_index("x"), lax.axis_index("y"), lax.axis_index("c")
    me, sibling = (x, y, c), (x, y, 1 - c)
    chips = [(1 - x, y), (x, 1 - y), (1 - x, 1 - y)]

    def rows(px, py, pc):
        return out_ref.at[pl.ds(pl.multiple_of((4 * px + 2 * py + pc) * M, SUBLANES), M), :]

    def copy(k, block, to, src=None):
        return pltpu.make_async_remote_copy(
            src_ref=rows(*block) if src is None else src, dst_ref=rows(*block),
            send_sem=send_sems.at[base + k], recv_sem=recv_sems.at[base + k], device_id=to, device_id_type=MESH)

    mine = pltpu.make_async_copy(x_ref, rows(*me), local_sem)
    mine.start()
    first = [copy(0, me, sibling, src=x_ref)]
    first += [copy(1 + j, me, (*chip, c), src=x_ref) for j, chip in enumerate(chips)]
    for cp in first:
        cp.start()
    passed = [copy(4 + j, (*chip, c), sibling) for j, chip in enumerate(chips)]
    for j, chip in enumerate(chips):
        copy(1 + j, (*chip, c), me).wait_recv()
        passed[j].start()
    copy(0, sibling, me).wait_recv()
    for j, chip in enumerate(chips):
        copy(4 + j, (*chip, 1 - c), me).wait_recv()
    for cp in first + passed:
        cp.wait_send()
    mine.wait()


def _prologue(pack, w_mod, jobs, name):
    M = pack.shape[0]
    L, Dm, N = w_mod.shape
    nj = len(jobs)
    rows_c = Dm // LANES
    tn = 768

    def body(*refs):
        pack_ref, w_ref = refs[:2]
        jin, refs = refs[2:2 + nj], refs[2 + nj:]
        g_ref, ca_ref, mod_ref = refs[:3]
        jout, refs = refs[3:3 + nj], refs[3 + nj:]
        part_ref, send_sems, recv_sems, local_sem, *jsems = refs
        _run_jobs("start", jobs, jin, jout, jsems)
        _two_level_gather(pack_ref, g_ref, send_sems, recv_sems, local_sem.at[0], 0)
        ca_ref[...] = jnp.zeros_like(ca_ref)
        for r in range(rows_c):
            cv = g_ref[pl.ds(r, N_DEV, stride=M), :]
            ca_ref[0:N_DEV, r * LANES:(r + 1) * LANES] = (cv * _sigmoid(cv)).astype(BF16)
        ca = ca_ref[...]
        for l in range(L):
            for n0 in range(0, N, tn):
                part_ref[l * 16:(l + 1) * 16, n0:n0 + tn] = _dot(ca, w_ref[l, :, n0:n0 + tn].astype(BF16))
        _two_level_gather(part_ref, mod_ref, send_sems, recv_sems, local_sem.at[1], GATHER_SEMS)
        _run_jobs("relay", jobs, jin, jout, jsems)
        _run_jobs("finish", jobs, jin, jout, jsems)

    vmem = pl.BlockSpec(memory_space=pltpu.VMEM)
    hbm = pl.BlockSpec(memory_space=pltpu.HBM)
    res = pl.pallas_call(
        body, name=name,
        out_shape=[jax.ShapeDtypeStruct((N_DEV * M, LANES), F32), jax.ShapeDtypeStruct((16, Dm), BF16),
                   jax.ShapeDtypeStruct((N_DEV * L * 16, N), F32)] + [job.out_shape for job in jobs],
        in_specs=[vmem, vmem] + [hbm] * nj, out_specs=[vmem, vmem, vmem] + [hbm] * nj,
        scratch_shapes=[pltpu.VMEM((L * 16, N), F32), pltpu.SemaphoreType.DMA((2 * GATHER_SEMS,)),
                        pltpu.SemaphoreType.DMA((2 * GATHER_SEMS,)), pltpu.SemaphoreType.DMA((2,))]
        + _job_scratch(nj),
        compiler_params=pltpu.CompilerParams(vmem_limit_bytes=VMEM_LIMIT),
    )(pack, w_mod, *[job.src for job in jobs])
    return res[0], res[1], res[2], list(res[3:])


def _all_gather_small(v, name):
    M, N = v.shape

    def body(x_ref, out_ref, sum_ref, send_sems, recv_sems, local_sem):
        _two_level_gather(x_ref, out_ref, send_sems, recv_sems, local_sem)
        acc = out_ref[0:M, :]
        for d in range(1, N_DEV):
            acc = acc + out_ref[d * M:(d + 1) * M, :]
        sum_ref[...] = acc

    return pl.pallas_call(
        body, name=name,
        out_shape=[jax.ShapeDtypeStruct((N_DEV * M, N), F32), jax.ShapeDtypeStruct((M, N), F32)],
        in_specs=[pl.BlockSpec(memory_space=pltpu.VMEM)],
        out_specs=[pl.BlockSpec(memory_space=pltpu.VMEM), pl.BlockSpec(memory_space=pltpu.VMEM)],
        scratch_shapes=[pltpu.SemaphoreType.DMA((GATHER_SEMS,)), pltpu.SemaphoreType.DMA((GATHER_SEMS,)),
                        pltpu.SemaphoreType.DMA],
        compiler_params=pltpu.CompilerParams(vmem_limit_bytes=VMEM_LIMIT),
    )(v)


_BIG = (("w_ffn1_gu", 1), ("w_ffn1_down", 0), ("w_ffn2_gu", 1), ("w_ffn2_down", 0), ("w_in", 1), ("w_out", 0))
_AXIS = dict(_BIG)

_GATHER_PLAN = {
    "first": [(0, "w_ffn1_gu"), (0, "w_ffn1_down")],
    (0, "ffn1"): [(0, "w_in"), (0, "w_out"), (0, "w_ffn2_gu")],
    (0, "mix"): [(0, "w_ffn2_down")],
    (0, "ffn2"): [(1, "w_ffn1_gu"), (1, "w_ffn1_down")],
    (1, "ffn1"): [(1, "w_in"), (1, "w_out"), (1, "w_ffn2_gu")],
    (1, "mix"): [(1, "w_ffn2_down")],
}


def _pack(arrs, rows_multiple=SUBLANES):
    flat = jnp.concatenate([a.astype(F32).reshape(-1) for a in arrs])
    unit = rows_multiple * LANES
    total = -(-flat.shape[0] // unit) * unit
    return jnp.pad(flat, (0, total - flat.shape[0])).reshape(total // LANES, LANES)


def _unpack(flat, shapes):
    out, off = [], 0
    for shp in shapes:
        n = int(math.prod(shp))
        out.append(flat[off:off + n].reshape(shp))
        off += n
    return out


def _block_diag(w):
    out = jnp.zeros((LRU_WIDTH, LRU_WIDTH), F32)
    for h in range(4):
        out = lax.dynamic_update_slice(out, w[h], (h * HEAD_DIM, h * HEAD_DIM))
    return out


def _diag_blocks(w):
    return jnp.stack([w[h * HEAD_DIM:(h + 1) * HEAD_DIM, h * HEAD_DIM:(h + 1) * HEAD_DIM] for h in range(4)])


def _rows8(*rows):
    z = jnp.zeros((8 - len(rows), rows[0].shape[-1]), F32)
    return jnp.concatenate([jnp.stack(rows), z], axis=0)


def kernel(x, c, w_mod, b_mod, g_norm, w_ffn1_gu, w_ffn1_down, w_ffn2_gu, w_ffn2_down, w_in, w_out, attn_sinks, lru_conv_w, lru_conv_b, lru_gate_a_w, lru_gate_a_b, lru_gate_x_w, lru_gate_x_b, lru_lambda, sc_conv_w, g_final, loss_target, m_w_mod, m_b_mod, m_g_norm, m_w_ffn1_gu, m_w_ffn1_down, m_w_ffn2_gu, m_w_ffn2_down, m_w_in, m_w_out, m_attn_sinks, m_lru_conv_w, m_lru_conv_b, m_lru_gate_a_w, m_lru_gate_a_b, m_lru_gate_x_w, m_lru_gate_x_b, m_lru_lambda, m_sc_conv_w, m_g_final, v_w_mod, v_b_mod, v_g_norm, v_w_ffn1_gu, v_w_ffn1_down, v_w_ffn2_gu, v_w_ffn2_down, v_w_in, v_w_out, v_attn_sinks, v_lru_conv_w, v_lru_conv_b, v_lru_gate_a_w, v_lru_gate_a_b, v_lru_gate_x_w, v_lru_gate_x_b, v_lru_lambda, v_sc_conv_w, v_g_final):
    W = dict(w_mod=w_mod, b_mod=b_mod, g_norm=g_norm, w_ffn1_gu=w_ffn1_gu, w_ffn1_down=w_ffn1_down,
             w_ffn2_gu=w_ffn2_gu, w_ffn2_down=w_ffn2_down, w_in=w_in, w_out=w_out, attn_sinks=attn_sinks,
             lru_conv_w=lru_conv_w, lru_conv_b=lru_conv_b, lru_gate_a_w=lru_gate_a_w, lru_gate_a_b=lru_gate_a_b,
             lru_gate_x_w=lru_gate_x_w, lru_gate_x_b=lru_gate_x_b, lru_lambda=lru_lambda, sc_conv_w=sc_conv_w,
             g_final=g_final)
    M1 = dict(w_mod=m_w_mod, b_mod=m_b_mod, g_norm=m_g_norm, w_ffn1_gu=m_w_ffn1_gu, w_ffn1_down=m_w_ffn1_down,
              w_ffn2_gu=m_w_ffn2_gu, w_ffn2_down=m_w_ffn2_down, w_in=m_w_in, w_out=m_w_out,
              attn_sinks=m_attn_sinks, lru_conv_w=m_lru_conv_w, lru_conv_b=m_lru_conv_b,
              lru_gate_a_w=m_lru_gate_a_w, lru_gate_a_b=m_lru_gate_a_b, lru_gate_x_w=m_lru_gate_x_w,
              lru_gate_x_b=m_lru_gate_x_b, lru_lambda=m_lru_lambda, sc_conv_w=m_sc_conv_w, g_final=m_g_final)
    V1 = dict(w_mod=v_w_mod, b_mod=v_b_mod, g_norm=v_g_norm, w_ffn1_gu=v_w_ffn1_gu, w_ffn1_down=v_w_ffn1_down,
              w_ffn2_gu=v_w_ffn2_gu, w_ffn2_down=v_w_ffn2_down, w_in=v_w_in, w_out=v_w_out,
              attn_sinks=v_attn_sinks, lru_conv_w=v_lru_conv_w, lru_conv_b=v_lru_conv_b,
              lru_gate_a_w=v_lru_gate_a_w, lru_gate_a_b=v_lru_gate_a_b, lru_gate_x_w=v_lru_gate_x_w,
              lru_gate_x_b=v_lru_gate_x_b, lru_lambda=v_lru_lambda, sc_conv_w=v_sc_conv_w, g_final=v_g_final)
    names = ["w_mod", "b_mod", "g_norm", "w_ffn1_gu", "w_ffn1_down", "w_ffn2_gu", "w_ffn2_down", "w_in", "w_out",
             "attn_sinks", "lru_conv_w", "lru_conv_b", "lru_gate_a_w", "lru_gate_a_b", "lru_gate_x_w",
             "lru_gate_x_b", "lru_lambda", "sc_conv_w", "g_final"]

    xs = x[0]
    tgt = loss_target[0]
    S = xs.shape[0]
    chip = 2 * lax.axis_index("x") + lax.axis_index("y")
    batch = 2 * chip + lax.axis_index("c")
    L = DEPTH

    full = {}

    def gather_jobs(key):
        return [_GatherJob(W[n][l].astype(BF16), _AXIS[n]) for l, n in _GATHER_PLAN.get(key, ())]

    def landed(key, outs):
        full.update(zip(_GATHER_PLAN.get(key, ()), outs))

    fwd_shapes = [(D_MODEL,), g_norm.shape, lru_conv_w.shape, sc_conv_w.shape]
    gathered, c_act, mod_all, ex = _prologue(_pack([c[0], g_norm, lru_conv_w, sc_conv_w]), w_mod,
                                             gather_jobs("first"), "prologue")
    landed("first", ex)
    gathered = gathered.reshape(N_DEV, -1)
    per_chip = [_unpack(gathered[2 * jj], fwd_shapes) for jj in range(N_CHIPS)]
    g_norm_full = jnp.concatenate([p[1] for p in per_chip], axis=-1)
    lru_conv_w_full = jnp.concatenate([p[2] for p in per_chip], axis=-1)
    sc_conv_w_full = jnp.concatenate([p[3] for p in per_chip], axis=-1)
    mod_all = mod_all.reshape(N_DEV, L, 16, -1)
    mod_rows = [lax.dynamic_index_in_dim(mod_all[2 * jj], batch, axis=1, keepdims=False) for jj in range(N_CHIPS)]
    mod = (jnp.concatenate(mod_rows, axis=-1) + b_mod).reshape(L, 9, D_MODEL)

    def nrm_rows(l, s):
        return _rows8(g_norm_full[l, s], mod[l, 3 * s], mod[l, 3 * s + 1], mod[l, 3 * s + 2])

    def mixer_params(l):
        small = jnp.concatenate([lru_conv_w_full[l], lru_conv_b[l][None], lru_gate_a_b[l][None],
                                 lru_gate_x_b[l][None], lru_lambda[l][None], sc_conv_w_full[l],
                                 jnp.zeros((5, LRU_WIDTH), F32)], axis=0)
        return (attn_sinks[l], small, _block_diag(lru_gate_a_w[l]).astype(BF16),
                _block_diag(lru_gate_x_w[l]).astype(BF16))

    saved = []
    xcur = xs
    for l in range(L):
        n1, n2, n3 = nrm_rows(l, 0), nrm_rows(l, 1), nrm_rows(l, 2)

        def ffn(which, xin, nrm, head=None):
            key = (l, which)
            (xo, h, gu, y, *stats), ex = _ffn_fwd(xin, nrm, full[(l, f"w_{which}_gu")], full[(l, f"w_{which}_down")],
                                                  f"l{l}_{which}", gather_jobs(key), head)
            landed(key, ex)
            return (xo, *stats), (xin, h, gu, y)

        (x1,), s1 = ffn("ffn1", xcur, n1)
        mp = mixer_params(l)
        (x2, h2, proj, ymix, ymo, hprev), ex = _mixer_fwd(x1, n2, full[(l, "w_in")], full[(l, "w_out")], *mp,
                                                          f"l{l}_mix", gather_jobs((l, "mix")))
        landed((l, "mix"), ex)
        s2 = (x1, h2, proj, ymix, ymo, hprev, mp)
        (xcur, *stats), s3 = ffn("ffn2", x2, n3, (_rows8(g_final), tgt) if l == L - 1 else None)
        saved.append((n1, n2, n3, s1, s2, s3))

    dx, stats = xcur, stats[0]
    loss_here, d_g_final = stats[1, 0:1], stats[0]

    recv, theirs = {}, {}
    waiting = []

    def carried(fn, *a, extra=()):
        items = waiting + list(extra)
        waiting.clear()
        outs, landed_now = fn(*a, jobs=[_SiblingJob(recv[(ll, n)]) if g is None else _ScatterJob(g, _AXIS[n])
                                        for ll, n, g in items])
        for (ll, n, g), arr in zip(items, landed_now):
            if g is None:
                theirs[(ll, n)] = arr
            else:
                recv[(ll, n)] = arr
                waiting.append((ll, n, None))
        return outs

    dmod, d_gnorm, d_small = [None] * L, [None] * L, [None] * L
    for l in reversed(range(L)):
        n1, n2, n3, s1, s2, s3 = saved[l]

        def plain(fn, *a):
            return fn(*a)[0]

        def ffn_bwd(which, dxo, sv, nrm, last):
            xin, h, gu, y = sv
            tag = f"l{l}_{which}"
            dgu, dgate, dw_down = carried(
                _ffn_down_bwd, dxo, gu, y, full[(l, f"w_{which}_down")], nrm, tag + "_down_bwd")
            dw_gu = carried(_atb, h, dgu, BF16, 1024, 2816, tag + "_dw_gu", extra=[(l, f"w_{which}_down", dw_down)])
            mine = [(l, f"w_{which}_gu", dw_gu)]
            dxi, red = (carried if last else plain)(
                _nt_norm_bwd, dgu, full[(l, f"w_{which}_gu")], xin, nrm, dxo, tag + "_gu_bwd",
                **(dict(extra=mine) if last else {}))
            if not last:
                waiting.extend(mine)
            return dxi, (red[0], red[1], dgate[0]), red[2]

        dx, dm3, dg3 = ffn_bwd("ffn2", dx, s3, n3, False)
        x_in, h2, proj, ymix, ymo, hprev, mp = s2
        dx, red, dgate, dw_out, dw_in, dsm, dsink, dwa, dwx = carried(
            _mixer_bwd, x_in, n2, dx, h2, proj, ymix, ymo, hprev, full[(l, "w_in")], full[(l, "w_out")], *mp,
            f"l{l}_mix_bwd")
        waiting.extend([(l, "w_out", dw_out), (l, "w_in", dw_in)])
        dm2, dg2 = (red[0], red[1], dgate[0]), red[2]
        dx, dm1, dg1 = ffn_bwd("ffn1", dx, s1, n1, l == 0)
        dmod[l] = jnp.stack(list(dm1) + list(dm2) + list(dm3))
        d_gnorm[l] = jnp.stack([dg1, dg2, dg3])
        d_small[l] = (dsink[:, 0], dsm[0:4], dsm[4], _diag_blocks(dwa), dsm[5], _diag_blocks(dwx), dsm[6],
                      dsm[7], dsm[8:11])
    grad_x = dx[None]

    def both(k):
        return jnp.stack([d_small[0][k], d_small[1][k]])
    small_names = ["g_norm", "attn_sinks", "lru_conv_w", "lru_conv_b", "lru_gate_a_w", "lru_gate_a_b",
                   "lru_gate_x_w", "lru_gate_x_b", "lru_lambda", "sc_conv_w", "g_final"]
    small_parts = [jnp.stack(d_gnorm)] + [both(k) for k in range(9)] + [d_g_final]
    dmod_flat = jnp.stack(dmod).reshape(-1)
    bwd_gathered, bwd_sum = _all_gather_small(_pack([dmod_flat] + small_parts + [loss_here]), "gather_small_bwd")
    n_mod = dmod_flat.shape[0]
    dmod_all = bwd_gathered.reshape(N_DEV, -1)[:, :n_mod].reshape(N_DEV, L, 9 * D_MODEL)
    bwd_sum = bwd_sum.reshape(-1)
    G = {"b_mod": bwd_sum[:n_mod].reshape(L, 9 * D_MODEL)}
    *small_sums, loss = _unpack(bwd_sum[n_mod:], [p.shape for p in small_parts] + [(1,)])
    loss = loss[0]
    G.update(zip(small_names, small_sums))
    for n in ("g_norm", "lru_conv_w", "sc_conv_w"):
        wdt = W[n].shape[-1]
        G[n] = lax.dynamic_slice_in_dim(G[n], chip * wdt, wdt, axis=G[n].ndim - 1)

    ncol = w_mod.shape[-1]
    dmod_cols = lax.dynamic_slice_in_dim(dmod_all, chip * ncol, ncol, axis=2)
    zeros8 = jnp.zeros((N_DEV, ncol), F32)
    g_w_mod = jnp.stack([carried(_atb, c_act, jnp.concatenate([dmod_cols[:, l], zeros8], axis=0).astype(BF16), F32,
                                 D_MODEL, 768, f"l{l}_dw_mod") for l in range(L)])

    out_g, out_d, out_m, out_v = {}, {}, {}, {}
    res, _ = _adamw(w_mod.reshape(-1, ncol), g_w_mod.reshape(-1, ncol), m_w_mod.reshape(-1, ncol),
                    v_w_mod.reshape(-1, ncol), "adamw_w_mod")
    out_g["w_mod"], out_d["w_mod"], out_m["w_mod"], out_v["w_mod"] = [r.reshape(w_mod.shape) for r in res]
    for n, _ in _BIG:
        shp = W[n].shape
        flat = (shp[0] * shp[1], shp[2])
        res = _adamw_partials(W[n].reshape(flat), [(recv[(l, n)], theirs[(l, n)]) for l in range(L)],
                              M1[n].reshape(flat), V1[n].reshape(flat), f"adamw_{n}")
        out_g[n], out_d[n], out_m[n], out_v[n] = [r.reshape(shp) for r in res]
    rest = ["b_mod"] + small_names

    def rows(a):
        return a.reshape(-1, a.shape[-1])
    res = _adamw_many([rows(W[n]) for n in rest], [rows(G[n]) for n in rest], [rows(M1[n]) for n in rest],
                      [rows(V1[n]) for n in rest], "adamw_small")
    for dst, group in zip((out_g, out_d, out_m, out_v), res):
        dst.update({n: r.reshape(W[n].shape) for n, r in zip(rest, group)})

    return (loss, grad_x, *[out_g[n] for n in names], *[out_d[n] for n in names],
            *[out_m[n] for n in names], *[out_v[n] for n in names])
```

```python
import math

import jax
import jax.numpy as jnp
from jax import lax
from jax.experimental import pallas as pl
from jax.experimental.pallas import tpu as pltpu

F32 = jnp.float32
BF16 = jnp.bfloat16

D_MODEL = 1024
DEPTH = 2
HEAD_DIM = 64
N_Q_HEADS = 8
ATTN_WIDTH = 512
KV_WIDTH = 128
LRU_WIDTH = 256
CONV_WIDTH = 256
IN_PROJ_WIDTH = 2048
BLOCK = 128
D_FF = 2816
EPS = 1e-6
NEG_INF = -1e30
LRU_C = 8.0
N_CHIPS = 4
N_DEV = 8

C_Q, C_KV, C_LX, C_LG, C_SB, C_SC, C_SX = 0, 512, 768, 1024, 1280, 1536, 1792

ADAM_LR = 0.001
ADAM_B1 = 0.9
ADAM_B2 = 0.999
ADAM_EPS = 1e-08
ADAM_WD = 0.01
ADAM_STEP = 10

LANES = 128
SUBLANES = 8
VMEM_LIMIT = 56 * 1024 * 1024
MIX_TILE = 256

MESH = pl.DeviceIdType.MESH


def _cp(*sem):
    return pltpu.CompilerParams(dimension_semantics=sem, vmem_limit_bytes=VMEM_LIMIT)


def _tile(n, pref):
    t = min(n, pref)
    while n % t:
        t //= 2
    return t


MXU_DIM = 256


def _resident(shape):
    return pl.BlockSpec(shape, lambda *_: (0, 0), pipeline_mode=pl.Buffered(1))


def _sigmoid(v):
    return 1.0 / (1.0 + jnp.exp(-v))


def _expm1(v):
    series = v * (1.0 + v * (0.5 + v * (1.0 / 6.0)))
    return jnp.where(v > -0.01, series, jnp.exp(v) - 1.0)


def _softplus_neg(lam):
    e = jnp.exp(-jnp.abs(lam))
    log1p = jnp.where(e < 1e-2, e * (1.0 - e * (0.5 - e * (1.0 / 3.0))), jnp.log(1.0 + e))
    return jnp.maximum(-lam, 0.0) + log1p


_GELU_K = math.sqrt(2.0 / math.pi)
_GELU_C = 0.044715


def _gelu(v):
    t = jnp.tanh(_GELU_K * (v + _GELU_C * v * v * v))
    return 0.5 * v * (1.0 + t), t


def _gelu_grad(v, t):
    return 0.5 * (1.0 + t) + 0.5 * v * (1.0 - t * t) * _GELU_K * (1.0 + 3.0 * _GELU_C * v * v)


def _dot(a, b):
    return jnp.dot(a, b, preferred_element_type=F32)


def _dot_nt(a, b):
    return lax.dot_general(a, b, (((1,), (1,)), ((), ())), preferred_element_type=F32)


def _dot_tn(a, b):
    return lax.dot_general(a, b, (((0,), (0,)), ((), ())), preferred_element_type=F32)


def _window(ref, axis, j, width):
    start = pl.multiple_of(j * width, LANES if axis == 1 else 16)
    if axis == 1:
        return ref.at[:, pl.ds(start, width)]
    return ref.at[pl.ds(start, width), :]


def _chip_peers():
    x, y, c = lax.axis_index("x"), lax.axis_index("y"), lax.axis_index("c")
    return x, y, c, [(1 - x, y), (x, 1 - y), (1 - x, 1 - y)]


class _GatherJob:
    def __init__(self, shard, axis):
        self.src, self.axis, self.width, self.half = shard, axis, shard.shape[axis], shard.shape[0] // 2
        full = tuple(d * N_CHIPS if k == axis else d for k, d in enumerate(shard.shape))
        self.out_shape = jax.ShapeDtypeStruct(full, shard.dtype)

    def _piece(self, ref, j, hf):
        if self.axis == 1:
            return ref.at[pl.ds(pl.multiple_of(hf * self.half, 16), self.half),
                          pl.ds(pl.multiple_of(j * self.width, LANES), self.width)]
        return ref.at[pl.ds(pl.multiple_of(j * self.width + hf * self.half, 16), self.half), :]

    def _copies(self, src, dst, send, recv, loc, t):
        x, y, c, chips = _chip_peers()
        j = 2 * x + y
        owners = [2 * px + py for px, py in chips]
        local = pltpu.make_async_copy(src, _window(dst, self.axis, j, self.width), loc.at[t])
        mine = src.at[pl.ds(pl.multiple_of(c * self.half, 16), self.half), :]

        def ici(k, owner):
            return pltpu.make_async_remote_copy(
                src_ref=mine, dst_ref=self._piece(dst, owner, c), send_sem=send.at[JOB_SEMS * t + k],
                recv_sem=recv.at[JOB_SEMS * t + k], device_id=(*chips[k], c), device_id_type=MESH)

        def relay(k, hf):
            piece = self._piece(dst, owners[k], hf)
            return pltpu.make_async_remote_copy(
                src_ref=piece, dst_ref=piece, send_sem=send.at[JOB_SEMS * t + 4 + k],
                recv_sem=recv.at[JOB_SEMS * t + 4 + k],
                device_id=(x, y, 1 - c), device_id_type=MESH)

        return (local, [ici(k, j) for k in range(3)], [ici(k, owners[k]) for k in range(3)],
                [relay(k, c) for k in range(3)], [relay(k, 1 - c) for k in range(3)])

    def start(self, *a):
        local, ici_out, _, _, _ = self._copies(*a)
        local.start()
        for cp in ici_out:
            cp.start()

    def relay(self, *a):
        _, _, ici_in, relay_out, _ = self._copies(*a)
        for arrived, onward in zip(ici_in, relay_out):
            arrived.wait_recv()
            onward.start()

    def finish(self, *a):
        local, ici_out, _, relay_out, relay_in = self._copies(*a)
        for cp in relay_in:
            cp.wait_recv()
        for cp in ici_out + relay_out:
            cp.wait_send()
        local.wait()


class _ScatterJob:
    def __init__(self, full, axis):
        self.src, self.axis, self.width = full, axis, full.shape[axis] // N_CHIPS
        shard = tuple(self.width if k == axis else d for k, d in enumerate(full.shape))
        self.out_shape = jax.ShapeDtypeStruct((N_CHIPS,) + shard, full.dtype)

    def _copies(self, src, dst, send, recv, loc, t):
        x, y, c, chips = _chip_peers()
        local = pltpu.make_async_copy(_window(src, self.axis, 2 * x + y, self.width), dst.at[3], loc.at[t])
        sends = [pltpu.make_async_remote_copy(
            src_ref=_window(src, self.axis, 2 * px + py, self.width), dst_ref=dst.at[k],
            send_sem=send.at[JOB_SEMS * t + k], recv_sem=recv.at[JOB_SEMS * t + k], device_id=(px, py, c),
            device_id_type=MESH) for k, (px, py) in enumerate(chips)]
        return local, sends

    def start(self, *a):
        local, sends = self._copies(*a)
        local.start()
        for cp in sends:
            cp.start()

    def relay(self, *a):
        pass

    def finish(self, *a):
        local, sends = self._copies(*a)
        for cp in sends:
            cp.wait_recv()
        for cp in sends:
            cp.wait_send()
        local.wait()


class _SiblingJob:
    def __init__(self, arr):
        self.src, self.out_shape = arr, jax.ShapeDtypeStruct(arr.shape, arr.dtype)

    def _copy(self, src, dst, send, recv, loc, t):
        x, y, c = lax.axis_index("x"), lax.axis_index("y"), lax.axis_index("c")
        return pltpu.make_async_remote_copy(
            src_ref=src, dst_ref=dst, send_sem=send.at[JOB_SEMS * t], recv_sem=recv.at[JOB_SEMS * t],
            device_id=(x, y, 1 - c), device_id_type=MESH)

    def start(self, *a):
        self._copy(*a).start()

    def relay(self, *a):
        pass

    def finish(self, *a):
        self._copy(*a).wait()


JOB_SEMS = 8


def _run_jobs(phase, jobs, srcs, dsts, sems):
    for t, job in enumerate(jobs):
        getattr(job, phase)(srcs[t], dsts[t], *sems, t)


def _job_scratch(n):
    return [pltpu.SemaphoreType.DMA((JOB_SEMS * n,)), pltpu.SemaphoreType.DMA((JOB_SEMS * n,)),
            pltpu.SemaphoreType.DMA((n,))]


def _pcall(body, *, name, grid, in_specs, out_specs, out_shape, sem, args, scratch_shapes=(), jobs=()):
    in_specs, out_specs, out_shape = list(in_specs), list(out_specs), list(out_shape)
    scratch_shapes = list(scratch_shapes)
    if not jobs:
        res = pl.pallas_call(body, name=name, grid=grid, in_specs=in_specs, out_specs=out_specs, out_shape=out_shape,
                             scratch_shapes=scratch_shapes, compiler_params=_cp(*sem))(*args)
        return list(res), []
    n_in, n_out, n_scr, nj = len(args), len(out_shape), len(scratch_shapes), len(jobs)
    n_steps = math.prod(grid)
    relay_step = (3 * n_steps) // 4
    relay_early = 0 < relay_step < n_steps - 1

    def wrapped(*refs):
        ins, refs = refs[:n_in], refs[n_in:]
        jin, refs = refs[:nj], refs[nj:]
        outs, refs = refs[:n_out], refs[n_out:]
        jout, refs = refs[:nj], refs[nj:]
        scr, sems = refs[:n_scr], refs[n_scr:]
        step = pl.program_id(0)
        for d in range(1, len(grid)):
            step = step * grid[d] + pl.program_id(d)

        @pl.when(step == 0)
        def _():
            _run_jobs("start", jobs, jin, jout, sems)

        if relay_early:
            @pl.when(step == relay_step)
            def _():
                _run_jobs("relay", jobs, jin, jout, sems)
        body(*ins, *outs, *scr)

        @pl.when(step == n_steps - 1)
        def _():
            if not relay_early:
                _run_jobs("relay", jobs, jin, jout, sems)
            _run_jobs("finish", jobs, jin, jout, sems)

    hbm = pl.BlockSpec(memory_space=pltpu.HBM)
    res = pl.pallas_call(
        wrapped, name=name, grid=grid, in_specs=in_specs + [hbm] * nj, out_specs=out_specs + [hbm] * nj,
        out_shape=out_shape + [job.out_shape for job in jobs], scratch_shapes=scratch_shapes + _job_scratch(nj),
        compiler_params=_cp(*sem))(*args, *[job.src for job in jobs])
    return list(res[:n_out]), list(res[n_out:])


def _hidden_chunks(k):
    return [(c0, min(6 * MXU_DIM, k - c0)) for c0 in range(0, k, 6 * MXU_DIM)]


def _loss_head(xv, gain, tgt, st_ref):
    dm = xv.shape[-1]
    rstd = lax.rsqrt(jnp.mean(xv * xv, axis=-1, keepdims=True) + EPS)
    xn = xv * rstd
    err = xn * gain - tgt
    st_ref[1:2, :] += jnp.full((1, dm), 0.5 / dm, F32) * jnp.sum(err * err)
    dy = err * (1.0 / dm)
    st_ref[0:1, :] += jnp.sum(dy * xn, axis=0, keepdims=True)
    dxn = dy * gain
    return rstd * (dxn - xn * jnp.mean(dxn * xn, axis=-1, keepdims=True))


def _ffn_fwd(x, nrm, w_gu, w_down, name, jobs=(), head=None):
    S, Dm = x.shape
    K = w_down.shape[0]
    tm = _tile(S, 256)

    def body(x_ref, nrm_ref, wgu_ref, wdn_ref, *rest):
        if head is None:
            o_ref, h_ref, gu_ref, y_ref = rest
        else:
            gf_ref, t_ref, o_ref, h_ref, gu_ref, y_ref, st_ref = rest

            @pl.when(pl.program_id(0) == 0)
            def _():
                st_ref[...] = jnp.zeros_like(st_ref)
        xv = x_ref[...]
        rstd = lax.rsqrt(jnp.mean(xv * xv, axis=-1, keepdims=True) + EPS)
        hn = (xv * rstd) * nrm_ref[0:1, :]
        hb = (hn * (1.0 + nrm_ref[2:3, :]) + nrm_ref[1:2, :]).astype(BF16)
        h_ref[...] = hb
        y = jnp.zeros((tm, Dm), F32)
        for c0, cs in _hidden_chunks(K):
            g = _dot(hb, wgu_ref[:, c0:c0 + cs])
            u = _dot(hb, wgu_ref[:, K + c0:K + c0 + cs])
            gu_ref[:, c0:c0 + cs] = g.astype(BF16)
            gu_ref[:, K + c0:K + c0 + cs] = u.astype(BF16)
            y = y + _dot((g * _sigmoid(g) * u).astype(BF16), wdn_ref[c0:c0 + cs, :])
        xo = xv + (0.5 * nrm_ref[3:4, :]) * y
        o_ref[...] = xo if head is None else _loss_head(xo, gf_ref[0:1, :], t_ref[...], st_ref)
        y_ref[...] = y.astype(BF16)

    row = lambda i: (i, 0)
    fix = lambda i: (0, 0)
    in_specs = [pl.BlockSpec((tm, Dm), row), pl.BlockSpec((8, Dm), fix), _resident((Dm, 2 * K)), _resident((K, Dm))]
    out_specs = [pl.BlockSpec((tm, Dm), row), pl.BlockSpec((tm, Dm), row), pl.BlockSpec((tm, 2 * K), row),
                 pl.BlockSpec((tm, Dm), row)]
    out_shape = [jax.ShapeDtypeStruct((S, Dm), F32), jax.ShapeDtypeStruct((S, Dm), BF16),
                 jax.ShapeDtypeStruct((S, 2 * K), BF16), jax.ShapeDtypeStruct((S, Dm), BF16)]
    args = (x, nrm, w_gu, w_down)
    if head is not None:
        in_specs += [pl.BlockSpec((8, Dm), fix), pl.BlockSpec((tm, Dm), row)]
        out_specs.append(pl.BlockSpec((8, Dm), fix))
        out_shape.append(jax.ShapeDtypeStruct((8, Dm), F32))
        args += tuple(head)
    return _pcall(body, name=name, grid=(S // tm,), in_specs=in_specs, out_specs=out_specs, out_shape=out_shape,
                  sem=("arbitrary",), args=args, jobs=jobs)


def _ffn_down_bwd(dxo, gu, y, w, nrm, name, jobs=()):
    S, Dm = dxo.shape
    K = w.shape[0]
    Ka = gu.shape[1]
    coef = 0.5
    tm = _tile(S, 256)
    n_steps = S // tm
    chunks = _hidden_chunks(K)

    def body(dxo_ref, y_ref, w_ref, nrm_ref, a_ref, da_ref, dgate_ref, dw_ref, acc):
        @pl.when(pl.program_id(0) == 0)
        def _():
            dgate_ref[...] = jnp.zeros_like(dgate_ref)
            acc[...] = jnp.zeros_like(acc)

        dxo_v = dxo_ref[...]
        dyb = ((coef * nrm_ref[3:4, :]) * dxo_v).astype(BF16)
        dgate_ref[0:1, :] += jnp.sum(coef * y_ref[...].astype(F32) * dxo_v, axis=0, keepdims=True)
        for c0, cs in chunks:
            dact = _dot_nt(dyb, w_ref[c0:c0 + cs, :])
            g = a_ref[:, c0:c0 + cs].astype(F32)
            u = a_ref[:, K + c0:K + c0 + cs].astype(F32)
            s = _sigmoid(g)
            si = g * s
            da_ref[:, c0:c0 + cs] = (dact * u * (s * (1.0 + g * (1.0 - s)))).astype(BF16)
            da_ref[:, K + c0:K + c0 + cs] = (dact * si).astype(BF16)
            acc[c0:c0 + cs, :] += _dot_tn((si * u).astype(BF16), dyb)

        @pl.when(pl.program_id(0) == n_steps - 1)
        def _():
            dw_ref[...] = acc[...].astype(BF16)

    row = lambda i: (i, 0)
    fix = lambda i: (0, 0)
    return _pcall(
        body, name=name, grid=(n_steps,),
        in_specs=[pl.BlockSpec((tm, Dm), row), pl.BlockSpec((tm, Dm), row), _resident((K, Dm)),
                  pl.BlockSpec((8, Dm), fix), pl.BlockSpec((tm, Ka), row)],
        out_specs=[pl.BlockSpec((tm, Ka), row), pl.BlockSpec((8, Dm), fix), _resident((K, Dm))],
        out_shape=[jax.ShapeDtypeStruct((S, Ka), BF16), jax.ShapeDtypeStruct((8, Dm), F32),
                   jax.ShapeDtypeStruct((K, Dm), BF16)],
        scratch_shapes=[pltpu.VMEM((K, Dm), F32)],
        sem=("arbitrary",), args=(dxo, y, w, nrm, gu), jobs=jobs)


def _atb(a, b, out_dtype, bm, bn, name, jobs=()):
    S, M = a.shape
    N = b.shape[1]
    bk = _tile(S, 1024)
    nk = S // bk

    def body(a_ref, b_ref, o_ref, acc):
        k = pl.program_id(2)

        @pl.when(k == 0)
        def _():
            acc[...] = jnp.zeros_like(acc)
        acc[...] += _dot_tn(a_ref[...], b_ref[...])

        @pl.when(k == nk - 1)
        def _():
            o_ref[...] = acc[...].astype(o_ref.dtype)

    (out,), extra = _pcall(
        body, name=name, grid=(M // bm, N // bn, nk),
        in_specs=[pl.BlockSpec((bk, bm), lambda m, n, k: (k, m)),
                  pl.BlockSpec((bk, bn), lambda m, n, k: (k, n))],
        out_specs=[pl.BlockSpec((bm, bn), lambda m, n, k: (m, n))],
        out_shape=[jax.ShapeDtypeStruct((M, N), out_dtype)],
        scratch_shapes=[pltpu.VMEM((bm, bn), F32)],
        sem=("arbitrary", "arbitrary", "arbitrary"), args=(a, b), jobs=jobs)
    return out, extra


def _norm_bwd(dh, xv, nrm_ref, red_ref):
    rstd = lax.rsqrt(jnp.mean(xv * xv, axis=-1, keepdims=True) + EPS)
    xn = xv * rstd
    gain = nrm_ref[0:1, :]
    hn = xn * gain
    dhn = dh * (1.0 + nrm_ref[2:3, :])
    red_ref[0:1, :] += jnp.sum(dh, axis=0, keepdims=True)
    red_ref[1:2, :] += jnp.sum(dh * hn, axis=0, keepdims=True)
    red_ref[2:3, :] += jnp.sum(dhn * xn, axis=0, keepdims=True)
    dxn = dhn * gain
    return rstd * (dxn - xn * jnp.mean(dxn * xn, axis=-1, keepdims=True))


def _nt_norm_bwd(dout, w, x, nrm, dxo, name, jobs=()):
    S, N = dout.shape
    Dm = w.shape[0]
    tm = _tile(S, 512)

    def body(do_ref, w_ref, x_ref, nrm_ref, dxo_ref, dx_ref, red_ref):
        @pl.when(pl.program_id(0) == 0)
        def _():
            red_ref[...] = jnp.zeros_like(red_ref)
        dh = _dot_nt(do_ref[...], w_ref[...])
        dx_ref[...] = dxo_ref[...] + _norm_bwd(dh, x_ref[...], nrm_ref, red_ref)

    return _pcall(
        body, name=name, grid=(S // tm,),
        in_specs=[pl.BlockSpec((tm, N), lambda i: (i, 0)),
                  _resident((Dm, N)),
                  pl.BlockSpec((tm, Dm), lambda i: (i, 0)),
                  pl.BlockSpec((8, Dm), lambda i: (0, 0)),
                  pl.BlockSpec((tm, Dm), lambda i: (i, 0))],
        out_specs=[pl.BlockSpec((tm, Dm), lambda i: (i, 0)),
                   pl.BlockSpec((8, Dm), lambda i: (0, 0))],
        out_shape=[jax.ShapeDtypeStruct((S, Dm), F32), jax.ShapeDtypeStruct((8, Dm), F32)],
        sem=("arbitrary",), args=(dout, w, x, nrm, dxo), jobs=jobs)


def _alibi_slope(h):
    return float(2.0 ** (-8.0 * (h + 1) / N_Q_HEADS))


def _head_planes(pair_cols):
    lane = lax.broadcasted_iota(jnp.int32, pair_cols.shape, 1)
    low = lane < HEAD_DIM
    h0_lo = jnp.where(low, pair_cols, 0.0)
    h1_hi = jnp.where(low, 0.0, pair_cols)
    h0_hi = pltpu.roll(h0_lo, HEAD_DIM, 1)
    h1_lo = pltpu.roll(h1_hi, HEAD_DIM, 1)
    return ((h0_lo.astype(BF16), h0_hi.astype(BF16)), (h1_lo.astype(BF16), h1_hi.astype(BF16)))


def _band_geometry(first_block):
    qi = lax.broadcasted_iota(jnp.int32, (BLOCK, BLOCK), 0)
    kj = lax.broadcasted_iota(jnp.int32, (BLOCK, BLOCK), 1)
    own = kj <= qi
    dist = jnp.where(own, qi - kj, qi + BLOCK - kj).astype(F32)
    valid = kj <= qi + BLOCK * (1 - first_block)
    return own, dist, valid


def _fold(band, own):
    return jnp.where(own, band[:, BLOCK:], band[:, :BLOCK])


def _unfold(v, own):
    return jnp.concatenate([jnp.where(own, 0.0, v), jnp.where(own, v, 0.0)], axis=1)


def _softmax_band(s, h, geometry, sink):
    own, dist, valid = geometry
    s = jnp.where(valid, s - _alibi_slope(h) * dist, NEG_INF)
    m = jnp.maximum(jnp.max(s, axis=-1, keepdims=True), sink)
    p = jnp.exp(s - m)
    e_sink = jnp.exp(sink - m)
    inv = 1.0 / (jnp.sum(p, axis=-1, keepdims=True) + e_sink)
    return p * inv, e_sink * inv


def _past(cur, prev, s, row):
    return jnp.where(row < s, pltpu.roll(prev, s, 0), pltpu.roll(cur, s, 0))


def _future(cur, nxt, s, row):
    T = cur.shape[0]
    return jnp.where(row >= T - s, pltpu.roll(nxt, T - s, 0), pltpu.roll(cur, T - s, 0))


def _edge_row(v, last):
    T = v.shape[0]
    r8 = lax.broadcasted_iota(jnp.int32, (SUBLANES, v.shape[1]), 0)
    blk = v[T - SUBLANES:, :] if last else v[:SUBLANES, :]
    return jnp.sum(jnp.where(r8 == (SUBLANES - 1 if last else 0), blk, 0.0), axis=0, keepdims=True)


def _lru_gates(lx, lx_prev, small_ref, wa_ref, wx_ref, row, t0):
    xc = (small_ref[4:5, :] + small_ref[3:4, :] * lx + small_ref[2:3, :] * _past(lx, lx_prev, 1, row)
          + small_ref[1:2, :] * _past(lx, lx_prev, 2, row) + small_ref[0:1, :] * _past(lx, lx_prev, 3, row))
    xcb = xc.astype(BF16)
    r = _sigmoid(_dot(xcb, wa_ref[...]) + small_ref[5:6, :])
    ig = _sigmoid(_dot(xcb, wx_ref[...]) + small_ref[6:7, :])
    sp = _softplus_neg(small_ref[7:8, :])
    la = (-LRU_C) * r * sp
    a = jnp.exp(la)
    first = (row + t0) == 0
    mult = jnp.where(first, 1.0, jnp.sqrt(-_expm1(2.0 * la)))
    return xc, xcb, r, ig, sp, a, mult, first


def _mixer_fwd(x, nrm, w_in, w_out, sinks, small, wa, wx, name, jobs=()):
    S, Dm = x.shape
    T = MIX_TILE
    nT = S // T
    nb = T // BLOCK

    def body(x_ref, nrm_ref, w_in_ref, w_out_ref, sink_ref, small_ref, wa_ref, wx_ref,
             xo_ref, h_ref, proj_ref, y_ref, ymo_ref, hp_ref, *carried_state):
        xv = x_ref[...]
        rstd = lax.rsqrt(jnp.mean(xv * xv, axis=-1, keepdims=True) + EPS)
        hn = (xv * rstd) * nrm_ref[0:1, :]
        hb = (hn * (1.0 + nrm_ref[2:3, :]) + nrm_ref[1:2, :]).astype(BF16)
        h_ref[...] = hb
        proj_ref[...] = _dot(hb, w_in_ref[...])
        core(proj_ref, sink_ref, small_ref, wa_ref, wx_ref, y_ref, hp_ref, *carried_state)
        yo = _dot(y_ref[...], w_out_ref[...])
        xo_ref[...] = xv + nrm_ref[3:4, :] * yo
        ymo_ref[...] = yo.astype(BF16)

    def core(proj_ref, sink_ref, small_ref, wa_ref, wx_ref, y_ref, hp_ref, kvp, lxp, zp, hcar):
        i = pl.program_id(0)

        @pl.when(i == 0)
        def _():
            kvp[...] = jnp.zeros_like(kvp)
            lxp[...] = jnp.zeros_like(lxp)
            zp[...] = jnp.zeros_like(zp)
            hcar[...] = jnp.zeros_like(hcar)

        row = lax.broadcasted_iota(jnp.int32, (T, LRU_WIDTH), 0)

        kv = proj_ref[:, C_KV:C_KV + 2 * KV_WIDTH]
        ext = jnp.concatenate([kvp[...], kv], axis=0)
        kx = _head_planes(ext[:, :KV_WIDTH])
        vx = _head_planes(ext[:, KV_WIDTH:])
        first_tile = jnp.where(i == 0, 1, 0)
        units = [(b, pair, e) for b in range(nb) for pair in range(N_Q_HEADS // 2) for e in range(2)]
        geometry = [_band_geometry(first_tile if b == 0 else 0) for b in range(nb)]
        keys = [slice(b * BLOCK, (b + 2) * BLOCK) for b in range(nb)]
        qp = {(b, pair): (proj_ref[b * BLOCK:(b + 1) * BLOCK, pair * LANES:(pair + 1) * LANES] * 0.125).astype(BF16)
              for b in range(nb) for pair in range(N_Q_HEADS // 2)}
        scores = [_fold(_dot_nt(qp[(b, pair)], kx[pair // 2][e][keys[b]]), geometry[b][0]) for b, pair, e in units]
        probs = [_unfold(_softmax_band(s, 2 * pair + e, geometry[b], sink_ref[2 * pair + e])[0],
                         geometry[b][0]).astype(BF16) for s, (b, pair, e) in zip(scores, units)]
        outs = [_dot(p, vx[pair // 2][e][keys[b]]) for p, (b, pair, e) in zip(probs, units)]
        for u in range(0, len(units), 2):
            b, pair, _ = units[u]
            y_ref[b * BLOCK:(b + 1) * BLOCK, pair * LANES:(pair + 1) * LANES] = (outs[u] + outs[u + 1]).astype(BF16)
        kvp[...] = kv[T - BLOCK:, :]

        lx = proj_ref[:, C_LX:C_LX + LRU_WIDTH]
        xc, _, _, ig, _, a, mult, _ = _lru_gates(lx, lxp[...], small_ref, wa_ref, wx_ref, row, i * T)
        lxp[...] = lx
        aa = a
        bb = mult * (ig * xc)
        s = 1
        while s < T:
            a_sh = jnp.where(row >= s, pltpu.roll(aa, s, 0), 1.0)
            b_sh = jnp.where(row >= s, pltpu.roll(bb, s, 0), 0.0)
            bb = aa * b_sh + bb
            aa = aa * a_sh
            s *= 2
        hc = hcar[0:1, :]
        hh = bb + aa * hc
        hp_ref[...] = jnp.where(row < 1, hc, pltpu.roll(hh, 1, 0))
        hcar[...] = jnp.broadcast_to(_edge_row(hh, True), hcar.shape)
        gl, _ = _gelu(proj_ref[:, C_LG:C_LG + LRU_WIDTH])
        y_ref[:, ATTN_WIDTH:ATTN_WIDTH + LRU_WIDTH] = (gl * hh).astype(BF16)

        z = proj_ref[:, C_SC:C_SC + CONV_WIDTH] * proj_ref[:, C_SX:C_SX + CONV_WIDTH]
        c3 = (small_ref[10:11, :] * z + small_ref[9:10, :] * _past(z, zp[...], 1, row)
              + small_ref[8:9, :] * _past(z, zp[...], 2, row))
        zp[...] = z
        y_ref[:, ATTN_WIDTH + LRU_WIDTH:] = (proj_ref[:, C_SB:C_SB + CONV_WIDTH] * c3).astype(BF16)

    fix = lambda i: (0, 0)
    row = lambda i: (i, 0)
    return _pcall(
        body, name=name, grid=(nT,),
        in_specs=[pl.BlockSpec((T, Dm), row), pl.BlockSpec((8, Dm), fix),
                  _resident((Dm, IN_PROJ_WIDTH)), _resident((D_MODEL, Dm)),
                  pl.BlockSpec(memory_space=pltpu.SMEM),
                  pl.BlockSpec((16, LRU_WIDTH), fix),
                  pl.BlockSpec((LRU_WIDTH, LRU_WIDTH), fix),
                  pl.BlockSpec((LRU_WIDTH, LRU_WIDTH), fix)],
        out_specs=[pl.BlockSpec((T, Dm), row), pl.BlockSpec((T, Dm), row), pl.BlockSpec((T, IN_PROJ_WIDTH), row),
                   pl.BlockSpec((T, D_MODEL), row), pl.BlockSpec((T, Dm), row), pl.BlockSpec((T, LRU_WIDTH), row)],
        out_shape=[jax.ShapeDtypeStruct((S, Dm), F32), jax.ShapeDtypeStruct((S, Dm), BF16),
                   jax.ShapeDtypeStruct((S, IN_PROJ_WIDTH), F32), jax.ShapeDtypeStruct((S, D_MODEL), BF16),
                   jax.ShapeDtypeStruct((S, Dm), BF16), jax.ShapeDtypeStruct((S, LRU_WIDTH), F32)],
        scratch_shapes=[pltpu.VMEM((BLOCK, 2 * KV_WIDTH), F32), pltpu.VMEM((T, LRU_WIDTH), F32),
                        pltpu.VMEM((T, CONV_WIDTH), F32), pltpu.VMEM((SUBLANES, LRU_WIDTH), F32)],
        sem=("arbitrary",), args=(x, nrm, w_in, w_out, sinks, small, wa, wx), jobs=jobs)


def _mixer_bwd(x, nrm, dxo, h, proj, ymix, ymo, hprev, w_in, w_out, sinks, small, wa, wx, name, jobs=()):
    S, Dm = x.shape
    T = MIX_TILE
    nT = S // T
    nb = T // BLOCK
    bpt = T // BLOCK

    def body(x_ref, nrm_ref, dxo_ref, h_ref, proj_ref, kvprev_ref, lxprev_ref, scprev_ref, sxprev_ref, ymix_ref,
             ymo_ref, hp_ref, w_in_ref, w_out_ref, sink_ref, small_ref, wa_ref, wx_ref,
             dx_ref, red_ref, dgate_ref, dwo_ref, dwi_ref, dsm_ref, dsink_ref, dwa_ref, dwx_ref,
             dy_s, dp_s, acc_o, acc_i, *carried_state):
        @pl.when(pl.program_id(0) == 0)
        def _():
            for r in (red_ref, dgate_ref, acc_o, acc_i):
                r[...] = jnp.zeros_like(r)

        dxo_v = dxo_ref[...]
        dyb = (nrm_ref[3:4, :] * dxo_v).astype(BF16)
        dgate_ref[0:1, :] += jnp.sum(ymo_ref[...].astype(F32) * dxo_v, axis=0, keepdims=True)
        dy_s[...] = _dot_nt(dyb, w_out_ref[...])
        acc_o[...] += _dot_tn(ymix_ref[...], dyb)
        core(proj_ref, kvprev_ref, lxprev_ref, scprev_ref, sxprev_ref, dy_s, hp_ref, sink_ref, small_ref,
             wa_ref, wx_ref, dp_s, dsm_ref, dsink_ref, dwa_ref, dwx_ref, *carried_state)
        dpb = dp_s[...]
        acc_i[...] += _dot_tn(h_ref[...], dpb)
        dx_ref[...] = dxo_v + _norm_bwd(_dot_nt(dpb, w_in_ref[...]), x_ref[...], nrm_ref, red_ref)

        @pl.when(pl.program_id(0) == nT - 1)
        def _():
            dwo_ref[...] = acc_o[...].astype(BF16)
            dwi_ref[...] = acc_i[...].astype(BF16)

    def core(proj_ref, kvprev_ref, lxprev_ref, scprev_ref, sxprev_ref, dy_ref, hp_ref, sink_ref, small_ref,
             wa_ref, wx_ref, dp_ref, dsm_ref, dsink_ref, dwa_ref, dwx_ref,
             dk_s, dv_s, dkv_c, dxc_n, dc3_n, p_c):
        i = pl.program_id(0)
        ti = nT - 1 - i
        has_prev = jnp.where(ti == 0, 0.0, 1.0)

        @pl.when(i == 0)
        def _():
            for r in (dkv_c, dxc_n, dc3_n, p_c, dsm_ref, dsink_ref, dwa_ref, dwx_ref):
                r[...] = jnp.zeros_like(r)

        row = lax.broadcasted_iota(jnp.int32, (T, LRU_WIDTH), 0)

        kv = proj_ref[:, C_KV:C_KV + 2 * KV_WIDTH]
        ext = jnp.concatenate([kvprev_ref[...] * has_prev, kv], axis=0)
        kx = _head_planes(ext[:, :KV_WIDTH])
        vx = _head_planes(ext[:, KV_WIDTH:])
        dk_s[...] = jnp.zeros_like(dk_s)
        dv_s[...] = jnp.zeros_like(dv_s)
        dk_s[:, T:] = dkv_c[:, :BLOCK]
        dv_s[:, T:] = dkv_c[:, BLOCK:]
        first_tile = jnp.where(ti == 0, 1, 0)
        units = [(b, pair, e) for b in range(nb) for pair in range(N_Q_HEADS // 2) for e in range(2)]
        geometry = [_band_geometry(first_tile if b == 0 else 0) for b in range(nb)]
        keys = [slice(b * BLOCK, (b + 2) * BLOCK) for b in range(nb)]
        tile = {(b, pair): (slice(b * BLOCK, (b + 1) * BLOCK), slice(pair * LANES, (pair + 1) * LANES))
                for b in range(nb) for pair in range(N_Q_HEADS // 2)}
        qp = {k: (proj_ref[rc] * 0.125).astype(BF16) for k, rc in tile.items()}
        dob = {k: dy_ref[rc].astype(BF16) for k, rc in tile.items()}
        qp_t = {k: jnp.transpose(proj_ref[rc] * 0.125).astype(BF16) for k, rc in tile.items()}
        dob_t = {k: jnp.transpose(dy_ref[rc]).astype(BF16) for k, rc in tile.items()}
        scores = [_fold(_dot_nt(qp[(b, pair)], kx[pair // 2][e][keys[b]]), geometry[b][0]) for b, pair, e in units]
        dprob = [_fold(_dot_nt(dob[(b, pair)], vx[pair // 2][e][keys[b]]), geometry[b][0]) for b, pair, e in units]
        pn_wide, ds_wide = [], []
        for s, dpm, (b, pair, e) in zip(scores, dprob, units):
            h = 2 * pair + e
            own = geometry[b][0]
            pn, psink = _softmax_band(s, h, geometry[b], sink_ref[h])
            dsum = jnp.sum(pn * dpm, axis=-1, keepdims=True)
            dsink_ref[h:h + 1, :] += jnp.full((1, LANES), -1.0, F32) * jnp.sum(psink * dsum)
            pn_wide.append(_unfold(pn, own).astype(BF16))
            ds_wide.append(_unfold(pn * (dpm - dsum), own).astype(BF16))
        dq = {}
        for pw, ds, (b, pair, e) in zip(pn_wide, ds_wide, units):
            g = pair // 2
            head_e = slice(e * HEAD_DIM, (e + 1) * HEAD_DIM)
            head_g = slice(g * HEAD_DIM, (g + 1) * HEAD_DIM)
            dv_s[head_g, keys[b]] += _dot(dob_t[(b, pair)], pw)[head_e, :]
            dk_s[head_g, keys[b]] += _dot(qp_t[(b, pair)], ds)[head_e, :]
            part = _dot(ds, kx[g][e][keys[b]])
            dq[(b, pair)] = part if e == 0 else dq[(b, pair)] + part
        for k, rc in tile.items():
            dp_ref[rc] = (0.125 * dq[k]).astype(BF16)
        dp_ref[:, C_KV:C_KV + KV_WIDTH] = jnp.transpose(dk_s[:, BLOCK:]).astype(BF16)
        dp_ref[:, C_KV + KV_WIDTH:C_KV + 2 * KV_WIDTH] = jnp.transpose(dv_s[:, BLOCK:]).astype(BF16)
        dkv_c[:, :BLOCK] = dk_s[:, :BLOCK]
        dkv_c[:, BLOCK:] = dv_s[:, :BLOCK]

        lx = proj_ref[:, C_LX:C_LX + LRU_WIDTH]
        lxprev = lxprev_ref[...] * has_prev
        xc, xcb, r, ig, sp, a, mult, first = _lru_gates(lx, lxprev, small_ref, wa_ref, wx_ref, row, ti * T)
        hp = hp_ref[...]
        hh = a * hp + mult * (ig * xc)
        lg = proj_ref[:, C_LG:C_LG + LRU_WIDTH]
        gl, th = _gelu(lg)
        dyl = dy_ref[:, ATTN_WIDTH:ATTN_WIDTH + LRU_WIDTH]
        dp_ref[:, C_LG:C_LG + LRU_WIDTH] = (dyl * hh * _gelu_grad(lg, th)).astype(BF16)
        aa = jnp.where(row < T - 1, pltpu.roll(a, T - 1, 0), 1.0)
        bb = dyl * gl
        s = 1
        while s < T:
            a_sh = jnp.where(row < T - s, pltpu.roll(aa, T - s, 0), 1.0)
            b_sh = jnp.where(row < T - s, pltpu.roll(bb, T - s, 0), 0.0)
            bb = bb + aa * b_sh
            aa = aa * a_sh
            s *= 2
        G = bb + aa * p_c[0:1, :]
        p_c[...] = jnp.broadcast_to(_edge_row(a * G, False), p_c.shape)
        da = G * hp
        dmult = G * (ig * xc)
        dig = G * mult * xc
        dxc = G * mult * ig
        dla = da * a + dmult * jnp.where(first, 0.0, -(a * a) / mult)
        dr = dla * ((-LRU_C) * sp)
        lam = small_ref[7:8, :]
        dsm_ref[7:8, :] += jnp.sum(dla * ((-LRU_C) * r), axis=0, keepdims=True) * (-_sigmoid(-lam))
        dpa = dr * r * (1.0 - r)
        dpx = dig * ig * (1.0 - ig)
        dsm_ref[5:6, :] += jnp.sum(dpa, axis=0, keepdims=True)
        dsm_ref[6:7, :] += jnp.sum(dpx, axis=0, keepdims=True)
        dpab = dpa.astype(BF16)
        dpxb = dpx.astype(BF16)
        dwa_ref[...] += _dot_tn(xcb, dpab)
        dwx_ref[...] += _dot_tn(xcb, dpxb)
        dxc = dxc + _dot_nt(dpab, wa_ref[...]) + _dot_nt(dpxb, wx_ref[...])
        dsm_ref[4:5, :] += jnp.sum(dxc, axis=0, keepdims=True)
        dsm_ref[3:4, :] += jnp.sum(dxc * lx, axis=0, keepdims=True)
        for k in range(3):
            dsm_ref[k:k + 1, :] += jnp.sum(dxc * _past(lx, lxprev, 3 - k, row), axis=0, keepdims=True)
        nxt = dxc_n[...]
        dlx = (small_ref[3:4, :] * dxc + small_ref[2:3, :] * _future(dxc, nxt, 1, row)
               + small_ref[1:2, :] * _future(dxc, nxt, 2, row) + small_ref[0:1, :] * _future(dxc, nxt, 3, row))
        dxc_n[...] = dxc
        dp_ref[:, C_LX:C_LX + LRU_WIDTH] = dlx.astype(BF16)

        sc = proj_ref[:, C_SC:C_SC + CONV_WIDTH]
        sx = proj_ref[:, C_SX:C_SX + CONV_WIDTH]
        sb = proj_ref[:, C_SB:C_SB + CONV_WIDTH]
        z = sc * sx
        zprev = (scprev_ref[...] * sxprev_ref[...]) * has_prev
        z1 = _past(z, zprev, 1, row)
        z2 = _past(z, zprev, 2, row)
        c3 = small_ref[10:11, :] * z + small_ref[9:10, :] * z1 + small_ref[8:9, :] * z2
        dys = dy_ref[:, ATTN_WIDTH + LRU_WIDTH:]
        dp_ref[:, C_SB:C_SB + CONV_WIDTH] = (dys * c3).astype(BF16)
        dc3 = dys * sb
        dsm_ref[10:11, :] += jnp.sum(dc3 * z, axis=0, keepdims=True)
        dsm_ref[9:10, :] += jnp.sum(dc3 * z1, axis=0, keepdims=True)
        dsm_ref[8:9, :] += jnp.sum(dc3 * z2, axis=0, keepdims=True)
        nxt3 = dc3_n[...]
        dz = (small_ref[10:11, :] * dc3 + small_ref[9:10, :] * _future(dc3, nxt3, 1, row)
              + small_ref[8:9, :] * _future(dc3, nxt3, 2, row))
        dc3_n[...] = dc3
        dp_ref[:, C_SC:C_SC + CONV_WIDTH] = (dz * sx).astype(BF16)
        dp_ref[:, C_SX:C_SX + CONV_WIDTH] = (dz * sc).astype(BF16)

    fix = lambda i: (0, 0)
    cur = lambda i: (nT - 1 - i, 0)
    prev_cols = lambda cb: (lambda i: (jnp.maximum(nT - 2 - i, 0), cb))
    return _pcall(
        body, name=name, grid=(nT,),
        in_specs=[pl.BlockSpec((T, Dm), cur), pl.BlockSpec((8, Dm), fix), pl.BlockSpec((T, Dm), cur),
                  pl.BlockSpec((T, Dm), cur),
                  pl.BlockSpec((T, IN_PROJ_WIDTH), cur),
                  pl.BlockSpec((BLOCK, 2 * KV_WIDTH),
                               lambda i: (jnp.maximum((nT - 1 - i) * bpt - 1, 0), C_KV // (2 * KV_WIDTH))),
                  pl.BlockSpec((T, LRU_WIDTH), prev_cols(C_LX // LRU_WIDTH)),
                  pl.BlockSpec((T, CONV_WIDTH), prev_cols(C_SC // CONV_WIDTH)),
                  pl.BlockSpec((T, CONV_WIDTH), prev_cols(C_SX // CONV_WIDTH)),
                  pl.BlockSpec((T, D_MODEL), cur), pl.BlockSpec((T, Dm), cur),
                  pl.BlockSpec((T, LRU_WIDTH), cur),
                  _resident((Dm, IN_PROJ_WIDTH)), _resident((D_MODEL, Dm)),
                  pl.BlockSpec(memory_space=pltpu.SMEM),
                  pl.BlockSpec((16, LRU_WIDTH), fix),
                  pl.BlockSpec((LRU_WIDTH, LRU_WIDTH), fix),
                  pl.BlockSpec((LRU_WIDTH, LRU_WIDTH), fix)],
        out_specs=[pl.BlockSpec((T, Dm), cur), pl.BlockSpec((8, Dm), fix), pl.BlockSpec((8, Dm), fix),
                   _resident((D_MODEL, Dm)), _resident((Dm, IN_PROJ_WIDTH)),
                   pl.BlockSpec((16, LRU_WIDTH), fix),
                   pl.BlockSpec((SUBLANES, LANES), fix),
                   pl.BlockSpec((LRU_WIDTH, LRU_WIDTH), fix),
                   pl.BlockSpec((LRU_WIDTH, LRU_WIDTH), fix)],
        out_shape=[jax.ShapeDtypeStruct((S, Dm), F32), jax.ShapeDtypeStruct((8, Dm), F32),
                   jax.ShapeDtypeStruct((8, Dm), F32),
                   jax.ShapeDtypeStruct((D_MODEL, Dm), BF16), jax.ShapeDtypeStruct((Dm, IN_PROJ_WIDTH), BF16),
                   jax.ShapeDtypeStruct((16, LRU_WIDTH), F32),
                   jax.ShapeDtypeStruct((SUBLANES, LANES), F32),
                   jax.ShapeDtypeStruct((LRU_WIDTH, LRU_WIDTH), F32),
                   jax.ShapeDtypeStruct((LRU_WIDTH, LRU_WIDTH), F32)],
        scratch_shapes=[pltpu.VMEM((T, D_MODEL), F32), pltpu.VMEM((T, IN_PROJ_WIDTH), BF16),
                        pltpu.VMEM((D_MODEL, Dm), F32), pltpu.VMEM((Dm, IN_PROJ_WIDTH), F32),
                        pltpu.VMEM((KV_WIDTH, T + BLOCK), F32), pltpu.VMEM((KV_WIDTH, T + BLOCK), F32),
                        pltpu.VMEM((BLOCK, 2 * KV_WIDTH), F32), pltpu.VMEM((T, LRU_WIDTH), F32),
                        pltpu.VMEM((T, CONV_WIDTH), F32), pltpu.VMEM((SUBLANES, LRU_WIDTH), F32)],
        sem=("arbitrary",),
        args=(x, nrm, dxo, h, proj, proj, proj, proj, proj, ymix, ymo, hprev, w_in, w_out, sinks, small, wa, wx),
        jobs=jobs)


def _adamw_update(g, w_ref, m_ref, v_ref, go_ref, d_ref, mo_ref, vo_ref):
    mn = ADAM_B1 * m_ref[...] + (1.0 - ADAM_B1) * g
    vn = ADAM_B2 * v_ref[...] + (1.0 - ADAM_B2) * (g * g)
    go_ref[...] = g
    mo_ref[...] = mn
    vo_ref[...] = vn
    m_hat = mn / (1.0 - ADAM_B1 ** ADAM_STEP)
    v_hat = vn / (1.0 - ADAM_B2 ** ADAM_STEP)
    d_ref[...] = (-ADAM_LR) * (m_hat / (jnp.sqrt(v_hat) + ADAM_EPS) + ADAM_WD * w_ref[...])


def _adamw(w, g, m, v, name):
    R, C = w.shape
    tr = 8
    for cand in (512, 256, 128, 64, 32, 16, 8):
        if R % cand == 0 and cand * C * 4 <= (1 << 20):
            tr = cand
            break

    def body(w_ref, g_ref, *rest):
        _adamw_update(g_ref[...], w_ref, *rest)

    spec = pl.BlockSpec((tr, C), lambda i: (i, 0))
    return _pcall(body, name=name, grid=(R // tr,), in_specs=[spec] * 4, out_specs=[spec] * 4,
                  out_shape=[jax.ShapeDtypeStruct((R, C), F32)] * 4, sem=("arbitrary",), args=(w, g, m, v))


def _adamw_many(ws, gs, ms, vs, name):
    n = len(ws)

    def body(*refs):
        w_r, g_r, m_r, v_r, go, do, mo, vo = (refs[k * n:(k + 1) * n] for k in range(8))
        for t in range(n):
            _adamw_update(g_r[t][...], w_r[t], m_r[t], v_r[t], go[t], do[t], mo[t], vo[t])

    vmem = pl.BlockSpec(memory_space=pltpu.VMEM)
    res = pl.pallas_call(
        body, name=name, in_specs=[vmem] * (4 * n), out_specs=[vmem] * (4 * n),
        out_shape=[jax.ShapeDtypeStruct(w.shape, F32) for w in ws] * 4,
        compiler_params=pltpu.CompilerParams(vmem_limit_bytes=VMEM_LIMIT),
    )(*ws, *gs, *ms, *vs)
    return [res[k * n:(k + 1) * n] for k in range(4)]


def _adamw_partials(w, partials, m, v, name):
    nl = len(partials)
    _, R, C = partials[0][0].shape
    tr = 8
    for cand in (256, 128, 64, 32, 16):
        if R % cand == 0 and cand * C * 4 <= (1 << 19):
            tr = cand
            break
    ni = R // tr

    def body(*refs):
        w_ref, p_refs = refs[0], refs[1:1 + 2 * nl]
        m_ref, v_ref, go_ref, d_ref, mo_ref, vo_ref = refs[1 + 2 * nl:]
        for l in range(nl):
            @pl.when(pl.program_id(0) == l)
            def _(pair=p_refs[2 * l:2 * l + 2]):
                own, sib = [((p[0].astype(F32) + p[1].astype(F32)) + p[2].astype(F32)) + p[3].astype(F32)
                            for p in pair]
                _adamw_update(own + sib, w_ref, m_ref, v_ref, go_ref, d_ref, mo_ref, vo_ref)

    def slots(l):
        return pl.BlockSpec((N_CHIPS, tr, C),
                            lambda ll, i: (0, jnp.where(ll == l, i, jnp.where(ll < l, 0, ni - 1)), 0))

    spec = pl.BlockSpec((tr, C), lambda ll, i: (ll * ni + i, 0))
    return pl.pallas_call(
        body, name=name, grid=(nl, ni),
        in_specs=[spec] + [slots(l) for l in range(nl) for _ in range(2)] + [spec, spec], out_specs=[spec] * 4,
        out_shape=[jax.ShapeDtypeStruct((nl * R, C), F32)] * 4,
        compiler_params=_cp("arbitrary", "arbitrary"),
    )(w, *[p for pair in partials for p in pair], m, v)


GATHER_SEMS = 7


def _two_level_gather(x_ref, out_ref, send_sems, recv_sems, local_sem, base=0):
    M = x_ref.shape[0]
    x, y, c = lax.axis_index("x"), lax.axis_index("y"), lax.axis_index("c")
    me, sibling = (x, y, c), (x, y, 1 - c)
    chips = [(1 - x, y), (x, 1 - y), (1 - x, 1 - y)]

    def rows(px, py, pc):
        return out_ref.at[pl.ds(pl.multiple_of((4 * px + 2 * py + pc) * M, SUBLANES), M), :]

    def copy(k, block, to, src=None):
        return pltpu.make_async_remote_copy(
            src_ref=rows(*block) if src is None else src, dst_ref=rows(*block),
            send_sem=send_sems.at[base + k], recv_sem=recv_sems.at[base + k], device_id=to, device_id_type=MESH)

    mine = pltpu.make_async_copy(x_ref, rows(*me), local_sem)
    mine.start()
    first = [copy(0, me, sibling, src=x_ref)]
    first += [copy(1 + j, me, (*chip, c), src=x_ref) for j, chip in enumerate(chips)]
    for cp in first:
        cp.start()
    passed = [copy(4 + j, (*chip, c), sibling) for j, chip in enumerate(chips)]
    for j, chip in enumerate(chips):
        copy(1 + j, (*chip, c), me).wait_recv()
        passed[j].start()
    copy(0, sibling, me).wait_recv()
    for j, chip in enumerate(chips):
        copy(4 + j, (*chip, 1 - c), me).wait_recv()
    for cp in first + passed:
        cp.wait_send()
    mine.wait()


def _prologue(pack, w_mod, jobs, name):
    M = pack.shape[0]
    L, Dm, N = w_mod.shape
    nj = len(jobs)
    rows_c = Dm // LANES
    tn = 768

    def body(*refs):
        pack_ref, w_ref = refs[:2]
        jin, refs = refs[2:2 + nj], refs[2 + nj:]
        g_ref, ca_ref, mod_ref = refs[:3]
        jout, refs = refs[3:3 + nj], refs[3 + nj:]
        part_ref, send_sems, recv_sems, local_sem, *jsems = refs
        _run_jobs("start", jobs, jin, jout, jsems)
        _two_level_gather(pack_ref, g_ref, send_sems, recv_sems, local_sem.at[0], 0)
        ca_ref[...] = jnp.zeros_like(ca_ref)
        for r in range(rows_c):
            cv = g_ref[pl.ds(r, N_DEV, stride=M), :]
            ca_ref[0:N_DEV, r * LANES:(r + 1) * LANES] = (cv * _sigmoid(cv)).astype(BF16)
        ca = ca_ref[...]
        for l in range(L):
            for n0 in range(0, N, tn):
                part_ref[l * 16:(l + 1) * 16, n0:n0 + tn] = _dot(ca, w_ref[l, :, n0:n0 + tn].astype(BF16))
        _two_level_gather(part_ref, mod_ref, send_sems, recv_sems, local_sem.at[1], GATHER_SEMS)
        _run_jobs("relay", jobs, jin, jout, jsems)
        _run_jobs("finish", jobs, jin, jout, jsems)

    vmem = pl.BlockSpec(memory_space=pltpu.VMEM)
    hbm = pl.BlockSpec(memory_space=pltpu.HBM)
    res = pl.pallas_call(
        body, name=name,
        out_shape=[jax.ShapeDtypeStruct((N_DEV * M, LANES), F32), jax.ShapeDtypeStruct((16, Dm), BF16),
                   jax.ShapeDtypeStruct((N_DEV * L * 16, N), F32)] + [job.out_shape for job in jobs],
        in_specs=[vmem, vmem] + [hbm] * nj, out_specs=[vmem, vmem, vmem] + [hbm] * nj,
        scratch_shapes=[pltpu.VMEM((L * 16, N), F32), pltpu.SemaphoreType.DMA((2 * GATHER_SEMS,)),
                        pltpu.SemaphoreType.DMA((2 * GATHER_SEMS,)), pltpu.SemaphoreType.DMA((2,))]
        + _job_scratch(nj),
        compiler_params=pltpu.CompilerParams(vmem_limit_bytes=VMEM_LIMIT),
    )(pack, w_mod, *[job.src for job in jobs])
    return res[0], res[1], res[2], list(res[3:])


def _all_gather_small(v, name, jobs=()):
    M, N = v.shape
    nj = len(jobs)

    def body(*refs):
        x_ref, jin = refs[0], refs[1:1 + nj]
        out_ref, sum_ref = refs[1 + nj:3 + nj]
        jout, (send_sems, recv_sems, local_sem, *jsems) = refs[3 + nj:3 + 2 * nj], refs[3 + 2 * nj:]
        _run_jobs("start", jobs, jin, jout, jsems)
        _two_level_gather(x_ref, out_ref, send_sems, recv_sems, local_sem)
        acc = out_ref[0:M, :]
        for d in range(1, N_DEV):
            acc = acc + out_ref[d * M:(d + 1) * M, :]
        sum_ref[...] = acc
        _run_jobs("relay", jobs, jin, jout, jsems)
        _run_jobs("finish", jobs, jin, jout, jsems)

    vmem = pl.BlockSpec(memory_space=pltpu.VMEM)
    hbm = pl.BlockSpec(memory_space=pltpu.HBM)
    res = pl.pallas_call(
        body, name=name,
        out_shape=[jax.ShapeDtypeStruct((N_DEV * M, N), F32), jax.ShapeDtypeStruct((M, N), F32)]
        + [job.out_shape for job in jobs],
        in_specs=[vmem] + [hbm] * nj, out_specs=[vmem, vmem] + [hbm] * nj,
        scratch_shapes=[pltpu.SemaphoreType.DMA((GATHER_SEMS,)), pltpu.SemaphoreType.DMA((GATHER_SEMS,)),
                        pltpu.SemaphoreType.DMA] + (_job_scratch(nj) if nj else []),
        compiler_params=pltpu.CompilerParams(vmem_limit_bytes=VMEM_LIMIT),
    )(v, *[job.src for job in jobs])
    return list(res[:2]), list(res[2:])


def _dw_mod(c_act, dmod, name):
    L, R, N = dmod.shape
    Dm = c_act.shape[1]
    tn = 768

    def body(c_ref, d_ref, o_ref):
        o_ref[0] = _dot_tn(c_ref[...], d_ref[0])

    return pl.pallas_call(
        body, name=name, grid=(L, N // tn),
        in_specs=[pl.BlockSpec((R, Dm), lambda l, n: (0, 0)), pl.BlockSpec((1, R, tn), lambda l, n: (l, 0, n))],
        out_specs=pl.BlockSpec((1, Dm, tn), lambda l, n: (l, 0, n)),
        out_shape=jax.ShapeDtypeStruct((L, Dm, N), F32),
        compiler_params=_cp("arbitrary", "arbitrary"),
    )(c_act, dmod)


_BIG = (("w_ffn1_gu", 1), ("w_ffn1_down", 0), ("w_ffn2_gu", 1), ("w_ffn2_down", 0), ("w_in", 1), ("w_out", 0))
_AXIS = dict(_BIG)

_GATHER_PLAN = {
    "first": [(0, "w_ffn1_gu"), (0, "w_ffn1_down")],
    (0, "ffn1"): [(0, "w_in"), (0, "w_out"), (0, "w_ffn2_gu")],
    (0, "mix"): [(0, "w_ffn2_down")],
    (0, "ffn2"): [(1, "w_ffn1_gu"), (1, "w_ffn1_down")],
    (1, "ffn1"): [(1, "w_in"), (1, "w_out"), (1, "w_ffn2_gu")],
    (1, "mix"): [(1, "w_ffn2_down")],
}


def _pack(arrs, rows_multiple=SUBLANES):
    flat = jnp.concatenate([a.astype(F32).reshape(-1) for a in arrs])
    unit = rows_multiple * LANES
    total = -(-flat.shape[0] // unit) * unit
    return jnp.pad(flat, (0, total - flat.shape[0])).reshape(total // LANES, LANES)


def _unpack(flat, shapes):
    out, off = [], 0
    for shp in shapes:
        n = int(math.prod(shp))
        out.append(flat[off:off + n].reshape(shp))
        off += n
    return out


def _block_diag(w):
    out = jnp.zeros((LRU_WIDTH, LRU_WIDTH), F32)
    for h in range(4):
        out = lax.dynamic_update_slice(out, w[h], (h * HEAD_DIM, h * HEAD_DIM))
    return out


def _diag_blocks(w):
    return jnp.stack([w[h * HEAD_DIM:(h + 1) * HEAD_DIM, h * HEAD_DIM:(h + 1) * HEAD_DIM] for h in range(4)])


def _rows8(*rows):
    z = jnp.zeros((8 - len(rows), rows[0].shape[-1]), F32)
    return jnp.concatenate([jnp.stack(rows), z], axis=0)


def kernel(x, c, w_mod, b_mod, g_norm, w_ffn1_gu, w_ffn1_down, w_ffn2_gu, w_ffn2_down, w_in, w_out, attn_sinks, lru_conv_w, lru_conv_b, lru_gate_a_w, lru_gate_a_b, lru_gate_x_w, lru_gate_x_b, lru_lambda, sc_conv_w, g_final, loss_target, m_w_mod, m_b_mod, m_g_norm, m_w_ffn1_gu, m_w_ffn1_down, m_w_ffn2_gu, m_w_ffn2_down, m_w_in, m_w_out, m_attn_sinks, m_lru_conv_w, m_lru_conv_b, m_lru_gate_a_w, m_lru_gate_a_b, m_lru_gate_x_w, m_lru_gate_x_b, m_lru_lambda, m_sc_conv_w, m_g_final, v_w_mod, v_b_mod, v_g_norm, v_w_ffn1_gu, v_w_ffn1_down, v_w_ffn2_gu, v_w_ffn2_down, v_w_in, v_w_out, v_attn_sinks, v_lru_conv_w, v_lru_conv_b, v_lru_gate_a_w, v_lru_gate_a_b, v_lru_gate_x_w, v_lru_gate_x_b, v_lru_lambda, v_sc_conv_w, v_g_final):
    W = dict(w_mod=w_mod, b_mod=b_mod, g_norm=g_norm, w_ffn1_gu=w_ffn1_gu, w_ffn1_down=w_ffn1_down,
             w_ffn2_gu=w_ffn2_gu, w_ffn2_down=w_ffn2_down, w_in=w_in, w_out=w_out, attn_sinks=attn_sinks,
             lru_conv_w=lru_conv_w, lru_conv_b=lru_conv_b, lru_gate_a_w=lru_gate_a_w, lru_gate_a_b=lru_gate_a_b,
             lru_gate_x_w=lru_gate_x_w, lru_gate_x_b=lru_gate_x_b, lru_lambda=lru_lambda, sc_conv_w=sc_conv_w,
             g_final=g_final)
    M1 = dict(w_mod=m_w_mod, b_mod=m_b_mod, g_norm=m_g_norm, w_ffn1_gu=m_w_ffn1_gu, w_ffn1_down=m_w_ffn1_down,
              w_ffn2_gu=m_w_ffn2_gu, w_ffn2_down=m_w_ffn2_down, w_in=m_w_in, w_out=m_w_out,
              attn_sinks=m_attn_sinks, lru_conv_w=m_lru_conv_w, lru_conv_b=m_lru_conv_b,
              lru_gate_a_w=m_lru_gate_a_w, lru_gate_a_b=m_lru_gate_a_b, lru_gate_x_w=m_lru_gate_x_w,
              lru_gate_x_b=m_lru_gate_x_b, lru_lambda=m_lru_lambda, sc_conv_w=m_sc_conv_w, g_final=m_g_final)
    V1 = dict(w_mod=v_w_mod, b_mod=v_b_mod, g_norm=v_g_norm, w_ffn1_gu=v_w_ffn1_gu, w_ffn1_down=v_w_ffn1_down,
              w_ffn2_gu=v_w_ffn2_gu, w_ffn2_down=v_w_ffn2_down, w_in=v_w_in, w_out=v_w_out,
              attn_sinks=v_attn_sinks, lru_conv_w=v_lru_conv_w, lru_conv_b=v_lru_conv_b,
              lru_gate_a_w=v_lru_gate_a_w, lru_gate_a_b=v_lru_gate_a_b, lru_gate_x_w=v_lru_gate_x_w,
              lru_gate_x_b=v_lru_gate_x_b, lru_lambda=v_lru_lambda, sc_conv_w=v_sc_conv_w, g_final=v_g_final)
    names = ["w_mod", "b_mod", "g_norm", "w_ffn1_gu", "w_ffn1_down", "w_ffn2_gu", "w_ffn2_down", "w_in", "w_out",
             "attn_sinks", "lru_conv_w", "lru_conv_b", "lru_gate_a_w", "lru_gate_a_b", "lru_gate_x_w",
             "lru_gate_x_b", "lru_lambda", "sc_conv_w", "g_final"]

    xs = x[0]
    tgt = loss_target[0]
    S = xs.shape[0]
    chip = 2 * lax.axis_index("x") + lax.axis_index("y")
    batch = 2 * chip + lax.axis_index("c")
    L = DEPTH

    full = {}

    def gather_jobs(key):
        return [_GatherJob(W[n][l].astype(BF16), _AXIS[n]) for l, n in _GATHER_PLAN.get(key, ())]

    def landed(key, outs):
        full.update(zip(_GATHER_PLAN.get(key, ()), outs))

    fwd_shapes = [(D_MODEL,), g_norm.shape, lru_conv_w.shape, sc_conv_w.shape]
    gathered, c_act, mod_all, ex = _prologue(_pack([c[0], g_norm, lru_conv_w, sc_conv_w]), w_mod,
                                             gather_jobs("first"), "prologue")
    landed("first", ex)
    gathered = gathered.reshape(N_DEV, -1)
    per_chip = [_unpack(gathered[2 * jj], fwd_shapes) for jj in range(N_CHIPS)]
    g_norm_full = jnp.concatenate([p[1] for p in per_chip], axis=-1)
    lru_conv_w_full = jnp.concatenate([p[2] for p in per_chip], axis=-1)
    sc_conv_w_full = jnp.concatenate([p[3] for p in per_chip], axis=-1)
    mod_all = mod_all.reshape(N_DEV, L, 16, -1)
    mod_rows = [lax.dynamic_index_in_dim(mod_all[2 * jj], batch, axis=1, keepdims=False) for jj in range(N_CHIPS)]
    mod = (jnp.concatenate(mod_rows, axis=-1) + b_mod).reshape(L, 9, D_MODEL)

    def nrm_rows(l, s):
        return _rows8(g_norm_full[l, s], mod[l, 3 * s], mod[l, 3 * s + 1], mod[l, 3 * s + 2])

    def mixer_params(l):
        small = jnp.concatenate([lru_conv_w_full[l], lru_conv_b[l][None], lru_gate_a_b[l][None],
                                 lru_gate_x_b[l][None], lru_lambda[l][None], sc_conv_w_full[l],
                                 jnp.zeros((5, LRU_WIDTH), F32)], axis=0)
        return (attn_sinks[l], small, _block_diag(lru_gate_a_w[l]).astype(BF16),
                _block_diag(lru_gate_x_w[l]).astype(BF16))

    saved = []
    xcur = xs
    for l in range(L):
        n1, n2, n3 = nrm_rows(l, 0), nrm_rows(l, 1), nrm_rows(l, 2)

        def ffn(which, xin, nrm, head=None):
            key = (l, which)
            (xo, h, gu, y, *stats), ex = _ffn_fwd(xin, nrm, full[(l, f"w_{which}_gu")], full[(l, f"w_{which}_down")],
                                                  f"l{l}_{which}", gather_jobs(key), head)
            landed(key, ex)
            return (xo, *stats), (xin, h, gu, y)

        (x1,), s1 = ffn("ffn1", xcur, n1)
        mp = mixer_params(l)
        (x2, h2, proj, ymix, ymo, hprev), ex = _mixer_fwd(x1, n2, full[(l, "w_in")], full[(l, "w_out")], *mp,
                                                          f"l{l}_mix", gather_jobs((l, "mix")))
        landed((l, "mix"), ex)
        s2 = (x1, h2, proj, ymix, ymo, hprev, mp)
        (xcur, *stats), s3 = ffn("ffn2", x2, n3, (_rows8(g_final), tgt) if l == L - 1 else None)
        saved.append((n1, n2, n3, s1, s2, s3))

    dx, stats = xcur, stats[0]
    loss_here, d_g_final = stats[1, 0:1], stats[0]

    recv, theirs = {}, {}
    waiting = []

    def carried(fn, *a, extra=()):
        items = waiting + list(extra)
        waiting.clear()
        outs, landed_now = fn(*a, jobs=[_SiblingJob(recv[(ll, n)]) if g is None else _ScatterJob(g, _AXIS[n])
                                        for ll, n, g in items])
        for (ll, n, g), arr in zip(items, landed_now):
            if g is None:
                theirs[(ll, n)] = arr
            else:
                recv[(ll, n)] = arr
                waiting.append((ll, n, None))
        return outs

    dmod, d_gnorm, d_small = [None] * L, [None] * L, [None] * L
    for l in reversed(range(L)):
        n1, n2, n3, s1, s2, s3 = saved[l]

        def plain(fn, *a):
            return fn(*a)[0]

        def ffn_bwd(which, dxo, sv, nrm, last):
            xin, h, gu, y = sv
            tag = f"l{l}_{which}"
            dgu, dgate, dw_down = carried(
                _ffn_down_bwd, dxo, gu, y, full[(l, f"w_{which}_down")], nrm, tag + "_down_bwd")
            dw_gu = carried(_atb, h, dgu, BF16, 1024, 2816, tag + "_dw_gu", extra=[(l, f"w_{which}_down", dw_down)])
            mine = [(l, f"w_{which}_gu", dw_gu)]
            dxi, red = (carried if last else plain)(
                _nt_norm_bwd, dgu, full[(l, f"w_{which}_gu")], xin, nrm, dxo, tag + "_gu_bwd",
                **(dict(extra=mine) if last else {}))
            if not last:
                waiting.extend(mine)
            return dxi, (red[0], red[1], dgate[0]), red[2]

        dx, dm3, dg3 = ffn_bwd("ffn2", dx, s3, n3, False)
        x_in, h2, proj, ymix, ymo, hprev, mp = s2
        dx, red, dgate, dw_out, dw_in, dsm, dsink, dwa, dwx = carried(
            _mixer_bwd, x_in, n2, dx, h2, proj, ymix, ymo, hprev, full[(l, "w_in")], full[(l, "w_out")], *mp,
            f"l{l}_mix_bwd")
        waiting.extend([(l, "w_out", dw_out), (l, "w_in", dw_in)])
        dm2, dg2 = (red[0], red[1], dgate[0]), red[2]
        dx, dm1, dg1 = ffn_bwd("ffn1", dx, s1, n1, l == 0)
        dmod[l] = jnp.stack(list(dm1) + list(dm2) + list(dm3))
        d_gnorm[l] = jnp.stack([dg1, dg2, dg3])
        d_small[l] = (dsink[:, 0], dsm[0:4], dsm[4], _diag_blocks(dwa), dsm[5], _diag_blocks(dwx), dsm[6],
                      dsm[7], dsm[8:11])
    grad_x = dx[None]

    def both(k):
        return jnp.stack([d_small[0][k], d_small[1][k]])
    small_names = ["g_norm", "attn_sinks", "lru_conv_w", "lru_conv_b", "lru_gate_a_w", "lru_gate_a_b",
                   "lru_gate_x_w", "lru_gate_x_b", "lru_lambda", "sc_conv_w", "g_final"]
    small_parts = [jnp.stack(d_gnorm)] + [both(k) for k in range(9)] + [d_g_final]
    dmod_flat = jnp.stack(dmod).reshape(-1)
    bwd_gathered, bwd_sum = carried(_all_gather_small, _pack([dmod_flat] + small_parts + [loss_here]),
                                    "gather_small_bwd")
    n_mod = dmod_flat.shape[0]
    dmod_all = bwd_gathered.reshape(N_DEV, -1)[:, :n_mod].reshape(N_DEV, L, 9 * D_MODEL)
    bwd_sum = bwd_sum.reshape(-1)
    G = {"b_mod": bwd_sum[:n_mod].reshape(L, 9 * D_MODEL)}
    *small_sums, loss = _unpack(bwd_sum[n_mod:], [p.shape for p in small_parts] + [(1,)])
    loss = loss[0]
    G.update(zip(small_names, small_sums))
    for n in ("g_norm", "lru_conv_w", "sc_conv_w"):
        wdt = W[n].shape[-1]
        G[n] = lax.dynamic_slice_in_dim(G[n], chip * wdt, wdt, axis=G[n].ndim - 1)

    ncol = w_mod.shape[-1]
    dmod_cols = lax.dynamic_slice_in_dim(dmod_all, chip * ncol, ncol, axis=2)
    dmod_rows = jnp.pad(jnp.swapaxes(dmod_cols, 0, 1), ((0, 0), (0, 16 - N_DEV), (0, 0))).astype(BF16)
    g_w_mod = _dw_mod(c_act, dmod_rows, "dw_mod")

    out_g, out_d, out_m, out_v = {}, {}, {}, {}
    res, _ = _adamw(w_mod.reshape(-1, ncol), g_w_mod.reshape(-1, ncol), m_w_mod.reshape(-1, ncol),
                    v_w_mod.reshape(-1, ncol), "adamw_w_mod")
    out_g["w_mod"], out_d["w_mod"], out_m["w_mod"], out_v["w_mod"] = [r.reshape(w_mod.shape) for r in res]
    for n, _ in _BIG:
        shp = W[n].shape
        flat = (shp[0] * shp[1], shp[2])
        res = _adamw_partials(W[n].reshape(flat), [(recv[(l, n)], theirs[(l, n)]) for l in range(L)],
                              M1[n].reshape(flat), V1[n].reshape(flat), f"adamw_{n}")
        out_g[n], out_d[n], out_m[n], out_v[n] = [r.reshape(shp) for r in res]
    rest = ["b_mod"] + small_names

    def rows(a):
        return a.reshape(-1, a.shape[-1])
    res = _adamw_many([rows(W[n]) for n in rest], [rows(G[n]) for n in rest], [rows(M1[n]) for n in rest],
                      [rows(V1[n]) for n in rest], "adamw_small")
    for dst, group in zip((out_g, out_d, out_m, out_v), res):
        dst.update({n: r.reshape(W[n].shape) for n, r in zip(rest, group)})

    return (loss, grad_x, *[out_g[n] for n in names], *[out_d[n] for n in names],
            *[out_m[n] for n in names], *[out_v[n] for n in names])
```

```python
import math

import jax
import jax.numpy as jnp
from jax import lax
from jax.experimental import pallas as pl
from jax.experimental.pallas import tpu as pltpu

F32 = jnp.float32
BF16 = jnp.bfloat16

D_MODEL = 1024
DEPTH = 2
HEAD_DIM = 64
N_Q_HEADS = 8
ATTN_WIDTH = 512
KV_WIDTH = 128
LRU_WIDTH = 256
CONV_WIDTH = 256
IN_PROJ_WIDTH = 2048
BLOCK = 128
D_FF = 2816
EPS = 1e-6
NEG_INF = -1e30
LRU_C = 8.0
N_CHIPS = 4
N_DEV = 8

C_Q, C_KV, C_LX, C_LG, C_SB, C_SC, C_SX = 0, 512, 768, 1024, 1280, 1536, 1792

ADAM_LR = 0.001
ADAM_B1 = 0.9
ADAM_B2 = 0.999
ADAM_EPS = 1e-08
ADAM_WD = 0.01
ADAM_STEP = 10

LANES = 128
SUBLANES = 8
VMEM_LIMIT = 56 * 1024 * 1024
MIX_TILE = 256

MESH = pl.DeviceIdType.MESH


def _cp(*sem):
    return pltpu.CompilerParams(dimension_semantics=sem, vmem_limit_bytes=VMEM_LIMIT)


def _tile(n, pref):
    t = min(n, pref)
    while n % t:
        t //= 2
    return t


MXU_DIM = 256


def _resident(shape):
    return pl.BlockSpec(shape, lambda *_: (0, 0), pipeline_mode=pl.Buffered(1))


def _sigmoid(v):
    return 1.0 / (1.0 + jnp.exp(-v))


def _expm1(v):
    series = v * (1.0 + v * (0.5 + v * (1.0 / 6.0)))
    return jnp.where(v > -0.01, series, jnp.exp(v) - 1.0)


def _softplus_neg(lam):
    e = jnp.exp(-jnp.abs(lam))
    log1p = jnp.where(e < 1e-2, e * (1.0 - e * (0.5 - e * (1.0 / 3.0))), jnp.log(1.0 + e))
    return jnp.maximum(-lam, 0.0) + log1p


_GELU_K = math.sqrt(2.0 / math.pi)
_GELU_C = 0.044715


def _gelu(v):
    t = jnp.tanh(_GELU_K * (v + _GELU_C * v * v * v))
    return 0.5 * v * (1.0 + t), t


def _gelu_grad(v, t):
    return 0.5 * (1.0 + t) + 0.5 * v * (1.0 - t * t) * _GELU_K * (1.0 + 3.0 * _GELU_C * v * v)


def _dot(a, b):
    return jnp.dot(a, b, preferred_element_type=F32)


def _dot_nt(a, b):
    return lax.dot_general(a, b, (((1,), (1,)), ((), ())), preferred_element_type=F32)


def _dot_tn(a, b):
    return lax.dot_general(a, b, (((0,), (0,)), ((), ())), preferred_element_type=F32)


def _window(ref, axis, j, width):
    start = pl.multiple_of(j * width, LANES if axis == 1 else 16)
    if axis == 1:
        return ref.at[:, pl.ds(start, width)]
    return ref.at[pl.ds(start, width), :]


def _chip_peers():
    x, y, c = lax.axis_index("x"), lax.axis_index("y"), lax.axis_index("c")
    return x, y, c, [(1 - x, y), (x, 1 - y), (1 - x, 1 - y)]


class _GatherJob:
    def __init__(self, shard, axis):
        self.src, self.axis, self.width, self.half = shard, axis, shard.shape[axis], shard.shape[0] // 2
        full = tuple(d * N_CHIPS if k == axis else d for k, d in enumerate(shard.shape))
        self.out_shape = jax.ShapeDtypeStruct(full, shard.dtype)

    def _piece(self, ref, j, hf):
        if self.axis == 1:
            return ref.at[pl.ds(pl.multiple_of(hf * self.half, 16), self.half),
                          pl.ds(pl.multiple_of(j * self.width, LANES), self.width)]
        return ref.at[pl.ds(pl.multiple_of(j * self.width + hf * self.half, 16), self.half), :]

    def _copies(self, src, dst, send, recv, loc, t):
        x, y, c, chips = _chip_peers()
        j = 2 * x + y
        owners = [2 * px + py for px, py in chips]
        local = pltpu.make_async_copy(src, _window(dst, self.axis, j, self.width), loc.at[t])
        mine = src.at[pl.ds(pl.multiple_of(c * self.half, 16), self.half), :]

        def ici(k, owner):
            return pltpu.make_async_remote_copy(
                src_ref=mine, dst_ref=self._piece(dst, owner, c), send_sem=send.at[JOB_SEMS * t + k],
                recv_sem=recv.at[JOB_SEMS * t + k], device_id=(*chips[k], c), device_id_type=MESH)

        def relay(k, hf):
            piece = self._piece(dst, owners[k], hf)
            return pltpu.make_async_remote_copy(
                src_ref=piece, dst_ref=piece, send_sem=send.at[JOB_SEMS * t + 4 + k],
                recv_sem=recv.at[JOB_SEMS * t + 4 + k],
                device_id=(x, y, 1 - c), device_id_type=MESH)

        return (local, [ici(k, j) for k in range(3)], [ici(k, owners[k]) for k in range(3)],
                [relay(k, c) for k in range(3)], [relay(k, 1 - c) for k in range(3)])

    def start(self, *a):
        local, ici_out, _, _, _ = self._copies(*a)
        local.start()
        for cp in ici_out:
            cp.start()

    def relay(self, *a):
        _, _, ici_in, relay_out, _ = self._copies(*a)
        for arrived, onward in zip(ici_in, relay_out):
            arrived.wait_recv()
            onward.start()

    def finish(self, *a):
        local, ici_out, _, relay_out, relay_in = self._copies(*a)
        for cp in relay_in:
            cp.wait_recv()
        for cp in ici_out + relay_out:
            cp.wait_send()
        local.wait()


class _ScatterJob:
    def __init__(self, full, axis):
        self.src, self.axis, self.width = full, axis, full.shape[axis] // N_CHIPS
        shard = tuple(self.width if k == axis else d for k, d in enumerate(full.shape))
        self.out_shape = jax.ShapeDtypeStruct((N_CHIPS,) + shard, full.dtype)

    def _copies(self, src, dst, send, recv, loc, t):
        x, y, c, chips = _chip_peers()
        local = pltpu.make_async_copy(_window(src, self.axis, 2 * x + y, self.width), dst.at[3], loc.at[t])
        sends = [pltpu.make_async_remote_copy(
            src_ref=_window(src, self.axis, 2 * px + py, self.width), dst_ref=dst.at[k],
            send_sem=send.at[JOB_SEMS * t + k], recv_sem=recv.at[JOB_SEMS * t + k], device_id=(px, py, c),
            device_id_type=MESH) for k, (px, py) in enumerate(chips)]
        return local, sends

    def start(self, *a):
        local, sends = self._copies(*a)
        local.start()
        for cp in sends:
            cp.start()

    def relay(self, *a):
        pass

    def finish(self, *a):
        local, sends = self._copies(*a)
        for cp in sends:
            cp.wait_recv()
        for cp in sends:
            cp.wait_send()
        local.wait()


class _SiblingJob:
    def __init__(self, arr):
        self.src, self.out_shape = arr, jax.ShapeDtypeStruct(arr.shape, arr.dtype)

    def _copy(self, src, dst, send, recv, loc, t):
        x, y, c = lax.axis_index("x"), lax.axis_index("y"), lax.axis_index("c")
        return pltpu.make_async_remote_copy(
            src_ref=src, dst_ref=dst, send_sem=send.at[JOB_SEMS * t], recv_sem=recv.at[JOB_SEMS * t],
            device_id=(x, y, 1 - c), device_id_type=MESH)

    def start(self, *a):
        self._copy(*a).start()

    def relay(self, *a):
        pass

    def finish(self, *a):
        self._copy(*a).wait()


JOB_SEMS = 8


def _run_jobs(phase, jobs, srcs, dsts, sems):
    for t, job in enumerate(jobs):
        getattr(job, phase)(srcs[t], dsts[t], *sems, t)


def _job_scratch(n):
    return [pltpu.SemaphoreType.DMA((JOB_SEMS * n,)), pltpu.SemaphoreType.DMA((JOB_SEMS * n,)),
            pltpu.SemaphoreType.DMA((n,))]


def _pcall(body, *, name, grid, in_specs, out_specs, out_shape, sem, args, scratch_shapes=(), jobs=()):
    in_specs, out_specs, out_shape = list(in_specs), list(out_specs), list(out_shape)
    scratch_shapes = list(scratch_shapes)
    if not jobs:
        res = pl.pallas_call(body, name=name, grid=grid, in_specs=in_specs, out_specs=out_specs, out_shape=out_shape,
                             scratch_shapes=scratch_shapes, compiler_params=_cp(*sem))(*args)
        return list(res), []
    n_in, n_out, n_scr, nj = len(args), len(out_shape), len(scratch_shapes), len(jobs)
    n_steps = math.prod(grid)
    relay_step = (3 * n_steps) // 4
    relay_early = 0 < relay_step < n_steps - 1

    def wrapped(*refs):
        ins, refs = refs[:n_in], refs[n_in:]
        jin, refs = refs[:nj], refs[nj:]
        outs, refs = refs[:n_out], refs[n_out:]
        jout, refs = refs[:nj], refs[nj:]
        scr, sems = refs[:n_scr], refs[n_scr:]
        step = pl.program_id(0)
        for d in range(1, len(grid)):
            step = step * grid[d] + pl.program_id(d)

        @pl.when(step == 0)
        def _():
            _run_jobs("start", jobs, jin, jout, sems)

        if relay_early:
            @pl.when(step == relay_step)
            def _():
                _run_jobs("relay", jobs, jin, jout, sems)
        body(*ins, *outs, *scr)

        @pl.when(step == n_steps - 1)
        def _():
            if not relay_early:
                _run_jobs("relay", jobs, jin, jout, sems)
            _run_jobs("finish", jobs, jin, jout, sems)

    hbm = pl.BlockSpec(memory_space=pltpu.HBM)
    res = pl.pallas_call(
        wrapped, name=name, grid=grid, in_specs=in_specs + [hbm] * nj, out_specs=out_specs + [hbm] * nj,
        out_shape=out_shape + [job.out_shape for job in jobs], scratch_shapes=scratch_shapes + _job_scratch(nj),
        compiler_params=_cp(*sem))(*args, *[job.src for job in jobs])
    return list(res[:n_out]), list(res[n_out:])


def _hidden_chunks(k):
    return [(c0, min(6 * MXU_DIM, k - c0)) for c0 in range(0, k, 6 * MXU_DIM)]


def _loss_head(xv, gain, tgt, st_ref):
    dm = xv.shape[-1]
    rstd = lax.rsqrt(jnp.mean(xv * xv, axis=-1, keepdims=True) + EPS)
    xn = xv * rstd
    err = xn * gain - tgt
    st_ref[1:2, :] += jnp.full((1, dm), 0.5 / dm, F32) * jnp.sum(err * err)
    dy = err * (1.0 / dm)
    st_ref[0:1, :] += jnp.sum(dy * xn, axis=0, keepdims=True)
    dxn = dy * gain
    return rstd * (dxn - xn * jnp.mean(dxn * xn, axis=-1, keepdims=True))


def _ffn_fwd(x, nrm, w_gu, w_down, name, jobs=(), head=None):
    S, Dm = x.shape
    K = w_down.shape[0]
    tm = _tile(S, 256)

    def body(x_ref, nrm_ref, wgu_ref, wdn_ref, *rest):
        if head is None:
            o_ref, h_ref, gu_ref, y_ref = rest
        else:
            gf_ref, t_ref, o_ref, h_ref, gu_ref, y_ref, st_ref = rest

            @pl.when(pl.program_id(0) == 0)
            def _():
                st_ref[...] = jnp.zeros_like(st_ref)
        xv = x_ref[...]
        rstd = lax.rsqrt(jnp.mean(xv * xv, axis=-1, keepdims=True) + EPS)
        hn = (xv * rstd) * nrm_ref[0:1, :]
        hb = (hn * (1.0 + nrm_ref[2:3, :]) + nrm_ref[1:2, :]).astype(BF16)
        h_ref[...] = hb
        y = jnp.zeros((tm, Dm), F32)
        for c0, cs in _hidden_chunks(K):
            g = _dot(hb, wgu_ref[:, c0:c0 + cs])
            u = _dot(hb, wgu_ref[:, K + c0:K + c0 + cs])
            gu_ref[:, c0:c0 + cs] = g.astype(BF16)
            gu_ref[:, K + c0:K + c0 + cs] = u.astype(BF16)
            y = y + _dot((g * _sigmoid(g) * u).astype(BF16), wdn_ref[c0:c0 + cs, :])
        xo = xv + (0.5 * nrm_ref[3:4, :]) * y
        o_ref[...] = xo if head is None else _loss_head(xo, gf_ref[0:1, :], t_ref[...], st_ref)
        y_ref[...] = y.astype(BF16)

    row = lambda i: (i, 0)
    fix = lambda i: (0, 0)
    in_specs = [pl.BlockSpec((tm, Dm), row), pl.BlockSpec((8, Dm), fix), _resident((Dm, 2 * K)), _resident((K, Dm))]
    out_specs = [pl.BlockSpec((tm, Dm), row), pl.BlockSpec((tm, Dm), row), pl.BlockSpec((tm, 2 * K), row),
                 pl.BlockSpec((tm, Dm), row)]
    out_shape = [jax.ShapeDtypeStruct((S, Dm), F32), jax.ShapeDtypeStruct((S, Dm), BF16),
                 jax.ShapeDtypeStruct((S, 2 * K), BF16), jax.ShapeDtypeStruct((S, Dm), BF16)]
    args = (x, nrm, w_gu, w_down)
    if head is not None:
        in_specs += [pl.BlockSpec((8, Dm), fix), pl.BlockSpec((tm, Dm), row)]
        out_specs.append(pl.BlockSpec((8, Dm), fix))
        out_shape.append(jax.ShapeDtypeStruct((8, Dm), F32))
        args += tuple(head)
    return _pcall(body, name=name, grid=(S // tm,), in_specs=in_specs, out_specs=out_specs, out_shape=out_shape,
                  sem=("arbitrary",), args=args, jobs=jobs)


def _ffn_down_bwd(dxo, gu, y, w, nrm, name, jobs=()):
    S, Dm = dxo.shape
    K = w.shape[0]
    Ka = gu.shape[1]
    coef = 0.5
    tm = _tile(S, 256)
    n_steps = S // tm
    chunks = _hidden_chunks(K)

    def body(dxo_ref, y_ref, w_ref, nrm_ref, a_ref, da_ref, dgate_ref, dw_ref, acc):
        @pl.when(pl.program_id(0) == 0)
        def _():
            dgate_ref[...] = jnp.zeros_like(dgate_ref)
            acc[...] = jnp.zeros_like(acc)

        dxo_v = dxo_ref[...]
        dyb = ((coef * nrm_ref[3:4, :]) * dxo_v).astype(BF16)
        dgate_ref[0:1, :] += jnp.sum(coef * y_ref[...].astype(F32) * dxo_v, axis=0, keepdims=True)
        for c0, cs in chunks:
            dact = _dot_nt(dyb, w_ref[c0:c0 + cs, :])
            g = a_ref[:, c0:c0 + cs].astype(F32)
            u = a_ref[:, K + c0:K + c0 + cs].astype(F32)
            s = _sigmoid(g)
            si = g * s
            da_ref[:, c0:c0 + cs] = (dact * u * (s * (1.0 + g * (1.0 - s)))).astype(BF16)
            da_ref[:, K + c0:K + c0 + cs] = (dact * si).astype(BF16)
            acc[c0:c0 + cs, :] += _dot_tn((si * u).astype(BF16), dyb)

        @pl.when(pl.program_id(0) == n_steps - 1)
        def _():
            dw_ref[...] = acc[...].astype(BF16)

    row = lambda i: (i, 0)
    fix = lambda i: (0, 0)
    return _pcall(
        body, name=name, grid=(n_steps,),
        in_specs=[pl.BlockSpec((tm, Dm), row), pl.BlockSpec((tm, Dm), row), _resident((K, Dm)),
                  pl.BlockSpec((8, Dm), fix), pl.BlockSpec((tm, Ka), row)],
        out_specs=[pl.BlockSpec((tm, Ka), row), pl.BlockSpec((8, Dm), fix), _resident((K, Dm))],
        out_shape=[jax.ShapeDtypeStruct((S, Ka), BF16), jax.ShapeDtypeStruct((8, Dm), F32),
                   jax.ShapeDtypeStruct((K, Dm), BF16)],
        scratch_shapes=[pltpu.VMEM((K, Dm), F32)],
        sem=("arbitrary",), args=(dxo, y, w, nrm, gu), jobs=jobs)


def _atb(a, b, out_dtype, bm, bn, name, jobs=()):
    S, M = a.shape
    N = b.shape[1]
    bk = _tile(S, 1024)
    nk = S // bk

    def body(a_ref, b_ref, o_ref, acc):
        k = pl.program_id(2)

        @pl.when(k == 0)
        def _():
            acc[...] = jnp.zeros_like(acc)
        acc[...] += _dot_tn(a_ref[...], b_ref[...])

        @pl.when(k == nk - 1)
        def _():
            o_ref[...] = acc[...].astype(o_ref.dtype)

    (out,), extra = _pcall(
        body, name=name, grid=(M // bm, N // bn, nk),
        in_specs=[pl.BlockSpec((bk, bm), lambda m, n, k: (k, m)),
                  pl.BlockSpec((bk, bn), lambda m, n, k: (k, n))],
        out_specs=[pl.BlockSpec((bm, bn), lambda m, n, k: (m, n))],
        out_shape=[jax.ShapeDtypeStruct((M, N), out_dtype)],
        scratch_shapes=[pltpu.VMEM((bm, bn), F32)],
        sem=("arbitrary", "arbitrary", "arbitrary"), args=(a, b), jobs=jobs)
    return out, extra


def _norm_bwd(dh, xv, nrm_ref, red_ref):
    rstd = lax.rsqrt(jnp.mean(xv * xv, axis=-1, keepdims=True) + EPS)
    xn = xv * rstd
    gain = nrm_ref[0:1, :]
    hn = xn * gain
    dhn = dh * (1.0 + nrm_ref[2:3, :])
    red_ref[0:1, :] += jnp.sum(dh, axis=0, keepdims=True)
    red_ref[1:2, :] += jnp.sum(dh * hn, axis=0, keepdims=True)
    red_ref[2:3, :] += jnp.sum(dhn * xn, axis=0, keepdims=True)
    dxn = dhn * gain
    return rstd * (dxn - xn * jnp.mean(dxn * xn, axis=-1, keepdims=True))


def _nt_norm_bwd(dout, w, x, nrm, dxo, name, jobs=()):
    S, N = dout.shape
    Dm = w.shape[0]
    tm = _tile(S, 512)

    def body(do_ref, w_ref, x_ref, nrm_ref, dxo_ref, dx_ref, red_ref):
        @pl.when(pl.program_id(0) == 0)
        def _():
            red_ref[...] = jnp.zeros_like(red_ref)
        dh = _dot_nt(do_ref[...], w_ref[...])
        dx_ref[...] = dxo_ref[...] + _norm_bwd(dh, x_ref[...], nrm_ref, red_ref)

    return _pcall(
        body, name=name, grid=(S // tm,),
        in_specs=[pl.BlockSpec((tm, N), lambda i: (i, 0)),
                  _resident((Dm, N)),
                  pl.BlockSpec((tm, Dm), lambda i: (i, 0)),
                  pl.BlockSpec((8, Dm), lambda i: (0, 0)),
                  pl.BlockSpec((tm, Dm), lambda i: (i, 0))],
        out_specs=[pl.BlockSpec((tm, Dm), lambda i: (i, 0)),
                   pl.BlockSpec((8, Dm), lambda i: (0, 0))],
        out_shape=[jax.ShapeDtypeStruct((S, Dm), F32), jax.ShapeDtypeStruct((8, Dm), F32)],
        sem=("arbitrary",), args=(dout, w, x, nrm, dxo), jobs=jobs)


def _alibi_slope(h):
    return float(2.0 ** (-8.0 * (h + 1) / N_Q_HEADS))


def _head_planes(pair_cols):
    lane = lax.broadcasted_iota(jnp.int32, pair_cols.shape, 1)
    low = lane < HEAD_DIM
    h0_lo = jnp.where(low, pair_cols, 0.0)
    h1_hi = jnp.where(low, 0.0, pair_cols)
    h0_hi = pltpu.roll(h0_lo, HEAD_DIM, 1)
    h1_lo = pltpu.roll(h1_hi, HEAD_DIM, 1)
    return ((h0_lo.astype(BF16), h0_hi.astype(BF16)), (h1_lo.astype(BF16), h1_hi.astype(BF16)))


def _band_geometry(first_block):
    qi = lax.broadcasted_iota(jnp.int32, (BLOCK, BLOCK), 0)
    kj = lax.broadcasted_iota(jnp.int32, (BLOCK, BLOCK), 1)
    own = kj <= qi
    dist = jnp.where(own, qi - kj, qi + BLOCK - kj).astype(F32)
    valid = kj <= qi + BLOCK * (1 - first_block)
    return own, dist, valid


def _fold(band, own):
    return jnp.where(own, band[:, BLOCK:], band[:, :BLOCK])


def _unfold(v, own):
    return jnp.concatenate([jnp.where(own, 0.0, v), jnp.where(own, v, 0.0)], axis=1)


def _softmax_band(s, h, geometry, sink):
    own, dist, valid = geometry
    s = jnp.where(valid, s - _alibi_slope(h) * dist, NEG_INF)
    m = jnp.maximum(jnp.max(s, axis=-1, keepdims=True), sink)
    p = jnp.exp(s - m)
    e_sink = jnp.exp(sink - m)
    inv = 1.0 / (jnp.sum(p, axis=-1, keepdims=True) + e_sink)
    return p * inv, e_sink * inv


def _past(cur, prev, s, row):
    return jnp.where(row < s, pltpu.roll(prev, s, 0), pltpu.roll(cur, s, 0))


def _future(cur, nxt, s, row):
    T = cur.shape[0]
    return jnp.where(row >= T - s, pltpu.roll(nxt, T - s, 0), pltpu.roll(cur, T - s, 0))


def _edge_row(v, last):
    T = v.shape[0]
    r8 = lax.broadcasted_iota(jnp.int32, (SUBLANES, v.shape[1]), 0)
    blk = v[T - SUBLANES:, :] if last else v[:SUBLANES, :]
    return jnp.sum(jnp.where(r8 == (SUBLANES - 1 if last else 0), blk, 0.0), axis=0, keepdims=True)


def _lru_gates(lx, lx_prev, small_ref, wa_ref, wx_ref, row, t0):
    xc = (small_ref[4:5, :] + small_ref[3:4, :] * lx + small_ref[2:3, :] * _past(lx, lx_prev, 1, row)
          + small_ref[1:2, :] * _past(lx, lx_prev, 2, row) + small_ref[0:1, :] * _past(lx, lx_prev, 3, row))
    xcb = xc.astype(BF16)
    r = _sigmoid(_dot(xcb, wa_ref[...]) + small_ref[5:6, :])
    ig = _sigmoid(_dot(xcb, wx_ref[...]) + small_ref[6:7, :])
    sp = _softplus_neg(small_ref[7:8, :])
    la = (-LRU_C) * r * sp
    a = jnp.exp(la)
    first = (row + t0) == 0
    mult = jnp.where(first, 1.0, jnp.sqrt(-_expm1(2.0 * la)))
    return xc, xcb, r, ig, sp, a, mult, first


def _mixer_fwd(x, nrm, w_in, w_out, sinks, small, wa, wx, name, jobs=()):
    S, Dm = x.shape
    T = MIX_TILE
    nT = S // T
    nb = T // BLOCK

    def body(x_ref, nrm_ref, w_in_ref, w_out_ref, sink_ref, small_ref, wa_ref, wx_ref,
             xo_ref, h_ref, proj_ref, y_ref, ymo_ref, hp_ref, *carried_state):
        xv = x_ref[...]
        rstd = lax.rsqrt(jnp.mean(xv * xv, axis=-1, keepdims=True) + EPS)
        hn = (xv * rstd) * nrm_ref[0:1, :]
        hb = (hn * (1.0 + nrm_ref[2:3, :]) + nrm_ref[1:2, :]).astype(BF16)
        h_ref[...] = hb
        proj_ref[...] = _dot(hb, w_in_ref[...])
        core(proj_ref, sink_ref, small_ref, wa_ref, wx_ref, y_ref, hp_ref, *carried_state)
        yo = _dot(y_ref[...], w_out_ref[...])
        xo_ref[...] = xv + nrm_ref[3:4, :] * yo
        ymo_ref[...] = yo.astype(BF16)

    def core(proj_ref, sink_ref, small_ref, wa_ref, wx_ref, y_ref, hp_ref, kvp, lxp, zp, hcar):
        i = pl.program_id(0)

        @pl.when(i == 0)
        def _():
            kvp[...] = jnp.zeros_like(kvp)
            lxp[...] = jnp.zeros_like(lxp)
            zp[...] = jnp.zeros_like(zp)
            hcar[...] = jnp.zeros_like(hcar)

        row = lax.broadcasted_iota(jnp.int32, (T, LRU_WIDTH), 0)

        kv = proj_ref[:, C_KV:C_KV + 2 * KV_WIDTH]
        ext = jnp.concatenate([kvp[...], kv], axis=0)
        kx = _head_planes(ext[:, :KV_WIDTH])
        vx = _head_planes(ext[:, KV_WIDTH:])
        first_tile = jnp.where(i == 0, 1, 0)
        units = [(b, pair, e) for b in range(nb) for pair in range(N_Q_HEADS // 2) for e in range(2)]
        geometry = [_band_geometry(first_tile if b == 0 else 0) for b in range(nb)]
        keys = [slice(b * BLOCK, (b + 2) * BLOCK) for b in range(nb)]
        qp = {(b, pair): (proj_ref[b * BLOCK:(b + 1) * BLOCK, pair * LANES:(pair + 1) * LANES] * 0.125).astype(BF16)
              for b in range(nb) for pair in range(N_Q_HEADS // 2)}
        scores = [_fold(_dot_nt(qp[(b, pair)], kx[pair // 2][e][keys[b]]), geometry[b][0]) for b, pair, e in units]
        probs = [_unfold(_softmax_band(s, 2 * pair + e, geometry[b], sink_ref[2 * pair + e])[0],
                         geometry[b][0]).astype(BF16) for s, (b, pair, e) in zip(scores, units)]
        outs = [_dot(p, vx[pair // 2][e][keys[b]]) for p, (b, pair, e) in zip(probs, units)]
        for u in range(0, len(units), 2):
            b, pair, _ = units[u]
            y_ref[b * BLOCK:(b + 1) * BLOCK, pair * LANES:(pair + 1) * LANES] = (outs[u] + outs[u + 1]).astype(BF16)
        kvp[...] = kv[T - BLOCK:, :]

        lx = proj_ref[:, C_LX:C_LX + LRU_WIDTH]
        xc, _, _, ig, _, a, mult, _ = _lru_gates(lx, lxp[...], small_ref, wa_ref, wx_ref, row, i * T)
        lxp[...] = lx
        aa = a
        bb = mult * (ig * xc)
        s = 1
        while s < T:
            a_sh = jnp.where(row >= s, pltpu.roll(aa, s, 0), 1.0)
            b_sh = jnp.where(row >= s, pltpu.roll(bb, s, 0), 0.0)
            bb = aa * b_sh + bb
            aa = aa * a_sh
            s *= 2
        hc = hcar[0:1, :]
        hh = bb + aa * hc
        hp_ref[...] = jnp.where(row < 1, hc, pltpu.roll(hh, 1, 0))
        hcar[...] = jnp.broadcast_to(_edge_row(hh, True), hcar.shape)
        gl, _ = _gelu(proj_ref[:, C_LG:C_LG + LRU_WIDTH])
        y_ref[:, ATTN_WIDTH:ATTN_WIDTH + LRU_WIDTH] = (gl * hh).astype(BF16)

        z = proj_ref[:, C_SC:C_SC + CONV_WIDTH] * proj_ref[:, C_SX:C_SX + CONV_WIDTH]
        c3 = (small_ref[10:11, :] * z + small_ref[9:10, :] * _past(z, zp[...], 1, row)
              + small_ref[8:9, :] * _past(z, zp[...], 2, row))
        zp[...] = z
        y_ref[:, ATTN_WIDTH + LRU_WIDTH:] = (proj_ref[:, C_SB:C_SB + CONV_WIDTH] * c3).astype(BF16)

    fix = lambda i: (0, 0)
    row = lambda i: (i, 0)
    return _pcall(
        body, name=name, grid=(nT,),
        in_specs=[pl.BlockSpec((T, Dm), row), pl.BlockSpec((8, Dm), fix),
                  _resident((Dm, IN_PROJ_WIDTH)), _resident((D_MODEL, Dm)),
                  pl.BlockSpec(memory_space=pltpu.SMEM),
                  pl.BlockSpec((16, LRU_WIDTH), fix),
                  pl.BlockSpec((LRU_WIDTH, LRU_WIDTH), fix),
                  pl.BlockSpec((LRU_WIDTH, LRU_WIDTH), fix)],
        out_specs=[pl.BlockSpec((T, Dm), row), pl.BlockSpec((T, Dm), row), pl.BlockSpec((T, IN_PROJ_WIDTH), row),
                   pl.BlockSpec((T, D_MODEL), row), pl.BlockSpec((T, Dm), row), pl.BlockSpec((T, LRU_WIDTH), row)],
        out_shape=[jax.ShapeDtypeStruct((S, Dm), F32), jax.ShapeDtypeStruct((S, Dm), BF16),
                   jax.ShapeDtypeStruct((S, IN_PROJ_WIDTH), F32), jax.ShapeDtypeStruct((S, D_MODEL), BF16),
                   jax.ShapeDtypeStruct((S, Dm), BF16), jax.ShapeDtypeStruct((S, LRU_WIDTH), F32)],
        scratch_shapes=[pltpu.VMEM((BLOCK, 2 * KV_WIDTH), F32), pltpu.VMEM((T, LRU_WIDTH), F32),
                        pltpu.VMEM((T, CONV_WIDTH), F32), pltpu.VMEM((SUBLANES, LRU_WIDTH), F32)],
        sem=("arbitrary",), args=(x, nrm, w_in, w_out, sinks, small, wa, wx), jobs=jobs)


def _mixer_bwd(x, nrm, dxo, h, proj, ymix, ymo, hprev, w_in, w_out, sinks, small, wa, wx, name, jobs=()):
    S, Dm = x.shape
    T = MIX_TILE
    nT = S // T
    nb = T // BLOCK
    bpt = T // BLOCK

    def body(x_ref, nrm_ref, dxo_ref, h_ref, proj_ref, kvprev_ref, lxprev_ref, scprev_ref, sxprev_ref, ymix_ref,
             ymo_ref, hp_ref, w_in_ref, w_out_ref, sink_ref, small_ref, wa_ref, wx_ref,
             dx_ref, red_ref, dgate_ref, dwo_ref, dwi_ref, dsm_ref, dsink_ref, dwa_ref, dwx_ref,
             dy_s, dp_s, acc_o, acc_i, *carried_state):
        @pl.when(pl.program_id(0) == 0)
        def _():
            for r in (red_ref, dgate_ref, acc_o, acc_i):
                r[...] = jnp.zeros_like(r)

        dxo_v = dxo_ref[...]
        dyb = (nrm_ref[3:4, :] * dxo_v).astype(BF16)
        dgate_ref[0:1, :] += jnp.sum(ymo_ref[...].astype(F32) * dxo_v, axis=0, keepdims=True)
        dy_s[...] = _dot_nt(dyb, w_out_ref[...])
        acc_o[...] += _dot_tn(ymix_ref[...], dyb)
        core(proj_ref, kvprev_ref, lxprev_ref, scprev_ref, sxprev_ref, dy_s, hp_ref, sink_ref, small_ref,
             wa_ref, wx_ref, dp_s, dsm_ref, dsink_ref, dwa_ref, dwx_ref, *carried_state)
        dpb = dp_s[...]
        acc_i[...] += _dot_tn(h_ref[...], dpb)
        dx_ref[...] = dxo_v + _norm_bwd(_dot_nt(dpb, w_in_ref[...]), x_ref[...], nrm_ref, red_ref)

        @pl.when(pl.program_id(0) == nT - 1)
        def _():
            dwo_ref[...] = acc_o[...].astype(BF16)
            dwi_ref[...] = acc_i[...].astype(BF16)

    def core(proj_ref, kvprev_ref, lxprev_ref, scprev_ref, sxprev_ref, dy_ref, hp_ref, sink_ref, small_ref,
             wa_ref, wx_ref, dp_ref, dsm_ref, dsink_ref, dwa_ref, dwx_ref,
             dk_s, dv_s, dkv_c, dxc_n, dc3_n, p_c):
        i = pl.program_id(0)
        ti = nT - 1 - i
        has_prev = jnp.where(ti == 0, 0.0, 1.0)

        @pl.when(i == 0)
        def _():
            for r in (dkv_c, dxc_n, dc3_n, p_c, dsm_ref, dsink_ref, dwa_ref, dwx_ref):
                r[...] = jnp.zeros_like(r)

        row = lax.broadcasted_iota(jnp.int32, (T, LRU_WIDTH), 0)

        kv = proj_ref[:, C_KV:C_KV + 2 * KV_WIDTH]
        ext = jnp.concatenate([kvprev_ref[...] * has_prev, kv], axis=0)
        kx = _head_planes(ext[:, :KV_WIDTH])
        vx = _head_planes(ext[:, KV_WIDTH:])
        dk_s[...] = jnp.zeros_like(dk_s)
        dv_s[...] = jnp.zeros_like(dv_s)
        dk_s[:, T:] = dkv_c[:, :BLOCK]
        dv_s[:, T:] = dkv_c[:, BLOCK:]
        first_tile = jnp.where(ti == 0, 1, 0)
        units = [(b, pair, e) for b in range(nb) for pair in range(N_Q_HEADS // 2) for e in range(2)]
        geometry = [_band_geometry(first_tile if b == 0 else 0) for b in range(nb)]
        keys = [slice(b * BLOCK, (b + 2) * BLOCK) for b in range(nb)]
        tile = {(b, pair): (slice(b * BLOCK, (b + 1) * BLOCK), slice(pair * LANES, (pair + 1) * LANES))
                for b in range(nb) for pair in range(N_Q_HEADS // 2)}
        qp = {k: (proj_ref[rc] * 0.125).astype(BF16) for k, rc in tile.items()}
        dob = {k: dy_ref[rc].astype(BF16) for k, rc in tile.items()}
        qp_t = {k: jnp.transpose(proj_ref[rc] * 0.125).astype(BF16) for k, rc in tile.items()}
        dob_t = {k: jnp.transpose(dy_ref[rc]).astype(BF16) for k, rc in tile.items()}
        scores = [_fold(_dot_nt(qp[(b, pair)], kx[pair // 2][e][keys[b]]), geometry[b][0]) for b, pair, e in units]
        dprob = [_fold(_dot_nt(dob[(b, pair)], vx[pair // 2][e][keys[b]]), geometry[b][0]) for b, pair, e in units]
        pn_wide, ds_wide = [], []
        for s, dpm, (b, pair, e) in zip(scores, dprob, units):
            h = 2 * pair + e
            own = geometry[b][0]
            pn, psink = _softmax_band(s, h, geometry[b], sink_ref[h])
            dsum = jnp.sum(pn * dpm, axis=-1, keepdims=True)
            dsink_ref[h:h + 1, :] += jnp.full((1, LANES), -1.0, F32) * jnp.sum(psink * dsum)
            pn_wide.append(_unfold(pn, own).astype(BF16))
            ds_wide.append(_unfold(pn * (dpm - dsum), own).astype(BF16))
        dq = {}
        for pw, ds, (b, pair, e) in zip(pn_wide, ds_wide, units):
            g = pair // 2
            head_e = slice(e * HEAD_DIM, (e + 1) * HEAD_DIM)
            head_g = slice(g * HEAD_DIM, (g + 1) * HEAD_DIM)
            dv_s[head_g, keys[b]] += _dot(dob_t[(b, pair)], pw)[head_e, :]
            dk_s[head_g, keys[b]] += _dot(qp_t[(b, pair)], ds)[head_e, :]
            part = _dot(ds, kx[g][e][keys[b]])
            dq[(b, pair)] = part if e == 0 else dq[(b, pair)] + part
        for k, rc in tile.items():
            dp_ref[rc] = (0.125 * dq[k]).astype(BF16)
        dp_ref[:, C_KV:C_KV + KV_WIDTH] = jnp.transpose(dk_s[:, BLOCK:]).astype(BF16)
        dp_ref[:, C_KV + KV_WIDTH:C_KV + 2 * KV_WIDTH] = jnp.transpose(dv_s[:, BLOCK:]).astype(BF16)
        dkv_c[:, :BLOCK] = dk_s[:, :BLOCK]
        dkv_c[:, BLOCK:] = dv_s[:, :BLOCK]

        lx = proj_ref[:, C_LX:C_LX + LRU_WIDTH]
        lxprev = lxprev_ref[...] * has_prev
        xc, xcb, r, ig, sp, a, mult, first = _lru_gates(lx, lxprev, small_ref, wa_ref, wx_ref, row, ti * T)
        hp = hp_ref[...]
        hh = a * hp + mult * (ig * xc)
        lg = proj_ref[:, C_LG:C_LG + LRU_WIDTH]
        gl, th = _gelu(lg)
        dyl = dy_ref[:, ATTN_WIDTH:ATTN_WIDTH + LRU_WIDTH]
        dp_ref[:, C_LG:C_LG + LRU_WIDTH] = (dyl * hh * _gelu_grad(lg, th)).astype(BF16)
        aa = jnp.where(row < T - 1, pltpu.roll(a, T - 1, 0), 1.0)
        bb = dyl * gl
        s = 1
        while s < T:
            a_sh = jnp.where(row < T - s, pltpu.roll(aa, T - s, 0), 1.0)
            b_sh = jnp.where(row < T - s, pltpu.roll(bb, T - s, 0), 0.0)
            bb = bb + aa * b_sh
            aa = aa * a_sh
            s *= 2
        G = bb + aa * p_c[0:1, :]
        p_c[...] = jnp.broadcast_to(_edge_row(a * G, False), p_c.shape)
        da = G * hp
        dmult = G * (ig * xc)
        dig = G * mult * xc
        dxc = G * mult * ig
        dla = da * a + dmult * jnp.where(first, 0.0, -(a * a) / mult)
        dr = dla * ((-LRU_C) * sp)
        lam = small_ref[7:8, :]
        dsm_ref[7:8, :] += jnp.sum(dla * ((-LRU_C) * r), axis=0, keepdims=True) * (-_sigmoid(-lam))
        dpa = dr * r * (1.0 - r)
        dpx = dig * ig * (1.0 - ig)
        dsm_ref[5:6, :] += jnp.sum(dpa, axis=0, keepdims=True)
        dsm_ref[6:7, :] += jnp.sum(dpx, axis=0, keepdims=True)
        dpab = dpa.astype(BF16)
        dpxb = dpx.astype(BF16)
        dwa_ref[...] += _dot_tn(xcb, dpab)
        dwx_ref[...] += _dot_tn(xcb, dpxb)
        dxc = dxc + _dot_nt(dpab, wa_ref[...]) + _dot_nt(dpxb, wx_ref[...])
        dsm_ref[4:5, :] += jnp.sum(dxc, axis=0, keepdims=True)
        dsm_ref[3:4, :] += jnp.sum(dxc * lx, axis=0, keepdims=True)
        for k in range(3):
            dsm_ref[k:k + 1, :] += jnp.sum(dxc * _past(lx, lxprev, 3 - k, row), axis=0, keepdims=True)
        nxt = dxc_n[...]
        dlx = (small_ref[3:4, :] * dxc + small_ref[2:3, :] * _future(dxc, nxt, 1, row)
               + small_ref[1:2, :] * _future(dxc, nxt, 2, row) + small_ref[0:1, :] * _future(dxc, nxt, 3, row))
        dxc_n[...] = dxc
        dp_ref[:, C_LX:C_LX + LRU_WIDTH] = dlx.astype(BF16)

        sc = proj_ref[:, C_SC:C_SC + CONV_WIDTH]
        sx = proj_ref[:, C_SX:C_SX + CONV_WIDTH]
        sb = proj_ref[:, C_SB:C_SB + CONV_WIDTH]
        z = sc * sx
        zprev = (scprev_ref[...] * sxprev_ref[...]) * has_prev
        z1 = _past(z, zprev, 1, row)
        z2 = _past(z, zprev, 2, row)
        c3 = small_ref[10:11, :] * z + small_ref[9:10, :] * z1 + small_ref[8:9, :] * z2
        dys = dy_ref[:, ATTN_WIDTH + LRU_WIDTH:]
        dp_ref[:, C_SB:C_SB + CONV_WIDTH] = (dys * c3).astype(BF16)
        dc3 = dys * sb
        dsm_ref[10:11, :] += jnp.sum(dc3 * z, axis=0, keepdims=True)
        dsm_ref[9:10, :] += jnp.sum(dc3 * z1, axis=0, keepdims=True)
        dsm_ref[8:9, :] += jnp.sum(dc3 * z2, axis=0, keepdims=True)
        nxt3 = dc3_n[...]
        dz = (small_ref[10:11, :] * dc3 + small_ref[9:10, :] * _future(dc3, nxt3, 1, row)
              + small_ref[8:9, :] * _future(dc3, nxt3, 2, row))
        dc3_n[...] = dc3
        dp_ref[:, C_SC:C_SC + CONV_WIDTH] = (dz * sx).astype(BF16)
        dp_ref[:, C_SX:C_SX + CONV_WIDTH] = (dz * sc).astype(BF16)

    fix = lambda i: (0, 0)
    cur = lambda i: (nT - 1 - i, 0)
    prev_cols = lambda cb: (lambda i: (jnp.maximum(nT - 2 - i, 0), cb))
    return _pcall(
        body, name=name, grid=(nT,),
        in_specs=[pl.BlockSpec((T, Dm), cur), pl.BlockSpec((8, Dm), fix), pl.BlockSpec((T, Dm), cur),
                  pl.BlockSpec((T, Dm), cur),
                  pl.BlockSpec((T, IN_PROJ_WIDTH), cur),
                  pl.BlockSpec((BLOCK, 2 * KV_WIDTH),
                               lambda i: (jnp.maximum((nT - 1 - i) * bpt - 1, 0), C_KV // (2 * KV_WIDTH))),
                  pl.BlockSpec((T, LRU_WIDTH), prev_cols(C_LX // LRU_WIDTH)),
                  pl.BlockSpec((T, CONV_WIDTH), prev_cols(C_SC // CONV_WIDTH)),
                  pl.BlockSpec((T, CONV_WIDTH), prev_cols(C_SX // CONV_WIDTH)),
                  pl.BlockSpec((T, D_MODEL), cur), pl.BlockSpec((T, Dm), cur),
                  pl.BlockSpec((T, LRU_WIDTH), cur),
                  _resident((Dm, IN_PROJ_WIDTH)), _resident((D_MODEL, Dm)),
                  pl.BlockSpec(memory_space=pltpu.SMEM),
                  pl.BlockSpec((16, LRU_WIDTH), fix),
                  pl.BlockSpec((LRU_WIDTH, LRU_WIDTH), fix),
                  pl.BlockSpec((LRU_WIDTH, LRU_WIDTH), fix)],
        out_specs=[pl.BlockSpec((T, Dm), cur), pl.BlockSpec((8, Dm), fix), pl.BlockSpec((8, Dm), fix),
                   _resident((D_MODEL, Dm)), _resident((Dm, IN_PROJ_WIDTH)),
                   pl.BlockSpec((16, LRU_WIDTH), fix),
                   pl.BlockSpec((SUBLANES, LANES), fix),
                   pl.BlockSpec((LRU_WIDTH, LRU_WIDTH), fix),
                   pl.BlockSpec((LRU_WIDTH, LRU_WIDTH), fix)],
        out_shape=[jax.ShapeDtypeStruct((S, Dm), F32), jax.ShapeDtypeStruct((8, Dm), F32),
                   jax.ShapeDtypeStruct((8, Dm), F32),
                   jax.ShapeDtypeStruct((D_MODEL, Dm), BF16), jax.ShapeDtypeStruct((Dm, IN_PROJ_WIDTH), BF16),
                   jax.ShapeDtypeStruct((16, LRU_WIDTH), F32),
                   jax.ShapeDtypeStruct((SUBLANES, LANES), F32),
                   jax.ShapeDtypeStruct((LRU_WIDTH, LRU_WIDTH), F32),
                   jax.ShapeDtypeStruct((LRU_WIDTH, LRU_WIDTH), F32)],
        scratch_shapes=[pltpu.VMEM((T, D_MODEL), F32), pltpu.VMEM((T, IN_PROJ_WIDTH), BF16),
                        pltpu.VMEM((D_MODEL, Dm), F32), pltpu.VMEM((Dm, IN_PROJ_WIDTH), F32),
                        pltpu.VMEM((KV_WIDTH, T + BLOCK), F32), pltpu.VMEM((KV_WIDTH, T + BLOCK), F32),
                        pltpu.VMEM((BLOCK, 2 * KV_WIDTH), F32), pltpu.VMEM((T, LRU_WIDTH), F32),
                        pltpu.VMEM((T, CONV_WIDTH), F32), pltpu.VMEM((SUBLANES, LRU_WIDTH), F32)],
        sem=("arbitrary",),
        args=(x, nrm, dxo, h, proj, proj, proj, proj, proj, ymix, ymo, hprev, w_in, w_out, sinks, small, wa, wx),
        jobs=jobs)


def _adamw_update(g, w_ref, m_ref, v_ref, go_ref, d_ref, mo_ref, vo_ref):
    mn = ADAM_B1 * m_ref[...] + (1.0 - ADAM_B1) * g
    vn = ADAM_B2 * v_ref[...] + (1.0 - ADAM_B2) * (g * g)
    go_ref[...] = g
    mo_ref[...] = mn
    vo_ref[...] = vn
    m_hat = mn / (1.0 - ADAM_B1 ** ADAM_STEP)
    v_hat = vn / (1.0 - ADAM_B2 ** ADAM_STEP)
    d_ref[...] = (-ADAM_LR) * (m_hat / (jnp.sqrt(v_hat) + ADAM_EPS) + ADAM_WD * w_ref[...])


def _adamw(w, g, m, v, name):
    R, C = w.shape
    tr = 8
    for cand in (512, 256, 128, 64, 32, 16, 8):
        if R % cand == 0 and cand * C * 4 <= (1 << 20):
            tr = cand
            break

    def body(w_ref, g_ref, *rest):
        _adamw_update(g_ref[...], w_ref, *rest)

    spec = pl.BlockSpec((tr, C), lambda i: (i, 0))
    return _pcall(body, name=name, grid=(R // tr,), in_specs=[spec] * 4, out_specs=[spec] * 4,
                  out_shape=[jax.ShapeDtypeStruct((R, C), F32)] * 4, sem=("arbitrary",), args=(w, g, m, v))


def _adamw_many(ws, gs, ms, vs, name):
    n = len(ws)

    def body(*refs):
        w_r, g_r, m_r, v_r, go, do, mo, vo = (refs[k * n:(k + 1) * n] for k in range(8))
        for t in range(n):
            _adamw_update(g_r[t][...], w_r[t], m_r[t], v_r[t], go[t], do[t], mo[t], vo[t])

    vmem = pl.BlockSpec(memory_space=pltpu.VMEM)
    res = pl.pallas_call(
        body, name=name, in_specs=[vmem] * (4 * n), out_specs=[vmem] * (4 * n),
        out_shape=[jax.ShapeDtypeStruct(w.shape, F32) for w in ws] * 4,
        compiler_params=pltpu.CompilerParams(vmem_limit_bytes=VMEM_LIMIT),
    )(*ws, *gs, *ms, *vs)
    return [res[k * n:(k + 1) * n] for k in range(4)]


def _adamw_partials(w, partials, m, v, name):
    nl = len(partials)
    _, R, C = partials[0][0].shape
    tr = 8
    for cand in (256, 128, 64, 32, 16):
        if R % cand == 0 and cand * C * 4 <= (1 << 19):
            tr = cand
            break
    ni = R // tr

    def body(*refs):
        w_ref, p_refs = refs[0], refs[1:1 + 2 * nl]
        m_ref, v_ref, go_ref, d_ref, mo_ref, vo_ref = refs[1 + 2 * nl:]
        for l in range(nl):
            @pl.when(pl.program_id(0) == l)
            def _(pair=p_refs[2 * l:2 * l + 2]):
                own, sib = [((p[0].astype(F32) + p[1].astype(F32)) + p[2].astype(F32)) + p[3].astype(F32)
                            for p in pair]
                _adamw_update(own + sib, w_ref, m_ref, v_ref, go_ref, d_ref, mo_ref, vo_ref)

    def slots(l):
        return pl.BlockSpec((N_CHIPS, tr, C),
                            lambda ll, i: (0, jnp.where(ll == l, i, jnp.where(ll < l, 0, ni - 1)), 0))

    spec = pl.BlockSpec((tr, C), lambda ll, i: (ll * ni + i, 0))
    return pl.pallas_call(
        body, name=name, grid=(nl, ni),
        in_specs=[spec] + [slots(l) for l in range(nl) for _ in range(2)] + [spec, spec], out_specs=[spec] * 4,
        out_shape=[jax.ShapeDtypeStruct((nl * R, C), F32)] * 4,
        compiler_params=_cp("arbitrary", "arbitrary"),
    )(w, *[p for pair in partials for p in pair], m, v)


GATHER_SEMS = 7


def _two_level_gather(x_ref, out_ref, send_sems, recv_sems, local_sem, base=0):
    M = x_ref.shape[0]
    x, y, c = lax.axis_index("x"), lax.axis_index("y"), lax.axis_index("c")
    me, sibling = (x, y, c), (x, y, 1 - c)
    chips = [(1 - x, y), (x, 1 - y), (1 - x, 1 - y)]

    def rows(px, py, pc):
        return out_ref.at[pl.ds(pl.multiple_of((4 * px + 2 * py + pc) * M, SUBLANES), M), :]

    def copy(k, block, to, src=None):
        return pltpu.make_async_remote_copy(
            src_ref=rows(*block) if src is None else src, dst_ref=rows(*block),
            send_sem=send_sems.at[base + k], recv_sem=recv_sems.at[base + k], device_id=to, device_id_type=MESH)

    mine = pltpu.make_async_copy(x_ref, rows(*me), local_sem)
    mine.start()
    first = [copy(0, me, sibling, src=x_ref)]
    first += [copy(1 + j, me, (*chip, c), src=x_ref) for j, chip in enumerate(chips)]
    for cp in first:
        cp.start()
    passed = [copy(4 + j, (*chip, c), sibling) for j, chip in enumerate(chips)]
    for j, chip in enumerate(chips):
        copy(1 + j, (*chip, c), me).wait_recv()
        passed[j].start()
    copy(0, sibling, me).wait_recv()
    for j, chip in enumerate(chips):
        copy(4 + j, (*chip, 1 - c), me).wait_recv()
    for cp in first + passed:
        cp.wait_send()
    mine.wait()


def _prologue(pack, w_mod, jobs, name):
    M = pack.shape[0]
    L, Dm, N = w_mod.shape
    nj = len(jobs)
    rows_c = Dm // LANES
    tn = 768

    def body(*refs):
        pack_ref, w_ref = refs[:2]
        jin, refs = refs[2:2 + nj], refs[2 + nj:]
        g_ref, ca_ref, mod_ref = refs[:3]
        jout, refs = refs[3:3 + nj], refs[3 + nj:]
        part_ref, send_sems, recv_sems, local_sem, *jsems = refs
        _run_jobs("start", jobs, jin, jout, jsems)
        _two_level_gather(pack_ref, g_ref, send_sems, recv_sems, local_sem.at[0], 0)
        ca_ref[...] = jnp.zeros_like(ca_ref)
        for r in range(rows_c):
            cv = g_ref[pl.ds(r, N_DEV, stride=M), :]
            ca_ref[0:N_DEV, r * LANES:(r + 1) * LANES] = (cv * _sigmoid(cv)).astype(BF16)
        ca = ca_ref[...]
        for l in range(L):
            for n0 in range(0, N, tn):
                part_ref[l * 16:(l + 1) * 16, n0:n0 + tn] = _dot(ca, w_ref[l, :, n0:n0 + tn].astype(BF16))
        _two_level_gather(part_ref, mod_ref, send_sems, recv_sems, local_sem.at[1], GATHER_SEMS)
        _run_jobs("relay", jobs, jin, jout, jsems)
        _run_jobs("finish", jobs, jin, jout, jsems)

    vmem = pl.BlockSpec(memory_space=pltpu.VMEM)
    hbm = pl.BlockSpec(memory_space=pltpu.HBM)
    res = pl.pallas_call(
        body, name=name,
        out_shape=[jax.ShapeDtypeStruct((N_DEV * M, LANES), F32), jax.ShapeDtypeStruct((16, Dm), BF16),
                   jax.ShapeDtypeStruct((N_DEV * L * 16, N), F32)] + [job.out_shape for job in jobs],
        in_specs=[vmem, vmem] + [hbm] * nj, out_specs=[vmem, vmem, vmem] + [hbm] * nj,
        scratch_shapes=[pltpu.VMEM((L * 16, N), F32), pltpu.SemaphoreType.DMA((2 * GATHER_SEMS,)),
                        pltpu.SemaphoreType.DMA((2 * GATHER_SEMS,)), pltpu.SemaphoreType.DMA((2,))]
        + _job_scratch(nj),
        compiler_params=pltpu.CompilerParams(vmem_limit_bytes=VMEM_LIMIT),
    )(pack, w_mod, *[job.src for job in jobs])
    return res[0], res[1], res[2], list(res[3:])


def _all_gather_small(v, name, jobs=()):
    M, N = v.shape
    nj = len(jobs)

    def body(*refs):
        x_ref, jin = refs[0], refs[1:1 + nj]
        out_ref, sum_ref = refs[1 + nj:3 + nj]
        jout, (send_sems, recv_sems, local_sem, *jsems) = refs[3 + nj:3 + 2 * nj], refs[3 + 2 * nj:]
        _run_jobs("start", jobs, jin, jout, jsems)
        _two_level_gather(x_ref, out_ref, send_sems, recv_sems, local_sem)
        acc = out_ref[0:M, :]
        for d in range(1, N_DEV):
            acc = acc + out_ref[d * M:(d + 1) * M, :]
        sum_ref[...] = acc
        _run_jobs("relay", jobs, jin, jout, jsems)
        _run_jobs("finish", jobs, jin, jout, jsems)

    vmem = pl.BlockSpec(memory_space=pltpu.VMEM)
    hbm = pl.BlockSpec(memory_space=pltpu.HBM)
    res = pl.pallas_call(
        body, name=name,
        out_shape=[jax.ShapeDtypeStruct((N_DEV * M, N), F32), jax.ShapeDtypeStruct((M, N), F32)]
        + [job.out_shape for job in jobs],
        in_specs=[vmem] + [hbm] * nj, out_specs=[vmem, vmem] + [hbm] * nj,
        scratch_shapes=[pltpu.SemaphoreType.DMA((GATHER_SEMS,)), pltpu.SemaphoreType.DMA((GATHER_SEMS,)),
                        pltpu.SemaphoreType.DMA] + (_job_scratch(nj) if nj else []),
        compiler_params=pltpu.CompilerParams(vmem_limit_bytes=VMEM_LIMIT),
    )(v, *[job.src for job in jobs])
    return list(res[:2]), list(res[2:])


def _dw_mod(c_act, dmod, name):
    L, R, N = dmod.shape
    Dm = c_act.shape[1]
    tn = 768

    def body(c_ref, d_ref, o_ref):
        o_ref[0] = _dot_tn(c_ref[...], d_ref[0])

    return pl.pallas_call(
        body, name=name, grid=(L, N // tn),
        in_specs=[pl.BlockSpec((R, Dm), lambda l, n: (0, 0)), pl.BlockSpec((1, R, tn), lambda l, n: (l, 0, n))],
        out_specs=pl.BlockSpec((1, Dm, tn), lambda l, n: (l, 0, n)),
        out_shape=jax.ShapeDtypeStruct((L, Dm, N), F32),
        compiler_params=_cp("arbitrary", "arbitrary"),
    )(c_act, dmod)


_BIG = (("w_ffn1_gu", 1), ("w_ffn1_down", 0), ("w_ffn2_gu", 1), ("w_ffn2_down", 0), ("w_in", 1), ("w_out", 0))
_AXIS = dict(_BIG)

_GATHER_PLAN = {
    "first": [(0, "w_ffn1_gu"), (0, "w_ffn1_down")],
    (0, "ffn1"): [(0, "w_in"), (0, "w_out"), (0, "w_ffn2_gu")],
    (0, "mix"): [(0, "w_ffn2_down")],
    (0, "ffn2"): [(1, "w_ffn1_gu"), (1, "w_ffn1_down")],
    (1, "ffn1"): [(1, "w_in"), (1, "w_out"), (1, "w_ffn2_gu")],
    (1, "mix"): [(1, "w_ffn2_down")],
}


def _pack(arrs, rows_multiple=SUBLANES):
    flat = jnp.concatenate([a.astype(F32).reshape(-1) for a in arrs])
    unit = rows_multiple * LANES
    total = -(-flat.shape[0] // unit) * unit
    return jnp.pad(flat, (0, total - flat.shape[0])).reshape(total // LANES, LANES)


def _unpack(flat, shapes):
    out, off = [], 0
    for shp in shapes:
        n = int(math.prod(shp))
        out.append(flat[off:off + n].reshape(shp))
        off += n
    return out


def _block_diag(w):
    out = jnp.zeros((LRU_WIDTH, LRU_WIDTH), F32)
    for h in range(4):
        out = lax.dynamic_update_slice(out, w[h], (h * HEAD_DIM, h * HEAD_DIM))
    return out


def _diag_blocks(w):
    return jnp.stack([w[h * HEAD_DIM:(h + 1) * HEAD_DIM, h * HEAD_DIM:(h + 1) * HEAD_DIM] for h in range(4)])


def _rows8(*rows):
    z = jnp.zeros((8 - len(rows), rows[0].shape[-1]), F32)
    return jnp.concatenate([jnp.stack(rows), z], axis=0)


def kernel(x, c, w_mod, b_mod, g_norm, w_ffn1_gu, w_ffn1_down, w_ffn2_gu, w_ffn2_down, w_in, w_out, attn_sinks, lru_conv_w, lru_conv_b, lru_gate_a_w, lru_gate_a_b, lru_gate_x_w, lru_gate_x_b, lru_lambda, sc_conv_w, g_final, loss_target, m_w_mod, m_b_mod, m_g_norm, m_w_ffn1_gu, m_w_ffn1_down, m_w_ffn2_gu, m_w_ffn2_down, m_w_in, m_w_out, m_attn_sinks, m_lru_conv_w, m_lru_conv_b, m_lru_gate_a_w, m_lru_gate_a_b, m_lru_gate_x_w, m_lru_gate_x_b, m_lru_lambda, m_sc_conv_w, m_g_final, v_w_mod, v_b_mod, v_g_norm, v_w_ffn1_gu, v_w_ffn1_down, v_w_ffn2_gu, v_w_ffn2_down, v_w_in, v_w_out, v_attn_sinks, v_lru_conv_w, v_lru_conv_b, v_lru_gate_a_w, v_lru_gate_a_b, v_lru_gate_x_w, v_lru_gate_x_b, v_lru_lambda, v_sc_conv_w, v_g_final):
    W = dict(w_mod=w_mod, b_mod=b_mod, g_norm=g_norm, w_ffn1_gu=w_ffn1_gu, w_ffn1_down=w_ffn1_down,
             w_ffn2_gu=w_ffn2_gu, w_ffn2_down=w_ffn2_down, w_in=w_in, w_out=w_out, attn_sinks=attn_sinks,
             lru_conv_w=lru_conv_w, lru_conv_b=lru_conv_b, lru_gate_a_w=lru_gate_a_w, lru_gate_a_b=lru_gate_a_b,
             lru_gate_x_w=lru_gate_x_w, lru_gate_x_b=lru_gate_x_b, lru_lambda=lru_lambda, sc_conv_w=sc_conv_w,
             g_final=g_final)
    M1 = dict(w_mod=m_w_mod, b_mod=m_b_mod, g_norm=m_g_norm, w_ffn1_gu=m_w_ffn1_gu, w_ffn1_down=m_w_ffn1_down,
              w_ffn2_gu=m_w_ffn2_gu, w_ffn2_down=m_w_ffn2_down, w_in=m_w_in, w_out=m_w_out,
              attn_sinks=m_attn_sinks, lru_conv_w=m_lru_conv_w, lru_conv_b=m_lru_conv_b,
              lru_gate_a_w=m_lru_gate_a_w, lru_gate_a_b=m_lru_gate_a_b, lru_gate_x_w=m_lru_gate_x_w,
              lru_gate_x_b=m_lru_gate_x_b, lru_lambda=m_lru_lambda, sc_conv_w=m_sc_conv_w, g_final=m_g_final)
    V1 = dict(w_mod=v_w_mod, b_mod=v_b_mod, g_norm=v_g_norm, w_ffn1_gu=v_w_ffn1_gu, w_ffn1_down=v_w_ffn1_down,
              w_ffn2_gu=v_w_ffn2_gu, w_ffn2_down=v_w_ffn2_down, w_in=v_w_in, w_out=v_w_out,
              attn_sinks=v_attn_sinks, lru_conv_w=v_lru_conv_w, lru_conv_b=v_lru_conv_b,
              lru_gate_a_w=v_lru_gate_a_w, lru_gate_a_b=v_lru_gate_a_b, lru_gate_x_w=v_lru_gate_x_w,
              lru_gate_x_b=v_lru_gate_x_b, lru_lambda=v_lru_lambda, sc_conv_w=v_sc_conv_w, g_final=v_g_final)
    names = ["w_mod", "b_mod", "g_norm", "w_ffn1_gu", "w_ffn1_down", "w_ffn2_gu", "w_ffn2_down", "w_in", "w_out",
             "attn_sinks", "lru_conv_w", "lru_conv_b", "lru_gate_a_w", "lru_gate_a_b", "lru_gate_x_w",
             "lru_gate_x_b", "lru_lambda", "sc_conv_w", "g_final"]

    xs = x[0]
    tgt = loss_target[0]
    S = xs.shape[0]
    chip = 2 * lax.axis_index("x") + lax.axis_index("y")
    batch = 2 * chip + lax.axis_index("c")
    L = DEPTH

    full = {}

    def gather_jobs(key):
        return [_GatherJob(W[n][l].astype(BF16), _AXIS[n]) for l, n in _GATHER_PLAN.get(key, ())]

    def landed(key, outs):
        full.update(zip(_GATHER_PLAN.get(key, ()), outs))

    fwd_shapes = [(D_MODEL,), g_norm.shape, lru_conv_w.shape, sc_conv_w.shape]
    gathered, c_act, mod_all, ex = _prologue(_pack([c[0], g_norm, lru_conv_w, sc_conv_w]), w_mod,
                                             gather_jobs("first"), "prologue")
    landed("first", ex)
    gathered = gathered.reshape(N_DEV, -1)
    per_chip = [_unpack(gathered[2 * jj], fwd_shapes) for jj in range(N_CHIPS)]
    g_norm_full = jnp.concatenate([p[1] for p in per_chip], axis=-1)
    lru_conv_w_full = jnp.concatenate([p[2] for p in per_chip], axis=-1)
    sc_conv_w_full = jnp.concatenate([p[3] for p in per_chip], axis=-1)
    mod_all = mod_all.reshape(N_DEV, L, 16, -1)
    mod_rows = [lax.dynamic_index_in_dim(mod_all[2 * jj], batch, axis=1, keepdims=False) for jj in range(N_CHIPS)]
    mod = (jnp.concatenate(mod_rows, axis=-1) + b_mod).reshape(L, 9, D_MODEL)

    nrm_all = jnp.concatenate([g_norm_full[:, :, None, :], mod.reshape(L, 3, 3, D_MODEL),
                               jnp.zeros((L, 3, 4, D_MODEL), F32)], axis=2)

    def nrm_rows(l, s):
        return nrm_all[l, s]

    def mixer_params(l):
        small = jnp.concatenate([lru_conv_w_full[l], lru_conv_b[l][None], lru_gate_a_b[l][None],
                                 lru_gate_x_b[l][None], lru_lambda[l][None], sc_conv_w_full[l],
                                 jnp.zeros((5, LRU_WIDTH), F32)], axis=0)
        return (attn_sinks[l], small, _block_diag(lru_gate_a_w[l]).astype(BF16),
                _block_diag(lru_gate_x_w[l]).astype(BF16))

    saved = []
    xcur = xs
    for l in range(L):
        n1, n2, n3 = nrm_rows(l, 0), nrm_rows(l, 1), nrm_rows(l, 2)

        def ffn(which, xin, nrm, head=None):
            key = (l, which)
            (xo, h, gu, y, *stats), ex = _ffn_fwd(xin, nrm, full[(l, f"w_{which}_gu")], full[(l, f"w_{which}_down")],
                                                  f"l{l}_{which}", gather_jobs(key), head)
            landed(key, ex)
            return (xo, *stats), (xin, h, gu, y)

        (x1,), s1 = ffn("ffn1", xcur, n1)
        mp = mixer_params(l)
        (x2, h2, proj, ymix, ymo, hprev), ex = _mixer_fwd(x1, n2, full[(l, "w_in")], full[(l, "w_out")], *mp,
                                                          f"l{l}_mix", gather_jobs((l, "mix")))
        landed((l, "mix"), ex)
        s2 = (x1, h2, proj, ymix, ymo, hprev, mp)
        (xcur, *stats), s3 = ffn("ffn2", x2, n3, (_rows8(g_final), tgt) if l == L - 1 else None)
        saved.append((n1, n2, n3, s1, s2, s3))

    dx, stats = xcur, stats[0]
    loss_here, d_g_final = stats[1, 0:1], stats[0]

    recv, theirs = {}, {}
    waiting = []

    def carried(fn, *a, extra=()):
        items = waiting + list(extra)
        waiting.clear()
        outs, landed_now = fn(*a, jobs=[_SiblingJob(recv[(ll, n)]) if g is None else _ScatterJob(g, _AXIS[n])
                                        for ll, n, g in items])
        for (ll, n, g), arr in zip(items, landed_now):
            if g is None:
                theirs[(ll, n)] = arr
            else:
                recv[(ll, n)] = arr
                waiting.append((ll, n, None))
        return outs

    dmod, d_gnorm, d_small = [None] * L, [None] * L, [None] * L
    for l in reversed(range(L)):
        n1, n2, n3, s1, s2, s3 = saved[l]

        def plain(fn, *a):
            return fn(*a)[0]

        def ffn_bwd(which, dxo, sv, nrm, last):
            xin, h, gu, y = sv
            tag = f"l{l}_{which}"
            dgu, dgate, dw_down = carried(
                _ffn_down_bwd, dxo, gu, y, full[(l, f"w_{which}_down")], nrm, tag + "_down_bwd")
            dw_gu = carried(_atb, h, dgu, BF16, 1024, 2816, tag + "_dw_gu", extra=[(l, f"w_{which}_down", dw_down)])
            mine = [(l, f"w_{which}_gu", dw_gu)]
            dxi, red = (carried if last else plain)(
                _nt_norm_bwd, dgu, full[(l, f"w_{which}_gu")], xin, nrm, dxo, tag + "_gu_bwd",
                **(dict(extra=mine) if last else {}))
            if not last:
                waiting.extend(mine)
            return dxi, (red[0], red[1], dgate[0]), red[2]

        dx, dm3, dg3 = ffn_bwd("ffn2", dx, s3, n3, False)
        x_in, h2, proj, ymix, ymo, hprev, mp = s2
        dx, red, dgate, dw_out, dw_in, dsm, dsink, dwa, dwx = carried(
            _mixer_bwd, x_in, n2, dx, h2, proj, ymix, ymo, hprev, full[(l, "w_in")], full[(l, "w_out")], *mp,
            f"l{l}_mix_bwd")
        waiting.extend([(l, "w_out", dw_out), (l, "w_in", dw_in)])
        dm2, dg2 = (red[0], red[1], dgate[0]), red[2]
        dx, dm1, dg1 = ffn_bwd("ffn1", dx, s1, n1, l == 0)
        dmod[l] = jnp.stack(list(dm1) + list(dm2) + list(dm3))
        d_gnorm[l] = jnp.stack([dg1, dg2, dg3])
        d_small[l] = (dsink[:, 0], dsm[0:4], dsm[4], _diag_blocks(dwa), dsm[5], _diag_blocks(dwx), dsm[6],
                      dsm[7], dsm[8:11])
    grad_x = dx[None]

    def both(k):
        return jnp.stack([d_small[0][k], d_small[1][k]])
    small_names = ["g_norm", "attn_sinks", "lru_conv_w", "lru_conv_b", "lru_gate_a_w", "lru_gate_a_b",
                   "lru_gate_x_w", "lru_gate_x_b", "lru_lambda", "sc_conv_w", "g_final"]
    small_parts = [jnp.stack(d_gnorm)] + [both(k) for k in range(9)] + [d_g_final]
    dmod_flat = jnp.stack(dmod).reshape(-1)
    bwd_gathered, bwd_sum = carried(_all_gather_small, _pack([dmod_flat] + small_parts + [loss_here]),
                                    "gather_small_bwd")
    n_mod = dmod_flat.shape[0]
    dmod_all = bwd_gathered.reshape(N_DEV, -1)[:, :n_mod].reshape(N_DEV, L, 9 * D_MODEL)
    bwd_sum = bwd_sum.reshape(-1)
    G = {"b_mod": bwd_sum[:n_mod].reshape(L, 9 * D_MODEL)}
    *small_sums, loss = _unpack(bwd_sum[n_mod:], [p.shape for p in small_parts] + [(1,)])
    loss = loss[0]
    G.update(zip(small_names, small_sums))
    for n in ("g_norm", "lru_conv_w", "sc_conv_w"):
        wdt = W[n].shape[-1]
        G[n] = lax.dynamic_slice_in_dim(G[n], chip * wdt, wdt, axis=G[n].ndim - 1)

    ncol = w_mod.shape[-1]
    dmod_cols = lax.dynamic_slice_in_dim(dmod_all, chip * ncol, ncol, axis=2)
    dmod_rows = jnp.pad(jnp.swapaxes(dmod_cols, 0, 1), ((0, 0), (0, 16 - N_DEV), (0, 0))).astype(BF16)
    g_w_mod = _dw_mod(c_act, dmod_rows, "dw_mod")

    out_g, out_d, out_m, out_v = {}, {}, {}, {}
    res, _ = _adamw(w_mod.reshape(-1, ncol), g_w_mod.reshape(-1, ncol), m_w_mod.reshape(-1, ncol),
                    v_w_mod.reshape(-1, ncol), "adamw_w_mod")
    out_g["w_mod"], out_d["w_mod"], out_m["w_mod"], out_v["w_mod"] = [r.reshape(w_mod.shape) for r in res]
    for n, _ in _BIG:
        shp = W[n].shape
        flat = (shp[0] * shp[1], shp[2])
        res = _adamw_partials(W[n].reshape(flat), [(recv[(l, n)], theirs[(l, n)]) for l in range(L)],
                              M1[n].reshape(flat), V1[n].reshape(flat), f"adamw_{n}")
        out_g[n], out_d[n], out_m[n], out_v[n] = [r.reshape(shp) for r in res]
    rest = ["b_mod"] + small_names

    def rows(a):
        return a.reshape(-1, a.shape[-1])
    res = _adamw_many([rows(W[n]) for n in rest], [rows(G[n]) for n in rest], [rows(M1[n]) for n in rest],
                      [rows(V1[n]) for n in rest], "adamw_small")
    for dst, group in zip((out_g, out_d, out_m, out_v), res):
        dst.update({n: r.reshape(W[n].shape) for n, r in zip(rest, group)})

    return (loss, grad_x, *[out_g[n] for n in names], *[out_d[n] for n in names],
            *[out_m[n] for n in names], *[out_v[n] for n in names])
```

```python
import math

import jax
import jax.numpy as jnp
from jax import lax
from jax.experimental import pallas as pl
from jax.experimental.pallas import tpu as pltpu

F32 = jnp.float32
BF16 = jnp.bfloat16

D_MODEL = 1024
DEPTH = 2
HEAD_DIM = 64
N_Q_HEADS = 8
ATTN_WIDTH = 512
KV_WIDTH = 128
LRU_WIDTH = 256
CONV_WIDTH = 256
IN_PROJ_WIDTH = 2048
BLOCK = 128
D_FF = 2816
EPS = 1e-6
NEG_INF = -1e30
LRU_C = 8.0
N_CHIPS = 4
N_DEV = 8

C_Q, C_KV, C_LX, C_LG, C_SB, C_SC, C_SX = 0, 512, 768, 1024, 1280, 1536, 1792

ADAM_LR = 0.001
ADAM_B1 = 0.9
ADAM_B2 = 0.999
ADAM_EPS = 1e-08
ADAM_WD = 0.01
ADAM_STEP = 10

LANES = 128
SUBLANES = 8
VMEM_LIMIT = 56 * 1024 * 1024
MIX_TILE = 256

MESH = pl.DeviceIdType.MESH


def _cp(*sem):
    return pltpu.CompilerParams(dimension_semantics=sem, vmem_limit_bytes=VMEM_LIMIT)


def _tile(n, pref):
    t = min(n, pref)
    while n % t:
        t //= 2
    return t


MXU_DIM = 256


def _resident(shape):
    return pl.BlockSpec(shape, lambda *_: (0, 0), pipeline_mode=pl.Buffered(1))


def _sigmoid(v):
    return 1.0 / (1.0 + jnp.exp(-v))


def _expm1(v):
    series = v * (1.0 + v * (0.5 + v * (1.0 / 6.0)))
    return jnp.where(v > -0.01, series, jnp.exp(v) - 1.0)


def _softplus_neg(lam):
    e = jnp.exp(-jnp.abs(lam))
    log1p = jnp.where(e < 1e-2, e * (1.0 - e * (0.5 - e * (1.0 / 3.0))), jnp.log(1.0 + e))
    return jnp.maximum(-lam, 0.0) + log1p


_GELU_K = math.sqrt(2.0 / math.pi)
_GELU_C = 0.044715


def _gelu(v):
    t = jnp.tanh(_GELU_K * (v + _GELU_C * v * v * v))
    return 0.5 * v * (1.0 + t), t


def _gelu_grad(v, t):
    return 0.5 * (1.0 + t) + 0.5 * v * (1.0 - t * t) * _GELU_K * (1.0 + 3.0 * _GELU_C * v * v)


def _dot(a, b):
    return jnp.dot(a, b, preferred_element_type=F32)


def _dot_nt(a, b):
    return lax.dot_general(a, b, (((1,), (1,)), ((), ())), preferred_element_type=F32)


def _dot_tn(a, b):
    return lax.dot_general(a, b, (((0,), (0,)), ((), ())), preferred_element_type=F32)


def _window(ref, axis, j, width):
    start = pl.multiple_of(j * width, LANES if axis == 1 else 16)
    if axis == 1:
        return ref.at[:, pl.ds(start, width)]
    return ref.at[pl.ds(start, width), :]


def _chip_peers():
    x, y, c = lax.axis_index("x"), lax.axis_index("y"), lax.axis_index("c")
    return x, y, c, [(1 - x, y), (x, 1 - y), (1 - x, 1 - y)]


class _GatherJob:
    def __init__(self, shard, axis):
        self.src, self.axis, self.width, self.half = shard, axis, shard.shape[axis], shard.shape[0] // 2
        full = tuple(d * N_CHIPS if k == axis else d for k, d in enumerate(shard.shape))
        self.out_shape = jax.ShapeDtypeStruct(full, shard.dtype)

    def _piece(self, ref, j, hf):
        if self.axis == 1:
            return ref.at[pl.ds(pl.multiple_of(hf * self.half, 16), self.half),
                          pl.ds(pl.multiple_of(j * self.width, LANES), self.width)]
        return ref.at[pl.ds(pl.multiple_of(j * self.width + hf * self.half, 16), self.half), :]

    def _copies(self, src, dst, send, recv, loc, t):
        x, y, c, chips = _chip_peers()
        j = 2 * x + y
        owners = [2 * px + py for px, py in chips]
        local = pltpu.make_async_copy(src, _window(dst, self.axis, j, self.width), loc.at[t])
        mine = src.at[pl.ds(pl.multiple_of(c * self.half, 16), self.half), :]

        def ici(k, owner):
            return pltpu.make_async_remote_copy(
                src_ref=mine, dst_ref=self._piece(dst, owner, c), send_sem=send.at[JOB_SEMS * t + k],
                recv_sem=recv.at[JOB_SEMS * t + k], device_id=(*chips[k], c), device_id_type=MESH)

        def relay(k, hf):
            piece = self._piece(dst, owners[k], hf)
            return pltpu.make_async_remote_copy(
                src_ref=piece, dst_ref=piece, send_sem=send.at[JOB_SEMS * t + 4 + k],
                recv_sem=recv.at[JOB_SEMS * t + 4 + k],
                device_id=(x, y, 1 - c), device_id_type=MESH)

        return (local, [ici(k, j) for k in range(3)], [ici(k, owners[k]) for k in range(3)],
                [relay(k, c) for k in range(3)], [relay(k, 1 - c) for k in range(3)])

    def start(self, *a):
        local, ici_out, _, _, _ = self._copies(*a)
        local.start()
        for cp in ici_out:
            cp.start()

    def relay(self, *a):
        _, _, ici_in, relay_out, _ = self._copies(*a)
        for arrived, onward in zip(ici_in, relay_out):
            arrived.wait_recv()
            onward.start()

    def finish(self, *a):
        local, ici_out, _, relay_out, relay_in = self._copies(*a)
        for cp in relay_in:
            cp.wait_recv()
        for cp in ici_out + relay_out:
            cp.wait_send()
        local.wait()


class _ScatterJob:
    def __init__(self, full, axis):
        self.src, self.axis, self.width = full, axis, full.shape[axis] // N_CHIPS
        shard = tuple(self.width if k == axis else d for k, d in enumerate(full.shape))
        self.out_shape = jax.ShapeDtypeStruct((N_CHIPS,) + shard, full.dtype)

    def _copies(self, src, dst, send, recv, loc, t):
        x, y, c, chips = _chip_peers()
        local = pltpu.make_async_copy(_window(src, self.axis, 2 * x + y, self.width), dst.at[3], loc.at[t])
        sends = [pltpu.make_async_remote_copy(
            src_ref=_window(src, self.axis, 2 * px + py, self.width), dst_ref=dst.at[k],
            send_sem=send.at[JOB_SEMS * t + k], recv_sem=recv.at[JOB_SEMS * t + k], device_id=(px, py, c),
            device_id_type=MESH) for k, (px, py) in enumerate(chips)]
        return local, sends

    def start(self, *a):
        local, sends = self._copies(*a)
        local.start()
        for cp in sends:
            cp.start()

    def relay(self, *a):
        pass

    def finish(self, *a):
        local, sends = self._copies(*a)
        for cp in sends:
            cp.wait_recv()
        for cp in sends:
            cp.wait_send()
        local.wait()


class _SiblingJob:
    def __init__(self, arr):
        self.src, self.out_shape = arr, jax.ShapeDtypeStruct(arr.shape, arr.dtype)

    def _copy(self, src, dst, send, recv, loc, t):
        x, y, c = lax.axis_index("x"), lax.axis_index("y"), lax.axis_index("c")
        return pltpu.make_async_remote_copy(
            src_ref=src, dst_ref=dst, send_sem=send.at[JOB_SEMS * t], recv_sem=recv.at[JOB_SEMS * t],
            device_id=(x, y, 1 - c), device_id_type=MESH)

    def start(self, *a):
        self._copy(*a).start()

    def relay(self, *a):
        pass

    def finish(self, *a):
        self._copy(*a).wait()


JOB_SEMS = 8


def _run_jobs(phase, jobs, srcs, dsts, sems):
    for t, job in enumerate(jobs):
        getattr(job, phase)(srcs[t], dsts[t], *sems, t)


def _job_scratch(n):
    return [pltpu.SemaphoreType.DMA((JOB_SEMS * n,)), pltpu.SemaphoreType.DMA((JOB_SEMS * n,)),
            pltpu.SemaphoreType.DMA((n,))]


def _pcall(body, *, name, grid, in_specs, out_specs, out_shape, sem, args, scratch_shapes=(), jobs=()):
    in_specs, out_specs, out_shape = list(in_specs), list(out_specs), list(out_shape)
    scratch_shapes = list(scratch_shapes)
    if not jobs:
        res = pl.pallas_call(body, name=name, grid=grid, in_specs=in_specs, out_specs=out_specs, out_shape=out_shape,
                             scratch_shapes=scratch_shapes, compiler_params=_cp(*sem))(*args)
        return list(res), []
    n_in, n_out, n_scr, nj = len(args), len(out_shape), len(scratch_shapes), len(jobs)
    n_steps = math.prod(grid)
    relay_step = (3 * n_steps) // 4
    relay_early = 0 < relay_step < n_steps - 1

    def wrapped(*refs):
        ins, refs = refs[:n_in], refs[n_in:]
        jin, refs = refs[:nj], refs[nj:]
        outs, refs = refs[:n_out], refs[n_out:]
        jout, refs = refs[:nj], refs[nj:]
        scr, sems = refs[:n_scr], refs[n_scr:]
        step = pl.program_id(0)
        for d in range(1, len(grid)):
            step = step * grid[d] + pl.program_id(d)

        @pl.when(step == 0)
        def _():
            _run_jobs("start", jobs, jin, jout, sems)

        if relay_early:
            @pl.when(step == relay_step)
            def _():
                _run_jobs("relay", jobs, jin, jout, sems)
        body(*ins, *outs, *scr)

        @pl.when(step == n_steps - 1)
        def _():
            if not relay_early:
                _run_jobs("relay", jobs, jin, jout, sems)
            _run_jobs("finish", jobs, jin, jout, sems)

    hbm = pl.BlockSpec(memory_space=pltpu.HBM)
    res = pl.pallas_call(
        wrapped, name=name, grid=grid, in_specs=in_specs + [hbm] * nj, out_specs=out_specs + [hbm] * nj,
        out_shape=out_shape + [job.out_shape for job in jobs], scratch_shapes=scratch_shapes + _job_scratch(nj),
        compiler_params=_cp(*sem))(*args, *[job.src for job in jobs])
    return list(res[:n_out]), list(res[n_out:])


def _hidden_chunks(k):
    return [(c0, min(6 * MXU_DIM, k - c0)) for c0 in range(0, k, 6 * MXU_DIM)]


def _loss_head(xv, gain, tgt, st_ref):
    dm = xv.shape[-1]
    rstd = lax.rsqrt(jnp.mean(xv * xv, axis=-1, keepdims=True) + EPS)
    xn = xv * rstd
    err = xn * gain - tgt
    st_ref[1:2, :] += jnp.full((1, dm), 0.5 / dm, F32) * jnp.sum(err * err)
    dy = err * (1.0 / dm)
    st_ref[0:1, :] += jnp.sum(dy * xn, axis=0, keepdims=True)
    dxn = dy * gain
    return rstd * (dxn - xn * jnp.mean(dxn * xn, axis=-1, keepdims=True))


def _ffn_fwd(x, nrm, w_gu, w_down, name, jobs=(), head=None):
    S, Dm = x.shape
    K = w_down.shape[0]
    tm = _tile(S, 256)

    def body(x_ref, nrm_ref, wgu_ref, wdn_ref, *rest):
        if head is None:
            o_ref, h_ref, gu_ref, y_ref = rest
        else:
            gf_ref, t_ref, o_ref, h_ref, gu_ref, y_ref, st_ref = rest

            @pl.when(pl.program_id(0) == 0)
            def _():
                st_ref[...] = jnp.zeros_like(st_ref)
        xv = x_ref[...]
        rstd = lax.rsqrt(jnp.mean(xv * xv, axis=-1, keepdims=True) + EPS)
        hn = (xv * rstd) * nrm_ref[0:1, :]
        hb = (hn * (1.0 + nrm_ref[2:3, :]) + nrm_ref[1:2, :]).astype(BF16)
        h_ref[...] = hb
        y = jnp.zeros((tm, Dm), F32)
        for c0, cs in _hidden_chunks(K):
            g = _dot(hb, wgu_ref[:, c0:c0 + cs])
            u = _dot(hb, wgu_ref[:, K + c0:K + c0 + cs])
            gu_ref[:, c0:c0 + cs] = g.astype(BF16)
            gu_ref[:, K + c0:K + c0 + cs] = u.astype(BF16)
            y = y + _dot((g * _sigmoid(g) * u).astype(BF16), wdn_ref[c0:c0 + cs, :])
        xo = xv + (0.5 * nrm_ref[3:4, :]) * y
        o_ref[...] = xo if head is None else _loss_head(xo, gf_ref[0:1, :], t_ref[...], st_ref)
        y_ref[...] = y.astype(BF16)

    row = lambda i: (i, 0)
    fix = lambda i: (0, 0)
    in_specs = [pl.BlockSpec((tm, Dm), row), pl.BlockSpec((8, Dm), fix), _resident((Dm, 2 * K)), _resident((K, Dm))]
    out_specs = [pl.BlockSpec((tm, Dm), row), pl.BlockSpec((tm, Dm), row), pl.BlockSpec((tm, 2 * K), row),
                 pl.BlockSpec((tm, Dm), row)]
    out_shape = [jax.ShapeDtypeStruct((S, Dm), F32), jax.ShapeDtypeStruct((S, Dm), BF16),
                 jax.ShapeDtypeStruct((S, 2 * K), BF16), jax.ShapeDtypeStruct((S, Dm), BF16)]
    args = (x, nrm, w_gu, w_down)
    if head is not None:
        in_specs += [pl.BlockSpec((8, Dm), fix), pl.BlockSpec((tm, Dm), row)]
        out_specs.append(pl.BlockSpec((8, Dm), fix))
        out_shape.append(jax.ShapeDtypeStruct((8, Dm), F32))
        args += tuple(head)
    return _pcall(body, name=name, grid=(S // tm,), in_specs=in_specs, out_specs=out_specs, out_shape=out_shape,
                  sem=("arbitrary",), args=args, jobs=jobs)


def _ffn_down_bwd(dxo, gu, y, w, nrm, name, jobs=()):
    S, Dm = dxo.shape
    K = w.shape[0]
    Ka = gu.shape[1]
    coef = 0.5
    tm = _tile(S, 256)
    n_steps = S // tm
    chunks = _hidden_chunks(K)

    def body(dxo_ref, y_ref, w_ref, nrm_ref, a_ref, da_ref, dgate_ref, dw_ref, acc):
        @pl.when(pl.program_id(0) == 0)
        def _():
            dgate_ref[...] = jnp.zeros_like(dgate_ref)
            acc[...] = jnp.zeros_like(acc)

        dxo_v = dxo_ref[...]
        dyb = ((coef * nrm_ref[3:4, :]) * dxo_v).astype(BF16)
        dgate_ref[0:1, :] += jnp.sum(coef * y_ref[...].astype(F32) * dxo_v, axis=0, keepdims=True)
        for c0, cs in chunks:
            dact = _dot_nt(dyb, w_ref[c0:c0 + cs, :])
            g = a_ref[:, c0:c0 + cs].astype(F32)
            u = a_ref[:, K + c0:K + c0 + cs].astype(F32)
            s = _sigmoid(g)
            si = g * s
            da_ref[:, c0:c0 + cs] = (dact * u * (s * (1.0 + g * (1.0 - s)))).astype(BF16)
            da_ref[:, K + c0:K + c0 + cs] = (dact * si).astype(BF16)
            acc[c0:c0 + cs, :] += _dot_tn((si * u).astype(BF16), dyb)

        @pl.when(pl.program_id(0) == n_steps - 1)
        def _():
            dw_ref[...] = acc[...].astype(BF16)

    row = lambda i: (i, 0)
    fix = lambda i: (0, 0)
    return _pcall(
        body, name=name, grid=(n_steps,),
        in_specs=[pl.BlockSpec((tm, Dm), row), pl.BlockSpec((tm, Dm), row), _resident((K, Dm)),
                  pl.BlockSpec((8, Dm), fix), pl.BlockSpec((tm, Ka), row)],
        out_specs=[pl.BlockSpec((tm, Ka), row), pl.BlockSpec((8, Dm), fix), _resident((K, Dm))],
        out_shape=[jax.ShapeDtypeStruct((S, Ka), BF16), jax.ShapeDtypeStruct((8, Dm), F32),
                   jax.ShapeDtypeStruct((K, Dm), BF16)],
        scratch_shapes=[pltpu.VMEM((K, Dm), F32)],
        sem=("arbitrary",), args=(dxo, y, w, nrm, gu), jobs=jobs)


def _atb(a, b, out_dtype, bm, bn, name, jobs=()):
    S, M = a.shape
    N = b.shape[1]
    bk = _tile(S, 1024)
    nk = S // bk

    def body(a_ref, b_ref, o_ref, acc):
        k = pl.program_id(2)

        @pl.when(k == 0)
        def _():
            acc[...] = jnp.zeros_like(acc)
        acc[...] += _dot_tn(a_ref[...], b_ref[...])

        @pl.when(k == nk - 1)
        def _():
            o_ref[...] = acc[...].astype(o_ref.dtype)

    (out,), extra = _pcall(
        body, name=name, grid=(M // bm, N // bn, nk),
        in_specs=[pl.BlockSpec((bk, bm), lambda m, n, k: (k, m)),
                  pl.BlockSpec((bk, bn), lambda m, n, k: (k, n))],
        out_specs=[pl.BlockSpec((bm, bn), lambda m, n, k: (m, n))],
        out_shape=[jax.ShapeDtypeStruct((M, N), out_dtype)],
        scratch_shapes=[pltpu.VMEM((bm, bn), F32)],
        sem=("arbitrary", "arbitrary", "arbitrary"), args=(a, b), jobs=jobs)
    return out, extra


def _norm_bwd(dh, xv, nrm_ref, red_ref):
    rstd = lax.rsqrt(jnp.mean(xv * xv, axis=-1, keepdims=True) + EPS)
    xn = xv * rstd
    gain = nrm_ref[0:1, :]
    hn = xn * gain
    dhn = dh * (1.0 + nrm_ref[2:3, :])
    red_ref[0:1, :] += jnp.sum(dh, axis=0, keepdims=True)
    red_ref[1:2, :] += jnp.sum(dh * hn, axis=0, keepdims=True)
    red_ref[2:3, :] += jnp.sum(dhn * xn, axis=0, keepdims=True)
    dxn = dhn * gain
    return rstd * (dxn - xn * jnp.mean(dxn * xn, axis=-1, keepdims=True))


def _nt_norm_bwd(dout, w, x, nrm, dxo, name, jobs=()):
    S, N = dout.shape
    Dm = w.shape[0]
    tm = _tile(S, 512)

    def body(do_ref, w_ref, x_ref, nrm_ref, dxo_ref, dx_ref, red_ref):
        @pl.when(pl.program_id(0) == 0)
        def _():
            red_ref[...] = jnp.zeros_like(red_ref)
        dh = _dot_nt(do_ref[...], w_ref[...])
        dx_ref[...] = dxo_ref[...] + _norm_bwd(dh, x_ref[...], nrm_ref, red_ref)

    return _pcall(
        body, name=name, grid=(S // tm,),
        in_specs=[pl.BlockSpec((tm, N), lambda i: (i, 0)),
                  _resident((Dm, N)),
                  pl.BlockSpec((tm, Dm), lambda i: (i, 0)),
                  pl.BlockSpec((8, Dm), lambda i: (0, 0)),
                  pl.BlockSpec((tm, Dm), lambda i: (i, 0))],
        out_specs=[pl.BlockSpec((tm, Dm), lambda i: (i, 0)),
                   pl.BlockSpec((8, Dm), lambda i: (0, 0))],
        out_shape=[jax.ShapeDtypeStruct((S, Dm), F32), jax.ShapeDtypeStruct((8, Dm), F32)],
        sem=("arbitrary",), args=(dout, w, x, nrm, dxo), jobs=jobs)


def _alibi_slope(h):
    return float(2.0 ** (-8.0 * (h + 1) / N_Q_HEADS))


def _head_planes(pair_cols):
    lane = lax.broadcasted_iota(jnp.int32, pair_cols.shape, 1)
    low = lane < HEAD_DIM
    h0_lo = jnp.where(low, pair_cols, 0.0)
    h1_hi = jnp.where(low, 0.0, pair_cols)
    h0_hi = pltpu.roll(h0_lo, HEAD_DIM, 1)
    h1_lo = pltpu.roll(h1_hi, HEAD_DIM, 1)
    return ((h0_lo.astype(BF16), h0_hi.astype(BF16)), (h1_lo.astype(BF16), h1_hi.astype(BF16)))


def _band_geometry(first_block):
    qi = lax.broadcasted_iota(jnp.int32, (BLOCK, BLOCK), 0)
    kj = lax.broadcasted_iota(jnp.int32, (BLOCK, BLOCK), 1)
    own = kj <= qi
    dist = jnp.where(own, qi - kj, qi + BLOCK - kj).astype(F32)
    valid = kj <= qi + BLOCK * (1 - first_block)
    return own, dist, valid


def _fold(band, own):
    return jnp.where(own, band[:, BLOCK:], band[:, :BLOCK])


def _unfold(v, own):
    return jnp.concatenate([jnp.where(own, 0.0, v), jnp.where(own, v, 0.0)], axis=1)


def _softmax_band(s, h, geometry, sink):
    own, dist, valid = geometry
    s = jnp.where(valid, s - _alibi_slope(h) * dist, NEG_INF)
    m = jnp.maximum(jnp.max(s, axis=-1, keepdims=True), sink)
    p = jnp.exp(s - m)
    e_sink = jnp.exp(sink - m)
    inv = 1.0 / (jnp.sum(p, axis=-1, keepdims=True) + e_sink)
    return p * inv, e_sink * inv


def _past(cur, prev, s, row):
    return jnp.where(row < s, pltpu.roll(prev, s, 0), pltpu.roll(cur, s, 0))


def _future(cur, nxt, s, row):
    T = cur.shape[0]
    return jnp.where(row >= T - s, pltpu.roll(nxt, T - s, 0), pltpu.roll(cur, T - s, 0))


def _edge_row(v, last):
    T = v.shape[0]
    r8 = lax.broadcasted_iota(jnp.int32, (SUBLANES, v.shape[1]), 0)
    blk = v[T - SUBLANES:, :] if last else v[:SUBLANES, :]
    return jnp.sum(jnp.where(r8 == (SUBLANES - 1 if last else 0), blk, 0.0), axis=0, keepdims=True)


def _lru_gates(lx, lx_prev, small_ref, wa_ref, wx_ref, row, t0):
    xc = (small_ref[4:5, :] + small_ref[3:4, :] * lx + small_ref[2:3, :] * _past(lx, lx_prev, 1, row)
          + small_ref[1:2, :] * _past(lx, lx_prev, 2, row) + small_ref[0:1, :] * _past(lx, lx_prev, 3, row))
    xcb = xc.astype(BF16)
    r = _sigmoid(_dot(xcb, wa_ref[...]) + small_ref[5:6, :])
    ig = _sigmoid(_dot(xcb, wx_ref[...]) + small_ref[6:7, :])
    sp = _softplus_neg(small_ref[7:8, :])
    la = (-LRU_C) * r * sp
    a = jnp.exp(la)
    first = (row + t0) == 0
    mult = jnp.where(first, 1.0, jnp.sqrt(-_expm1(2.0 * la)))
    return xc, xcb, r, ig, sp, a, mult, first


def _mixer_fwd(x, nrm, w_in, w_out, sinks, small, wa, wx, name, jobs=()):
    S, Dm = x.shape
    T = MIX_TILE
    nT = S // T
    nb = T // BLOCK

    def body(x_ref, nrm_ref, w_in_ref, w_out_ref, sink_ref, small_ref, wa_ref, wx_ref,
             xo_ref, h_ref, proj_ref, y_ref, ymo_ref, hp_ref, *carried_state):
        xv = x_ref[...]
        rstd = lax.rsqrt(jnp.mean(xv * xv, axis=-1, keepdims=True) + EPS)
        hn = (xv * rstd) * nrm_ref[0:1, :]
        hb = (hn * (1.0 + nrm_ref[2:3, :]) + nrm_ref[1:2, :]).astype(BF16)
        h_ref[...] = hb
        proj_ref[...] = _dot(hb, w_in_ref[...])
        core(proj_ref, sink_ref, small_ref, wa_ref, wx_ref, y_ref, hp_ref, *carried_state)
        yo = _dot(y_ref[...], w_out_ref[...])
        xo_ref[...] = xv + nrm_ref[3:4, :] * yo
        ymo_ref[...] = yo.astype(BF16)

    def core(proj_ref, sink_ref, small_ref, wa_ref, wx_ref, y_ref, hp_ref, kvp, lxp, zp, hcar):
        i = pl.program_id(0)

        @pl.when(i == 0)
        def _():
            kvp[...] = jnp.zeros_like(kvp)
            lxp[...] = jnp.zeros_like(lxp)
            zp[...] = jnp.zeros_like(zp)
            hcar[...] = jnp.zeros_like(hcar)

        row = lax.broadcasted_iota(jnp.int32, (T, LRU_WIDTH), 0)

        kv = proj_ref[:, C_KV:C_KV + 2 * KV_WIDTH]
        ext = jnp.concatenate([kvp[...], kv], axis=0)
        kx = _head_planes(ext[:, :KV_WIDTH])
        vx = _head_planes(ext[:, KV_WIDTH:])
        first_tile = jnp.where(i == 0, 1, 0)
        units = [(b, pair, e) for b in range(nb) for pair in range(N_Q_HEADS // 2) for e in range(2)]
        geometry = [_band_geometry(first_tile if b == 0 else 0) for b in range(nb)]
        keys = [slice(b * BLOCK, (b + 2) * BLOCK) for b in range(nb)]
        qp = {(b, pair): (proj_ref[b * BLOCK:(b + 1) * BLOCK, pair * LANES:(pair + 1) * LANES] * 0.125).astype(BF16)
              for b in range(nb) for pair in range(N_Q_HEADS // 2)}
        scores = [_fold(_dot_nt(qp[(b, pair)], kx[pair // 2][e][keys[b]]), geometry[b][0]) for b, pair, e in units]
        probs = [_unfold(_softmax_band(s, 2 * pair + e, geometry[b], sink_ref[2 * pair + e])[0],
                         geometry[b][0]).astype(BF16) for s, (b, pair, e) in zip(scores, units)]
        outs = [_dot(p, vx[pair // 2][e][keys[b]]) for p, (b, pair, e) in zip(probs, units)]
        for u in range(0, len(units), 2):
            b, pair, _ = units[u]
            y_ref[b * BLOCK:(b + 1) * BLOCK, pair * LANES:(pair + 1) * LANES] = (outs[u] + outs[u + 1]).astype(BF16)
        kvp[...] = kv[T - BLOCK:, :]

        lx = proj_ref[:, C_LX:C_LX + LRU_WIDTH]
        xc, _, _, ig, _, a, mult, _ = _lru_gates(lx, lxp[...], small_ref, wa_ref, wx_ref, row, i * T)
        lxp[...] = lx
        aa = a
        bb = mult * (ig * xc)
        s = 1
        while s < T:
            a_sh = jnp.where(row >= s, pltpu.roll(aa, s, 0), 1.0)
            b_sh = jnp.where(row >= s, pltpu.roll(bb, s, 0), 0.0)
            bb = aa * b_sh + bb
            aa = aa * a_sh
            s *= 2
        hc = hcar[0:1, :]
        hh = bb + aa * hc
        hp_ref[...] = jnp.where(row < 1, hc, pltpu.roll(hh, 1, 0))
        hcar[...] = jnp.broadcast_to(_edge_row(hh, True), hcar.shape)
        gl, _ = _gelu(proj_ref[:, C_LG:C_LG + LRU_WIDTH])
        y_ref[:, ATTN_WIDTH:ATTN_WIDTH + LRU_WIDTH] = (gl * hh).astype(BF16)

        z = proj_ref[:, C_SC:C_SC + CONV_WIDTH] * proj_ref[:, C_SX:C_SX + CONV_WIDTH]
        c3 = (small_ref[10:11, :] * z + small_ref[9:10, :] * _past(z, zp[...], 1, row)
              + small_ref[8:9, :] * _past(z, zp[...], 2, row))
        zp[...] = z
        y_ref[:, ATTN_WIDTH + LRU_WIDTH:] = (proj_ref[:, C_SB:C_SB + CONV_WIDTH] * c3).astype(BF16)

    fix = lambda i: (0, 0)
    row = lambda i: (i, 0)
    return _pcall(
        body, name=name, grid=(nT,),
        in_specs=[pl.BlockSpec((T, Dm), row), pl.BlockSpec((8, Dm), fix),
                  _resident((Dm, IN_PROJ_WIDTH)), _resident((D_MODEL, Dm)),
                  pl.BlockSpec(memory_space=pltpu.SMEM),
                  pl.BlockSpec((16, LRU_WIDTH), fix),
                  pl.BlockSpec((LRU_WIDTH, LRU_WIDTH), fix),
                  pl.BlockSpec((LRU_WIDTH, LRU_WIDTH), fix)],
        out_specs=[pl.BlockSpec((T, Dm), row), pl.BlockSpec((T, Dm), row), pl.BlockSpec((T, IN_PROJ_WIDTH), row),
                   pl.BlockSpec((T, D_MODEL), row), pl.BlockSpec((T, Dm), row), pl.BlockSpec((T, LRU_WIDTH), row)],
        out_shape=[jax.ShapeDtypeStruct((S, Dm), F32), jax.ShapeDtypeStruct((S, Dm), BF16),
                   jax.ShapeDtypeStruct((S, IN_PROJ_WIDTH), F32), jax.ShapeDtypeStruct((S, D_MODEL), BF16),
                   jax.ShapeDtypeStruct((S, Dm), BF16), jax.ShapeDtypeStruct((S, LRU_WIDTH), F32)],
        scratch_shapes=[pltpu.VMEM((BLOCK, 2 * KV_WIDTH), F32), pltpu.VMEM((T, LRU_WIDTH), F32),
                        pltpu.VMEM((T, CONV_WIDTH), F32), pltpu.VMEM((SUBLANES, LRU_WIDTH), F32)],
        sem=("arbitrary",), args=(x, nrm, w_in, w_out, sinks, small, wa, wx), jobs=jobs)


def _mixer_bwd(x, nrm, dxo, h, proj, ymix, ymo, hprev, w_in, w_out, sinks, small, wa, wx, name, jobs=()):
    S, Dm = x.shape
    T = MIX_TILE
    nT = S // T
    nb = T // BLOCK
    bpt = T // BLOCK

    def body(x_ref, nrm_ref, dxo_ref, h_ref, proj_ref, kvprev_ref, lxprev_ref, scprev_ref, sxprev_ref, ymix_ref,
             ymo_ref, hp_ref, w_in_ref, w_out_ref, sink_ref, small_ref, wa_ref, wx_ref,
             dx_ref, red_ref, dgate_ref, dwo_ref, dwi_ref, dsm_ref, dsink_ref, dwa_ref, dwx_ref,
             dy_s, dp_s, acc_o, acc_i, *carried_state):
        @pl.when(pl.program_id(0) == 0)
        def _():
            for r in (red_ref, dgate_ref, acc_o, acc_i):
                r[...] = jnp.zeros_like(r)

        dxo_v = dxo_ref[...]
        dyb = (nrm_ref[3:4, :] * dxo_v).astype(BF16)
        dgate_ref[0:1, :] += jnp.sum(ymo_ref[...].astype(F32) * dxo_v, axis=0, keepdims=True)
        dy_s[...] = _dot_nt(dyb, w_out_ref[...])
        acc_o[...] += _dot_tn(ymix_ref[...], dyb)
        core(proj_ref, kvprev_ref, lxprev_ref, scprev_ref, sxprev_ref, dy_s, hp_ref, sink_ref, small_ref,
             wa_ref, wx_ref, dp_s, dsm_ref, dsink_ref, dwa_ref, dwx_ref, *carried_state)
        dpb = dp_s[...]
        acc_i[...] += _dot_tn(h_ref[...], dpb)
        dx_ref[...] = dxo_v + _norm_bwd(_dot_nt(dpb, w_in_ref[...]), x_ref[...], nrm_ref, red_ref)

        @pl.when(pl.program_id(0) == nT - 1)
        def _():
            dwo_ref[...] = acc_o[...].astype(BF16)
            dwi_ref[...] = acc_i[...].astype(BF16)

    def core(proj_ref, kvprev_ref, lxprev_ref, scprev_ref, sxprev_ref, dy_ref, hp_ref, sink_ref, small_ref,
             wa_ref, wx_ref, dp_ref, dsm_ref, dsink_ref, dwa_ref, dwx_ref,
             dk_s, dv_s, dkv_c, dxc_n, dc3_n, p_c):
        i = pl.program_id(0)
        ti = nT - 1 - i
        has_prev = jnp.where(ti == 0, 0.0, 1.0)

        @pl.when(i == 0)
        def _():
            for r in (dkv_c, dxc_n, dc3_n, p_c, dsm_ref, dsink_ref, dwa_ref, dwx_ref):
                r[...] = jnp.zeros_like(r)

        row = lax.broadcasted_iota(jnp.int32, (T, LRU_WIDTH), 0)

        kv = proj_ref[:, C_KV:C_KV + 2 * KV_WIDTH]
        ext = jnp.concatenate([kvprev_ref[...] * has_prev, kv], axis=0)
        kx = _head_planes(ext[:, :KV_WIDTH])
        vx = _head_planes(ext[:, KV_WIDTH:])
        dk_s[...] = jnp.zeros_like(dk_s)
        dv_s[...] = jnp.zeros_like(dv_s)
        dk_s[:, T:] = dkv_c[:, :BLOCK]
        dv_s[:, T:] = dkv_c[:, BLOCK:]
        first_tile = jnp.where(ti == 0, 1, 0)
        for b in range(nb):
            units = [(pair, e) for pair in range(N_Q_HEADS // 2) for e in range(2)]
            geometry = _band_geometry(first_tile if b == 0 else 0)
            own = geometry[0]
            keys = slice(b * BLOCK, (b + 2) * BLOCK)
            tile = {pair: (slice(b * BLOCK, (b + 1) * BLOCK), slice(pair * LANES, (pair + 1) * LANES))
                    for pair in range(N_Q_HEADS // 2)}
            qp = {k: (proj_ref[rc] * 0.125).astype(BF16) for k, rc in tile.items()}
            dob = {k: dy_ref[rc].astype(BF16) for k, rc in tile.items()}
            qp_t = {k: jnp.transpose(proj_ref[rc] * 0.125).astype(BF16) for k, rc in tile.items()}
            dob_t = {k: jnp.transpose(dy_ref[rc]).astype(BF16) for k, rc in tile.items()}
            scores = [_fold(_dot_nt(qp[pair], kx[pair // 2][e][keys]), own) for pair, e in units]
            dprob = [_fold(_dot_nt(dob[pair], vx[pair // 2][e][keys]), own) for pair, e in units]
            pn_wide, ds_wide = [], []
            for s, dpm, (pair, e) in zip(scores, dprob, units):
                h = 2 * pair + e
                pn, psink = _softmax_band(s, h, geometry, sink_ref[h])
                dsum = jnp.sum(pn * dpm, axis=-1, keepdims=True)
                dsink_ref[h:h + 1, :] += jnp.full((1, LANES), -1.0, F32) * jnp.sum(psink * dsum)
                pn_wide.append(_unfold(pn, own).astype(BF16))
                ds_wide.append(_unfold(pn * (dpm - dsum), own).astype(BF16))
            dq = {}
            for pw, ds, (pair, e) in zip(pn_wide, ds_wide, units):
                g = pair // 2
                head_e = slice(e * HEAD_DIM, (e + 1) * HEAD_DIM)
                head_g = slice(g * HEAD_DIM, (g + 1) * HEAD_DIM)
                dv_s[head_g, keys] += _dot(dob_t[pair], pw)[head_e, :]
                dk_s[head_g, keys] += _dot(qp_t[pair], ds)[head_e, :]
                part = _dot(ds, kx[g][e][keys])
                dq[pair] = part if e == 0 else dq[pair] + part
            for k, rc in tile.items():
                dp_ref[rc] = (0.125 * dq[k]).astype(BF16)
        dp_ref[:, C_KV:C_KV + KV_WIDTH] = jnp.transpose(dk_s[:, BLOCK:]).astype(BF16)
        dp_ref[:, C_KV + KV_WIDTH:C_KV + 2 * KV_WIDTH] = jnp.transpose(dv_s[:, BLOCK:]).astype(BF16)
        dkv_c[:, :BLOCK] = dk_s[:, :BLOCK]
        dkv_c[:, BLOCK:] = dv_s[:, :BLOCK]

        lx = proj_ref[:, C_LX:C_LX + LRU_WIDTH]
        lxprev = lxprev_ref[...] * has_prev
        xc, xcb, r, ig, sp, a, mult, first = _lru_gates(lx, lxprev, small_ref, wa_ref, wx_ref, row, ti * T)
        hp = hp_ref[...]
        hh = a * hp + mult * (ig * xc)
        lg = proj_ref[:, C_LG:C_LG + LRU_WIDTH]
        gl, th = _gelu(lg)
        dyl = dy_ref[:, ATTN_WIDTH:ATTN_WIDTH + LRU_WIDTH]
        dp_ref[:, C_LG:C_LG + LRU_WIDTH] = (dyl * hh * _gelu_grad(lg, th)).astype(BF16)
        aa = jnp.where(row < T - 1, pltpu.roll(a, T - 1, 0), 1.0)
        bb = dyl * gl
        s = 1
        while s < T:
            a_sh = jnp.where(row < T - s, pltpu.roll(aa, T - s, 0), 1.0)
            b_sh = jnp.where(row < T - s, pltpu.roll(bb, T - s, 0), 0.0)
            bb = bb + aa * b_sh
            aa = aa * a_sh
            s *= 2
        G = bb + aa * p_c[0:1, :]
        p_c[...] = jnp.broadcast_to(_edge_row(a * G, False), p_c.shape)
        da = G * hp
        dmult = G * (ig * xc)
        dig = G * mult * xc
        dxc = G * mult * ig
        dla = da * a + dmult * jnp.where(first, 0.0, -(a * a) / mult)
        dr = dla * ((-LRU_C) * sp)
        lam = small_ref[7:8, :]
        dsm_ref[7:8, :] += jnp.sum(dla * ((-LRU_C) * r), axis=0, keepdims=True) * (-_sigmoid(-lam))
        dpa = dr * r * (1.0 - r)
        dpx = dig * ig * (1.0 - ig)
        dsm_ref[5:6, :] += jnp.sum(dpa, axis=0, keepdims=True)
        dsm_ref[6:7, :] += jnp.sum(dpx, axis=0, keepdims=True)
        dpab = dpa.astype(BF16)
        dpxb = dpx.astype(BF16)
        dwa_ref[...] += _dot_tn(xcb, dpab)
        dwx_ref[...] += _dot_tn(xcb, dpxb)
        dxc = dxc + _dot_nt(dpab, wa_ref[...]) + _dot_nt(dpxb, wx_ref[...])
        dsm_ref[4:5, :] += jnp.sum(dxc, axis=0, keepdims=True)
        dsm_ref[3:4, :] += jnp.sum(dxc * lx, axis=0, keepdims=True)
        for k in range(3):
            dsm_ref[k:k + 1, :] += jnp.sum(dxc * _past(lx, lxprev, 3 - k, row), axis=0, keepdims=True)
        nxt = dxc_n[...]
        dlx = (small_ref[3:4, :] * dxc + small_ref[2:3, :] * _future(dxc, nxt, 1, row)
               + small_ref[1:2, :] * _future(dxc, nxt, 2, row) + small_ref[0:1, :] * _future(dxc, nxt, 3, row))
        dxc_n[...] = dxc
        dp_ref[:, C_LX:C_LX + LRU_WIDTH] = dlx.astype(BF16)

        sc = proj_ref[:, C_SC:C_SC + CONV_WIDTH]
        sx = proj_ref[:, C_SX:C_SX + CONV_WIDTH]
        sb = proj_ref[:, C_SB:C_SB + CONV_WIDTH]
        z = sc * sx
        zprev = (scprev_ref[...] * sxprev_ref[...]) * has_prev
        z1 = _past(z, zprev, 1, row)
        z2 = _past(z, zprev, 2, row)
        c3 = small_ref[10:11, :] * z + small_ref[9:10, :] * z1 + small_ref[8:9, :] * z2
        dys = dy_ref[:, ATTN_WIDTH + LRU_WIDTH:]
        dp_ref[:, C_SB:C_SB + CONV_WIDTH] = (dys * c3).astype(BF16)
        dc3 = dys * sb
        dsm_ref[10:11, :] += jnp.sum(dc3 * z, axis=0, keepdims=True)
        dsm_ref[9:10, :] += jnp.sum(dc3 * z1, axis=0, keepdims=True)
        dsm_ref[8:9, :] += jnp.sum(dc3 * z2, axis=0, keepdims=True)
        nxt3 = dc3_n[...]
        dz = (small_ref[10:11, :] * dc3 + small_ref[9:10, :] * _future(dc3, nxt3, 1, row)
              + small_ref[8:9, :] * _future(dc3, nxt3, 2, row))
        dc3_n[...] = dc3
        dp_ref[:, C_SC:C_SC + CONV_WIDTH] = (dz * sx).astype(BF16)
        dp_ref[:, C_SX:C_SX + CONV_WIDTH] = (dz * sc).astype(BF16)

    fix = lambda i: (0, 0)
    cur = lambda i: (nT - 1 - i, 0)
    prev_cols = lambda cb: (lambda i: (jnp.maximum(nT - 2 - i, 0), cb))
    return _pcall(
        body, name=name, grid=(nT,),
        in_specs=[pl.BlockSpec((T, Dm), cur), pl.BlockSpec((8, Dm), fix), pl.BlockSpec((T, Dm), cur),
                  pl.BlockSpec((T, Dm), cur),
                  pl.BlockSpec((T, IN_PROJ_WIDTH), cur),
                  pl.BlockSpec((BLOCK, 2 * KV_WIDTH),
                               lambda i: (jnp.maximum((nT - 1 - i) * bpt - 1, 0), C_KV // (2 * KV_WIDTH))),
                  pl.BlockSpec((T, LRU_WIDTH), prev_cols(C_LX // LRU_WIDTH)),
                  pl.BlockSpec((T, CONV_WIDTH), prev_cols(C_SC // CONV_WIDTH)),
                  pl.BlockSpec((T, CONV_WIDTH), prev_cols(C_SX // CONV_WIDTH)),
                  pl.BlockSpec((T, D_MODEL), cur), pl.BlockSpec((T, Dm), cur),
                  pl.BlockSpec((T, LRU_WIDTH), cur),
                  _resident((Dm, IN_PROJ_WIDTH)), _resident((D_MODEL, Dm)),
                  pl.BlockSpec(memory_space=pltpu.SMEM),
                  pl.BlockSpec((16, LRU_WIDTH), fix),
                  pl.BlockSpec((LRU_WIDTH, LRU_WIDTH), fix),
                  pl.BlockSpec((LRU_WIDTH, LRU_WIDTH), fix)],
        out_specs=[pl.BlockSpec((T, Dm), cur), pl.BlockSpec((8, Dm), fix), pl.BlockSpec((8, Dm), fix),
                   _resident((D_MODEL, Dm)), _resident((Dm, IN_PROJ_WIDTH)),
                   pl.BlockSpec((16, LRU_WIDTH), fix),
                   pl.BlockSpec((SUBLANES, LANES), fix),
                   pl.BlockSpec((LRU_WIDTH, LRU_WIDTH), fix),
                   pl.BlockSpec((LRU_WIDTH, LRU_WIDTH), fix)],
        out_shape=[jax.ShapeDtypeStruct((S, Dm), F32), jax.ShapeDtypeStruct((8, Dm), F32),
                   jax.ShapeDtypeStruct((8, Dm), F32),
                   jax.ShapeDtypeStruct((D_MODEL, Dm), BF16), jax.ShapeDtypeStruct((Dm, IN_PROJ_WIDTH), BF16),
                   jax.ShapeDtypeStruct((16, LRU_WIDTH), F32),
                   jax.ShapeDtypeStruct((SUBLANES, LANES), F32),
                   jax.ShapeDtypeStruct((LRU_WIDTH, LRU_WIDTH), F32),
                   jax.ShapeDtypeStruct((LRU_WIDTH, LRU_WIDTH), F32)],
        scratch_shapes=[pltpu.VMEM((T, D_MODEL), F32), pltpu.VMEM((T, IN_PROJ_WIDTH), BF16),
                        pltpu.VMEM((D_MODEL, Dm), F32), pltpu.VMEM((Dm, IN_PROJ_WIDTH), F32),
                        pltpu.VMEM((KV_WIDTH, T + BLOCK), F32), pltpu.VMEM((KV_WIDTH, T + BLOCK), F32),
                        pltpu.VMEM((BLOCK, 2 * KV_WIDTH), F32), pltpu.VMEM((T, LRU_WIDTH), F32),
                        pltpu.VMEM((T, CONV_WIDTH), F32), pltpu.VMEM((SUBLANES, LRU_WIDTH), F32)],
        sem=("arbitrary",),
        args=(x, nrm, dxo, h, proj, proj, proj, proj, proj, ymix, ymo, hprev, w_in, w_out, sinks, small, wa, wx),
        jobs=jobs)


def _adamw_update(g, w_ref, m_ref, v_ref, go_ref, d_ref, mo_ref, vo_ref):
    mn = ADAM_B1 * m_ref[...] + (1.0 - ADAM_B1) * g
    vn = ADAM_B2 * v_ref[...] + (1.0 - ADAM_B2) * (g * g)
    go_ref[...] = g
    mo_ref[...] = mn
    vo_ref[...] = vn
    m_hat = mn / (1.0 - ADAM_B1 ** ADAM_STEP)
    v_hat = vn / (1.0 - ADAM_B2 ** ADAM_STEP)
    d_ref[...] = (-ADAM_LR) * (m_hat / (jnp.sqrt(v_hat) + ADAM_EPS) + ADAM_WD * w_ref[...])


def _adamw(w, g, m, v, name):
    R, C = w.shape
    tr = 8
    for cand in (512, 256, 128, 64, 32, 16, 8):
        if R % cand == 0 and cand * C * 4 <= (1 << 20):
            tr = cand
            break

    def body(w_ref, g_ref, *rest):
        _adamw_update(g_ref[...], w_ref, *rest)

    spec = pl.BlockSpec((tr, C), lambda i: (i, 0))
    return _pcall(body, name=name, grid=(R // tr,), in_specs=[spec] * 4, out_specs=[spec] * 4,
                  out_shape=[jax.ShapeDtypeStruct((R, C), F32)] * 4, sem=("arbitrary",), args=(w, g, m, v))


def _adamw_many(ws, gs, ms, vs, name):
    n = len(ws)

    def body(*refs):
        w_r, g_r, m_r, v_r, go, do, mo, vo = (refs[k * n:(k + 1) * n] for k in range(8))
        for t in range(n):
            _adamw_update(g_r[t][...], w_r[t], m_r[t], v_r[t], go[t], do[t], mo[t], vo[t])

    vmem = pl.BlockSpec(memory_space=pltpu.VMEM)
    res = pl.pallas_call(
        body, name=name, in_specs=[vmem] * (4 * n), out_specs=[vmem] * (4 * n),
        out_shape=[jax.ShapeDtypeStruct(w.shape, F32) for w in ws] * 4,
        compiler_params=pltpu.CompilerParams(vmem_limit_bytes=VMEM_LIMIT),
    )(*ws, *gs, *ms, *vs)
    return [res[k * n:(k + 1) * n] for k in range(4)]


def _adamw_partials(w, partials, m, v, name):
    nl = len(partials)
    _, R, C = partials[0][0].shape
    tr = 8
    for cand in (256, 128, 64, 32, 16):
        if R % cand == 0 and cand * C * 4 <= (1 << 19):
            tr = cand
            break
    ni = R // tr

    def body(*refs):
        w_ref, p_refs = refs[0], refs[1:1 + 2 * nl]
        m_ref, v_ref, go_ref, d_ref, mo_ref, vo_ref = refs[1 + 2 * nl:]
        for l in range(nl):
            @pl.when(pl.program_id(0) == l)
            def _(pair=p_refs[2 * l:2 * l + 2]):
                own, sib = [((p[0].astype(F32) + p[1].astype(F32)) + p[2].astype(F32)) + p[3].astype(F32)
                            for p in pair]
                _adamw_update(own + sib, w_ref, m_ref, v_ref, go_ref, d_ref, mo_ref, vo_ref)

    def slots(l):
        return pl.BlockSpec((N_CHIPS, tr, C),
                            lambda ll, i: (0, jnp.where(ll == l, i, jnp.where(ll < l, 0, ni - 1)), 0))

    spec = pl.BlockSpec((tr, C), lambda ll, i: (ll * ni + i, 0))
    return pl.pallas_call(
        body, name=name, grid=(nl, ni),
        in_specs=[spec] + [slots(l) for l in range(nl) for _ in range(2)] + [spec, spec], out_specs=[spec] * 4,
        out_shape=[jax.ShapeDtypeStruct((nl * R, C), F32)] * 4,
        compiler_params=_cp("arbitrary", "arbitrary"),
    )(w, *[p for pair in partials for p in pair], m, v)


GATHER_SEMS = 7


def _two_level_gather(x_ref, out_ref, send_sems, recv_sems, local_sem, base=0):
    M = x_ref.shape[0]
    x, y, c = lax.axis_index("x"), lax.axis_index("y"), lax.axis_index("c")
    me, sibling = (x, y, c), (x, y, 1 - c)
    chips = [(1 - x, y), (x, 1 - y), (1 - x, 1 - y)]

    def rows(px, py, pc):
        return out_ref.at[pl.ds(pl.multiple_of((4 * px + 2 * py + pc) * M, SUBLANES), M), :]

    def copy(k, block, to, src=None):
        return pltpu.make_async_remote_copy(
            src_ref=rows(*block) if src is None else src, dst_ref=rows(*block),
            send_sem=send_sems.at[base + k], recv_sem=recv_sems.at[base + k], device_id=to, device_id_type=MESH)

    mine = pltpu.make_async_copy(x_ref, rows(*me), local_sem)
    mine.start()
    first = [copy(0, me, sibling, src=x_ref)]
    first += [copy(1 + j, me, (*chip, c), src=x_ref) for j, chip in enumerate(chips)]
    for cp in first:
        cp.start()
    passed = [copy(4 + j, (*chip, c), sibling) for j, chip in enumerate(chips)]
    for j, chip in enumerate(chips):
        copy(1 + j, (*chip, c), me).wait_recv()
        passed[j].start()
    copy(0, sibling, me).wait_recv()
    for j, chip in enumerate(chips):
        copy(4 + j, (*chip, 1 - c), me).wait_recv()
    for cp in first + passed:
        cp.wait_send()
    mine.wait()


def _prologue(pack, w_mod, jobs, name):
    M = pack.shape[0]
    L, Dm, N = w_mod.shape
    nj = len(jobs)
    rows_c = Dm // LANES
    tn = 768

    def body(*refs):
        pack_ref, w_ref = refs[:2]
        jin, refs = refs[2:2 + nj], refs[2 + nj:]
        g_ref, ca_ref, mod_ref = refs[:3]
        jout, refs = refs[3:3 + nj], refs[3 + nj:]
        part_ref, send_sems, recv_sems, local_sem, *jsems = refs
        _run_jobs("start", jobs, jin, jout, jsems)
        _two_level_gather(pack_ref, g_ref, send_sems, recv_sems, local_sem.at[0], 0)
        ca_ref[...] = jnp.zeros_like(ca_ref)
        for r in range(rows_c):
            cv = g_ref[pl.ds(r, N_DEV, stride=M), :]
            ca_ref[0:N_DEV, r * LANES:(r + 1) * LANES] = (cv * _sigmoid(cv)).astype(BF16)
        ca = ca_ref[...]
        for l in range(L):
            for n0 in range(0, N, tn):
                part_ref[l * 16:(l + 1) * 16, n0:n0 + tn] = _dot(ca, w_ref[l, :, n0:n0 + tn].astype(BF16))
        _two_level_gather(part_ref, mod_ref, send_sems, recv_sems, local_sem.at[1], GATHER_SEMS)
        _run_jobs("relay", jobs, jin, jout, jsems)
        _run_jobs("finish", jobs, jin, jout, jsems)

    vmem = pl.BlockSpec(memory_space=pltpu.VMEM)
    hbm = pl.BlockSpec(memory_space=pltpu.HBM)
    res = pl.pallas_call(
        body, name=name,
        out_shape=[jax.ShapeDtypeStruct((N_DEV * M, LANES), F32), jax.ShapeDtypeStruct((16, Dm), BF16),
                   jax.ShapeDtypeStruct((N_DEV * L * 16, N), F32)] + [job.out_shape for job in jobs],
        in_specs=[vmem, vmem] + [hbm] * nj, out_specs=[vmem, vmem, vmem] + [hbm] * nj,
        scratch_shapes=[pltpu.VMEM((L * 16, N), F32), pltpu.SemaphoreType.DMA((2 * GATHER_SEMS,)),
                        pltpu.SemaphoreType.DMA((2 * GATHER_SEMS,)), pltpu.SemaphoreType.DMA((2,))]
        + _job_scratch(nj),
        compiler_params=pltpu.CompilerParams(vmem_limit_bytes=VMEM_LIMIT),
    )(pack, w_mod, *[job.src for job in jobs])
    return res[0], res[1], res[2], list(res[3:])


def _all_gather_small(v, name, jobs=()):
    M, N = v.shape
    nj = len(jobs)

    def body(*refs):
        x_ref, jin = refs[0], refs[1:1 + nj]
        out_ref, sum_ref = refs[1 + nj:3 + nj]
        jout, (send_sems, recv_sems, local_sem, *jsems) = refs[3 + nj:3 + 2 * nj], refs[3 + 2 * nj:]
        _run_jobs("start", jobs, jin, jout, jsems)
        _two_level_gather(x_ref, out_ref, send_sems, recv_sems, local_sem)
        acc = out_ref[0:M, :]
        for d in range(1, N_DEV):
            acc = acc + out_ref[d * M:(d + 1) * M, :]
        sum_ref[...] = acc
        _run_jobs("relay", jobs, jin, jout, jsems)
        _run_jobs("finish", jobs, jin, jout, jsems)

    vmem = pl.BlockSpec(memory_space=pltpu.VMEM)
    hbm = pl.BlockSpec(memory_space=pltpu.HBM)
    res = pl.pallas_call(
        body, name=name,
        out_shape=[jax.ShapeDtypeStruct((N_DEV * M, N), F32), jax.ShapeDtypeStruct((M, N), F32)]
        + [job.out_shape for job in jobs],
        in_specs=[vmem] + [hbm] * nj, out_specs=[vmem, vmem] + [hbm] * nj,
        scratch_shapes=[pltpu.SemaphoreType.DMA((GATHER_SEMS,)), pltpu.SemaphoreType.DMA((GATHER_SEMS,)),
                        pltpu.SemaphoreType.DMA] + (_job_scratch(nj) if nj else []),
        compiler_params=pltpu.CompilerParams(vmem_limit_bytes=VMEM_LIMIT),
    )(v, *[job.src for job in jobs])
    return list(res[:2]), list(res[2:])


def _dw_mod(c_act, dmod, name):
    L, R, N = dmod.shape
    Dm = c_act.shape[1]
    tn = 768

    def body(c_ref, d_ref, o_ref):
        o_ref[0] = _dot_tn(c_ref[...], d_ref[0])

    return pl.pallas_call(
        body, name=name, grid=(L, N // tn),
        in_specs=[pl.BlockSpec((R, Dm), lambda l, n: (0, 0)), pl.BlockSpec((1, R, tn), lambda l, n: (l, 0, n))],
        out_specs=pl.BlockSpec((1, Dm, tn), lambda l, n: (l, 0, n)),
        out_shape=jax.ShapeDtypeStruct((L, Dm, N), F32),
        compiler_params=_cp("arbitrary", "arbitrary"),
    )(c_act, dmod)


_BIG = (("w_ffn1_gu", 1), ("w_ffn1_down", 0), ("w_ffn2_gu", 1), ("w_ffn2_down", 0), ("w_in", 1), ("w_out", 0))
_AXIS = dict(_BIG)

_GATHER_PLAN = {
    "first": [(0, "w_ffn1_gu"), (0, "w_ffn1_down")],
    (0, "ffn1"): [(0, "w_in"), (0, "w_out"), (0, "w_ffn2_gu")],
    (0, "mix"): [(0, "w_ffn2_down")],
    (0, "ffn2"): [(1, "w_ffn1_gu"), (1, "w_ffn1_down")],
    (1, "ffn1"): [(1, "w_in"), (1, "w_out"), (1, "w_ffn2_gu")],
    (1, "mix"): [(1, "w_ffn2_down")],
}


def _pack(arrs, rows_multiple=SUBLANES):
    flat = jnp.concatenate([a.astype(F32).reshape(-1) for a in arrs])
    unit = rows_multiple * LANES
    total = -(-flat.shape[0] // unit) * unit
    return jnp.pad(flat, (0, total - flat.shape[0])).reshape(total // LANES, LANES)


def _unpack(flat, shapes):
    out, off = [], 0
    for shp in shapes:
        n = int(math.prod(shp))
        out.append(flat[off:off + n].reshape(shp))
        off += n
    return out


def _block_diag(w):
    out = jnp.zeros((LRU_WIDTH, LRU_WIDTH), F32)
    for h in range(4):
        out = lax.dynamic_update_slice(out, w[h], (h * HEAD_DIM, h * HEAD_DIM))
    return out


def _diag_blocks(w):
    return jnp.stack([w[h * HEAD_DIM:(h + 1) * HEAD_DIM, h * HEAD_DIM:(h + 1) * HEAD_DIM] for h in range(4)])


def _rows8(*rows):
    z = jnp.zeros((8 - len(rows), rows[0].shape[-1]), F32)
    return jnp.concatenate([jnp.stack(rows), z], axis=0)


def kernel(x, c, w_mod, b_mod, g_norm, w_ffn1_gu, w_ffn1_down, w_ffn2_gu, w_ffn2_down, w_in, w_out, attn_sinks, lru_conv_w, lru_conv_b, lru_gate_a_w, lru_gate_a_b, lru_gate_x_w, lru_gate_x_b, lru_lambda, sc_conv_w, g_final, loss_target, m_w_mod, m_b_mod, m_g_norm, m_w_ffn1_gu, m_w_ffn1_down, m_w_ffn2_gu, m_w_ffn2_down, m_w_in, m_w_out, m_attn_sinks, m_lru_conv_w, m_lru_conv_b, m_lru_gate_a_w, m_lru_gate_a_b, m_lru_gate_x_w, m_lru_gate_x_b, m_lru_lambda, m_sc_conv_w, m_g_final, v_w_mod, v_b_mod, v_g_norm, v_w_ffn1_gu, v_w_ffn1_down, v_w_ffn2_gu, v_w_ffn2_down, v_w_in, v_w_out, v_attn_sinks, v_lru_conv_w, v_lru_conv_b, v_lru_gate_a_w, v_lru_gate_a_b, v_lru_gate_x_w, v_lru_gate_x_b, v_lru_lambda, v_sc_conv_w, v_g_final):
    W = dict(w_mod=w_mod, b_mod=b_mod, g_norm=g_norm, w_ffn1_gu=w_ffn1_gu, w_ffn1_down=w_ffn1_down,
             w_ffn2_gu=w_ffn2_gu, w_ffn2_down=w_ffn2_down, w_in=w_in, w_out=w_out, attn_sinks=attn_sinks,
             lru_conv_w=lru_conv_w, lru_conv_b=lru_conv_b, lru_gate_a_w=lru_gate_a_w, lru_gate_a_b=lru_gate_a_b,
             lru_gate_x_w=lru_gate_x_w, lru_gate_x_b=lru_gate_x_b, lru_lambda=lru_lambda, sc_conv_w=sc_conv_w,
             g_final=g_final)
    M1 = dict(w_mod=m_w_mod, b_mod=m_b_mod, g_norm=m_g_norm, w_ffn1_gu=m_w_ffn1_gu, w_ffn1_down=m_w_ffn1_down,
              w_ffn2_gu=m_w_ffn2_gu, w_ffn2_down=m_w_ffn2_down, w_in=m_w_in, w_out=m_w_out,
              attn_sinks=m_attn_sinks, lru_conv_w=m_lru_conv_w, lru_conv_b=m_lru_conv_b,
              lru_gate_a_w=m_lru_gate_a_w, lru_gate_a_b=m_lru_gate_a_b, lru_gate_x_w=m_lru_gate_x_w,
              lru_gate_x_b=m_lru_gate_x_b, lru_lambda=m_lru_lambda, sc_conv_w=m_sc_conv_w, g_final=m_g_final)
    V1 = dict(w_mod=v_w_mod, b_mod=v_b_mod, g_norm=v_g_norm, w_ffn1_gu=v_w_ffn1_gu, w_ffn1_down=v_w_ffn1_down,
              w_ffn2_gu=v_w_ffn2_gu, w_ffn2_down=v_w_ffn2_down, w_in=v_w_in, w_out=v_w_out,
              attn_sinks=v_attn_sinks, lru_conv_w=v_lru_conv_w, lru_conv_b=v_lru_conv_b,
              lru_gate_a_w=v_lru_gate_a_w, lru_gate_a_b=v_lru_gate_a_b, lru_gate_x_w=v_lru_gate_x_w,
              lru_gate_x_b=v_lru_gate_x_b, lru_lambda=v_lru_lambda, sc_conv_w=v_sc_conv_w, g_final=v_g_final)
    names = ["w_mod", "b_mod", "g_norm", "w_ffn1_gu", "w_ffn1_down", "w_ffn2_gu", "w_ffn2_down", "w_in", "w_out",
             "attn_sinks", "lru_conv_w", "lru_conv_b", "lru_gate_a_w", "lru_gate_a_b", "lru_gate_x_w",
             "lru_gate_x_b", "lru_lambda", "sc_conv_w", "g_final"]

    xs = x[0]
    tgt = loss_target[0]
    S = xs.shape[0]
    chip = 2 * lax.axis_index("x") + lax.axis_index("y")
    batch = 2 * chip + lax.axis_index("c")
    L = DEPTH

    full = {}

    def gather_jobs(key):
        return [_GatherJob(W[n][l].astype(BF16), _AXIS[n]) for l, n in _GATHER_PLAN.get(key, ())]

    def landed(key, outs):
        full.update(zip(_GATHER_PLAN.get(key, ()), outs))

    fwd_shapes = [(D_MODEL,), g_norm.shape, lru_conv_w.shape, sc_conv_w.shape]
    gathered, c_act, mod_all, ex = _prologue(_pack([c[0], g_norm, lru_conv_w, sc_conv_w]), w_mod,
                                             gather_jobs("first"), "prologue")
    landed("first", ex)
    gathered = gathered.reshape(N_DEV, -1)
    per_chip = [_unpack(gathered[2 * jj], fwd_shapes) for jj in range(N_CHIPS)]
    g_norm_full = jnp.concatenate([p[1] for p in per_chip], axis=-1)
    lru_conv_w_full = jnp.concatenate([p[2] for p in per_chip], axis=-1)
    sc_conv_w_full = jnp.concatenate([p[3] for p in per_chip], axis=-1)
    mod_all = mod_all.reshape(N_DEV, L, 16, -1)
    mod_rows = [lax.dynamic_index_in_dim(mod_all[2 * jj], batch, axis=1, keepdims=False) for jj in range(N_CHIPS)]
    mod = (jnp.concatenate(mod_rows, axis=-1) + b_mod).reshape(L, 9, D_MODEL)

    nrm_all = jnp.concatenate([g_norm_full[:, :, None, :], mod.reshape(L, 3, 3, D_MODEL),
                               jnp.zeros((L, 3, 4, D_MODEL), F32)], axis=2)

    def nrm_rows(l, s):
        return nrm_all[l, s]

    def mixer_params(l):
        small = jnp.concatenate([lru_conv_w_full[l], lru_conv_b[l][None], lru_gate_a_b[l][None],
                                 lru_gate_x_b[l][None], lru_lambda[l][None], sc_conv_w_full[l],
                                 jnp.zeros((5, LRU_WIDTH), F32)], axis=0)
        return (attn_sinks[l], small, _block_diag(lru_gate_a_w[l]).astype(BF16),
                _block_diag(lru_gate_x_w[l]).astype(BF16))

    saved = []
    xcur = xs
    for l in range(L):
        n1, n2, n3 = nrm_rows(l, 0), nrm_rows(l, 1), nrm_rows(l, 2)

        def ffn(which, xin, nrm, head=None):
            key = (l, which)
            (xo, h, gu, y, *stats), ex = _ffn_fwd(xin, nrm, full[(l, f"w_{which}_gu")], full[(l, f"w_{which}_down")],
                                                  f"l{l}_{which}", gather_jobs(key), head)
            landed(key, ex)
            return (xo, *stats), (xin, h, gu, y)

        (x1,), s1 = ffn("ffn1", xcur, n1)
        mp = mixer_params(l)
        (x2, h2, proj, ymix, ymo, hprev), ex = _mixer_fwd(x1, n2, full[(l, "w_in")], full[(l, "w_out")], *mp,
                                                          f"l{l}_mix", gather_jobs((l, "mix")))
        landed((l, "mix"), ex)
        s2 = (x1, h2, proj, ymix, ymo, hprev, mp)
        (xcur, *stats), s3 = ffn("ffn2", x2, n3, (_rows8(g_final), tgt) if l == L - 1 else None)
        saved.append((n1, n2, n3, s1, s2, s3))

    dx, stats = xcur, stats[0]
    loss_here, d_g_final = stats[1, 0:1], stats[0]

    recv, theirs = {}, {}
    waiting = []

    def carried(fn, *a, extra=()):
        items = waiting + list(extra)
        waiting.clear()
        outs, landed_now = fn(*a, jobs=[_SiblingJob(recv[(ll, n)]) if g is None else _ScatterJob(g, _AXIS[n])
                                        for ll, n, g in items])
        for (ll, n, g), arr in zip(items, landed_now):
            if g is None:
                theirs[(ll, n)] = arr
            else:
                recv[(ll, n)] = arr
                waiting.append((ll, n, None))
        return outs

    dmod, d_gnorm, d_small = [None] * L, [None] * L, [None] * L
    for l in reversed(range(L)):
        n1, n2, n3, s1, s2, s3 = saved[l]

        def plain(fn, *a):
            return fn(*a)[0]

        def ffn_bwd(which, dxo, sv, nrm, last):
            xin, h, gu, y = sv
            tag = f"l{l}_{which}"
            dgu, dgate, dw_down = carried(
                _ffn_down_bwd, dxo, gu, y, full[(l, f"w_{which}_down")], nrm, tag + "_down_bwd")
            dw_gu = carried(_atb, h, dgu, BF16, 1024, 2816, tag + "_dw_gu", extra=[(l, f"w_{which}_down", dw_down)])
            mine = [(l, f"w_{which}_gu", dw_gu)]
            dxi, red = (carried if last else plain)(
                _nt_norm_bwd, dgu, full[(l, f"w_{which}_gu")], xin, nrm, dxo, tag + "_gu_bwd",
                **(dict(extra=mine) if last else {}))
            if not last:
                waiting.extend(mine)
            return dxi, (red[0], red[1], dgate[0]), red[2]

        dx, dm3, dg3 = ffn_bwd("ffn2", dx, s3, n3, False)
        x_in, h2, proj, ymix, ymo, hprev, mp = s2
        dx, red, dgate, dw_out, dw_in, dsm, dsink, dwa, dwx = carried(
            _mixer_bwd, x_in, n2, dx, h2, proj, ymix, ymo, hprev, full[(l, "w_in")], full[(l, "w_out")], *mp,
            f"l{l}_mix_bwd")
        waiting.extend([(l, "w_out", dw_out), (l, "w_in", dw_in)])
        dm2, dg2 = (red[0], red[1], dgate[0]), red[2]
        dx, dm1, dg1 = ffn_bwd("ffn1", dx, s1, n1, l == 0)
        dmod[l] = jnp.stack(list(dm1) + list(dm2) + list(dm3))
        d_gnorm[l] = jnp.stack([dg1, dg2, dg3])
        d_small[l] = (dsink[:, 0], dsm[0:4], dsm[4], _diag_blocks(dwa), dsm[5], _diag_blocks(dwx), dsm[6],
                      dsm[7], dsm[8:11])
    grad_x = dx[None]

    def both(k):
        return jnp.stack([d_small[0][k], d_small[1][k]])
    small_names = ["g_norm", "attn_sinks", "lru_conv_w", "lru_conv_b", "lru_gate_a_w", "lru_gate_a_b",
                   "lru_gate_x_w", "lru_gate_x_b", "lru_lambda", "sc_conv_w", "g_final"]
    small_parts = [jnp.stack(d_gnorm)] + [both(k) for k in range(9)] + [d_g_final]
    dmod_flat = jnp.stack(dmod).reshape(-1)
    bwd_gathered, bwd_sum = carried(_all_gather_small, _pack([dmod_flat] + small_parts + [loss_here]),
                                    "gather_small_bwd")
    n_mod = dmod_flat.shape[0]
    dmod_all = bwd_gathered.reshape(N_DEV, -1)[:, :n_mod].reshape(N_DEV, L, 9 * D_MODEL)
    bwd_sum = bwd_sum.reshape(-1)
    G = {"b_mod": bwd_sum[:n_mod].reshape(L, 9 * D_MODEL)}
    *small_sums, loss = _unpack(bwd_sum[n_mod:], [p.shape for p in small_parts] + [(1,)])
    loss = loss[0]
    G.update(zip(small_names, small_sums))
    for n in ("g_norm", "lru_conv_w", "sc_conv_w"):
        wdt = W[n].shape[-1]
        G[n] = lax.dynamic_slice_in_dim(G[n], chip * wdt, wdt, axis=G[n].ndim - 1)

    ncol = w_mod.shape[-1]
    dmod_cols = lax.dynamic_slice_in_dim(dmod_all, chip * ncol, ncol, axis=2)
    dmod_rows = jnp.pad(jnp.swapaxes(dmod_cols, 0, 1), ((0, 0), (0, 16 - N_DEV), (0, 0))).astype(BF16)
    g_w_mod = _dw_mod(c_act, dmod_rows, "dw_mod")

    out_g, out_d, out_m, out_v = {}, {}, {}, {}
    res, _ = _adamw(w_mod.reshape(-1, ncol), g_w_mod.reshape(-1, ncol), m_w_mod.reshape(-1, ncol),
                    v_w_mod.reshape(-1, ncol), "adamw_w_mod")
    out_g["w_mod"], out_d["w_mod"], out_m["w_mod"], out_v["w_mod"] = [r.reshape(w_mod.shape) for r in res]
    for n, _ in _BIG:
        shp = W[n].shape
        flat = (shp[0] * shp[1], shp[2])
        res = _adamw_partials(W[n].reshape(flat), [(recv[(l, n)], theirs[(l, n)]) for l in range(L)],
                              M1[n].reshape(flat), V1[n].reshape(flat), f"adamw_{n}")
        out_g[n], out_d[n], out_m[n], out_v[n] = [r.reshape(shp) for r in res]
    rest = ["b_mod"] + small_names

    def rows(a):
        return a.reshape(-1, a.shape[-1])
    res = _adamw_many([rows(W[n]) for n in rest], [rows(G[n]) for n in rest], [rows(M1[n]) for n in rest],
                      [rows(V1[n]) for n in rest], "adamw_small")
    for dst, group in zip((out_g, out_d, out_m, out_v), res):
        dst.update({n: r.reshape(W[n].shape) for n, r in zip(rest, group)})

    return (loss, grad_x, *[out_g[n] for n in names], *[out_d[n] for n in names],
            *[out_m[n] for n in names], *[out_v[n] for n in names])
```

```python
import math

import jax
import jax.numpy as jnp
from jax import lax
from jax.experimental import pallas as pl
from jax.experimental.pallas import tpu as pltpu

F32 = jnp.float32
BF16 = jnp.bfloat16

D_MODEL = 1024
DEPTH = 2
HEAD_DIM = 64
N_Q_HEADS = 8
ATTN_WIDTH = 512
KV_WIDTH = 128
LRU_WIDTH = 256
CONV_WIDTH = 256
IN_PROJ_WIDTH = 2048
BLOCK = 128
D_FF = 2816
EPS = 1e-6
NEG_INF = -1e30
LRU_C = 8.0
N_CHIPS = 4
N_DEV = 8

C_Q, C_KV, C_LX, C_LG, C_SB, C_SC, C_SX = 0, 512, 768, 1024, 1280, 1536, 1792

ADAM_LR = 0.001
ADAM_B1 = 0.9
ADAM_B2 = 0.999
ADAM_EPS = 1e-08
ADAM_WD = 0.01
ADAM_STEP = 10

LANES = 128
SUBLANES = 8
VMEM_LIMIT = 56 * 1024 * 1024
MIX_TILE = 256

MESH = pl.DeviceIdType.MESH


def _cp(*sem):
    return pltpu.CompilerParams(dimension_semantics=sem, vmem_limit_bytes=VMEM_LIMIT)


def _tile(n, pref):
    t = min(n, pref)
    while n % t:
        t //= 2
    return t


MXU_DIM = 256


def _resident(shape):
    return pl.BlockSpec(shape, lambda *_: (0, 0), pipeline_mode=pl.Buffered(1))


def _sigmoid(v):
    return 1.0 / (1.0 + jnp.exp(-v))


def _expm1(v):
    series = v * (1.0 + v * (0.5 + v * (1.0 / 6.0)))
    return jnp.where(v > -0.01, series, jnp.exp(v) - 1.0)


def _softplus_neg(lam):
    e = jnp.exp(-jnp.abs(lam))
    log1p = jnp.where(e < 1e-2, e * (1.0 - e * (0.5 - e * (1.0 / 3.0))), jnp.log(1.0 + e))
    return jnp.maximum(-lam, 0.0) + log1p


_GELU_K = math.sqrt(2.0 / math.pi)
_GELU_C = 0.044715


def _gelu(v):
    t = jnp.tanh(_GELU_K * (v + _GELU_C * v * v * v))
    return 0.5 * v * (1.0 + t), t


def _gelu_grad(v, t):
    return 0.5 * (1.0 + t) + 0.5 * v * (1.0 - t * t) * _GELU_K * (1.0 + 3.0 * _GELU_C * v * v)


def _dot(a, b):
    return jnp.dot(a, b, preferred_element_type=F32)


def _dot_nt(a, b):
    return lax.dot_general(a, b, (((1,), (1,)), ((), ())), preferred_element_type=F32)


def _dot_tn(a, b):
    return lax.dot_general(a, b, (((0,), (0,)), ((), ())), preferred_element_type=F32)


def _window(ref, axis, j, width):
    start = pl.multiple_of(j * width, LANES if axis == 1 else 16)
    if axis == 1:
        return ref.at[:, pl.ds(start, width)]
    return ref.at[pl.ds(start, width), :]


def _chip_peers():
    x, y, c = lax.axis_index("x"), lax.axis_index("y"), lax.axis_index("c")
    return x, y, c, [(1 - x, y), (x, 1 - y), (1 - x, 1 - y)]


class _GatherJob:
    def __init__(self, shard, axis):
        self.src, self.axis, self.width, self.half = shard, axis, shard.shape[axis], shard.shape[0] // 2
        full = tuple(d * N_CHIPS if k == axis else d for k, d in enumerate(shard.shape))
        self.out_shape = jax.ShapeDtypeStruct(full, shard.dtype)

    def _piece(self, ref, j, hf):
        if self.axis == 1:
            return ref.at[pl.ds(pl.multiple_of(hf * self.half, 16), self.half),
                          pl.ds(pl.multiple_of(j * self.width, LANES), self.width)]
        return ref.at[pl.ds(pl.multiple_of(j * self.width + hf * self.half, 16), self.half), :]

    def _copies(self, src, dst, send, recv, loc, t):
        x, y, c, chips = _chip_peers()
        j = 2 * x + y
        owners = [2 * px + py for px, py in chips]
        local = pltpu.make_async_copy(src, _window(dst, self.axis, j, self.width), loc.at[t])
        mine = src.at[pl.ds(pl.multiple_of(c * self.half, 16), self.half), :]

        def ici(k, owner):
            return pltpu.make_async_remote_copy(
                src_ref=mine, dst_ref=self._piece(dst, owner, c), send_sem=send.at[JOB_SEMS * t + k],
                recv_sem=recv.at[JOB_SEMS * t + k], device_id=(*chips[k], c), device_id_type=MESH)

        def relay(k, hf):
            piece = self._piece(dst, owners[k], hf)
            return pltpu.make_async_remote_copy(
                src_ref=piece, dst_ref=piece, send_sem=send.at[JOB_SEMS * t + 4 + k],
                recv_sem=recv.at[JOB_SEMS * t + 4 + k],
                device_id=(x, y, 1 - c), device_id_type=MESH)

        return (local, [ici(k, j) for k in range(3)], [ici(k, owners[k]) for k in range(3)],
                [relay(k, c) for k in range(3)], [relay(k, 1 - c) for k in range(3)])

    def start(self, *a):
        local, ici_out, _, _, _ = self._copies(*a)
        local.start()
        for cp in ici_out:
            cp.start()

    def relay(self, *a):
        _, _, ici_in, relay_out, _ = self._copies(*a)
        for arrived, onward in zip(ici_in, relay_out):
            arrived.wait_recv()
            onward.start()

    def finish(self, *a):
        local, ici_out, _, relay_out, relay_in = self._copies(*a)
        for cp in relay_in:
            cp.wait_recv()
        for cp in ici_out + relay_out:
            cp.wait_send()
        local.wait()


class _ScatterJob:
    def __init__(self, full, axis):
        self.src, self.axis, self.width = full, axis, full.shape[axis] // N_CHIPS
        shard = tuple(self.width if k == axis else d for k, d in enumerate(full.shape))
        self.out_shape = jax.ShapeDtypeStruct((N_CHIPS,) + shard, full.dtype)

    def _copies(self, src, dst, send, recv, loc, t):
        x, y, c, chips = _chip_peers()
        local = pltpu.make_async_copy(_window(src, self.axis, 2 * x + y, self.width), dst.at[3], loc.at[t])
        sends = [pltpu.make_async_remote_copy(
            src_ref=_window(src, self.axis, 2 * px + py, self.width), dst_ref=dst.at[k],
            send_sem=send.at[JOB_SEMS * t + k], recv_sem=recv.at[JOB_SEMS * t + k], device_id=(px, py, c),
            device_id_type=MESH) for k, (px, py) in enumerate(chips)]
        return local, sends

    def start(self, *a):
        local, sends = self._copies(*a)
        local.start()
        for cp in sends:
            cp.start()

    def relay(self, *a):
        pass

    def finish(self, *a):
        local, sends = self._copies(*a)
        for cp in sends:
            cp.wait_recv()
        for cp in sends:
            cp.wait_send()
        local.wait()


class _SiblingJob:
    def __init__(self, arr):
        self.src, self.out_shape = arr, jax.ShapeDtypeStruct(arr.shape, arr.dtype)

    def _copy(self, src, dst, send, recv, loc, t):
        x, y, c = lax.axis_index("x"), lax.axis_index("y"), lax.axis_index("c")
        return pltpu.make_async_remote_copy(
            src_ref=src, dst_ref=dst, send_sem=send.at[JOB_SEMS * t], recv_sem=recv.at[JOB_SEMS * t],
            device_id=(x, y, 1 - c), device_id_type=MESH)

    def start(self, *a):
        self._copy(*a).start()

    def relay(self, *a):
        pass

    def finish(self, *a):
        self._copy(*a).wait()


JOB_SEMS = 8


def _run_jobs(phase, jobs, srcs, dsts, sems):
    for t, job in enumerate(jobs):
        getattr(job, phase)(srcs[t], dsts[t], *sems, t)


def _job_scratch(n):
    return [pltpu.SemaphoreType.DMA((JOB_SEMS * n,)), pltpu.SemaphoreType.DMA((JOB_SEMS * n,)),
            pltpu.SemaphoreType.DMA((n,))]


def _pcall(body, *, name, grid, in_specs, out_specs, out_shape, sem, args, scratch_shapes=(), jobs=()):
    in_specs, out_specs, out_shape = list(in_specs), list(out_specs), list(out_shape)
    scratch_shapes = list(scratch_shapes)
    if not jobs:
        res = pl.pallas_call(body, name=name, grid=grid, in_specs=in_specs, out_specs=out_specs, out_shape=out_shape,
                             scratch_shapes=scratch_shapes, compiler_params=_cp(*sem))(*args)
        return list(res), []
    n_in, n_out, n_scr, nj = len(args), len(out_shape), len(scratch_shapes), len(jobs)
    n_steps = math.prod(grid)
    relay_step = (3 * n_steps) // 4
    relay_early = 0 < relay_step < n_steps - 1

    def wrapped(*refs):
        ins, refs = refs[:n_in], refs[n_in:]
        jin, refs = refs[:nj], refs[nj:]
        outs, refs = refs[:n_out], refs[n_out:]
        jout, refs = refs[:nj], refs[nj:]
        scr, sems = refs[:n_scr], refs[n_scr:]
        step = pl.program_id(0)
        for d in range(1, len(grid)):
            step = step * grid[d] + pl.program_id(d)

        @pl.when(step == 0)
        def _():
            _run_jobs("start", jobs, jin, jout, sems)

        if relay_early:
            @pl.when(step == relay_step)
            def _():
                _run_jobs("relay", jobs, jin, jout, sems)
        body(*ins, *outs, *scr)

        @pl.when(step == n_steps - 1)
        def _():
            if not relay_early:
                _run_jobs("relay", jobs, jin, jout, sems)
            _run_jobs("finish", jobs, jin, jout, sems)

    hbm = pl.BlockSpec(memory_space=pltpu.HBM)
    res = pl.pallas_call(
        wrapped, name=name, grid=grid, in_specs=in_specs + [hbm] * nj, out_specs=out_specs + [hbm] * nj,
        out_shape=out_shape + [job.out_shape for job in jobs], scratch_shapes=scratch_shapes + _job_scratch(nj),
        compiler_params=_cp(*sem))(*args, *[job.src for job in jobs])
    return list(res[:n_out]), list(res[n_out:])


def _hidden_chunks(k):
    return [(c0, min(6 * MXU_DIM, k - c0)) for c0 in range(0, k, 6 * MXU_DIM)]


def _loss_head(xv, gain, tgt, st_ref):
    dm = xv.shape[-1]
    rstd = lax.rsqrt(jnp.mean(xv * xv, axis=-1, keepdims=True) + EPS)
    xn = xv * rstd
    err = xn * gain - tgt
    st_ref[1:2, :] += jnp.full((1, dm), 0.5 / dm, F32) * jnp.sum(err * err)
    dy = err * (1.0 / dm)
    st_ref[0:1, :] += jnp.sum(dy * xn, axis=0, keepdims=True)
    dxn = dy * gain
    return rstd * (dxn - xn * jnp.mean(dxn * xn, axis=-1, keepdims=True))


def _ffn_fwd(x, nrm, w_gu, w_down, name, jobs=(), head=None):
    S, Dm = x.shape
    K = w_down.shape[0]
    tm = _tile(S, 256)

    def body(x_ref, nrm_ref, wgu_ref, wdn_ref, *rest):
        if head is None:
            o_ref, h_ref, gu_ref, y_ref = rest
        else:
            gf_ref, t_ref, o_ref, h_ref, gu_ref, y_ref, st_ref = rest

            @pl.when(pl.program_id(0) == 0)
            def _():
                st_ref[...] = jnp.zeros_like(st_ref)
        xv = x_ref[...]
        rstd = lax.rsqrt(jnp.mean(xv * xv, axis=-1, keepdims=True) + EPS)
        hn = (xv * rstd) * nrm_ref[0:1, :]
        hb = (hn * (1.0 + nrm_ref[2:3, :]) + nrm_ref[1:2, :]).astype(BF16)
        h_ref[...] = hb
        y = jnp.zeros((tm, Dm), F32)
        for c0, cs in ((0, K),):
            g = _dot(hb, wgu_ref[:, c0:c0 + cs])
            u = _dot(hb, wgu_ref[:, K + c0:K + c0 + cs])
            gu_ref[:, c0:c0 + cs] = g.astype(BF16)
            gu_ref[:, K + c0:K + c0 + cs] = u.astype(BF16)
            y = y + _dot((g * _sigmoid(g) * u).astype(BF16), wdn_ref[c0:c0 + cs, :])
        xo = xv + (0.5 * nrm_ref[3:4, :]) * y
        o_ref[...] = xo if head is None else _loss_head(xo, gf_ref[0:1, :], t_ref[...], st_ref)
        y_ref[...] = y.astype(BF16)

    row = lambda i: (i, 0)
    fix = lambda i: (0, 0)
    in_specs = [pl.BlockSpec((tm, Dm), row), pl.BlockSpec((8, Dm), fix), _resident((Dm, 2 * K)), _resident((K, Dm))]
    out_specs = [pl.BlockSpec((tm, Dm), row), pl.BlockSpec((tm, Dm), row), pl.BlockSpec((tm, 2 * K), row),
                 pl.BlockSpec((tm, Dm), row)]
    out_shape = [jax.ShapeDtypeStruct((S, Dm), F32), jax.ShapeDtypeStruct((S, Dm), BF16),
                 jax.ShapeDtypeStruct((S, 2 * K), BF16), jax.ShapeDtypeStruct((S, Dm), BF16)]
    args = (x, nrm, w_gu, w_down)
    if head is not None:
        in_specs += [pl.BlockSpec((8, Dm), fix), pl.BlockSpec((tm, Dm), row)]
        out_specs.append(pl.BlockSpec((8, Dm), fix))
        out_shape.append(jax.ShapeDtypeStruct((8, Dm), F32))
        args += tuple(head)
    return _pcall(body, name=name, grid=(S // tm,), in_specs=in_specs, out_specs=out_specs, out_shape=out_shape,
                  sem=("arbitrary",), args=args, jobs=jobs)


def _ffn_down_bwd(dxo, gu, y, w, nrm, name, jobs=()):
    S, Dm = dxo.shape
    K = w.shape[0]
    Ka = gu.shape[1]
    coef = 0.5
    tm = _tile(S, 256)
    n_steps = S // tm
    chunks = _hidden_chunks(K)

    def body(dxo_ref, y_ref, w_ref, nrm_ref, a_ref, da_ref, dgate_ref, dw_ref, acc):
        @pl.when(pl.program_id(0) == 0)
        def _():
            dgate_ref[...] = jnp.zeros_like(dgate_ref)
            acc[...] = jnp.zeros_like(acc)

        dxo_v = dxo_ref[...]
        dyb = ((coef * nrm_ref[3:4, :]) * dxo_v).astype(BF16)
        dgate_ref[0:1, :] += jnp.sum(coef * y_ref[...].astype(F32) * dxo_v, axis=0, keepdims=True)
        for c0, cs in chunks:
            dact = _dot_nt(dyb, w_ref[c0:c0 + cs, :])
            g = a_ref[:, c0:c0 + cs].astype(F32)
            u = a_ref[:, K + c0:K + c0 + cs].astype(F32)
            s = _sigmoid(g)
            si = g * s
            da_ref[:, c0:c0 + cs] = (dact * u * (s * (1.0 + g * (1.0 - s)))).astype(BF16)
            da_ref[:, K + c0:K + c0 + cs] = (dact * si).astype(BF16)
            acc[c0:c0 + cs, :] += _dot_tn((si * u).astype(BF16), dyb)

        @pl.when(pl.program_id(0) == n_steps - 1)
        def _():
            dw_ref[...] = acc[...].astype(BF16)

    row = lambda i: (i, 0)
    fix = lambda i: (0, 0)
    return _pcall(
        body, name=name, grid=(n_steps,),
        in_specs=[pl.BlockSpec((tm, Dm), row), pl.BlockSpec((tm, Dm), row), _resident((K, Dm)),
                  pl.BlockSpec((8, Dm), fix), pl.BlockSpec((tm, Ka), row)],
        out_specs=[pl.BlockSpec((tm, Ka), row), pl.BlockSpec((8, Dm), fix), _resident((K, Dm))],
        out_shape=[jax.ShapeDtypeStruct((S, Ka), BF16), jax.ShapeDtypeStruct((8, Dm), F32),
                   jax.ShapeDtypeStruct((K, Dm), BF16)],
        scratch_shapes=[pltpu.VMEM((K, Dm), F32)],
        sem=("arbitrary",), args=(dxo, y, w, nrm, gu), jobs=jobs)


def _atb(a, b, out_dtype, bm, bn, name, jobs=()):
    S, M = a.shape
    N = b.shape[1]
    bk = _tile(S, 1024)
    nk = S // bk

    def body(a_ref, b_ref, o_ref, acc):
        k = pl.program_id(2)

        @pl.when(k == 0)
        def _():
            acc[...] = jnp.zeros_like(acc)
        acc[...] += _dot_tn(a_ref[...], b_ref[...])

        @pl.when(k == nk - 1)
        def _():
            o_ref[...] = acc[...].astype(o_ref.dtype)

    (out,), extra = _pcall(
        body, name=name, grid=(M // bm, N // bn, nk),
        in_specs=[pl.BlockSpec((bk, bm), lambda m, n, k: (k, m)),
                  pl.BlockSpec((bk, bn), lambda m, n, k: (k, n))],
        out_specs=[pl.BlockSpec((bm, bn), lambda m, n, k: (m, n))],
        out_shape=[jax.ShapeDtypeStruct((M, N), out_dtype)],
        scratch_shapes=[pltpu.VMEM((bm, bn), F32)],
        sem=("arbitrary", "arbitrary", "arbitrary"), args=(a, b), jobs=jobs)
    return out, extra


def _norm_bwd(dh, xv, nrm_ref, red_ref):
    rstd = lax.rsqrt(jnp.mean(xv * xv, axis=-1, keepdims=True) + EPS)
    xn = xv * rstd
    gain = nrm_ref[0:1, :]
    hn = xn * gain
    dhn = dh * (1.0 + nrm_ref[2:3, :])
    red_ref[0:1, :] += jnp.sum(dh, axis=0, keepdims=True)
    red_ref[1:2, :] += jnp.sum(dh * hn, axis=0, keepdims=True)
    red_ref[2:3, :] += jnp.sum(dhn * xn, axis=0, keepdims=True)
    dxn = dhn * gain
    return rstd * (dxn - xn * jnp.mean(dxn * xn, axis=-1, keepdims=True))


def _nt_norm_bwd(dout, w, x, nrm, dxo, name, jobs=()):
    S, N = dout.shape
    Dm = w.shape[0]
    tm = _tile(S, 512)

    def body(do_ref, w_ref, x_ref, nrm_ref, dxo_ref, dx_ref, red_ref):
        @pl.when(pl.program_id(0) == 0)
        def _():
            red_ref[...] = jnp.zeros_like(red_ref)
        dh = _dot_nt(do_ref[...], w_ref[...])
        dx_ref[...] = dxo_ref[...] + _norm_bwd(dh, x_ref[...], nrm_ref, red_ref)

    return _pcall(
        body, name=name, grid=(S // tm,),
        in_specs=[pl.BlockSpec((tm, N), lambda i: (i, 0)),
                  _resident((Dm, N)),
                  pl.BlockSpec((tm, Dm), lambda i: (i, 0)),
                  pl.BlockSpec((8, Dm), lambda i: (0, 0)),
                  pl.BlockSpec((tm, Dm), lambda i: (i, 0))],
        out_specs=[pl.BlockSpec((tm, Dm), lambda i: (i, 0)),
                   pl.BlockSpec((8, Dm), lambda i: (0, 0))],
        out_shape=[jax.ShapeDtypeStruct((S, Dm), F32), jax.ShapeDtypeStruct((8, Dm), F32)],
        sem=("arbitrary",), args=(dout, w, x, nrm, dxo), jobs=jobs)


def _alibi_slope(h):
    return float(2.0 ** (-8.0 * (h + 1) / N_Q_HEADS))


def _head_planes(pair_cols):
    lane = lax.broadcasted_iota(jnp.int32, pair_cols.shape, 1)
    low = lane < HEAD_DIM
    h0_lo = jnp.where(low, pair_cols, 0.0)
    h1_hi = jnp.where(low, 0.0, pair_cols)
    h0_hi = pltpu.roll(h0_lo, HEAD_DIM, 1)
    h1_lo = pltpu.roll(h1_hi, HEAD_DIM, 1)
    return ((h0_lo.astype(BF16), h0_hi.astype(BF16)), (h1_lo.astype(BF16), h1_hi.astype(BF16)))


def _band_geometry(first_block):
    qi = lax.broadcasted_iota(jnp.int32, (BLOCK, BLOCK), 0)
    kj = lax.broadcasted_iota(jnp.int32, (BLOCK, BLOCK), 1)
    own = kj <= qi
    dist = jnp.where(own, qi - kj, qi + BLOCK - kj).astype(F32)
    valid = kj <= qi + BLOCK * (1 - first_block)
    return own, dist, valid


def _fold(band, own):
    return jnp.where(own, band[:, BLOCK:], band[:, :BLOCK])


def _unfold(v, own):
    return jnp.concatenate([jnp.where(own, 0.0, v), jnp.where(own, v, 0.0)], axis=1)


def _softmax_band(s, h, geometry, sink):
    own, dist, valid = geometry
    s = jnp.where(valid, s - _alibi_slope(h) * dist, NEG_INF)
    m = jnp.maximum(jnp.max(s, axis=-1, keepdims=True), sink)
    p = jnp.exp(s - m)
    e_sink = jnp.exp(sink - m)
    inv = 1.0 / (jnp.sum(p, axis=-1, keepdims=True) + e_sink)
    return p * inv, e_sink * inv


def _past(cur, prev, s, row):
    return jnp.where(row < s, pltpu.roll(prev, s, 0), pltpu.roll(cur, s, 0))


def _future(cur, nxt, s, row):
    T = cur.shape[0]
    return jnp.where(row >= T - s, pltpu.roll(nxt, T - s, 0), pltpu.roll(cur, T - s, 0))


def _edge_row(v, last):
    T = v.shape[0]
    r8 = lax.broadcasted_iota(jnp.int32, (SUBLANES, v.shape[1]), 0)
    blk = v[T - SUBLANES:, :] if last else v[:SUBLANES, :]
    return jnp.sum(jnp.where(r8 == (SUBLANES - 1 if last else 0), blk, 0.0), axis=0, keepdims=True)


def _lru_gates(lx, lx_prev, small_ref, wa_ref, wx_ref, row, t0):
    xc = (small_ref[4:5, :] + small_ref[3:4, :] * lx + small_ref[2:3, :] * _past(lx, lx_prev, 1, row)
          + small_ref[1:2, :] * _past(lx, lx_prev, 2, row) + small_ref[0:1, :] * _past(lx, lx_prev, 3, row))
    xcb = xc.astype(BF16)
    r = _sigmoid(_dot(xcb, wa_ref[...]) + small_ref[5:6, :])
    ig = _sigmoid(_dot(xcb, wx_ref[...]) + small_ref[6:7, :])
    sp = _softplus_neg(small_ref[7:8, :])
    la = (-LRU_C) * r * sp
    a = jnp.exp(la)
    first = (row + t0) == 0
    mult = jnp.where(first, 1.0, jnp.sqrt(-_expm1(2.0 * la)))
    return xc, xcb, r, ig, sp, a, mult, first


def _mixer_fwd(x, nrm, w_in, w_out, sinks, small, wa, wx, name, jobs=()):
    S, Dm = x.shape
    T = MIX_TILE
    nT = S // T
    nb = T // BLOCK

    def body(x_ref, nrm_ref, w_in_ref, w_out_ref, sink_ref, small_ref, wa_ref, wx_ref,
             xo_ref, h_ref, proj_ref, y_ref, ymo_ref, hp_ref, *carried_state):
        xv = x_ref[...]
        rstd = lax.rsqrt(jnp.mean(xv * xv, axis=-1, keepdims=True) + EPS)
        hn = (xv * rstd) * nrm_ref[0:1, :]
        hb = (hn * (1.0 + nrm_ref[2:3, :]) + nrm_ref[1:2, :]).astype(BF16)
        h_ref[...] = hb
        proj_ref[...] = _dot(hb, w_in_ref[...])
        core(proj_ref, sink_ref, small_ref, wa_ref, wx_ref, y_ref, hp_ref, *carried_state)
        yo = _dot(y_ref[...], w_out_ref[...])
        xo_ref[...] = xv + nrm_ref[3:4, :] * yo
        ymo_ref[...] = yo.astype(BF16)

    def core(proj_ref, sink_ref, small_ref, wa_ref, wx_ref, y_ref, hp_ref, kvp, lxp, zp, hcar):
        i = pl.program_id(0)

        @pl.when(i == 0)
        def _():
            kvp[...] = jnp.zeros_like(kvp)
            lxp[...] = jnp.zeros_like(lxp)
            zp[...] = jnp.zeros_like(zp)
            hcar[...] = jnp.zeros_like(hcar)

        row = lax.broadcasted_iota(jnp.int32, (T, LRU_WIDTH), 0)

        kv = proj_ref[:, C_KV:C_KV + 2 * KV_WIDTH]
        ext = jnp.concatenate([kvp[...], kv], axis=0)
        kx = _head_planes(ext[:, :KV_WIDTH])
        vx = _head_planes(ext[:, KV_WIDTH:])
        first_tile = jnp.where(i == 0, 1, 0)
        units = [(b, pair, e) for b in range(nb) for pair in range(N_Q_HEADS // 2) for e in range(2)]
        geometry = [_band_geometry(first_tile if b == 0 else 0) for b in range(nb)]
        keys = [slice(b * BLOCK, (b + 2) * BLOCK) for b in range(nb)]
        qp = {(b, pair): (proj_ref[b * BLOCK:(b + 1) * BLOCK, pair * LANES:(pair + 1) * LANES] * 0.125).astype(BF16)
              for b in range(nb) for pair in range(N_Q_HEADS // 2)}
        scores = [_fold(_dot_nt(qp[(b, pair)], kx[pair // 2][e][keys[b]]), geometry[b][0]) for b, pair, e in units]
        probs = [_unfold(_softmax_band(s, 2 * pair + e, geometry[b], sink_ref[2 * pair + e])[0],
                         geometry[b][0]).astype(BF16) for s, (b, pair, e) in zip(scores, units)]
        outs = [_dot(p, vx[pair // 2][e][keys[b]]) for p, (b, pair, e) in zip(probs, units)]
        for u in range(0, len(units), 2):
            b, pair, _ = units[u]
            y_ref[b * BLOCK:(b + 1) * BLOCK, pair * LANES:(pair + 1) * LANES] = (outs[u] + outs[u + 1]).astype(BF16)
        kvp[...] = kv[T - BLOCK:, :]

        lx = proj_ref[:, C_LX:C_LX + LRU_WIDTH]
        xc, _, _, ig, _, a, mult, _ = _lru_gates(lx, lxp[...], small_ref, wa_ref, wx_ref, row, i * T)
        lxp[...] = lx
        aa = a
        bb = mult * (ig * xc)
        s = 1
        while s < T:
            a_sh = jnp.where(row >= s, pltpu.roll(aa, s, 0), 1.0)
            b_sh = jnp.where(row >= s, pltpu.roll(bb, s, 0), 0.0)
            bb = aa * b_sh + bb
            aa = aa * a_sh
            s *= 2
        hc = hcar[0:1, :]
        hh = bb + aa * hc
        hp_ref[...] = jnp.where(row < 1, hc, pltpu.roll(hh, 1, 0))
        hcar[...] = jnp.broadcast_to(_edge_row(hh, True), hcar.shape)
        gl, _ = _gelu(proj_ref[:, C_LG:C_LG + LRU_WIDTH])
        y_ref[:, ATTN_WIDTH:ATTN_WIDTH + LRU_WIDTH] = (gl * hh).astype(BF16)

        z = proj_ref[:, C_SC:C_SC + CONV_WIDTH] * proj_ref[:, C_SX:C_SX + CONV_WIDTH]
        c3 = (small_ref[10:11, :] * z + small_ref[9:10, :] * _past(z, zp[...], 1, row)
              + small_ref[8:9, :] * _past(z, zp[...], 2, row))
        zp[...] = z
        y_ref[:, ATTN_WIDTH + LRU_WIDTH:] = (proj_ref[:, C_SB:C_SB + CONV_WIDTH] * c3).astype(BF16)

    fix = lambda i: (0, 0)
    row = lambda i: (i, 0)
    return _pcall(
        body, name=name, grid=(nT,),
        in_specs=[pl.BlockSpec((T, Dm), row), pl.BlockSpec((8, Dm), fix),
                  _resident((Dm, IN_PROJ_WIDTH)), _resident((D_MODEL, Dm)),
                  pl.BlockSpec(memory_space=pltpu.SMEM),
                  pl.BlockSpec((16, LRU_WIDTH), fix),
                  pl.BlockSpec((LRU_WIDTH, LRU_WIDTH), fix),
                  pl.BlockSpec((LRU_WIDTH, LRU_WIDTH), fix)],
        out_specs=[pl.BlockSpec((T, Dm), row), pl.BlockSpec((T, Dm), row), pl.BlockSpec((T, IN_PROJ_WIDTH), row),
                   pl.BlockSpec((T, D_MODEL), row), pl.BlockSpec((T, Dm), row), pl.BlockSpec((T, LRU_WIDTH), row)],
        out_shape=[jax.ShapeDtypeStruct((S, Dm), F32), jax.ShapeDtypeStruct((S, Dm), BF16),
                   jax.ShapeDtypeStruct((S, IN_PROJ_WIDTH), F32), jax.ShapeDtypeStruct((S, D_MODEL), BF16),
                   jax.ShapeDtypeStruct((S, Dm), BF16), jax.ShapeDtypeStruct((S, LRU_WIDTH), F32)],
        scratch_shapes=[pltpu.VMEM((BLOCK, 2 * KV_WIDTH), F32), pltpu.VMEM((T, LRU_WIDTH), F32),
                        pltpu.VMEM((T, CONV_WIDTH), F32), pltpu.VMEM((SUBLANES, LRU_WIDTH), F32)],
        sem=("arbitrary",), args=(x, nrm, w_in, w_out, sinks, small, wa, wx), jobs=jobs)


def _mixer_bwd(x, nrm, dxo, h, proj, ymix, ymo, hprev, w_in, w_out, sinks, small, wa, wx, name, jobs=()):
    S, Dm = x.shape
    T = MIX_TILE
    nT = S // T
    nb = T // BLOCK
    bpt = T // BLOCK

    def body(x_ref, nrm_ref, dxo_ref, h_ref, proj_ref, kvprev_ref, lxprev_ref, scprev_ref, sxprev_ref, ymix_ref,
             ymo_ref, hp_ref, w_in_ref, w_out_ref, sink_ref, small_ref, wa_ref, wx_ref,
             dx_ref, red_ref, dgate_ref, dwo_ref, dwi_ref, dsm_ref, dsink_ref, dwa_ref, dwx_ref,
             dy_s, dp_s, acc_o, acc_i, *carried_state):
        @pl.when(pl.program_id(0) == 0)
        def _():
            for r in (red_ref, dgate_ref, acc_o, acc_i):
                r[...] = jnp.zeros_like(r)

        dxo_v = dxo_ref[...]
        dyb = (nrm_ref[3:4, :] * dxo_v).astype(BF16)
        dgate_ref[0:1, :] += jnp.sum(ymo_ref[...].astype(F32) * dxo_v, axis=0, keepdims=True)
        dy_s[...] = _dot_nt(dyb, w_out_ref[...])
        acc_o[...] += _dot_tn(ymix_ref[...], dyb)
        core(proj_ref, kvprev_ref, lxprev_ref, scprev_ref, sxprev_ref, dy_s, hp_ref, sink_ref, small_ref,
             wa_ref, wx_ref, dp_s, dsm_ref, dsink_ref, dwa_ref, dwx_ref, *carried_state)
        dpb = dp_s[...]
        acc_i[...] += _dot_tn(h_ref[...], dpb)
        dx_ref[...] = dxo_v + _norm_bwd(_dot_nt(dpb, w_in_ref[...]), x_ref[...], nrm_ref, red_ref)

        @pl.when(pl.program_id(0) == nT - 1)
        def _():
            dwo_ref[...] = acc_o[...].astype(BF16)
            dwi_ref[...] = acc_i[...].astype(BF16)

    def core(proj_ref, kvprev_ref, lxprev_ref, scprev_ref, sxprev_ref, dy_ref, hp_ref, sink_ref, small_ref,
             wa_ref, wx_ref, dp_ref, dsm_ref, dsink_ref, dwa_ref, dwx_ref,
             dk_s, dv_s, dkv_c, dxc_n, dc3_n, p_c):
        i = pl.program_id(0)
        ti = nT - 1 - i
        has_prev = jnp.where(ti == 0, 0.0, 1.0)

        @pl.when(i == 0)
        def _():
            for r in (dkv_c, dxc_n, dc3_n, p_c, dsm_ref, dsink_ref, dwa_ref, dwx_ref):
                r[...] = jnp.zeros_like(r)

        row = lax.broadcasted_iota(jnp.int32, (T, LRU_WIDTH), 0)

        kv = proj_ref[:, C_KV:C_KV + 2 * KV_WIDTH]
        ext = jnp.concatenate([kvprev_ref[...] * has_prev, kv], axis=0)
        kx = _head_planes(ext[:, :KV_WIDTH])
        vx = _head_planes(ext[:, KV_WIDTH:])
        dk_s[...] = jnp.zeros_like(dk_s)
        dv_s[...] = jnp.zeros_like(dv_s)
        dk_s[:, T:] = dkv_c[:, :BLOCK]
        dv_s[:, T:] = dkv_c[:, BLOCK:]
        first_tile = jnp.where(ti == 0, 1, 0)
        for b in range(nb):
            units = [(pair, e) for pair in range(N_Q_HEADS // 2) for e in range(2)]
            geometry = _band_geometry(first_tile if b == 0 else 0)
            own = geometry[0]
            keys = slice(b * BLOCK, (b + 2) * BLOCK)
            tile = {pair: (slice(b * BLOCK, (b + 1) * BLOCK), slice(pair * LANES, (pair + 1) * LANES))
                    for pair in range(N_Q_HEADS // 2)}
            qp = {k: (proj_ref[rc] * 0.125).astype(BF16) for k, rc in tile.items()}
            dob = {k: dy_ref[rc].astype(BF16) for k, rc in tile.items()}
            qp_t = {k: jnp.transpose(proj_ref[rc] * 0.125).astype(BF16) for k, rc in tile.items()}
            dob_t = {k: jnp.transpose(dy_ref[rc]).astype(BF16) for k, rc in tile.items()}
            scores = [_fold(_dot_nt(qp[pair], kx[pair // 2][e][keys]), own) for pair, e in units]
            dprob = [_fold(_dot_nt(dob[pair], vx[pair // 2][e][keys]), own) for pair, e in units]
            pn_wide, ds_wide = [], []
            for s, dpm, (pair, e) in zip(scores, dprob, units):
                h = 2 * pair + e
                pn, psink = _softmax_band(s, h, geometry, sink_ref[h])
                dsum = jnp.sum(pn * dpm, axis=-1, keepdims=True)
                dsink_ref[h:h + 1, :] += jnp.full((1, LANES), -1.0, F32) * jnp.sum(psink * dsum)
                pn_wide.append(_unfold(pn, own).astype(BF16))
                ds_wide.append(_unfold(pn * (dpm - dsum), own).astype(BF16))
            dq = {}
            for pw, ds, (pair, e) in zip(pn_wide, ds_wide, units):
                g = pair // 2
                head_e = slice(e * HEAD_DIM, (e + 1) * HEAD_DIM)
                head_g = slice(g * HEAD_DIM, (g + 1) * HEAD_DIM)
                dv_s[head_g, keys] += _dot(dob_t[pair], pw)[head_e, :]
                dk_s[head_g, keys] += _dot(qp_t[pair], ds)[head_e, :]
                part = _dot(ds, kx[g][e][keys])
                dq[pair] = part if e == 0 else dq[pair] + part
            for k, rc in tile.items():
                dp_ref[rc] = (0.125 * dq[k]).astype(BF16)
        dp_ref[:, C_KV:C_KV + KV_WIDTH] = jnp.transpose(dk_s[:, BLOCK:]).astype(BF16)
        dp_ref[:, C_KV + KV_WIDTH:C_KV + 2 * KV_WIDTH] = jnp.transpose(dv_s[:, BLOCK:]).astype(BF16)
        dkv_c[:, :BLOCK] = dk_s[:, :BLOCK]
        dkv_c[:, BLOCK:] = dv_s[:, :BLOCK]

        lx = proj_ref[:, C_LX:C_LX + LRU_WIDTH]
        lxprev = lxprev_ref[...] * has_prev
        xc, xcb, r, ig, sp, a, mult, first = _lru_gates(lx, lxprev, small_ref, wa_ref, wx_ref, row, ti * T)
        hp = hp_ref[...]
        hh = a * hp + mult * (ig * xc)
        lg = proj_ref[:, C_LG:C_LG + LRU_WIDTH]
        gl, th = _gelu(lg)
        dyl = dy_ref[:, ATTN_WIDTH:ATTN_WIDTH + LRU_WIDTH]
        dp_ref[:, C_LG:C_LG + LRU_WIDTH] = (dyl * hh * _gelu_grad(lg, th)).astype(BF16)
        aa = jnp.where(row < T - 1, pltpu.roll(a, T - 1, 0), 1.0)
        bb = dyl * gl
        s = 1
        while s < T:
            a_sh = jnp.where(row < T - s, pltpu.roll(aa, T - s, 0), 1.0)
            b_sh = jnp.where(row < T - s, pltpu.roll(bb, T - s, 0), 0.0)
            bb = bb + aa * b_sh
            aa = aa * a_sh
            s *= 2
        G = bb + aa * p_c[0:1, :]
        p_c[...] = jnp.broadcast_to(_edge_row(a * G, False), p_c.shape)
        da = G * hp
        dmult = G * (ig * xc)
        dig = G * mult * xc
        dxc = G * mult * ig
        dla = da * a + dmult * jnp.where(first, 0.0, -(a * a) / mult)
        dr = dla * ((-LRU_C) * sp)
        lam = small_ref[7:8, :]
        dsm_ref[7:8, :] += jnp.sum(dla * ((-LRU_C) * r), axis=0, keepdims=True) * (-_sigmoid(-lam))
        dpa = dr * r * (1.0 - r)
        dpx = dig * ig * (1.0 - ig)
        dsm_ref[5:6, :] += jnp.sum(dpa, axis=0, keepdims=True)
        dsm_ref[6:7, :] += jnp.sum(dpx, axis=0, keepdims=True)
        dpab = dpa.astype(BF16)
        dpxb = dpx.astype(BF16)
        dwa_ref[...] += _dot_tn(xcb, dpab)
        dwx_ref[...] += _dot_tn(xcb, dpxb)
        dxc = dxc + _dot_nt(dpab, wa_ref[...]) + _dot_nt(dpxb, wx_ref[...])
        dsm_ref[4:5, :] += jnp.sum(dxc, axis=0, keepdims=True)
        dsm_ref[3:4, :] += jnp.sum(dxc * lx, axis=0, keepdims=True)
        for k in range(3):
            dsm_ref[k:k + 1, :] += jnp.sum(dxc * _past(lx, lxprev, 3 - k, row), axis=0, keepdims=True)
        nxt = dxc_n[...]
        dlx = (small_ref[3:4, :] * dxc + small_ref[2:3, :] * _future(dxc, nxt, 1, row)
               + small_ref[1:2, :] * _future(dxc, nxt, 2, row) + small_ref[0:1, :] * _future(dxc, nxt, 3, row))
        dxc_n[...] = dxc
        dp_ref[:, C_LX:C_LX + LRU_WIDTH] = dlx.astype(BF16)

        sc = proj_ref[:, C_SC:C_SC + CONV_WIDTH]
        sx = proj_ref[:, C_SX:C_SX + CONV_WIDTH]
        sb = proj_ref[:, C_SB:C_SB + CONV_WIDTH]
        z = sc * sx
        zprev = (scprev_ref[...] * sxprev_ref[...]) * has_prev
        z1 = _past(z, zprev, 1, row)
        z2 = _past(z, zprev, 2, row)
        c3 = small_ref[10:11, :] * z + small_ref[9:10, :] * z1 + small_ref[8:9, :] * z2
        dys = dy_ref[:, ATTN_WIDTH + LRU_WIDTH:]
        dp_ref[:, C_SB:C_SB + CONV_WIDTH] = (dys * c3).astype(BF16)
        dc3 = dys * sb
        dsm_ref[10:11, :] += jnp.sum(dc3 * z, axis=0, keepdims=True)
        dsm_ref[9:10, :] += jnp.sum(dc3 * z1, axis=0, keepdims=True)
        dsm_ref[8:9, :] += jnp.sum(dc3 * z2, axis=0, keepdims=True)
        nxt3 = dc3_n[...]
        dz = (small_ref[10:11, :] * dc3 + small_ref[9:10, :] * _future(dc3, nxt3, 1, row)
              + small_ref[8:9, :] * _future(dc3, nxt3, 2, row))
        dc3_n[...] = dc3
        dp_ref[:, C_SC:C_SC + CONV_WIDTH] = (dz * sx).astype(BF16)
        dp_ref[:, C_SX:C_SX + CONV_WIDTH] = (dz * sc).astype(BF16)

    fix = lambda i: (0, 0)
    cur = lambda i: (nT - 1 - i, 0)
    prev_cols = lambda cb: (lambda i: (jnp.maximum(nT - 2 - i, 0), cb))
    return _pcall(
        body, name=name, grid=(nT,),
        in_specs=[pl.BlockSpec((T, Dm), cur), pl.BlockSpec((8, Dm), fix), pl.BlockSpec((T, Dm), cur),
                  pl.BlockSpec((T, Dm), cur),
                  pl.BlockSpec((T, IN_PROJ_WIDTH), cur),
                  pl.BlockSpec((BLOCK, 2 * KV_WIDTH),
                               lambda i: (jnp.maximum((nT - 1 - i) * bpt - 1, 0), C_KV // (2 * KV_WIDTH))),
                  pl.BlockSpec((T, LRU_WIDTH), prev_cols(C_LX // LRU_WIDTH)),
                  pl.BlockSpec((T, CONV_WIDTH), prev_cols(C_SC // CONV_WIDTH)),
                  pl.BlockSpec((T, CONV_WIDTH), prev_cols(C_SX // CONV_WIDTH)),
                  pl.BlockSpec((T, D_MODEL), cur), pl.BlockSpec((T, Dm), cur),
                  pl.BlockSpec((T, LRU_WIDTH), cur),
                  _resident((Dm, IN_PROJ_WIDTH)), _resident((D_MODEL, Dm)),
                  pl.BlockSpec(memory_space=pltpu.SMEM),
                  pl.BlockSpec((16, LRU_WIDTH), fix),
                  pl.BlockSpec((LRU_WIDTH, LRU_WIDTH), fix),
                  pl.BlockSpec((LRU_WIDTH, LRU_WIDTH), fix)],
        out_specs=[pl.BlockSpec((T, Dm), cur), pl.BlockSpec((8, Dm), fix), pl.BlockSpec((8, Dm), fix),
                   _resident((D_MODEL, Dm)), _resident((Dm, IN_PROJ_WIDTH)),
                   pl.BlockSpec((16, LRU_WIDTH), fix),
                   pl.BlockSpec((SUBLANES, LANES), fix),
                   pl.BlockSpec((LRU_WIDTH, LRU_WIDTH), fix),
                   pl.BlockSpec((LRU_WIDTH, LRU_WIDTH), fix)],
        out_shape=[jax.ShapeDtypeStruct((S, Dm), F32), jax.ShapeDtypeStruct((8, Dm), F32),
                   jax.ShapeDtypeStruct((8, Dm), F32),
                   jax.ShapeDtypeStruct((D_MODEL, Dm), BF16), jax.ShapeDtypeStruct((Dm, IN_PROJ_WIDTH), BF16),
                   jax.ShapeDtypeStruct((16, LRU_WIDTH), F32),
                   jax.ShapeDtypeStruct((SUBLANES, LANES), F32),
                   jax.ShapeDtypeStruct((LRU_WIDTH, LRU_WIDTH), F32),
                   jax.ShapeDtypeStruct((LRU_WIDTH, LRU_WIDTH), F32)],
        scratch_shapes=[pltpu.VMEM((T, D_MODEL), F32), pltpu.VMEM((T, IN_PROJ_WIDTH), BF16),
                        pltpu.VMEM((D_MODEL, Dm), F32), pltpu.VMEM((Dm, IN_PROJ_WIDTH), F32),
                        pltpu.VMEM((KV_WIDTH, T + BLOCK), F32), pltpu.VMEM((KV_WIDTH, T + BLOCK), F32),
                        pltpu.VMEM((BLOCK, 2 * KV_WIDTH), F32), pltpu.VMEM((T, LRU_WIDTH), F32),
                        pltpu.VMEM((T, CONV_WIDTH), F32), pltpu.VMEM((SUBLANES, LRU_WIDTH), F32)],
        sem=("arbitrary",),
        args=(x, nrm, dxo, h, proj, proj, proj, proj, proj, ymix, ymo, hprev, w_in, w_out, sinks, small, wa, wx),
        jobs=jobs)


def _adamw_update(g, w_ref, m_ref, v_ref, go_ref, d_ref, mo_ref, vo_ref):
    mn = ADAM_B1 * m_ref[...] + (1.0 - ADAM_B1) * g
    vn = ADAM_B2 * v_ref[...] + (1.0 - ADAM_B2) * (g * g)
    go_ref[...] = g
    mo_ref[...] = mn
    vo_ref[...] = vn
    m_hat = mn / (1.0 - ADAM_B1 ** ADAM_STEP)
    v_hat = vn / (1.0 - ADAM_B2 ** ADAM_STEP)
    d_ref[...] = (-ADAM_LR) * (m_hat / (jnp.sqrt(v_hat) + ADAM_EPS) + ADAM_WD * w_ref[...])


def _adamw(w, g, m, v, name):
    R, C = w.shape
    tr = 8
    for cand in (512, 256, 128, 64, 32, 16, 8):
        if R % cand == 0 and cand * C * 4 <= (1 << 20):
            tr = cand
            break

    def body(w_ref, g_ref, *rest):
        _adamw_update(g_ref[...], w_ref, *rest)

    spec = pl.BlockSpec((tr, C), lambda i: (i, 0))
    return _pcall(body, name=name, grid=(R // tr,), in_specs=[spec] * 4, out_specs=[spec] * 4,
                  out_shape=[jax.ShapeDtypeStruct((R, C), F32)] * 4, sem=("arbitrary",), args=(w, g, m, v))


def _adamw_many(ws, gs, ms, vs, name):
    n = len(ws)

    def body(*refs):
        w_r, g_r, m_r, v_r, go, do, mo, vo = (refs[k * n:(k + 1) * n] for k in range(8))
        for t in range(n):
            _adamw_update(g_r[t][...], w_r[t], m_r[t], v_r[t], go[t], do[t], mo[t], vo[t])

    vmem = pl.BlockSpec(memory_space=pltpu.VMEM)
    res = pl.pallas_call(
        body, name=name, in_specs=[vmem] * (4 * n), out_specs=[vmem] * (4 * n),
        out_shape=[jax.ShapeDtypeStruct(w.shape, F32) for w in ws] * 4,
        compiler_params=pltpu.CompilerParams(vmem_limit_bytes=VMEM_LIMIT),
    )(*ws, *gs, *ms, *vs)
    return [res[k * n:(k + 1) * n] for k in range(4)]


def _adamw_partials(w, partials, m, v, name):
    nl = len(partials)
    _, R, C = partials[0][0].shape
    tr = 8
    for cand in (256, 128, 64, 32, 16):
        if R % cand == 0 and cand * C * 4 <= (1 << 19):
            tr = cand
            break
    ni = R // tr

    def body(*refs):
        w_ref, p_refs = refs[0], refs[1:1 + 2 * nl]
        m_ref, v_ref, go_ref, d_ref, mo_ref, vo_ref = refs[1 + 2 * nl:]
        for l in range(nl):
            @pl.when(pl.program_id(0) == l)
            def _(pair=p_refs[2 * l:2 * l + 2]):
                own, sib = [((p[0].astype(F32) + p[1].astype(F32)) + p[2].astype(F32)) + p[3].astype(F32)
                            for p in pair]
                _adamw_update(own + sib, w_ref, m_ref, v_ref, go_ref, d_ref, mo_ref, vo_ref)

    def slots(l):
        return pl.BlockSpec((N_CHIPS, tr, C),
                            lambda ll, i: (0, jnp.where(ll == l, i, jnp.where(ll < l, 0, ni - 1)), 0))

    spec = pl.BlockSpec((tr, C), lambda ll, i: (ll * ni + i, 0))
    return pl.pallas_call(
        body, name=name, grid=(nl, ni),
        in_specs=[spec] + [slots(l) for l in range(nl) for _ in range(2)] + [spec, spec], out_specs=[spec] * 4,
        out_shape=[jax.ShapeDtypeStruct((nl * R, C), F32)] * 4,
        compiler_params=_cp("arbitrary", "arbitrary"),
    )(w, *[p for pair in partials for p in pair], m, v)


GATHER_SEMS = 7


def _two_level_gather(x_ref, out_ref, send_sems, recv_sems, local_sem, base=0):
    M = x_ref.shape[0]
    x, y, c = lax.axis_index("x"), lax.axis_index("y"), lax.axis_index("c")
    me, sibling = (x, y, c), (x, y, 1 - c)
    chips = [(1 - x, y), (x, 1 - y), (1 - x, 1 - y)]

    def rows(px, py, pc):
        return out_ref.at[pl.ds(pl.multiple_of((4 * px + 2 * py + pc) * M, SUBLANES), M), :]

    def copy(k, block, to, src=None):
        return pltpu.make_async_remote_copy(
            src_ref=rows(*block) if src is None else src, dst_ref=rows(*block),
            send_sem=send_sems.at[base + k], recv_sem=recv_sems.at[base + k], device_id=to, device_id_type=MESH)

    mine = pltpu.make_async_copy(x_ref, rows(*me), local_sem)
    mine.start()
    first = [copy(0, me, sibling, src=x_ref)]
    first += [copy(1 + j, me, (*chip, c), src=x_ref) for j, chip in enumerate(chips)]
    for cp in first:
        cp.start()
    passed = [copy(4 + j, (*chip, c), sibling) for j, chip in enumerate(chips)]
    for j, chip in enumerate(chips):
        copy(1 + j, (*chip, c), me).wait_recv()
        passed[j].start()
    copy(0, sibling, me).wait_recv()
    for j, chip in enumerate(chips):
        copy(4 + j, (*chip, 1 - c), me).wait_recv()
    for cp in first + passed:
        cp.wait_send()
    mine.wait()


def _prologue(pack, w_mod, jobs, name):
    M = pack.shape[0]
    L, Dm, N = w_mod.shape
    nj = len(jobs)
    rows_c = Dm // LANES
    tn = 768

    def body(*refs):
        pack_ref, w_hbm = refs[:2]
        jin, refs = refs[2:2 + nj], refs[2 + nj:]
        g_ref, ca_ref, mod_ref = refs[:3]
        jout, refs = refs[3:3 + nj], refs[3 + nj:]
        part_ref, w_ref, send_sems, recv_sems, local_sem, *jsems = refs
        _run_jobs("start", jobs, jin, jout, jsems)
        load_w = pltpu.make_async_copy(w_hbm, w_ref, local_sem.at[2])
        load_w.start()
        _two_level_gather(pack_ref, g_ref, send_sems, recv_sems, local_sem.at[0], 0)
        ca_ref[...] = jnp.zeros_like(ca_ref)
        for r in range(rows_c):
            cv = g_ref[pl.ds(r, N_DEV, stride=M), :]
            ca_ref[0:N_DEV, r * LANES:(r + 1) * LANES] = (cv * _sigmoid(cv)).astype(BF16)
        ca = ca_ref[...]
        load_w.wait()
        for l in range(L):
            for n0 in range(0, N, tn):
                part_ref[l * 16:(l + 1) * 16, n0:n0 + tn] = _dot(ca, w_ref[l, :, n0:n0 + tn].astype(BF16))
        _two_level_gather(part_ref, mod_ref, send_sems, recv_sems, local_sem.at[1], GATHER_SEMS)
        _run_jobs("relay", jobs, jin, jout, jsems)
        _run_jobs("finish", jobs, jin, jout, jsems)

    vmem = pl.BlockSpec(memory_space=pltpu.VMEM)
    hbm = pl.BlockSpec(memory_space=pltpu.HBM)
    res = pl.pallas_call(
        body, name=name,
        out_shape=[jax.ShapeDtypeStruct((N_DEV * M, LANES), F32), jax.ShapeDtypeStruct((16, Dm), BF16),
                   jax.ShapeDtypeStruct((N_DEV * L * 16, N), F32)] + [job.out_shape for job in jobs],
        in_specs=[vmem, hbm] + [hbm] * nj, out_specs=[vmem, vmem, vmem] + [hbm] * nj,
        scratch_shapes=[pltpu.VMEM((L * 16, N), F32), pltpu.VMEM((L, Dm, N), F32),
                        pltpu.SemaphoreType.DMA((2 * GATHER_SEMS,)),
                        pltpu.SemaphoreType.DMA((2 * GATHER_SEMS,)), pltpu.SemaphoreType.DMA((3,))]
        + _job_scratch(nj),
        compiler_params=pltpu.CompilerParams(vmem_limit_bytes=VMEM_LIMIT),
    )(pack, w_mod, *[job.src for job in jobs])
    return res[0], res[1], res[2], list(res[3:])


def _all_gather_small(v, name, jobs=()):
    M, N = v.shape
    nj = len(jobs)

    def body(*refs):
        x_ref, jin = refs[0], refs[1:1 + nj]
        out_ref, sum_ref = refs[1 + nj:3 + nj]
        jout, (send_sems, recv_sems, local_sem, *jsems) = refs[3 + nj:3 + 2 * nj], refs[3 + 2 * nj:]
        _run_jobs("start", jobs, jin, jout, jsems)
        _two_level_gather(x_ref, out_ref, send_sems, recv_sems, local_sem)
        acc = out_ref[0:M, :]
        for d in range(1, N_DEV):
            acc = acc + out_ref[d * M:(d + 1) * M, :]
        sum_ref[...] = acc
        _run_jobs("relay", jobs, jin, jout, jsems)
        _run_jobs("finish", jobs, jin, jout, jsems)

    vmem = pl.BlockSpec(memory_space=pltpu.VMEM)
    hbm = pl.BlockSpec(memory_space=pltpu.HBM)
    res = pl.pallas_call(
        body, name=name,
        out_shape=[jax.ShapeDtypeStruct((N_DEV * M, N), F32), jax.ShapeDtypeStruct((M, N), F32)]
        + [job.out_shape for job in jobs],
        in_specs=[vmem] + [hbm] * nj, out_specs=[vmem, vmem] + [hbm] * nj,
        scratch_shapes=[pltpu.SemaphoreType.DMA((GATHER_SEMS,)), pltpu.SemaphoreType.DMA((GATHER_SEMS,)),
                        pltpu.SemaphoreType.DMA] + (_job_scratch(nj) if nj else []),
        compiler_params=pltpu.CompilerParams(vmem_limit_bytes=VMEM_LIMIT),
    )(v, *[job.src for job in jobs])
    return list(res[:2]), list(res[2:])


def _dw_mod(c_act, dmod, name):
    L, R, N = dmod.shape
    Dm = c_act.shape[1]
    tn = 768

    def body(c_ref, d_ref, o_ref):
        o_ref[0] = _dot_tn(c_ref[...], d_ref[0])

    return pl.pallas_call(
        body, name=name, grid=(L, N // tn),
        in_specs=[pl.BlockSpec((R, Dm), lambda l, n: (0, 0)), pl.BlockSpec((1, R, tn), lambda l, n: (l, 0, n))],
        out_specs=pl.BlockSpec((1, Dm, tn), lambda l, n: (l, 0, n)),
        out_shape=jax.ShapeDtypeStruct((L, Dm, N), F32),
        compiler_params=_cp("arbitrary", "arbitrary"),
    )(c_act, dmod)


_BIG = (("w_ffn1_gu", 1), ("w_ffn1_down", 0), ("w_ffn2_gu", 1), ("w_ffn2_down", 0), ("w_in", 1), ("w_out", 0))
_AXIS = dict(_BIG)

_GATHER_PLAN = {
    "first": [(0, "w_ffn1_gu"), (0, "w_ffn1_down")],
    (0, "ffn1"): [(0, "w_in"), (0, "w_out"), (0, "w_ffn2_gu")],
    (0, "mix"): [(0, "w_ffn2_down")],
    (0, "ffn2"): [(1, "w_ffn1_gu"), (1, "w_ffn1_down")],
    (1, "ffn1"): [(1, "w_in"), (1, "w_out"), (1, "w_ffn2_gu")],
    (1, "mix"): [(1, "w_ffn2_down")],
}


def _pack(arrs, rows_multiple=SUBLANES):
    flat = jnp.concatenate([a.astype(F32).reshape(-1) for a in arrs])
    unit = rows_multiple * LANES
    total = -(-flat.shape[0] // unit) * unit
    return jnp.pad(flat, (0, total - flat.shape[0])).reshape(total // LANES, LANES)


def _unpack(flat, shapes):
    out, off = [], 0
    for shp in shapes:
        n = int(math.prod(shp))
        out.append(flat[off:off + n].reshape(shp))
        off += n
    return out


def _block_diag(w):
    out = jnp.zeros((LRU_WIDTH, LRU_WIDTH), F32)
    for h in range(4):
        out = lax.dynamic_update_slice(out, w[h], (h * HEAD_DIM, h * HEAD_DIM))
    return out


def _diag_blocks(w):
    return jnp.stack([w[h * HEAD_DIM:(h + 1) * HEAD_DIM, h * HEAD_DIM:(h + 1) * HEAD_DIM] for h in range(4)])


def _rows8(*rows):
    z = jnp.zeros((8 - len(rows), rows[0].shape[-1]), F32)
    return jnp.concatenate([jnp.stack(rows), z], axis=0)


def kernel(x, c, w_mod, b_mod, g_norm, w_ffn1_gu, w_ffn1_down, w_ffn2_gu, w_ffn2_down, w_in, w_out, attn_sinks, lru_conv_w, lru_conv_b, lru_gate_a_w, lru_gate_a_b, lru_gate_x_w, lru_gate_x_b, lru_lambda, sc_conv_w, g_final, loss_target, m_w_mod, m_b_mod, m_g_norm, m_w_ffn1_gu, m_w_ffn1_down, m_w_ffn2_gu, m_w_ffn2_down, m_w_in, m_w_out, m_attn_sinks, m_lru_conv_w, m_lru_conv_b, m_lru_gate_a_w, m_lru_gate_a_b, m_lru_gate_x_w, m_lru_gate_x_b, m_lru_lambda, m_sc_conv_w, m_g_final, v_w_mod, v_b_mod, v_g_norm, v_w_ffn1_gu, v_w_ffn1_down, v_w_ffn2_gu, v_w_ffn2_down, v_w_in, v_w_out, v_attn_sinks, v_lru_conv_w, v_lru_conv_b, v_lru_gate_a_w, v_lru_gate_a_b, v_lru_gate_x_w, v_lru_gate_x_b, v_lru_lambda, v_sc_conv_w, v_g_final):
    W = dict(w_mod=w_mod, b_mod=b_mod, g_norm=g_norm, w_ffn1_gu=w_ffn1_gu, w_ffn1_down=w_ffn1_down,
             w_ffn2_gu=w_ffn2_gu, w_ffn2_down=w_ffn2_down, w_in=w_in, w_out=w_out, attn_sinks=attn_sinks,
             lru_conv_w=lru_conv_w, lru_conv_b=lru_conv_b, lru_gate_a_w=lru_gate_a_w, lru_gate_a_b=lru_gate_a_b,
             lru_gate_x_w=lru_gate_x_w, lru_gate_x_b=lru_gate_x_b, lru_lambda=lru_lambda, sc_conv_w=sc_conv_w,
             g_final=g_final)
    M1 = dict(w_mod=m_w_mod, b_mod=m_b_mod, g_norm=m_g_norm, w_ffn1_gu=m_w_ffn1_gu, w_ffn1_down=m_w_ffn1_down,
              w_ffn2_gu=m_w_ffn2_gu, w_ffn2_down=m_w_ffn2_down, w_in=m_w_in, w_out=m_w_out,
              attn_sinks=m_attn_sinks, lru_conv_w=m_lru_conv_w, lru_conv_b=m_lru_conv_b,
              lru_gate_a_w=m_lru_gate_a_w, lru_gate_a_b=m_lru_gate_a_b, lru_gate_x_w=m_lru_gate_x_w,
              lru_gate_x_b=m_lru_gate_x_b, lru_lambda=m_lru_lambda, sc_conv_w=m_sc_conv_w, g_final=m_g_final)
    V1 = dict(w_mod=v_w_mod, b_mod=v_b_mod, g_norm=v_g_norm, w_ffn1_gu=v_w_ffn1_gu, w_ffn1_down=v_w_ffn1_down,
              w_ffn2_gu=v_w_ffn2_gu, w_ffn2_down=v_w_ffn2_down, w_in=v_w_in, w_out=v_w_out,
              attn_sinks=v_attn_sinks, lru_conv_w=v_lru_conv_w, lru_conv_b=v_lru_conv_b,
              lru_gate_a_w=v_lru_gate_a_w, lru_gate_a_b=v_lru_gate_a_b, lru_gate_x_w=v_lru_gate_x_w,
              lru_gate_x_b=v_lru_gate_x_b, lru_lambda=v_lru_lambda, sc_conv_w=v_sc_conv_w, g_final=v_g_final)
    names = ["w_mod", "b_mod", "g_norm", "w_ffn1_gu", "w_ffn1_down", "w_ffn2_gu", "w_ffn2_down", "w_in", "w_out",
             "attn_sinks", "lru_conv_w", "lru_conv_b", "lru_gate_a_w", "lru_gate_a_b", "lru_gate_x_w",
             "lru_gate_x_b", "lru_lambda", "sc_conv_w", "g_final"]

    xs = x[0]
    tgt = loss_target[0]
    S = xs.shape[0]
    chip = 2 * lax.axis_index("x") + lax.axis_index("y")
    batch = 2 * chip + lax.axis_index("c")
    L = DEPTH

    full = {}

    def gather_jobs(key):
        return [_GatherJob(W[n][l].astype(BF16), _AXIS[n]) for l, n in _GATHER_PLAN.get(key, ())]

    def landed(key, outs):
        full.update(zip(_GATHER_PLAN.get(key, ()), outs))

    fwd_shapes = [(D_MODEL,), g_norm.shape, lru_conv_w.shape, sc_conv_w.shape]
    gathered, c_act, mod_all, ex = _prologue(_pack([c[0], g_norm, lru_conv_w, sc_conv_w]), w_mod,
                                             gather_jobs("first"), "prologue")
    landed("first", ex)
    gathered = gathered.reshape(N_DEV, -1)
    per_chip = [_unpack(gathered[2 * jj], fwd_shapes) for jj in range(N_CHIPS)]
    g_norm_full = jnp.concatenate([p[1] for p in per_chip], axis=-1)
    lru_conv_w_full = jnp.concatenate([p[2] for p in per_chip], axis=-1)
    sc_conv_w_full = jnp.concatenate([p[3] for p in per_chip], axis=-1)
    mod_all = mod_all.reshape(N_DEV, L, 16, -1)
    mod_rows = [lax.dynamic_index_in_dim(mod_all[2 * jj], batch, axis=1, keepdims=False) for jj in range(N_CHIPS)]
    mod = (jnp.concatenate(mod_rows, axis=-1) + b_mod).reshape(L, 9, D_MODEL)

    nrm_all = jnp.concatenate([g_norm_full[:, :, None, :], mod.reshape(L, 3, 3, D_MODEL),
                               jnp.zeros((L, 3, 4, D_MODEL), F32)], axis=2)

    def nrm_rows(l, s):
        return nrm_all[l, s]

    def mixer_params(l):
        small = jnp.concatenate([lru_conv_w_full[l], lru_conv_b[l][None], lru_gate_a_b[l][None],
                                 lru_gate_x_b[l][None], lru_lambda[l][None], sc_conv_w_full[l],
                                 jnp.zeros((5, LRU_WIDTH), F32)], axis=0)
        return (attn_sinks[l], small, _block_diag(lru_gate_a_w[l]).astype(BF16),
                _block_diag(lru_gate_x_w[l]).astype(BF16))

    saved = []
    xcur = xs
    for l in range(L):
        n1, n2, n3 = nrm_rows(l, 0), nrm_rows(l, 1), nrm_rows(l, 2)

        def ffn(which, xin, nrm, head=None):
            key = (l, which)
            (xo, h, gu, y, *stats), ex = _ffn_fwd(xin, nrm, full[(l, f"w_{which}_gu")], full[(l, f"w_{which}_down")],
                                                  f"l{l}_{which}", gather_jobs(key), head)
            landed(key, ex)
            return (xo, *stats), (xin, h, gu, y)

        (x1,), s1 = ffn("ffn1", xcur, n1)
        mp = mixer_params(l)
        (x2, h2, proj, ymix, ymo, hprev), ex = _mixer_fwd(x1, n2, full[(l, "w_in")], full[(l, "w_out")], *mp,
                                                          f"l{l}_mix", gather_jobs((l, "mix")))
        landed((l, "mix"), ex)
        s2 = (x1, h2, proj, ymix, ymo, hprev, mp)
        (xcur, *stats), s3 = ffn("ffn2", x2, n3, (_rows8(g_final), tgt) if l == L - 1 else None)
        saved.append((n1, n2, n3, s1, s2, s3))

    dx, stats = xcur, stats[0]
    loss_here, d_g_final = stats[1, 0:1], stats[0]

    recv, theirs = {}, {}
    waiting = []

    def carried(fn, *a, extra=()):
        items = waiting + list(extra)
        waiting.clear()
        outs, landed_now = fn(*a, jobs=[_SiblingJob(recv[(ll, n)]) if g is None else _ScatterJob(g, _AXIS[n])
                                        for ll, n, g in items])
        for (ll, n, g), arr in zip(items, landed_now):
            if g is None:
                theirs[(ll, n)] = arr
            else:
                recv[(ll, n)] = arr
                waiting.append((ll, n, None))
        return outs

    dmod, d_gnorm, d_small = [None] * L, [None] * L, [None] * L
    for l in reversed(range(L)):
        n1, n2, n3, s1, s2, s3 = saved[l]

        def plain(fn, *a):
            return fn(*a)[0]

        def ffn_bwd(which, dxo, sv, nrm, last):
            xin, h, gu, y = sv
            tag = f"l{l}_{which}"
            dgu, dgate, dw_down = carried(
                _ffn_down_bwd, dxo, gu, y, full[(l, f"w_{which}_down")], nrm, tag + "_down_bwd")
            dw_gu = carried(_atb, h, dgu, BF16, 1024, 2816, tag + "_dw_gu", extra=[(l, f"w_{which}_down", dw_down)])
            mine = [(l, f"w_{which}_gu", dw_gu)]
            dxi, red = (carried if last else plain)(
                _nt_norm_bwd, dgu, full[(l, f"w_{which}_gu")], xin, nrm, dxo, tag + "_gu_bwd",
                **(dict(extra=mine) if last else {}))
            if not last:
                waiting.extend(mine)
            return dxi, (red[0], red[1], dgate[0]), red[2]

        dx, dm3, dg3 = ffn_bwd("ffn2", dx, s3, n3, False)
        x_in, h2, proj, ymix, ymo, hprev, mp = s2
        dx, red, dgate, dw_out, dw_in, dsm, dsink, dwa, dwx = carried(
            _mixer_bwd, x_in, n2, dx, h2, proj, ymix, ymo, hprev, full[(l, "w_in")], full[(l, "w_out")], *mp,
            f"l{l}_mix_bwd")
        waiting.extend([(l, "w_out", dw_out), (l, "w_in", dw_in)])
        dm2, dg2 = (red[0], red[1], dgate[0]), red[2]
        dx, dm1, dg1 = ffn_bwd("ffn1", dx, s1, n1, l == 0)
        dmod[l] = jnp.stack(list(dm1) + list(dm2) + list(dm3))
        d_gnorm[l] = jnp.stack([dg1, dg2, dg3])
        d_small[l] = (dsink[:, 0], dsm[0:4], dsm[4], _diag_blocks(dwa), dsm[5], _diag_blocks(dwx), dsm[6],
                      dsm[7], dsm[8:11])
    grad_x = dx[None]

    def both(k):
        return jnp.stack([d_small[0][k], d_small[1][k]])
    small_names = ["g_norm", "attn_sinks", "lru_conv_w", "lru_conv_b", "lru_gate_a_w", "lru_gate_a_b",
                   "lru_gate_x_w", "lru_gate_x_b", "lru_lambda", "sc_conv_w", "g_final"]
    small_parts = [jnp.stack(d_gnorm)] + [both(k) for k in range(9)] + [d_g_final]
    dmod_flat = jnp.stack(dmod).reshape(-1)
    bwd_gathered, bwd_sum = carried(_all_gather_small, _pack([dmod_flat] + small_parts + [loss_here]),
                                    "gather_small_bwd")
    n_mod = dmod_flat.shape[0]
    dmod_all = bwd_gathered.reshape(N_DEV, -1)[:, :n_mod].reshape(N_DEV, L, 9 * D_MODEL)
    bwd_sum = bwd_sum.reshape(-1)
    G = {"b_mod": bwd_sum[:n_mod].reshape(L, 9 * D_MODEL)}
    *small_sums, loss = _unpack(bwd_sum[n_mod:], [p.shape for p in small_parts] + [(1,)])
    loss = loss[0]
    G.update(zip(small_names, small_sums))
    for n in ("g_norm", "lru_conv_w", "sc_conv_w"):
        wdt = W[n].shape[-1]
        G[n] = lax.dynamic_slice_in_dim(G[n], chip * wdt, wdt, axis=G[n].ndim - 1)

    ncol = w_mod.shape[-1]
    dmod_cols = lax.dynamic_slice_in_dim(dmod_all, chip * ncol, ncol, axis=2)
    dmod_rows = jnp.pad(jnp.swapaxes(dmod_cols, 0, 1), ((0, 0), (0, 16 - N_DEV), (0, 0))).astype(BF16)
    g_w_mod = _dw_mod(c_act, dmod_rows, "dw_mod")

    out_g, out_d, out_m, out_v = {}, {}, {}, {}
    res, _ = _adamw(w_mod.reshape(-1, ncol), g_w_mod.reshape(-1, ncol), m_w_mod.reshape(-1, ncol),
                    v_w_mod.reshape(-1, ncol), "adamw_w_mod")
    out_g["w_mod"], out_d["w_mod"], out_m["w_mod"], out_v["w_mod"] = [r.reshape(w_mod.shape) for r in res]
    for n, _ in _BIG:
        shp = W[n].shape
        flat = (shp[0] * shp[1], shp[2])
        res = _adamw_partials(W[n].reshape(flat), [(recv[(l, n)], theirs[(l, n)]) for l in range(L)],
                              M1[n].reshape(flat), V1[n].reshape(flat), f"adamw_{n}")
        out_g[n], out_d[n], out_m[n], out_v[n] = [r.reshape(shp) for r in res]
    rest = ["b_mod"] + small_names

    def rows(a):
        return a.reshape(-1, a.shape[-1])
    res = _adamw_many([rows(W[n]) for n in rest], [rows(G[n]) for n in rest], [rows(M1[n]) for n in rest],
                      [rows(V1[n]) for n in rest], "adamw_small")
    for dst, group in zip((out_g, out_d, out_m, out_v), res):
        dst.update({n: r.reshape(W[n].shape) for n, r in zip(rest, group)})

    return (loss, grad_x, *[out_g[n] for n in names], *[out_d[n] for n in names],
            *[out_m[n] for n in names], *[out_v[n] for n in names])
```

```python
import math

import jax
import jax.numpy as jnp
from jax import lax
from jax.experimental import pallas as pl
from jax.experimental.pallas import tpu as pltpu

F32 = jnp.float32
BF16 = jnp.bfloat16

D_MODEL = 1024
DEPTH = 2
HEAD_DIM = 64
N_Q_HEADS = 8
ATTN_WIDTH = 512
KV_WIDTH = 128
LRU_WIDTH = 256
CONV_WIDTH = 256
IN_PROJ_WIDTH = 2048
BLOCK = 128
D_FF = 2816
EPS = 1e-6
NEG_INF = -1e30
LRU_C = 8.0
N_CHIPS = 4
N_DEV = 8

C_Q, C_KV, C_LX, C_LG, C_SB, C_SC, C_SX = 0, 512, 768, 1024, 1280, 1536, 1792

ADAM_LR = 0.001
ADAM_B1 = 0.9
ADAM_B2 = 0.999
ADAM_EPS = 1e-08
ADAM_WD = 0.01
ADAM_STEP = 10

LANES = 128
SUBLANES = 8
VMEM_LIMIT = 56 * 1024 * 1024
MIX_TILE = 256

MESH = pl.DeviceIdType.MESH


def _cp(*sem):
    return pltpu.CompilerParams(dimension_semantics=sem, vmem_limit_bytes=VMEM_LIMIT)


def _tile(n, pref):
    t = min(n, pref)
    while n % t:
        t //= 2
    return t


MXU_DIM = 256


def _resident(shape):
    return pl.BlockSpec(shape, lambda *_: (0, 0), pipeline_mode=pl.Buffered(1))


def _sigmoid(v):
    return 1.0 / (1.0 + jnp.exp(-v))


def _expm1(v):
    series = v * (1.0 + v * (0.5 + v * (1.0 / 6.0)))
    return jnp.where(v > -0.01, series, jnp.exp(v) - 1.0)


def _softplus_neg(lam):
    e = jnp.exp(-jnp.abs(lam))
    log1p = jnp.where(e < 1e-2, e * (1.0 - e * (0.5 - e * (1.0 / 3.0))), jnp.log(1.0 + e))
    return jnp.maximum(-lam, 0.0) + log1p


_GELU_K = math.sqrt(2.0 / math.pi)
_GELU_C = 0.044715


def _gelu(v):
    t = jnp.tanh(_GELU_K * (v + _GELU_C * v * v * v))
    return 0.5 * v * (1.0 + t), t


def _gelu_grad(v, t):
    return 0.5 * (1.0 + t) + 0.5 * v * (1.0 - t * t) * _GELU_K * (1.0 + 3.0 * _GELU_C * v * v)


def _dot(a, b):
    return jnp.dot(a, b, preferred_element_type=F32)


def _dot_nt(a, b):
    return lax.dot_general(a, b, (((1,), (1,)), ((), ())), preferred_element_type=F32)


def _dot_tn(a, b):
    return lax.dot_general(a, b, (((0,), (0,)), ((), ())), preferred_element_type=F32)


def _window(ref, axis, j, width):
    start = pl.multiple_of(j * width, LANES if axis == 1 else 16)
    if axis == 1:
        return ref.at[:, pl.ds(start, width)]
    return ref.at[pl.ds(start, width), :]


def _chip_peers():
    x, y, c = lax.axis_index("x"), lax.axis_index("y"), lax.axis_index("c")
    return x, y, c, [(1 - x, y), (x, 1 - y), (1 - x, 1 - y)]


class _GatherJob:
    def __init__(self, shard, axis):
        self.src, self.axis, self.width, self.half = shard, axis, shard.shape[axis], shard.shape[0] // 2
        full = tuple(d * N_CHIPS if k == axis else d for k, d in enumerate(shard.shape))
        self.out_shape = jax.ShapeDtypeStruct(full, shard.dtype)

    def _piece(self, ref, j, hf):
        if self.axis == 1:
            return ref.at[pl.ds(pl.multiple_of(hf * self.half, 16), self.half),
                          pl.ds(pl.multiple_of(j * self.width, LANES), self.width)]
        return ref.at[pl.ds(pl.multiple_of(j * self.width + hf * self.half, 16), self.half), :]

    def _copies(self, src, dst, send, recv, loc, t):
        x, y, c, chips = _chip_peers()
        j = 2 * x + y
        owners = [2 * px + py for px, py in chips]
        local = pltpu.make_async_copy(src, _window(dst, self.axis, j, self.width), loc.at[t])
        mine = src.at[pl.ds(pl.multiple_of(c * self.half, 16), self.half), :]

        def ici(k, owner):
            return pltpu.make_async_remote_copy(
                src_ref=mine, dst_ref=self._piece(dst, owner, c), send_sem=send.at[JOB_SEMS * t + k],
                recv_sem=recv.at[JOB_SEMS * t + k], device_id=(*chips[k], c), device_id_type=MESH)

        def relay(k, hf):
            piece = self._piece(dst, owners[k], hf)
            return pltpu.make_async_remote_copy(
                src_ref=piece, dst_ref=piece, send_sem=send.at[JOB_SEMS * t + 4 + k],
                recv_sem=recv.at[JOB_SEMS * t + 4 + k],
                device_id=(x, y, 1 - c), device_id_type=MESH)

        return (local, [ici(k, j) for k in range(3)], [ici(k, owners[k]) for k in range(3)],
                [relay(k, c) for k in range(3)], [relay(k, 1 - c) for k in range(3)])

    def start(self, *a):
        local, ici_out, _, _, _ = self._copies(*a)
        local.start()
        for cp in ici_out:
            cp.start()

    def relay(self, *a):
        _, _, ici_in, relay_out, _ = self._copies(*a)
        for arrived, onward in zip(ici_in, relay_out):
            arrived.wait_recv()
            onward.start()

    def finish(self, *a):
        local, ici_out, _, relay_out, relay_in = self._copies(*a)
        for cp in relay_in:
            cp.wait_recv()
        for cp in ici_out + relay_out:
            cp.wait_send()
        local.wait()


class _ScatterJob:
    def __init__(self, full, axis):
        self.src, self.axis, self.width = full, axis, full.shape[axis] // N_CHIPS
        shard = tuple(self.width if k == axis else d for k, d in enumerate(full.shape))
        self.out_shape = jax.ShapeDtypeStruct((N_CHIPS,) + shard, full.dtype)

    def _copies(self, src, dst, send, recv, loc, t):
        x, y, c, chips = _chip_peers()
        local = pltpu.make_async_copy(_window(src, self.axis, 2 * x + y, self.width), dst.at[3], loc.at[t])
        sends = [pltpu.make_async_remote_copy(
            src_ref=_window(src, self.axis, 2 * px + py, self.width), dst_ref=dst.at[k],
            send_sem=send.at[JOB_SEMS * t + k], recv_sem=recv.at[JOB_SEMS * t + k], device_id=(px, py, c),
            device_id_type=MESH) for k, (px, py) in enumerate(chips)]
        return local, sends

    def start(self, *a):
        local, sends = self._copies(*a)
        local.start()
        for cp in sends:
            cp.start()

    def relay(self, *a):
        pass

    def finish(self, *a):
        local, sends = self._copies(*a)
        for cp in sends:
            cp.wait_recv()
        for cp in sends:
            cp.wait_send()
        local.wait()


class _SiblingJob:
    def __init__(self, arr):
        self.src, self.out_shape = arr, jax.ShapeDtypeStruct(arr.shape, arr.dtype)

    def _copy(self, src, dst, send, recv, loc, t):
        x, y, c = lax.axis_index("x"), lax.axis_index("y"), lax.axis_index("c")
        return pltpu.make_async_remote_copy(
            src_ref=src, dst_ref=dst, send_sem=send.at[JOB_SEMS * t], recv_sem=recv.at[JOB_SEMS * t],
            device_id=(x, y, 1 - c), device_id_type=MESH)

    def start(self, *a):
        self._copy(*a).start()

    def relay(self, *a):
        pass

    def finish(self, *a):
        self._copy(*a).wait()


JOB_SEMS = 8


def _run_jobs(phase, jobs, srcs, dsts, sems):
    for t, job in enumerate(jobs):
        getattr(job, phase)(srcs[t], dsts[t], *sems, t)


def _job_scratch(n):
    return [pltpu.SemaphoreType.DMA((JOB_SEMS * n,)), pltpu.SemaphoreType.DMA((JOB_SEMS * n,)),
            pltpu.SemaphoreType.DMA((n,))]


def _pcall(body, *, name, grid, in_specs, out_specs, out_shape, sem, args, scratch_shapes=(), jobs=()):
    in_specs, out_specs, out_shape = list(in_specs), list(out_specs), list(out_shape)
    scratch_shapes = list(scratch_shapes)
    if not jobs:
        res = pl.pallas_call(body, name=name, grid=grid, in_specs=in_specs, out_specs=out_specs, out_shape=out_shape,
                             scratch_shapes=scratch_shapes, compiler_params=_cp(*sem))(*args)
        return list(res), []
    n_in, n_out, n_scr, nj = len(args), len(out_shape), len(scratch_shapes), len(jobs)
    n_steps = math.prod(grid)
    relay_step = (3 * n_steps) // 4
    relay_early = 0 < relay_step < n_steps - 1

    def wrapped(*refs):
        ins, refs = refs[:n_in], refs[n_in:]
        jin, refs = refs[:nj], refs[nj:]
        outs, refs = refs[:n_out], refs[n_out:]
        jout, refs = refs[:nj], refs[nj:]
        scr, sems = refs[:n_scr], refs[n_scr:]
        step = pl.program_id(0)
        for d in range(1, len(grid)):
            step = step * grid[d] + pl.program_id(d)

        @pl.when(step == 0)
        def _():
            _run_jobs("start", jobs, jin, jout, sems)

        if relay_early:
            @pl.when(step == relay_step)
            def _():
                _run_jobs("relay", jobs, jin, jout, sems)
        body(*ins, *outs, *scr)

        @pl.when(step == n_steps - 1)
        def _():
            if not relay_early:
                _run_jobs("relay", jobs, jin, jout, sems)
            _run_jobs("finish", jobs, jin, jout, sems)

    hbm = pl.BlockSpec(memory_space=pltpu.HBM)
    res = pl.pallas_call(
        wrapped, name=name, grid=grid, in_specs=in_specs + [hbm] * nj, out_specs=out_specs + [hbm] * nj,
        out_shape=out_shape + [job.out_shape for job in jobs], scratch_shapes=scratch_shapes + _job_scratch(nj),
        compiler_params=_cp(*sem))(*args, *[job.src for job in jobs])
    return list(res[:n_out]), list(res[n_out:])


def _hidden_chunks(k):
    return [(c0, min(6 * MXU_DIM, k - c0)) for c0 in range(0, k, 6 * MXU_DIM)]


def _loss_head(xv, gain, tgt, st_ref):
    dm = xv.shape[-1]
    rstd = lax.rsqrt(jnp.mean(xv * xv, axis=-1, keepdims=True) + EPS)
    xn = xv * rstd
    err = xn * gain - tgt
    st_ref[1:2, :] += jnp.full((1, dm), 0.5 / dm, F32) * jnp.sum(err * err)
    dy = err * (1.0 / dm)
    st_ref[0:1, :] += jnp.sum(dy * xn, axis=0, keepdims=True)
    dxn = dy * gain
    return rstd * (dxn - xn * jnp.mean(dxn * xn, axis=-1, keepdims=True))


def _ffn_fwd(x, nrm, w_gu, w_down, name, jobs=(), head=None):
    S, Dm = x.shape
    K = w_down.shape[0]
    tm = _tile(S, 256)

    def body(x_ref, nrm_ref, wgu_ref, wdn_ref, *rest):
        if head is None:
            o_ref, h_ref, gu_ref, y_ref = rest
        else:
            gf_ref, t_ref, o_ref, h_ref, gu_ref, y_ref, st_ref = rest

            @pl.when(pl.program_id(0) == 0)
            def _():
                st_ref[...] = jnp.zeros_like(st_ref)
        xv = x_ref[...]
        rstd = lax.rsqrt(jnp.mean(xv * xv, axis=-1, keepdims=True) + EPS)
        hn = (xv * rstd) * nrm_ref[0:1, :]
        hb = (hn * (1.0 + nrm_ref[2:3, :]) + nrm_ref[1:2, :]).astype(BF16)
        h_ref[...] = hb
        y = jnp.zeros((tm, Dm), F32)
        for c0, cs in _hidden_chunks(K):
            g = _dot(hb, wgu_ref[:, c0:c0 + cs])
            u = _dot(hb, wgu_ref[:, K + c0:K + c0 + cs])
            gu_ref[:, c0:c0 + cs] = g.astype(BF16)
            gu_ref[:, K + c0:K + c0 + cs] = u.astype(BF16)
            y = y + _dot((g * _sigmoid(g) * u).astype(BF16), wdn_ref[c0:c0 + cs, :])
        xo = xv + (0.5 * nrm_ref[3:4, :]) * y
        o_ref[...] = xo if head is None else _loss_head(xo, gf_ref[0:1, :], t_ref[...], st_ref)
        y_ref[...] = y.astype(BF16)

    row = lambda i: (i, 0)
    fix = lambda i: (0, 0)
    in_specs = [pl.BlockSpec((tm, Dm), row), pl.BlockSpec((8, Dm), fix), _resident((Dm, 2 * K)), _resident((K, Dm))]
    out_specs = [pl.BlockSpec((tm, Dm), row), pl.BlockSpec((tm, Dm), row), pl.BlockSpec((tm, 2 * K), row),
                 pl.BlockSpec((tm, Dm), row)]
    out_shape = [jax.ShapeDtypeStruct((S, Dm), F32), jax.ShapeDtypeStruct((S, Dm), BF16),
                 jax.ShapeDtypeStruct((S, 2 * K), BF16), jax.ShapeDtypeStruct((S, Dm), BF16)]
    args = (x, nrm, w_gu, w_down)
    if head is not None:
        in_specs += [pl.BlockSpec((8, Dm), fix), pl.BlockSpec((tm, Dm), row)]
        out_specs.append(pl.BlockSpec((8, Dm), fix))
        out_shape.append(jax.ShapeDtypeStruct((8, Dm), F32))
        args += tuple(head)
    return _pcall(body, name=name, grid=(S // tm,), in_specs=in_specs, out_specs=out_specs, out_shape=out_shape,
                  sem=("arbitrary",), args=args, jobs=jobs)


def _ffn_down_bwd(dxo, gu, y, w, nrm, name, jobs=()):
    S, Dm = dxo.shape
    K = w.shape[0]
    Ka = gu.shape[1]
    coef = 0.5
    tm = _tile(S, 256)
    n_steps = S // tm
    chunks = _hidden_chunks(K)

    def body(dxo_ref, y_ref, w_ref, nrm_ref, a_ref, da_ref, dgate_ref, dw_ref, acc):
        @pl.when(pl.program_id(0) == 0)
        def _():
            dgate_ref[...] = jnp.zeros_like(dgate_ref)
            acc[...] = jnp.zeros_like(acc)

        dxo_v = dxo_ref[...]
        dyb = ((coef * nrm_ref[3:4, :]) * dxo_v).astype(BF16)
        dgate_ref[0:1, :] += jnp.sum(coef * y_ref[...].astype(F32) * dxo_v, axis=0, keepdims=True)
        for c0, cs in chunks:
            dact = _dot_nt(dyb, w_ref[c0:c0 + cs, :])
            g = a_ref[:, c0:c0 + cs].astype(F32)
            u = a_ref[:, K + c0:K + c0 + cs].astype(F32)
            s = _sigmoid(g)
            si = g * s
            da_ref[:, c0:c0 + cs] = (dact * u * (s * (1.0 + g * (1.0 - s)))).astype(BF16)
            da_ref[:, K + c0:K + c0 + cs] = (dact * si).astype(BF16)
            acc[c0:c0 + cs, :] += _dot_tn((si * u).astype(BF16), dyb)

        @pl.when(pl.program_id(0) == n_steps - 1)
        def _():
            dw_ref[...] = acc[...].astype(BF16)

    row = lambda i: (i, 0)
    fix = lambda i: (0, 0)
    return _pcall(
        body, name=name, grid=(n_steps,),
        in_specs=[pl.BlockSpec((tm, Dm), row), pl.BlockSpec((tm, Dm), row), _resident((K, Dm)),
                  pl.BlockSpec((8, Dm), fix), pl.BlockSpec((tm, Ka), row)],
        out_specs=[pl.BlockSpec((tm, Ka), row), pl.BlockSpec((8, Dm), fix), _resident((K, Dm))],
        out_shape=[jax.ShapeDtypeStruct((S, Ka), BF16), jax.ShapeDtypeStruct((8, Dm), F32),
                   jax.ShapeDtypeStruct((K, Dm), BF16)],
        scratch_shapes=[pltpu.VMEM((K, Dm), F32)],
        sem=("arbitrary",), args=(dxo, y, w, nrm, gu), jobs=jobs)


def _atb(a, b, out_dtype, bm, bn, name, jobs=()):
    S, M = a.shape
    N = b.shape[1]
    bk = _tile(S, 1024)
    nk = S // bk

    def body(a_ref, b_ref, o_ref, acc):
        k = pl.program_id(2)

        @pl.when(k == 0)
        def _():
            acc[...] = jnp.zeros_like(acc)
        acc[...] += _dot_tn(a_ref[...], b_ref[...])

        @pl.when(k == nk - 1)
        def _():
            o_ref[...] = acc[...].astype(o_ref.dtype)

    (out,), extra = _pcall(
        body, name=name, grid=(M // bm, N // bn, nk),
        in_specs=[pl.BlockSpec((bk, bm), lambda m, n, k: (k, m)),
                  pl.BlockSpec((bk, bn), lambda m, n, k: (k, n))],
        out_specs=[pl.BlockSpec((bm, bn), lambda m, n, k: (m, n))],
        out_shape=[jax.ShapeDtypeStruct((M, N), out_dtype)],
        scratch_shapes=[pltpu.VMEM((bm, bn), F32)],
        sem=("arbitrary", "arbitrary", "arbitrary"), args=(a, b), jobs=jobs)
    return out, extra


def _norm_bwd(dh, xv, nrm_ref, red_ref):
    rstd = lax.rsqrt(jnp.mean(xv * xv, axis=-1, keepdims=True) + EPS)
    xn = xv * rstd
    gain = nrm_ref[0:1, :]
    hn = xn * gain
    dhn = dh * (1.0 + nrm_ref[2:3, :])
    red_ref[0:1, :] += jnp.sum(dh, axis=0, keepdims=True)
    red_ref[1:2, :] += jnp.sum(dh * hn, axis=0, keepdims=True)
    red_ref[2:3, :] += jnp.sum(dhn * xn, axis=0, keepdims=True)
    dxn = dhn * gain
    return rstd * (dxn - xn * jnp.mean(dxn * xn, axis=-1, keepdims=True))


def _nt_norm_bwd(dout, w, x, nrm, dxo, name, jobs=()):
    S, N = dout.shape
    Dm = w.shape[0]
    tm = _tile(S, 512)

    def body(do_ref, w_ref, x_ref, nrm_ref, dxo_ref, dx_ref, red_ref):
        @pl.when(pl.program_id(0) == 0)
        def _():
            red_ref[...] = jnp.zeros_like(red_ref)
        dh = _dot_nt(do_ref[...], w_ref[...])
        dx_ref[...] = dxo_ref[...] + _norm_bwd(dh, x_ref[...], nrm_ref, red_ref)

    return _pcall(
        body, name=name, grid=(S // tm,),
        in_specs=[pl.BlockSpec((tm, N), lambda i: (i, 0)),
                  _resident((Dm, N)),
                  pl.BlockSpec((tm, Dm), lambda i: (i, 0)),
                  pl.BlockSpec((8, Dm), lambda i: (0, 0)),
                  pl.BlockSpec((tm, Dm), lambda i: (i, 0))],
        out_specs=[pl.BlockSpec((tm, Dm), lambda i: (i, 0)),
                   pl.BlockSpec((8, Dm), lambda i: (0, 0))],
        out_shape=[jax.ShapeDtypeStruct((S, Dm), F32), jax.ShapeDtypeStruct((8, Dm), F32)],
        sem=("arbitrary",), args=(dout, w, x, nrm, dxo), jobs=jobs)


def _alibi_slope(h):
    return float(2.0 ** (-8.0 * (h + 1) / N_Q_HEADS))


def _head_planes(pair_cols):
    lane = lax.broadcasted_iota(jnp.int32, pair_cols.shape, 1)
    low = lane < HEAD_DIM
    h0_lo = jnp.where(low, pair_cols, 0.0)
    h1_hi = jnp.where(low, 0.0, pair_cols)
    h0_hi = pltpu.roll(h0_lo, HEAD_DIM, 1)
    h1_lo = pltpu.roll(h1_hi, HEAD_DIM, 1)
    return ((h0_lo.astype(BF16), h0_hi.astype(BF16)), (h1_lo.astype(BF16), h1_hi.astype(BF16)))


def _band_geometry(first_block):
    qi = lax.broadcasted_iota(jnp.int32, (BLOCK, BLOCK), 0)
    kj = lax.broadcasted_iota(jnp.int32, (BLOCK, BLOCK), 1)
    own = kj <= qi
    dist = jnp.where(own, qi - kj, qi + BLOCK - kj).astype(F32)
    valid = kj <= qi + BLOCK * (1 - first_block)
    return own, dist, valid


def _fold(band, own):
    return jnp.where(own, band[:, BLOCK:], band[:, :BLOCK])


def _unfold(v, own):
    return jnp.concatenate([jnp.where(own, 0.0, v), jnp.where(own, v, 0.0)], axis=1)


def _softmax_band(s, h, geometry, sink):
    own, dist, valid = geometry
    s = jnp.where(valid, s - _alibi_slope(h) * dist, NEG_INF)
    m = jnp.maximum(jnp.max(s, axis=-1, keepdims=True), sink)
    p = jnp.exp(s - m)
    e_sink = jnp.exp(sink - m)
    inv = 1.0 / (jnp.sum(p, axis=-1, keepdims=True) + e_sink)
    return p * inv, e_sink * inv


def _past(cur, prev, s, row):
    return jnp.where(row < s, pltpu.roll(prev, s, 0), pltpu.roll(cur, s, 0))


def _future(cur, nxt, s, row):
    T = cur.shape[0]
    return jnp.where(row >= T - s, pltpu.roll(nxt, T - s, 0), pltpu.roll(cur, T - s, 0))


def _edge_row(v, last):
    T = v.shape[0]
    r8 = lax.broadcasted_iota(jnp.int32, (SUBLANES, v.shape[1]), 0)
    blk = v[T - SUBLANES:, :] if last else v[:SUBLANES, :]
    return jnp.sum(jnp.where(r8 == (SUBLANES - 1 if last else 0), blk, 0.0), axis=0, keepdims=True)


def _lru_gates(lx, lx_prev, small_ref, wa_ref, wx_ref, row, t0):
    xc = (small_ref[4:5, :] + small_ref[3:4, :] * lx + small_ref[2:3, :] * _past(lx, lx_prev, 1, row)
          + small_ref[1:2, :] * _past(lx, lx_prev, 2, row) + small_ref[0:1, :] * _past(lx, lx_prev, 3, row))
    xcb = xc.astype(BF16)
    r = _sigmoid(_dot(xcb, wa_ref[...]) + small_ref[5:6, :])
    ig = _sigmoid(_dot(xcb, wx_ref[...]) + small_ref[6:7, :])
    sp = _softplus_neg(small_ref[7:8, :])
    la = (-LRU_C) * r * sp
    a = jnp.exp(la)
    first = (row + t0) == 0
    mult = jnp.where(first, 1.0, jnp.sqrt(-_expm1(2.0 * la)))
    return xc, xcb, r, ig, sp, a, mult, first


def _mixer_fwd(x, nrm, w_in, w_out, sinks, small, wa, wx, name, jobs=()):
    S, Dm = x.shape
    T = MIX_TILE
    nT = S // T
    nb = T // BLOCK

    def body(x_ref, nrm_ref, w_in_ref, w_out_ref, sink_ref, small_ref, wa_ref, wx_ref,
             xo_ref, h_ref, proj_ref, y_ref, ymo_ref, hp_ref, *carried_state):
        xv = x_ref[...]
        rstd = lax.rsqrt(jnp.mean(xv * xv, axis=-1, keepdims=True) + EPS)
        hn = (xv * rstd) * nrm_ref[0:1, :]
        hb = (hn * (1.0 + nrm_ref[2:3, :]) + nrm_ref[1:2, :]).astype(BF16)
        h_ref[...] = hb
        proj_ref[...] = _dot(hb, w_in_ref[...])
        core(proj_ref, sink_ref, small_ref, wa_ref, wx_ref, y_ref, hp_ref, *carried_state)
        yo = _dot(y_ref[...], w_out_ref[...])
        xo_ref[...] = xv + nrm_ref[3:4, :] * yo
        ymo_ref[...] = yo.astype(BF16)

    def core(proj_ref, sink_ref, small_ref, wa_ref, wx_ref, y_ref, hp_ref, kvp, lxp, zp, hcar):
        i = pl.program_id(0)

        @pl.when(i == 0)
        def _():
            kvp[...] = jnp.zeros_like(kvp)
            lxp[...] = jnp.zeros_like(lxp)
            zp[...] = jnp.zeros_like(zp)
            hcar[...] = jnp.zeros_like(hcar)

        row = lax.broadcasted_iota(jnp.int32, (T, LRU_WIDTH), 0)

        kv = proj_ref[:, C_KV:C_KV + 2 * KV_WIDTH]
        ext = jnp.concatenate([kvp[...], kv], axis=0)
        kx = _head_planes(ext[:, :KV_WIDTH])
        vx = _head_planes(ext[:, KV_WIDTH:])
        first_tile = jnp.where(i == 0, 1, 0)
        units = [(b, pair, e) for b in range(nb) for pair in range(N_Q_HEADS // 2) for e in range(2)]
        geometry = [_band_geometry(first_tile if b == 0 else 0) for b in range(nb)]
        keys = [slice(b * BLOCK, (b + 2) * BLOCK) for b in range(nb)]
        qp = {(b, pair): (proj_ref[b * BLOCK:(b + 1) * BLOCK, pair * LANES:(pair + 1) * LANES] * 0.125).astype(BF16)
              for b in range(nb) for pair in range(N_Q_HEADS // 2)}
        scores = [_fold(_dot_nt(qp[(b, pair)], kx[pair // 2][e][keys[b]]), geometry[b][0]) for b, pair, e in units]
        probs = [_unfold(_softmax_band(s, 2 * pair + e, geometry[b], sink_ref[2 * pair + e])[0],
                         geometry[b][0]).astype(BF16) for s, (b, pair, e) in zip(scores, units)]
        outs = [_dot(p, vx[pair // 2][e][keys[b]]) for p, (b, pair, e) in zip(probs, units)]
        for u in range(0, len(units), 2):
            b, pair, _ = units[u]
            y_ref[b * BLOCK:(b + 1) * BLOCK, pair * LANES:(pair + 1) * LANES] = (outs[u] + outs[u + 1]).astype(BF16)
        kvp[...] = kv[T - BLOCK:, :]

        lx = proj_ref[:, C_LX:C_LX + LRU_WIDTH]
        xc, _, _, ig, _, a, mult, _ = _lru_gates(lx, lxp[...], small_ref, wa_ref, wx_ref, row, i * T)
        lxp[...] = lx
        aa = a
        bb = mult * (ig * xc)
        s = 1
        while s < T:
            a_sh = jnp.where(row >= s, pltpu.roll(aa, s, 0), 1.0)
            b_sh = jnp.where(row >= s, pltpu.roll(bb, s, 0), 0.0)
            bb = aa * b_sh + bb
            aa = aa * a_sh
            s *= 2
        hc = hcar[0:1, :]
        hh = bb + aa * hc
        hp_ref[...] = jnp.where(row < 1, hc, pltpu.roll(hh, 1, 0))
        hcar[...] = jnp.broadcast_to(_edge_row(hh, True), hcar.shape)
        gl, _ = _gelu(proj_ref[:, C_LG:C_LG + LRU_WIDTH])
        y_ref[:, ATTN_WIDTH:ATTN_WIDTH + LRU_WIDTH] = (gl * hh).astype(BF16)

        z = proj_ref[:, C_SC:C_SC + CONV_WIDTH] * proj_ref[:, C_SX:C_SX + CONV_WIDTH]
        c3 = (small_ref[10:11, :] * z + small_ref[9:10, :] * _past(z, zp[...], 1, row)
              + small_ref[8:9, :] * _past(z, zp[...], 2, row))
        zp[...] = z
        y_ref[:, ATTN_WIDTH + LRU_WIDTH:] = (proj_ref[:, C_SB:C_SB + CONV_WIDTH] * c3).astype(BF16)

    fix = lambda i: (0, 0)
    row = lambda i: (i, 0)
    return _pcall(
        body, name=name, grid=(nT,),
        in_specs=[pl.BlockSpec((T, Dm), row), pl.BlockSpec((8, Dm), fix),
                  _resident((Dm, IN_PROJ_WIDTH)), _resident((D_MODEL, Dm)),
                  pl.BlockSpec(memory_space=pltpu.SMEM),
                  pl.BlockSpec((16, LRU_WIDTH), fix),
                  pl.BlockSpec((LRU_WIDTH, LRU_WIDTH), fix),
                  pl.BlockSpec((LRU_WIDTH, LRU_WIDTH), fix)],
        out_specs=[pl.BlockSpec((T, Dm), row), pl.BlockSpec((T, Dm), row), pl.BlockSpec((T, IN_PROJ_WIDTH), row),
                   pl.BlockSpec((T, D_MODEL), row), pl.BlockSpec((T, Dm), row), pl.BlockSpec((T, LRU_WIDTH), row)],
        out_shape=[jax.ShapeDtypeStruct((S, Dm), F32), jax.ShapeDtypeStruct((S, Dm), BF16),
                   jax.ShapeDtypeStruct((S, IN_PROJ_WIDTH), F32), jax.ShapeDtypeStruct((S, D_MODEL), BF16),
                   jax.ShapeDtypeStruct((S, Dm), BF16), jax.ShapeDtypeStruct((S, LRU_WIDTH), F32)],
        scratch_shapes=[pltpu.VMEM((BLOCK, 2 * KV_WIDTH), F32), pltpu.VMEM((T, LRU_WIDTH), F32),
                        pltpu.VMEM((T, CONV_WIDTH), F32), pltpu.VMEM((SUBLANES, LRU_WIDTH), F32)],
        sem=("arbitrary",), args=(x, nrm, w_in, w_out, sinks, small, wa, wx), jobs=jobs)


def _mixer_bwd(x, nrm, dxo, h, proj, ymix, ymo, hprev, w_in, w_out, sinks, small, wa, wx, name, jobs=()):
    S, Dm = x.shape
    T = MIX_TILE
    nT = S // T
    nb = T // BLOCK
    bpt = T // BLOCK

    def body(x_ref, nrm_ref, dxo_ref, h_ref, proj_ref, kvprev_ref, lxprev_ref, scprev_ref, sxprev_ref, ymix_ref,
             ymo_ref, hp_ref, w_in_ref, w_out_ref, sink_ref, small_ref, wa_ref, wx_ref,
             dx_ref, red_ref, dgate_ref, dwo_ref, dwi_ref, dsm_ref, dsink_ref, dwa_ref, dwx_ref,
             dy_s, dp_s, acc_o, acc_i, *carried_state):
        @pl.when(pl.program_id(0) == 0)
        def _():
            for r in (red_ref, dgate_ref, acc_o, acc_i):
                r[...] = jnp.zeros_like(r)

        dxo_v = dxo_ref[...]
        dyb = (nrm_ref[3:4, :] * dxo_v).astype(BF16)
        dgate_ref[0:1, :] += jnp.sum(ymo_ref[...].astype(F32) * dxo_v, axis=0, keepdims=True)
        dy_s[...] = _dot_nt(dyb, w_out_ref[...])
        acc_o[...] += _dot_tn(ymix_ref[...], dyb)
        core(proj_ref, kvprev_ref, lxprev_ref, scprev_ref, sxprev_ref, dy_s, hp_ref, sink_ref, small_ref,
             wa_ref, wx_ref, dp_s, dsm_ref, dsink_ref, dwa_ref, dwx_ref, *carried_state)
        dpb = dp_s[...]
        acc_i[...] += _dot_tn(h_ref[...], dpb)
        dx_ref[...] = dxo_v + _norm_bwd(_dot_nt(dpb, w_in_ref[...]), x_ref[...], nrm_ref, red_ref)

        @pl.when(pl.program_id(0) == nT - 1)
        def _():
            dwo_ref[...] = acc_o[...].astype(BF16)
            dwi_ref[...] = acc_i[...].astype(BF16)

    def core(proj_ref, kvprev_ref, lxprev_ref, scprev_ref, sxprev_ref, dy_ref, hp_ref, sink_ref, small_ref,
             wa_ref, wx_ref, dp_ref, dsm_ref, dsink_ref, dwa_ref, dwx_ref,
             dk_s, dv_s, dkv_c, dxc_n, dc3_n, p_c):
        i = pl.program_id(0)
        ti = nT - 1 - i
        has_prev = jnp.where(ti == 0, 0.0, 1.0)

        @pl.when(i == 0)
        def _():
            for r in (dkv_c, dxc_n, dc3_n, p_c, dsm_ref, dsink_ref, dwa_ref, dwx_ref):
                r[...] = jnp.zeros_like(r)

        row = lax.broadcasted_iota(jnp.int32, (T, LRU_WIDTH), 0)

        kv = proj_ref[:, C_KV:C_KV + 2 * KV_WIDTH]
        ext = jnp.concatenate([kvprev_ref[...] * has_prev, kv], axis=0)
        kx = _head_planes(ext[:, :KV_WIDTH])
        vx = _head_planes(ext[:, KV_WIDTH:])
        dk_s[...] = jnp.zeros_like(dk_s)
        dv_s[...] = jnp.zeros_like(dv_s)
        dk_s[:, T:] = dkv_c[:, :BLOCK]
        dv_s[:, T:] = dkv_c[:, BLOCK:]
        first_tile = jnp.where(ti == 0, 1, 0)
        for b in range(nb):
            units = [(pair, e) for pair in range(N_Q_HEADS // 2) for e in range(2)]
            geometry = _band_geometry(first_tile if b == 0 else 0)
            own = geometry[0]
            keys = slice(b * BLOCK, (b + 2) * BLOCK)
            tile = {pair: (slice(b * BLOCK, (b + 1) * BLOCK), slice(pair * LANES, (pair + 1) * LANES))
                    for pair in range(N_Q_HEADS // 2)}
            qp = {k: (proj_ref[rc] * 0.125).astype(BF16) for k, rc in tile.items()}
            dob = {k: dy_ref[rc].astype(BF16) for k, rc in tile.items()}
            qp_t = {k: jnp.transpose(proj_ref[rc] * 0.125).astype(BF16) for k, rc in tile.items()}
            dob_t = {k: jnp.transpose(dy_ref[rc]).astype(BF16) for k, rc in tile.items()}
            scores = [_fold(_dot_nt(qp[pair], kx[pair // 2][e][keys]), own) for pair, e in units]
            dprob = [_fold(_dot_nt(dob[pair], vx[pair // 2][e][keys]), own) for pair, e in units]
            pn_wide, ds_wide = [], []
            for s, dpm, (pair, e) in zip(scores, dprob, units):
                h = 2 * pair + e
                pn, psink = _softmax_band(s, h, geometry, sink_ref[h])
                dsum = jnp.sum(pn * dpm, axis=-1, keepdims=True)
                dsink_ref[h:h + 1, :] += jnp.full((1, LANES), -1.0, F32) * jnp.sum(psink * dsum)
                pn_wide.append(_unfold(pn, own).astype(BF16))
                ds_wide.append(_unfold(pn * (dpm - dsum), own).astype(BF16))
            dq = {}
            for pw, ds, (pair, e) in zip(pn_wide, ds_wide, units):
                g = pair // 2
                head_e = slice(e * HEAD_DIM, (e + 1) * HEAD_DIM)
                head_g = slice(g * HEAD_DIM, (g + 1) * HEAD_DIM)
                dv_s[head_g, keys] += _dot(dob_t[pair], pw)[head_e, :]
                dk_s[head_g, keys] += _dot(qp_t[pair], ds)[head_e, :]
                part = _dot(ds, kx[g][e][keys])
                dq[pair] = part if e == 0 else dq[pair] + part
            for k, rc in tile.items():
                dp_ref[rc] = (0.125 * dq[k]).astype(BF16)
        dp_ref[:, C_KV:C_KV + KV_WIDTH] = jnp.transpose(dk_s[:, BLOCK:]).astype(BF16)
        dp_ref[:, C_KV + KV_WIDTH:C_KV + 2 * KV_WIDTH] = jnp.transpose(dv_s[:, BLOCK:]).astype(BF16)
        dkv_c[:, :BLOCK] = dk_s[:, :BLOCK]
        dkv_c[:, BLOCK:] = dv_s[:, :BLOCK]

        lx = proj_ref[:, C_LX:C_LX + LRU_WIDTH]
        lxprev = lxprev_ref[...] * has_prev
        xc, xcb, r, ig, sp, a, mult, first = _lru_gates(lx, lxprev, small_ref, wa_ref, wx_ref, row, ti * T)
        hp = hp_ref[...]
        hh = a * hp + mult * (ig * xc)
        lg = proj_ref[:, C_LG:C_LG + LRU_WIDTH]
        gl, th = _gelu(lg)
        dyl = dy_ref[:, ATTN_WIDTH:ATTN_WIDTH + LRU_WIDTH]
        dp_ref[:, C_LG:C_LG + LRU_WIDTH] = (dyl * hh * _gelu_grad(lg, th)).astype(BF16)
        aa = jnp.where(row < T - 1, pltpu.roll(a, T - 1, 0), 1.0)
        bb = dyl * gl
        s = 1
        while s < T:
            a_sh = jnp.where(row < T - s, pltpu.roll(aa, T - s, 0), 1.0)
            b_sh = jnp.where(row < T - s, pltpu.roll(bb, T - s, 0), 0.0)
            bb = bb + aa * b_sh
            aa = aa * a_sh
            s *= 2
        G = bb + aa * p_c[0:1, :]
        p_c[...] = jnp.broadcast_to(_edge_row(a * G, False), p_c.shape)
        da = G * hp
        dmult = G * (ig * xc)
        dig = G * mult * xc
        dxc = G * mult * ig
        dla = da * a + dmult * jnp.where(first, 0.0, -(a * a) / mult)
        dr = dla * ((-LRU_C) * sp)
        lam = small_ref[7:8, :]
        dsm_ref[7:8, :] += jnp.sum(dla * ((-LRU_C) * r), axis=0, keepdims=True) * (-_sigmoid(-lam))
        dpa = dr * r * (1.0 - r)
        dpx = dig * ig * (1.0 - ig)
        dsm_ref[5:6, :] += jnp.sum(dpa, axis=0, keepdims=True)
        dsm_ref[6:7, :] += jnp.sum(dpx, axis=0, keepdims=True)
        dpab = dpa.astype(BF16)
        dpxb = dpx.astype(BF16)
        dwa_ref[...] += _dot_tn(xcb, dpab)
        dwx_ref[...] += _dot_tn(xcb, dpxb)
        dxc = dxc + _dot_nt(dpab, wa_ref[...]) + _dot_nt(dpxb, wx_ref[...])
        dsm_ref[4:5, :] += jnp.sum(dxc, axis=0, keepdims=True)
        dsm_ref[3:4, :] += jnp.sum(dxc * lx, axis=0, keepdims=True)
        for k in range(3):
            dsm_ref[k:k + 1, :] += jnp.sum(dxc * _past(lx, lxprev, 3 - k, row), axis=0, keepdims=True)
        nxt = dxc_n[...]
        dlx = (small_ref[3:4, :] * dxc + small_ref[2:3, :] * _future(dxc, nxt, 1, row)
               + small_ref[1:2, :] * _future(dxc, nxt, 2, row) + small_ref[0:1, :] * _future(dxc, nxt, 3, row))
        dxc_n[...] = dxc
        dp_ref[:, C_LX:C_LX + LRU_WIDTH] = dlx.astype(BF16)

        sc = proj_ref[:, C_SC:C_SC + CONV_WIDTH]
        sx = proj_ref[:, C_SX:C_SX + CONV_WIDTH]
        sb = proj_ref[:, C_SB:C_SB + CONV_WIDTH]
        z = sc * sx
        zprev = (scprev_ref[...] * sxprev_ref[...]) * has_prev
        z1 = _past(z, zprev, 1, row)
        z2 = _past(z, zprev, 2, row)
        c3 = small_ref[10:11, :] * z + small_ref[9:10, :] * z1 + small_ref[8:9, :] * z2
        dys = dy_ref[:, ATTN_WIDTH + LRU_WIDTH:]
        dp_ref[:, C_SB:C_SB + CONV_WIDTH] = (dys * c3).astype(BF16)
        dc3 = dys * sb
        dsm_ref[10:11, :] += jnp.sum(dc3 * z, axis=0, keepdims=True)
        dsm_ref[9:10, :] += jnp.sum(dc3 * z1, axis=0, keepdims=True)
        dsm_ref[8:9, :] += jnp.sum(dc3 * z2, axis=0, keepdims=True)
        nxt3 = dc3_n[...]
        dz = (small_ref[10:11, :] * dc3 + small_ref[9:10, :] * _future(dc3, nxt3, 1, row)
              + small_ref[8:9, :] * _future(dc3, nxt3, 2, row))
        dc3_n[...] = dc3
        dp_ref[:, C_SC:C_SC + CONV_WIDTH] = (dz * sx).astype(BF16)
        dp_ref[:, C_SX:C_SX + CONV_WIDTH] = (dz * sc).astype(BF16)

    fix = lambda i: (0, 0)
    cur = lambda i: (nT - 1 - i, 0)
    prev_cols = lambda cb: (lambda i: (jnp.maximum(nT - 2 - i, 0), cb))
    return _pcall(
        body, name=name, grid=(nT,),
        in_specs=[pl.BlockSpec((T, Dm), cur), pl.BlockSpec((8, Dm), fix), pl.BlockSpec((T, Dm), cur),
                  pl.BlockSpec((T, Dm), cur),
                  pl.BlockSpec((T, IN_PROJ_WIDTH), cur),
                  pl.BlockSpec((BLOCK, 2 * KV_WIDTH),
                               lambda i: (jnp.maximum((nT - 1 - i) * bpt - 1, 0), C_KV // (2 * KV_WIDTH))),
                  pl.BlockSpec((T, LRU_WIDTH), prev_cols(C_LX // LRU_WIDTH)),
                  pl.BlockSpec((T, CONV_WIDTH), prev_cols(C_SC // CONV_WIDTH)),
                  pl.BlockSpec((T, CONV_WIDTH), prev_cols(C_SX // CONV_WIDTH)),
                  pl.BlockSpec((T, D_MODEL), cur), pl.BlockSpec((T, Dm), cur),
                  pl.BlockSpec((T, LRU_WIDTH), cur),
                  _resident((Dm, IN_PROJ_WIDTH)), _resident((D_MODEL, Dm)),
                  pl.BlockSpec(memory_space=pltpu.SMEM),
                  pl.BlockSpec((16, LRU_WIDTH), fix),
                  pl.BlockSpec((LRU_WIDTH, LRU_WIDTH), fix),
                  pl.BlockSpec((LRU_WIDTH, LRU_WIDTH), fix)],
        out_specs=[pl.BlockSpec((T, Dm), cur), pl.BlockSpec((8, Dm), fix), pl.BlockSpec((8, Dm), fix),
                   _resident((D_MODEL, Dm)), _resident((Dm, IN_PROJ_WIDTH)),
                   pl.BlockSpec((16, LRU_WIDTH), fix),
                   pl.BlockSpec((SUBLANES, LANES), fix),
                   pl.BlockSpec((LRU_WIDTH, LRU_WIDTH), fix),
                   pl.BlockSpec((LRU_WIDTH, LRU_WIDTH), fix)],
        out_shape=[jax.ShapeDtypeStruct((S, Dm), F32), jax.ShapeDtypeStruct((8, Dm), F32),
                   jax.ShapeDtypeStruct((8, Dm), F32),
                   jax.ShapeDtypeStruct((D_MODEL, Dm), BF16), jax.ShapeDtypeStruct((Dm, IN_PROJ_WIDTH), BF16),
                   jax.ShapeDtypeStruct((16, LRU_WIDTH), F32),
                   jax.ShapeDtypeStruct((SUBLANES, LANES), F32),
                   jax.ShapeDtypeStruct((LRU_WIDTH, LRU_WIDTH), F32),
                   jax.ShapeDtypeStruct((LRU_WIDTH, LRU_WIDTH), F32)],
        scratch_shapes=[pltpu.VMEM((T, D_MODEL), F32), pltpu.VMEM((T, IN_PROJ_WIDTH), BF16),
                        pltpu.VMEM((D_MODEL, Dm), F32), pltpu.VMEM((Dm, IN_PROJ_WIDTH), F32),
                        pltpu.VMEM((KV_WIDTH, T + BLOCK), F32), pltpu.VMEM((KV_WIDTH, T + BLOCK), F32),
                        pltpu.VMEM((BLOCK, 2 * KV_WIDTH), F32), pltpu.VMEM((T, LRU_WIDTH), F32),
                        pltpu.VMEM((T, CONV_WIDTH), F32), pltpu.VMEM((SUBLANES, LRU_WIDTH), F32)],
        sem=("arbitrary",),
        args=(x, nrm, dxo, h, proj, proj, proj, proj, proj, ymix, ymo, hprev, w_in, w_out, sinks, small, wa, wx),
        jobs=jobs)


def _adamw_update(g, w_ref, m_ref, v_ref, go_ref, d_ref, mo_ref, vo_ref):
    mn = ADAM_B1 * m_ref[...] + (1.0 - ADAM_B1) * g
    vn = ADAM_B2 * v_ref[...] + (1.0 - ADAM_B2) * (g * g)
    go_ref[...] = g
    mo_ref[...] = mn
    vo_ref[...] = vn
    m_hat = mn / (1.0 - ADAM_B1 ** ADAM_STEP)
    v_hat = vn / (1.0 - ADAM_B2 ** ADAM_STEP)
    d_ref[...] = (-ADAM_LR) * (m_hat / (jnp.sqrt(v_hat) + ADAM_EPS) + ADAM_WD * w_ref[...])


def _adamw_many(ws, gs, ms, vs, name):
    n = len(ws)

    def body(*refs):
        w_r, g_r, m_r, v_r, go, do, mo, vo = (refs[k * n:(k + 1) * n] for k in range(8))
        for t in range(n):
            _adamw_update(g_r[t][...], w_r[t], m_r[t], v_r[t], go[t], do[t], mo[t], vo[t])

    vmem = pl.BlockSpec(memory_space=pltpu.VMEM)
    res = pl.pallas_call(
        body, name=name, in_specs=[vmem] * (4 * n), out_specs=[vmem] * (4 * n),
        out_shape=[jax.ShapeDtypeStruct(w.shape, F32) for w in ws] * 4,
        compiler_params=pltpu.CompilerParams(vmem_limit_bytes=VMEM_LIMIT),
    )(*ws, *gs, *ms, *vs)
    return [res[k * n:(k + 1) * n] for k in range(4)]


def _adamw_partials(w, partials, m, v, name):
    nl = len(partials)
    _, R, C = partials[0][0].shape
    tr = 8
    for cand in (256, 128, 64, 32, 16):
        if R % cand == 0 and cand * C * 4 <= (1 << 19):
            tr = cand
            break
    ni = R // tr

    def body(*refs):
        w_ref, p_refs = refs[0], refs[1:1 + 2 * nl]
        m_ref, v_ref, go_ref, d_ref, mo_ref, vo_ref = refs[1 + 2 * nl:]
        for l in range(nl):
            @pl.when(pl.program_id(0) == l)
            def _(pair=p_refs[2 * l:2 * l + 2]):
                own, sib = [((p[0].astype(F32) + p[1].astype(F32)) + p[2].astype(F32)) + p[3].astype(F32)
                            for p in pair]
                _adamw_update(own + sib, w_ref, m_ref, v_ref, go_ref, d_ref, mo_ref, vo_ref)

    def slots(l):
        return pl.BlockSpec((N_CHIPS, tr, C),
                            lambda ll, i: (0, jnp.where(ll == l, i, jnp.where(ll < l, 0, ni - 1)), 0))

    spec = pl.BlockSpec((tr, C), lambda ll, i: (ll * ni + i, 0))
    return pl.pallas_call(
        body, name=name, grid=(nl, ni),
        in_specs=[spec] + [slots(l) for l in range(nl) for _ in range(2)] + [spec, spec], out_specs=[spec] * 4,
        out_shape=[jax.ShapeDtypeStruct((nl * R, C), F32)] * 4,
        compiler_params=_cp("arbitrary", "arbitrary"),
    )(w, *[p for pair in partials for p in pair], m, v)


GATHER_SEMS = 7


def _two_level_gather(x_ref, out_ref, send_sems, recv_sems, local_sem, base=0):
    M = x_ref.shape[0]
    x, y, c = lax.axis_index("x"), lax.axis_index("y"), lax.axis_index("c")
    me, sibling = (x, y, c), (x, y, 1 - c)
    chips = [(1 - x, y), (x, 1 - y), (1 - x, 1 - y)]

    def rows(px, py, pc):
        return out_ref.at[pl.ds(pl.multiple_of((4 * px + 2 * py + pc) * M, SUBLANES), M), :]

    def copy(k, block, to, src=None):
        return pltpu.make_async_remote_copy(
            src_ref=rows(*block) if src is None else src, dst_ref=rows(*block),
            send_sem=send_sems.at[base + k], recv_sem=recv_sems.at[base + k], device_id=to, device_id_type=MESH)

    mine = pltpu.make_async_copy(x_ref, rows(*me), local_sem)
    mine.start()
    first = [copy(0, me, sibling, src=x_ref)]
    first += [copy(1 + j, me, (*chip, c), src=x_ref) for j, chip in enumerate(chips)]
    for cp in first:
        cp.start()
    passed = [copy(4 + j, (*chip, c), sibling) for j, chip in enumerate(chips)]
    for j, chip in enumerate(chips):
        copy(1 + j, (*chip, c), me).wait_recv()
        passed[j].start()
    copy(0, sibling, me).wait_recv()
    for j, chip in enumerate(chips):
        copy(4 + j, (*chip, 1 - c), me).wait_recv()
    for cp in first + passed:
        cp.wait_send()
    mine.wait()


def _prologue(pack, w_mod, jobs, name):
    M = pack.shape[0]
    L, Dm, N = w_mod.shape
    nj = len(jobs)
    rows_c = Dm // LANES
    tn = 768

    def body(*refs):
        pack_ref, w_ref = refs[:2]
        jin, refs = refs[2:2 + nj], refs[2 + nj:]
        g_ref, ca_ref, mod_ref = refs[:3]
        jout, refs = refs[3:3 + nj], refs[3 + nj:]
        part_ref, send_sems, recv_sems, local_sem, *jsems = refs
        _run_jobs("start", jobs, jin, jout, jsems)
        _two_level_gather(pack_ref, g_ref, send_sems, recv_sems, local_sem.at[0], 0)
        ca_ref[...] = jnp.zeros_like(ca_ref)
        for r in range(rows_c):
            cv = g_ref[pl.ds(r, N_DEV, stride=M), :]
            ca_ref[0:N_DEV, r * LANES:(r + 1) * LANES] = (cv * _sigmoid(cv)).astype(BF16)
        ca = ca_ref[...]
        for l in range(L):
            for n0 in range(0, N, tn):
                part_ref[l * 16:(l + 1) * 16, n0:n0 + tn] = _dot(ca, w_ref[l, :, n0:n0 + tn].astype(BF16))
        _two_level_gather(part_ref, mod_ref, send_sems, recv_sems, local_sem.at[1], GATHER_SEMS)
        _run_jobs("relay", jobs, jin, jout, jsems)
        _run_jobs("finish", jobs, jin, jout, jsems)

    vmem = pl.BlockSpec(memory_space=pltpu.VMEM)
    hbm = pl.BlockSpec(memory_space=pltpu.HBM)
    res = pl.pallas_call(
        body, name=name,
        out_shape=[jax.ShapeDtypeStruct((N_DEV * M, LANES), F32), jax.ShapeDtypeStruct((16, Dm), BF16),
                   jax.ShapeDtypeStruct((N_DEV * L * 16, N), F32)] + [job.out_shape for job in jobs],
        in_specs=[vmem, vmem] + [hbm] * nj, out_specs=[vmem, vmem, vmem] + [hbm] * nj,
        scratch_shapes=[pltpu.VMEM((L * 16, N), F32), pltpu.SemaphoreType.DMA((2 * GATHER_SEMS,)),
                        pltpu.SemaphoreType.DMA((2 * GATHER_SEMS,)), pltpu.SemaphoreType.DMA((2,))]
        + _job_scratch(nj),
        compiler_params=pltpu.CompilerParams(vmem_limit_bytes=VMEM_LIMIT),
    )(pack, w_mod, *[job.src for job in jobs])
    return res[0], res[1], res[2], list(res[3:])


def _all_gather_small(v, name, jobs=()):
    M, N = v.shape
    nj = len(jobs)

    def body(*refs):
        x_ref, jin = refs[0], refs[1:1 + nj]
        out_ref, sum_ref = refs[1 + nj:3 + nj]
        jout, (send_sems, recv_sems, local_sem, *jsems) = refs[3 + nj:3 + 2 * nj], refs[3 + 2 * nj:]
        _run_jobs("start", jobs, jin, jout, jsems)
        _two_level_gather(x_ref, out_ref, send_sems, recv_sems, local_sem)
        acc = out_ref[0:M, :]
        for d in range(1, N_DEV):
            acc = acc + out_ref[d * M:(d + 1) * M, :]
        sum_ref[...] = acc
        _run_jobs("relay", jobs, jin, jout, jsems)
        _run_jobs("finish", jobs, jin, jout, jsems)

    vmem = pl.BlockSpec(memory_space=pltpu.VMEM)
    hbm = pl.BlockSpec(memory_space=pltpu.HBM)
    res = pl.pallas_call(
        body, name=name,
        out_shape=[jax.ShapeDtypeStruct((N_DEV * M, N), F32), jax.ShapeDtypeStruct((M, N), F32)]
        + [job.out_shape for job in jobs],
        in_specs=[vmem] + [hbm] * nj, out_specs=[vmem, vmem] + [hbm] * nj,
        scratch_shapes=[pltpu.SemaphoreType.DMA((GATHER_SEMS,)), pltpu.SemaphoreType.DMA((GATHER_SEMS,)),
                        pltpu.SemaphoreType.DMA] + (_job_scratch(nj) if nj else []),
        compiler_params=pltpu.CompilerParams(vmem_limit_bytes=VMEM_LIMIT),
    )(v, *[job.src for job in jobs])
    return list(res[:2]), list(res[2:])


def _adamw_w_mod(w, c_act, dmod, m, v, name):
    L, Dm, N = w.shape
    R = c_act.shape[0]
    tr = LANES

    def body(w_ref, c_ref, d_ref, m_ref, v_ref, go_ref, dl_ref, mo_ref, vo_ref):
        g = _dot_tn(c_ref[...], d_ref[0])
        _adamw_update(g, w_ref.at[0], m_ref.at[0], v_ref.at[0], go_ref.at[0], dl_ref.at[0], mo_ref.at[0], vo_ref.at[0])

    spec = pl.BlockSpec((1, tr, N), lambda l, i: (l, i, 0))
    return pl.pallas_call(
        body, name=name, grid=(L, Dm // tr),
        in_specs=[spec, pl.BlockSpec((R, tr), lambda l, i: (0, i)), pl.BlockSpec((1, R, N), lambda l, i: (l, 0, 0)),
                  spec, spec],
        out_specs=[spec] * 4, out_shape=[jax.ShapeDtypeStruct((L, Dm, N), F32)] * 4,
        compiler_params=_cp("arbitrary", "arbitrary"),
    )(w, c_act, dmod, m, v)


_BIG = (("w_ffn1_gu", 1), ("w_ffn1_down", 0), ("w_ffn2_gu", 1), ("w_ffn2_down", 0), ("w_in", 1), ("w_out", 0))
_AXIS = dict(_BIG)

_GATHER_PLAN = {
    "first": [(0, "w_ffn1_gu"), (0, "w_ffn1_down")],
    (0, "ffn1"): [(0, "w_in"), (0, "w_out"), (0, "w_ffn2_gu")],
    (0, "mix"): [(0, "w_ffn2_down")],
    (0, "ffn2"): [(1, "w_ffn1_gu"), (1, "w_ffn1_down")],
    (1, "ffn1"): [(1, "w_in"), (1, "w_out"), (1, "w_ffn2_gu")],
    (1, "mix"): [(1, "w_ffn2_down")],
}


def _pack(arrs, rows_multiple=SUBLANES):
    flat = jnp.concatenate([a.astype(F32).reshape(-1) for a in arrs])
    unit = rows_multiple * LANES
    total = -(-flat.shape[0] // unit) * unit
    return jnp.pad(flat, (0, total - flat.shape[0])).reshape(total // LANES, LANES)


def _unpack(flat, shapes):
    out, off = [], 0
    for shp in shapes:
        n = int(math.prod(shp))
        out.append(flat[off:off + n].reshape(shp))
        off += n
    return out


def _block_diag(w):
    out = jnp.zeros((LRU_WIDTH, LRU_WIDTH), F32)
    for h in range(4):
        out = lax.dynamic_update_slice(out, w[h], (h * HEAD_DIM, h * HEAD_DIM))
    return out


def _diag_blocks(w):
    return jnp.stack([w[h * HEAD_DIM:(h + 1) * HEAD_DIM, h * HEAD_DIM:(h + 1) * HEAD_DIM] for h in range(4)])


def _rows8(*rows):
    z = jnp.zeros((8 - len(rows), rows[0].shape[-1]), F32)
    return jnp.concatenate([jnp.stack(rows), z], axis=0)


def kernel(x, c, w_mod, b_mod, g_norm, w_ffn1_gu, w_ffn1_down, w_ffn2_gu, w_ffn2_down, w_in, w_out, attn_sinks, lru_conv_w, lru_conv_b, lru_gate_a_w, lru_gate_a_b, lru_gate_x_w, lru_gate_x_b, lru_lambda, sc_conv_w, g_final, loss_target, m_w_mod, m_b_mod, m_g_norm, m_w_ffn1_gu, m_w_ffn1_down, m_w_ffn2_gu, m_w_ffn2_down, m_w_in, m_w_out, m_attn_sinks, m_lru_conv_w, m_lru_conv_b, m_lru_gate_a_w, m_lru_gate_a_b, m_lru_gate_x_w, m_lru_gate_x_b, m_lru_lambda, m_sc_conv_w, m_g_final, v_w_mod, v_b_mod, v_g_norm, v_w_ffn1_gu, v_w_ffn1_down, v_w_ffn2_gu, v_w_ffn2_down, v_w_in, v_w_out, v_attn_sinks, v_lru_conv_w, v_lru_conv_b, v_lru_gate_a_w, v_lru_gate_a_b, v_lru_gate_x_w, v_lru_gate_x_b, v_lru_lambda, v_sc_conv_w, v_g_final):
    W = dict(w_mod=w_mod, b_mod=b_mod, g_norm=g_norm, w_ffn1_gu=w_ffn1_gu, w_ffn1_down=w_ffn1_down,
             w_ffn2_gu=w_ffn2_gu, w_ffn2_down=w_ffn2_down, w_in=w_in, w_out=w_out, attn_sinks=attn_sinks,
             lru_conv_w=lru_conv_w, lru_conv_b=lru_conv_b, lru_gate_a_w=lru_gate_a_w, lru_gate_a_b=lru_gate_a_b,
             lru_gate_x_w=lru_gate_x_w, lru_gate_x_b=lru_gate_x_b, lru_lambda=lru_lambda, sc_conv_w=sc_conv_w,
             g_final=g_final)
    M1 = dict(w_mod=m_w_mod, b_mod=m_b_mod, g_norm=m_g_norm, w_ffn1_gu=m_w_ffn1_gu, w_ffn1_down=m_w_ffn1_down,
              w_ffn2_gu=m_w_ffn2_gu, w_ffn2_down=m_w_ffn2_down, w_in=m_w_in, w_out=m_w_out,
              attn_sinks=m_attn_sinks, lru_conv_w=m_lru_conv_w, lru_conv_b=m_lru_conv_b,
              lru_gate_a_w=m_lru_gate_a_w, lru_gate_a_b=m_lru_gate_a_b, lru_gate_x_w=m_lru_gate_x_w,
              lru_gate_x_b=m_lru_gate_x_b, lru_lambda=m_lru_lambda, sc_conv_w=m_sc_conv_w, g_final=m_g_final)
    V1 = dict(w_mod=v_w_mod, b_mod=v_b_mod, g_norm=v_g_norm, w_ffn1_gu=v_w_ffn1_gu, w_ffn1_down=v_w_ffn1_down,
              w_ffn2_gu=v_w_ffn2_gu, w_ffn2_down=v_w_ffn2_down, w_in=v_w_in, w_out=v_w_out,
              attn_sinks=v_attn_sinks, lru_conv_w=v_lru_conv_w, lru_conv_b=v_lru_conv_b,
              lru_gate_a_w=v_lru_gate_a_w, lru_gate_a_b=v_lru_gate_a_b, lru_gate_x_w=v_lru_gate_x_w,
              lru_gate_x_b=v_lru_gate_x_b, lru_lambda=v_lru_lambda, sc_conv_w=v_sc_conv_w, g_final=v_g_final)
    names = ["w_mod", "b_mod", "g_norm", "w_ffn1_gu", "w_ffn1_down", "w_ffn2_gu", "w_ffn2_down", "w_in", "w_out",
             "attn_sinks", "lru_conv_w", "lru_conv_b", "lru_gate_a_w", "lru_gate_a_b", "lru_gate_x_w",
             "lru_gate_x_b", "lru_lambda", "sc_conv_w", "g_final"]

    xs = x[0]
    tgt = loss_target[0]
    S = xs.shape[0]
    chip = 2 * lax.axis_index("x") + lax.axis_index("y")
    batch = 2 * chip + lax.axis_index("c")
    L = DEPTH

    full = {}

    def gather_jobs(key):
        return [_GatherJob(W[n][l].astype(BF16), _AXIS[n]) for l, n in _GATHER_PLAN.get(key, ())]

    def landed(key, outs):
        full.update(zip(_GATHER_PLAN.get(key, ()), outs))

    fwd_shapes = [(D_MODEL,), g_norm.shape, lru_conv_w.shape, sc_conv_w.shape]
    gathered, c_act, mod_all, ex = _prologue(_pack([c[0], g_norm, lru_conv_w, sc_conv_w]), w_mod,
                                             gather_jobs("first"), "prologue")
    landed("first", ex)
    gathered = gathered.reshape(N_DEV, -1)
    per_chip = [_unpack(gathered[2 * jj], fwd_shapes) for jj in range(N_CHIPS)]
    g_norm_full = jnp.concatenate([p[1] for p in per_chip], axis=-1)
    lru_conv_w_full = jnp.concatenate([p[2] for p in per_chip], axis=-1)
    sc_conv_w_full = jnp.concatenate([p[3] for p in per_chip], axis=-1)
    mod_all = mod_all.reshape(N_DEV, L, 16, -1)
    mod_rows = [lax.dynamic_index_in_dim(mod_all[2 * jj], batch, axis=1, keepdims=False) for jj in range(N_CHIPS)]
    mod = (jnp.concatenate(mod_rows, axis=-1) + b_mod).reshape(L, 9, D_MODEL)

    nrm_all = jnp.concatenate([g_norm_full[:, :, None, :], mod.reshape(L, 3, 3, D_MODEL),
                               jnp.zeros((L, 3, 4, D_MODEL), F32)], axis=2)

    def nrm_rows(l, s):
        return nrm_all[l, s]

    def mixer_params(l):
        small = jnp.concatenate([lru_conv_w_full[l], lru_conv_b[l][None], lru_gate_a_b[l][None],
                                 lru_gate_x_b[l][None], lru_lambda[l][None], sc_conv_w_full[l],
                                 jnp.zeros((5, LRU_WIDTH), F32)], axis=0)
        return (attn_sinks[l], small, _block_diag(lru_gate_a_w[l]).astype(BF16),
                _block_diag(lru_gate_x_w[l]).astype(BF16))

    saved = []
    xcur = xs
    for l in range(L):
        n1, n2, n3 = nrm_rows(l, 0), nrm_rows(l, 1), nrm_rows(l, 2)

        def ffn(which, xin, nrm, head=None):
            key = (l, which)
            (xo, h, gu, y, *stats), ex = _ffn_fwd(xin, nrm, full[(l, f"w_{which}_gu")], full[(l, f"w_{which}_down")],
                                                  f"l{l}_{which}", gather_jobs(key), head)
            landed(key, ex)
            return (xo, *stats), (xin, h, gu, y)

        (x1,), s1 = ffn("ffn1", xcur, n1)
        mp = mixer_params(l)
        (x2, h2, proj, ymix, ymo, hprev), ex = _mixer_fwd(x1, n2, full[(l, "w_in")], full[(l, "w_out")], *mp,
                                                          f"l{l}_mix", gather_jobs((l, "mix")))
        landed((l, "mix"), ex)
        s2 = (x1, h2, proj, ymix, ymo, hprev, mp)
        (xcur, *stats), s3 = ffn("ffn2", x2, n3, (_rows8(g_final), tgt) if l == L - 1 else None)
        saved.append((n1, n2, n3, s1, s2, s3))

    dx, stats = xcur, stats[0]
    loss_here, d_g_final = stats[1, 0:1], stats[0]

    recv, theirs = {}, {}
    waiting = []

    def carried(fn, *a, extra=()):
        items = waiting + list(extra)
        waiting.clear()
        outs, landed_now = fn(*a, jobs=[_SiblingJob(recv[(ll, n)]) if g is None else _ScatterJob(g, _AXIS[n])
                                        for ll, n, g in items])
        for (ll, n, g), arr in zip(items, landed_now):
            if g is None:
                theirs[(ll, n)] = arr
            else:
                recv[(ll, n)] = arr
                waiting.append((ll, n, None))
        return outs

    dmod, d_gnorm, d_small = [None] * L, [None] * L, [None] * L
    for l in reversed(range(L)):
        n1, n2, n3, s1, s2, s3 = saved[l]

        def plain(fn, *a):
            return fn(*a)[0]

        def ffn_bwd(which, dxo, sv, nrm, last):
            xin, h, gu, y = sv
            tag = f"l{l}_{which}"
            dgu, dgate, dw_down = carried(
                _ffn_down_bwd, dxo, gu, y, full[(l, f"w_{which}_down")], nrm, tag + "_down_bwd")
            dw_gu = carried(_atb, h, dgu, BF16, 1024, 2816, tag + "_dw_gu", extra=[(l, f"w_{which}_down", dw_down)])
            mine = [(l, f"w_{which}_gu", dw_gu)]
            dxi, red = (carried if last else plain)(
                _nt_norm_bwd, dgu, full[(l, f"w_{which}_gu")], xin, nrm, dxo, tag + "_gu_bwd",
                **(dict(extra=mine) if last else {}))
            if not last:
                waiting.extend(mine)
            return dxi, (red[0], red[1], dgate[0]), red[2]

        dx, dm3, dg3 = ffn_bwd("ffn2", dx, s3, n3, False)
        x_in, h2, proj, ymix, ymo, hprev, mp = s2
        dx, red, dgate, dw_out, dw_in, dsm, dsink, dwa, dwx = carried(
            _mixer_bwd, x_in, n2, dx, h2, proj, ymix, ymo, hprev, full[(l, "w_in")], full[(l, "w_out")], *mp,
            f"l{l}_mix_bwd")
        waiting.extend([(l, "w_out", dw_out), (l, "w_in", dw_in)])
        dm2, dg2 = (red[0], red[1], dgate[0]), red[2]
        dx, dm1, dg1 = ffn_bwd("ffn1", dx, s1, n1, l == 0)
        dmod[l] = jnp.stack(list(dm1) + list(dm2) + list(dm3))
        d_gnorm[l] = jnp.stack([dg1, dg2, dg3])
        d_small[l] = (dsink[:, 0], dsm[0:4], dsm[4], _diag_blocks(dwa), dsm[5], _diag_blocks(dwx), dsm[6],
                      dsm[7], dsm[8:11])
    grad_x = dx[None]

    def both(k):
        return jnp.stack([d_small[0][k], d_small[1][k]])
    small_names = ["g_norm", "attn_sinks", "lru_conv_w", "lru_conv_b", "lru_gate_a_w", "lru_gate_a_b",
                   "lru_gate_x_w", "lru_gate_x_b", "lru_lambda", "sc_conv_w", "g_final"]
    small_parts = [jnp.stack(d_gnorm)] + [both(k) for k in range(9)] + [d_g_final]
    dmod_flat = jnp.stack(dmod).reshape(-1)
    bwd_gathered, bwd_sum = carried(_all_gather_small, _pack([dmod_flat] + small_parts + [loss_here]),
                                    "gather_small_bwd")
    n_mod = dmod_flat.shape[0]
    dmod_all = bwd_gathered.reshape(N_DEV, -1)[:, :n_mod].reshape(N_DEV, L, 9 * D_MODEL)
    bwd_sum = bwd_sum.reshape(-1)
    G = {"b_mod": bwd_sum[:n_mod].reshape(L, 9 * D_MODEL)}
    *small_sums, loss = _unpack(bwd_sum[n_mod:], [p.shape for p in small_parts] + [(1,)])
    loss = loss[0]
    G.update(zip(small_names, small_sums))
    for n in ("g_norm", "lru_conv_w", "sc_conv_w"):
        wdt = W[n].shape[-1]
        G[n] = lax.dynamic_slice_in_dim(G[n], chip * wdt, wdt, axis=G[n].ndim - 1)

    ncol = w_mod.shape[-1]
    dmod_cols = lax.dynamic_slice_in_dim(dmod_all, chip * ncol, ncol, axis=2)
    dmod_rows = jnp.pad(jnp.swapaxes(dmod_cols, 0, 1), ((0, 0), (0, 16 - N_DEV), (0, 0))).astype(BF16)

    out_g, out_d, out_m, out_v = {}, {}, {}, {}
    out_g["w_mod"], out_d["w_mod"], out_m["w_mod"], out_v["w_mod"] = _adamw_w_mod(
        w_mod, c_act, dmod_rows, m_w_mod, v_w_mod, "adamw_w_mod")
    for n, _ in _BIG:
        shp = W[n].shape
        flat = (shp[0] * shp[1], shp[2])
        res = _adamw_partials(W[n].reshape(flat), [(recv[(l, n)], theirs[(l, n)]) for l in range(L)],
                              M1[n].reshape(flat), V1[n].reshape(flat), f"adamw_{n}")
        out_g[n], out_d[n], out_m[n], out_v[n] = [r.reshape(shp) for r in res]
    rest = ["b_mod"] + small_names

    def rows(a):
        return a.reshape(-1, a.shape[-1])
    res = _adamw_many([rows(W[n]) for n in rest], [rows(G[n]) for n in rest], [rows(M1[n]) for n in rest],
                      [rows(V1[n]) for n in rest], "adamw_small")
    for dst, group in zip((out_g, out_d, out_m, out_v), res):
        dst.update({n: r.reshape(W[n].shape) for n, r in zip(rest, group)})

    return (loss, grad_x, *[out_g[n] for n in names], *[out_d[n] for n in names],
            *[out_m[n] for n in names], *[out_v[n] for n in names])
```

```python
import math

import jax
import jax.numpy as jnp
from jax import lax
from jax.experimental import pallas as pl
from jax.experimental.pallas import tpu as pltpu

F32 = jnp.float32
BF16 = jnp.bfloat16

D_MODEL = 1024
DEPTH = 2
HEAD_DIM = 64
N_Q_HEADS = 8
ATTN_WIDTH = 512
KV_WIDTH = 128
LRU_WIDTH = 256
CONV_WIDTH = 256
IN_PROJ_WIDTH = 2048
BLOCK = 128
D_FF = 2816
EPS = 1e-6
NEG_INF = -1e30
LRU_C = 8.0
N_CHIPS = 4
N_DEV = 8

C_Q, C_KV, C_LX, C_LG, C_SB, C_SC, C_SX = 0, 512, 768, 1024, 1280, 1536, 1792

ADAM_LR = 0.001
ADAM_B1 = 0.9
ADAM_B2 = 0.999
ADAM_EPS = 1e-08
ADAM_WD = 0.01
ADAM_STEP = 10

LANES = 128
SUBLANES = 8
VMEM_LIMIT = 56 * 1024 * 1024
MIX_TILE = 256

MESH = pl.DeviceIdType.MESH


def _cp(*sem):
    return pltpu.CompilerParams(dimension_semantics=sem, vmem_limit_bytes=VMEM_LIMIT)


def _tile(n, pref):
    t = min(n, pref)
    while n % t:
        t //= 2
    return t


MXU_DIM = 256


def _resident(shape):
    return pl.BlockSpec(shape, lambda *_: (0, 0), pipeline_mode=pl.Buffered(1))


def _sigmoid(v):
    return 1.0 / (1.0 + jnp.exp(-v))


def _expm1(v):
    series = v * (1.0 + v * (0.5 + v * (1.0 / 6.0)))
    return jnp.where(v > -0.01, series, jnp.exp(v) - 1.0)


def _softplus_neg(lam):
    e = jnp.exp(-jnp.abs(lam))
    log1p = jnp.where(e < 1e-2, e * (1.0 - e * (0.5 - e * (1.0 / 3.0))), jnp.log(1.0 + e))
    return jnp.maximum(-lam, 0.0) + log1p


_GELU_K = math.sqrt(2.0 / math.pi)
_GELU_C = 0.044715


def _gelu(v):
    t = jnp.tanh(_GELU_K * (v + _GELU_C * v * v * v))
    return 0.5 * v * (1.0 + t), t


def _gelu_grad(v, t):
    return 0.5 * (1.0 + t) + 0.5 * v * (1.0 - t * t) * _GELU_K * (1.0 + 3.0 * _GELU_C * v * v)


def _dot(a, b):
    return jnp.dot(a, b, preferred_element_type=F32)


def _dot_nt(a, b):
    return lax.dot_general(a, b, (((1,), (1,)), ((), ())), preferred_element_type=F32)


def _dot_tn(a, b):
    return lax.dot_general(a, b, (((0,), (0,)), ((), ())), preferred_element_type=F32)


def _window(ref, axis, j, width):
    start = pl.multiple_of(j * width, LANES if axis == 1 else 16)
    if axis == 1:
        return ref.at[:, pl.ds(start, width)]
    return ref.at[pl.ds(start, width), :]


def _chip_peers():
    x, y, c = lax.axis_index("x"), lax.axis_index("y"), lax.axis_index("c")
    return x, y, c, [(1 - x, y), (x, 1 - y), (1 - x, 1 - y)]


class _GatherJob:
    def __init__(self, shard, axis):
        self.src, self.axis, self.width, self.half = shard, axis, shard.shape[axis], shard.shape[0] // 2
        full = tuple(d * N_CHIPS if k == axis else d for k, d in enumerate(shard.shape))
        self.out_shape = jax.ShapeDtypeStruct(full, shard.dtype)

    def _piece(self, ref, j, hf):
        if self.axis == 1:
            return ref.at[pl.ds(pl.multiple_of(hf * self.half, 16), self.half),
                          pl.ds(pl.multiple_of(j * self.width, LANES), self.width)]
        return ref.at[pl.ds(pl.multiple_of(j * self.width + hf * self.half, 16), self.half), :]

    def _copies(self, src, dst, send, recv, loc, t):
        x, y, c, chips = _chip_peers()
        j = 2 * x + y
        owners = [2 * px + py for px, py in chips]
        local = pltpu.make_async_copy(src, _window(dst, self.axis, j, self.width), loc.at[t])
        mine = src.at[pl.ds(pl.multiple_of(c * self.half, 16), self.half), :]

        def ici(k, owner):
            return pltpu.make_async_remote_copy(
                src_ref=mine, dst_ref=self._piece(dst, owner, c), send_sem=send.at[JOB_SEMS * t + k],
                recv_sem=recv.at[JOB_SEMS * t + k], device_id=(*chips[k], c), device_id_type=MESH)

        def relay(k, hf):
            piece = self._piece(dst, owners[k], hf)
            return pltpu.make_async_remote_copy(
                src_ref=piece, dst_ref=piece, send_sem=send.at[JOB_SEMS * t + 4 + k],
                recv_sem=recv.at[JOB_SEMS * t + 4 + k],
                device_id=(x, y, 1 - c), device_id_type=MESH)

        return (local, [ici(k, j) for k in range(3)], [ici(k, owners[k]) for k in range(3)],
                [relay(k, c) for k in range(3)], [relay(k, 1 - c) for k in range(3)])

    def start(self, *a):
        local, ici_out, _, _, _ = self._copies(*a)
        local.start()
        for cp in ici_out:
            cp.start()

    def relay(self, *a):
        _, _, ici_in, relay_out, _ = self._copies(*a)
        for arrived, onward in zip(ici_in, relay_out):
            arrived.wait_recv()
            onward.start()

    def finish(self, *a):
        local, ici_out, _, relay_out, relay_in = self._copies(*a)
        for cp in relay_in:
            cp.wait_recv()
        for cp in ici_out + relay_out:
            cp.wait_send()
        local.wait()


class _ScatterJob:
    def __init__(self, full, axis):
        self.src, self.axis, self.width = full, axis, full.shape[axis] // N_CHIPS
        shard = tuple(self.width if k == axis else d for k, d in enumerate(full.shape))
        self.out_shape = jax.ShapeDtypeStruct((N_CHIPS,) + shard, full.dtype)

    def _copies(self, src, dst, send, recv, loc, t):
        x, y, c, chips = _chip_peers()
        local = pltpu.make_async_copy(_window(src, self.axis, 2 * x + y, self.width), dst.at[3], loc.at[t])
        sends = [pltpu.make_async_remote_copy(
            src_ref=_window(src, self.axis, 2 * px + py, self.width), dst_ref=dst.at[k],
            send_sem=send.at[JOB_SEMS * t + k], recv_sem=recv.at[JOB_SEMS * t + k], device_id=(px, py, c),
            device_id_type=MESH) for k, (px, py) in enumerate(chips)]
        return local, sends

    def start(self, *a):
        local, sends = self._copies(*a)
        local.start()
        for cp in sends:
            cp.start()

    def relay(self, *a):
        pass

    def finish(self, *a):
        local, sends = self._copies(*a)
        for cp in sends:
            cp.wait_recv()
        for cp in sends:
            cp.wait_send()
        local.wait()


class _SiblingJob:
    def __init__(self, arr):
        self.src, self.out_shape = arr, jax.ShapeDtypeStruct(arr.shape, arr.dtype)

    def _copy(self, src, dst, send, recv, loc, t):
        x, y, c = lax.axis_index("x"), lax.axis_index("y"), lax.axis_index("c")
        return pltpu.make_async_remote_copy(
            src_ref=src, dst_ref=dst, send_sem=send.at[JOB_SEMS * t], recv_sem=recv.at[JOB_SEMS * t],
            device_id=(x, y, 1 - c), device_id_type=MESH)

    def start(self, *a):
        self._copy(*a).start()

    def relay(self, *a):
        pass

    def finish(self, *a):
        self._copy(*a).wait()


JOB_SEMS = 8


def _run_jobs(phase, jobs, srcs, dsts, sems):
    for t, job in enumerate(jobs):
        getattr(job, phase)(srcs[t], dsts[t], *sems, t)


def _job_scratch(n):
    return [pltpu.SemaphoreType.DMA((JOB_SEMS * n,)), pltpu.SemaphoreType.DMA((JOB_SEMS * n,)),
            pltpu.SemaphoreType.DMA((n,))]


def _pcall(body, *, name, grid, in_specs, out_specs, out_shape, sem, args, scratch_shapes=(), jobs=()):
    in_specs, out_specs, out_shape = list(in_specs), list(out_specs), list(out_shape)
    scratch_shapes = list(scratch_shapes)
    if not jobs:
        res = pl.pallas_call(body, name=name, grid=grid, in_specs=in_specs, out_specs=out_specs, out_shape=out_shape,
                             scratch_shapes=scratch_shapes, compiler_params=_cp(*sem))(*args)
        return list(res), []
    n_in, n_out, n_scr, nj = len(args), len(out_shape), len(scratch_shapes), len(jobs)
    n_steps = math.prod(grid)
    relay_step = (3 * n_steps) // 4
    relay_early = 0 < relay_step < n_steps - 1

    def wrapped(*refs):
        ins, refs = refs[:n_in], refs[n_in:]
        jin, refs = refs[:nj], refs[nj:]
        outs, refs = refs[:n_out], refs[n_out:]
        jout, refs = refs[:nj], refs[nj:]
        scr, sems = refs[:n_scr], refs[n_scr:]
        step = pl.program_id(0)
        for d in range(1, len(grid)):
            step = step * grid[d] + pl.program_id(d)

        @pl.when(step == 0)
        def _():
            _run_jobs("start", jobs, jin, jout, sems)

        if relay_early:
            @pl.when(step == relay_step)
            def _():
                _run_jobs("relay", jobs, jin, jout, sems)
        body(*ins, *outs, *scr)

        @pl.when(step == n_steps - 1)
        def _():
            if not relay_early:
                _run_jobs("relay", jobs, jin, jout, sems)
            _run_jobs("finish", jobs, jin, jout, sems)

    hbm = pl.BlockSpec(memory_space=pltpu.HBM)
    res = pl.pallas_call(
        wrapped, name=name, grid=grid, in_specs=in_specs + [hbm] * nj, out_specs=out_specs + [hbm] * nj,
        out_shape=out_shape + [job.out_shape for job in jobs], scratch_shapes=scratch_shapes + _job_scratch(nj),
        compiler_params=_cp(*sem))(*args, *[job.src for job in jobs])
    return list(res[:n_out]), list(res[n_out:])


def _hidden_chunks(k):
    return [(c0, min(6 * MXU_DIM, k - c0)) for c0 in range(0, k, 6 * MXU_DIM)]


def _loss_head(xv, gain, tgt, st_ref):
    dm = xv.shape[-1]
    rstd = lax.rsqrt(jnp.mean(xv * xv, axis=-1, keepdims=True) + EPS)
    xn = xv * rstd
    err = xn * gain - tgt
    st_ref[1:2, :] += jnp.full((1, dm), 0.5 / dm, F32) * jnp.sum(err * err)
    dy = err * (1.0 / dm)
    st_ref[0:1, :] += jnp.sum(dy * xn, axis=0, keepdims=True)
    dxn = dy * gain
    return rstd * (dxn - xn * jnp.mean(dxn * xn, axis=-1, keepdims=True))


def _ffn_fwd(x, nrm, w_gu, w_down, name, jobs=(), head=None):
    S, Dm = x.shape
    K = w_down.shape[0]
    tm = _tile(S, 256)

    def body(x_ref, nrm_ref, wgu_ref, wdn_ref, *rest):
        if head is None:
            o_ref, h_ref, gu_ref, y_ref = rest
        else:
            gf_ref, t_ref, o_ref, h_ref, gu_ref, y_ref, st_ref = rest

            @pl.when(pl.program_id(0) == 0)
            def _():
                st_ref[...] = jnp.zeros_like(st_ref)
        xv = x_ref[...]
        rstd = lax.rsqrt(jnp.mean(xv * xv, axis=-1, keepdims=True) + EPS)
        hn = (xv * rstd) * nrm_ref[0:1, :]
        hb = (hn * (1.0 + nrm_ref[2:3, :]) + nrm_ref[1:2, :]).astype(BF16)
        h_ref[...] = hb
        y = jnp.zeros((tm, Dm), F32)
        for c0, cs in _hidden_chunks(K):
            g = _dot(hb, wgu_ref[:, c0:c0 + cs])
            u = _dot(hb, wgu_ref[:, K + c0:K + c0 + cs])
            gu_ref[:, c0:c0 + cs] = g.astype(BF16)
            gu_ref[:, K + c0:K + c0 + cs] = u.astype(BF16)
            y = y + _dot((g * _sigmoid(g) * u).astype(BF16), wdn_ref[c0:c0 + cs, :])
        xo = xv + (0.5 * nrm_ref[3:4, :]) * y
        o_ref[...] = xo if head is None else _loss_head(xo, gf_ref[0:1, :], t_ref[...], st_ref)
        y_ref[...] = y.astype(BF16)

    row = lambda i: (i, 0)
    fix = lambda i: (0, 0)
    in_specs = [pl.BlockSpec((tm, Dm), row), pl.BlockSpec((8, Dm), fix), _resident((Dm, 2 * K)), _resident((K, Dm))]
    out_specs = [pl.BlockSpec((tm, Dm), row), pl.BlockSpec((tm, Dm), row), pl.BlockSpec((tm, 2 * K), row),
                 pl.BlockSpec((tm, Dm), row)]
    out_shape = [jax.ShapeDtypeStruct((S, Dm), F32), jax.ShapeDtypeStruct((S, Dm), BF16),
                 jax.ShapeDtypeStruct((S, 2 * K), BF16), jax.ShapeDtypeStruct((S, Dm), BF16)]
    args = (x, nrm, w_gu, w_down)
    if head is not None:
        in_specs += [pl.BlockSpec((8, Dm), fix), pl.BlockSpec((tm, Dm), row)]
        out_specs.append(pl.BlockSpec((8, Dm), fix))
        out_shape.append(jax.ShapeDtypeStruct((8, Dm), F32))
        args += tuple(head)
    return _pcall(body, name=name, grid=(S // tm,), in_specs=in_specs, out_specs=out_specs, out_shape=out_shape,
                  sem=("arbitrary",), args=args, jobs=jobs)


def _ffn_down_bwd(dxo, gu, y, w, nrm, name, jobs=()):
    S, Dm = dxo.shape
    K = w.shape[0]
    Ka = gu.shape[1]
    coef = 0.5
    tm = _tile(S, 256)
    n_steps = S // tm
    chunks = _hidden_chunks(K)

    def body(dxo_ref, y_ref, w_ref, nrm_ref, a_ref, da_ref, dgate_ref, dw_ref, acc):
        @pl.when(pl.program_id(0) == 0)
        def _():
            dgate_ref[...] = jnp.zeros_like(dgate_ref)
            acc[...] = jnp.zeros_like(acc)

        dxo_v = dxo_ref[...]
        dyb = ((coef * nrm_ref[3:4, :]) * dxo_v).astype(BF16)
        dgate_ref[0:1, :] += jnp.sum(coef * y_ref[...].astype(F32) * dxo_v, axis=0, keepdims=True)
        for c0, cs in chunks:
            dact = _dot_nt(dyb, w_ref[c0:c0 + cs, :])
            g = a_ref[:, c0:c0 + cs].astype(F32)
            u = a_ref[:, K + c0:K + c0 + cs].astype(F32)
            s = _sigmoid(g)
            si = g * s
            da_ref[:, c0:c0 + cs] = (dact * u * (s * (1.0 + g * (1.0 - s)))).astype(BF16)
            da_ref[:, K + c0:K + c0 + cs] = (dact * si).astype(BF16)
            acc[c0:c0 + cs, :] += _dot_tn((si * u).astype(BF16), dyb)

        @pl.when(pl.program_id(0) == n_steps - 1)
        def _():
            dw_ref[...] = acc[...].astype(BF16)

    row = lambda i: (i, 0)
    fix = lambda i: (0, 0)
    return _pcall(
        body, name=name, grid=(n_steps,),
        in_specs=[pl.BlockSpec((tm, Dm), row), pl.BlockSpec((tm, Dm), row), _resident((K, Dm)),
                  pl.BlockSpec((8, Dm), fix), pl.BlockSpec((tm, Ka), row)],
        out_specs=[pl.BlockSpec((tm, Ka), row), pl.BlockSpec((8, Dm), fix), _resident((K, Dm))],
        out_shape=[jax.ShapeDtypeStruct((S, Ka), BF16), jax.ShapeDtypeStruct((8, Dm), F32),
                   jax.ShapeDtypeStruct((K, Dm), BF16)],
        scratch_shapes=[pltpu.VMEM((K, Dm), F32)],
        sem=("arbitrary",), args=(dxo, y, w, nrm, gu), jobs=jobs)


def _atb(a, b, out_dtype, bm, bn, name, jobs=()):
    S, M = a.shape
    N = b.shape[1]
    bk = _tile(S, 1024)
    nk = S // bk

    def body(a_ref, b_ref, o_ref, acc):
        k = pl.program_id(2)

        @pl.when(k == 0)
        def _():
            acc[...] = jnp.zeros_like(acc)
        acc[...] += _dot_tn(a_ref[...], b_ref[...])

        @pl.when(k == nk - 1)
        def _():
            o_ref[...] = acc[...].astype(o_ref.dtype)

    (out,), extra = _pcall(
        body, name=name, grid=(M // bm, N // bn, nk),
        in_specs=[pl.BlockSpec((bk, bm), lambda m, n, k: (k, m)),
                  pl.BlockSpec((bk, bn), lambda m, n, k: (k, n))],
        out_specs=[pl.BlockSpec((bm, bn), lambda m, n, k: (m, n))],
        out_shape=[jax.ShapeDtypeStruct((M, N), out_dtype)],
        scratch_shapes=[pltpu.VMEM((bm, bn), F32)],
        sem=("arbitrary", "arbitrary", "arbitrary"), args=(a, b), jobs=jobs)
    return out, extra


def _norm_bwd(dh, xv, nrm_ref, red_ref):
    rstd = lax.rsqrt(jnp.mean(xv * xv, axis=-1, keepdims=True) + EPS)
    xn = xv * rstd
    gain = nrm_ref[0:1, :]
    hn = xn * gain
    dhn = dh * (1.0 + nrm_ref[2:3, :])
    red_ref[0:1, :] += jnp.sum(dh, axis=0, keepdims=True)
    red_ref[1:2, :] += jnp.sum(dh * hn, axis=0, keepdims=True)
    red_ref[2:3, :] += jnp.sum(dhn * xn, axis=0, keepdims=True)
    dxn = dhn * gain
    return rstd * (dxn - xn * jnp.mean(dxn * xn, axis=-1, keepdims=True))


def _nt_norm_bwd(dout, w, x, nrm, dxo, name, jobs=()):
    S, N = dout.shape
    Dm = w.shape[0]
    tm = _tile(S, 512)

    def body(do_ref, w_ref, x_ref, nrm_ref, dxo_ref, dx_ref, red_ref):
        @pl.when(pl.program_id(0) == 0)
        def _():
            red_ref[...] = jnp.zeros_like(red_ref)
        dh = _dot_nt(do_ref[...], w_ref[...])
        dx_ref[...] = dxo_ref[...] + _norm_bwd(dh, x_ref[...], nrm_ref, red_ref)

    return _pcall(
        body, name=name, grid=(S // tm,),
        in_specs=[pl.BlockSpec((tm, N), lambda i: (i, 0)),
                  _resident((Dm, N)),
                  pl.BlockSpec((tm, Dm), lambda i: (i, 0)),
                  pl.BlockSpec((8, Dm), lambda i: (0, 0)),
                  pl.BlockSpec((tm, Dm), lambda i: (i, 0))],
        out_specs=[pl.BlockSpec((tm, Dm), lambda i: (i, 0)),
                   pl.BlockSpec((8, Dm), lambda i: (0, 0))],
        out_shape=[jax.ShapeDtypeStruct((S, Dm), F32), jax.ShapeDtypeStruct((8, Dm), F32)],
        sem=("arbitrary",), args=(dout, w, x, nrm, dxo), jobs=jobs)


def _alibi_slope(h):
    return float(2.0 ** (-8.0 * (h + 1) / N_Q_HEADS))


def _head_planes(pair_cols):
    lane = lax.broadcasted_iota(jnp.int32, pair_cols.shape, 1)
    low = lane < HEAD_DIM
    h0_lo = jnp.where(low, pair_cols, 0.0)
    h1_hi = jnp.where(low, 0.0, pair_cols)
    h0_hi = pltpu.roll(h0_lo, HEAD_DIM, 1)
    h1_lo = pltpu.roll(h1_hi, HEAD_DIM, 1)
    return ((h0_lo.astype(BF16), h0_hi.astype(BF16)), (h1_lo.astype(BF16), h1_hi.astype(BF16)))


def _band_geometry(first_block):
    qi = lax.broadcasted_iota(jnp.int32, (BLOCK, BLOCK), 0)
    kj = lax.broadcasted_iota(jnp.int32, (BLOCK, BLOCK), 1)
    own = kj <= qi
    dist = jnp.where(own, qi - kj, qi + BLOCK - kj).astype(F32)
    valid = kj <= qi + BLOCK * (1 - first_block)
    return own, dist, valid


def _fold(band, own):
    return jnp.where(own, band[:, BLOCK:], band[:, :BLOCK])


def _unfold(v, own):
    return jnp.concatenate([jnp.where(own, 0.0, v), jnp.where(own, v, 0.0)], axis=1)


def _softmax_band(s, h, geometry, sink):
    own, dist, valid = geometry
    s = jnp.where(valid, s - _alibi_slope(h) * dist, NEG_INF)
    m = jnp.maximum(jnp.max(s, axis=-1, keepdims=True), sink)
    p = jnp.exp(s - m)
    e_sink = jnp.exp(sink - m)
    inv = 1.0 / (jnp.sum(p, axis=-1, keepdims=True) + e_sink)
    return p * inv, e_sink * inv


def _past(cur, prev, s, row):
    return jnp.where(row < s, pltpu.roll(prev, s, 0), pltpu.roll(cur, s, 0))


def _future(cur, nxt, s, row):
    T = cur.shape[0]
    return jnp.where(row >= T - s, pltpu.roll(nxt, T - s, 0), pltpu.roll(cur, T - s, 0))


def _edge_row(v, last):
    T = v.shape[0]
    r8 = lax.broadcasted_iota(jnp.int32, (SUBLANES, v.shape[1]), 0)
    blk = v[T - SUBLANES:, :] if last else v[:SUBLANES, :]
    return jnp.sum(jnp.where(r8 == (SUBLANES - 1 if last else 0), blk, 0.0), axis=0, keepdims=True)


def _lru_gates(lx, lx_prev, small_ref, wa_ref, wx_ref, row, t0):
    xc = (small_ref[4:5, :] + small_ref[3:4, :] * lx + small_ref[2:3, :] * _past(lx, lx_prev, 1, row)
          + small_ref[1:2, :] * _past(lx, lx_prev, 2, row) + small_ref[0:1, :] * _past(lx, lx_prev, 3, row))
    xcb = xc.astype(BF16)
    r = _sigmoid(_dot(xcb, wa_ref[...]) + small_ref[5:6, :])
    ig = _sigmoid(_dot(xcb, wx_ref[...]) + small_ref[6:7, :])
    sp = _softplus_neg(small_ref[7:8, :])
    la = (-LRU_C) * r * sp
    a = jnp.exp(la)
    first = (row + t0) == 0
    mult = jnp.where(first, 1.0, jnp.sqrt(-_expm1(2.0 * la)))
    return xc, xcb, r, ig, sp, a, mult, first


def _mixer_fwd(x, nrm, w_in, w_out, sinks, small, wa, wx, name, jobs=()):
    S, Dm = x.shape
    T = _tile(S, 2 * MIX_TILE)
    nT = S // T
    nb = T // BLOCK

    def body(x_ref, nrm_ref, w_in_ref, w_out_ref, sink_ref, small_ref, wa_ref, wx_ref,
             xo_ref, h_ref, proj_ref, y_ref, ymo_ref, hp_ref, *carried_state):
        xv = x_ref[...]
        rstd = lax.rsqrt(jnp.mean(xv * xv, axis=-1, keepdims=True) + EPS)
        hn = (xv * rstd) * nrm_ref[0:1, :]
        hb = (hn * (1.0 + nrm_ref[2:3, :]) + nrm_ref[1:2, :]).astype(BF16)
        h_ref[...] = hb
        proj_ref[...] = _dot(hb, w_in_ref[...])
        core(proj_ref, sink_ref, small_ref, wa_ref, wx_ref, y_ref, hp_ref, *carried_state)
        yo = _dot(y_ref[...], w_out_ref[...])
        xo_ref[...] = xv + nrm_ref[3:4, :] * yo
        ymo_ref[...] = yo.astype(BF16)

    def core(proj_ref, sink_ref, small_ref, wa_ref, wx_ref, y_ref, hp_ref, kvp, lxp, zp, hcar):
        i = pl.program_id(0)

        @pl.when(i == 0)
        def _():
            kvp[...] = jnp.zeros_like(kvp)
            lxp[...] = jnp.zeros_like(lxp)
            zp[...] = jnp.zeros_like(zp)
            hcar[...] = jnp.zeros_like(hcar)

        row = lax.broadcasted_iota(jnp.int32, (T, LRU_WIDTH), 0)

        kv = proj_ref[:, C_KV:C_KV + 2 * KV_WIDTH]
        ext = jnp.concatenate([kvp[...], kv], axis=0)
        kx = _head_planes(ext[:, :KV_WIDTH])
        vx = _head_planes(ext[:, KV_WIDTH:])
        first_tile = jnp.where(i == 0, 1, 0)
        units = [(b, pair, e) for b in range(nb) for pair in range(N_Q_HEADS // 2) for e in range(2)]
        geometry = [_band_geometry(first_tile if b == 0 else 0) for b in range(nb)]
        keys = [slice(b * BLOCK, (b + 2) * BLOCK) for b in range(nb)]
        qp = {(b, pair): (proj_ref[b * BLOCK:(b + 1) * BLOCK, pair * LANES:(pair + 1) * LANES] * 0.125).astype(BF16)
              for b in range(nb) for pair in range(N_Q_HEADS // 2)}
        scores = [_fold(_dot_nt(qp[(b, pair)], kx[pair // 2][e][keys[b]]), geometry[b][0]) for b, pair, e in units]
        probs = [_unfold(_softmax_band(s, 2 * pair + e, geometry[b], sink_ref[2 * pair + e])[0],
                         geometry[b][0]).astype(BF16) for s, (b, pair, e) in zip(scores, units)]
        outs = [_dot(p, vx[pair // 2][e][keys[b]]) for p, (b, pair, e) in zip(probs, units)]
        for u in range(0, len(units), 2):
            b, pair, _ = units[u]
            y_ref[b * BLOCK:(b + 1) * BLOCK, pair * LANES:(pair + 1) * LANES] = (outs[u] + outs[u + 1]).astype(BF16)
        kvp[...] = kv[T - BLOCK:, :]

        lx = proj_ref[:, C_LX:C_LX + LRU_WIDTH]
        xc, _, _, ig, _, a, mult, _ = _lru_gates(lx, lxp[...], small_ref, wa_ref, wx_ref, row, i * T)
        lxp[...] = lx
        aa = a
        bb = mult * (ig * xc)
        s = 1
        while s < T:
            a_sh = jnp.where(row >= s, pltpu.roll(aa, s, 0), 1.0)
            b_sh = jnp.where(row >= s, pltpu.roll(bb, s, 0), 0.0)
            bb = aa * b_sh + bb
            aa = aa * a_sh
            s *= 2
        hc = hcar[0:1, :]
        hh = bb + aa * hc
        hp_ref[...] = jnp.where(row < 1, hc, pltpu.roll(hh, 1, 0))
        hcar[...] = jnp.broadcast_to(_edge_row(hh, True), hcar.shape)
        gl, _ = _gelu(proj_ref[:, C_LG:C_LG + LRU_WIDTH])
        y_ref[:, ATTN_WIDTH:ATTN_WIDTH + LRU_WIDTH] = (gl * hh).astype(BF16)

        z = proj_ref[:, C_SC:C_SC + CONV_WIDTH] * proj_ref[:, C_SX:C_SX + CONV_WIDTH]
        c3 = (small_ref[10:11, :] * z + small_ref[9:10, :] * _past(z, zp[...], 1, row)
              + small_ref[8:9, :] * _past(z, zp[...], 2, row))
        zp[...] = z
        y_ref[:, ATTN_WIDTH + LRU_WIDTH:] = (proj_ref[:, C_SB:C_SB + CONV_WIDTH] * c3).astype(BF16)

    fix = lambda i: (0, 0)
    row = lambda i: (i, 0)
    return _pcall(
        body, name=name, grid=(nT,),
        in_specs=[pl.BlockSpec((T, Dm), row), pl.BlockSpec((8, Dm), fix),
                  _resident((Dm, IN_PROJ_WIDTH)), _resident((D_MODEL, Dm)),
                  pl.BlockSpec(memory_space=pltpu.SMEM),
                  pl.BlockSpec((16, LRU_WIDTH), fix),
                  pl.BlockSpec((LRU_WIDTH, LRU_WIDTH), fix),
                  pl.BlockSpec((LRU_WIDTH, LRU_WIDTH), fix)],
        out_specs=[pl.BlockSpec((T, Dm), row), pl.BlockSpec((T, Dm), row), pl.BlockSpec((T, IN_PROJ_WIDTH), row),
                   pl.BlockSpec((T, D_MODEL), row), pl.BlockSpec((T, Dm), row), pl.BlockSpec((T, LRU_WIDTH), row)],
        out_shape=[jax.ShapeDtypeStruct((S, Dm), F32), jax.ShapeDtypeStruct((S, Dm), BF16),
                   jax.ShapeDtypeStruct((S, IN_PROJ_WIDTH), F32), jax.ShapeDtypeStruct((S, D_MODEL), BF16),
                   jax.ShapeDtypeStruct((S, Dm), BF16), jax.ShapeDtypeStruct((S, LRU_WIDTH), F32)],
        scratch_shapes=[pltpu.VMEM((BLOCK, 2 * KV_WIDTH), F32), pltpu.VMEM((T, LRU_WIDTH), F32),
                        pltpu.VMEM((T, CONV_WIDTH), F32), pltpu.VMEM((SUBLANES, LRU_WIDTH), F32)],
        sem=("arbitrary",), args=(x, nrm, w_in, w_out, sinks, small, wa, wx), jobs=jobs)


def _mixer_bwd(x, nrm, dxo, h, proj, ymix, ymo, hprev, w_in, w_out, sinks, small, wa, wx, name, jobs=()):
    S, Dm = x.shape
    T = MIX_TILE
    nT = S // T
    nb = T // BLOCK
    bpt = T // BLOCK

    def body(x_ref, nrm_ref, dxo_ref, h_ref, proj_ref, kvprev_ref, lxprev_ref, scprev_ref, sxprev_ref, ymix_ref,
             ymo_ref, hp_ref, w_in_ref, w_out_ref, sink_ref, small_ref, wa_ref, wx_ref,
             dx_ref, red_ref, dgate_ref, dwo_ref, dwi_ref, dsm_ref, dsink_ref, dwa_ref, dwx_ref,
             dy_s, dp_s, acc_o, acc_i, *carried_state):
        @pl.when(pl.program_id(0) == 0)
        def _():
            for r in (red_ref, dgate_ref, acc_o, acc_i):
                r[...] = jnp.zeros_like(r)

        dxo_v = dxo_ref[...]
        dyb = (nrm_ref[3:4, :] * dxo_v).astype(BF16)
        dgate_ref[0:1, :] += jnp.sum(ymo_ref[...].astype(F32) * dxo_v, axis=0, keepdims=True)
        dy_s[...] = _dot_nt(dyb, w_out_ref[...])
        acc_o[...] += _dot_tn(ymix_ref[...], dyb)
        core(proj_ref, kvprev_ref, lxprev_ref, scprev_ref, sxprev_ref, dy_s, hp_ref, sink_ref, small_ref,
             wa_ref, wx_ref, dp_s, dsm_ref, dsink_ref, dwa_ref, dwx_ref, *carried_state)
        dpb = dp_s[...]
        acc_i[...] += _dot_tn(h_ref[...], dpb)
        dx_ref[...] = dxo_v + _norm_bwd(_dot_nt(dpb, w_in_ref[...]), x_ref[...], nrm_ref, red_ref)

        @pl.when(pl.program_id(0) == nT - 1)
        def _():
            dwo_ref[...] = acc_o[...].astype(BF16)
            dwi_ref[...] = acc_i[...].astype(BF16)

    def core(proj_ref, kvprev_ref, lxprev_ref, scprev_ref, sxprev_ref, dy_ref, hp_ref, sink_ref, small_ref,
             wa_ref, wx_ref, dp_ref, dsm_ref, dsink_ref, dwa_ref, dwx_ref,
             dk_s, dv_s, dkv_c, dxc_n, dc3_n, p_c):
        i = pl.program_id(0)
        ti = nT - 1 - i
        has_prev = jnp.where(ti == 0, 0.0, 1.0)

        @pl.when(i == 0)
        def _():
            for r in (dkv_c, dxc_n, dc3_n, p_c, dsm_ref, dsink_ref, dwa_ref, dwx_ref):
                r[...] = jnp.zeros_like(r)

        row = lax.broadcasted_iota(jnp.int32, (T, LRU_WIDTH), 0)

        kv = proj_ref[:, C_KV:C_KV + 2 * KV_WIDTH]
        ext = jnp.concatenate([kvprev_ref[...] * has_prev, kv], axis=0)
        kx = _head_planes(ext[:, :KV_WIDTH])
        vx = _head_planes(ext[:, KV_WIDTH:])
        dk_s[...] = jnp.zeros_like(dk_s)
        dv_s[...] = jnp.zeros_like(dv_s)
        dk_s[:, T:] = dkv_c[:, :BLOCK]
        dv_s[:, T:] = dkv_c[:, BLOCK:]
        first_tile = jnp.where(ti == 0, 1, 0)
        for b in range(nb):
            units = [(pair, e) for pair in range(N_Q_HEADS // 2) for e in range(2)]
            geometry = _band_geometry(first_tile if b == 0 else 0)
            own = geometry[0]
            keys = slice(b * BLOCK, (b + 2) * BLOCK)
            tile = {pair: (slice(b * BLOCK, (b + 1) * BLOCK), slice(pair * LANES, (pair + 1) * LANES))
                    for pair in range(N_Q_HEADS // 2)}
            qp = {k: (proj_ref[rc] * 0.125).astype(BF16) for k, rc in tile.items()}
            dob = {k: dy_ref[rc].astype(BF16) for k, rc in tile.items()}
            qp_t = {k: jnp.transpose(proj_ref[rc] * 0.125).astype(BF16) for k, rc in tile.items()}
            dob_t = {k: jnp.transpose(dy_ref[rc]).astype(BF16) for k, rc in tile.items()}
            scores = [_fold(_dot_nt(qp[pair], kx[pair // 2][e][keys]), own) for pair, e in units]
            dprob = [_fold(_dot_nt(dob[pair], vx[pair // 2][e][keys]), own) for pair, e in units]
            pn_wide, ds_wide = [], []
            for s, dpm, (pair, e) in zip(scores, dprob, units):
                h = 2 * pair + e
                pn, psink = _softmax_band(s, h, geometry, sink_ref[h])
                dsum = jnp.sum(pn * dpm, axis=-1, keepdims=True)
                dsink_ref[h:h + 1, :] += jnp.full((1, LANES), -1.0, F32) * jnp.sum(psink * dsum)
                pn_wide.append(_unfold(pn, own).astype(BF16))
                ds_wide.append(_unfold(pn * (dpm - dsum), own).astype(BF16))
            dq = {}
            for pw, ds, (pair, e) in zip(pn_wide, ds_wide, units):
                g = pair // 2
                head_e = slice(e * HEAD_DIM, (e + 1) * HEAD_DIM)
                head_g = slice(g * HEAD_DIM, (g + 1) * HEAD_DIM)
                dv_s[head_g, keys] += _dot(dob_t[pair], pw)[head_e, :]
                dk_s[head_g, keys] += _dot(qp_t[pair], ds)[head_e, :]
                part = _dot(ds, kx[g][e][keys])
                dq[pair] = part if e == 0 else dq[pair] + part
            for k, rc in tile.items():
                dp_ref[rc] = (0.125 * dq[k]).astype(BF16)
        dp_ref[:, C_KV:C_KV + KV_WIDTH] = jnp.transpose(dk_s[:, BLOCK:]).astype(BF16)
        dp_ref[:, C_KV + KV_WIDTH:C_KV + 2 * KV_WIDTH] = jnp.transpose(dv_s[:, BLOCK:]).astype(BF16)
        dkv_c[:, :BLOCK] = dk_s[:, :BLOCK]
        dkv_c[:, BLOCK:] = dv_s[:, :BLOCK]

        lx = proj_ref[:, C_LX:C_LX + LRU_WIDTH]
        lxprev = lxprev_ref[...] * has_prev
        xc, xcb, r, ig, sp, a, mult, first = _lru_gates(lx, lxprev, small_ref, wa_ref, wx_ref, row, ti * T)
        hp = hp_ref[...]
        hh = a * hp + mult * (ig * xc)
        lg = proj_ref[:, C_LG:C_LG + LRU_WIDTH]
        gl, th = _gelu(lg)
        dyl = dy_ref[:, ATTN_WIDTH:ATTN_WIDTH + LRU_WIDTH]
        dp_ref[:, C_LG:C_LG + LRU_WIDTH] = (dyl * hh * _gelu_grad(lg, th)).astype(BF16)
        aa = jnp.where(row < T - 1, pltpu.roll(a, T - 1, 0), 1.0)
        bb = dyl * gl
        s = 1
        while s < T:
            a_sh = jnp.where(row < T - s, pltpu.roll(aa, T - s, 0), 1.0)
            b_sh = jnp.where(row < T - s, pltpu.roll(bb, T - s, 0), 0.0)
            bb = bb + aa * b_sh
            aa = aa * a_sh
            s *= 2
        G = bb + aa * p_c[0:1, :]
        p_c[...] = jnp.broadcast_to(_edge_row(a * G, False), p_c.shape)
        da = G * hp
        dmult = G * (ig * xc)
        dig = G * mult * xc
        dxc = G * mult * ig
        dla = da * a + dmult * jnp.where(first, 0.0, -(a * a) / mult)
        dr = dla * ((-LRU_C) * sp)
        lam = small_ref[7:8, :]
        dsm_ref[7:8, :] += jnp.sum(dla * ((-LRU_C) * r), axis=0, keepdims=True) * (-_sigmoid(-lam))
        dpa = dr * r * (1.0 - r)
        dpx = dig * ig * (1.0 - ig)
        dsm_ref[5:6, :] += jnp.sum(dpa, axis=0, keepdims=True)
        dsm_ref[6:7, :] += jnp.sum(dpx, axis=0, keepdims=True)
        dpab = dpa.astype(BF16)
        dpxb = dpx.astype(BF16)
        dwa_ref[...] += _dot_tn(xcb, dpab)
        dwx_ref[...] += _dot_tn(xcb, dpxb)
        dxc = dxc + _dot_nt(dpab, wa_ref[...]) + _dot_nt(dpxb, wx_ref[...])
        dsm_ref[4:5, :] += jnp.sum(dxc, axis=0, keepdims=True)
        dsm_ref[3:4, :] += jnp.sum(dxc * lx, axis=0, keepdims=True)
        for k in range(3):
            dsm_ref[k:k + 1, :] += jnp.sum(dxc * _past(lx, lxprev, 3 - k, row), axis=0, keepdims=True)
        nxt = dxc_n[...]
        dlx = (small_ref[3:4, :] * dxc + small_ref[2:3, :] * _future(dxc, nxt, 1, row)
               + small_ref[1:2, :] * _future(dxc, nxt, 2, row) + small_ref[0:1, :] * _future(dxc, nxt, 3, row))
        dxc_n[...] = dxc
        dp_ref[:, C_LX:C_LX + LRU_WIDTH] = dlx.astype(BF16)

        sc = proj_ref[:, C_SC:C_SC + CONV_WIDTH]
        sx = proj_ref[:, C_SX:C_SX + CONV_WIDTH]
        sb = proj_ref[:, C_SB:C_SB + CONV_WIDTH]
        z = sc * sx
        zprev = (scprev_ref[...] * sxprev_ref[...]) * has_prev
        z1 = _past(z, zprev, 1, row)
        z2 = _past(z, zprev, 2, row)
        c3 = small_ref[10:11, :] * z + small_ref[9:10, :] * z1 + small_ref[8:9, :] * z2
        dys = dy_ref[:, ATTN_WIDTH + LRU_WIDTH:]
        dp_ref[:, C_SB:C_SB + CONV_WIDTH] = (dys * c3).astype(BF16)
        dc3 = dys * sb
        dsm_ref[10:11, :] += jnp.sum(dc3 * z, axis=0, keepdims=True)
        dsm_ref[9:10, :] += jnp.sum(dc3 * z1, axis=0, keepdims=True)
        dsm_ref[8:9, :] += jnp.sum(dc3 * z2, axis=0, keepdims=True)
        nxt3 = dc3_n[...]
        dz = (small_ref[10:11, :] * dc3 + small_ref[9:10, :] * _future(dc3, nxt3, 1, row)
              + small_ref[8:9, :] * _future(dc3, nxt3, 2, row))
        dc3_n[...] = dc3
        dp_ref[:, C_SC:C_SC + CONV_WIDTH] = (dz * sx).astype(BF16)
        dp_ref[:, C_SX:C_SX + CONV_WIDTH] = (dz * sc).astype(BF16)

    fix = lambda i: (0, 0)
    cur = lambda i: (nT - 1 - i, 0)
    prev_cols = lambda cb: (lambda i: (jnp.maximum(nT - 2 - i, 0), cb))
    return _pcall(
        body, name=name, grid=(nT,),
        in_specs=[pl.BlockSpec((T, Dm), cur), pl.BlockSpec((8, Dm), fix), pl.BlockSpec((T, Dm), cur),
                  pl.BlockSpec((T, Dm), cur),
                  pl.BlockSpec((T, IN_PROJ_WIDTH), cur),
                  pl.BlockSpec((BLOCK, 2 * KV_WIDTH),
                               lambda i: (jnp.maximum((nT - 1 - i) * bpt - 1, 0), C_KV // (2 * KV_WIDTH))),
                  pl.BlockSpec((T, LRU_WIDTH), prev_cols(C_LX // LRU_WIDTH)),
                  pl.BlockSpec((T, CONV_WIDTH), prev_cols(C_SC // CONV_WIDTH)),
                  pl.BlockSpec((T, CONV_WIDTH), prev_cols(C_SX // CONV_WIDTH)),
                  pl.BlockSpec((T, D_MODEL), cur), pl.BlockSpec((T, Dm), cur),
                  pl.BlockSpec((T, LRU_WIDTH), cur),
                  _resident((Dm, IN_PROJ_WIDTH)), _resident((D_MODEL, Dm)),
                  pl.BlockSpec(memory_space=pltpu.SMEM),
                  pl.BlockSpec((16, LRU_WIDTH), fix),
                  pl.BlockSpec((LRU_WIDTH, LRU_WIDTH), fix),
                  pl.BlockSpec((LRU_WIDTH, LRU_WIDTH), fix)],
        out_specs=[pl.BlockSpec((T, Dm), cur), pl.BlockSpec((8, Dm), fix), pl.BlockSpec((8, Dm), fix),
                   _resident((D_MODEL, Dm)), _resident((Dm, IN_PROJ_WIDTH)),
                   pl.BlockSpec((16, LRU_WIDTH), fix),
                   pl.BlockSpec((SUBLANES, LANES), fix),
                   pl.BlockSpec((LRU_WIDTH, LRU_WIDTH), fix),
                   pl.BlockSpec((LRU_WIDTH, LRU_WIDTH), fix)],
        out_shape=[jax.ShapeDtypeStruct((S, Dm), F32), jax.ShapeDtypeStruct((8, Dm), F32),
                   jax.ShapeDtypeStruct((8, Dm), F32),
                   jax.ShapeDtypeStruct((D_MODEL, Dm), BF16), jax.ShapeDtypeStruct((Dm, IN_PROJ_WIDTH), BF16),
                   jax.ShapeDtypeStruct((16, LRU_WIDTH), F32),
                   jax.ShapeDtypeStruct((SUBLANES, LANES), F32),
                   jax.ShapeDtypeStruct((LRU_WIDTH, LRU_WIDTH), F32),
                   jax.ShapeDtypeStruct((LRU_WIDTH, LRU_WIDTH), F32)],
        scratch_shapes=[pltpu.VMEM((T, D_MODEL), F32), pltpu.VMEM((T, IN_PROJ_WIDTH), BF16),
                        pltpu.VMEM((D_MODEL, Dm), F32), pltpu.VMEM((Dm, IN_PROJ_WIDTH), F32),
                        pltpu.VMEM((KV_WIDTH, T + BLOCK), F32), pltpu.VMEM((KV_WIDTH, T + BLOCK), F32),
                        pltpu.VMEM((BLOCK, 2 * KV_WIDTH), F32), pltpu.VMEM((T, LRU_WIDTH), F32),
                        pltpu.VMEM((T, CONV_WIDTH), F32), pltpu.VMEM((SUBLANES, LRU_WIDTH), F32)],
        sem=("arbitrary",),
        args=(x, nrm, dxo, h, proj, proj, proj, proj, proj, ymix, ymo, hprev, w_in, w_out, sinks, small, wa, wx),
        jobs=jobs)


def _adamw_update(g, w_ref, m_ref, v_ref, go_ref, d_ref, mo_ref, vo_ref):
    mn = ADAM_B1 * m_ref[...] + (1.0 - ADAM_B1) * g
    vn = ADAM_B2 * v_ref[...] + (1.0 - ADAM_B2) * (g * g)
    go_ref[...] = g
    mo_ref[...] = mn
    vo_ref[...] = vn
    m_hat = mn / (1.0 - ADAM_B1 ** ADAM_STEP)
    v_hat = vn / (1.0 - ADAM_B2 ** ADAM_STEP)
    d_ref[...] = (-ADAM_LR) * (m_hat / (jnp.sqrt(v_hat) + ADAM_EPS) + ADAM_WD * w_ref[...])


def _adamw_many(ws, gs, ms, vs, name):
    n = len(ws)

    def body(*refs):
        w_r, g_r, m_r, v_r, go, do, mo, vo = (refs[k * n:(k + 1) * n] for k in range(8))
        for t in range(n):
            _adamw_update(g_r[t][...], w_r[t], m_r[t], v_r[t], go[t], do[t], mo[t], vo[t])

    vmem = pl.BlockSpec(memory_space=pltpu.VMEM)
    res = pl.pallas_call(
        body, name=name, in_specs=[vmem] * (4 * n), out_specs=[vmem] * (4 * n),
        out_shape=[jax.ShapeDtypeStruct(w.shape, F32) for w in ws] * 4,
        compiler_params=pltpu.CompilerParams(vmem_limit_bytes=VMEM_LIMIT),
    )(*ws, *gs, *ms, *vs)
    return [res[k * n:(k + 1) * n] for k in range(4)]


def _adamw_partials(w, partials, m, v, name):
    nl = len(partials)
    _, R, C = partials[0][0].shape
    tr = 8
    for cand in (256, 128, 64, 32, 16):
        if R % cand == 0 and cand * C * 4 <= (1 << 19):
            tr = cand
            break
    ni = R // tr

    def body(*refs):
        w_ref, p_refs = refs[0], refs[1:1 + 2 * nl]
        m_ref, v_ref, go_ref, d_ref, mo_ref, vo_ref = refs[1 + 2 * nl:]
        for l in range(nl):
            @pl.when(pl.program_id(0) == l)
            def _(pair=p_refs[2 * l:2 * l + 2]):
                own, sib = [((p[0].astype(F32) + p[1].astype(F32)) + p[2].astype(F32)) + p[3].astype(F32)
                            for p in pair]
                _adamw_update(own + sib, w_ref, m_ref, v_ref, go_ref, d_ref, mo_ref, vo_ref)

    def slots(l):
        return pl.BlockSpec((N_CHIPS, tr, C),
                            lambda ll, i: (0, jnp.where(ll == l, i, jnp.where(ll < l, 0, ni - 1)), 0))

    spec = pl.BlockSpec((tr, C), lambda ll, i: (ll * ni + i, 0))
    return pl.pallas_call(
        body, name=name, grid=(nl, ni),
        in_specs=[spec] + [slots(l) for l in range(nl) for _ in range(2)] + [spec, spec], out_specs=[spec] * 4,
        out_shape=[jax.ShapeDtypeStruct((nl * R, C), F32)] * 4,
        compiler_params=_cp("arbitrary", "arbitrary"),
    )(w, *[p for pair in partials for p in pair], m, v)


GATHER_SEMS = 7


def _two_level_gather(x_ref, out_ref, send_sems, recv_sems, local_sem, base=0):
    M = x_ref.shape[0]
    x, y, c = lax.axis_index("x"), lax.axis_index("y"), lax.axis_index("c")
    me, sibling = (x, y, c), (x, y, 1 - c)
    chips = [(1 - x, y), (x, 1 - y), (1 - x, 1 - y)]

    def rows(px, py, pc):
        return out_ref.at[pl.ds(pl.multiple_of((4 * px + 2 * py + pc) * M, SUBLANES), M), :]

    def copy(k, block, to, src=None):
        return pltpu.make_async_remote_copy(
            src_ref=rows(*block) if src is None else src, dst_ref=rows(*block),
            send_sem=send_sems.at[base + k], recv_sem=recv_sems.at[base + k], device_id=to, device_id_type=MESH)

    mine = pltpu.make_async_copy(x_ref, rows(*me), local_sem)
    mine.start()
    first = [copy(0, me, sibling, src=x_ref)]
    first += [copy(1 + j, me, (*chip, c), src=x_ref) for j, chip in enumerate(chips)]
    for cp in first:
        cp.start()
    passed = [copy(4 + j, (*chip, c), sibling) for j, chip in enumerate(chips)]
    for j, chip in enumerate(chips):
        copy(1 + j, (*chip, c), me).wait_recv()
        passed[j].start()
    copy(0, sibling, me).wait_recv()
    for j, chip in enumerate(chips):
        copy(4 + j, (*chip, 1 - c), me).wait_recv()
    for cp in first + passed:
        cp.wait_send()
    mine.wait()


def _prologue(pack, w_mod, jobs, name):
    M = pack.shape[0]
    L, Dm, N = w_mod.shape
    nj = len(jobs)
    rows_c = Dm // LANES
    tn = 768

    def body(*refs):
        pack_ref, w_ref = refs[:2]
        jin, refs = refs[2:2 + nj], refs[2 + nj:]
        g_ref, ca_ref, mod_ref = refs[:3]
        jout, refs = refs[3:3 + nj], refs[3 + nj:]
        part_ref, send_sems, recv_sems, local_sem, *jsems = refs
        _run_jobs("start", jobs, jin, jout, jsems)
        _two_level_gather(pack_ref, g_ref, send_sems, recv_sems, local_sem.at[0], 0)
        ca_ref[...] = jnp.zeros_like(ca_ref)
        for r in range(rows_c):
            cv = g_ref[pl.ds(r, N_DEV, stride=M), :]
            ca_ref[0:N_DEV, r * LANES:(r + 1) * LANES] = (cv * _sigmoid(cv)).astype(BF16)
        ca = ca_ref[...]
        for l in range(L):
            for n0 in range(0, N, tn):
                part_ref[l * 16:(l + 1) * 16, n0:n0 + tn] = _dot(ca, w_ref[l, :, n0:n0 + tn].astype(BF16))
        _two_level_gather(part_ref, mod_ref, send_sems, recv_sems, local_sem.at[1], GATHER_SEMS)
        _run_jobs("relay", jobs, jin, jout, jsems)
        _run_jobs("finish", jobs, jin, jout, jsems)

    vmem = pl.BlockSpec(memory_space=pltpu.VMEM)
    hbm = pl.BlockSpec(memory_space=pltpu.HBM)
    res = pl.pallas_call(
        body, name=name,
        out_shape=[jax.ShapeDtypeStruct((N_DEV * M, LANES), F32), jax.ShapeDtypeStruct((16, Dm), BF16),
                   jax.ShapeDtypeStruct((N_DEV * L * 16, N), F32)] + [job.out_shape for job in jobs],
        in_specs=[vmem, vmem] + [hbm] * nj, out_specs=[vmem, vmem, vmem] + [hbm] * nj,
        scratch_shapes=[pltpu.VMEM((L * 16, N), F32), pltpu.SemaphoreType.DMA((2 * GATHER_SEMS,)),
                        pltpu.SemaphoreType.DMA((2 * GATHER_SEMS,)), pltpu.SemaphoreType.DMA((2,))]
        + _job_scratch(nj),
        compiler_params=pltpu.CompilerParams(vmem_limit_bytes=VMEM_LIMIT),
    )(pack, w_mod, *[job.src for job in jobs])
    return res[0], res[1], res[2], list(res[3:])


def _all_gather_small(v, name, jobs=()):
    M, N = v.shape
    nj = len(jobs)

    def body(*refs):
        x_ref, jin = refs[0], refs[1:1 + nj]
        out_ref, sum_ref = refs[1 + nj:3 + nj]
        jout, (send_sems, recv_sems, local_sem, *jsems) = refs[3 + nj:3 + 2 * nj], refs[3 + 2 * nj:]
        _run_jobs("start", jobs, jin, jout, jsems)
        _two_level_gather(x_ref, out_ref, send_sems, recv_sems, local_sem)
        acc = out_ref[0:M, :]
        for d in range(1, N_DEV):
            acc = acc + out_ref[d * M:(d + 1) * M, :]
        sum_ref[...] = acc
        _run_jobs("relay", jobs, jin, jout, jsems)
        _run_jobs("finish", jobs, jin, jout, jsems)

    vmem = pl.BlockSpec(memory_space=pltpu.VMEM)
    hbm = pl.BlockSpec(memory_space=pltpu.HBM)
    res = pl.pallas_call(
        body, name=name,
        out_shape=[jax.ShapeDtypeStruct((N_DEV * M, N), F32), jax.ShapeDtypeStruct((M, N), F32)]
        + [job.out_shape for job in jobs],
        in_specs=[vmem] + [hbm] * nj, out_specs=[vmem, vmem] + [hbm] * nj,
        scratch_shapes=[pltpu.SemaphoreType.DMA((GATHER_SEMS,)), pltpu.SemaphoreType.DMA((GATHER_SEMS,)),
                        pltpu.SemaphoreType.DMA] + (_job_scratch(nj) if nj else []),
        compiler_params=pltpu.CompilerParams(vmem_limit_bytes=VMEM_LIMIT),
    )(v, *[job.src for job in jobs])
    return list(res[:2]), list(res[2:])


def _adamw_w_mod(w, c_act, dmod, m, v, name):
    L, Dm, N = w.shape
    R = c_act.shape[0]
    tr = LANES

    def body(w_ref, c_ref, d_ref, m_ref, v_ref, go_ref, dl_ref, mo_ref, vo_ref):
        g = _dot_tn(c_ref[...], d_ref[0])
        _adamw_update(g, w_ref.at[0], m_ref.at[0], v_ref.at[0], go_ref.at[0], dl_ref.at[0], mo_ref.at[0], vo_ref.at[0])

    spec = pl.BlockSpec((1, tr, N), lambda l, i: (l, i, 0))
    return pl.pallas_call(
        body, name=name, grid=(L, Dm // tr),
        in_specs=[spec, pl.BlockSpec((R, tr), lambda l, i: (0, i)), pl.BlockSpec((1, R, N), lambda l, i: (l, 0, 0)),
                  spec, spec],
        out_specs=[spec] * 4, out_shape=[jax.ShapeDtypeStruct((L, Dm, N), F32)] * 4,
        compiler_params=_cp("arbitrary", "arbitrary"),
    )(w, c_act, dmod, m, v)


_BIG = (("w_ffn1_gu", 1), ("w_ffn1_down", 0), ("w_ffn2_gu", 1), ("w_ffn2_down", 0), ("w_in", 1), ("w_out", 0))
_AXIS = dict(_BIG)

_GATHER_PLAN = {
    "first": [(0, "w_ffn1_gu"), (0, "w_ffn1_down")],
    (0, "ffn1"): [(0, "w_in"), (0, "w_out"), (0, "w_ffn2_gu")],
    (0, "mix"): [(0, "w_ffn2_down")],
    (0, "ffn2"): [(1, "w_ffn1_gu"), (1, "w_ffn1_down")],
    (1, "ffn1"): [(1, "w_in"), (1, "w_out"), (1, "w_ffn2_gu")],
    (1, "mix"): [(1, "w_ffn2_down")],
}


def _pack(arrs, rows_multiple=SUBLANES):
    flat = jnp.concatenate([a.astype(F32).reshape(-1) for a in arrs])
    unit = rows_multiple * LANES
    total = -(-flat.shape[0] // unit) * unit
    return jnp.pad(flat, (0, total - flat.shape[0])).reshape(total // LANES, LANES)


def _unpack(flat, shapes):
    out, off = [], 0
    for shp in shapes:
        n = int(math.prod(shp))
        out.append(flat[off:off + n].reshape(shp))
        off += n
    return out


def _block_diag(w):
    out = jnp.zeros((LRU_WIDTH, LRU_WIDTH), F32)
    for h in range(4):
        out = lax.dynamic_update_slice(out, w[h], (h * HEAD_DIM, h * HEAD_DIM))
    return out


def _diag_blocks(w):
    return jnp.stack([w[h * HEAD_DIM:(h + 1) * HEAD_DIM, h * HEAD_DIM:(h + 1) * HEAD_DIM] for h in range(4)])


def _rows8(*rows):
    z = jnp.zeros((8 - len(rows), rows[0].shape[-1]), F32)
    return jnp.concatenate([jnp.stack(rows), z], axis=0)


def kernel(x, c, w_mod, b_mod, g_norm, w_ffn1_gu, w_ffn1_down, w_ffn2_gu, w_ffn2_down, w_in, w_out, attn_sinks, lru_conv_w, lru_conv_b, lru_gate_a_w, lru_gate_a_b, lru_gate_x_w, lru_gate_x_b, lru_lambda, sc_conv_w, g_final, loss_target, m_w_mod, m_b_mod, m_g_norm, m_w_ffn1_gu, m_w_ffn1_down, m_w_ffn2_gu, m_w_ffn2_down, m_w_in, m_w_out, m_attn_sinks, m_lru_conv_w, m_lru_conv_b, m_lru_gate_a_w, m_lru_gate_a_b, m_lru_gate_x_w, m_lru_gate_x_b, m_lru_lambda, m_sc_conv_w, m_g_final, v_w_mod, v_b_mod, v_g_norm, v_w_ffn1_gu, v_w_ffn1_down, v_w_ffn2_gu, v_w_ffn2_down, v_w_in, v_w_out, v_attn_sinks, v_lru_conv_w, v_lru_conv_b, v_lru_gate_a_w, v_lru_gate_a_b, v_lru_gate_x_w, v_lru_gate_x_b, v_lru_lambda, v_sc_conv_w, v_g_final):
    W = dict(w_mod=w_mod, b_mod=b_mod, g_norm=g_norm, w_ffn1_gu=w_ffn1_gu, w_ffn1_down=w_ffn1_down,
             w_ffn2_gu=w_ffn2_gu, w_ffn2_down=w_ffn2_down, w_in=w_in, w_out=w_out, attn_sinks=attn_sinks,
             lru_conv_w=lru_conv_w, lru_conv_b=lru_conv_b, lru_gate_a_w=lru_gate_a_w, lru_gate_a_b=lru_gate_a_b,
             lru_gate_x_w=lru_gate_x_w, lru_gate_x_b=lru_gate_x_b, lru_lambda=lru_lambda, sc_conv_w=sc_conv_w,
             g_final=g_final)
    M1 = dict(w_mod=m_w_mod, b_mod=m_b_mod, g_norm=m_g_norm, w_ffn1_gu=m_w_ffn1_gu, w_ffn1_down=m_w_ffn1_down,
              w_ffn2_gu=m_w_ffn2_gu, w_ffn2_down=m_w_ffn2_down, w_in=m_w_in, w_out=m_w_out,
              attn_sinks=m_attn_sinks, lru_conv_w=m_lru_conv_w, lru_conv_b=m_lru_conv_b,
              lru_gate_a_w=m_lru_gate_a_w, lru_gate_a_b=m_lru_gate_a_b, lru_gate_x_w=m_lru_gate_x_w,
              lru_gate_x_b=m_lru_gate_x_b, lru_lambda=m_lru_lambda, sc_conv_w=m_sc_conv_w, g_final=m_g_final)
    V1 = dict(w_mod=v_w_mod, b_mod=v_b_mod, g_norm=v_g_norm, w_ffn1_gu=v_w_ffn1_gu, w_ffn1_down=v_w_ffn1_down,
              w_ffn2_gu=v_w_ffn2_gu, w_ffn2_down=v_w_ffn2_down, w_in=v_w_in, w_out=v_w_out,
              attn_sinks=v_attn_sinks, lru_conv_w=v_lru_conv_w, lru_conv_b=v_lru_conv_b,
              lru_gate_a_w=v_lru_gate_a_w, lru_gate_a_b=v_lru_gate_a_b, lru_gate_x_w=v_lru_gate_x_w,
              lru_gate_x_b=v_lru_gate_x_b, lru_lambda=v_lru_lambda, sc_conv_w=v_sc_conv_w, g_final=v_g_final)
    names = ["w_mod", "b_mod", "g_norm", "w_ffn1_gu", "w_ffn1_down", "w_ffn2_gu", "w_ffn2_down", "w_in", "w_out",
             "attn_sinks", "lru_conv_w", "lru_conv_b", "lru_gate_a_w", "lru_gate_a_b", "lru_gate_x_w",
             "lru_gate_x_b", "lru_lambda", "sc_conv_w", "g_final"]

    xs = x[0]
    tgt = loss_target[0]
    S = xs.shape[0]
    chip = 2 * lax.axis_index("x") + lax.axis_index("y")
    batch = 2 * chip + lax.axis_index("c")
    L = DEPTH

    full = {}

    def gather_jobs(key):
        return [_GatherJob(W[n][l].astype(BF16), _AXIS[n]) for l, n in _GATHER_PLAN.get(key, ())]

    def landed(key, outs):
        full.update(zip(_GATHER_PLAN.get(key, ()), outs))

    fwd_shapes = [(D_MODEL,), g_norm.shape, lru_conv_w.shape, sc_conv_w.shape]
    gathered, c_act, mod_all, ex = _prologue(_pack([c[0], g_norm, lru_conv_w, sc_conv_w]), w_mod,
                                             gather_jobs("first"), "prologue")
    landed("first", ex)
    gathered = gathered.reshape(N_DEV, -1)
    per_chip = [_unpack(gathered[2 * jj], fwd_shapes) for jj in range(N_CHIPS)]
    g_norm_full = jnp.concatenate([p[1] for p in per_chip], axis=-1)
    lru_conv_w_full = jnp.concatenate([p[2] for p in per_chip], axis=-1)
    sc_conv_w_full = jnp.concatenate([p[3] for p in per_chip], axis=-1)
    mod_all = mod_all.reshape(N_DEV, L, 16, -1)
    mod_rows = [lax.dynamic_index_in_dim(mod_all[2 * jj], batch, axis=1, keepdims=False) for jj in range(N_CHIPS)]
    mod = (jnp.concatenate(mod_rows, axis=-1) + b_mod).reshape(L, 9, D_MODEL)

    nrm_all = jnp.concatenate([g_norm_full[:, :, None, :], mod.reshape(L, 3, 3, D_MODEL),
                               jnp.zeros((L, 3, 4, D_MODEL), F32)], axis=2)

    def nrm_rows(l, s):
        return nrm_all[l, s]

    def mixer_params(l):
        small = jnp.concatenate([lru_conv_w_full[l], lru_conv_b[l][None], lru_gate_a_b[l][None],
                                 lru_gate_x_b[l][None], lru_lambda[l][None], sc_conv_w_full[l],
                                 jnp.zeros((5, LRU_WIDTH), F32)], axis=0)
        return (attn_sinks[l], small, _block_diag(lru_gate_a_w[l]).astype(BF16),
                _block_diag(lru_gate_x_w[l]).astype(BF16))

    saved = []
    xcur = xs
    for l in range(L):
        n1, n2, n3 = nrm_rows(l, 0), nrm_rows(l, 1), nrm_rows(l, 2)

        def ffn(which, xin, nrm, head=None):
            key = (l, which)
            (xo, h, gu, y, *stats), ex = _ffn_fwd(xin, nrm, full[(l, f"w_{which}_gu")], full[(l, f"w_{which}_down")],
                                                  f"l{l}_{which}", gather_jobs(key), head)
            landed(key, ex)
            return (xo, *stats), (xin, h, gu, y)

        (x1,), s1 = ffn("ffn1", xcur, n1)
        mp = mixer_params(l)
        (x2, h2, proj, ymix, ymo, hprev), ex = _mixer_fwd(x1, n2, full[(l, "w_in")], full[(l, "w_out")], *mp,
                                                          f"l{l}_mix", gather_jobs((l, "mix")))
        landed((l, "mix"), ex)
        s2 = (x1, h2, proj, ymix, ymo, hprev, mp)
        (xcur, *stats), s3 = ffn("ffn2", x2, n3, (_rows8(g_final), tgt) if l == L - 1 else None)
        saved.append((n1, n2, n3, s1, s2, s3))

    dx, stats = xcur, stats[0]
    loss_here, d_g_final = stats[1, 0:1], stats[0]

    recv, theirs = {}, {}
    waiting = []

    def carried(fn, *a, extra=()):
        items = waiting + list(extra)
        waiting.clear()
        outs, landed_now = fn(*a, jobs=[_SiblingJob(recv[(ll, n)]) if g is None else _ScatterJob(g, _AXIS[n])
                                        for ll, n, g in items])
        for (ll, n, g), arr in zip(items, landed_now):
            if g is None:
                theirs[(ll, n)] = arr
            else:
                recv[(ll, n)] = arr
                waiting.append((ll, n, None))
        return outs

    dmod, d_gnorm, d_small = [None] * L, [None] * L, [None] * L
    for l in reversed(range(L)):
        n1, n2, n3, s1, s2, s3 = saved[l]

        def plain(fn, *a):
            return fn(*a)[0]

        def ffn_bwd(which, dxo, sv, nrm, last):
            xin, h, gu, y = sv
            tag = f"l{l}_{which}"
            dgu, dgate, dw_down = carried(
                _ffn_down_bwd, dxo, gu, y, full[(l, f"w_{which}_down")], nrm, tag + "_down_bwd")
            dw_gu = carried(_atb, h, dgu, BF16, 1024, 2816, tag + "_dw_gu", extra=[(l, f"w_{which}_down", dw_down)])
            mine = [(l, f"w_{which}_gu", dw_gu)]
            dxi, red = (carried if last else plain)(
                _nt_norm_bwd, dgu, full[(l, f"w_{which}_gu")], xin, nrm, dxo, tag + "_gu_bwd",
                **(dict(extra=mine) if last else {}))
            if not last:
                waiting.extend(mine)
            return dxi, (red[0], red[1], dgate[0]), red[2]

        dx, dm3, dg3 = ffn_bwd("ffn2", dx, s3, n3, False)
        x_in, h2, proj, ymix, ymo, hprev, mp = s2
        dx, red, dgate, dw_out, dw_in, dsm, dsink, dwa, dwx = carried(
            _mixer_bwd, x_in, n2, dx, h2, proj, ymix, ymo, hprev, full[(l, "w_in")], full[(l, "w_out")], *mp,
            f"l{l}_mix_bwd")
        waiting.extend([(l, "w_out", dw_out), (l, "w_in", dw_in)])
        dm2, dg2 = (red[0], red[1], dgate[0]), red[2]
        dx, dm1, dg1 = ffn_bwd("ffn1", dx, s1, n1, l == 0)
        dmod[l] = jnp.stack(list(dm1) + list(dm2) + list(dm3))
        d_gnorm[l] = jnp.stack([dg1, dg2, dg3])
        d_small[l] = (dsink[:, 0], dsm[0:4], dsm[4], _diag_blocks(dwa), dsm[5], _diag_blocks(dwx), dsm[6],
                      dsm[7], dsm[8:11])
    grad_x = dx[None]

    def both(k):
        return jnp.stack([d_small[0][k], d_small[1][k]])
    small_names = ["g_norm", "attn_sinks", "lru_conv_w", "lru_conv_b", "lru_gate_a_w", "lru_gate_a_b",
                   "lru_gate_x_w", "lru_gate_x_b", "lru_lambda", "sc_conv_w", "g_final"]
    small_parts = [jnp.stack(d_gnorm)] + [both(k) for k in range(9)] + [d_g_final]
    dmod_flat = jnp.stack(dmod).reshape(-1)
    bwd_gathered, bwd_sum = carried(_all_gather_small, _pack([dmod_flat] + small_parts + [loss_here]),
                                    "gather_small_bwd")
    n_mod = dmod_flat.shape[0]
    dmod_all = bwd_gathered.reshape(N_DEV, -1)[:, :n_mod].reshape(N_DEV, L, 9 * D_MODEL)
    bwd_sum = bwd_sum.reshape(-1)
    G = {"b_mod": bwd_sum[:n_mod].reshape(L, 9 * D_MODEL)}
    *small_sums, loss = _unpack(bwd_sum[n_mod:], [p.shape for p in small_parts] + [(1,)])
    loss = loss[0]
    G.update(zip(small_names, small_sums))
    for n in ("g_norm", "lru_conv_w", "sc_conv_w"):
        wdt = W[n].shape[-1]
        G[n] = lax.dynamic_slice_in_dim(G[n], chip * wdt, wdt, axis=G[n].ndim - 1)

    ncol = w_mod.shape[-1]
    dmod_cols = lax.dynamic_slice_in_dim(dmod_all, chip * ncol, ncol, axis=2)
    dmod_rows = jnp.pad(jnp.swapaxes(dmod_cols, 0, 1), ((0, 0), (0, 16 - N_DEV), (0, 0))).astype(BF16)

    out_g, out_d, out_m, out_v = {}, {}, {}, {}
    out_g["w_mod"], out_d["w_mod"], out_m["w_mod"], out_v["w_mod"] = _adamw_w_mod(
        w_mod, c_act, dmod_rows, m_w_mod, v_w_mod, "adamw_w_mod")
    for n, _ in _BIG:
        shp = W[n].shape
        flat = (shp[0] * shp[1], shp[2])
        res = _adamw_partials(W[n].reshape(flat), [(recv[(l, n)], theirs[(l, n)]) for l in range(L)],
                              M1[n].reshape(flat), V1[n].reshape(flat), f"adamw_{n}")
        out_g[n], out_d[n], out_m[n], out_v[n] = [r.reshape(shp) for r in res]
    rest = ["b_mod"] + small_names

    def rows(a):
        return a.reshape(-1, a.shape[-1])
    res = _adamw_many([rows(W[n]) for n in rest], [rows(G[n]) for n in rest], [rows(M1[n]) for n in rest],
                      [rows(V1[n]) for n in rest], "adamw_small")
    for dst, group in zip((out_g, out_d, out_m, out_v), res):
        dst.update({n: r.reshape(W[n].shape) for n, r in zip(rest, group)})

    return (loss, grad_x, *[out_g[n] for n in names], *[out_d[n] for n in names],
            *[out_m[n] for n in names], *[out_v[n] for n in names])
```
